```python
import jax, jax.numpy as jnp
from jax import lax
import numpy as np

D_MODEL = 1024
BATCH = 8
SEQ = 4096
DEPTH = 2

CHUNK = 64
Q_BLOCK = 128
EPS = 1e-6
HEAD_DIM = 64
FOX_HEADS = 8
FOX_W = FOX_HEADS * HEAD_DIM
SB_HEADS = 8
SB_W = SB_HEADS * HEAD_DIM
MLA_HEADS = 8
MLA_Q_RANK = 384
MLA_KV_RANK = 256
MLA_NOPE = 64
MLA_ROPE = 32
MLA_V = 64
MLA_W = MLA_HEADS * MLA_V
ROPE_BASE = 10000.0
N_BRANCH = 3
D_FF = -(-8 * D_MODEL // (3 * 256)) * 256

SPLIT_WIDTHS = (3 * FOX_W, FOX_HEADS, MLA_Q_RANK, MLA_KV_RANK, MLA_ROPE, 3 * SB_W, N_BRANCH * D_MODEL)
IN_WIDTH = int(sum(SPLIT_WIDTHS))
SPLIT_POINTS = [int(p) for p in np.cumsum(SPLIT_WIDTHS)[:-1]]

kernel_name = "hybrid_fox_mla_stickbreak_gated_encoder"


def rms_norm(x, g):
    xf = x.astype(jnp.float32)
    y = xf * lax.rsqrt(jnp.mean(xf * xf, axis=-1, keepdims=True) + EPS)
    return (y * g.astype(jnp.float32)).astype(x.dtype)


def rope_tables(positions, dim):
    half = dim // 2
    inv = ROPE_BASE ** (-jnp.arange(half, dtype=jnp.float32) / half)
    ang = positions.astype(jnp.float32)[..., None] * inv
    return jnp.cos(ang), jnp.sin(ang)


def apply_rope(x, cos, sin):
    x1, x2 = jnp.split(x.astype(jnp.float32), 2, axis=-1)
    out = jnp.concatenate([x1 * cos - x2 * sin, x1 * sin + x2 * cos], axis=-1)
    return out.astype(x.dtype)


def to_heads(t, n_heads):
    b, s, w = t.shape
    return t.reshape(b, s, n_heads, w // n_heads).transpose(0, 2, 1, 3)


def merge_heads(t):
    b, h, s, d = t.shape
    return t.transpose(0, 2, 1, 3).reshape(b, s, h * d)


def to_blocks(t):
    b, h, s = t.shape[:3]
    t = t.reshape((b, h, s // Q_BLOCK, Q_BLOCK) + t.shape[3:])
    return jnp.moveaxis(t, 2, 0)


def from_blocks(t):
    nb, b, h, qb = t.shape[:4]
    return jnp.moveaxis(t, 0, 2).reshape((b, h, nb * qb) + t.shape[4:])


def fox_attention(q, k, v, log_f):
    s = q.shape[2]
    scale = q.shape[-1] ** -0.5
    cum = jnp.cumsum(log_f, axis=-1)
    kpos = jnp.arange(s)
    starts = jnp.arange(s // Q_BLOCK) * Q_BLOCK

    def block(args):
        qb, fq, start = args
        qpos = start + jnp.arange(Q_BLOCK)
        logits = jnp.einsum('bhqd,bhkd->bhqk', qb, k, preferred_element_type=jnp.float32) * scale
        logits = logits + fq[..., None] - cum[..., None, :]
        logits = jnp.where(kpos[None, :] <= qpos[:, None], logits, -jnp.inf)
        p = jax.nn.softmax(logits, axis=-1)
        return jnp.einsum('bhqk,bhkd->bhqd', p.astype(v.dtype), v)

    return from_blocks(lax.map(block, (to_blocks(q), to_blocks(cum), starts)))


def mla_attention(q_nope, q_rope, k_nope, k_rope, v):
    s = q_nope.shape[2]
    scale = (MLA_NOPE + MLA_ROPE) ** -0.5
    kchunk = jnp.arange(s) // CHUNK
    starts = jnp.arange(s // Q_BLOCK) * Q_BLOCK

    def block(args):
        qn, qr, start = args
        qchunk = (start + jnp.arange(Q_BLOCK)) // CHUNK
        logits = (jnp.einsum('bhqn,bhkn->bhqk', qn, k_nope, preferred_element_type=jnp.float32)
                  + jnp.einsum('bhqr,bkr->bhqk', qr, k_rope, preferred_element_type=jnp.float32)) * scale
        logits = jnp.where(kchunk[None, :] <= qchunk[:, None], logits, -jnp.inf)
        p = jax.nn.softmax(logits, axis=-1)
        return jnp.einsum('bhqk,bhkv->bhqv', p.astype(v.dtype), v)

    return from_blocks(lax.map(block, (to_blocks(q_nope), to_blocks(q_rope), starts)))


def stick_breaking_attention(q, k, v):
    s = q.shape[2]
    scale = q.shape[-1] ** -0.5
    kpos = jnp.arange(s)
    starts = jnp.arange(s // Q_BLOCK) * Q_BLOCK

    def block(args):
        qb, start = args
        qpos = start + jnp.arange(Q_BLOCK)
        mask = kpos[None, :] < qpos[:, None]
        z = jnp.einsum('bhqd,bhkd->bhqk', qb, k, preferred_element_type=jnp.float32) * scale
        log_beta = jax.nn.log_sigmoid(z)
        log_keep = jnp.where(mask, jax.nn.log_sigmoid(-z), 0.0)
        suffix = lax.cumsum(log_keep, axis=3, reverse=True) - log_keep
        a = jnp.where(mask, jnp.exp(log_beta + suffix), 0.0)
        return jnp.einsum('bhqk,bhkd->bhqd', a.astype(v.dtype), v)

    return from_blocks(lax.map(block, (to_blocks(q), starts)))


def setup_inputs(seed: int = 0) -> dict:
    key = jax.random.key(seed)
    ks = jax.random.split(key, 24)

    def nrm(k, shape, fan_in, gain=1.0):
        return jax.random.normal(k, shape, jnp.float32) * (gain * fan_in ** -0.5)

    def gain(k, shape):
        return 1.0 + 0.1 * jax.random.normal(k, shape, jnp.float32)

    x = jax.random.normal(ks[0], (BATCH, SEQ, D_MODEL), jnp.float32)
    c = jax.random.normal(ks[1], (BATCH, D_MODEL), jnp.float32)
    offs = jax.random.randint(ks[2], (BATCH, 1), 0, 10000, dtype=jnp.int32)
    positions = jnp.arange(SEQ, dtype=jnp.int32)[None, :] + offs
    return {
        "x": x,
        "c": c,
        "positions": positions,
        "g_mix": gain(ks[3], (DEPTH, D_MODEL)),
        "w_ada": nrm(ks[4], (DEPTH, D_MODEL, 6 * D_MODEL), D_MODEL, 0.5),
        "b_ada": 0.01 * jax.random.normal(ks[5], (DEPTH, 6 * D_MODEL), jnp.float32),
        "w_in": nrm(ks[6], (DEPTH, D_MODEL, IN_WIDTH), D_MODEL),
        "b_fox_f": jax.random.uniform(ks[7], (DEPTH, FOX_HEADS), jnp.float32, 1.0, 5.0),
        "g_mla_q": gain(ks[8], (DEPTH, MLA_Q_RANK)),
        "w_mla_uq": nrm(ks[9], (DEPTH, MLA_Q_RANK, MLA_HEADS * (MLA_NOPE + MLA_ROPE)), MLA_Q_RANK),
        "g_mla_kv": gain(ks[10], (DEPTH, MLA_KV_RANK)),
        "w_mla_ukv": nrm(ks[11], (DEPTH, MLA_KV_RANK, MLA_HEADS * (MLA_NOPE + MLA_V)), MLA_KV_RANK),
        "w_o_fox": nrm(ks[12], (DEPTH, FOX_W, D_MODEL), FOX_W),
        "w_o_mla": nrm(ks[13], (DEPTH, MLA_W, D_MODEL), MLA_W),
        "w_o_sb": nrm(ks[14], (DEPTH, SB_W, D_MODEL), SB_W),
        "w_out": nrm(ks[15], (DEPTH, D_MODEL, D_MODEL), D_MODEL),
        "g_ffn": gain(ks[16], (DEPTH, D_MODEL)),
        "w_ffn_gate": nrm(ks[17], (DEPTH, D_MODEL, D_FF), D_MODEL),
        "w_ffn_up": nrm(ks[18], (DEPTH, D_MODEL, D_FF), D_MODEL),
        "w_ffn_down": nrm(ks[19], (DEPTH, D_FF, D_MODEL), D_FF),
        "g_final": gain(ks[20], (D_MODEL,)),
    }


def reference(x, c, positions, g_mix, w_ada, b_ada, w_in, b_fox_f, g_mla_q, w_mla_uq, g_mla_kv,
              w_mla_ukv, w_o_fox, w_o_mla, w_o_sb, w_out, g_ffn, w_ffn_gate, w_ffn_up, w_ffn_down,
              g_final):
    b, s, d = x.shape
    cos, sin = rope_tables(positions, MLA_ROPE)
    cond = jax.nn.silu(c)
    for l in range(DEPTH):
        mod = (cond @ w_ada[l] + b_ada[l]).reshape(b, 6, d)
        sh_a, sc_a, gt_a, sh_f, sc_f, gt_f = [mod[:, i, None, :] for i in range(6)]

        u = rms_norm(x, g_mix[l]) * (1 + sc_a) + sh_a
        proj = u @ w_in[l]
        fox_qkv, fox_f, mla_ql, mla_kvl, mla_kr, sb_qkv, gate_logit = jnp.split(proj, SPLIT_POINTS, axis=-1)

        fq, fk, fv = [to_heads(t, FOX_HEADS) for t in jnp.split(fox_qkv, 3, axis=-1)]
        log_f = jax.nn.log_sigmoid((fox_f + b_fox_f[l]).astype(jnp.float32)).transpose(0, 2, 1)
        y_fox = merge_heads(fox_attention(fq, fk, fv, log_f))

        cq = rms_norm(mla_ql, g_mla_q[l])
        q = to_heads(cq @ w_mla_uq[l], MLA_HEADS)
        q_nope = q[..., :MLA_NOPE]
        q_rope = apply_rope(q[..., MLA_NOPE:], cos[:, None], sin[:, None])
        ckv = rms_norm(mla_kvl, g_mla_kv[l])
        kv = to_heads(ckv @ w_mla_ukv[l], MLA_HEADS)
        k_nope, mla_v = kv[..., :MLA_NOPE], kv[..., MLA_NOPE:]
        k_rope = apply_rope(mla_kr, cos, sin)
        y_mla = merge_heads(mla_attention(q_nope, q_rope, k_nope, k_rope, mla_v))

        sq, sk, sv = [to_heads(t, SB_HEADS) for t in jnp.split(sb_qkv, 3, axis=-1)]
        y_sb = merge_heads(stick_breaking_attention(sq, sk, sv))

        g = jax.nn.sigmoid(gate_logit.astype(jnp.float32)).astype(x.dtype).reshape(b, s, N_BRANCH, d)
        merged = (g[:, :, 0] * (y_fox @ w_o_fox[l])
                  + g[:, :, 1] * (y_mla @ w_o_mla[l])
                  + g[:, :, 2] * (y_sb @ w_o_sb[l]))
        x = x + gt_a * (merged @ w_out[l])

        u = rms_norm(x, g_ffn[l]) * (1 + sc_f) + sh_f
        h = jax.nn.silu(u @ w_ffn_gate[l]) * (u @ w_ffn_up[l])
        x = x + gt_f * (h @ w_ffn_down[l])
    return rms_norm(x, g_final)
```

```python
import functools

import jax
import jax.numpy as jnp
from jax import lax
from jax.experimental import pallas as pl
from jax.experimental.pallas import tpu as pltpu

F32 = jnp.float32
BF16 = jnp.bfloat16

D_MODEL = 1024
EPS = 1e-6
HEAD_DIM = 64
N_HEADS = 8
HEAD_W = N_HEADS * HEAD_DIM
CHUNK_SHIFT = 6
MLA_Q_RANK = 384
MLA_KV_RANK = 256
MLA_NOPE = 64
MLA_ROPE = 32
ROPE_HALF = MLA_ROPE // 2
ROPE_BASE = 10000.0
MLA_SCALE = (MLA_NOPE + MLA_ROPE) ** -0.5
QK_SCALE = HEAD_DIM ** -0.5
D_FF = 2816
LANES = 128
SLOT_W = N_HEADS * LANES

C_FOX = 0
C_SB = 1536
C_GATE = 3072
C_QL = 6144
C_KVL = 6528
C_KR = 6784
C_FF = 6912
IN_W = 7040

NEG = -1e30
VMEM_LIMIT = 56 * 1024 * 1024


def _params(n_grid):
    return pltpu.CompilerParams(dimension_semantics=("arbitrary",) * n_grid,
                                vmem_limit_bytes=VMEM_LIMIT)


def _resident(shape):
    nd = len(shape)
    return pl.BlockSpec(shape, lambda *_: (0,) * nd, pipeline_mode=pl.Buffered(1))


def _bdot(a, b):
    return jnp.dot(a, b, preferred_element_type=F32)


def _dot_nt(a, b):
    return lax.dot_general(a, b, (((1,), (1,)), ((), ())), preferred_element_type=F32)


def _split3(v):
    hi = v.astype(BF16)
    r = v - hi.astype(F32)
    mid = r.astype(BF16)
    lo = (r - mid.astype(F32)).astype(BF16)
    return hi, mid, lo


def _log_sigmoid(z):
    return jnp.minimum(z, 0.0) - jnp.log1p(jnp.exp(-jnp.abs(z)))


def _sigmoid(z):
    return 1.0 / (1.0 + jnp.exp(-z))


def _rms(x):
    return x * lax.rsqrt(jnp.mean(x * x, axis=-1, keepdims=True) + EPS)


def _ada_kernel(c_ref, w_ref, b_ref, o_ref):
    c = c_ref[...]
    cond = c * _sigmoid(c)
    chi, cmid, clo = _split3(cond)
    whi, wmid, wlo = _split3(w_ref[0])
    acc = _bdot(chi, whi) + (_bdot(chi, wmid) + _bdot(cmid, whi))
    acc = acc + (_bdot(chi, wlo) + _bdot(clo, whi) + _bdot(cmid, wmid))
    o_ref[0] = acc + b_ref[0]


def _ada(c, w_ada, b_ada):
    depth, d, n = w_ada.shape
    b = c.shape[0]
    tn = 1024
    return pl.pallas_call(
        _ada_kernel,
        grid=(depth, n // tn),
        in_specs=[pl.BlockSpec((b, d), lambda l, j: (0, 0)),
                  pl.BlockSpec((1, d, tn), lambda l, j: (l, 0, j)),
                  pl.BlockSpec((1, 1, tn), lambda l, j: (l, 0, j))],
        out_specs=pl.BlockSpec((1, b, tn), lambda l, j: (l, 0, j)),
        out_shape=jax.ShapeDtypeStruct((depth, b, n), F32),
        compiler_params=_params(2),
        name="adaln",
    )(c, w_ada, b_ada.reshape(depth, 1, n))


def _rope_kernel(pos_ref, inv_ref, cos_ref, sina_ref, sinb_ref):
    ang = pos_ref[0].astype(F32) * inv_ref[...]
    lane = lax.broadcasted_iota(jnp.int32, (1, LANES), 1)
    sin = jnp.sin(ang)
    cos_ref[0] = jnp.cos(ang)
    lo, mid, hi = MLA_NOPE, MLA_NOPE + ROPE_HALF, MLA_NOPE + MLA_ROPE
    sina_ref[0] = jnp.where((lane >= lo) & (lane < mid), -sin, 0.0)
    sinb_ref[0] = jnp.where((lane >= mid) & (lane < hi), sin, 0.0)


def _rope_tables(positions):
    b, s = positions.shape
    ts = min(s, 512)
    inv = ROPE_BASE ** (-jnp.arange(ROPE_HALF, dtype=F32) / ROPE_HALF)
    inv_slot = jnp.concatenate([jnp.zeros((MLA_NOPE,), F32), inv, inv,
                                jnp.zeros((LANES - MLA_NOPE - MLA_ROPE,), F32)]).reshape(1, LANES)
    spec = pl.BlockSpec((1, ts, LANES), lambda i, j: (i, j, 0))
    return pl.pallas_call(
        _rope_kernel,
        grid=(b, s // ts),
        in_specs=[pl.BlockSpec((1, ts, 1), lambda i, j: (i, j, 0)),
                  pl.BlockSpec((1, LANES), lambda i, j: (0, 0))],
        out_specs=[spec, spec, spec],
        out_shape=[jax.ShapeDtypeStruct((b, s, LANES), F32)] * 3,
        compiler_params=_params(2),
        name="rope_tables",
    )(positions.reshape(b, s, 1), inv_slot)


def _rope_slot(x, cos, sina, sinb):
    return (x * cos + pltpu.roll(x, LANES - ROPE_HALF, 1) * sina
            + pltpu.roll(x, ROPE_HALF, 1) * sinb)


def _inproj_kernel(x_ref, mod_ref, g_ref, w_ref, bff_ref, gq_ref, gkv_ref, wuq_ref, wkn_ref,
                   wkv_ref, cos_ref, sina_ref, sinb_ref,
                   fox_ref, sb_ref, gate_ref, mq_ref, mkn_ref, mv_ref, mkr_ref, fcol_ref,
                   frow_ref, carry_ref, *, tm):
    x = x_ref[0]
    sh, sc = mod_ref[0, 0:1, :], mod_ref[0, 1:2, :]
    u = ((_rms(x) * g_ref[...]) * (1.0 + sc) + sh).astype(BF16)

    def seg(a, b):
        return _bdot(u, w_ref[:, a:b])

    for c in range(3):
        a = 512 * c
        fq = seg(C_FOX + a, C_FOX + a + 512)
        sq = seg(C_SB + a, C_SB + a + 512)
        if c == 0:
            fq, sq = fq * QK_SCALE, sq * QK_SCALE
        fox_ref[0, :, a:a + 512] = fq.astype(BF16)
        sb_ref[0, :, a:a + 512] = sq.astype(BF16)
    for c in range(6):
        a = 512 * c
        gate_ref[0, :, a:a + 512] = _sigmoid(seg(C_GATE + a, C_GATE + a + 512))

    cos, sina, sinb = cos_ref[0], sina_ref[0], sinb_ref[0]

    cq = (_rms(seg(C_QL, C_KVL)) * gq_ref[...]).astype(BF16)
    for h in range(N_HEADS):
        a = LANES * h
        q = _bdot(cq, wuq_ref[:, a:a + LANES])
        mq_ref[0, :, a:a + LANES] = _rope_slot(q, cos, sina, sinb).astype(BF16)
    ckv = (_rms(seg(C_KVL, C_KR)) * gkv_ref[...]).astype(BF16)
    mkn_ref[0] = _bdot(ckv, wkn_ref[...]).astype(BF16)
    mv_ref[0] = _bdot(ckv, wkv_ref[...]).astype(BF16)
    mkr_ref[0] = _rope_slot(seg(C_KR, C_FF), cos, sina, sinb).astype(BF16)

    @pl.when(pl.program_id(1) == 0)
    def _():
        carry_ref[...] = jnp.zeros_like(carry_ref)

    logf = _log_sigmoid(seg(C_FF, IN_W) + bff_ref[...])
    r = lax.broadcasted_iota(jnp.int32, (tm, tm), 0)
    c = lax.broadcasted_iota(jnp.int32, (tm, tm), 1)
    tri = jnp.where(r >= c, 1.0, 0.0).astype(BF16)
    hi, mid, lo = _split3(logf)
    cum = (_bdot(tri, hi) + _bdot(tri, mid) + _bdot(tri, lo)) + carry_ref[...]
    carry_ref[...] = cum[tm - 1:tm, :]
    fcol_ref[0] = cum[:, 0:N_HEADS]
    frow_ref[0] = cum.T[0:N_HEADS, :]


def _inproj(x, mod, g_mix, w_in_r, bff, g_q, g_kv, wuq_r, wkn_r, wkv_r, tables, *, tm):
    b, s, d = x.shape
    cos, sina, sinb = tables
    row = lambda w: pl.BlockSpec((1, tm, w), lambda i, j: (i, j, 0))
    out_shapes = [
        jax.ShapeDtypeStruct((b, s, 3 * HEAD_W), BF16),
        jax.ShapeDtypeStruct((b, s, 3 * HEAD_W), BF16),
        jax.ShapeDtypeStruct((b, s, 3 * D_MODEL), F32),
        jax.ShapeDtypeStruct((b, s, SLOT_W), BF16),
        jax.ShapeDtypeStruct((b, s, SLOT_W), BF16),
        jax.ShapeDtypeStruct((b, s, HEAD_W), BF16),
        jax.ShapeDtypeStruct((b, s, LANES), BF16),
        jax.ShapeDtypeStruct((b, s, N_HEADS), F32),
        jax.ShapeDtypeStruct((b, N_HEADS, s), F32),
    ]
    out_specs = [row(3 * HEAD_W), row(3 * HEAD_W), row(3 * D_MODEL), row(SLOT_W), row(SLOT_W),
                 row(HEAD_W), row(LANES), row(N_HEADS),
                 pl.BlockSpec((1, N_HEADS, tm), lambda i, j: (i, 0, j))]
    return pl.pallas_call(
        functools.partial(_inproj_kernel, tm=tm),
        grid=(b, s // tm),
        in_specs=[row(d),
                  pl.BlockSpec((1, 6, d), lambda i, j: (i, 0, 0)),
                  _resident((1, d)), _resident((d, IN_W)), _resident((1, LANES)),
                  _resident((1, MLA_Q_RANK)), _resident((1, MLA_KV_RANK)),
                  _resident((MLA_Q_RANK, SLOT_W)), _resident((MLA_KV_RANK, SLOT_W)),
                  _resident((MLA_KV_RANK, HEAD_W)),
                  row(LANES), row(LANES), row(LANES)],
        out_specs=out_specs,
        out_shape=out_shapes,
        scratch_shapes=[pltpu.VMEM((1, LANES), F32)],
        compiler_params=_params(2),
        name="inproj",
    )(x, mod, g_mix, w_in_r, bff, g_q, g_kv, wuq_r, wkn_r, wkv_r, cos, sina, sinb)


def _head_masks():
    lane = lax.broadcasted_iota(jnp.int32, (1, LANES), 1)
    return [(lane >= HEAD_DIM * hh) & (lane < HEAD_DIM * (hh + 1)) for hh in range(2)]


def _softmax_step(hh, a, bias, v1, acc_ref, m_ref):
    m_old = m_ref[hh]
    m_new = jnp.maximum(m_old, jnp.max(a, axis=1, keepdims=True) + bias)
    p = jnp.exp(a + (bias - m_new))
    alpha = jnp.exp(m_old - m_new)
    acc_ref[hh] = alpha * acc_ref[hh] + _bdot(p.astype(BF16), v1)
    m_ref[hh] = m_new


def _softmax_finish(o_ref, acc_ref, in_head):
    outs = []
    for hh in range(2):
        acc = acc_ref[hh]
        outs.append(acc / pltpu.roll(acc, HEAD_DIM, 1))
    o_ref[0] = jnp.where(in_head[0], outs[0], outs[1]).astype(o_ref.dtype)


def _fox_kernel(q_ref, k_ref, v_ref, fcol_ref, frow_ref, o_ref, acc_ref, m_ref, *, tq, tk):
    hp, qi = pl.program_id(1), pl.program_id(2)
    in_head = _head_masks()
    q = q_ref[0]
    fcol = fcol_ref[0]
    lane8 = lax.broadcasted_iota(jnp.int32, (1, N_HEADS), 1)
    qs = [jnp.where(in_head[hh], q, jnp.zeros_like(q)) for hh in range(2)]
    fq = [jnp.sum(jnp.where(lane8 == 2 * hp + hh, fcol, 0.0), axis=1, keepdims=True)
          for hh in range(2)]
    row = lax.broadcasted_iota(jnp.int32, (tq, 1), 0)
    col = lax.broadcasted_iota(jnp.int32, (1, tk), 1)
    acc_ref[...] = jnp.zeros_like(acc_ref)
    m_ref[...] = jnp.full_like(m_ref, NEG)

    def block(start, diag_off):
        k = k_ref[0, pl.ds(start, tk), :]
        v = v_ref[0, pl.ds(start, tk), :]
        for hh in range(2):
            a = _dot_nt(qs[hh], k) - frow_ref[0, 0, hh:hh + 1, pl.ds(start, tk)]
            if diag_off is not None:
                a = jnp.where(col + diag_off <= row, a, NEG)
            v1 = jnp.where(in_head[hh], v, jnp.ones_like(v))
            _softmax_step(hh, a, fq[hh], v1, acc_ref, m_ref)

    n_int = qi * (tq // tk)

    def body(kb, carry):
        block(pl.multiple_of(kb * tk, tk), None)
        return carry

    lax.fori_loop(0, n_int, body, 0)
    for j in range(tq // tk):
        block(pl.multiple_of(qi * tq + j * tk, tk), j * tk)
    _softmax_finish(o_ref, acc_ref, in_head)


def _fox_attention(qkv, fcol, frow, *, tq, tk):
    b, s, _ = qkv.shape
    frow4 = frow.reshape(b, N_HEADS // 2, 2, s)
    return pl.pallas_call(
        functools.partial(_fox_kernel, tq=tq, tk=tk),
        grid=(b, N_HEADS // 2, s // tq),
        in_specs=[pl.BlockSpec((1, tq, LANES), lambda i, h, j: (i, j, h)),
                  pl.BlockSpec((1, s, LANES), lambda i, h, j: (i, 0, 4 + h)),
                  pl.BlockSpec((1, s, LANES), lambda i, h, j: (i, 0, 8 + h)),
                  pl.BlockSpec((1, tq, N_HEADS), lambda i, h, j: (i, j, 0)),
                  pl.BlockSpec((1, 1, 2, s), lambda i, h, j: (i, h, 0, 0))],
        out_specs=pl.BlockSpec((1, tq, LANES), lambda i, h, j: (i, j, h)),
        out_shape=jax.ShapeDtypeStruct((b, s, HEAD_W), BF16),
        scratch_shapes=[pltpu.VMEM((2, tq, LANES), F32), pltpu.VMEM((2, tq, 1), F32)],
        compiler_params=_params(3),
        name="fox_attention",
    )(qkv, qkv, qkv, fcol, frow4)


def _mla_kernel(q_ref, kn_ref, kr_ref, v_ref, o_ref, acc_ref, m_ref, *, tq, tk):
    qi = pl.program_id(2)
    in_head = _head_masks()
    qs = [q_ref[0, :, LANES * hh:LANES * (hh + 1)] for hh in range(2)]
    row = lax.broadcasted_iota(jnp.int32, (tq, 1), 0)
    col = lax.broadcasted_iota(jnp.int32, (1, tk), 1)
    zero = jnp.zeros((tq, 1), F32)
    acc_ref[...] = jnp.zeros_like(acc_ref)
    m_ref[...] = jnp.full_like(m_ref, NEG)

    def block(start, diag_off):
        kr = kr_ref[0, pl.ds(start, tk), :]
        v = v_ref[0, pl.ds(start, tk), :]
        for hh in range(2):
            k = kn_ref[0, pl.ds(start, tk), LANES * hh:LANES * (hh + 1)] + kr
            a = _dot_nt(qs[hh], k) * MLA_SCALE
            if diag_off is not None:
                ok = ((col + diag_off) >> CHUNK_SHIFT) <= (row >> CHUNK_SHIFT)
                a = jnp.where(ok, a, NEG)
            v1 = jnp.where(in_head[hh], v, jnp.ones_like(v))
            _softmax_step(hh, a, zero, v1, acc_ref, m_ref)

    n_int = qi * (tq // tk)

    def body(kb, carry):
        block(pl.multiple_of(kb * tk, tk), None)
        return carry

    lax.fori_loop(0, n_int, body, 0)
    for j in range(tq // tk):
        block(pl.multiple_of(qi * tq + j * tk, tk), j * tk)
    _softmax_finish(o_ref, acc_ref, in_head)


def _mla_attention(mq, mkn, mkr, mv, *, tq, tk):
    b, s, _ = mq.shape
    return pl.pallas_call(
        functools.partial(_mla_kernel, tq=tq, tk=tk),
        grid=(b, N_HEADS // 2, s // tq),
        in_specs=[pl.BlockSpec((1, tq, 2 * LANES), lambda i, h, j: (i, j, h)),
                  pl.BlockSpec((1, s, 2 * LANES), lambda i, h, j: (i, 0, h)),
                  pl.BlockSpec((1, s, LANES), lambda i, h, j: (i, 0, 0)),
                  pl.BlockSpec((1, s, LANES), lambda i, h, j: (i, 0, h))],
        out_specs=pl.BlockSpec((1, tq, LANES), lambda i, h, j: (i, j, h)),
        out_shape=jax.ShapeDtypeStruct((b, s, HEAD_W), BF16),
        scratch_shapes=[pltpu.VMEM((2, tq, LANES), F32), pltpu.VMEM((2, tq, 1), F32)],
        compiler_params=_params(3),
        name="mla_attention",
    )(mq, mkn, mkr, mv)


def _sb_kernel(q_ref, k_ref, v_ref, o_ref, acc_ref, r_ref, *, tq, tk):
    qi = pl.program_id(2)
    in_head = _head_masks()
    q = q_ref[0]
    qs = [jnp.where(in_head[hh], q, jnp.zeros_like(q)) for hh in range(2)]
    row = lax.broadcasted_iota(jnp.int32, (tq, 1), 0)
    col = lax.broadcasted_iota(jnp.int32, (1, tk), 1)
    jj = lax.broadcasted_iota(jnp.int32, (tk, tk), 0)
    ss = lax.broadcasted_iota(jnp.int32, (tk, tk), 1)
    later = jnp.where(jj > ss, 1.0, 0.0).astype(BF16)
    acc_ref[...] = jnp.zeros_like(acc_ref)
    r_ref[...] = jnp.zeros_like(r_ref)

    def block(start, diag_off):
        k = k_ref[0, pl.ds(start, tk), :]
        v = v_ref[0, pl.ds(start, tk), :]
        upd = None
        for hh in range(2):
            z = _dot_nt(qs[hh], k)
            log_beta = _log_sigmoid(z)
            log_keep = log_beta - z
            if diag_off is not None:
                ok = col + diag_off < row
                log_keep = jnp.where(ok, log_keep, 0.0)
            hi = log_keep.astype(BF16)
            lo = (log_keep - hi.astype(F32)).astype(BF16)
            suffix = (_bdot(hi, later) + _bdot(lo, later)) + r_ref[hh]
            w = jnp.exp(log_beta + suffix)
            if diag_off is not None:
                w = jnp.where(ok, w, 0.0)
            vh = jnp.where(in_head[hh], v, jnp.zeros_like(v))
            pv = _bdot(w.astype(BF16), vh)
            upd = pv if upd is None else upd + pv
            r_ref[hh] = r_ref[hh] + jnp.sum(log_keep, axis=1, keepdims=True)
        acc_ref[...] = acc_ref[...] + upd

    for j in reversed(range(tq // tk)):
        block(pl.multiple_of(qi * tq + j * tk, tk), j * tk)
    n_int = qi * (tq // tk)

    def body(i, carry):
        block(pl.multiple_of((n_int - 1 - i) * tk, tk), None)
        return carry

    lax.fori_loop(0, n_int, body, 0)
    o_ref[0] = acc_ref[...].astype(o_ref.dtype)


def _sb_attention(qkv, *, tq, tk):
    b, s, _ = qkv.shape
    return pl.pallas_call(
        functools.partial(_sb_kernel, tq=tq, tk=tk),
        grid=(b, N_HEADS // 2, s // tq),
        in_specs=[pl.BlockSpec((1, tq, LANES), lambda i, h, j: (i, j, h)),
                  pl.BlockSpec((1, s, LANES), lambda i, h, j: (i, 0, 4 + h)),
                  pl.BlockSpec((1, s, LANES), lambda i, h, j: (i, 0, 8 + h))],
        out_specs=pl.BlockSpec((1, tq, LANES), lambda i, h, j: (i, j, h)),
        out_shape=jax.ShapeDtypeStruct((b, s, HEAD_W), BF16),
        scratch_shapes=[pltpu.VMEM((tq, LANES), F32), pltpu.VMEM((2, tq, 1), F32)],
        compiler_params=_params(3),
        name="sb_attention",
    )(qkv, qkv, qkv)


def _merge_kernel(x_ref, yf_ref, ym_ref, ys_ref, g_ref, mod_ref, wf_ref, wm_ref, ws_ref,
                  wo_ref, o_ref):
    d = D_MODEL
    merged = (g_ref[0, :, 0:d] * _bdot(yf_ref[0], wf_ref[...])
              + g_ref[0, :, d:2 * d] * _bdot(ym_ref[0], wm_ref[...])
              + g_ref[0, :, 2 * d:3 * d] * _bdot(ys_ref[0], ws_ref[...]))
    o_ref[0] = x_ref[0] + mod_ref[0, 2:3, :] * _bdot(merged.astype(BF16), wo_ref[...])


def _merge(x, y_fox, y_mla, y_sb, gates, mod, wf, wm, ws, wo, *, tm):
    b, s, d = x.shape
    row = lambda w: pl.BlockSpec((1, tm, w), lambda i, j: (i, j, 0))
    return pl.pallas_call(
        _merge_kernel,
        grid=(b, s // tm),
        in_specs=[row(d), row(HEAD_W), row(HEAD_W), row(HEAD_W), row(3 * d),
                  pl.BlockSpec((1, 6, d), lambda i, j: (i, 0, 0)),
                  _resident((HEAD_W, d)), _resident((HEAD_W, d)), _resident((HEAD_W, d)),
                  _resident((d, d))],
        out_specs=row(d),
        out_shape=jax.ShapeDtypeStruct((b, s, d), F32),
        compiler_params=_params(2),
        name="merge_out",
    )(x, y_fox, y_mla, y_sb, gates, mod, wf, wm, ws, wo)


def _ffn_kernel(x_ref, mod_ref, g_ref, wg_ref, wu_ref, wd_ref, gfin_ref, o_ref, *, final, fc):
    x = x_ref[0]
    sh, sc, gt = mod_ref[0, 3:4, :], mod_ref[0, 4:5, :], mod_ref[0, 5:6, :]
    u = ((_rms(x) * g_ref[...]) * (1.0 + sc) + sh).astype(BF16)
    acc = None
    for a in range(0, D_FF, fc):
        hg = _bdot(u, wg_ref[:, a:a + fc])
        hu = _bdot(u, wu_ref[:, a:a + fc])
        h = ((hg * _sigmoid(hg)) * hu).astype(BF16)
        part = _bdot(h, wd_ref[a:a + fc, :])
        acc = part if acc is None else acc + part
    out = x + gt * acc
    if final:
        out = _rms(out) * gfin_ref[...]
    o_ref[0] = out


def _ffn(x, mod, g_ffn, wg, wu, wd, g_final, *, tm, final):
    b, s, d = x.shape
    row = lambda w: pl.BlockSpec((1, tm, w), lambda i, j: (i, j, 0))
    return pl.pallas_call(
        functools.partial(_ffn_kernel, final=final, fc=D_FF // 2),
        grid=(b, s // tm),
        in_specs=[row(d), pl.BlockSpec((1, 6, d), lambda i, j: (i, 0, 0)), _resident((1, d)),
                  _resident((d, D_FF)), _resident((d, D_FF)), _resident((D_FF, d)),
                  _resident((1, d))],
        out_specs=row(d),
        out_shape=jax.ShapeDtypeStruct((b, s, d), F32),
        compiler_params=_params(2),
        name="ffn",
    )(x, mod, g_ffn, wg, wu, wd, g_final)


def _slot_cols(w, width_per_head, used):
    k = w.shape[0]
    w = w.reshape(k, N_HEADS, width_per_head)[:, :, :used]
    return jnp.pad(w, ((0, 0), (0, 0), (0, LANES - used))).reshape(k, SLOT_W)


def _layout_w_in(w):
    d = w.shape[0]
    fox, ff, ql, kvl, kr, sb, gate = jnp.split(
        w, [1536, 1544, 1928, 2184, 2216, 3752], axis=-1)
    z = lambda n: jnp.zeros((d, n), w.dtype)
    kr_slot = jnp.concatenate([z(MLA_NOPE), kr, z(LANES - MLA_NOPE - MLA_ROPE)], axis=-1)
    ff_slot = jnp.concatenate([ff, z(LANES - N_HEADS)], axis=-1)
    return jnp.concatenate([fox, sb, gate, ql, kvl, kr_slot, ff_slot], axis=-1).astype(BF16)


def kernel(x, c, positions, g_mix, w_ada, b_ada, w_in, b_fox_f, g_mla_q, w_mla_uq, g_mla_kv,
           w_mla_ukv, w_o_fox, w_o_mla, w_o_sb, w_out, g_ffn, w_ffn_gate, w_ffn_up, w_ffn_down,
           g_final):
    b, s, d = x.shape
    depth = w_in.shape[0]
    tm = min(s, 512)
    tq, tk = min(s, 256), min(s, 256)

    mod = _ada(c, w_ada, b_ada).reshape(depth, b, 6, d)
    tables = _rope_tables(positions)
    for l in range(depth):
        w_in_r = _layout_w_in(w_in[l])
        wuq_r = _slot_cols(w_mla_uq[l], MLA_NOPE + MLA_ROPE, MLA_NOPE + MLA_ROPE).astype(BF16)
        wkn_r = _slot_cols(w_mla_ukv[l], 2 * HEAD_DIM, MLA_NOPE).astype(BF16)
        wkv_r = w_mla_ukv[l].reshape(MLA_KV_RANK, N_HEADS, 2 * HEAD_DIM)[:, :, MLA_NOPE:]
        wkv_r = wkv_r.reshape(MLA_KV_RANK, HEAD_W).astype(BF16)
        bff = jnp.pad(b_fox_f[l], (0, LANES - N_HEADS)).reshape(1, LANES)

        fox_qkv, sb_qkv, gates, mq, mkn, mv, mkr, fcol, frow = _inproj(
            x, mod[l], g_mix[l].reshape(1, d), w_in_r, bff, g_mla_q[l].reshape(1, -1),
            g_mla_kv[l].reshape(1, -1), wuq_r, wkn_r, wkv_r, tables, tm=tm)
        y_fox = _fox_attention(fox_qkv, fcol, frow, tq=tq, tk=tk)
        y_mla = _mla_attention(mq, mkn, mkr, mv, tq=tq, tk=tk)
        y_sb = _sb_attention(sb_qkv, tq=tq, tk=tk)
        x = _merge(x, y_fox, y_mla, y_sb, gates, mod[l], w_o_fox[l].astype(BF16),
                   w_o_mla[l].astype(BF16), w_o_sb[l].astype(BF16), w_out[l].astype(BF16), tm=tm)
        x = _ffn(x, mod[l], g_ffn[l].reshape(1, d), w_ffn_gate[l].astype(BF16),
                 w_ffn_up[l].astype(BF16), w_ffn_down[l].astype(BF16), g_final.reshape(1, d),
                 tm=tm, final=(l == depth - 1))
    return x
```

```python
import functools

import numpy as np
import jax
import jax.numpy as jnp
from jax import lax
from jax.experimental import pallas as pl
from jax.experimental.pallas import tpu as pltpu

F32 = jnp.float32
BF16 = jnp.bfloat16

D_MODEL = 1024
EPS = 1e-6
HEAD_DIM = 64
N_HEADS = 8
HEAD_W = N_HEADS * HEAD_DIM
CHUNK_SHIFT = 6
MLA_Q_RANK = 384
MLA_KV_RANK = 256
MLA_NOPE = 64
MLA_ROPE = 32
ROPE_HALF = MLA_ROPE // 2
ROPE_BASE = 10000.0
MLA_SCALE = (MLA_NOPE + MLA_ROPE) ** -0.5
QK_SCALE = HEAD_DIM ** -0.5
D_FF = 2816
LANES = 128
SLOT_W = N_HEADS * LANES
N_PIECES = 3

C_FQ = 0
C_FK = 1024
C_FV = 2048
C_SB = 2560
C_GATE = 4096
C_QL = 7168
C_KVL = 7552
C_KR = 7808
C_FF = 7936
IN_W = 8064

NEG = -1e30
VMEM_LIMIT = 56 * 1024 * 1024


def _params(n_grid):
    return pltpu.CompilerParams(dimension_semantics=("arbitrary",) * n_grid,
                                vmem_limit_bytes=VMEM_LIMIT)


def _resident(shape):
    nd = len(shape)
    return pl.BlockSpec(shape, lambda *_: (0,) * nd, pipeline_mode=pl.Buffered(1))


def _bdot(a, b):
    return jnp.dot(a, b, preferred_element_type=F32)


def _dot_nt(a, b):
    return lax.dot_general(a, b, (((1,), (1,)), ((), ())), preferred_element_type=F32)


def _dot_tn(a, b):
    return lax.dot_general(a, b, (((0,), (0,)), ((), ())), preferred_element_type=F32)


def _split3(v):
    hi = v.astype(BF16)
    r = v - hi.astype(F32)
    mid = r.astype(BF16)
    lo = (r - mid.astype(F32)).astype(BF16)
    return hi, mid, lo


def _log_sigmoid(z):
    return jnp.minimum(z, 0.0) - jnp.log1p(jnp.exp(-jnp.abs(z)))


def _sigmoid(z):
    return 1.0 / (1.0 + jnp.exp(-z))


def _rms(x):
    return x * lax.rsqrt(jnp.mean(x * x, axis=-1, keepdims=True) + EPS)


def _ada_kernel(c_ref, w_ref, b_ref, o_ref):
    c = c_ref[...]
    cond = c * _sigmoid(c)
    chi, cmid, clo = _split3(cond)
    whi, wmid, wlo = _split3(w_ref[0])
    acc = _bdot(chi, whi) + (_bdot(chi, wmid) + _bdot(cmid, whi))
    acc = acc + (_bdot(chi, wlo) + _bdot(clo, whi) + _bdot(cmid, wmid))
    o_ref[0] = acc + b_ref[0]


def _ada(c, w_ada, b_ada):
    depth, d, n = w_ada.shape
    b = c.shape[0]
    tn = 1024
    return pl.pallas_call(
        _ada_kernel,
        grid=(depth, n // tn),
        in_specs=[pl.BlockSpec((b, d), lambda l, j: (0, 0)),
                  pl.BlockSpec((1, d, tn), lambda l, j: (l, 0, j)),
                  pl.BlockSpec((1, 1, tn), lambda l, j: (l, 0, j))],
        out_specs=pl.BlockSpec((1, b, tn), lambda l, j: (l, 0, j)),
        out_shape=jax.ShapeDtypeStruct((depth, b, n), F32),
        compiler_params=_params(2),
        name="adaln",
    )(c, w_ada, b_ada.reshape(depth, 1, n))


def _rope_kernel(pos_ref, inv_ref, cos_ref, sina_ref, sinb_ref):
    ang = pos_ref[0].astype(F32) * inv_ref[...]
    lane = lax.broadcasted_iota(jnp.int32, (1, LANES), 1)
    sin = jnp.sin(ang)
    cos_ref[0] = jnp.cos(ang)
    lo, mid, hi = MLA_NOPE, MLA_NOPE + ROPE_HALF, MLA_NOPE + MLA_ROPE
    sina_ref[0] = jnp.where((lane >= lo) & (lane < mid), -sin, 0.0)
    sinb_ref[0] = jnp.where((lane >= mid) & (lane < hi), sin, 0.0)


def _rope_tables(positions):
    b, s = positions.shape
    ts = min(s, 512)
    inv = ROPE_BASE ** (-jnp.arange(ROPE_HALF, dtype=F32) / ROPE_HALF)
    inv_slot = jnp.concatenate([jnp.zeros((MLA_NOPE,), F32), inv, inv,
                                jnp.zeros((LANES - MLA_NOPE - MLA_ROPE,), F32)]).reshape(1, LANES)
    spec = pl.BlockSpec((1, ts, LANES), lambda i, j: (i, j, 0))
    return pl.pallas_call(
        _rope_kernel,
        grid=(b, s // ts),
        in_specs=[pl.BlockSpec((1, ts, 1), lambda i, j: (i, j, 0)),
                  pl.BlockSpec((1, LANES), lambda i, j: (0, 0))],
        out_specs=[spec, spec, spec],
        out_shape=[jax.ShapeDtypeStruct((b, s, LANES), F32)] * 3,
        compiler_params=_params(2),
        name="rope_tables",
    )(positions.reshape(b, s, 1), inv_slot)


def _rope_slot(x, cos, sina, sinb):
    return (x * cos + pltpu.roll(x, LANES - ROPE_HALF, 1) * sina
            + pltpu.roll(x, ROPE_HALF, 1) * sinb)


def _inproj_kernel(x_ref, mod_ref, g_ref, w_ref, bff_ref, gq_ref, gkv_ref, wuq_ref, wkn_ref,
                   wkv_ref, cos_ref, sina_ref, sinb_ref, place_ref, qc_ref, kc_ref,
                   fq_ref, fk_ref, fv_ref, sb_ref, gate_ref, mq_ref, mk_ref, mv_ref,
                   carry_ref, *, tm):
    x = x_ref[0]
    sh, sc = mod_ref[0, 0:1, :], mod_ref[0, 1:2, :]
    u = ((_rms(x) * g_ref[...]) * (1.0 + sc) + sh).astype(BF16)

    def seg(a, b):
        return _bdot(u, w_ref[:, a:b])

    @pl.when(pl.program_id(1) == 0)
    def _():
        carry_ref[...] = jnp.zeros_like(carry_ref)

    logf = _log_sigmoid(seg(C_FF, IN_W) + bff_ref[...])
    r = lax.broadcasted_iota(jnp.int32, (tm, tm), 0)
    c = lax.broadcasted_iota(jnp.int32, (tm, tm), 1)
    tri = jnp.where(r >= c, 1.0, 0.0).astype(BF16)
    hi, mid, lo = _split3(logf)
    cum = (_bdot(tri, hi) + _bdot(tri, mid) + _bdot(tri, lo)) + carry_ref[...]
    carry_ref[...] = cum[tm - 1:tm, :]
    p0, p1, p2 = (p.astype(F32) for p in _split3(cum))
    lane = lax.broadcasted_iota(jnp.int32, (1, LANES), 1)
    pieces = jnp.where(lane < N_HEADS, p0,
                       jnp.where(lane < 2 * N_HEADS, pltpu.roll(p1, N_HEADS, 1),
                                 jnp.where(lane < 3 * N_HEADS, pltpu.roll(p2, 2 * N_HEADS, 1),
                                           0.0))).astype(BF16)

    for c in range(2):
        a = 512 * c
        placed = _bdot(pieces, place_ref[:, a:a + 512])
        q = seg(C_FQ + a, C_FQ + a + 512) * QK_SCALE + placed + qc_ref[:, a:a + 512]
        fq_ref[0, :, a:a + 512] = q.astype(BF16)
        placed = _bdot(pieces, place_ref[:, SLOT_W + a:SLOT_W + a + 512])
        k = seg(C_FK + a, C_FK + a + 512) + placed + kc_ref[:, a:a + 512]
        fk_ref[0, :, a:a + 512] = k.astype(BF16)
    fv_ref[0] = seg(C_FV, C_SB).astype(BF16)

    for c in range(3):
        a = 512 * c
        sq = seg(C_SB + a, C_SB + a + 512)
        if c == 0:
            sq = sq * QK_SCALE
        sb_ref[0, :, a:a + 512] = sq.astype(BF16)
    for c in range(6):
        a = 512 * c
        gate_ref[0, :, a:a + 512] = _sigmoid(seg(C_GATE + a, C_GATE + a + 512))

    cos, sina, sinb = cos_ref[0], sina_ref[0], sinb_ref[0]
    cq = (_rms(seg(C_QL, C_KVL)) * gq_ref[...]).astype(BF16)
    ckv = (_rms(seg(C_KVL, C_KR)) * gkv_ref[...]).astype(BF16)
    kr = _rope_slot(seg(C_KR, C_FF), cos, sina, sinb)
    for h in range(N_HEADS):
        a = LANES * h
        q = _rope_slot(_bdot(cq, wuq_ref[:, a:a + LANES]), cos, sina, sinb)
        mq_ref[0, :, a:a + LANES] = (q * MLA_SCALE).astype(BF16)
        mk_ref[0, :, a:a + LANES] = (_bdot(ckv, wkn_ref[:, a:a + LANES]) + kr).astype(BF16)
    mv_ref[0] = _bdot(ckv, wkv_ref[...]).astype(BF16)


def _fox_placement():
    place = np.zeros((LANES, 2 * SLOT_W), np.float32)
    qc = np.zeros((1, SLOT_W), np.float32)
    kc = np.zeros((1, SLOT_W), np.float32)
    for h in range(N_HEADS):
        for p in range(N_PIECES):
            place[N_HEADS * p + h, LANES * h + HEAD_DIM + N_PIECES + p] = 1.0
            place[N_HEADS * p + h, SLOT_W + LANES * h + HEAD_DIM + p] = 1.0
            qc[0, LANES * h + HEAD_DIM + p] = -1.0
            kc[0, LANES * h + HEAD_DIM + N_PIECES + p] = 1.0
    return jnp.asarray(place, BF16), jnp.asarray(qc), jnp.asarray(kc)


def _inproj(x, mod, g_mix, w_in_r, bff, g_q, g_kv, wuq_r, wkn_r, wkv_r, tables, *, tm):
    b, s, d = x.shape
    cos, sina, sinb = tables
    place, qc, kc = _fox_placement()
    row = lambda w: pl.BlockSpec((1, tm, w), lambda i, j: (i, j, 0))
    widths = [(SLOT_W, BF16),
              (SLOT_W, BF16),
              (HEAD_W, BF16),
              (3 * HEAD_W, BF16),
              (3 * D_MODEL, F32),
              (SLOT_W, BF16),
              (SLOT_W, BF16),
              (HEAD_W, BF16)]
    return pl.pallas_call(
        functools.partial(_inproj_kernel, tm=tm),
        grid=(b, s // tm),
        in_specs=[row(d),
                  pl.BlockSpec((1, 6, d), lambda i, j: (i, 0, 0)),
                  _resident((1, d)), _resident((d, IN_W)), _resident((1, LANES)),
                  _resident((1, MLA_Q_RANK)), _resident((1, MLA_KV_RANK)),
                  _resident((MLA_Q_RANK, SLOT_W)), _resident((MLA_KV_RANK, SLOT_W)),
                  _resident((MLA_KV_RANK, HEAD_W)),
                  row(LANES), row(LANES), row(LANES),
                  _resident((LANES, 2 * SLOT_W)), _resident((1, SLOT_W)), _resident((1, SLOT_W))],
        out_specs=[row(w) for w, _ in widths],
        out_shape=[jax.ShapeDtypeStruct((b, s, w), dt) for w, dt in widths],
        scratch_shapes=[pltpu.VMEM((1, LANES), F32)],
        compiler_params=_params(2),
        name="inproj",
    )(x, mod, g_mix, w_in_r, bff, g_q, g_kv, wuq_r, wkn_r, wkv_r, cos, sina, sinb, place, qc, kc)


def _head_masks():
    lane = lax.broadcasted_iota(jnp.int32, (1, LANES), 1)
    return [(lane >= HEAD_DIM * hh) & (lane < HEAD_DIM * (hh + 1)) for hh in range(2)]


def _softmax_attn_kernel(q_ref, k_ref, v_ref, o_ref, acc_ref, *, tq, tk, chunk_shift):
    qi = pl.program_id(2)
    in_head = _head_masks()
    qs = [q_ref[0, :, LANES * hh:LANES * (hh + 1)] for hh in range(2)]
    qpos = lax.broadcasted_iota(jnp.int32, (1, tq), 1)
    acc_ref[...] = jnp.zeros_like(acc_ref)

    def block(start, size, m, diag):
        v = v_ref[0, pl.ds(start, size), :]
        m_out = []
        for hh in range(2):
            k = k_ref[0, pl.ds(start, size), LANES * hh:LANES * (hh + 1)]
            s = _dot_nt(k, qs[hh])
            if diag:
                kpos = lax.broadcasted_iota(jnp.int32, (size, 1), 0)
                s = jnp.where((kpos >> chunk_shift) <= (qpos >> chunk_shift), s, NEG)
            m_new = jnp.maximum(m[hh], jnp.max(s, axis=0, keepdims=True))
            p = jnp.exp(s - m_new)
            alpha = jnp.exp(m[hh] - m_new)
            v1 = jnp.where(in_head[hh], v, jnp.ones_like(v))
            acc_ref[hh] = acc_ref[hh] * alpha + _dot_tn(v1, p.astype(BF16))
            m_out.append(m_new)
        return tuple(m_out)

    ratio = tk // tq
    n_full, n_rem = qi // ratio, qi % ratio
    m = (jnp.full((1, tq), NEG, F32),) * 2
    m = lax.fori_loop(
        0, n_full, lambda i, m: block(pl.multiple_of(i * tk, tk), tk, m, False), m)
    m = lax.fori_loop(
        0, n_rem, lambda i, m: block(pl.multiple_of(n_full * tk + i * tq, tq), tq, m, False), m)
    block(pl.multiple_of(qi * tq, tq), tq, m, True)

    sub = lax.broadcasted_iota(jnp.int32, (LANES, 1), 0)
    outs = []
    for hh in range(2):
        acc = acc_ref[hh]
        denom = acc[HEAD_DIM * (1 - hh):HEAD_DIM * (1 - hh) + 1, :]
        outs.append(acc / denom)
    o_ref[0] = jnp.where(sub < HEAD_DIM, outs[0], outs[1]).T.astype(o_ref.dtype)


def _softmax_attention(q, k, v, *, tq, tk, chunk_shift, name):
    b, s, _ = q.shape
    return pl.pallas_call(
        functools.partial(_softmax_attn_kernel, tq=tq, tk=tk, chunk_shift=chunk_shift),
        grid=(b, N_HEADS // 2, s // tq),
        in_specs=[pl.BlockSpec((1, tq, 2 * LANES), lambda i, h, j: (i, j, h)),
                  pl.BlockSpec((1, s, 2 * LANES), lambda i, h, j: (i, 0, h)),
                  pl.BlockSpec((1, s, LANES), lambda i, h, j: (i, 0, h))],
        out_specs=pl.BlockSpec((1, tq, LANES), lambda i, h, j: (i, j, h)),
        out_shape=jax.ShapeDtypeStruct((b, s, HEAD_W), BF16),
        scratch_shapes=[pltpu.VMEM((2, LANES, tq), F32)],
        compiler_params=_params(3),
        name=name,
    )(q, k, v)


def _sb_kernel(q_ref, k_ref, v_ref, o_ref, acc_ref, *, t, unroll):
    qi = pl.program_id(2)
    in_head = _head_masks()
    q = q_ref[0]
    qs = [jnp.where(in_head[hh], q, jnp.zeros_like(q)) for hh in range(2)]
    qpos = lax.broadcasted_iota(jnp.int32, (1, t), 1)
    kpos = lax.broadcasted_iota(jnp.int32, (t, 1), 0)
    ss = lax.broadcasted_iota(jnp.int32, (t, t), 0)
    jj = lax.broadcasted_iota(jnp.int32, (t, t), 1)
    later = jnp.where(jj > ss, 1.0, 0.0).astype(BF16)
    acc_ref[...] = jnp.zeros_like(acc_ref)

    def block(start, r, diag):
        k = k_ref[0, pl.ds(start, t), :]
        v = v_ref[0, pl.ds(start, t), :]
        upd, r_out = None, []
        for hh in range(2):
            z = _dot_nt(k, qs[hh])
            log_beta = _log_sigmoid(z)
            log_keep = log_beta - z
            if diag:
                ok = kpos < qpos
                log_keep = jnp.where(ok, log_keep, 0.0)
            hi = log_keep.astype(BF16)
            lo = (log_keep - hi.astype(F32)).astype(BF16)
            suffix = (_bdot(later, hi) + _bdot(later, lo)) + r[hh]
            w = jnp.exp(log_beta + suffix)
            if diag:
                w = jnp.where(ok, w, 0.0)
            vh = jnp.where(in_head[hh], v, jnp.zeros_like(v))
            pv = _dot_tn(vh, w.astype(BF16))
            upd = pv if upd is None else upd + pv
            r_out.append(r[hh] + jnp.sum(log_keep, axis=0, keepdims=True))
        acc_ref[...] = acc_ref[...] + upd
        return tuple(r_out)

    r = (jnp.zeros((1, t), F32),) * 2
    r = block(pl.multiple_of(qi * t, t), r, True)
    n_full, n_rem = qi // unroll, qi % unroll
    r = lax.fori_loop(
        0, n_rem, lambda i, r: block(pl.multiple_of((qi - 1 - i) * t, t), r, False), r)

    def body(i, r):
        top = (n_full - i) * unroll - 1
        for u in range(unroll):
            r = block(pl.multiple_of((top - u) * t, t), r, False)
        return r

    lax.fori_loop(0, n_full, body, r)
    o_ref[0] = acc_ref[...].T.astype(o_ref.dtype)


def _sb_attention(qkv, *, t, unroll):
    b, s, _ = qkv.shape
    return pl.pallas_call(
        functools.partial(_sb_kernel, t=t, unroll=unroll),
        grid=(b, N_HEADS // 2, s // t),
        in_specs=[pl.BlockSpec((1, t, LANES), lambda i, h, j: (i, j, h)),
                  pl.BlockSpec((1, s, LANES), lambda i, h, j: (i, 0, 4 + h)),
                  pl.BlockSpec((1, s, LANES), lambda i, h, j: (i, 0, 8 + h))],
        out_specs=pl.BlockSpec((1, t, LANES), lambda i, h, j: (i, j, h)),
        out_shape=jax.ShapeDtypeStruct((b, s, HEAD_W), BF16),
        scratch_shapes=[pltpu.VMEM((LANES, t), F32)],
        compiler_params=_params(3),
        name="sb_attention",
    )(qkv, qkv, qkv)


def _merge_kernel(x_ref, yf_ref, ym_ref, ys_ref, g_ref, mod_ref, wf_ref, wm_ref, ws_ref,
                  wo_ref, o_ref):
    d = D_MODEL
    merged = (g_ref[0, :, 0:d] * _bdot(yf_ref[0], wf_ref[...])
              + g_ref[0, :, d:2 * d] * _bdot(ym_ref[0], wm_ref[...])
              + g_ref[0, :, 2 * d:3 * d] * _bdot(ys_ref[0], ws_ref[...]))
    o_ref[0] = x_ref[0] + mod_ref[0, 2:3, :] * _bdot(merged.astype(BF16), wo_ref[...])


def _merge(x, y_fox, y_mla, y_sb, gates, mod, wf, wm, ws, wo, *, tm):
    b, s, d = x.shape
    row = lambda w: pl.BlockSpec((1, tm, w), lambda i, j: (i, j, 0))
    return pl.pallas_call(
        _merge_kernel,
        grid=(b, s // tm),
        in_specs=[row(d), row(HEAD_W), row(HEAD_W), row(HEAD_W), row(3 * d),
                  pl.BlockSpec((1, 6, d), lambda i, j: (i, 0, 0)),
                  _resident((HEAD_W, d)), _resident((HEAD_W, d)), _resident((HEAD_W, d)),
                  _resident((d, d))],
        out_specs=row(d),
        out_shape=jax.ShapeDtypeStruct((b, s, d), F32),
        compiler_params=_params(2),
        name="merge_out",
    )(x, y_fox, y_mla, y_sb, gates, mod, wf, wm, ws, wo)


def _ffn_kernel(x_ref, mod_ref, g_ref, wg_ref, wu_ref, wd_ref, gfin_ref, o_ref, *, final, fc):
    x = x_ref[0]
    sh, sc, gt = mod_ref[0, 3:4, :], mod_ref[0, 4:5, :], mod_ref[0, 5:6, :]
    u = ((_rms(x) * g_ref[...]) * (1.0 + sc) + sh).astype(BF16)
    acc = None
    for a in range(0, D_FF, fc):
        hg = _bdot(u, wg_ref[:, a:a + fc])
        hu = _bdot(u, wu_ref[:, a:a + fc])
        h = ((hg * _sigmoid(hg)) * hu).astype(BF16)
        part = _bdot(h, wd_ref[a:a + fc, :])
        acc = part if acc is None else acc + part
    out = x + gt * acc
    if final:
        out = _rms(out) * gfin_ref[...]
    o_ref[0] = out


def _ffn(x, mod, g_ffn, wg, wu, wd, g_final, *, tm, final):
    b, s, d = x.shape
    row = lambda w: pl.BlockSpec((1, tm, w), lambda i, j: (i, j, 0))
    return pl.pallas_call(
        functools.partial(_ffn_kernel, final=final, fc=D_FF // 2),
        grid=(b, s // tm),
        in_specs=[row(d), pl.BlockSpec((1, 6, d), lambda i, j: (i, 0, 0)), _resident((1, d)),
                  _resident((d, D_FF)), _resident((d, D_FF)), _resident((D_FF, d)),
                  _resident((1, d))],
        out_specs=row(d),
        out_shape=jax.ShapeDtypeStruct((b, s, d), F32),
        compiler_params=_params(2),
        name="ffn",
    )(x, mod, g_ffn, wg, wu, wd, g_final)


def _slot_cols(w, width_per_head, used):
    k = w.shape[0]
    w = w.reshape(k, N_HEADS, width_per_head)[:, :, :used]
    return jnp.pad(w, ((0, 0), (0, 0), (0, LANES - used))).reshape(k, SLOT_W)


def _layout_w_in(w):
    d = w.shape[0]
    fox, ff, ql, kvl, kr, sb, gate = jnp.split(
        w, [1536, 1544, 1928, 2184, 2216, 3752], axis=-1)
    z = lambda n: jnp.zeros((d, n), w.dtype)
    fq, fk, fv = jnp.split(fox, 3, axis=-1)
    kr_slot = jnp.concatenate([z(MLA_NOPE), kr, z(LANES - MLA_NOPE - MLA_ROPE)], axis=-1)
    ff_slot = jnp.concatenate([ff, z(LANES - N_HEADS)], axis=-1)
    cols = [_slot_cols(fq, HEAD_DIM, HEAD_DIM), _slot_cols(fk, HEAD_DIM, HEAD_DIM), fv, sb, gate,
            ql, kvl, kr_slot, ff_slot]
    return jnp.concatenate(cols, axis=-1).astype(BF16)


def kernel(x, c, positions, g_mix, w_ada, b_ada, w_in, b_fox_f, g_mla_q, w_mla_uq, g_mla_kv,
           w_mla_ukv, w_o_fox, w_o_mla, w_o_sb, w_out, g_ffn, w_ffn_gate, w_ffn_up, w_ffn_down,
           g_final):
    b, s, d = x.shape
    depth = w_in.shape[0]
    tm = min(s, 512)
    tq = min(s, 256)
    tk = min(s, 512)

    mod = _ada(c, w_ada, b_ada).reshape(depth, b, 6, d)
    tables = _rope_tables(positions)
    for l in range(depth):
        w_in_r = _layout_w_in(w_in[l])
        wuq_r = _slot_cols(w_mla_uq[l], MLA_NOPE + MLA_ROPE, MLA_NOPE + MLA_ROPE).astype(BF16)
        wkn_r = _slot_cols(w_mla_ukv[l], 2 * HEAD_DIM, MLA_NOPE).astype(BF16)
        wkv_r = w_mla_ukv[l].reshape(MLA_KV_RANK, N_HEADS, 2 * HEAD_DIM)[:, :, MLA_NOPE:]
        wkv_r = wkv_r.reshape(MLA_KV_RANK, HEAD_W).astype(BF16)
        bff = jnp.pad(b_fox_f[l], (0, LANES - N_HEADS)).reshape(1, LANES)

        fox_q, fox_k, fox_v, sb_qkv, gates, mq, mk, mv = _inproj(
            x, mod[l], g_mix[l].reshape(1, d), w_in_r, bff, g_mla_q[l].reshape(1, -1),
            g_mla_kv[l].reshape(1, -1), wuq_r, wkn_r, wkv_r, tables, tm=tm)
        y_fox = _softmax_attention(fox_q, fox_k, fox_v, tq=tq, tk=tk, chunk_shift=0,
                                   name="fox_attention")
        y_mla = _softmax_attention(mq, mk, mv, tq=tq, tk=tk, chunk_shift=CHUNK_SHIFT,
                                   name="mla_attention")
        y_sb = _sb_attention(sb_qkv, t=tq, unroll=2)
        x = _merge(x, y_fox, y_mla, y_sb, gates, mod[l], w_o_fox[l].astype(BF16),
                   w_o_mla[l].astype(BF16), w_o_sb[l].astype(BF16), w_out[l].astype(BF16), tm=tm)
        x = _ffn(x, mod[l], g_ffn[l].reshape(1, d), w_ffn_gate[l].astype(BF16),
                 w_ffn_up[l].astype(BF16), w_ffn_down[l].astype(BF16), g_final.reshape(1, d),
                 tm=tm, final=(l == depth - 1))
    return x
```

```python
import functools

import numpy as np
import jax
import jax.numpy as jnp
from jax import lax
from jax.experimental import pallas as pl
from jax.experimental.pallas import tpu as pltpu

F32 = jnp.float32
BF16 = jnp.bfloat16

D_MODEL = 1024
EPS = 1e-6
HEAD_DIM = 64
N_HEADS = 8
HEAD_W = N_HEADS * HEAD_DIM
CHUNK_SHIFT = 6
MLA_Q_RANK = 384
MLA_KV_RANK = 256
MLA_NOPE = 64
MLA_ROPE = 32
ROPE_HALF = MLA_ROPE // 2
ROPE_BASE = 10000.0
MLA_SCALE = (MLA_NOPE + MLA_ROPE) ** -0.5
QK_SCALE = HEAD_DIM ** -0.5
D_FF = 2816
LANES = 128
SLOT_W = N_HEADS * LANES
N_PIECES = 3

C_FQ = 0
C_FK = 1024
C_FV = 2048
C_SB = 2560
C_GATE = 4096
C_QL = 7168
C_KVL = 7552
C_KR = 7808
C_FF = 7936
IN_W = 8064

NEG = -1e30
VMEM_LIMIT = 56 * 1024 * 1024


def _params(n_grid):
    return pltpu.CompilerParams(dimension_semantics=("arbitrary",) * n_grid,
                                vmem_limit_bytes=VMEM_LIMIT)


def _resident(shape):
    nd = len(shape)
    return pl.BlockSpec(shape, lambda *_: (0,) * nd, pipeline_mode=pl.Buffered(1))


def _bdot(a, b):
    return jnp.dot(a, b, preferred_element_type=F32)


def _dot_nt(a, b):
    return lax.dot_general(a, b, (((1,), (1,)), ((), ())), preferred_element_type=F32)


def _dot_tn(a, b):
    return lax.dot_general(a, b, (((0,), (0,)), ((), ())), preferred_element_type=F32)


def _split3(v):
    hi = v.astype(BF16)
    r = v - hi.astype(F32)
    mid = r.astype(BF16)
    lo = (r - mid.astype(F32)).astype(BF16)
    return hi, mid, lo


def _log_sigmoid(z):
    return jnp.minimum(z, 0.0) - jnp.log1p(jnp.exp(-jnp.abs(z)))


def _sigmoid(z):
    return 1.0 / (1.0 + jnp.exp(-z))


def _rms(x):
    return x * lax.rsqrt(jnp.mean(x * x, axis=-1, keepdims=True) + EPS)


def _ada_kernel(c_ref, w_ref, b_ref, o_ref):
    c = c_ref[...]
    cond = c * _sigmoid(c)
    chi, cmid, clo = _split3(cond)
    whi, wmid, wlo = _split3(w_ref[0])
    acc = _bdot(chi, whi) + (_bdot(chi, wmid) + _bdot(cmid, whi))
    acc = acc + (_bdot(chi, wlo) + _bdot(clo, whi) + _bdot(cmid, wmid))
    o_ref[0] = acc + b_ref[0]


def _ada(c, w_ada, b_ada):
    depth, d, n = w_ada.shape
    b = c.shape[0]
    tn = 1024
    return pl.pallas_call(
        _ada_kernel,
        grid=(depth, n // tn),
        in_specs=[pl.BlockSpec((b, d), lambda l, j: (0, 0)),
                  pl.BlockSpec((1, d, tn), lambda l, j: (l, 0, j)),
                  pl.BlockSpec((1, 1, tn), lambda l, j: (l, 0, j))],
        out_specs=pl.BlockSpec((1, b, tn), lambda l, j: (l, 0, j)),
        out_shape=jax.ShapeDtypeStruct((depth, b, n), F32),
        compiler_params=_params(2),
        name="adaln",
    )(c, w_ada, b_ada.reshape(depth, 1, n))


def _rope_kernel(pos_ref, inv_ref, cos_ref, sina_ref, sinb_ref):
    ang = pos_ref[0].astype(F32) * inv_ref[...]
    lane = lax.broadcasted_iota(jnp.int32, (1, LANES), 1)
    sin = jnp.sin(ang)
    cos_ref[0] = jnp.cos(ang)
    lo, mid, hi = MLA_NOPE, MLA_NOPE + ROPE_HALF, MLA_NOPE + MLA_ROPE
    sina_ref[0] = jnp.where((lane >= lo) & (lane < mid), -sin, 0.0)
    sinb_ref[0] = jnp.where((lane >= mid) & (lane < hi), sin, 0.0)


def _rope_tables(positions):
    b, s = positions.shape
    ts = min(s, 512)
    inv = ROPE_BASE ** (-jnp.arange(ROPE_HALF, dtype=F32) / ROPE_HALF)
    inv_slot = jnp.concatenate([jnp.zeros((MLA_NOPE,), F32), inv, inv,
                                jnp.zeros((LANES - MLA_NOPE - MLA_ROPE,), F32)]).reshape(1, LANES)
    spec = pl.BlockSpec((1, ts, LANES), lambda i, j: (i, j, 0))
    return pl.pallas_call(
        _rope_kernel,
        grid=(b, s // ts),
        in_specs=[pl.BlockSpec((1, ts, 1), lambda i, j: (i, j, 0)),
                  pl.BlockSpec((1, LANES), lambda i, j: (0, 0))],
        out_specs=[spec, spec, spec],
        out_shape=[jax.ShapeDtypeStruct((b, s, LANES), F32)] * 3,
        compiler_params=_params(2),
        name="rope_tables",
    )(positions.reshape(b, s, 1), inv_slot)


def _rope_slot(x, cos, sina, sinb):
    return (x * cos + pltpu.roll(x, LANES - ROPE_HALF, 1) * sina
            + pltpu.roll(x, ROPE_HALF, 1) * sinb)


def _inproj_kernel(x_ref, mod_ref, g_ref, w_ref, bff_ref, gq_ref, gkv_ref, wuq_ref, wkn_ref,
                   wkv_ref, cos_ref, sina_ref, sinb_ref, place_ref, qc_ref, kc_ref,
                   fq_ref, fk_ref, fv_ref, sb_ref, gate_ref, mq_ref, mk_ref, mv_ref,
                   carry_ref, *, tm):
    x = x_ref[0]
    sh, sc = mod_ref[0, 0:1, :], mod_ref[0, 1:2, :]
    u = ((_rms(x) * g_ref[...]) * (1.0 + sc) + sh).astype(BF16)

    def seg(a, b):
        return _bdot(u, w_ref[:, a:b])

    @pl.when(pl.program_id(1) == 0)
    def _():
        carry_ref[...] = jnp.zeros_like(carry_ref)

    logf = _log_sigmoid(seg(C_FF, IN_W) + bff_ref[...])
    r = lax.broadcasted_iota(jnp.int32, (tm, tm), 0)
    c = lax.broadcasted_iota(jnp.int32, (tm, tm), 1)
    tri = jnp.where(r >= c, 1.0, 0.0).astype(BF16)
    hi, mid, lo = _split3(logf)
    cum = (_bdot(tri, hi) + _bdot(tri, mid) + _bdot(tri, lo)) + carry_ref[...]
    carry_ref[...] = cum[tm - 1:tm, :]
    p0, p1, p2 = (p.astype(F32) for p in _split3(cum))
    lane = lax.broadcasted_iota(jnp.int32, (1, LANES), 1)
    pieces = jnp.where(lane < N_HEADS, p0,
                       jnp.where(lane < 2 * N_HEADS, pltpu.roll(p1, N_HEADS, 1),
                                 jnp.where(lane < 3 * N_HEADS, pltpu.roll(p2, 2 * N_HEADS, 1),
                                           0.0))).astype(BF16)

    for c in range(2):
        a = 512 * c
        placed = _bdot(pieces, place_ref[:, a:a + 512])
        q = seg(C_FQ + a, C_FQ + a + 512) * QK_SCALE + placed + qc_ref[:, a:a + 512]
        fq_ref[0, :, a:a + 512] = q.astype(BF16)
        placed = _bdot(pieces, place_ref[:, SLOT_W + a:SLOT_W + a + 512])
        k = seg(C_FK + a, C_FK + a + 512) + placed + kc_ref[:, a:a + 512]
        fk_ref[0, :, a:a + 512] = k.astype(BF16)
    fv_ref[0] = seg(C_FV, C_SB).astype(BF16)

    for c in range(3):
        a = 512 * c
        sq = seg(C_SB + a, C_SB + a + 512)
        if c == 0:
            sq = sq * QK_SCALE
        sb_ref[0, :, a:a + 512] = sq.astype(BF16)
    for c in range(6):
        a = 512 * c
        gate_ref[0, :, a:a + 512] = _sigmoid(seg(C_GATE + a, C_GATE + a + 512))

    cos, sina, sinb = cos_ref[0], sina_ref[0], sinb_ref[0]
    cq = (_rms(seg(C_QL, C_KVL)) * gq_ref[...]).astype(BF16)
    ckv = (_rms(seg(C_KVL, C_KR)) * gkv_ref[...]).astype(BF16)
    kr = _rope_slot(seg(C_KR, C_FF), cos, sina, sinb)
    for h in range(N_HEADS):
        a = LANES * h
        q = _rope_slot(_bdot(cq, wuq_ref[:, a:a + LANES]), cos, sina, sinb)
        mq_ref[0, :, a:a + LANES] = (q * MLA_SCALE).astype(BF16)
        mk_ref[0, :, a:a + LANES] = (_bdot(ckv, wkn_ref[:, a:a + LANES]) + kr).astype(BF16)
    mv_ref[0] = _bdot(ckv, wkv_ref[...]).astype(BF16)


def _fox_placement():
    place = np.zeros((LANES, 2 * SLOT_W), np.float32)
    qc = np.zeros((1, SLOT_W), np.float32)
    kc = np.zeros((1, SLOT_W), np.float32)
    for h in range(N_HEADS):
        for p in range(N_PIECES):
            place[N_HEADS * p + h, LANES * h + HEAD_DIM + N_PIECES + p] = 1.0
            place[N_HEADS * p + h, SLOT_W + LANES * h + HEAD_DIM + p] = 1.0
            qc[0, LANES * h + HEAD_DIM + p] = -1.0
            kc[0, LANES * h + HEAD_DIM + N_PIECES + p] = 1.0
    return jnp.asarray(place, BF16), jnp.asarray(qc), jnp.asarray(kc)


def _inproj(x, mod, g_mix, w_in_r, bff, g_q, g_kv, wuq_r, wkn_r, wkv_r, tables, *, tm):
    b, s, d = x.shape
    cos, sina, sinb = tables
    place, qc, kc = _fox_placement()
    row = lambda w: pl.BlockSpec((1, tm, w), lambda i, j: (i, j, 0))
    widths = [(SLOT_W, BF16),
              (SLOT_W, BF16),
              (HEAD_W, BF16),
              (3 * HEAD_W, BF16),
              (3 * D_MODEL, F32),
              (SLOT_W, BF16),
              (SLOT_W, BF16),
              (HEAD_W, BF16)]
    return pl.pallas_call(
        functools.partial(_inproj_kernel, tm=tm),
        grid=(b, s // tm),
        in_specs=[row(d),
                  pl.BlockSpec((1, 6, d), lambda i, j: (i, 0, 0)),
                  _resident((1, d)), _resident((d, IN_W)), _resident((1, LANES)),
                  _resident((1, MLA_Q_RANK)), _resident((1, MLA_KV_RANK)),
                  _resident((MLA_Q_RANK, SLOT_W)), _resident((MLA_KV_RANK, SLOT_W)),
                  _resident((MLA_KV_RANK, HEAD_W)),
                  row(LANES), row(LANES), row(LANES),
                  _resident((LANES, 2 * SLOT_W)), _resident((1, SLOT_W)), _resident((1, SLOT_W))],
        out_specs=[row(w) for w, _ in widths],
        out_shape=[jax.ShapeDtypeStruct((b, s, w), dt) for w, dt in widths],
        scratch_shapes=[pltpu.VMEM((1, LANES), F32)],
        compiler_params=_params(2),
        name="inproj",
    )(x, mod, g_mix, w_in_r, bff, g_q, g_kv, wuq_r, wkn_r, wkv_r, cos, sina, sinb, place, qc, kc)


def _head_masks():
    lane = lax.broadcasted_iota(jnp.int32, (1, LANES), 1)
    return [(lane >= HEAD_DIM * hh) & (lane < HEAD_DIM * (hh + 1)) for hh in range(2)]


def _col_reduce(x, op, final):
    rows = x.shape[0]
    while rows > 8 and rows % 16 == 0:
        rows //= 2
        x = op(x[:rows], x[rows:])
    return final(x, axis=0, keepdims=True)


def _colmax(x):
    return _col_reduce(x, jnp.maximum, jnp.max)


def _colsum(x):
    return _col_reduce(x, jnp.add, jnp.sum)


def _softmax_attn_kernel(q_ref, k_ref, v_ref, o_ref, acc_ref, s0_ref, s1_ref, *, tq, tk,
                         chunk_shift):
    qi = pl.program_id(2)
    in_head = _head_masks()
    qs = [q_ref[0, :, LANES * hh:LANES * (hh + 1)] for hh in range(2)]
    qpos = qi * tq + lax.broadcasted_iota(jnp.int32, (1, tq), 1)
    acc_ref[...] = jnp.zeros_like(acc_ref)

    def produce(j, buf):
        start = pl.multiple_of(j * tk, tk)
        for hh in range(2):
            k = k_ref[0, pl.ds(start, tk), LANES * hh:LANES * (hh + 1)]
            buf[hh] = _dot_nt(k, qs[hh])

    def consume(j, buf, m, masked):
        start = pl.multiple_of(j * tk, tk)
        v = v_ref[0, pl.ds(start, tk), :]
        m_out = []
        for hh in range(2):
            s = buf[hh]
            if masked:
                kpos = start + lax.broadcasted_iota(jnp.int32, (tk, 1), 0)
                s = jnp.where((kpos >> chunk_shift) <= (qpos >> chunk_shift), s, NEG)
            m_new = jnp.maximum(m[hh], _colmax(s))
            p = jnp.exp(s - m_new)
            alpha = jnp.exp(m[hh] - m_new)
            v1 = jnp.where(in_head[hh], v, jnp.ones_like(v))
            acc_ref[hh] = acc_ref[hh] * alpha + _dot_tn(v1, p.astype(BF16))
            m_out.append(m_new)
        return tuple(m_out)

    n_last = (qi * tq + tq - 1) // tk
    n_pairs = n_last // 2

    def body(i, m):
        j = 2 * i
        produce(j + 1, s1_ref)
        m = consume(j, s0_ref, m, False)
        produce(j + 2, s0_ref)
        return consume(j + 1, s1_ref, m, False)

    produce(0, s0_ref)
    m = (jnp.full((1, tq), NEG, F32),) * 2
    m = lax.fori_loop(0, n_pairs, body, m)
    j = 2 * n_pairs

    @pl.when(n_last % 2 == 1)
    def _():
        produce(j + 1, s1_ref)
        consume(j + 1, s1_ref, consume(j, s0_ref, m, False), True)

    @pl.when(n_last % 2 == 0)
    def _():
        consume(j, s0_ref, m, True)

    sub = lax.broadcasted_iota(jnp.int32, (LANES, 1), 0)
    outs = []
    for hh in range(2):
        acc = acc_ref[hh]
        denom = acc[HEAD_DIM * (1 - hh):HEAD_DIM * (1 - hh) + 1, :]
        outs.append(acc / denom)
    o_ref[0] = jnp.where(sub < HEAD_DIM, outs[0], outs[1]).T.astype(o_ref.dtype)


def _softmax_attention(q, k, v, *, tq, tk, chunk_shift, name):
    b, s, _ = q.shape
    return pl.pallas_call(
        functools.partial(_softmax_attn_kernel, tq=tq, tk=tk, chunk_shift=chunk_shift),
        grid=(b, N_HEADS // 2, s // tq),
        in_specs=[pl.BlockSpec((1, tq, 2 * LANES), lambda i, h, j: (i, j, h)),
                  pl.BlockSpec((1, s, 2 * LANES), lambda i, h, j: (i, 0, h)),
                  pl.BlockSpec((1, s, LANES), lambda i, h, j: (i, 0, h))],
        out_specs=pl.BlockSpec((1, tq, LANES), lambda i, h, j: (i, j, h)),
        out_shape=jax.ShapeDtypeStruct((b, s, HEAD_W), BF16),
        scratch_shapes=[pltpu.VMEM((2, LANES, tq), F32), pltpu.VMEM((2, tk, tq), F32),
                        pltpu.VMEM((2, tk, tq), F32)],
        compiler_params=_params(3),
        name=name,
    )(q, k, v)


def _sb_kernel(q_ref, k_ref, v_ref, o_ref, acc_ref, z_ref, r_ref, *, t):
    qi = pl.program_id(2)
    in_head = _head_masks()
    q = q_ref[0]
    qs = [jnp.where(in_head[hh], q, jnp.zeros_like(q)) for hh in range(2)]
    qpos = lax.broadcasted_iota(jnp.int32, (1, t), 1)
    kpos = lax.broadcasted_iota(jnp.int32, (t, 1), 0)
    ss = lax.broadcasted_iota(jnp.int32, (t, 2 * t), 0)
    jj = lax.broadcasted_iota(jnp.int32, (t, 2 * t), 1) & (t - 1)
    from_here = jnp.where(jj >= ss, 1.0, 0.0).astype(BF16)
    acc_ref[...] = jnp.zeros_like(acc_ref)
    r_ref[...] = jnp.zeros_like(r_ref)

    def key_start(blk):
        return pl.multiple_of(jnp.maximum(blk, 0) * t, t)

    def scores(blk_hi, nblk):
        for u in range(nblk):
            k = k_ref[0, pl.ds(key_start(blk_hi - u), t), :]
            for hh in range(2):
                z_ref[u, hh] = _dot_nt(k, qs[hh])

    def group(blk_hi, nblk, diag_first, next_hi):
        chains = []
        for u in range(nblk):
            for hh in range(2):
                z = z_ref[u, hh]
                drop = jnp.maximum(z, 0.0) + jnp.log(1.0 + jnp.exp(-jnp.abs(z)))
                if diag_first and u == 0:
                    ok = kpos < qpos
                    drop = jnp.where(ok, drop, 0.0)
                hi = drop.astype(BF16)
                lo = (drop - hi.astype(F32)).astype(BF16)
                incl = _bdot(from_here, jnp.concatenate([hi, lo], axis=0))
                e = z - incl
                if diag_first and u == 0:
                    e = jnp.where(ok, e, NEG)
                chains.append((u, hh, e, incl[0:1, :]))
        scores(next_hi, 2)
        r = [r_ref[hh] for hh in range(2)]
        upd = None
        for u, hh, e, total in chains:
            w = jnp.exp(e - r[hh])
            v = v_ref[0, pl.ds(key_start(blk_hi - u), t), :]
            vh = jnp.where(in_head[hh], v, jnp.zeros_like(v))
            pv = _dot_tn(vh, w.astype(BF16))
            upd = pv if upd is None else upd + pv
            r[hh] = r[hh] + total
        acc_ref[...] = acc_ref[...] + upd
        for hh in range(2):
            r_ref[hh] = r[hh]

    @pl.when(qi % 2 == 0)
    def _():
        scores(qi, 1)
        group(qi, 1, True, qi - 1)

    @pl.when(qi % 2 == 1)
    def _():
        scores(qi, 2)
        group(qi, 2, True, qi - 2)

    n_rest = qi // 2
    top = 2 * n_rest - 1

    def body(i, carry):
        group(top - 2 * i, 2, False, top - 2 * i - 2)
        return carry

    lax.fori_loop(0, n_rest, body, 0)
    o_ref[0] = acc_ref[...].T.astype(o_ref.dtype)


def _sb_attention(qkv, *, t):
    b, s, _ = qkv.shape
    return pl.pallas_call(
        functools.partial(_sb_kernel, t=t),
        grid=(b, N_HEADS // 2, s // t),
        in_specs=[pl.BlockSpec((1, t, LANES), lambda i, h, j: (i, j, h)),
                  pl.BlockSpec((1, s, LANES), lambda i, h, j: (i, 0, 4 + h)),
                  pl.BlockSpec((1, s, LANES), lambda i, h, j: (i, 0, 8 + h))],
        out_specs=pl.BlockSpec((1, t, LANES), lambda i, h, j: (i, j, h)),
        out_shape=jax.ShapeDtypeStruct((b, s, HEAD_W), BF16),
        scratch_shapes=[pltpu.VMEM((LANES, t), F32), pltpu.VMEM((2, 2, t, t), F32),
                        pltpu.VMEM((2, 1, t), F32)],
        compiler_params=_params(3),
        name="sb_attention",
    )(qkv, qkv, qkv)


def _merge_kernel(x_ref, yf_ref, ym_ref, ys_ref, g_ref, mod_ref, wf_ref, wm_ref, ws_ref,
                  wo_ref, o_ref):
    d = D_MODEL
    merged = (g_ref[0, :, 0:d] * _bdot(yf_ref[0], wf_ref[...])
              + g_ref[0, :, d:2 * d] * _bdot(ym_ref[0], wm_ref[...])
              + g_ref[0, :, 2 * d:3 * d] * _bdot(ys_ref[0], ws_ref[...]))
    o_ref[0] = x_ref[0] + mod_ref[0, 2:3, :] * _bdot(merged.astype(BF16), wo_ref[...])


def _merge(x, y_fox, y_mla, y_sb, gates, mod, wf, wm, ws, wo, *, tm):
    b, s, d = x.shape
    row = lambda w: pl.BlockSpec((1, tm, w), lambda i, j: (i, j, 0))
    return pl.pallas_call(
        _merge_kernel,
        grid=(b, s // tm),
        in_specs=[row(d), row(HEAD_W), row(HEAD_W), row(HEAD_W), row(3 * d),
                  pl.BlockSpec((1, 6, d), lambda i, j: (i, 0, 0)),
                  _resident((HEAD_W, d)), _resident((HEAD_W, d)), _resident((HEAD_W, d)),
                  _resident((d, d))],
        out_specs=row(d),
        out_shape=jax.ShapeDtypeStruct((b, s, d), F32),
        compiler_params=_params(2),
        name="merge_out",
    )(x, y_fox, y_mla, y_sb, gates, mod, wf, wm, ws, wo)


def _ffn_kernel(x_ref, mod_ref, g_ref, wg_ref, wu_ref, wd_ref, gfin_ref, o_ref, *, final, fc):
    x = x_ref[0]
    sh, sc, gt = mod_ref[0, 3:4, :], mod_ref[0, 4:5, :], mod_ref[0, 5:6, :]
    u = ((_rms(x) * g_ref[...]) * (1.0 + sc) + sh).astype(BF16)
    acc = None
    for a in range(0, D_FF, fc):
        hg = _bdot(u, wg_ref[:, a:a + fc])
        hu = _bdot(u, wu_ref[:, a:a + fc])
        h = ((hg * _sigmoid(hg)) * hu).astype(BF16)
        part = _bdot(h, wd_ref[a:a + fc, :])
        acc = part if acc is None else acc + part
    out = x + gt * acc
    if final:
        out = _rms(out) * gfin_ref[...]
    o_ref[0] = out


def _ffn(x, mod, g_ffn, wg, wu, wd, g_final, *, tm, final):
    b, s, d = x.shape
    row = lambda w: pl.BlockSpec((1, tm, w), lambda i, j: (i, j, 0))
    return pl.pallas_call(
        functools.partial(_ffn_kernel, final=final, fc=D_FF // 2),
        grid=(b, s // tm),
        in_specs=[row(d), pl.BlockSpec((1, 6, d), lambda i, j: (i, 0, 0)), _resident((1, d)),
                  _resident((d, D_FF)), _resident((d, D_FF)), _resident((D_FF, d)),
                  _resident((1, d))],
        out_specs=row(d),
        out_shape=jax.ShapeDtypeStruct((b, s, d), F32),
        compiler_params=_params(2),
        name="ffn",
    )(x, mod, g_ffn, wg, wu, wd, g_final)


def _slot_cols(w, width_per_head, used):
    k = w.shape[0]
    w = w.reshape(k, N_HEADS, width_per_head)[:, :, :used]
    return jnp.pad(w, ((0, 0), (0, 0), (0, LANES - used))).reshape(k, SLOT_W)


def _layout_w_in(w):
    d = w.shape[0]
    fox, ff, ql, kvl, kr, sb, gate = jnp.split(
        w, [1536, 1544, 1928, 2184, 2216, 3752], axis=-1)
    z = lambda n: jnp.zeros((d, n), w.dtype)
    fq, fk, fv = jnp.split(fox, 3, axis=-1)
    kr_slot = jnp.concatenate([z(MLA_NOPE), kr, z(LANES - MLA_NOPE - MLA_ROPE)], axis=-1)
    ff_slot = jnp.concatenate([ff, z(LANES - N_HEADS)], axis=-1)
    cols = [_slot_cols(fq, HEAD_DIM, HEAD_DIM), _slot_cols(fk, HEAD_DIM, HEAD_DIM), fv, sb, gate,
            ql, kvl, kr_slot, ff_slot]
    return jnp.concatenate(cols, axis=-1).astype(BF16)


def kernel(x, c, positions, g_mix, w_ada, b_ada, w_in, b_fox_f, g_mla_q, w_mla_uq, g_mla_kv,
           w_mla_ukv, w_o_fox, w_o_mla, w_o_sb, w_out, g_ffn, w_ffn_gate, w_ffn_up, w_ffn_down,
           g_final):
    b, s, d = x.shape
    depth = w_in.shape[0]
    tm = min(s, 512)
    tq = min(s, 256)
    tk = min(s, 512)

    mod = _ada(c, w_ada, b_ada).reshape(depth, b, 6, d)
    tables = _rope_tables(positions)
    for l in range(depth):
        w_in_r = _layout_w_in(w_in[l])
        wuq_r = _slot_cols(w_mla_uq[l], MLA_NOPE + MLA_ROPE, MLA_NOPE + MLA_ROPE).astype(BF16)
        wkn_r = _slot_cols(w_mla_ukv[l], 2 * HEAD_DIM, MLA_NOPE).astype(BF16)
        wkv_r = w_mla_ukv[l].reshape(MLA_KV_RANK, N_HEADS, 2 * HEAD_DIM)[:, :, MLA_NOPE:]
        wkv_r = wkv_r.reshape(MLA_KV_RANK, HEAD_W).astype(BF16)
        bff = jnp.pad(b_fox_f[l], (0, LANES - N_HEADS)).reshape(1, LANES)

        fox_q, fox_k, fox_v, sb_qkv, gates, mq, mk, mv = _inproj(
            x, mod[l], g_mix[l].reshape(1, d), w_in_r, bff, g_mla_q[l].reshape(1, -1),
            g_mla_kv[l].reshape(1, -1), wuq_r, wkn_r, wkv_r, tables, tm=tm)
        y_fox = _softmax_attention(fox_q, fox_k, fox_v, tq=tq, tk=tk, chunk_shift=0,
                                   name="fox_attention")
        y_mla = _softmax_attention(mq, mk, mv, tq=tq, tk=tk, chunk_shift=CHUNK_SHIFT,
                                   name="mla_attention")
        y_sb = _sb_attention(sb_qkv, t=tq)
        x = _merge(x, y_fox, y_mla, y_sb, gates, mod[l], w_o_fox[l].astype(BF16),
                   w_o_mla[l].astype(BF16), w_o_sb[l].astype(BF16), w_out[l].astype(BF16), tm=tm)
        x = _ffn(x, mod[l], g_ffn[l].reshape(1, d), w_ffn_gate[l].astype(BF16),
                 w_ffn_up[l].astype(BF16), w_ffn_down[l].astype(BF16), g_final.reshape(1, d),
                 tm=tm, final=(l == depth - 1))
    return x
```

```python
import functools

import numpy as np
import jax
import jax.numpy as jnp
from jax import lax
from jax.experimental import pallas as pl
from jax.experimental.pallas import tpu as pltpu

F32 = jnp.float32
BF16 = jnp.bfloat16

D_MODEL = 1024
EPS = 1e-6
HEAD_DIM = 64
N_HEADS = 8
HEAD_W = N_HEADS * HEAD_DIM
CHUNK_SHIFT = 6
MLA_Q_RANK = 384
MLA_KV_RANK = 256
MLA_NOPE = 64
MLA_ROPE = 32
ROPE_HALF = MLA_ROPE // 2
ROPE_BASE = 10000.0
MLA_SCALE = (MLA_NOPE + MLA_ROPE) ** -0.5
QK_SCALE = HEAD_DIM ** -0.5
D_FF = 2816
LANES = 128
SLOT_W = N_HEADS * LANES
N_PIECES = 3

C_FQ = 0
C_FK = 1024
C_FV = 2048
C_SB = 2560
C_GATE = 4096
C_QL = 7168
C_KVL = 7552
C_KR = 7808
C_FF = 7936
IN_W = 8064

ROW_CHUNK = 64
NEG = -1e30
VMEM_LIMIT = 56 * 1024 * 1024


def _params(n_grid):
    return pltpu.CompilerParams(dimension_semantics=("arbitrary",) * n_grid,
                                vmem_limit_bytes=VMEM_LIMIT)


def _resident(shape):
    nd = len(shape)
    return pl.BlockSpec(shape, lambda *_: (0,) * nd, pipeline_mode=pl.Buffered(1))


def _bdot(a, b):
    return jnp.dot(a, b, preferred_element_type=F32)


def _dot_nt(a, b):
    return lax.dot_general(a, b, (((1,), (1,)), ((), ())), preferred_element_type=F32)


def _dot_tn(a, b):
    return lax.dot_general(a, b, (((0,), (0,)), ((), ())), preferred_element_type=F32)


def _split3(v):
    hi = v.astype(BF16)
    r = v - hi.astype(F32)
    mid = r.astype(BF16)
    lo = (r - mid.astype(F32)).astype(BF16)
    return hi, mid, lo


def _log_sigmoid(z):
    return jnp.minimum(z, 0.0) - jnp.log1p(jnp.exp(-jnp.abs(z)))


def _sigmoid(z):
    return 1.0 / (1.0 + jnp.exp(-z))


def _rms(x):
    return x * lax.rsqrt(jnp.mean(x * x, axis=-1, keepdims=True) + EPS)


def _ada_kernel(c_ref, w_ref, b_ref, o_ref):
    c = c_ref[...]
    cond = c * _sigmoid(c)
    chi, cmid, clo = _split3(cond)
    whi, wmid, wlo = _split3(w_ref[0])
    acc = _bdot(chi, whi) + (_bdot(chi, wmid) + _bdot(cmid, whi))
    acc = acc + (_bdot(chi, wlo) + _bdot(clo, whi) + _bdot(cmid, wmid))
    o_ref[0] = acc + b_ref[0]


def _ada(c, w_ada, b_ada):
    depth, d, n = w_ada.shape
    b = c.shape[0]
    tn = 1024
    return pl.pallas_call(
        _ada_kernel,
        grid=(depth, n // tn),
        in_specs=[pl.BlockSpec((b, d), lambda l, j: (0, 0)),
                  pl.BlockSpec((1, d, tn), lambda l, j: (l, 0, j)),
                  pl.BlockSpec((1, 1, tn), lambda l, j: (l, 0, j))],
        out_specs=pl.BlockSpec((1, b, tn), lambda l, j: (l, 0, j)),
        out_shape=jax.ShapeDtypeStruct((depth, b, n), F32),
        compiler_params=_params(2),
        name="adaln",
    )(c, w_ada, b_ada.reshape(depth, 1, n))


def _rope_kernel(pos_ref, inv_ref, cos_ref, sina_ref, sinb_ref):
    ang = pos_ref[0].astype(F32) * inv_ref[...]
    lane = lax.broadcasted_iota(jnp.int32, (1, LANES), 1)
    sin = jnp.sin(ang)
    cos_ref[0] = jnp.cos(ang)
    lo, mid, hi = MLA_NOPE, MLA_NOPE + ROPE_HALF, MLA_NOPE + MLA_ROPE
    sina_ref[0] = jnp.where((lane >= lo) & (lane < mid), -sin, 0.0)
    sinb_ref[0] = jnp.where((lane >= mid) & (lane < hi), sin, 0.0)


def _rope_tables(positions):
    b, s = positions.shape
    ts = min(s, 512)
    inv = ROPE_BASE ** (-jnp.arange(ROPE_HALF, dtype=F32) / ROPE_HALF)
    inv_slot = jnp.concatenate([jnp.zeros((MLA_NOPE,), F32), inv, inv,
                                jnp.zeros((LANES - MLA_NOPE - MLA_ROPE,), F32)]).reshape(1, LANES)
    spec = pl.BlockSpec((1, ts, LANES), lambda i, j: (i, j, 0))
    return pl.pallas_call(
        _rope_kernel,
        grid=(b, s // ts),
        in_specs=[pl.BlockSpec((1, ts, 1), lambda i, j: (i, j, 0)),
                  pl.BlockSpec((1, LANES), lambda i, j: (0, 0))],
        out_specs=[spec, spec, spec],
        out_shape=[jax.ShapeDtypeStruct((b, s, LANES), F32)] * 3,
        compiler_params=_params(2),
        name="rope_tables",
    )(positions.reshape(b, s, 1), inv_slot)


def _rope_slot(x, cos, sina, sinb):
    return (x * cos + pltpu.roll(x, LANES - ROPE_HALF, 1) * sina
            + pltpu.roll(x, ROPE_HALF, 1) * sinb)


def _inproj_kernel(x_ref, mod_ref, g_ref, w_ref, bff_ref, gq_ref, gkv_ref, wuq_ref, wkn_ref,
                   wkv_ref, cos_ref, sina_ref, sinb_ref, place_ref, qc_ref, kc_ref,
                   fq_ref, fk_ref, fv_ref, sb_ref, gate_ref, mq_ref, mk_ref, mv_ref,
                   carry_ref, *, tm):
    x = x_ref[0]
    sh, sc = mod_ref[0, 0:1, :], mod_ref[0, 1:2, :]
    u = ((_rms(x) * g_ref[...]) * (1.0 + sc) + sh).astype(BF16)

    def seg(a, b):
        return _bdot(u, w_ref[:, a:b])

    @pl.when(pl.program_id(1) == 0)
    def _():
        carry_ref[...] = jnp.zeros_like(carry_ref)

    logf = _log_sigmoid(seg(C_FF, IN_W) + bff_ref[...])
    r = lax.broadcasted_iota(jnp.int32, (tm, tm), 0)
    c = lax.broadcasted_iota(jnp.int32, (tm, tm), 1)
    tri = jnp.where(r >= c, 1.0, 0.0).astype(BF16)
    hi, mid, lo = _split3(logf)
    cum = (_bdot(tri, hi) + _bdot(tri, mid) + _bdot(tri, lo)) + carry_ref[...]
    carry_ref[...] = cum[tm - 1:tm, :]
    p0, p1, p2 = (p.astype(F32) for p in _split3(cum))
    lane = lax.broadcasted_iota(jnp.int32, (1, LANES), 1)
    pieces = jnp.where(lane < N_HEADS, p0,
                       jnp.where(lane < 2 * N_HEADS, pltpu.roll(p1, N_HEADS, 1),
                                 jnp.where(lane < 3 * N_HEADS, pltpu.roll(p2, 2 * N_HEADS, 1),
                                           0.0))).astype(BF16)

    for c in range(2):
        a = 512 * c
        placed = _bdot(pieces, place_ref[:, a:a + 512])
        q = seg(C_FQ + a, C_FQ + a + 512) * QK_SCALE + placed + qc_ref[:, a:a + 512]
        fq_ref[0, :, a:a + 512] = q.astype(BF16)
        placed = _bdot(pieces, place_ref[:, SLOT_W + a:SLOT_W + a + 512])
        k = seg(C_FK + a, C_FK + a + 512) + placed + kc_ref[:, a:a + 512]
        fk_ref[0, :, a:a + 512] = k.astype(BF16)
    fv_ref[0] = seg(C_FV, C_SB).astype(BF16)

    for c in range(3):
        a = 512 * c
        sq = seg(C_SB + a, C_SB + a + 512)
        if c == 0:
            sq = sq * QK_SCALE
        sb_ref[0, :, a:a + 512] = sq.astype(BF16)
    for c in range(6):
        a = 512 * c
        gate_ref[0, :, a:a + 512] = _sigmoid(seg(C_GATE + a, C_GATE + a + 512))

    cos, sina, sinb = cos_ref[0], sina_ref[0], sinb_ref[0]
    cq = (_rms(seg(C_QL, C_KVL)) * gq_ref[...]).astype(BF16)
    ckv = (_rms(seg(C_KVL, C_KR)) * gkv_ref[...]).astype(BF16)
    kr = _rope_slot(seg(C_KR, C_FF), cos, sina, sinb)
    for h in range(N_HEADS):
        a = LANES * h
        q = _rope_slot(_bdot(cq, wuq_ref[:, a:a + LANES]), cos, sina, sinb)
        mq_ref[0, :, a:a + LANES] = (q * MLA_SCALE).astype(BF16)
        mk_ref[0, :, a:a + LANES] = (_bdot(ckv, wkn_ref[:, a:a + LANES]) + kr).astype(BF16)
    mv_ref[0] = _bdot(ckv, wkv_ref[...]).astype(BF16)


def _fox_placement():
    place = np.zeros((LANES, 2 * SLOT_W), np.float32)
    qc = np.zeros((1, SLOT_W), np.float32)
    kc = np.zeros((1, SLOT_W), np.float32)
    for h in range(N_HEADS):
        for p in range(N_PIECES):
            place[N_HEADS * p + h, LANES * h + HEAD_DIM + N_PIECES + p] = 1.0
            place[N_HEADS * p + h, SLOT_W + LANES * h + HEAD_DIM + p] = 1.0
            qc[0, LANES * h + HEAD_DIM + p] = -1.0
            kc[0, LANES * h + HEAD_DIM + N_PIECES + p] = 1.0
    return jnp.asarray(place, BF16), jnp.asarray(qc), jnp.asarray(kc)


def _inproj(x, mod, g_mix, w_in_r, bff, g_q, g_kv, wuq_r, wkn_r, wkv_r, tables, *, tm):
    b, s, d = x.shape
    cos, sina, sinb = tables
    place, qc, kc = _fox_placement()
    row = lambda w: pl.BlockSpec((1, tm, w), lambda i, j: (i, j, 0))
    widths = [(SLOT_W, BF16),
              (SLOT_W, BF16),
              (HEAD_W, BF16),
              (3 * HEAD_W, BF16),
              (3 * D_MODEL, F32),
              (SLOT_W, BF16),
              (SLOT_W, BF16),
              (HEAD_W, BF16)]
    return pl.pallas_call(
        functools.partial(_inproj_kernel, tm=tm),
        grid=(b, s // tm),
        in_specs=[row(d),
                  pl.BlockSpec((1, 6, d), lambda i, j: (i, 0, 0)),
                  _resident((1, d)), _resident((d, IN_W)), _resident((1, LANES)),
                  _resident((1, MLA_Q_RANK)), _resident((1, MLA_KV_RANK)),
                  _resident((MLA_Q_RANK, SLOT_W)), _resident((MLA_KV_RANK, SLOT_W)),
                  _resident((MLA_KV_RANK, HEAD_W)),
                  row(LANES), row(LANES), row(LANES),
                  _resident((LANES, 2 * SLOT_W)), _resident((1, SLOT_W)), _resident((1, SLOT_W))],
        out_specs=[row(w) for w, _ in widths],
        out_shape=[jax.ShapeDtypeStruct((b, s, w), dt) for w, dt in widths],
        scratch_shapes=[pltpu.VMEM((1, LANES), F32)],
        compiler_params=_params(2),
        name="inproj",
    )(x, mod, g_mix, w_in_r, bff, g_q, g_kv, wuq_r, wkn_r, wkv_r, cos, sina, sinb, place, qc, kc)


def _head_masks():
    lane = lax.broadcasted_iota(jnp.int32, (1, LANES), 1)
    return [(lane >= HEAD_DIM * hh) & (lane < HEAD_DIM * (hh + 1)) for hh in range(2)]


def _col_reduce(x, op, final, chunk=64):
    rows = x.shape[0]
    if rows > chunk and rows % chunk == 0:
        acc = x[:chunk]
        for a in range(chunk, rows, chunk):
            acc = op(acc, x[a:a + chunk])
        x, rows = acc, chunk
    while rows > 8 and rows % 16 == 0:
        rows //= 2
        x = op(x[:rows], x[rows:])
    return final(x, axis=0, keepdims=True)


def _colmax(x):
    return _col_reduce(x, jnp.maximum, jnp.max)


def _colsum(x):
    return _col_reduce(x, jnp.add, jnp.sum)


def _pipeline(n, stages):
    depth = len(stages)

    def static_step(t):
        for s, stage in enumerate(stages):
            if 0 <= t - s < n:
                stage(t - s, (t - s) % 2)

    first_full = depth - 1
    n_full = max(n - depth + 1, 0)
    for t in range(min(first_full, n + depth - 1)):
        static_step(t)
    if n_full >= 2:
        def body(pair, carry):
            t = first_full + 2 * pair
            for u in range(2):
                for s, stage in enumerate(stages):
                    stage(t + u - s, (first_full + u - s) % 2)
            return carry

        lax.fori_loop(0, n_full // 2, body, 0)
    for t in range(first_full + 2 * (n_full // 2), n + depth - 1):
        static_step(t)


def _block_tables(n_tiles, t, descending):
    diag = [(i * t, i * t) for i in range(n_tiles)]
    below = []
    for i in range(1, n_tiles):
        js = range(i - 1, -1, -1) if descending else range(i)
        below += [(i * t, j * t) for j in js]
    return jnp.asarray(np.array(diag + below, np.int32).T), len(diag), len(below)


def _softmax_flat_kernel(tab_ref, q_ref, k_ref, v_ref, o_ref, acc_ref, m_ref, al_ref,
                         s0_ref, s1_ref, p0_ref, p1_ref, *, t, n_diag, n_below, chunk_shift):
    in_head = _head_masks()
    s_bufs, p_bufs = (s0_ref, s1_ref), (p0_ref, p1_ref)
    kio = lax.broadcasted_iota(jnp.int32, (t, 1), 0)
    qio = lax.broadcasted_iota(jnp.int32, (1, t), 1)
    acc_ref[...] = jnp.zeros_like(acc_ref)
    m_ref[...] = jnp.full_like(m_ref, NEG)

    def stages(base, diag):
        def item(i):
            return (pl.multiple_of(tab_ref[0, base + i], t), pl.multiple_of(tab_ref[1, base + i], t))

        def scores(i, par):
            q0, k0 = item(i)
            for hh in range(2):
                lanes = slice(LANES * hh, LANES * (hh + 1))
                s_bufs[par][hh] = _dot_nt(k_ref[0, pl.ds(k0, t), lanes],
                                          q_ref[0, pl.ds(q0, t), lanes])

        def numerators(i, par):
            q0, _ = item(i)
            for hh in range(2):
                def rows(a):
                    s = s_bufs[par][hh, a:a + ROW_CHUNK, :]
                    if diag:
                        ok = ((kio[a:a + ROW_CHUNK] >> chunk_shift) <= (qio >> chunk_shift))
                        s = jnp.where(ok, s, NEG)
                    return s

                top = rows(0)
                for a in range(ROW_CHUNK, t, ROW_CHUNK):
                    top = jnp.maximum(top, rows(a))
                m_old = m_ref[hh, :, pl.ds(q0, t)]
                m_new = jnp.maximum(m_old, _colmax(top))
                for a in range(0, t, ROW_CHUNK):
                    p_bufs[par][hh, a:a + ROW_CHUNK, :] = jnp.exp(rows(a) - m_new).astype(BF16)
                al_ref[par, hh] = jnp.exp(m_old - m_new)
                m_ref[hh, :, pl.ds(q0, t)] = m_new

        def values(i, par):
            q0, k0 = item(i)
            v = v_ref[0, pl.ds(k0, t), :]
            for hh in range(2):
                v1 = jnp.where(in_head[hh], v, jnp.ones_like(v))
                acc_ref[hh, :, pl.ds(q0, t)] = (acc_ref[hh, :, pl.ds(q0, t)] * al_ref[par, hh]
                                                + _dot_tn(v1, p_bufs[par][hh]))

        return [scores, numerators, values]

    _pipeline(n_diag, stages(0, True))
    _pipeline(n_below, stages(n_diag, False))

    sub = lax.broadcasted_iota(jnp.int32, (LANES, 1), 0)

    def finish(tile, carry):
        q0 = pl.multiple_of(tile * t, t)
        outs = []
        for hh in range(2):
            acc = acc_ref[hh, :, pl.ds(q0, t)]
            denom = acc[HEAD_DIM * (1 - hh):HEAD_DIM * (1 - hh) + 1, :]
            outs.append(acc / denom)
        o_ref[0, pl.ds(q0, t), :] = jnp.where(sub < HEAD_DIM, outs[0], outs[1]).T.astype(o_ref.dtype)
        return carry

    lax.fori_loop(0, n_diag, finish, 0)


def _softmax_attention_flat(q, k, v, *, t, chunk_shift, name):
    b, s, _ = q.shape
    table, n_diag, n_below = _block_tables(s // t, t, descending=False)
    grid_spec = pltpu.PrefetchScalarGridSpec(
        num_scalar_prefetch=1,
        grid=(b, N_HEADS // 2),
        in_specs=[pl.BlockSpec((1, s, 2 * LANES), lambda i, h, tab: (i, 0, h)),
                  pl.BlockSpec((1, s, 2 * LANES), lambda i, h, tab: (i, 0, h)),
                  pl.BlockSpec((1, s, LANES), lambda i, h, tab: (i, 0, h))],
        out_specs=pl.BlockSpec((1, s, LANES), lambda i, h, tab: (i, 0, h)),
        scratch_shapes=[pltpu.VMEM((2, LANES, s), F32),
                        pltpu.VMEM((2, 1, s), F32),
                        pltpu.VMEM((2, 2, 1, t), F32),
                        pltpu.VMEM((2, t, t), F32), pltpu.VMEM((2, t, t), F32),
                        pltpu.VMEM((2, t, t), BF16), pltpu.VMEM((2, t, t), BF16)])
    return pl.pallas_call(
        functools.partial(_softmax_flat_kernel, t=t, n_diag=n_diag, n_below=n_below,
                          chunk_shift=chunk_shift),
        grid_spec=grid_spec,
        out_shape=jax.ShapeDtypeStruct((b, s, HEAD_W), BF16),
        compiler_params=_params(2),
        name=name,
    )(table, q, k, v)


def _sb_flat_kernel(tab_ref, q_ref, k_ref, v_ref, o_ref, acc_ref, r_ref, tot_ref,
                    z0_ref, z1_ref, e0_ref, e1_ref, w0_ref, w1_ref, *, t, n_diag, n_below):
    in_head = _head_masks()
    z_bufs, e_bufs, w_bufs = (z0_ref, z1_ref), (e0_ref, e1_ref), (w0_ref, w1_ref)
    kio = lax.broadcasted_iota(jnp.int32, (t, 1), 0)
    qio = lax.broadcasted_iota(jnp.int32, (1, t), 1)
    ss = lax.broadcasted_iota(jnp.int32, (t, 2 * t), 0)
    jj = lax.broadcasted_iota(jnp.int32, (t, 2 * t), 1) & (t - 1)
    from_here = jnp.where(jj >= ss, 1.0, 0.0).astype(BF16)
    acc_ref[...] = jnp.zeros_like(acc_ref)
    r_ref[...] = jnp.zeros_like(r_ref)

    def stages(base, diag):
        def item(i):
            return (pl.multiple_of(tab_ref[0, base + i], t), pl.multiple_of(tab_ref[1, base + i], t))

        def scores(i, par):
            q0, k0 = item(i)
            k = k_ref[0, pl.ds(k0, t), :]
            q = q_ref[0, pl.ds(q0, t), :]
            for hh in range(2):
                qh = jnp.where(in_head[hh], q, jnp.zeros_like(q))
                z_bufs[par][hh] = _dot_nt(k, qh)

        def drops(i, par):
            for hh in range(2):
                z = z_bufs[par][hh]
                drop = jnp.maximum(z, 0.0) + jnp.log(1.0 + jnp.exp(-jnp.abs(z)))
                if diag:
                    drop = jnp.where(kio < qio, drop, 0.0)
                hi = drop.astype(BF16)
                lo = (drop - hi.astype(F32)).astype(BF16)
                incl = _bdot(from_here, jnp.concatenate([hi, lo], axis=0))
                e = z - incl
                if diag:
                    e = jnp.where(kio < qio, e, NEG)
                e_bufs[par][hh] = e
                tot_ref[par, hh] = incl[0:1, :]

        def weights(i, par):
            q0, _ = item(i)
            for hh in range(2):
                r_old = r_ref[hh, :, pl.ds(q0, t)]
                w_bufs[par][hh] = jnp.exp(e_bufs[par][hh] - r_old).astype(BF16)
                r_ref[hh, :, pl.ds(q0, t)] = r_old + tot_ref[par, hh]

        def values(i, par):
            q0, k0 = item(i)
            v = v_ref[0, pl.ds(k0, t), :]
            upd = None
            for hh in range(2):
                vh = jnp.where(in_head[hh], v, jnp.zeros_like(v))
                pv = _dot_tn(vh, w_bufs[par][hh])
                upd = pv if upd is None else upd + pv
            acc_ref[:, pl.ds(q0, t)] = acc_ref[:, pl.ds(q0, t)] + upd

        return [scores, drops, weights, values]

    _pipeline(n_diag, stages(0, True))
    _pipeline(n_below, stages(n_diag, False))

    def finish(tile, carry):
        q0 = pl.multiple_of(tile * t, t)
        o_ref[0, pl.ds(q0, t), :] = acc_ref[:, pl.ds(q0, t)].T.astype(o_ref.dtype)
        return carry

    lax.fori_loop(0, n_diag, finish, 0)


def _sb_attention_flat(qkv, *, t):
    b, s, _ = qkv.shape
    table, n_diag, n_below = _block_tables(s // t, t, descending=True)
    grid_spec = pltpu.PrefetchScalarGridSpec(
        num_scalar_prefetch=1,
        grid=(b, N_HEADS // 2),
        in_specs=[pl.BlockSpec((1, s, LANES), lambda i, h, tab: (i, 0, h)),
                  pl.BlockSpec((1, s, LANES), lambda i, h, tab: (i, 0, 4 + h)),
                  pl.BlockSpec((1, s, LANES), lambda i, h, tab: (i, 0, 8 + h))],
        out_specs=pl.BlockSpec((1, s, LANES), lambda i, h, tab: (i, 0, h)),
        scratch_shapes=[pltpu.VMEM((LANES, s), F32),
                        pltpu.VMEM((2, 1, s), F32),
                        pltpu.VMEM((2, 2, 1, t), F32),
                        pltpu.VMEM((2, t, t), F32), pltpu.VMEM((2, t, t), F32),
                        pltpu.VMEM((2, t, t), F32), pltpu.VMEM((2, t, t), F32),
                        pltpu.VMEM((2, t, t), BF16), pltpu.VMEM((2, t, t), BF16)])
    return pl.pallas_call(
        functools.partial(_sb_flat_kernel, t=t, n_diag=n_diag, n_below=n_below),
        grid_spec=grid_spec,
        out_shape=jax.ShapeDtypeStruct((b, s, HEAD_W), BF16),
        compiler_params=_params(2),
        name="sb_attention",
    )(table, qkv, qkv, qkv)


def _softmax_attn_kernel(q_ref, k_ref, v_ref, o_ref, acc_ref, s0_ref, s1_ref, *, tq, tk,
                         chunk_shift):
    qi = pl.program_id(2)
    in_head = _head_masks()
    qs = [q_ref[0, :, LANES * hh:LANES * (hh + 1)] for hh in range(2)]
    qpos = qi * tq + lax.broadcasted_iota(jnp.int32, (1, tq), 1)
    acc_ref[...] = jnp.zeros_like(acc_ref)

    def produce(j, buf):
        start = pl.multiple_of(j * tk, tk)
        for hh in range(2):
            k = k_ref[0, pl.ds(start, tk), LANES * hh:LANES * (hh + 1)]
            buf[hh] = _dot_nt(k, qs[hh])

    def consume(j, buf, m, masked):
        start = pl.multiple_of(j * tk, tk)
        v = v_ref[0, pl.ds(start, tk), :]
        m_out = []
        for hh in range(2):
            s = buf[hh]
            if masked:
                kpos = start + lax.broadcasted_iota(jnp.int32, (tk, 1), 0)
                s = jnp.where((kpos >> chunk_shift) <= (qpos >> chunk_shift), s, NEG)
            m_new = jnp.maximum(m[hh], _colmax(s))
            p = jnp.exp(s - m_new)
            alpha = jnp.exp(m[hh] - m_new)
            v1 = jnp.where(in_head[hh], v, jnp.ones_like(v))
            acc_ref[hh] = acc_ref[hh] * alpha + _dot_tn(v1, p.astype(BF16))
            m_out.append(m_new)
        return tuple(m_out)

    n_last = (qi * tq + tq - 1) // tk
    n_pairs = n_last // 2

    def body(i, m):
        j = 2 * i
        produce(j + 1, s1_ref)
        m = consume(j, s0_ref, m, False)
        produce(j + 2, s0_ref)
        return consume(j + 1, s1_ref, m, False)

    produce(0, s0_ref)
    m = (jnp.full((1, tq), NEG, F32),) * 2
    m = lax.fori_loop(0, n_pairs, body, m)
    j = 2 * n_pairs

    @pl.when(n_last % 2 == 1)
    def _():
        produce(j + 1, s1_ref)
        consume(j + 1, s1_ref, consume(j, s0_ref, m, False), True)

    @pl.when(n_last % 2 == 0)
    def _():
        consume(j, s0_ref, m, True)

    sub = lax.broadcasted_iota(jnp.int32, (LANES, 1), 0)
    outs = []
    for hh in range(2):
        acc = acc_ref[hh]
        denom = acc[HEAD_DIM * (1 - hh):HEAD_DIM * (1 - hh) + 1, :]
        outs.append(acc / denom)
    o_ref[0] = jnp.where(sub < HEAD_DIM, outs[0], outs[1]).T.astype(o_ref.dtype)


def _softmax_attention(q, k, v, *, tq, tk, chunk_shift, name):
    b, s, _ = q.shape
    return pl.pallas_call(
        functools.partial(_softmax_attn_kernel, tq=tq, tk=tk, chunk_shift=chunk_shift),
        grid=(b, N_HEADS // 2, s // tq),
        in_specs=[pl.BlockSpec((1, tq, 2 * LANES), lambda i, h, j: (i, j, h)),
                  pl.BlockSpec((1, s, 2 * LANES), lambda i, h, j: (i, 0, h)),
                  pl.BlockSpec((1, s, LANES), lambda i, h, j: (i, 0, h))],
        out_specs=pl.BlockSpec((1, tq, LANES), lambda i, h, j: (i, j, h)),
        out_shape=jax.ShapeDtypeStruct((b, s, HEAD_W), BF16),
        scratch_shapes=[pltpu.VMEM((2, LANES, tq), F32), pltpu.VMEM((2, tk, tq), F32),
                        pltpu.VMEM((2, tk, tq), F32)],
        compiler_params=_params(3),
        name=name,
    )(q, k, v)


def _sb_kernel(q_ref, k_ref, v_ref, o_ref, acc_ref, z_ref, r_ref, *, t):
    qi = pl.program_id(2)
    in_head = _head_masks()
    q = q_ref[0]
    qs = [jnp.where(in_head[hh], q, jnp.zeros_like(q)) for hh in range(2)]
    qpos = lax.broadcasted_iota(jnp.int32, (1, t), 1)
    kpos = lax.broadcasted_iota(jnp.int32, (t, 1), 0)
    ss = lax.broadcasted_iota(jnp.int32, (t, 2 * t), 0)
    jj = lax.broadcasted_iota(jnp.int32, (t, 2 * t), 1) & (t - 1)
    from_here = jnp.where(jj >= ss, 1.0, 0.0).astype(BF16)
    acc_ref[...] = jnp.zeros_like(acc_ref)
    r_ref[...] = jnp.zeros_like(r_ref)

    def key_start(blk):
        return pl.multiple_of(jnp.maximum(blk, 0) * t, t)

    def scores(blk_hi, nblk):
        for u in range(nblk):
            k = k_ref[0, pl.ds(key_start(blk_hi - u), t), :]
            for hh in range(2):
                z_ref[u, hh] = _dot_nt(k, qs[hh])

    def group(blk_hi, nblk, diag_first, next_hi):
        chains = []
        for u in range(nblk):
            for hh in range(2):
                z = z_ref[u, hh]
                drop = jnp.maximum(z, 0.0) + jnp.log(1.0 + jnp.exp(-jnp.abs(z)))
                if diag_first and u == 0:
                    ok = kpos < qpos
                    drop = jnp.where(ok, drop, 0.0)
                hi = drop.astype(BF16)
                lo = (drop - hi.astype(F32)).astype(BF16)
                incl = _bdot(from_here, jnp.concatenate([hi, lo], axis=0))
                e = z - incl
                if diag_first and u == 0:
                    e = jnp.where(ok, e, NEG)
                chains.append((u, hh, e, incl[0:1, :]))
        scores(next_hi, 2)
        r = [r_ref[hh] for hh in range(2)]
        upd = None
        for u, hh, e, total in chains:
            w = jnp.exp(e - r[hh])
            v = v_ref[0, pl.ds(key_start(blk_hi - u), t), :]
            vh = jnp.where(in_head[hh], v, jnp.zeros_like(v))
            pv = _dot_tn(vh, w.astype(BF16))
            upd = pv if upd is None else upd + pv
            r[hh] = r[hh] + total
        acc_ref[...] = acc_ref[...] + upd
        for hh in range(2):
            r_ref[hh] = r[hh]

    @pl.when(qi % 2 == 0)
    def _():
        scores(qi, 1)
        group(qi, 1, True, qi - 1)

    @pl.when(qi % 2 == 1)
    def _():
        scores(qi, 2)
        group(qi, 2, True, qi - 2)

    n_rest = qi // 2
    top = 2 * n_rest - 1

    def body(i, carry):
        group(top - 2 * i, 2, False, top - 2 * i - 2)
        return carry

    lax.fori_loop(0, n_rest, body, 0)
    o_ref[0] = acc_ref[...].T.astype(o_ref.dtype)


def _sb_attention(qkv, *, t):
    b, s, _ = qkv.shape
    return pl.pallas_call(
        functools.partial(_sb_kernel, t=t),
        grid=(b, N_HEADS // 2, s // t),
        in_specs=[pl.BlockSpec((1, t, LANES), lambda i, h, j: (i, j, h)),
                  pl.BlockSpec((1, s, LANES), lambda i, h, j: (i, 0, 4 + h)),
                  pl.BlockSpec((1, s, LANES), lambda i, h, j: (i, 0, 8 + h))],
        out_specs=pl.BlockSpec((1, t, LANES), lambda i, h, j: (i, j, h)),
        out_shape=jax.ShapeDtypeStruct((b, s, HEAD_W), BF16),
        scratch_shapes=[pltpu.VMEM((LANES, t), F32), pltpu.VMEM((2, 2, t, t), F32),
                        pltpu.VMEM((2, 1, t), F32)],
        compiler_params=_params(3),
        name="sb_attention",
    )(qkv, qkv, qkv)


def _merge_kernel(x_ref, yf_ref, ym_ref, ys_ref, g_ref, mod_ref, wf_ref, wm_ref, ws_ref,
                  wo_ref, o_ref):
    d = D_MODEL
    merged = (g_ref[0, :, 0:d] * _bdot(yf_ref[0], wf_ref[...])
              + g_ref[0, :, d:2 * d] * _bdot(ym_ref[0], wm_ref[...])
              + g_ref[0, :, 2 * d:3 * d] * _bdot(ys_ref[0], ws_ref[...]))
    o_ref[0] = x_ref[0] + mod_ref[0, 2:3, :] * _bdot(merged.astype(BF16), wo_ref[...])


def _merge(x, y_fox, y_mla, y_sb, gates, mod, wf, wm, ws, wo, *, tm):
    b, s, d = x.shape
    row = lambda w: pl.BlockSpec((1, tm, w), lambda i, j: (i, j, 0))
    return pl.pallas_call(
        _merge_kernel,
        grid=(b, s // tm),
        in_specs=[row(d), row(HEAD_W), row(HEAD_W), row(HEAD_W), row(3 * d),
                  pl.BlockSpec((1, 6, d), lambda i, j: (i, 0, 0)),
                  _resident((HEAD_W, d)), _resident((HEAD_W, d)), _resident((HEAD_W, d)),
                  _resident((d, d))],
        out_specs=row(d),
        out_shape=jax.ShapeDtypeStruct((b, s, d), F32),
        compiler_params=_params(2),
        name="merge_out",
    )(x, y_fox, y_mla, y_sb, gates, mod, wf, wm, ws, wo)


def _ffn_kernel(x_ref, mod_ref, g_ref, wg_ref, wu_ref, wd_ref, gfin_ref, o_ref, *, final, fc):
    x = x_ref[0]
    sh, sc, gt = mod_ref[0, 3:4, :], mod_ref[0, 4:5, :], mod_ref[0, 5:6, :]
    u = ((_rms(x) * g_ref[...]) * (1.0 + sc) + sh).astype(BF16)
    acc = None
    for a in range(0, D_FF, fc):
        hg = _bdot(u, wg_ref[:, a:a + fc])
        hu = _bdot(u, wu_ref[:, a:a + fc])
        h = ((hg * _sigmoid(hg)) * hu).astype(BF16)
        part = _bdot(h, wd_ref[a:a + fc, :])
        acc = part if acc is None else acc + part
    out = x + gt * acc
    if final:
        out = _rms(out) * gfin_ref[...]
    o_ref[0] = out


def _ffn(x, mod, g_ffn, wg, wu, wd, g_final, *, tm, final):
    b, s, d = x.shape
    row = lambda w: pl.BlockSpec((1, tm, w), lambda i, j: (i, j, 0))
    return pl.pallas_call(
        functools.partial(_ffn_kernel, final=final, fc=D_FF // 2),
        grid=(b, s // tm),
        in_specs=[row(d), pl.BlockSpec((1, 6, d), lambda i, j: (i, 0, 0)), _resident((1, d)),
                  _resident((d, D_FF)), _resident((d, D_FF)), _resident((D_FF, d)),
                  _resident((1, d))],
        out_specs=row(d),
        out_shape=jax.ShapeDtypeStruct((b, s, d), F32),
        compiler_params=_params(2),
        name="ffn",
    )(x, mod, g_ffn, wg, wu, wd, g_final)


def _slot_cols(w, width_per_head, used):
    k = w.shape[0]
    w = w.reshape(k, N_HEADS, width_per_head)[:, :, :used]
    return jnp.pad(w, ((0, 0), (0, 0), (0, LANES - used))).reshape(k, SLOT_W)


def _layout_w_in(w):
    d = w.shape[0]
    fox, ff, ql, kvl, kr, sb, gate = jnp.split(
        w, [1536, 1544, 1928, 2184, 2216, 3752], axis=-1)
    z = lambda n: jnp.zeros((d, n), w.dtype)
    fq, fk, fv = jnp.split(fox, 3, axis=-1)
    kr_slot = jnp.concatenate([z(MLA_NOPE), kr, z(LANES - MLA_NOPE - MLA_ROPE)], axis=-1)
    ff_slot = jnp.concatenate([ff, z(LANES - N_HEADS)], axis=-1)
    cols = [_slot_cols(fq, HEAD_DIM, HEAD_DIM), _slot_cols(fk, HEAD_DIM, HEAD_DIM), fv, sb, gate,
            ql, kvl, kr_slot, ff_slot]
    return jnp.concatenate(cols, axis=-1).astype(BF16)


def kernel(x, c, positions, g_mix, w_ada, b_ada, w_in, b_fox_f, g_mla_q, w_mla_uq, g_mla_kv,
           w_mla_ukv, w_o_fox, w_o_mla, w_o_sb, w_out, g_ffn, w_ffn_gate, w_ffn_up, w_ffn_down,
           g_final):
    b, s, d = x.shape
    depth = w_in.shape[0]
    tm = min(s, 512)
    ta = min(s, 512)
    tsb = min(s, 256)

    mod = _ada(c, w_ada, b_ada).reshape(depth, b, 6, d)
    tables = _rope_tables(positions)
    for l in range(depth):
        w_in_r = _layout_w_in(w_in[l])
        wuq_r = _slot_cols(w_mla_uq[l], MLA_NOPE + MLA_ROPE, MLA_NOPE + MLA_ROPE).astype(BF16)
        wkn_r = _slot_cols(w_mla_ukv[l], 2 * HEAD_DIM, MLA_NOPE).astype(BF16)
        wkv_r = w_mla_ukv[l].reshape(MLA_KV_RANK, N_HEADS, 2 * HEAD_DIM)[:, :, MLA_NOPE:]
        wkv_r = wkv_r.reshape(MLA_KV_RANK, HEAD_W).astype(BF16)
        bff = jnp.pad(b_fox_f[l], (0, LANES - N_HEADS)).reshape(1, LANES)

        fox_q, fox_k, fox_v, sb_qkv, gates, mq, mk, mv = _inproj(
            x, mod[l], g_mix[l].reshape(1, d), w_in_r, bff, g_mla_q[l].reshape(1, -1),
            g_mla_kv[l].reshape(1, -1), wuq_r, wkn_r, wkv_r, tables, tm=tm)
        y_fox = _softmax_attention_flat(fox_q, fox_k, fox_v, t=ta, chunk_shift=0,
                                        name="fox_attention")
        y_mla = _softmax_attention_flat(mq, mk, mv, t=ta, chunk_shift=CHUNK_SHIFT,
                                        name="mla_attention")
        y_sb = _sb_attention_flat(sb_qkv, t=tsb)
        x = _merge(x, y_fox, y_mla, y_sb, gates, mod[l], w_o_fox[l].astype(BF16),
                   w_o_mla[l].astype(BF16), w_o_sb[l].astype(BF16), w_out[l].astype(BF16), tm=tm)
        x = _ffn(x, mod[l], g_ffn[l].reshape(1, d), w_ffn_gate[l].astype(BF16),
                 w_ffn_up[l].astype(BF16), w_ffn_down[l].astype(BF16), g_final.reshape(1, d),
                 tm=tm, final=(l == depth - 1))
    return x
```

```python
import functools

import numpy as np
import jax
import jax.numpy as jnp
from jax import lax
from jax.experimental import pallas as pl
from jax.experimental.pallas import tpu as pltpu

F32 = jnp.float32
BF16 = jnp.bfloat16

D_MODEL = 1024
EPS = 1e-6
HEAD_DIM = 64
N_HEADS = 8
HEAD_W = N_HEADS * HEAD_DIM
CHUNK_SHIFT = 6
MLA_Q_RANK = 384
MLA_KV_RANK = 256
MLA_NOPE = 64
MLA_ROPE = 32
ROPE_HALF = MLA_ROPE // 2
ROPE_BASE = 10000.0
MLA_SCALE = (MLA_NOPE + MLA_ROPE) ** -0.5
QK_SCALE = HEAD_DIM ** -0.5
D_FF = 2816
LANES = 128
SLOT_W = N_HEADS * LANES
N_PIECES = 3

C_FQ = 0
C_FK = 1024
C_FV = 2048
C_SB = 2560
C_GATE = 4096
C_QL = 7168
C_KVL = 7552
C_KR = 7808
C_FF = 7936
IN_W = 8064

ROW_CHUNK = 64
SIGN_BIT = np.uint32(0x80000000)
NEG = -1e30
VMEM_LIMIT = 56 * 1024 * 1024


def _params(n_grid):
    return pltpu.CompilerParams(dimension_semantics=("arbitrary",) * n_grid,
                                vmem_limit_bytes=VMEM_LIMIT)


def _resident(shape):
    nd = len(shape)
    return pl.BlockSpec(shape, lambda *_: (0,) * nd, pipeline_mode=pl.Buffered(1))


def _bdot(a, b):
    return jnp.dot(a, b, preferred_element_type=F32)


def _dot_nt(a, b):
    return lax.dot_general(a, b, (((1,), (1,)), ((), ())), preferred_element_type=F32)


def _dot_tn(a, b):
    return lax.dot_general(a, b, (((0,), (0,)), ((), ())), preferred_element_type=F32)


def _split3(v):
    hi = v.astype(BF16)
    r = v - hi.astype(F32)
    mid = r.astype(BF16)
    lo = (r - mid.astype(F32)).astype(BF16)
    return hi, mid, lo


def _log_sigmoid(z):
    return jnp.minimum(z, 0.0) - jnp.log1p(jnp.exp(-jnp.abs(z)))


def _sigmoid(z):
    return 1.0 / (1.0 + jnp.exp(-z))


def _rms(x):
    return x * lax.rsqrt(jnp.mean(x * x, axis=-1, keepdims=True) + EPS)


def _ada_kernel(c_ref, w_ref, b_ref, o_ref):
    c = c_ref[...]
    cond = c * _sigmoid(c)
    chi, cmid, clo = _split3(cond)
    whi, wmid, wlo = _split3(w_ref[0])
    acc = _bdot(chi, whi) + (_bdot(chi, wmid) + _bdot(cmid, whi))
    acc = acc + (_bdot(chi, wlo) + _bdot(clo, whi) + _bdot(cmid, wmid))
    o_ref[0] = acc + b_ref[0]


def _ada(c, w_ada, b_ada):
    depth, d, n = w_ada.shape
    b = c.shape[0]
    tn = 1024
    return pl.pallas_call(
        _ada_kernel,
        grid=(depth, n // tn),
        in_specs=[pl.BlockSpec((b, d), lambda l, j: (0, 0)),
                  pl.BlockSpec((1, d, tn), lambda l, j: (l, 0, j)),
                  pl.BlockSpec((1, 1, tn), lambda l, j: (l, 0, j))],
        out_specs=pl.BlockSpec((1, b, tn), lambda l, j: (l, 0, j)),
        out_shape=jax.ShapeDtypeStruct((depth, b, n), F32),
        compiler_params=_params(2),
        name="adaln",
    )(c, w_ada, b_ada.reshape(depth, 1, n))


def _rope_kernel(pos_ref, inv_ref, cos_ref, sina_ref, sinb_ref):
    ang = pos_ref[0].astype(F32) * inv_ref[...]
    lane = lax.broadcasted_iota(jnp.int32, (1, LANES), 1)
    sin = jnp.sin(ang)
    cos_ref[0] = jnp.cos(ang)
    lo, mid, hi = MLA_NOPE, MLA_NOPE + ROPE_HALF, MLA_NOPE + MLA_ROPE
    sina_ref[0] = jnp.where((lane >= lo) & (lane < mid), -sin, 0.0)
    sinb_ref[0] = jnp.where((lane >= mid) & (lane < hi), sin, 0.0)


def _rope_tables(positions):
    b, s = positions.shape
    ts = min(s, 512)
    inv = ROPE_BASE ** (-jnp.arange(ROPE_HALF, dtype=F32) / ROPE_HALF)
    inv_slot = jnp.concatenate([jnp.zeros((MLA_NOPE,), F32), inv, inv,
                                jnp.zeros((LANES - MLA_NOPE - MLA_ROPE,), F32)]).reshape(1, LANES)
    spec = pl.BlockSpec((1, ts, LANES), lambda i, j: (i, j, 0))
    return pl.pallas_call(
        _rope_kernel,
        grid=(b, s // ts),
        in_specs=[pl.BlockSpec((1, ts, 1), lambda i, j: (i, j, 0)),
                  pl.BlockSpec((1, LANES), lambda i, j: (0, 0))],
        out_specs=[spec, spec, spec],
        out_shape=[jax.ShapeDtypeStruct((b, s, LANES), F32)] * 3,
        compiler_params=_params(2),
        name="rope_tables",
    )(positions.reshape(b, s, 1), inv_slot)


def _rope_slot(x, cos, sina, sinb):
    return (x * cos + pltpu.roll(x, LANES - ROPE_HALF, 1) * sina
            + pltpu.roll(x, ROPE_HALF, 1) * sinb)


def _inproj_kernel(x_ref, mod_ref, g_ref, w_ref, bff_ref, gq_ref, gkv_ref, wuq_ref, wkn_ref,
                   wkv_ref, cos_ref, sina_ref, sinb_ref, place_ref, qc_ref, kc_ref,
                   fq_ref, fk_ref, fv_ref, sb_ref, gate_ref, mq_ref, mk_ref, mv_ref,
                   carry_ref, *, tm):
    x = x_ref[0]
    sh, sc = mod_ref[0, 0:1, :], mod_ref[0, 1:2, :]
    u = ((_rms(x) * g_ref[...]) * (1.0 + sc) + sh).astype(BF16)

    def seg(a, b):
        return _bdot(u, w_ref[:, a:b])

    @pl.when(pl.program_id(1) == 0)
    def _():
        carry_ref[...] = jnp.zeros_like(carry_ref)

    logf = _log_sigmoid(seg(C_FF, IN_W) + bff_ref[...])
    r = lax.broadcasted_iota(jnp.int32, (tm, tm), 0)
    c = lax.broadcasted_iota(jnp.int32, (tm, tm), 1)
    tri = jnp.where(r >= c, 1.0, 0.0).astype(BF16)
    hi, mid, lo = _split3(logf)
    cum = (_bdot(tri, hi) + _bdot(tri, mid) + _bdot(tri, lo)) + carry_ref[...]
    carry_ref[...] = cum[tm - 1:tm, :]
    p0, p1, p2 = (p.astype(F32) for p in _split3(cum))
    lane = lax.broadcasted_iota(jnp.int32, (1, LANES), 1)
    pieces = jnp.where(lane < N_HEADS, p0,
                       jnp.where(lane < 2 * N_HEADS, pltpu.roll(p1, N_HEADS, 1),
                                 jnp.where(lane < 3 * N_HEADS, pltpu.roll(p2, 2 * N_HEADS, 1),
                                           0.0))).astype(BF16)

    for c in range(2):
        a = 512 * c
        placed = _bdot(pieces, place_ref[:, a:a + 512])
        q = seg(C_FQ + a, C_FQ + a + 512) * QK_SCALE + placed + qc_ref[:, a:a + 512]
        fq_ref[0, :, a:a + 512] = q.astype(BF16)
        placed = _bdot(pieces, place_ref[:, SLOT_W + a:SLOT_W + a + 512])
        k = seg(C_FK + a, C_FK + a + 512) + placed + kc_ref[:, a:a + 512]
        fk_ref[0, :, a:a + 512] = k.astype(BF16)
    fv_ref[0] = seg(C_FV, C_SB).astype(BF16)

    for c in range(3):
        a = 512 * c
        sq = seg(C_SB + a, C_SB + a + 512)
        if c == 0:
            sq = sq * QK_SCALE
        sb_ref[0, :, a:a + 512] = sq.astype(BF16)
    for c in range(6):
        a = 512 * c
        gate_ref[0, :, a:a + 512] = _sigmoid(seg(C_GATE + a, C_GATE + a + 512))

    cos, sina, sinb = cos_ref[0], sina_ref[0], sinb_ref[0]
    cq = (_rms(seg(C_QL, C_KVL)) * gq_ref[...]).astype(BF16)
    ckv = (_rms(seg(C_KVL, C_KR)) * gkv_ref[...]).astype(BF16)
    kr = _rope_slot(seg(C_KR, C_FF), cos, sina, sinb)
    for h in range(N_HEADS):
        a = LANES * h
        q = _rope_slot(_bdot(cq, wuq_ref[:, a:a + LANES]), cos, sina, sinb)
        mq_ref[0, :, a:a + LANES] = (q * MLA_SCALE).astype(BF16)
        mk_ref[0, :, a:a + LANES] = (_bdot(ckv, wkn_ref[:, a:a + LANES]) + kr).astype(BF16)
    mv_ref[0] = _bdot(ckv, wkv_ref[...]).astype(BF16)


def _fox_placement():
    place = np.zeros((LANES, 2 * SLOT_W), np.float32)
    qc = np.zeros((1, SLOT_W), np.float32)
    kc = np.zeros((1, SLOT_W), np.float32)
    for h in range(N_HEADS):
        for p in range(N_PIECES):
            place[N_HEADS * p + h, LANES * h + HEAD_DIM + N_PIECES + p] = 1.0
            place[N_HEADS * p + h, SLOT_W + LANES * h + HEAD_DIM + p] = 1.0
            qc[0, LANES * h + HEAD_DIM + p] = -1.0
            kc[0, LANES * h + HEAD_DIM + N_PIECES + p] = 1.0
    return jnp.asarray(place, BF16), jnp.asarray(qc), jnp.asarray(kc)


def _inproj(x, mod, g_mix, w_in_r, bff, g_q, g_kv, wuq_r, wkn_r, wkv_r, tables, *, tm):
    b, s, d = x.shape
    cos, sina, sinb = tables
    place, qc, kc = _fox_placement()
    row = lambda w: pl.BlockSpec((1, tm, w), lambda i, j: (i, j, 0))
    widths = [(SLOT_W, BF16),
              (SLOT_W, BF16),
              (HEAD_W, BF16),
              (3 * HEAD_W, BF16),
              (3 * D_MODEL, F32),
              (SLOT_W, BF16),
              (SLOT_W, BF16),
              (HEAD_W, BF16)]
    return pl.pallas_call(
        functools.partial(_inproj_kernel, tm=tm),
        grid=(b, s // tm),
        in_specs=[row(d),
                  pl.BlockSpec((1, 6, d), lambda i, j: (i, 0, 0)),
                  _resident((1, d)), _resident((d, IN_W)), _resident((1, LANES)),
                  _resident((1, MLA_Q_RANK)), _resident((1, MLA_KV_RANK)),
                  _resident((MLA_Q_RANK, SLOT_W)), _resident((MLA_KV_RANK, SLOT_W)),
                  _resident((MLA_KV_RANK, HEAD_W)),
                  row(LANES), row(LANES), row(LANES),
                  _resident((LANES, 2 * SLOT_W)), _resident((1, SLOT_W)), _resident((1, SLOT_W))],
        out_specs=[row(w) for w, _ in widths],
        out_shape=[jax.ShapeDtypeStruct((b, s, w), dt) for w, dt in widths],
        scratch_shapes=[pltpu.VMEM((1, LANES), F32)],
        compiler_params=_params(2),
        name="inproj",
    )(x, mod, g_mix, w_in_r, bff, g_q, g_kv, wuq_r, wkn_r, wkv_r, cos, sina, sinb, place, qc, kc)


def _head_masks():
    lane = lax.broadcasted_iota(jnp.int32, (1, LANES), 1)
    return [(lane >= HEAD_DIM * hh) & (lane < HEAD_DIM * (hh + 1)) for hh in range(2)]


def _col_reduce(x, op, final, chunk=64):
    rows = x.shape[0]
    if rows > chunk and rows % chunk == 0:
        acc = x[:chunk]
        for a in range(chunk, rows, chunk):
            acc = op(acc, x[a:a + chunk])
        x, rows = acc, chunk
    while rows > 8 and rows % 16 == 0:
        rows //= 2
        x = op(x[:rows], x[rows:])
    return final(x, axis=0, keepdims=True)


def _colmax(x):
    return _col_reduce(x, jnp.maximum, jnp.max)


def _colsum(x):
    return _col_reduce(x, jnp.add, jnp.sum)


def _pipeline(n, stages):
    depth = len(stages)

    def static_step(t):
        for s, stage in enumerate(stages):
            if 0 <= t - s < n:
                stage(t - s, (t - s) % 2)

    first_full = depth - 1
    n_full = max(n - depth + 1, 0)
    for t in range(min(first_full, n + depth - 1)):
        static_step(t)
    if n_full >= 2:
        def body(pair, carry):
            t = first_full + 2 * pair
            for u in range(2):
                for s, stage in enumerate(stages):
                    stage(t + u - s, (first_full + u - s) % 2)
            return carry

        lax.fori_loop(0, n_full // 2, body, 0)
    for t in range(first_full + 2 * (n_full // 2), n + depth - 1):
        static_step(t)


def _block_tables(n_tiles, t, descending):
    diag = [(i * t, i * t) for i in range(n_tiles)]
    below = []
    for i in range(1, n_tiles):
        js = range(i - 1, -1, -1) if descending else range(i)
        below += [(i * t, j * t) for j in js]
    return jnp.asarray(np.array(diag + below, np.int32).T), len(diag), len(below)


def _softmax_flat_kernel(tab_ref, q_ref, k_ref, v_ref, o_ref, acc_ref, m_ref, al_ref,
                         s0_ref, s1_ref, p0_ref, p1_ref, *, t, n_diag, n_below, chunk_shift):
    in_head = _head_masks()
    s_bufs, p_bufs = (s0_ref, s1_ref), (p0_ref, p1_ref)
    kio = lax.broadcasted_iota(jnp.int32, (t, 1), 0)
    qio = lax.broadcasted_iota(jnp.int32, (1, t), 1)
    acc_ref[...] = jnp.zeros_like(acc_ref)
    m_ref[...] = jnp.full_like(m_ref, NEG)

    def stages(base, diag):
        def item(i):
            return (pl.multiple_of(tab_ref[0, base + i], t), pl.multiple_of(tab_ref[1, base + i], t))

        def scores(i, par):
            q0, k0 = item(i)
            for hh in range(2):
                lanes = slice(LANES * hh, LANES * (hh + 1))
                s_bufs[par][hh] = _dot_nt(k_ref[0, pl.ds(k0, t), lanes],
                                          q_ref[0, pl.ds(q0, t), lanes])

        def numerators(i, par):
            q0, _ = item(i)
            for hh in range(2):
                def rows(a):
                    s = s_bufs[par][hh, a:a + ROW_CHUNK, :]
                    if diag:
                        ok = ((kio[a:a + ROW_CHUNK] >> chunk_shift) <= (qio >> chunk_shift))
                        s = jnp.where(ok, s, NEG)
                    return s

                top = rows(0)
                for a in range(ROW_CHUNK, t, ROW_CHUNK):
                    top = jnp.maximum(top, rows(a))
                m_old = m_ref[hh, :, pl.ds(q0, t)]
                m_new = jnp.maximum(m_old, _colmax(top))
                for a in range(0, t, ROW_CHUNK):
                    p_bufs[par][hh, a:a + ROW_CHUNK, :] = jnp.exp(rows(a) - m_new).astype(BF16)
                al_ref[par, hh] = jnp.exp(m_old - m_new)
                m_ref[hh, :, pl.ds(q0, t)] = m_new

        def values(i, par):
            q0, k0 = item(i)
            v = v_ref[0, pl.ds(k0, t), :]
            for hh in range(2):
                v1 = jnp.where(in_head[hh], v, jnp.ones_like(v))
                acc_ref[hh, :, pl.ds(q0, t)] = (acc_ref[hh, :, pl.ds(q0, t)] * al_ref[par, hh]
                                                + _dot_tn(v1, p_bufs[par][hh]))

        return [scores, numerators, values]

    _pipeline(n_diag, stages(0, True))
    _pipeline(n_below, stages(n_diag, False))

    sub = lax.broadcasted_iota(jnp.int32, (LANES, 1), 0)

    def finish(tile, carry):
        q0 = pl.multiple_of(tile * t, t)
        outs = []
        for hh in range(2):
            acc = acc_ref[hh, :, pl.ds(q0, t)]
            denom = acc[HEAD_DIM * (1 - hh):HEAD_DIM * (1 - hh) + 1, :]
            outs.append(acc / denom)
        o_ref[0, pl.ds(q0, t), :] = jnp.where(sub < HEAD_DIM, outs[0], outs[1]).T.astype(o_ref.dtype)
        return carry

    lax.fori_loop(0, n_diag, finish, 0)


def _softmax_attention_flat(q, k, v, *, t, chunk_shift, name):
    b, s, _ = q.shape
    table, n_diag, n_below = _block_tables(s // t, t, descending=False)
    grid_spec = pltpu.PrefetchScalarGridSpec(
        num_scalar_prefetch=1,
        grid=(b, N_HEADS // 2),
        in_specs=[pl.BlockSpec((1, s, 2 * LANES), lambda i, h, tab: (i, 0, h)),
                  pl.BlockSpec((1, s, 2 * LANES), lambda i, h, tab: (i, 0, h)),
                  pl.BlockSpec((1, s, LANES), lambda i, h, tab: (i, 0, h))],
        out_specs=pl.BlockSpec((1, s, LANES), lambda i, h, tab: (i, 0, h)),
        scratch_shapes=[pltpu.VMEM((2, LANES, s), F32),
                        pltpu.VMEM((2, 1, s), F32),
                        pltpu.VMEM((2, 2, 1, t), F32),
                        pltpu.VMEM((2, t, t), F32), pltpu.VMEM((2, t, t), F32),
                        pltpu.VMEM((2, t, t), BF16), pltpu.VMEM((2, t, t), BF16)])
    return pl.pallas_call(
        functools.partial(_softmax_flat_kernel, t=t, n_diag=n_diag, n_below=n_below,
                          chunk_shift=chunk_shift),
        grid_spec=grid_spec,
        out_shape=jax.ShapeDtypeStruct((b, s, HEAD_W), BF16),
        compiler_params=_params(2),
        name=name,
    )(table, q, k, v)


def _sb_flat_kernel(tab_ref, q_ref, k_ref, v_ref, o_ref, acc_ref, r_ref, tot_ref, d_ref,
                    z0_ref, z1_ref, e0_ref, e1_ref, w0_ref, w1_ref, *, t, n_diag, n_below):
    in_head = _head_masks()
    z_bufs, e_bufs, w_bufs = (z0_ref, z1_ref), (e0_ref, e1_ref), (w0_ref, w1_ref)
    kio = lax.broadcasted_iota(jnp.int32, (t, 1), 0)
    qio = lax.broadcasted_iota(jnp.int32, (1, t), 1)
    ss = lax.broadcasted_iota(jnp.int32, (t, t), 0)
    jj = lax.broadcasted_iota(jnp.int32, (t, t), 1)
    from_here = jnp.where(jj >= ss, 1.0, 0.0).astype(BF16)
    acc_ref[...] = jnp.zeros_like(acc_ref)
    r_ref[...] = jnp.zeros_like(r_ref)

    def stages(base, diag):
        def item(i):
            return (pl.multiple_of(tab_ref[0, base + i], t), pl.multiple_of(tab_ref[1, base + i], t))

        def scores(i, par):
            q0, k0 = item(i)
            k = k_ref[0, pl.ds(k0, t), :]
            q = q_ref[0, pl.ds(q0, t), :]
            for hh in range(2):
                qh = jnp.where(in_head[hh], q, jnp.zeros_like(q))
                z_bufs[par][hh] = _dot_nt(k, qh)

        def drops(i, par):
            for hh in range(2):
                for a in range(0, t, ROW_CHUNK):
                    rows = slice(a, a + ROW_CHUNK)
                    z = z_bufs[par][hh, rows, :]
                    neg_abs = pltpu.bitcast(pltpu.bitcast(z, jnp.uint32) | SIGN_BIT, F32)
                    drop = jnp.maximum(z, 0.0) + jnp.log(1.0 + jnp.exp(neg_abs))
                    if diag:
                        drop = jnp.where(kio[rows] < qio, drop, 0.0)
                    d_ref[hh, rows, :] = drop.astype(BF16)
                incl = _bdot(from_here, d_ref[hh])
                for a in range(0, t, ROW_CHUNK):
                    rows = slice(a, a + ROW_CHUNK)
                    e = z_bufs[par][hh, rows, :] - incl[rows]
                    if diag:
                        e = jnp.where(kio[rows] < qio, e, NEG)
                    e_bufs[par][hh, rows, :] = e
                tot_ref[par, hh] = incl[0:1, :]

        def weights(i, par):
            q0, _ = item(i)
            for hh in range(2):
                r_old = r_ref[hh, :, pl.ds(q0, t)]
                for a in range(0, t, ROW_CHUNK):
                    rows = slice(a, a + ROW_CHUNK)
                    w_bufs[par][hh, rows, :] = jnp.exp(e_bufs[par][hh, rows, :] - r_old).astype(BF16)
                r_ref[hh, :, pl.ds(q0, t)] = r_old + tot_ref[par, hh]

        def values(i, par):
            q0, k0 = item(i)
            v = v_ref[0, pl.ds(k0, t), :]
            upd = None
            for hh in range(2):
                vh = jnp.where(in_head[hh], v, jnp.zeros_like(v))
                pv = _dot_tn(vh, w_bufs[par][hh])
                upd = pv if upd is None else upd + pv
            acc_ref[:, pl.ds(q0, t)] = acc_ref[:, pl.ds(q0, t)] + upd

        return [scores, drops, weights, values]

    _pipeline(n_diag, stages(0, True))
    _pipeline(n_below, stages(n_diag, False))

    def finish(tile, carry):
        q0 = pl.multiple_of(tile * t, t)
        o_ref[0, pl.ds(q0, t), :] = acc_ref[:, pl.ds(q0, t)].T.astype(o_ref.dtype)
        return carry

    lax.fori_loop(0, n_diag, finish, 0)


def _sb_attention_flat(qkv, *, t):
    b, s, _ = qkv.shape
    table, n_diag, n_below = _block_tables(s // t, t, descending=True)
    grid_spec = pltpu.PrefetchScalarGridSpec(
        num_scalar_prefetch=1,
        grid=(b, N_HEADS // 2),
        in_specs=[pl.BlockSpec((1, s, LANES), lambda i, h, tab: (i, 0, h)),
                  pl.BlockSpec((1, s, LANES), lambda i, h, tab: (i, 0, 4 + h)),
                  pl.BlockSpec((1, s, LANES), lambda i, h, tab: (i, 0, 8 + h))],
        out_specs=pl.BlockSpec((1, s, LANES), lambda i, h, tab: (i, 0, h)),
        scratch_shapes=[pltpu.VMEM((LANES, s), F32),
                        pltpu.VMEM((2, 1, s), F32),
                        pltpu.VMEM((2, 2, 1, t), F32),
                        pltpu.VMEM((2, t, t), BF16),
                        pltpu.VMEM((2, t, t), F32), pltpu.VMEM((2, t, t), F32),
                        pltpu.VMEM((2, t, t), F32), pltpu.VMEM((2, t, t), F32),
                        pltpu.VMEM((2, t, t), BF16), pltpu.VMEM((2, t, t), BF16)])
    return pl.pallas_call(
        functools.partial(_sb_flat_kernel, t=t, n_diag=n_diag, n_below=n_below),
        grid_spec=grid_spec,
        out_shape=jax.ShapeDtypeStruct((b, s, HEAD_W), BF16),
        compiler_params=_params(2),
        name="sb_attention",
    )(table, qkv, qkv, qkv)


def _softmax_attn_kernel(q_ref, k_ref, v_ref, o_ref, acc_ref, s0_ref, s1_ref, *, tq, tk,
                         chunk_shift):
    qi = pl.program_id(2)
    in_head = _head_masks()
    qs = [q_ref[0, :, LANES * hh:LANES * (hh + 1)] for hh in range(2)]
    qpos = qi * tq + lax.broadcasted_iota(jnp.int32, (1, tq), 1)
    acc_ref[...] = jnp.zeros_like(acc_ref)

    def produce(j, buf):
        start = pl.multiple_of(j * tk, tk)
        for hh in range(2):
            k = k_ref[0, pl.ds(start, tk), LANES * hh:LANES * (hh + 1)]
            buf[hh] = _dot_nt(k, qs[hh])

    def consume(j, buf, m, masked):
        start = pl.multiple_of(j * tk, tk)
        v = v_ref[0, pl.ds(start, tk), :]
        m_out = []
        for hh in range(2):
            s = buf[hh]
            if masked:
                kpos = start + lax.broadcasted_iota(jnp.int32, (tk, 1), 0)
                s = jnp.where((kpos >> chunk_shift) <= (qpos >> chunk_shift), s, NEG)
            m_new = jnp.maximum(m[hh], _colmax(s))
            p = jnp.exp(s - m_new)
            alpha = jnp.exp(m[hh] - m_new)
            v1 = jnp.where(in_head[hh], v, jnp.ones_like(v))
            acc_ref[hh] = acc_ref[hh] * alpha + _dot_tn(v1, p.astype(BF16))
            m_out.append(m_new)
        return tuple(m_out)

    n_last = (qi * tq + tq - 1) // tk
    n_pairs = n_last // 2

    def body(i, m):
        j = 2 * i
        produce(j + 1, s1_ref)
        m = consume(j, s0_ref, m, False)
        produce(j + 2, s0_ref)
        return consume(j + 1, s1_ref, m, False)

    produce(0, s0_ref)
    m = (jnp.full((1, tq), NEG, F32),) * 2
    m = lax.fori_loop(0, n_pairs, body, m)
    j = 2 * n_pairs

    @pl.when(n_last % 2 == 1)
    def _():
        produce(j + 1, s1_ref)
        consume(j + 1, s1_ref, consume(j, s0_ref, m, False), True)

    @pl.when(n_last % 2 == 0)
    def _():
        consume(j, s0_ref, m, True)

    sub = lax.broadcasted_iota(jnp.int32, (LANES, 1), 0)
    outs = []
    for hh in range(2):
        acc = acc_ref[hh]
        denom = acc[HEAD_DIM * (1 - hh):HEAD_DIM * (1 - hh) + 1, :]
        outs.append(acc / denom)
    o_ref[0] = jnp.where(sub < HEAD_DIM, outs[0], outs[1]).T.astype(o_ref.dtype)


def _softmax_attention(q, k, v, *, tq, tk, chunk_shift, name):
    b, s, _ = q.shape
    return pl.pallas_call(
        functools.partial(_softmax_attn_kernel, tq=tq, tk=tk, chunk_shift=chunk_shift),
        grid=(b, N_HEADS // 2, s // tq),
        in_specs=[pl.BlockSpec((1, tq, 2 * LANES), lambda i, h, j: (i, j, h)),
                  pl.BlockSpec((1, s, 2 * LANES), lambda i, h, j: (i, 0, h)),
                  pl.BlockSpec((1, s, LANES), lambda i, h, j: (i, 0, h))],
        out_specs=pl.BlockSpec((1, tq, LANES), lambda i, h, j: (i, j, h)),
        out_shape=jax.ShapeDtypeStruct((b, s, HEAD_W), BF16),
        scratch_shapes=[pltpu.VMEM((2, LANES, tq), F32), pltpu.VMEM((2, tk, tq), F32),
                        pltpu.VMEM((2, tk, tq), F32)],
        compiler_params=_params(3),
        name=name,
    )(q, k, v)


def _sb_kernel(q_ref, k_ref, v_ref, o_ref, acc_ref, z_ref, r_ref, *, t):
    qi = pl.program_id(2)
    in_head = _head_masks()
    q = q_ref[0]
    qs = [jnp.where(in_head[hh], q, jnp.zeros_like(q)) for hh in range(2)]
    qpos = lax.broadcasted_iota(jnp.int32, (1, t), 1)
    kpos = lax.broadcasted_iota(jnp.int32, (t, 1), 0)
    ss = lax.broadcasted_iota(jnp.int32, (t, 2 * t), 0)
    jj = lax.broadcasted_iota(jnp.int32, (t, 2 * t), 1) & (t - 1)
    from_here = jnp.where(jj >= ss, 1.0, 0.0).astype(BF16)
    acc_ref[...] = jnp.zeros_like(acc_ref)
    r_ref[...] = jnp.zeros_like(r_ref)

    def key_start(blk):
        return pl.multiple_of(jnp.maximum(blk, 0) * t, t)

    def scores(blk_hi, nblk):
        for u in range(nblk):
            k = k_ref[0, pl.ds(key_start(blk_hi - u), t), :]
            for hh in range(2):
                z_ref[u, hh] = _dot_nt(k, qs[hh])

    def group(blk_hi, nblk, diag_first, next_hi):
        chains = []
        for u in range(nblk):
            for hh in range(2):
                z = z_ref[u, hh]
                drop = jnp.maximum(z, 0.0) + jnp.log(1.0 + jnp.exp(-jnp.abs(z)))
                if diag_first and u == 0:
                    ok = kpos < qpos
                    drop = jnp.where(ok, drop, 0.0)
                hi = drop.astype(BF16)
                lo = (drop - hi.astype(F32)).astype(BF16)
                incl = _bdot(from_here, jnp.concatenate([hi, lo], axis=0))
                e = z - incl
                if diag_first and u == 0:
                    e = jnp.where(ok, e, NEG)
                chains.append((u, hh, e, incl[0:1, :]))
        scores(next_hi, 2)
        r = [r_ref[hh] for hh in range(2)]
        upd = None
        for u, hh, e, total in chains:
            w = jnp.exp(e - r[hh])
            v = v_ref[0, pl.ds(key_start(blk_hi - u), t), :]
            vh = jnp.where(in_head[hh], v, jnp.zeros_like(v))
            pv = _dot_tn(vh, w.astype(BF16))
            upd = pv if upd is None else upd + pv
            r[hh] = r[hh] + total
        acc_ref[...] = acc_ref[...] + upd
        for hh in range(2):
            r_ref[hh] = r[hh]

    @pl.when(qi % 2 == 0)
    def _():
        scores(qi, 1)
        group(qi, 1, True, qi - 1)

    @pl.when(qi % 2 == 1)
    def _():
        scores(qi, 2)
        group(qi, 2, True, qi - 2)

    n_rest = qi // 2
    top = 2 * n_rest - 1

    def body(i, carry):
        group(top - 2 * i, 2, False, top - 2 * i - 2)
        return carry

    lax.fori_loop(0, n_rest, body, 0)
    o_ref[0] = acc_ref[...].T.astype(o_ref.dtype)


def _sb_attention(qkv, *, t):
    b, s, _ = qkv.shape
    return pl.pallas_call(
        functools.partial(_sb_kernel, t=t),
        grid=(b, N_HEADS // 2, s // t),
        in_specs=[pl.BlockSpec((1, t, LANES), lambda i, h, j: (i, j, h)),
                  pl.BlockSpec((1, s, LANES), lambda i, h, j: (i, 0, 4 + h)),
                  pl.BlockSpec((1, s, LANES), lambda i, h, j: (i, 0, 8 + h))],
        out_specs=pl.BlockSpec((1, t, LANES), lambda i, h, j: (i, j, h)),
        out_shape=jax.ShapeDtypeStruct((b, s, HEAD_W), BF16),
        scratch_shapes=[pltpu.VMEM((LANES, t), F32), pltpu.VMEM((2, 2, t, t), F32),
                        pltpu.VMEM((2, 1, t), F32)],
        compiler_params=_params(3),
        name="sb_attention",
    )(qkv, qkv, qkv)


def _merge_kernel(x_ref, yf_ref, ym_ref, ys_ref, g_ref, mod_ref, wf_ref, wm_ref, ws_ref,
                  wo_ref, o_ref):
    d = D_MODEL
    merged = (g_ref[0, :, 0:d] * _bdot(yf_ref[0], wf_ref[...])
              + g_ref[0, :, d:2 * d] * _bdot(ym_ref[0], wm_ref[...])
              + g_ref[0, :, 2 * d:3 * d] * _bdot(ys_ref[0], ws_ref[...]))
    o_ref[0] = x_ref[0] + mod_ref[0, 2:3, :] * _bdot(merged.astype(BF16), wo_ref[...])


def _merge(x, y_fox, y_mla, y_sb, gates, mod, wf, wm, ws, wo, *, tm):
    b, s, d = x.shape
    row = lambda w: pl.BlockSpec((1, tm, w), lambda i, j: (i, j, 0))
    return pl.pallas_call(
        _merge_kernel,
        grid=(b, s // tm),
        in_specs=[row(d), row(HEAD_W), row(HEAD_W), row(HEAD_W), row(3 * d),
                  pl.BlockSpec((1, 6, d), lambda i, j: (i, 0, 0)),
                  _resident((HEAD_W, d)), _resident((HEAD_W, d)), _resident((HEAD_W, d)),
                  _resident((d, d))],
        out_specs=row(d),
        out_shape=jax.ShapeDtypeStruct((b, s, d), F32),
        compiler_params=_params(2),
        name="merge_out",
    )(x, y_fox, y_mla, y_sb, gates, mod, wf, wm, ws, wo)


def _ffn_kernel(x_ref, mod_ref, g_ref, wg_ref, wu_ref, wd_ref, gfin_ref, o_ref, *, final, fc):
    x = x_ref[0]
    sh, sc, gt = mod_ref[0, 3:4, :], mod_ref[0, 4:5, :], mod_ref[0, 5:6, :]
    u = ((_rms(x) * g_ref[...]) * (1.0 + sc) + sh).astype(BF16)
    acc = None
    for a in range(0, D_FF, fc):
        hg = _bdot(u, wg_ref[:, a:a + fc])
        hu = _bdot(u, wu_ref[:, a:a + fc])
        h = ((hg * _sigmoid(hg)) * hu).astype(BF16)
        part = _bdot(h, wd_ref[a:a + fc, :])
        acc = part if acc is None else acc + part
    out = x + gt * acc
    if final:
        out = _rms(out) * gfin_ref[...]
    o_ref[0] = out


def _ffn(x, mod, g_ffn, wg, wu, wd, g_final, *, tm, final):
    b, s, d = x.shape
    row = lambda w: pl.BlockSpec((1, tm, w), lambda i, j: (i, j, 0))
    return pl.pallas_call(
        functools.partial(_ffn_kernel, final=final, fc=D_FF // 2),
        grid=(b, s // tm),
        in_specs=[row(d), pl.BlockSpec((1, 6, d), lambda i, j: (i, 0, 0)), _resident((1, d)),
                  _resident((d, D_FF)), _resident((d, D_FF)), _resident((D_FF, d)),
                  _resident((1, d))],
        out_specs=row(d),
        out_shape=jax.ShapeDtypeStruct((b, s, d), F32),
        compiler_params=_params(2),
        name="ffn",
    )(x, mod, g_ffn, wg, wu, wd, g_final)


def _slot_cols(w, width_per_head, used):
    k = w.shape[0]
    w = w.reshape(k, N_HEADS, width_per_head)[:, :, :used]
    return jnp.pad(w, ((0, 0), (0, 0), (0, LANES - used))).reshape(k, SLOT_W)


def _layout_w_in(w):
    d = w.shape[0]
    fox, ff, ql, kvl, kr, sb, gate = jnp.split(
        w, [1536, 1544, 1928, 2184, 2216, 3752], axis=-1)
    z = lambda n: jnp.zeros((d, n), w.dtype)
    fq, fk, fv = jnp.split(fox, 3, axis=-1)
    kr_slot = jnp.concatenate([z(MLA_NOPE), kr, z(LANES - MLA_NOPE - MLA_ROPE)], axis=-1)
    ff_slot = jnp.concatenate([ff, z(LANES - N_HEADS)], axis=-1)
    cols = [_slot_cols(fq, HEAD_DIM, HEAD_DIM), _slot_cols(fk, HEAD_DIM, HEAD_DIM), fv, sb, gate,
            ql, kvl, kr_slot, ff_slot]
    return jnp.concatenate(cols, axis=-1).astype(BF16)


def kernel(x, c, positions, g_mix, w_ada, b_ada, w_in, b_fox_f, g_mla_q, w_mla_uq, g_mla_kv,
           w_mla_ukv, w_o_fox, w_o_mla, w_o_sb, w_out, g_ffn, w_ffn_gate, w_ffn_up, w_ffn_down,
           g_final):
    b, s, d = x.shape
    depth = w_in.shape[0]
    tm = min(s, 512)
    ta = min(s, 512)
    tsb = min(s, 256)

    mod = _ada(c, w_ada, b_ada).reshape(depth, b, 6, d)
    tables = _rope_tables(positions)
    for l in range(depth):
        w_in_r = _layout_w_in(w_in[l])
        wuq_r = _slot_cols(w_mla_uq[l], MLA_NOPE + MLA_ROPE, MLA_NOPE + MLA_ROPE).astype(BF16)
        wkn_r = _slot_cols(w_mla_ukv[l], 2 * HEAD_DIM, MLA_NOPE).astype(BF16)
        wkv_r = w_mla_ukv[l].reshape(MLA_KV_RANK, N_HEADS, 2 * HEAD_DIM)[:, :, MLA_NOPE:]
        wkv_r = wkv_r.reshape(MLA_KV_RANK, HEAD_W).astype(BF16)
        bff = jnp.pad(b_fox_f[l], (0, LANES - N_HEADS)).reshape(1, LANES)

        fox_q, fox_k, fox_v, sb_qkv, gates, mq, mk, mv = _inproj(
            x, mod[l], g_mix[l].reshape(1, d), w_in_r, bff, g_mla_q[l].reshape(1, -1),
            g_mla_kv[l].reshape(1, -1), wuq_r, wkn_r, wkv_r, tables, tm=tm)
        y_fox = _softmax_attention_flat(fox_q, fox_k, fox_v, t=ta, chunk_shift=0,
                                        name="fox_attention")
        y_mla = _softmax_attention_flat(mq, mk, mv, t=ta, chunk_shift=CHUNK_SHIFT,
                                        name="mla_attention")
        y_sb = _sb_attention_flat(sb_qkv, t=tsb)
        x = _merge(x, y_fox, y_mla, y_sb, gates, mod[l], w_o_fox[l].astype(BF16),
                   w_o_mla[l].astype(BF16), w_o_sb[l].astype(BF16), w_out[l].astype(BF16), tm=tm)
        x = _ffn(x, mod[l], g_ffn[l].reshape(1, d), w_ffn_gate[l].astype(BF16),
                 w_ffn_up[l].astype(BF16), w_ffn_down[l].astype(BF16), g_final.reshape(1, d),
                 tm=tm, final=(l == depth - 1))
    return x
```

```python
import functools

import numpy as np
import jax
import jax.numpy as jnp
from jax import lax
from jax.experimental import pallas as pl
from jax.experimental.pallas import tpu as pltpu

F32 = jnp.float32
BF16 = jnp.bfloat16

D_MODEL = 1024
EPS = 1e-6
HEAD_DIM = 64
N_HEADS = 8
HEAD_W = N_HEADS * HEAD_DIM
CHUNK_SHIFT = 6
MLA_Q_RANK = 384
MLA_KV_RANK = 256
MLA_NOPE = 64
MLA_ROPE = 32
ROPE_HALF = MLA_ROPE // 2
ROPE_BASE = 10000.0
MLA_SCALE = (MLA_NOPE + MLA_ROPE) ** -0.5
QK_SCALE = HEAD_DIM ** -0.5
D_FF = 2816
LANES = 128
SLOT_W = N_HEADS * LANES
N_PIECES = 3

C_FQ = 0
C_FK = 1024
C_FV = 2048
C_SB = 2560
C_GATE = 4096
C_QL = 7168
C_KVL = 7552
C_KR = 7808
C_FF = 7936
IN_W = 8064

ROW_CHUNK = 64
SIGN_BIT = np.uint32(0x80000000)
NEG = -1e30
VMEM_LIMIT = 56 * 1024 * 1024


def _params(n_grid):
    return pltpu.CompilerParams(dimension_semantics=("arbitrary",) * n_grid,
                                vmem_limit_bytes=VMEM_LIMIT)


def _resident(shape):
    nd = len(shape)
    return pl.BlockSpec(shape, lambda *_: (0,) * nd, pipeline_mode=pl.Buffered(1))


def _bdot(a, b):
    return jnp.dot(a, b, preferred_element_type=F32)


def _dot_nt(a, b):
    return lax.dot_general(a, b, (((1,), (1,)), ((), ())), preferred_element_type=F32)


def _dot_tn(a, b):
    return lax.dot_general(a, b, (((0,), (0,)), ((), ())), preferred_element_type=F32)


def _split3(v):
    hi = v.astype(BF16)
    r = v - hi.astype(F32)
    mid = r.astype(BF16)
    lo = (r - mid.astype(F32)).astype(BF16)
    return hi, mid, lo


def _log_sigmoid(z):
    return jnp.minimum(z, 0.0) - jnp.log1p(jnp.exp(-jnp.abs(z)))


def _sigmoid(z):
    return 1.0 / (1.0 + jnp.exp(-z))


def _rms(x):
    return x * lax.rsqrt(jnp.mean(x * x, axis=-1, keepdims=True) + EPS)


def _ada_kernel(c_ref, w_ref, b_ref, o_ref):
    c = c_ref[...]
    cond = c * _sigmoid(c)
    chi, cmid, clo = _split3(cond)
    whi, wmid, wlo = _split3(w_ref[0])
    acc = _bdot(chi, whi) + (_bdot(chi, wmid) + _bdot(cmid, whi))
    acc = acc + (_bdot(chi, wlo) + _bdot(clo, whi) + _bdot(cmid, wmid))
    o_ref[0] = acc + b_ref[0]


def _ada(c, w_ada, b_ada):
    depth, d, n = w_ada.shape
    b = c.shape[0]
    tn = 1024
    return pl.pallas_call(
        _ada_kernel,
        grid=(depth, n // tn),
        in_specs=[pl.BlockSpec((b, d), lambda l, j: (0, 0)),
                  pl.BlockSpec((1, d, tn), lambda l, j: (l, 0, j)),
                  pl.BlockSpec((1, 1, tn), lambda l, j: (l, 0, j))],
        out_specs=pl.BlockSpec((1, b, tn), lambda l, j: (l, 0, j)),
        out_shape=jax.ShapeDtypeStruct((depth, b, n), F32),
        compiler_params=_params(2),
        name="adaln",
    )(c, w_ada, b_ada.reshape(depth, 1, n))


def _rope_kernel(pos_ref, inv_ref, cos_ref, sina_ref, sinb_ref):
    ang = pos_ref[0].astype(F32) * inv_ref[...]
    lane = lax.broadcasted_iota(jnp.int32, (1, LANES), 1)
    sin = jnp.sin(ang)
    cos_ref[0] = jnp.cos(ang)
    lo, mid, hi = MLA_NOPE, MLA_NOPE + ROPE_HALF, MLA_NOPE + MLA_ROPE
    sina_ref[0] = jnp.where((lane >= lo) & (lane < mid), -sin, 0.0)
    sinb_ref[0] = jnp.where((lane >= mid) & (lane < hi), sin, 0.0)


def _rope_tables(positions):
    b, s = positions.shape
    ts = min(s, 512)
    inv = ROPE_BASE ** (-jnp.arange(ROPE_HALF, dtype=F32) / ROPE_HALF)
    inv_slot = jnp.concatenate([jnp.zeros((MLA_NOPE,), F32), inv, inv,
                                jnp.zeros((LANES - MLA_NOPE - MLA_ROPE,), F32)]).reshape(1, LANES)
    spec = pl.BlockSpec((1, ts, LANES), lambda i, j: (i, j, 0))
    return pl.pallas_call(
        _rope_kernel,
        grid=(b, s // ts),
        in_specs=[pl.BlockSpec((1, ts, 1), lambda i, j: (i, j, 0)),
                  pl.BlockSpec((1, LANES), lambda i, j: (0, 0))],
        out_specs=[spec, spec, spec],
        out_shape=[jax.ShapeDtypeStruct((b, s, LANES), F32)] * 3,
        compiler_params=_params(2),
        name="rope_tables",
    )(positions.reshape(b, s, 1), inv_slot)


def _rope_slot(x, cos, sina, sinb):
    return (x * cos + pltpu.roll(x, LANES - ROPE_HALF, 1) * sina
            + pltpu.roll(x, ROPE_HALF, 1) * sinb)


def _inproj_kernel(x_ref, mod_ref, g_ref, w_ref, bff_ref, gq_ref, gkv_ref, wuq_ref, wkn_ref,
                   wkv_ref, cos_ref, sina_ref, sinb_ref, place_ref, qc_ref, kc_ref,
                   fq_ref, fk_ref, fv_ref, sb_ref, gate_ref, mq_ref, mk_ref, mv_ref,
                   carry_ref, *, tm):
    x = x_ref[0]
    sh, sc = mod_ref[0, 0:1, :], mod_ref[0, 1:2, :]
    u = ((_rms(x) * g_ref[...]) * (1.0 + sc) + sh).astype(BF16)

    def seg(a, b):
        return _bdot(u, w_ref[:, a:b])

    @pl.when(pl.program_id(1) == 0)
    def _():
        carry_ref[...] = jnp.zeros_like(carry_ref)

    logf = _log_sigmoid(seg(C_FF, IN_W) + bff_ref[...])
    r = lax.broadcasted_iota(jnp.int32, (tm, tm), 0)
    c = lax.broadcasted_iota(jnp.int32, (tm, tm), 1)
    tri = jnp.where(r >= c, 1.0, 0.0).astype(BF16)
    hi, mid, lo = _split3(logf)
    cum = (_bdot(tri, hi) + _bdot(tri, mid) + _bdot(tri, lo)) + carry_ref[...]
    carry_ref[...] = cum[tm - 1:tm, :]
    p0, p1, p2 = (p.astype(F32) for p in _split3(cum))
    lane = lax.broadcasted_iota(jnp.int32, (1, LANES), 1)
    pieces = jnp.where(lane < N_HEADS, p0,
                       jnp.where(lane < 2 * N_HEADS, pltpu.roll(p1, N_HEADS, 1),
                                 jnp.where(lane < 3 * N_HEADS, pltpu.roll(p2, 2 * N_HEADS, 1),
                                           0.0))).astype(BF16)

    for c in range(2):
        a = 512 * c
        placed = _bdot(pieces, place_ref[:, a:a + 512])
        q = seg(C_FQ + a, C_FQ + a + 512) * QK_SCALE + placed + qc_ref[:, a:a + 512]
        fq_ref[0, :, a:a + 512] = q.astype(BF16)
        placed = _bdot(pieces, place_ref[:, SLOT_W + a:SLOT_W + a + 512])
        k = seg(C_FK + a, C_FK + a + 512) + placed + kc_ref[:, a:a + 512]
        fk_ref[0, :, a:a + 512] = k.astype(BF16)
    fv_ref[0] = seg(C_FV, C_SB).astype(BF16)

    for c in range(3):
        a = 512 * c
        sq = seg(C_SB + a, C_SB + a + 512)
        if c == 0:
            sq = sq * QK_SCALE
        sb_ref[0, :, a:a + 512] = sq.astype(BF16)
    for c in range(6):
        a = 512 * c
        gate_ref[0, :, a:a + 512] = _sigmoid(seg(C_GATE + a, C_GATE + a + 512)).astype(BF16)

    cos, sina, sinb = cos_ref[0], sina_ref[0], sinb_ref[0]
    cq = (_rms(seg(C_QL, C_KVL)) * gq_ref[...]).astype(BF16)
    ckv = (_rms(seg(C_KVL, C_KR)) * gkv_ref[...]).astype(BF16)
    kr = _rope_slot(seg(C_KR, C_FF), cos, sina, sinb)
    for h in range(N_HEADS):
        a = LANES * h
        q = _rope_slot(_bdot(cq, wuq_ref[:, a:a + LANES]), cos, sina, sinb)
        mq_ref[0, :, a:a + LANES] = (q * MLA_SCALE).astype(BF16)
        mk_ref[0, :, a:a + LANES] = (_bdot(ckv, wkn_ref[:, a:a + LANES]) + kr).astype(BF16)
    mv_ref[0] = _bdot(ckv, wkv_ref[...]).astype(BF16)


def _fox_placement():
    place = np.zeros((LANES, 2 * SLOT_W), np.float32)
    qc = np.zeros((1, SLOT_W), np.float32)
    kc = np.zeros((1, SLOT_W), np.float32)
    for h in range(N_HEADS):
        for p in range(N_PIECES):
            place[N_HEADS * p + h, LANES * h + HEAD_DIM + N_PIECES + p] = 1.0
            place[N_HEADS * p + h, SLOT_W + LANES * h + HEAD_DIM + p] = 1.0
            qc[0, LANES * h + HEAD_DIM + p] = -1.0
            kc[0, LANES * h + HEAD_DIM + N_PIECES + p] = 1.0
    return jnp.asarray(place, BF16), jnp.asarray(qc), jnp.asarray(kc)


def _inproj(x, mod, g_mix, w_in_r, bff, g_q, g_kv, wuq_r, wkn_r, wkv_r, tables, *, tm):
    b, s, d = x.shape
    cos, sina, sinb = tables
    place, qc, kc = _fox_placement()
    row = lambda w: pl.BlockSpec((1, tm, w), lambda i, j: (i, j, 0))
    widths = [(SLOT_W, BF16),
              (SLOT_W, BF16),
              (HEAD_W, BF16),
              (3 * HEAD_W, BF16),
              (3 * D_MODEL, BF16),
              (SLOT_W, BF16),
              (SLOT_W, BF16),
              (HEAD_W, BF16)]
    return pl.pallas_call(
        functools.partial(_inproj_kernel, tm=tm),
        grid=(b, s // tm),
        in_specs=[row(d),
                  pl.BlockSpec((1, 6, d), lambda i, j: (i, 0, 0)),
                  _resident((1, d)), _resident((d, IN_W)), _resident((1, LANES)),
                  _resident((1, MLA_Q_RANK)), _resident((1, MLA_KV_RANK)),
                  _resident((MLA_Q_RANK, SLOT_W)), _resident((MLA_KV_RANK, SLOT_W)),
                  _resident((MLA_KV_RANK, HEAD_W)),
                  row(LANES), row(LANES), row(LANES),
                  _resident((LANES, 2 * SLOT_W)), _resident((1, SLOT_W)), _resident((1, SLOT_W))],
        out_specs=[row(w) for w, _ in widths],
        out_shape=[jax.ShapeDtypeStruct((b, s, w), dt) for w, dt in widths],
        scratch_shapes=[pltpu.VMEM((1, LANES), F32)],
        compiler_params=_params(2),
        name="inproj",
    )(x, mod, g_mix, w_in_r, bff, g_q, g_kv, wuq_r, wkn_r, wkv_r, cos, sina, sinb, place, qc, kc)


def _head_masks():
    lane = lax.broadcasted_iota(jnp.int32, (1, LANES), 1)
    return [(lane >= HEAD_DIM * hh) & (lane < HEAD_DIM * (hh + 1)) for hh in range(2)]


def _col_reduce(x, op, final, chunk=64):
    rows = x.shape[0]
    if rows > chunk and rows % chunk == 0:
        acc = x[:chunk]
        for a in range(chunk, rows, chunk):
            acc = op(acc, x[a:a + chunk])
        x, rows = acc, chunk
    while rows > 8 and rows % 16 == 0:
        rows //= 2
        x = op(x[:rows], x[rows:])
    return final(x, axis=0, keepdims=True)


def _colmax(x):
    return _col_reduce(x, jnp.maximum, jnp.max)


def _colsum(x):
    return _col_reduce(x, jnp.add, jnp.sum)


def _pipeline(n, stages):
    depth = len(stages)

    def static_step(t):
        for s, stage in enumerate(stages):
            if 0 <= t - s < n:
                stage(t - s, (t - s) % 2)

    first_full = depth - 1
    n_full = max(n - depth + 1, 0)
    for t in range(min(first_full, n + depth - 1)):
        static_step(t)
    if n_full >= 2:
        def body(pair, carry):
            t = first_full + 2 * pair
            for u in range(2):
                for s, stage in enumerate(stages):
                    stage(t + u - s, (first_full + u - s) % 2)
            return carry

        lax.fori_loop(0, n_full // 2, body, 0)
    for t in range(first_full + 2 * (n_full // 2), n + depth - 1):
        static_step(t)


def _block_tables(n_tiles, t, descending):
    diag = [(i * t, i * t) for i in range(n_tiles)]
    below = []
    for i in range(1, n_tiles):
        js = range(i - 1, -1, -1) if descending else range(i)
        below += [(i * t, j * t) for j in js]
    return jnp.asarray(np.array(diag + below, np.int32).T), len(diag), len(below)


def _softmax_flat_kernel(tab_ref, q_ref, k_ref, v_ref, o_ref, acc_ref, m_ref, al_ref, top_ref,
                         s0_ref, s1_ref, p0_ref, p1_ref, *, t, n_diag, n_below, chunk_shift):
    in_head = _head_masks()
    s_bufs, p_bufs = (s0_ref, s1_ref), (p0_ref, p1_ref)
    kio = lax.broadcasted_iota(jnp.int32, (t, 1), 0)
    qio = lax.broadcasted_iota(jnp.int32, (1, t), 1)
    acc_ref[...] = jnp.zeros_like(acc_ref)
    m_ref[...] = jnp.full_like(m_ref, NEG)

    def stages(base, diag):
        def item(i):
            return (pl.multiple_of(tab_ref[0, base + i], t), pl.multiple_of(tab_ref[1, base + i], t))

        def scores(i, par):
            q0, k0 = item(i)
            for hh in range(2):
                lanes = slice(LANES * hh, LANES * (hh + 1))
                s = _dot_nt(k_ref[0, pl.ds(k0, t), lanes],
                            q_ref[0, pl.ds(q0, t), lanes])
                if diag:
                    s = jnp.where((kio >> chunk_shift) <= (qio >> chunk_shift), s, NEG)
                s_bufs[par][hh] = s
                top_ref[par, hh] = _colmax(s)

        def numerators(i, par):
            q0, _ = item(i)
            for hh in range(2):
                m_old = m_ref[hh, :, pl.ds(q0, t)]
                m_new = jnp.maximum(m_old, top_ref[par, hh])
                for a in range(0, t, ROW_CHUNK):
                    rows = slice(a, a + ROW_CHUNK)
                    p_bufs[par][hh, rows, :] = jnp.exp(s_bufs[par][hh, rows, :] - m_new).astype(BF16)
                al_ref[par, hh] = jnp.exp(m_old - m_new)
                m_ref[hh, :, pl.ds(q0, t)] = m_new

        def values(i, par):
            q0, k0 = item(i)
            v = v_ref[0, pl.ds(k0, t), :]
            for hh in range(2):
                v1 = jnp.where(in_head[hh], v, jnp.ones_like(v))
                acc_ref[hh, :, pl.ds(q0, t)] = (acc_ref[hh, :, pl.ds(q0, t)] * al_ref[par, hh]
                                                + _dot_tn(v1, p_bufs[par][hh]))

        return [scores, numerators, values]

    _pipeline(n_diag, stages(0, True))
    _pipeline(n_below, stages(n_diag, False))

    sub = lax.broadcasted_iota(jnp.int32, (LANES, 1), 0)

    def finish(tile, carry):
        q0 = pl.multiple_of(tile * t, t)
        outs = []
        for hh in range(2):
            acc = acc_ref[hh, :, pl.ds(q0, t)]
            denom = acc[HEAD_DIM * (1 - hh):HEAD_DIM * (1 - hh) + 1, :]
            outs.append(acc / denom)
        o_ref[0, pl.ds(q0, t), :] = jnp.where(sub < HEAD_DIM, outs[0], outs[1]).T.astype(o_ref.dtype)
        return carry

    lax.fori_loop(0, n_diag, finish, 0)


def _softmax_attention_flat(q, k, v, *, t, chunk_shift, name):
    b, s, _ = q.shape
    table, n_diag, n_below = _block_tables(s // t, t, descending=False)
    grid_spec = pltpu.PrefetchScalarGridSpec(
        num_scalar_prefetch=1,
        grid=(b, N_HEADS // 2),
        in_specs=[pl.BlockSpec((1, s, 2 * LANES), lambda i, h, tab: (i, 0, h)),
                  pl.BlockSpec((1, s, 2 * LANES), lambda i, h, tab: (i, 0, h)),
                  pl.BlockSpec((1, s, LANES), lambda i, h, tab: (i, 0, h))],
        out_specs=pl.BlockSpec((1, s, LANES), lambda i, h, tab: (i, 0, h)),
        scratch_shapes=[pltpu.VMEM((2, LANES, s), F32),
                        pltpu.VMEM((2, 1, s), F32),
                        pltpu.VMEM((2, 2, 1, t), F32),
                        pltpu.VMEM((2, 2, 1, t), F32),
                        pltpu.VMEM((2, t, t), F32), pltpu.VMEM((2, t, t), F32),
                        pltpu.VMEM((2, t, t), BF16), pltpu.VMEM((2, t, t), BF16)])
    return pl.pallas_call(
        functools.partial(_softmax_flat_kernel, t=t, n_diag=n_diag, n_below=n_below,
                          chunk_shift=chunk_shift),
        grid_spec=grid_spec,
        out_shape=jax.ShapeDtypeStruct((b, s, HEAD_W), BF16),
        compiler_params=_params(2),
        name=name,
    )(table, q, k, v)


def _sb_flat_kernel(tab_ref, q_ref, k_ref, v_ref, o_ref, acc_ref, r_ref, tot_ref, d_ref,
                    z0_ref, z1_ref, e0_ref, e1_ref, w0_ref, w1_ref, *, t, n_diag, n_below):
    in_head = _head_masks()
    z_bufs, e_bufs, w_bufs = (z0_ref, z1_ref), (e0_ref, e1_ref), (w0_ref, w1_ref)
    kio = lax.broadcasted_iota(jnp.int32, (t, 1), 0)
    qio = lax.broadcasted_iota(jnp.int32, (1, t), 1)
    ss = lax.broadcasted_iota(jnp.int32, (t, t), 0)
    jj = lax.broadcasted_iota(jnp.int32, (t, t), 1)
    from_here = jnp.where(jj >= ss, 1.0, 0.0).astype(BF16)
    acc_ref[...] = jnp.zeros_like(acc_ref)
    r_ref[...] = jnp.zeros_like(r_ref)

    def stages(base, diag):
        def item(i):
            return (pl.multiple_of(tab_ref[0, base + i], t), pl.multiple_of(tab_ref[1, base + i], t))

        def scores(i, par):
            q0, k0 = item(i)
            k = k_ref[0, pl.ds(k0, t), :]
            q = q_ref[0, pl.ds(q0, t), :]
            for hh in range(2):
                qh = jnp.where(in_head[hh], q, jnp.zeros_like(q))
                z_bufs[par][hh] = _dot_nt(k, qh)

        def drops(i, par):
            for hh in range(2):
                for a in range(0, t, ROW_CHUNK):
                    rows = slice(a, a + ROW_CHUNK)
                    z = z_bufs[par][hh, rows, :]
                    neg_abs = pltpu.bitcast(pltpu.bitcast(z, jnp.uint32) | SIGN_BIT, F32)
                    drop = jnp.maximum(z, 0.0) + jnp.log(1.0 + jnp.exp(neg_abs))
                    if diag:
                        drop = jnp.where(kio[rows] < qio, drop, 0.0)
                    d_ref[hh, rows, :] = drop.astype(BF16)
                incl = _bdot(from_here, d_ref[hh])
                for a in range(0, t, ROW_CHUNK):
                    rows = slice(a, a + ROW_CHUNK)
                    e = z_bufs[par][hh, rows, :] - incl[rows]
                    if diag:
                        e = jnp.where(kio[rows] < qio, e, NEG)
                    e_bufs[par][hh, rows, :] = e
                tot_ref[par, hh] = incl[0:1, :]

        def weights(i, par):
            q0, _ = item(i)
            for hh in range(2):
                r_old = r_ref[hh, :, pl.ds(q0, t)]
                for a in range(0, t, ROW_CHUNK):
                    rows = slice(a, a + ROW_CHUNK)
                    w_bufs[par][hh, rows, :] = jnp.exp(e_bufs[par][hh, rows, :] - r_old).astype(BF16)
                r_ref[hh, :, pl.ds(q0, t)] = r_old + tot_ref[par, hh]

        def values(i, par):
            q0, k0 = item(i)
            v = v_ref[0, pl.ds(k0, t), :]
            upd = None
            for hh in range(2):
                vh = jnp.where(in_head[hh], v, jnp.zeros_like(v))
                pv = _dot_tn(vh, w_bufs[par][hh])
                upd = pv if upd is None else upd + pv
            acc_ref[:, pl.ds(q0, t)] = acc_ref[:, pl.ds(q0, t)] + upd

        return [scores, drops, weights, values]

    _pipeline(n_diag, stages(0, True))
    _pipeline(n_below, stages(n_diag, False))

    def finish(tile, carry):
        q0 = pl.multiple_of(tile * t, t)
        o_ref[0, pl.ds(q0, t), :] = acc_ref[:, pl.ds(q0, t)].T.astype(o_ref.dtype)
        return carry

    lax.fori_loop(0, n_diag, finish, 0)


def _sb_attention_flat(qkv, *, t):
    b, s, _ = qkv.shape
    table, n_diag, n_below = _block_tables(s // t, t, descending=True)
    grid_spec = pltpu.PrefetchScalarGridSpec(
        num_scalar_prefetch=1,
        grid=(b, N_HEADS // 2),
        in_specs=[pl.BlockSpec((1, s, LANES), lambda i, h, tab: (i, 0, h)),
                  pl.BlockSpec((1, s, LANES), lambda i, h, tab: (i, 0, 4 + h)),
                  pl.BlockSpec((1, s, LANES), lambda i, h, tab: (i, 0, 8 + h))],
        out_specs=pl.BlockSpec((1, s, LANES), lambda i, h, tab: (i, 0, h)),
        scratch_shapes=[pltpu.VMEM((LANES, s), F32),
                        pltpu.VMEM((2, 1, s), F32),
                        pltpu.VMEM((2, 2, 1, t), F32),
                        pltpu.VMEM((2, t, t), BF16),
                        pltpu.VMEM((2, t, t), F32), pltpu.VMEM((2, t, t), F32),
                        pltpu.VMEM((2, t, t), F32), pltpu.VMEM((2, t, t), F32),
                        pltpu.VMEM((2, t, t), BF16), pltpu.VMEM((2, t, t), BF16)])
    return pl.pallas_call(
        functools.partial(_sb_flat_kernel, t=t, n_diag=n_diag, n_below=n_below),
        grid_spec=grid_spec,
        out_shape=jax.ShapeDtypeStruct((b, s, HEAD_W), BF16),
        compiler_params=_params(2),
        name="sb_attention",
    )(table, qkv, qkv, qkv)


def _softmax_attn_kernel(q_ref, k_ref, v_ref, o_ref, acc_ref, s0_ref, s1_ref, *, tq, tk,
                         chunk_shift):
    qi = pl.program_id(2)
    in_head = _head_masks()
    qs = [q_ref[0, :, LANES * hh:LANES * (hh + 1)] for hh in range(2)]
    qpos = qi * tq + lax.broadcasted_iota(jnp.int32, (1, tq), 1)
    acc_ref[...] = jnp.zeros_like(acc_ref)

    def produce(j, buf):
        start = pl.multiple_of(j * tk, tk)
        for hh in range(2):
            k = k_ref[0, pl.ds(start, tk), LANES * hh:LANES * (hh + 1)]
            buf[hh] = _dot_nt(k, qs[hh])

    def consume(j, buf, m, masked):
        start = pl.multiple_of(j * tk, tk)
        v = v_ref[0, pl.ds(start, tk), :]
        m_out = []
        for hh in range(2):
            s = buf[hh]
            if masked:
                kpos = start + lax.broadcasted_iota(jnp.int32, (tk, 1), 0)
                s = jnp.where((kpos >> chunk_shift) <= (qpos >> chunk_shift), s, NEG)
            m_new = jnp.maximum(m[hh], _colmax(s))
            p = jnp.exp(s - m_new)
            alpha = jnp.exp(m[hh] - m_new)
            v1 = jnp.where(in_head[hh], v, jnp.ones_like(v))
            acc_ref[hh] = acc_ref[hh] * alpha + _dot_tn(v1, p.astype(BF16))
            m_out.append(m_new)
        return tuple(m_out)

    n_last = (qi * tq + tq - 1) // tk
    n_pairs = n_last // 2

    def body(i, m):
        j = 2 * i
        produce(j + 1, s1_ref)
        m = consume(j, s0_ref, m, False)
        produce(j + 2, s0_ref)
        return consume(j + 1, s1_ref, m, False)

    produce(0, s0_ref)
    m = (jnp.full((1, tq), NEG, F32),) * 2
    m = lax.fori_loop(0, n_pairs, body, m)
    j = 2 * n_pairs

    @pl.when(n_last % 2 == 1)
    def _():
        produce(j + 1, s1_ref)
        consume(j + 1, s1_ref, consume(j, s0_ref, m, False), True)

    @pl.when(n_last % 2 == 0)
    def _():
        consume(j, s0_ref, m, True)

    sub = lax.broadcasted_iota(jnp.int32, (LANES, 1), 0)
    outs = []
    for hh in range(2):
        acc = acc_ref[hh]
        denom = acc[HEAD_DIM * (1 - hh):HEAD_DIM * (1 - hh) + 1, :]
        outs.append(acc / denom)
    o_ref[0] = jnp.where(sub < HEAD_DIM, outs[0], outs[1]).T.astype(o_ref.dtype)


def _softmax_attention(q, k, v, *, tq, tk, chunk_shift, name):
    b, s, _ = q.shape
    return pl.pallas_call(
        functools.partial(_softmax_attn_kernel, tq=tq, tk=tk, chunk_shift=chunk_shift),
        grid=(b, N_HEADS // 2, s // tq),
        in_specs=[pl.BlockSpec((1, tq, 2 * LANES), lambda i, h, j: (i, j, h)),
                  pl.BlockSpec((1, s, 2 * LANES), lambda i, h, j: (i, 0, h)),
                  pl.BlockSpec((1, s, LANES), lambda i, h, j: (i, 0, h))],
        out_specs=pl.BlockSpec((1, tq, LANES), lambda i, h, j: (i, j, h)),
        out_shape=jax.ShapeDtypeStruct((b, s, HEAD_W), BF16),
        scratch_shapes=[pltpu.VMEM((2, LANES, tq), F32), pltpu.VMEM((2, tk, tq), F32),
                        pltpu.VMEM((2, tk, tq), F32)],
        compiler_params=_params(3),
        name=name,
    )(q, k, v)


def _sb_kernel(q_ref, k_ref, v_ref, o_ref, acc_ref, z_ref, r_ref, *, t):
    qi = pl.program_id(2)
    in_head = _head_masks()
    q = q_ref[0]
    qs = [jnp.where(in_head[hh], q, jnp.zeros_like(q)) for hh in range(2)]
    qpos = lax.broadcasted_iota(jnp.int32, (1, t), 1)
    kpos = lax.broadcasted_iota(jnp.int32, (t, 1), 0)
    ss = lax.broadcasted_iota(jnp.int32, (t, 2 * t), 0)
    jj = lax.broadcasted_iota(jnp.int32, (t, 2 * t), 1) & (t - 1)
    from_here = jnp.where(jj >= ss, 1.0, 0.0).astype(BF16)
    acc_ref[...] = jnp.zeros_like(acc_ref)
    r_ref[...] = jnp.zeros_like(r_ref)

    def key_start(blk):
        return pl.multiple_of(jnp.maximum(blk, 0) * t, t)

    def scores(blk_hi, nblk):
        for u in range(nblk):
            k = k_ref[0, pl.ds(key_start(blk_hi - u), t), :]
            for hh in range(2):
                z_ref[u, hh] = _dot_nt(k, qs[hh])

    def group(blk_hi, nblk, diag_first, next_hi):
        chains = []
        for u in range(nblk):
            for hh in range(2):
                z = z_ref[u, hh]
                drop = jnp.maximum(z, 0.0) + jnp.log(1.0 + jnp.exp(-jnp.abs(z)))
                if diag_first and u == 0:
                    ok = kpos < qpos
                    drop = jnp.where(ok, drop, 0.0)
                hi = drop.astype(BF16)
                lo = (drop - hi.astype(F32)).astype(BF16)
                incl = _bdot(from_here, jnp.concatenate([hi, lo], axis=0))
                e = z - incl
                if diag_first and u == 0:
                    e = jnp.where(ok, e, NEG)
                chains.append((u, hh, e, incl[0:1, :]))
        scores(next_hi, 2)
        r = [r_ref[hh] for hh in range(2)]
        upd = None
        for u, hh, e, total in chains:
            w = jnp.exp(e - r[hh])
            v = v_ref[0, pl.ds(key_start(blk_hi - u), t), :]
            vh = jnp.where(in_head[hh], v, jnp.zeros_like(v))
            pv = _dot_tn(vh, w.astype(BF16))
            upd = pv if upd is None else upd + pv
            r[hh] = r[hh] + total
        acc_ref[...] = acc_ref[...] + upd
        for hh in range(2):
            r_ref[hh] = r[hh]

    @pl.when(qi % 2 == 0)
    def _():
        scores(qi, 1)
        group(qi, 1, True, qi - 1)

    @pl.when(qi % 2 == 1)
    def _():
        scores(qi, 2)
        group(qi, 2, True, qi - 2)

    n_rest = qi // 2
    top = 2 * n_rest - 1

    def body(i, carry):
        group(top - 2 * i, 2, False, top - 2 * i - 2)
        return carry

    lax.fori_loop(0, n_rest, body, 0)
    o_ref[0] = acc_ref[...].T.astype(o_ref.dtype)


def _sb_attention(qkv, *, t):
    b, s, _ = qkv.shape
    return pl.pallas_call(
        functools.partial(_sb_kernel, t=t),
        grid=(b, N_HEADS // 2, s // t),
        in_specs=[pl.BlockSpec((1, t, LANES), lambda i, h, j: (i, j, h)),
                  pl.BlockSpec((1, s, LANES), lambda i, h, j: (i, 0, 4 + h)),
                  pl.BlockSpec((1, s, LANES), lambda i, h, j: (i, 0, 8 + h))],
        out_specs=pl.BlockSpec((1, t, LANES), lambda i, h, j: (i, j, h)),
        out_shape=jax.ShapeDtypeStruct((b, s, HEAD_W), BF16),
        scratch_shapes=[pltpu.VMEM((LANES, t), F32), pltpu.VMEM((2, 2, t, t), F32),
                        pltpu.VMEM((2, 1, t), F32)],
        compiler_params=_params(3),
        name="sb_attention",
    )(qkv, qkv, qkv)


def _merge_kernel(x_ref, yf_ref, ym_ref, ys_ref, g_ref, mod_ref, wf_ref, wm_ref, ws_ref,
                  wo_ref, o_ref):
    d = D_MODEL
    merged = (g_ref[0, :, 0:d].astype(F32) * _bdot(yf_ref[0], wf_ref[...])
              + g_ref[0, :, d:2 * d].astype(F32) * _bdot(ym_ref[0], wm_ref[...])
              + g_ref[0, :, 2 * d:3 * d].astype(F32) * _bdot(ys_ref[0], ws_ref[...]))
    o_ref[0] = x_ref[0] + mod_ref[0, 2:3, :] * _bdot(merged.astype(BF16), wo_ref[...])


def _merge(x, y_fox, y_mla, y_sb, gates, mod, wf, wm, ws, wo, *, tm):
    b, s, d = x.shape
    row = lambda w: pl.BlockSpec((1, tm, w), lambda i, j: (i, j, 0))
    return pl.pallas_call(
        _merge_kernel,
        grid=(b, s // tm),
        in_specs=[row(d), row(HEAD_W), row(HEAD_W), row(HEAD_W), row(3 * d),
                  pl.BlockSpec((1, 6, d), lambda i, j: (i, 0, 0)),
                  _resident((HEAD_W, d)), _resident((HEAD_W, d)), _resident((HEAD_W, d)),
                  _resident((d, d))],
        out_specs=row(d),
        out_shape=jax.ShapeDtypeStruct((b, s, d), F32),
        compiler_params=_params(2),
        name="merge_out",
    )(x, y_fox, y_mla, y_sb, gates, mod, wf, wm, ws, wo)


def _ffn_kernel(x_ref, mod_ref, g_ref, wg_ref, wu_ref, wd_ref, gfin_ref, o_ref, *, final, fc):
    x = x_ref[0]
    sh, sc, gt = mod_ref[0, 3:4, :], mod_ref[0, 4:5, :], mod_ref[0, 5:6, :]
    u = ((_rms(x) * g_ref[...]) * (1.0 + sc) + sh).astype(BF16)
    acc = None
    for a in range(0, D_FF, fc):
        hg = _bdot(u, wg_ref[:, a:a + fc])
        hu = _bdot(u, wu_ref[:, a:a + fc])
        h = ((hg * _sigmoid(hg)) * hu).astype(BF16)
        part = _bdot(h, wd_ref[a:a + fc, :])
        acc = part if acc is None else acc + part
    out = x + gt * acc
    if final:
        out = _rms(out) * gfin_ref[...]
    o_ref[0] = out


def _ffn(x, mod, g_ffn, wg, wu, wd, g_final, *, tm, final):
    b, s, d = x.shape
    row = lambda w: pl.BlockSpec((1, tm, w), lambda i, j: (i, j, 0))
    return pl.pallas_call(
        functools.partial(_ffn_kernel, final=final, fc=D_FF // 2),
        grid=(b, s // tm),
        in_specs=[row(d), pl.BlockSpec((1, 6, d), lambda i, j: (i, 0, 0)), _resident((1, d)),
                  _resident((d, D_FF)), _resident((d, D_FF)), _resident((D_FF, d)),
                  _resident((1, d))],
        out_specs=row(d),
        out_shape=jax.ShapeDtypeStruct((b, s, d), F32),
        compiler_params=_params(2),
        name="ffn",
    )(x, mod, g_ffn, wg, wu, wd, g_final)


def _slot_cols(w, width_per_head, used):
    k = w.shape[0]
    w = w.reshape(k, N_HEADS, width_per_head)[:, :, :used]
    return jnp.pad(w, ((0, 0), (0, 0), (0, LANES - used))).reshape(k, SLOT_W)


def _layout_w_in(w):
    d = w.shape[0]
    fox, ff, ql, kvl, kr, sb, gate = jnp.split(
        w, [1536, 1544, 1928, 2184, 2216, 3752], axis=-1)
    z = lambda n: jnp.zeros((d, n), w.dtype)
    fq, fk, fv = jnp.split(fox, 3, axis=-1)
    kr_slot = jnp.concatenate([z(MLA_NOPE), kr, z(LANES - MLA_NOPE - MLA_ROPE)], axis=-1)
    ff_slot = jnp.concatenate([ff, z(LANES - N_HEADS)], axis=-1)
    cols = [_slot_cols(fq, HEAD_DIM, HEAD_DIM), _slot_cols(fk, HEAD_DIM, HEAD_DIM), fv, sb, gate,
            ql, kvl, kr_slot, ff_slot]
    return jnp.concatenate(cols, axis=-1).astype(BF16)


def kernel(x, c, positions, g_mix, w_ada, b_ada, w_in, b_fox_f, g_mla_q, w_mla_uq, g_mla_kv,
           w_mla_ukv, w_o_fox, w_o_mla, w_o_sb, w_out, g_ffn, w_ffn_gate, w_ffn_up, w_ffn_down,
           g_final):
    b, s, d = x.shape
    depth = w_in.shape[0]
    tm = min(s, 512)
    ta = min(s, 512)
    tsb = min(s, 256)

    mod = _ada(c, w_ada, b_ada).reshape(depth, b, 6, d)
    tables = _rope_tables(positions)
    for l in range(depth):
        w_in_r = _layout_w_in(w_in[l])
        wuq_r = _slot_cols(w_mla_uq[l], MLA_NOPE + MLA_ROPE, MLA_NOPE + MLA_ROPE).astype(BF16)
        wkn_r = _slot_cols(w_mla_ukv[l], 2 * HEAD_DIM, MLA_NOPE).astype(BF16)
        wkv_r = w_mla_ukv[l].reshape(MLA_KV_RANK, N_HEADS, 2 * HEAD_DIM)[:, :, MLA_NOPE:]
        wkv_r = wkv_r.reshape(MLA_KV_RANK, HEAD_W).astype(BF16)
        bff = jnp.pad(b_fox_f[l], (0, LANES - N_HEADS)).reshape(1, LANES)

        fox_q, fox_k, fox_v, sb_qkv, gates, mq, mk, mv = _inproj(
            x, mod[l], g_mix[l].reshape(1, d), w_in_r, bff, g_mla_q[l].reshape(1, -1),
            g_mla_kv[l].reshape(1, -1), wuq_r, wkn_r, wkv_r, tables, tm=tm)
        y_fox = _softmax_attention_flat(fox_q, fox_k, fox_v, t=ta, chunk_shift=0,
                                        name="fox_attention")
        y_mla = _softmax_attention_flat(mq, mk, mv, t=ta, chunk_shift=CHUNK_SHIFT,
                                        name="mla_attention")
        y_sb = _sb_attention_flat(sb_qkv, t=tsb)
        x = _merge(x, y_fox, y_mla, y_sb, gates, mod[l], w_o_fox[l].astype(BF16),
                   w_o_mla[l].astype(BF16), w_o_sb[l].astype(BF16), w_out[l].astype(BF16), tm=tm)
        x = _ffn(x, mod[l], g_ffn[l].reshape(1, d), w_ffn_gate[l].astype(BF16),
                 w_ffn_up[l].astype(BF16), w_ffn_down[l].astype(BF16), g_final.reshape(1, d),
                 tm=tm, final=(l == depth - 1))
    return x
```

```python
import functools

import numpy as np
import jax
import jax.numpy as jnp
from jax import lax
from jax.experimental import pallas as pl
from jax.experimental.pallas import tpu as pltpu

F32 = jnp.float32
BF16 = jnp.bfloat16

D_MODEL = 1024
EPS = 1e-6
HEAD_DIM = 64
N_HEADS = 8
HEAD_W = N_HEADS * HEAD_DIM
CHUNK_SHIFT = 6
MLA_Q_RANK = 384
MLA_KV_RANK = 256
MLA_NOPE = 64
MLA_ROPE = 32
ROPE_HALF = MLA_ROPE // 2
ROPE_BASE = 10000.0
MLA_SCALE = (MLA_NOPE + MLA_ROPE) ** -0.5
QK_SCALE = HEAD_DIM ** -0.5
D_FF = 2816
LANES = 128
SLOT_W = N_HEADS * LANES
N_PIECES = 3

C_FQ = 0
C_FK = 1024
C_FV = 2048
C_SB = 2560
C_GATE = 4096
C_QL = 7168
C_KVL = 7552
C_KR = 7808
C_FF = 7936
IN_W = 8064

ROW_CHUNK = 64
SIGN_BIT = np.uint32(0x80000000)
NEG = -1e30
VMEM_LIMIT = 56 * 1024 * 1024


def _params(n_grid):
    return pltpu.CompilerParams(dimension_semantics=("arbitrary",) * n_grid,
                                vmem_limit_bytes=VMEM_LIMIT)


def _resident(shape):
    nd = len(shape)
    return pl.BlockSpec(shape, lambda *_: (0,) * nd, pipeline_mode=pl.Buffered(1))


def _bdot(a, b):
    return jnp.dot(a, b, preferred_element_type=F32)


def _dot_nt(a, b):
    return lax.dot_general(a, b, (((1,), (1,)), ((), ())), preferred_element_type=F32)


def _dot_tn(a, b):
    return lax.dot_general(a, b, (((0,), (0,)), ((), ())), preferred_element_type=F32)


def _split3(v):
    hi = v.astype(BF16)
    r = v - hi.astype(F32)
    mid = r.astype(BF16)
    lo = (r - mid.astype(F32)).astype(BF16)
    return hi, mid, lo


def _log_sigmoid(z):
    return jnp.minimum(z, 0.0) - jnp.log1p(jnp.exp(-jnp.abs(z)))


def _sigmoid(z):
    return 1.0 / (1.0 + jnp.exp(-z))


def _rms(x):
    return x * lax.rsqrt(jnp.mean(x * x, axis=-1, keepdims=True) + EPS)


def _ada_kernel(c_ref, w_ref, b_ref, o_ref):
    c = c_ref[...]
    cond = c * _sigmoid(c)
    chi, cmid, clo = _split3(cond)
    whi, wmid, wlo = _split3(w_ref[0])
    acc = _bdot(chi, whi) + (_bdot(chi, wmid) + _bdot(cmid, whi))
    acc = acc + (_bdot(chi, wlo) + _bdot(clo, whi) + _bdot(cmid, wmid))
    o_ref[0] = acc + b_ref[0]


def _ada(c, w_ada, b_ada):
    depth, d, n = w_ada.shape
    b = c.shape[0]
    tn = 1024
    return pl.pallas_call(
        _ada_kernel,
        grid=(depth, n // tn),
        in_specs=[pl.BlockSpec((b, d), lambda l, j: (0, 0)),
                  pl.BlockSpec((1, d, tn), lambda l, j: (l, 0, j)),
                  pl.BlockSpec((1, 1, tn), lambda l, j: (l, 0, j))],
        out_specs=pl.BlockSpec((1, b, tn), lambda l, j: (l, 0, j)),
        out_shape=jax.ShapeDtypeStruct((depth, b, n), F32),
        compiler_params=_params(2),
        name="adaln",
    )(c, w_ada, b_ada.reshape(depth, 1, n))


def _rope_kernel(pos_ref, inv_ref, cos_ref, sina_ref, sinb_ref):
    ang = pos_ref[0].astype(F32) * inv_ref[...]
    lane = lax.broadcasted_iota(jnp.int32, (1, LANES), 1)
    sin = jnp.sin(ang)
    cos_ref[0] = jnp.cos(ang)
    lo, mid, hi = MLA_NOPE, MLA_NOPE + ROPE_HALF, MLA_NOPE + MLA_ROPE
    sina_ref[0] = jnp.where((lane >= lo) & (lane < mid), -sin, 0.0)
    sinb_ref[0] = jnp.where((lane >= mid) & (lane < hi), sin, 0.0)


def _rope_tables(positions):
    b, s = positions.shape
    ts = min(s, 512)
    inv = ROPE_BASE ** (-jnp.arange(ROPE_HALF, dtype=F32) / ROPE_HALF)
    inv_slot = jnp.concatenate([jnp.zeros((MLA_NOPE,), F32), inv, inv,
                                jnp.zeros((LANES - MLA_NOPE - MLA_ROPE,), F32)]).reshape(1, LANES)
    spec = pl.BlockSpec((1, ts, LANES), lambda i, j: (i, j, 0))
    return pl.pallas_call(
        _rope_kernel,
        grid=(b, s // ts),
        in_specs=[pl.BlockSpec((1, ts, 1), lambda i, j: (i, j, 0)),
                  pl.BlockSpec((1, LANES), lambda i, j: (0, 0))],
        out_specs=[spec, spec, spec],
        out_shape=[jax.ShapeDtypeStruct((b, s, LANES), F32)] * 3,
        compiler_params=_params(2),
        name="rope_tables",
    )(positions.reshape(b, s, 1), inv_slot)


def _rope_slot(x, cos, sina, sinb):
    return (x * cos + pltpu.roll(x, LANES - ROPE_HALF, 1) * sina
            + pltpu.roll(x, ROPE_HALF, 1) * sinb)


def _inproj_kernel(x_ref, mod_ref, g_ref, w_ref, bff_ref, gq_ref, gkv_ref, wuq_ref, wkn_ref,
                   wkv_ref, cos_ref, sina_ref, sinb_ref, place_ref, qc_ref, kc_ref,
                   fq_ref, fk_ref, fv_ref, sb_ref, sv_ref, gate_ref, mq_ref, mk_ref, mv_ref,
                   carry_ref, *, tm):
    x = x_ref[0]
    sh, sc = mod_ref[0, 0:1, :], mod_ref[0, 1:2, :]
    u = ((_rms(x) * g_ref[...]) * (1.0 + sc) + sh).astype(BF16)

    def seg(a, b):
        return _bdot(u, w_ref[:, a:b])

    @pl.when(pl.program_id(1) == 0)
    def _():
        carry_ref[...] = jnp.zeros_like(carry_ref)

    logf = _log_sigmoid(seg(C_FF, IN_W) + bff_ref[...])
    r = lax.broadcasted_iota(jnp.int32, (tm, tm), 0)
    c = lax.broadcasted_iota(jnp.int32, (tm, tm), 1)
    tri = jnp.where(r >= c, 1.0, 0.0).astype(BF16)
    hi, mid, lo = _split3(logf)
    cum = (_bdot(tri, hi) + _bdot(tri, mid) + _bdot(tri, lo)) + carry_ref[...]
    carry_ref[...] = cum[tm - 1:tm, :]
    p0, p1, p2 = (p.astype(F32) for p in _split3(cum))
    lane = lax.broadcasted_iota(jnp.int32, (1, LANES), 1)
    pieces = jnp.where(lane < N_HEADS, p0,
                       jnp.where(lane < 2 * N_HEADS, pltpu.roll(p1, N_HEADS, 1),
                                 jnp.where(lane < 3 * N_HEADS, pltpu.roll(p2, 2 * N_HEADS, 1),
                                           0.0))).astype(BF16)

    for c in range(2):
        a = 512 * c
        placed = _bdot(pieces, place_ref[:, a:a + 512])
        q = seg(C_FQ + a, C_FQ + a + 512) * QK_SCALE + placed + qc_ref[:, a:a + 512]
        fq_ref[0, :, a:a + 512] = q.astype(BF16)
        placed = _bdot(pieces, place_ref[:, SLOT_W + a:SLOT_W + a + 512])
        k = seg(C_FK + a, C_FK + a + 512) + placed + kc_ref[:, a:a + 512]
        fk_ref[0, :, a:a + 512] = k.astype(BF16)
    fv_ref[0] = seg(C_FV, C_SB).T.astype(BF16)

    sb_ref[0, :, 0:512] = (seg(C_SB, C_SB + 512) * QK_SCALE).astype(BF16)
    sb_ref[0, :, 512:1024] = seg(C_SB + 512, C_SB + 1024).astype(BF16)
    sv_ref[0] = seg(C_SB + 1024, C_GATE).T.astype(BF16)
    for c in range(6):
        a = 512 * c
        gate_ref[0, :, a:a + 512] = _sigmoid(seg(C_GATE + a, C_GATE + a + 512)).astype(BF16)

    cos, sina, sinb = cos_ref[0], sina_ref[0], sinb_ref[0]
    cq = (_rms(seg(C_QL, C_KVL)) * gq_ref[...]).astype(BF16)
    ckv = (_rms(seg(C_KVL, C_KR)) * gkv_ref[...]).astype(BF16)
    kr = _rope_slot(seg(C_KR, C_FF), cos, sina, sinb)
    for h in range(N_HEADS):
        a = LANES * h
        q = _rope_slot(_bdot(cq, wuq_ref[:, a:a + LANES]), cos, sina, sinb)
        mq_ref[0, :, a:a + LANES] = (q * MLA_SCALE).astype(BF16)
        mk_ref[0, :, a:a + LANES] = (_bdot(ckv, wkn_ref[:, a:a + LANES]) + kr).astype(BF16)
    mv_ref[0] = _bdot(ckv, wkv_ref[...]).T.astype(BF16)


def _fox_placement():
    place = np.zeros((LANES, 2 * SLOT_W), np.float32)
    qc = np.zeros((1, SLOT_W), np.float32)
    kc = np.zeros((1, SLOT_W), np.float32)
    for h in range(N_HEADS):
        for p in range(N_PIECES):
            place[N_HEADS * p + h, LANES * h + HEAD_DIM + N_PIECES + p] = 1.0
            place[N_HEADS * p + h, SLOT_W + LANES * h + HEAD_DIM + p] = 1.0
            qc[0, LANES * h + HEAD_DIM + p] = -1.0
            kc[0, LANES * h + HEAD_DIM + N_PIECES + p] = 1.0
    return jnp.asarray(place, BF16), jnp.asarray(qc), jnp.asarray(kc)


def _inproj(x, mod, g_mix, w_in_r, bff, g_q, g_kv, wuq_r, wkn_r, wkv_r, tables, *, tm):
    b, s, d = x.shape
    cos, sina, sinb = tables
    place, qc, kc = _fox_placement()
    row = lambda w: pl.BlockSpec((1, tm, w), lambda i, j: (i, j, 0))
    value_t = None
    widths = [SLOT_W,
              SLOT_W,
              value_t,
              2 * HEAD_W,
              value_t,
              3 * D_MODEL,
              SLOT_W,
              SLOT_W,
              value_t]
    col = pl.BlockSpec((1, HEAD_W, tm), lambda i, j: (i, 0, j))
    out_specs = [col if w is value_t else row(w) for w in widths]
    out_shape = [jax.ShapeDtypeStruct((b, HEAD_W, s) if w is value_t else (b, s, w), BF16)
                 for w in widths]
    return pl.pallas_call(
        functools.partial(_inproj_kernel, tm=tm),
        grid=(b, s // tm),
        in_specs=[row(d),
                  pl.BlockSpec((1, 6, d), lambda i, j: (i, 0, 0)),
                  _resident((1, d)), _resident((d, IN_W)), _resident((1, LANES)),
                  _resident((1, MLA_Q_RANK)), _resident((1, MLA_KV_RANK)),
                  _resident((MLA_Q_RANK, SLOT_W)), _resident((MLA_KV_RANK, SLOT_W)),
                  _resident((MLA_KV_RANK, HEAD_W)),
                  row(LANES), row(LANES), row(LANES),
                  _resident((LANES, 2 * SLOT_W)), _resident((1, SLOT_W)), _resident((1, SLOT_W))],
        out_specs=out_specs,
        out_shape=out_shape,
        scratch_shapes=[pltpu.VMEM((1, LANES), F32)],
        compiler_params=_params(2),
        name="inproj",
    )(x, mod, g_mix, w_in_r, bff, g_q, g_kv, wuq_r, wkn_r, wkv_r, cos, sina, sinb, place, qc, kc)


def _head_rows():
    row = lax.broadcasted_iota(jnp.int32, (LANES, 1), 0)
    return [(row >= HEAD_DIM * hh) & (row < HEAD_DIM * (hh + 1)) for hh in range(2)]


def _head_masks():
    lane = lax.broadcasted_iota(jnp.int32, (1, LANES), 1)
    return [(lane >= HEAD_DIM * hh) & (lane < HEAD_DIM * (hh + 1)) for hh in range(2)]


def _col_reduce(x, op, final, chunk=64):
    rows = x.shape[0]
    if rows > chunk and rows % chunk == 0:
        acc = x[:chunk]
        for a in range(chunk, rows, chunk):
            acc = op(acc, x[a:a + chunk])
        x, rows = acc, chunk
    while rows > 8 and rows % 16 == 0:
        rows //= 2
        x = op(x[:rows], x[rows:])
    return final(x, axis=0, keepdims=True)


def _colmax(x):
    return _col_reduce(x, jnp.maximum, jnp.max)


def _colsum(x):
    return _col_reduce(x, jnp.add, jnp.sum)


def _pipeline(n, stages):
    depth = len(stages)

    def static_step(t):
        for s, stage in enumerate(stages):
            if 0 <= t - s < n:
                stage(t - s, (t - s) % 2)

    first_full = depth - 1
    n_full = max(n - depth + 1, 0)
    for t in range(min(first_full, n + depth - 1)):
        static_step(t)
    if n_full >= 2:
        def body(pair, carry):
            t = first_full + 2 * pair
            for u in range(2):
                for s, stage in enumerate(stages):
                    stage(t + u - s, (first_full + u - s) % 2)
            return carry

        lax.fori_loop(0, n_full // 2, body, 0)
    for t in range(first_full + 2 * (n_full // 2), n + depth - 1):
        static_step(t)


def _block_tables(n_tiles, t, descending):
    diag = [(i * t, i * t) for i in range(n_tiles)]
    below = []
    for i in range(1, n_tiles):
        js = range(i - 1, -1, -1) if descending else range(i)
        below += [(i * t, j * t) for j in js]
    return jnp.asarray(np.array(diag + below, np.int32).T), len(diag), len(below)


def _softmax_flat_kernel(tab_ref, q_ref, k_ref, vt_ref, o_ref, acc_ref, m_ref, al_ref, top_ref,
                         s0_ref, s1_ref, p0_ref, p1_ref, *, t, n_diag, n_below, chunk_shift):
    own_rows = _head_rows()
    s_bufs, p_bufs = (s0_ref, s1_ref), (p0_ref, p1_ref)
    kio = lax.broadcasted_iota(jnp.int32, (t, 1), 0)
    qio = lax.broadcasted_iota(jnp.int32, (1, t), 1)
    acc_ref[...] = jnp.zeros_like(acc_ref)
    m_ref[...] = jnp.full_like(m_ref, NEG)

    def stages(base, diag):
        def item(i):
            return (pl.multiple_of(tab_ref[0, base + i], t), pl.multiple_of(tab_ref[1, base + i], t))

        def scores(i, par):
            q0, k0 = item(i)
            for hh in range(2):
                lanes = slice(LANES * hh, LANES * (hh + 1))
                s = _dot_nt(k_ref[0, pl.ds(k0, t), lanes],
                            q_ref[0, pl.ds(q0, t), lanes])
                if diag:
                    s = jnp.where((kio >> chunk_shift) <= (qio >> chunk_shift), s, NEG)
                s_bufs[par][hh] = s
                top_ref[par, hh] = _colmax(s)

        def numerators(i, par):
            q0, _ = item(i)
            for hh in range(2):
                m_old = m_ref[hh, :, pl.ds(q0, t)]
                m_new = jnp.maximum(m_old, top_ref[par, hh])
                for a in range(0, t, ROW_CHUNK):
                    rows = slice(a, a + ROW_CHUNK)
                    p_bufs[par][hh, rows, :] = jnp.exp(s_bufs[par][hh, rows, :] - m_new).astype(BF16)
                al_ref[par, hh] = jnp.exp(m_old - m_new)
                m_ref[hh, :, pl.ds(q0, t)] = m_new

        def values(i, par):
            q0, k0 = item(i)
            vt = vt_ref[0, :, pl.ds(k0, t)]
            for hh in range(2):
                v1 = jnp.where(own_rows[hh], vt, jnp.ones_like(vt))
                acc_ref[hh, :, pl.ds(q0, t)] = (acc_ref[hh, :, pl.ds(q0, t)] * al_ref[par, hh]
                                                + _bdot(v1, p_bufs[par][hh]))

        return [scores, numerators, values]

    _pipeline(n_diag, stages(0, True))
    _pipeline(n_below, stages(n_diag, False))

    sub = lax.broadcasted_iota(jnp.int32, (LANES, 1), 0)

    def finish(tile, carry):
        q0 = pl.multiple_of(tile * t, t)
        outs = []
        for hh in range(2):
            acc = acc_ref[hh, :, pl.ds(q0, t)]
            denom = acc[HEAD_DIM * (1 - hh):HEAD_DIM * (1 - hh) + 1, :]
            outs.append(acc / denom)
        o_ref[0, pl.ds(q0, t), :] = jnp.where(sub < HEAD_DIM, outs[0], outs[1]).T.astype(o_ref.dtype)
        return carry

    lax.fori_loop(0, n_diag, finish, 0)


def _softmax_attention_flat(q, k, v, *, t, chunk_shift, name):
    b, s, _ = q.shape
    table, n_diag, n_below = _block_tables(s // t, t, descending=False)
    grid_spec = pltpu.PrefetchScalarGridSpec(
        num_scalar_prefetch=1,
        grid=(b, N_HEADS // 2),
        in_specs=[pl.BlockSpec((1, s, 2 * LANES), lambda i, h, tab: (i, 0, h)),
                  pl.BlockSpec((1, s, 2 * LANES), lambda i, h, tab: (i, 0, h)),
                  pl.BlockSpec((1, LANES, s), lambda i, h, tab: (i, h, 0))],
        out_specs=pl.BlockSpec((1, s, LANES), lambda i, h, tab: (i, 0, h)),
        scratch_shapes=[pltpu.VMEM((2, LANES, s), F32),
                        pltpu.VMEM((2, 1, s), F32),
                        pltpu.VMEM((2, 2, 1, t), F32),
                        pltpu.VMEM((2, 2, 1, t), F32),
                        pltpu.VMEM((2, t, t), F32), pltpu.VMEM((2, t, t), F32),
                        pltpu.VMEM((2, t, t), BF16), pltpu.VMEM((2, t, t), BF16)])
    return pl.pallas_call(
        functools.partial(_softmax_flat_kernel, t=t, n_diag=n_diag, n_below=n_below,
                          chunk_shift=chunk_shift),
        grid_spec=grid_spec,
        out_shape=jax.ShapeDtypeStruct((b, s, HEAD_W), BF16),
        compiler_params=_params(2),
        name=name,
    )(table, q, k, v)


def _sb_flat_kernel(tab_ref, q_ref, k_ref, vt_ref, o_ref, acc_ref, r_ref, tot_ref, d_ref,
                    z0_ref, z1_ref, e0_ref, e1_ref, w0_ref, w1_ref, *, t, n_diag, n_below):
    in_head = _head_masks()
    own_rows = _head_rows()
    z_bufs, e_bufs, w_bufs = (z0_ref, z1_ref), (e0_ref, e1_ref), (w0_ref, w1_ref)
    kio = lax.broadcasted_iota(jnp.int32, (t, 1), 0)
    qio = lax.broadcasted_iota(jnp.int32, (1, t), 1)
    ss = lax.broadcasted_iota(jnp.int32, (t, t), 0)
    jj = lax.broadcasted_iota(jnp.int32, (t, t), 1)
    from_here = jnp.where(jj >= ss, 1.0, 0.0).astype(BF16)
    acc_ref[...] = jnp.zeros_like(acc_ref)
    r_ref[...] = jnp.zeros_like(r_ref)

    def stages(base, diag):
        def item(i):
            return (pl.multiple_of(tab_ref[0, base + i], t), pl.multiple_of(tab_ref[1, base + i], t))

        def scores(i, par):
            q0, k0 = item(i)
            k = k_ref[0, pl.ds(k0, t), :]
            q = q_ref[0, pl.ds(q0, t), :]
            for hh in range(2):
                qh = jnp.where(in_head[hh], q, jnp.zeros_like(q))
                z_bufs[par][hh] = _dot_nt(k, qh)

        def drops(i, par):
            for hh in range(2):
                for a in range(0, t, ROW_CHUNK):
                    rows = slice(a, a + ROW_CHUNK)
                    z = z_bufs[par][hh, rows, :]
                    neg_abs = pltpu.bitcast(pltpu.bitcast(z, jnp.uint32) | SIGN_BIT, F32)
                    drop = jnp.maximum(z, 0.0) + jnp.log(1.0 + jnp.exp(neg_abs))
                    if diag:
                        drop = jnp.where(kio[rows] < qio, drop, 0.0)
                    d_ref[hh, rows, :] = drop.astype(BF16)
                incl = _bdot(from_here, d_ref[hh])
                for a in range(0, t, ROW_CHUNK):
                    rows = slice(a, a + ROW_CHUNK)
                    e = z_bufs[par][hh, rows, :] - incl[rows]
                    if diag:
                        e = jnp.where(kio[rows] < qio, e, NEG)
                    e_bufs[par][hh, rows, :] = e
                tot_ref[par, hh] = incl[0:1, :]

        def weights(i, par):
            q0, _ = item(i)
            for hh in range(2):
                r_old = r_ref[hh, :, pl.ds(q0, t)]
                for a in range(0, t, ROW_CHUNK):
                    rows = slice(a, a + ROW_CHUNK)
                    w_bufs[par][hh, rows, :] = jnp.exp(e_bufs[par][hh, rows, :] - r_old).astype(BF16)
                r_ref[hh, :, pl.ds(q0, t)] = r_old + tot_ref[par, hh]

        def values(i, par):
            q0, k0 = item(i)
            vt = vt_ref[0, :, pl.ds(k0, t)]
            zero = jnp.zeros_like(vt)
            lhs = jnp.concatenate([jnp.where(own_rows[hh], vt, zero) for hh in range(2)], axis=1)
            acc_ref[:, pl.ds(q0, t)] = (acc_ref[:, pl.ds(q0, t)]
                                        + _bdot(lhs, w_bufs[par][...].reshape(2 * t, t)))

        return [scores, drops, weights, values]

    _pipeline(n_diag, stages(0, True))
    _pipeline(n_below, stages(n_diag, False))

    def finish(tile, carry):
        q0 = pl.multiple_of(tile * t, t)
        o_ref[0, pl.ds(q0, t), :] = acc_ref[:, pl.ds(q0, t)].T.astype(o_ref.dtype)
        return carry

    lax.fori_loop(0, n_diag, finish, 0)


def _sb_attention_flat(qk, vt, *, t):
    b, s, _ = qk.shape
    table, n_diag, n_below = _block_tables(s // t, t, descending=True)
    grid_spec = pltpu.PrefetchScalarGridSpec(
        num_scalar_prefetch=1,
        grid=(b, N_HEADS // 2),
        in_specs=[pl.BlockSpec((1, s, LANES), lambda i, h, tab: (i, 0, h)),
                  pl.BlockSpec((1, s, LANES), lambda i, h, tab: (i, 0, 4 + h)),
                  pl.BlockSpec((1, LANES, s), lambda i, h, tab: (i, h, 0))],
        out_specs=pl.BlockSpec((1, s, LANES), lambda i, h, tab: (i, 0, h)),
        scratch_shapes=[pltpu.VMEM((LANES, s), F32),
                        pltpu.VMEM((2, 1, s), F32),
                        pltpu.VMEM((2, 2, 1, t), F32),
                        pltpu.VMEM((2, t, t), BF16),
                        pltpu.VMEM((2, t, t), F32), pltpu.VMEM((2, t, t), F32),
                        pltpu.VMEM((2, t, t), F32), pltpu.VMEM((2, t, t), F32),
                        pltpu.VMEM((2, t, t), BF16), pltpu.VMEM((2, t, t), BF16)])
    return pl.pallas_call(
        functools.partial(_sb_flat_kernel, t=t, n_diag=n_diag, n_below=n_below),
        grid_spec=grid_spec,
        out_shape=jax.ShapeDtypeStruct((b, s, HEAD_W), BF16),
        compiler_params=_params(2),
        name="sb_attention",
    )(table, qk, qk, vt)


def _softmax_attn_kernel(q_ref, k_ref, v_ref, o_ref, acc_ref, s0_ref, s1_ref, *, tq, tk,
                         chunk_shift):
    qi = pl.program_id(2)
    in_head = _head_masks()
    qs = [q_ref[0, :, LANES * hh:LANES * (hh + 1)] for hh in range(2)]
    qpos = qi * tq + lax.broadcasted_iota(jnp.int32, (1, tq), 1)
    acc_ref[...] = jnp.zeros_like(acc_ref)

    def produce(j, buf):
        start = pl.multiple_of(j * tk, tk)
        for hh in range(2):
            k = k_ref[0, pl.ds(start, tk), LANES * hh:LANES * (hh + 1)]
            buf[hh] = _dot_nt(k, qs[hh])

    def consume(j, buf, m, masked):
        start = pl.multiple_of(j * tk, tk)
        v = v_ref[0, pl.ds(start, tk), :]
        m_out = []
        for hh in range(2):
            s = buf[hh]
            if masked:
                kpos = start + lax.broadcasted_iota(jnp.int32, (tk, 1), 0)
                s = jnp.where((kpos >> chunk_shift) <= (qpos >> chunk_shift), s, NEG)
            m_new = jnp.maximum(m[hh], _colmax(s))
            p = jnp.exp(s - m_new)
            alpha = jnp.exp(m[hh] - m_new)
            v1 = jnp.where(in_head[hh], v, jnp.ones_like(v))
            acc_ref[hh] = acc_ref[hh] * alpha + _dot_tn(v1, p.astype(BF16))
            m_out.append(m_new)
        return tuple(m_out)

    n_last = (qi * tq + tq - 1) // tk
    n_pairs = n_last // 2

    def body(i, m):
        j = 2 * i
        produce(j + 1, s1_ref)
        m = consume(j, s0_ref, m, False)
        produce(j + 2, s0_ref)
        return consume(j + 1, s1_ref, m, False)

    produce(0, s0_ref)
    m = (jnp.full((1, tq), NEG, F32),) * 2
    m = lax.fori_loop(0, n_pairs, body, m)
    j = 2 * n_pairs

    @pl.when(n_last % 2 == 1)
    def _():
        produce(j + 1, s1_ref)
        consume(j + 1, s1_ref, consume(j, s0_ref, m, False), True)

    @pl.when(n_last % 2 == 0)
    def _():
        consume(j, s0_ref, m, True)

    sub = lax.broadcasted_iota(jnp.int32, (LANES, 1), 0)
    outs = []
    for hh in range(2):
        acc = acc_ref[hh]
        denom = acc[HEAD_DIM * (1 - hh):HEAD_DIM * (1 - hh) + 1, :]
        outs.append(acc / denom)
    o_ref[0] = jnp.where(sub < HEAD_DIM, outs[0], outs[1]).T.astype(o_ref.dtype)


def _softmax_attention(q, k, v, *, tq, tk, chunk_shift, name):
    b, s, _ = q.shape
    return pl.pallas_call(
        functools.partial(_softmax_attn_kernel, tq=tq, tk=tk, chunk_shift=chunk_shift),
        grid=(b, N_HEADS // 2, s // tq),
        in_specs=[pl.BlockSpec((1, tq, 2 * LANES), lambda i, h, j: (i, j, h)),
                  pl.BlockSpec((1, s, 2 * LANES), lambda i, h, j: (i, 0, h)),
                  pl.BlockSpec((1, s, LANES), lambda i, h, j: (i, 0, h))],
        out_specs=pl.BlockSpec((1, tq, LANES), lambda i, h, j: (i, j, h)),
        out_shape=jax.ShapeDtypeStruct((b, s, HEAD_W), BF16),
        scratch_shapes=[pltpu.VMEM((2, LANES, tq), F32), pltpu.VMEM((2, tk, tq), F32),
                        pltpu.VMEM((2, tk, tq), F32)],
        compiler_params=_params(3),
        name=name,
    )(q, k, v)


def _sb_kernel(q_ref, k_ref, v_ref, o_ref, acc_ref, z_ref, r_ref, *, t):
    qi = pl.program_id(2)
    in_head = _head_masks()
    q = q_ref[0]
    qs = [jnp.where(in_head[hh], q, jnp.zeros_like(q)) for hh in range(2)]
    qpos = lax.broadcasted_iota(jnp.int32, (1, t), 1)
    kpos = lax.broadcasted_iota(jnp.int32, (t, 1), 0)
    ss = lax.broadcasted_iota(jnp.int32, (t, 2 * t), 0)
    jj = lax.broadcasted_iota(jnp.int32, (t, 2 * t), 1) & (t - 1)
    from_here = jnp.where(jj >= ss, 1.0, 0.0).astype(BF16)
    acc_ref[...] = jnp.zeros_like(acc_ref)
    r_ref[...] = jnp.zeros_like(r_ref)

    def key_start(blk):
        return pl.multiple_of(jnp.maximum(blk, 0) * t, t)

    def scores(blk_hi, nblk):
        for u in range(nblk):
            k = k_ref[0, pl.ds(key_start(blk_hi - u), t), :]
            for hh in range(2):
                z_ref[u, hh] = _dot_nt(k, qs[hh])

    def group(blk_hi, nblk, diag_first, next_hi):
        chains = []
        for u in range(nblk):
            for hh in range(2):
                z = z_ref[u, hh]
                drop = jnp.maximum(z, 0.0) + jnp.log(1.0 + jnp.exp(-jnp.abs(z)))
                if diag_first and u == 0:
                    ok = kpos < qpos
                    drop = jnp.where(ok, drop, 0.0)
                hi = drop.astype(BF16)
                lo = (drop - hi.astype(F32)).astype(BF16)
                incl = _bdot(from_here, jnp.concatenate([hi, lo], axis=0))
                e = z - incl
                if diag_first and u == 0:
                    e = jnp.where(ok, e, NEG)
                chains.append((u, hh, e, incl[0:1, :]))
        scores(next_hi, 2)
        r = [r_ref[hh] for hh in range(2)]
        upd = None
        for u, hh, e, total in chains:
            w = jnp.exp(e - r[hh])
            v = v_ref[0, pl.ds(key_start(blk_hi - u), t), :]
            vh = jnp.where(in_head[hh], v, jnp.zeros_like(v))
            pv = _dot_tn(vh, w.astype(BF16))
            upd = pv if upd is None else upd + pv
            r[hh] = r[hh] + total
        acc_ref[...] = acc_ref[...] + upd
        for hh in range(2):
            r_ref[hh] = r[hh]

    @pl.when(qi % 2 == 0)
    def _():
        scores(qi, 1)
        group(qi, 1, True, qi - 1)

    @pl.when(qi % 2 == 1)
    def _():
        scores(qi, 2)
        group(qi, 2, True, qi - 2)

    n_rest = qi // 2
    top = 2 * n_rest - 1

    def body(i, carry):
        group(top - 2 * i, 2, False, top - 2 * i - 2)
        return carry

    lax.fori_loop(0, n_rest, body, 0)
    o_ref[0] = acc_ref[...].T.astype(o_ref.dtype)


def _sb_attention(qkv, *, t):
    b, s, _ = qkv.shape
    return pl.pallas_call(
        functools.partial(_sb_kernel, t=t),
        grid=(b, N_HEADS // 2, s // t),
        in_specs=[pl.BlockSpec((1, t, LANES), lambda i, h, j: (i, j, h)),
                  pl.BlockSpec((1, s, LANES), lambda i, h, j: (i, 0, 4 + h)),
                  pl.BlockSpec((1, s, LANES), lambda i, h, j: (i, 0, 8 + h))],
        out_specs=pl.BlockSpec((1, t, LANES), lambda i, h, j: (i, j, h)),
        out_shape=jax.ShapeDtypeStruct((b, s, HEAD_W), BF16),
        scratch_shapes=[pltpu.VMEM((LANES, t), F32), pltpu.VMEM((2, 2, t, t), F32),
                        pltpu.VMEM((2, 1, t), F32)],
        compiler_params=_params(3),
        name="sb_attention",
    )(qkv, qkv, qkv)


def _merge_kernel(x_ref, yf_ref, ym_ref, ys_ref, g_ref, mod_ref, wf_ref, wm_ref, ws_ref,
                  wo_ref, o_ref):
    d = D_MODEL
    merged = (g_ref[0, :, 0:d].astype(F32) * _bdot(yf_ref[0], wf_ref[...])
              + g_ref[0, :, d:2 * d].astype(F32) * _bdot(ym_ref[0], wm_ref[...])
              + g_ref[0, :, 2 * d:3 * d].astype(F32) * _bdot(ys_ref[0], ws_ref[...]))
    o_ref[0] = x_ref[0] + mod_ref[0, 2:3, :] * _bdot(merged.astype(BF16), wo_ref[...])


def _merge(x, y_fox, y_mla, y_sb, gates, mod, wf, wm, ws, wo, *, tm):
    b, s, d = x.shape
    row = lambda w: pl.BlockSpec((1, tm, w), lambda i, j: (i, j, 0))
    return pl.pallas_call(
        _merge_kernel,
        grid=(b, s // tm),
        in_specs=[row(d), row(HEAD_W), row(HEAD_W), row(HEAD_W), row(3 * d),
                  pl.BlockSpec((1, 6, d), lambda i, j: (i, 0, 0)),
                  _resident((HEAD_W, d)), _resident((HEAD_W, d)), _resident((HEAD_W, d)),
                  _resident((d, d))],
        out_specs=row(d),
        out_shape=jax.ShapeDtypeStruct((b, s, d), F32),
        compiler_params=_params(2),
        name="merge_out",
    )(x, y_fox, y_mla, y_sb, gates, mod, wf, wm, ws, wo)


def _ffn_kernel(x_ref, mod_ref, g_ref, wg_ref, wu_ref, wd_ref, gfin_ref, o_ref, *, final, fc):
    x = x_ref[0]
    sh, sc, gt = mod_ref[0, 3:4, :], mod_ref[0, 4:5, :], mod_ref[0, 5:6, :]
    u = ((_rms(x) * g_ref[...]) * (1.0 + sc) + sh).astype(BF16)
    acc = None
    for a in range(0, D_FF, fc):
        hg = _bdot(u, wg_ref[:, a:a + fc])
        hu = _bdot(u, wu_ref[:, a:a + fc])
        h = ((hg * _sigmoid(hg)) * hu).astype(BF16)
        part = _bdot(h, wd_ref[a:a + fc, :])
        acc = part if acc is None else acc + part
    out = x + gt * acc
    if final:
        out = _rms(out) * gfin_ref[...]
    o_ref[0] = out


def _ffn(x, mod, g_ffn, wg, wu, wd, g_final, *, tm, final):
    b, s, d = x.shape
    row = lambda w: pl.BlockSpec((1, tm, w), lambda i, j: (i, j, 0))
    return pl.pallas_call(
        functools.partial(_ffn_kernel, final=final, fc=D_FF // 2),
        grid=(b, s // tm),
        in_specs=[row(d), pl.BlockSpec((1, 6, d), lambda i, j: (i, 0, 0)), _resident((1, d)),
                  _resident((d, D_FF)), _resident((d, D_FF)), _resident((D_FF, d)),
                  _resident((1, d))],
        out_specs=row(d),
        out_shape=jax.ShapeDtypeStruct((b, s, d), F32),
        compiler_params=_params(2),
        name="ffn",
    )(x, mod, g_ffn, wg, wu, wd, g_final)


def _slot_cols(w, width_per_head, used):
    k = w.shape[0]
    w = w.reshape(k, N_HEADS, width_per_head)[:, :, :used]
    return jnp.pad(w, ((0, 0), (0, 0), (0, LANES - used))).reshape(k, SLOT_W)


def _layout_w_in(w):
    d = w.shape[0]
    fox, ff, ql, kvl, kr, sb, gate = jnp.split(
        w, [1536, 1544, 1928, 2184, 2216, 3752], axis=-1)
    z = lambda n: jnp.zeros((d, n), w.dtype)
    fq, fk, fv = jnp.split(fox, 3, axis=-1)
    kr_slot = jnp.concatenate([z(MLA_NOPE), kr, z(LANES - MLA_NOPE - MLA_ROPE)], axis=-1)
    ff_slot = jnp.concatenate([ff, z(LANES - N_HEADS)], axis=-1)
    cols = [_slot_cols(fq, HEAD_DIM, HEAD_DIM), _slot_cols(fk, HEAD_DIM, HEAD_DIM), fv, sb, gate,
            ql, kvl, kr_slot, ff_slot]
    return jnp.concatenate(cols, axis=-1).astype(BF16)


def kernel(x, c, positions, g_mix, w_ada, b_ada, w_in, b_fox_f, g_mla_q, w_mla_uq, g_mla_kv,
           w_mla_ukv, w_o_fox, w_o_mla, w_o_sb, w_out, g_ffn, w_ffn_gate, w_ffn_up, w_ffn_down,
           g_final):
    b, s, d = x.shape
    depth = w_in.shape[0]
    tm = min(s, 512)
    ta = min(s, 512)
    tsb = min(s, 256)

    mod = _ada(c, w_ada, b_ada).reshape(depth, b, 6, d)
    tables = _rope_tables(positions)
    for l in range(depth):
        w_in_r = _layout_w_in(w_in[l])
        wuq_r = _slot_cols(w_mla_uq[l], MLA_NOPE + MLA_ROPE, MLA_NOPE + MLA_ROPE).astype(BF16)
        wkn_r = _slot_cols(w_mla_ukv[l], 2 * HEAD_DIM, MLA_NOPE).astype(BF16)
        wkv_r = w_mla_ukv[l].reshape(MLA_KV_RANK, N_HEADS, 2 * HEAD_DIM)[:, :, MLA_NOPE:]
        wkv_r = wkv_r.reshape(MLA_KV_RANK, HEAD_W).astype(BF16)
        bff = jnp.pad(b_fox_f[l], (0, LANES - N_HEADS)).reshape(1, LANES)

        fox_q, fox_k, fox_vt, sb_qk, sb_vt, gates, mq, mk, mvt = _inproj(
            x, mod[l], g_mix[l].reshape(1, d), w_in_r, bff, g_mla_q[l].reshape(1, -1),
            g_mla_kv[l].reshape(1, -1), wuq_r, wkn_r, wkv_r, tables, tm=tm)
        y_fox = _softmax_attention_flat(fox_q, fox_k, fox_vt, t=ta, chunk_shift=0,
                                        name="fox_attention")
        y_mla = _softmax_attention_flat(mq, mk, mvt, t=ta, chunk_shift=CHUNK_SHIFT,
                                        name="mla_attention")
        y_sb = _sb_attention_flat(sb_qk, sb_vt, t=tsb)
        x = _merge(x, y_fox, y_mla, y_sb, gates, mod[l], w_o_fox[l].astype(BF16),
                   w_o_mla[l].astype(BF16), w_o_sb[l].astype(BF16), w_out[l].astype(BF16), tm=tm)
        x = _ffn(x, mod[l], g_ffn[l].reshape(1, d), w_ffn_gate[l].astype(BF16),
                 w_ffn_up[l].astype(BF16), w_ffn_down[l].astype(BF16), g_final.reshape(1, d),
                 tm=tm, final=(l == depth - 1))
    return x
```

```python
import functools

import numpy as np
import jax
import jax.numpy as jnp
from jax import lax
from jax.experimental import pallas as pl
from jax.experimental.pallas import tpu as pltpu

F32 = jnp.float32
BF16 = jnp.bfloat16

D_MODEL = 1024
EPS = 1e-6
HEAD_DIM = 64
N_HEADS = 8
HEAD_W = N_HEADS * HEAD_DIM
CHUNK_SHIFT = 6
MLA_Q_RANK = 384
MLA_KV_RANK = 256
MLA_NOPE = 64
MLA_ROPE = 32
ROPE_HALF = MLA_ROPE // 2
ROPE_BASE = 10000.0
MLA_SCALE = (MLA_NOPE + MLA_ROPE) ** -0.5
QK_SCALE = HEAD_DIM ** -0.5
D_FF = 2816
LANES = 128
SLOT_W = N_HEADS * LANES
N_PIECES = 3

C_FQ = 0
C_FK = 1024
C_FV = 2048
C_SB = 2560
C_GATE = 4096
C_QL = 7168
C_KVL = 7552
C_KR = 7808
C_FF = 7936
IN_W = 8064

ROW_CHUNK = 64
SIGN_BIT = np.uint32(0x80000000)
NEG = -1e30
VMEM_LIMIT = 56 * 1024 * 1024


def _params(n_grid):
    return pltpu.CompilerParams(dimension_semantics=("arbitrary",) * n_grid,
                                vmem_limit_bytes=VMEM_LIMIT)


def _resident(shape):
    nd = len(shape)
    return pl.BlockSpec(shape, lambda *_: (0,) * nd, pipeline_mode=pl.Buffered(1))


def _bdot(a, b):
    return jnp.dot(a, b, preferred_element_type=F32)


def _dot_nt(a, b):
    return lax.dot_general(a, b, (((1,), (1,)), ((), ())), preferred_element_type=F32)


def _dot_tn(a, b):
    return lax.dot_general(a, b, (((0,), (0,)), ((), ())), preferred_element_type=F32)


def _split3(v):
    hi = v.astype(BF16)
    r = v - hi.astype(F32)
    mid = r.astype(BF16)
    lo = (r - mid.astype(F32)).astype(BF16)
    return hi, mid, lo


def _log_sigmoid(z):
    return jnp.minimum(z, 0.0) - jnp.log1p(jnp.exp(-jnp.abs(z)))


def _sigmoid(z):
    return 1.0 / (1.0 + jnp.exp(-z))


def _rms(x):
    return x * lax.rsqrt(jnp.mean(x * x, axis=-1, keepdims=True) + EPS)


def _ada_kernel(c_ref, w_ref, b_ref, o_ref):
    c = c_ref[...]
    cond = c * _sigmoid(c)
    chi, cmid, clo = _split3(cond)
    whi, wmid, wlo = _split3(w_ref[0])
    acc = _bdot(chi, whi) + (_bdot(chi, wmid) + _bdot(cmid, whi))
    acc = acc + (_bdot(chi, wlo) + _bdot(clo, whi) + _bdot(cmid, wmid))
    o_ref[0] = acc + b_ref[0]


def _ada(c, w_ada, b_ada):
    depth, d, n = w_ada.shape
    b = c.shape[0]
    tn = 1024
    return pl.pallas_call(
        _ada_kernel,
        grid=(depth, n // tn),
        in_specs=[pl.BlockSpec((b, d), lambda l, j: (0, 0)),
                  pl.BlockSpec((1, d, tn), lambda l, j: (l, 0, j)),
                  pl.BlockSpec((1, 1, tn), lambda l, j: (l, 0, j))],
        out_specs=pl.BlockSpec((1, b, tn), lambda l, j: (l, 0, j)),
        out_shape=jax.ShapeDtypeStruct((depth, b, n), F32),
        compiler_params=_params(2),
        name="adaln",
    )(c, w_ada, b_ada.reshape(depth, 1, n))


def _rope_kernel(pos_ref, inv_ref, cos_ref, sina_ref, sinb_ref):
    ang = pos_ref[0].astype(F32) * inv_ref[...]
    lane = lax.broadcasted_iota(jnp.int32, (1, LANES), 1)
    sin = jnp.sin(ang)
    cos_ref[0] = jnp.cos(ang)
    lo, mid, hi = MLA_NOPE, MLA_NOPE + ROPE_HALF, MLA_NOPE + MLA_ROPE
    sina_ref[0] = jnp.where((lane >= lo) & (lane < mid), -sin, 0.0)
    sinb_ref[0] = jnp.where((lane >= mid) & (lane < hi), sin, 0.0)


def _rope_tables(positions):
    b, s = positions.shape
    ts = min(s, 512)
    inv = ROPE_BASE ** (-jnp.arange(ROPE_HALF, dtype=F32) / ROPE_HALF)
    inv_slot = jnp.concatenate([jnp.zeros((MLA_NOPE,), F32), inv, inv,
                                jnp.zeros((LANES - MLA_NOPE - MLA_ROPE,), F32)]).reshape(1, LANES)
    spec = pl.BlockSpec((1, ts, LANES), lambda i, j: (i, j, 0))
    return pl.pallas_call(
        _rope_kernel,
        grid=(b, s // ts),
        in_specs=[pl.BlockSpec((1, ts, 1), lambda i, j: (i, j, 0)),
                  pl.BlockSpec((1, LANES), lambda i, j: (0, 0))],
        out_specs=[spec, spec, spec],
        out_shape=[jax.ShapeDtypeStruct((b, s, LANES), F32)] * 3,
        compiler_params=_params(2),
        name="rope_tables",
    )(positions.reshape(b, s, 1), inv_slot)


def _rope_slot(x, cos, sina, sinb):
    return (x * cos + pltpu.roll(x, LANES - ROPE_HALF, 1) * sina
            + pltpu.roll(x, ROPE_HALF, 1) * sinb)


def _inproj_kernel(x_ref, mod_ref, g_ref, w_ref, bff_ref, gq_ref, gkv_ref, wuq_ref, wkn_ref,
                   wkv_ref, cos_ref, sina_ref, sinb_ref, place_ref, qc_ref, kc_ref,
                   fq_ref, fk_ref, fv_ref, sq_ref, sk_ref, sv_ref, gate_ref, mq_ref, mk_ref, mv_ref,
                   carry_ref, *, tm):
    x = x_ref[0]
    sh, sc = mod_ref[0, 0:1, :], mod_ref[0, 1:2, :]
    u = ((_rms(x) * g_ref[...]) * (1.0 + sc) + sh).astype(BF16)

    def seg(a, b):
        return _bdot(u, w_ref[:, a:b])

    @pl.when(pl.program_id(1) == 0)
    def _():
        carry_ref[...] = jnp.zeros_like(carry_ref)

    logf = _log_sigmoid(seg(C_FF, IN_W) + bff_ref[...])
    r = lax.broadcasted_iota(jnp.int32, (tm, tm), 0)
    c = lax.broadcasted_iota(jnp.int32, (tm, tm), 1)
    tri = jnp.where(r >= c, 1.0, 0.0).astype(BF16)
    hi, mid, lo = _split3(logf)
    cum = (_bdot(tri, hi) + _bdot(tri, mid) + _bdot(tri, lo)) + carry_ref[...]
    carry_ref[...] = cum[tm - 1:tm, :]
    p0, p1, p2 = (p.astype(F32) for p in _split3(cum))
    lane = lax.broadcasted_iota(jnp.int32, (1, LANES), 1)
    pieces = jnp.where(lane < N_HEADS, p0,
                       jnp.where(lane < 2 * N_HEADS, pltpu.roll(p1, N_HEADS, 1),
                                 jnp.where(lane < 3 * N_HEADS, pltpu.roll(p2, 2 * N_HEADS, 1),
                                           0.0))).astype(BF16)

    for c in range(2):
        a = 512 * c
        placed = _bdot(pieces, place_ref[:, a:a + 512])
        q = seg(C_FQ + a, C_FQ + a + 512) * QK_SCALE + placed + qc_ref[:, a:a + 512]
        fq_ref[0, a:a + 512, :] = q.T.astype(BF16)
        placed = _bdot(pieces, place_ref[:, SLOT_W + a:SLOT_W + a + 512])
        k = seg(C_FK + a, C_FK + a + 512) + placed + kc_ref[:, a:a + 512]
        fk_ref[0, :, a:a + 512] = k.astype(BF16)
    fv_ref[0] = seg(C_FV, C_SB).T.astype(BF16)

    sq_ref[0] = (seg(C_SB, C_SB + 512) * QK_SCALE).T.astype(BF16)
    sk_ref[0] = seg(C_SB + 512, C_SB + 1024).astype(BF16)
    sv_ref[0] = seg(C_SB + 1024, C_GATE).T.astype(BF16)
    for c in range(6):
        a = 512 * c
        gate_ref[0, :, a:a + 512] = _sigmoid(seg(C_GATE + a, C_GATE + a + 512)).astype(BF16)

    cos, sina, sinb = cos_ref[0], sina_ref[0], sinb_ref[0]
    cq = (_rms(seg(C_QL, C_KVL)) * gq_ref[...]).astype(BF16)
    ckv = (_rms(seg(C_KVL, C_KR)) * gkv_ref[...]).astype(BF16)
    kr = _rope_slot(seg(C_KR, C_FF), cos, sina, sinb)
    for h in range(N_HEADS):
        a = LANES * h
        q = _rope_slot(_bdot(cq, wuq_ref[:, a:a + LANES]), cos, sina, sinb)
        mq_ref[0, a:a + LANES, :] = (q * MLA_SCALE).T.astype(BF16)
        mk_ref[0, :, a:a + LANES] = (_bdot(ckv, wkn_ref[:, a:a + LANES]) + kr).astype(BF16)
    mv_ref[0] = _bdot(ckv, wkv_ref[...]).T.astype(BF16)


def _fox_placement():
    place = np.zeros((LANES, 2 * SLOT_W), np.float32)
    qc = np.zeros((1, SLOT_W), np.float32)
    kc = np.zeros((1, SLOT_W), np.float32)
    for h in range(N_HEADS):
        for p in range(N_PIECES):
            place[N_HEADS * p + h, LANES * h + HEAD_DIM + N_PIECES + p] = 1.0
            place[N_HEADS * p + h, SLOT_W + LANES * h + HEAD_DIM + p] = 1.0
            qc[0, LANES * h + HEAD_DIM + p] = -1.0
            kc[0, LANES * h + HEAD_DIM + N_PIECES + p] = 1.0
    return jnp.asarray(place, BF16), jnp.asarray(qc), jnp.asarray(kc)


def _inproj(x, mod, g_mix, w_in_r, bff, g_q, g_kv, wuq_r, wkn_r, wkv_r, tables, *, tm):
    b, s, d = x.shape
    cos, sina, sinb = tables
    place, qc, kc = _fox_placement()
    row = lambda w: pl.BlockSpec((1, tm, w), lambda i, j: (i, j, 0))
    col = lambda w: pl.BlockSpec((1, w, tm), lambda i, j: (i, 0, j))
    outs = [(SLOT_W, True),
            (SLOT_W, False),
            (HEAD_W, True),
            (HEAD_W, True),
            (HEAD_W, False),
            (HEAD_W, True),
            (3 * D_MODEL, False),
            (SLOT_W, True),
            (SLOT_W, False),
            (HEAD_W, True)]
    out_specs = [col(w) if tr else row(w) for w, tr in outs]
    out_shape = [jax.ShapeDtypeStruct((b, w, s) if tr else (b, s, w), BF16) for w, tr in outs]
    return pl.pallas_call(
        functools.partial(_inproj_kernel, tm=tm),
        grid=(b, s // tm),
        in_specs=[row(d),
                  pl.BlockSpec((1, 6, d), lambda i, j: (i, 0, 0)),
                  _resident((1, d)), _resident((d, IN_W)), _resident((1, LANES)),
                  _resident((1, MLA_Q_RANK)), _resident((1, MLA_KV_RANK)),
                  _resident((MLA_Q_RANK, SLOT_W)), _resident((MLA_KV_RANK, SLOT_W)),
                  _resident((MLA_KV_RANK, HEAD_W)),
                  row(LANES), row(LANES), row(LANES),
                  _resident((LANES, 2 * SLOT_W)), _resident((1, SLOT_W)), _resident((1, SLOT_W))],
        out_specs=out_specs,
        out_shape=out_shape,
        scratch_shapes=[pltpu.VMEM((1, LANES), F32)],
        compiler_params=_params(2),
        name="inproj",
    )(x, mod, g_mix, w_in_r, bff, g_q, g_kv, wuq_r, wkn_r, wkv_r, cos, sina, sinb, place, qc, kc)


def _head_rows():
    row = lax.broadcasted_iota(jnp.int32, (LANES, 1), 0)
    return [(row >= HEAD_DIM * hh) & (row < HEAD_DIM * (hh + 1)) for hh in range(2)]


def _head_masks():
    lane = lax.broadcasted_iota(jnp.int32, (1, LANES), 1)
    return [(lane >= HEAD_DIM * hh) & (lane < HEAD_DIM * (hh + 1)) for hh in range(2)]


def _col_reduce(x, op, final, chunk=64):
    rows = x.shape[0]
    if rows > chunk and rows % chunk == 0:
        acc = x[:chunk]
        for a in range(chunk, rows, chunk):
            acc = op(acc, x[a:a + chunk])
        x, rows = acc, chunk
    while rows > 8 and rows % 16 == 0:
        rows //= 2
        x = op(x[:rows], x[rows:])
    return final(x, axis=0, keepdims=True)


def _colmax(x):
    return _col_reduce(x, jnp.maximum, jnp.max)


def _colsum(x):
    return _col_reduce(x, jnp.add, jnp.sum)


def _pipeline(n, stages):
    depth = len(stages)

    def static_step(t):
        for s, stage in enumerate(stages):
            if 0 <= t - s < n:
                stage(t - s, (t - s) % 2)

    first_full = depth - 1
    n_full = max(n - depth + 1, 0)
    for t in range(min(first_full, n + depth - 1)):
        static_step(t)
    if n_full >= 2:
        def body(pair, carry):
            t = first_full + 2 * pair
            for u in range(2):
                for s, stage in enumerate(stages):
                    stage(t + u - s, (first_full + u - s) % 2)
            return carry

        lax.fori_loop(0, n_full // 2, body, 0)
    for t in range(first_full + 2 * (n_full // 2), n + depth - 1):
        static_step(t)


def _block_tables(n_tiles, t, descending):
    diag = [(i * t, i * t) for i in range(n_tiles)]
    below = []
    for i in range(1, n_tiles):
        js = range(i - 1, -1, -1) if descending else range(i)
        below += [(i * t, j * t) for j in js]
    return jnp.asarray(np.array(diag + below, np.int32).T), len(diag), len(below)


def _softmax_flat_kernel(tab_ref, qt_ref, k_ref, vt_ref, o_ref, acc_ref, m_ref, al_ref, top_ref,
                         s0_ref, s1_ref, p0_ref, p1_ref, *, t, n_diag, n_below, chunk_shift):
    own_rows = _head_rows()
    s_bufs, p_bufs = (s0_ref, s1_ref), (p0_ref, p1_ref)
    kio = lax.broadcasted_iota(jnp.int32, (t, 1), 0)
    qio = lax.broadcasted_iota(jnp.int32, (1, t), 1)
    acc_ref[...] = jnp.zeros_like(acc_ref)
    m_ref[...] = jnp.full_like(m_ref, NEG)

    def stages(base, diag):
        def item(i):
            return (pl.multiple_of(tab_ref[0, base + i], t), pl.multiple_of(tab_ref[1, base + i], t))

        def scores(i, par):
            q0, k0 = item(i)
            for hh in range(2):
                slot = slice(LANES * hh, LANES * (hh + 1))
                s = _bdot(k_ref[0, pl.ds(k0, t), slot],
                          qt_ref[0, slot, pl.ds(q0, t)])
                if diag:
                    s = jnp.where((kio >> chunk_shift) <= (qio >> chunk_shift), s, NEG)
                s_bufs[par][hh] = s
                top_ref[par, hh] = _colmax(s)

        def numerators(i, par):
            q0, _ = item(i)
            for hh in range(2):
                m_old = m_ref[hh, :, pl.ds(q0, t)]
                m_new = jnp.maximum(m_old, top_ref[par, hh])
                for a in range(0, t, ROW_CHUNK):
                    rows = slice(a, a + ROW_CHUNK)
                    p_bufs[par][hh, rows, :] = jnp.exp(s_bufs[par][hh, rows, :] - m_new).astype(BF16)
                al_ref[par, hh] = jnp.exp(m_old - m_new)
                m_ref[hh, :, pl.ds(q0, t)] = m_new

        def values(i, par):
            q0, k0 = item(i)
            vt = vt_ref[0, :, pl.ds(k0, t)]
            for hh in range(2):
                v1 = jnp.where(own_rows[hh], vt, jnp.ones_like(vt))
                acc_ref[hh, :, pl.ds(q0, t)] = (acc_ref[hh, :, pl.ds(q0, t)] * al_ref[par, hh]
                                                + _bdot(v1, p_bufs[par][hh]))

        return [scores, numerators, values]

    _pipeline(n_diag, stages(0, True))
    _pipeline(n_below, stages(n_diag, False))

    sub = lax.broadcasted_iota(jnp.int32, (LANES, 1), 0)

    def finish(tile, carry):
        q0 = pl.multiple_of(tile * t, t)
        outs = []
        for hh in range(2):
            acc = acc_ref[hh, :, pl.ds(q0, t)]
            denom = acc[HEAD_DIM * (1 - hh):HEAD_DIM * (1 - hh) + 1, :]
            outs.append(acc / denom)
        o_ref[0, pl.ds(q0, t), :] = jnp.where(sub < HEAD_DIM, outs[0], outs[1]).T.astype(o_ref.dtype)
        return carry

    lax.fori_loop(0, n_diag, finish, 0)


def _softmax_attention_flat(qt, k, vt, *, t, chunk_shift, name):
    b, s, _ = k.shape
    table, n_diag, n_below = _block_tables(s // t, t, descending=False)
    grid_spec = pltpu.PrefetchScalarGridSpec(
        num_scalar_prefetch=1,
        grid=(b, N_HEADS // 2),
        in_specs=[pl.BlockSpec((1, 2 * LANES, s), lambda i, h, tab: (i, h, 0)),
                  pl.BlockSpec((1, s, 2 * LANES), lambda i, h, tab: (i, 0, h)),
                  pl.BlockSpec((1, LANES, s), lambda i, h, tab: (i, h, 0))],
        out_specs=pl.BlockSpec((1, s, LANES), lambda i, h, tab: (i, 0, h)),
        scratch_shapes=[pltpu.VMEM((2, LANES, s), F32),
                        pltpu.VMEM((2, 1, s), F32),
                        pltpu.VMEM((2, 2, 1, t), F32),
                        pltpu.VMEM((2, 2, 1, t), F32),
                        pltpu.VMEM((2, t, t), F32), pltpu.VMEM((2, t, t), F32),
                        pltpu.VMEM((2, t, t), BF16), pltpu.VMEM((2, t, t), BF16)])
    return pl.pallas_call(
        functools.partial(_softmax_flat_kernel, t=t, n_diag=n_diag, n_below=n_below,
                          chunk_shift=chunk_shift),
        grid_spec=grid_spec,
        out_shape=jax.ShapeDtypeStruct((b, s, HEAD_W), BF16),
        compiler_params=_params(2),
        name=name,
    )(table, qt, k, vt)


def _sb_flat_kernel(tab_ref, qt_ref, k_ref, vt_ref, o_ref, acc_ref, r_ref, tot_ref, d_ref,
                    z0_ref, z1_ref, e0_ref, e1_ref, w0_ref, w1_ref, *, t, n_diag, n_below):
    own_rows = _head_rows()
    z_bufs, e_bufs, w_bufs = (z0_ref, z1_ref), (e0_ref, e1_ref), (w0_ref, w1_ref)
    kio = lax.broadcasted_iota(jnp.int32, (t, 1), 0)
    qio = lax.broadcasted_iota(jnp.int32, (1, t), 1)
    ss = lax.broadcasted_iota(jnp.int32, (t, t), 0)
    jj = lax.broadcasted_iota(jnp.int32, (t, t), 1)
    from_here = jnp.where(jj >= ss, 1.0, 0.0).astype(BF16)
    acc_ref[...] = jnp.zeros_like(acc_ref)
    r_ref[...] = jnp.zeros_like(r_ref)

    def stages(base, diag):
        def item(i):
            return (pl.multiple_of(tab_ref[0, base + i], t), pl.multiple_of(tab_ref[1, base + i], t))

        def scores(i, par):
            q0, k0 = item(i)
            k = k_ref[0, pl.ds(k0, t), :]
            qt = qt_ref[0, :, pl.ds(q0, t)]
            for hh in range(2):
                qh = jnp.where(own_rows[hh], qt, jnp.zeros_like(qt))
                z_bufs[par][hh] = _bdot(k, qh)

        def drops(i, par):
            for hh in range(2):
                for a in range(0, t, ROW_CHUNK):
                    rows = slice(a, a + ROW_CHUNK)
                    z = z_bufs[par][hh, rows, :]
                    neg_abs = pltpu.bitcast(pltpu.bitcast(z, jnp.uint32) | SIGN_BIT, F32)
                    drop = jnp.maximum(z, 0.0) + jnp.log(1.0 + jnp.exp(neg_abs))
                    if diag:
                        drop = jnp.where(kio[rows] < qio, drop, 0.0)
                    d_ref[hh, rows, :] = drop.astype(BF16)
                incl = _bdot(from_here, d_ref[hh])
                for a in range(0, t, ROW_CHUNK):
                    rows = slice(a, a + ROW_CHUNK)
                    e = z_bufs[par][hh, rows, :] - incl[rows]
                    if diag:
                        e = jnp.where(kio[rows] < qio, e, NEG)
                    e_bufs[par][hh, rows, :] = e
                tot_ref[par, hh] = incl[0:1, :]

        def weights(i, par):
            q0, _ = item(i)
            for hh in range(2):
                r_old = r_ref[hh, :, pl.ds(q0, t)]
                for a in range(0, t, ROW_CHUNK):
                    rows = slice(a, a + ROW_CHUNK)
                    w_bufs[par][hh, rows, :] = jnp.exp(e_bufs[par][hh, rows, :] - r_old).astype(BF16)
                r_ref[hh, :, pl.ds(q0, t)] = r_old + tot_ref[par, hh]

        def values(i, par):
            q0, k0 = item(i)
            vt = vt_ref[0, :, pl.ds(k0, t)]
            zero = jnp.zeros_like(vt)
            lhs = jnp.concatenate([jnp.where(own_rows[hh], vt, zero) for hh in range(2)], axis=1)
            acc_ref[:, pl.ds(q0, t)] = (acc_ref[:, pl.ds(q0, t)]
                                        + _bdot(lhs, w_bufs[par][...].reshape(2 * t, t)))

        return [scores, drops, weights, values]

    _pipeline(n_diag, stages(0, True))
    _pipeline(n_below, stages(n_diag, False))

    def finish(tile, carry):
        q0 = pl.multiple_of(tile * t, t)
        o_ref[0, pl.ds(q0, t), :] = acc_ref[:, pl.ds(q0, t)].T.astype(o_ref.dtype)
        return carry

    lax.fori_loop(0, n_diag, finish, 0)


def _sb_attention_flat(qt, k, vt, *, t):
    b, s, _ = k.shape
    table, n_diag, n_below = _block_tables(s // t, t, descending=True)
    grid_spec = pltpu.PrefetchScalarGridSpec(
        num_scalar_prefetch=1,
        grid=(b, N_HEADS // 2),
        in_specs=[pl.BlockSpec((1, LANES, s), lambda i, h, tab: (i, h, 0)),
                  pl.BlockSpec((1, s, LANES), lambda i, h, tab: (i, 0, h)),
                  pl.BlockSpec((1, LANES, s), lambda i, h, tab: (i, h, 0))],
        out_specs=pl.BlockSpec((1, s, LANES), lambda i, h, tab: (i, 0, h)),
        scratch_shapes=[pltpu.VMEM((LANES, s), F32),
                        pltpu.VMEM((2, 1, s), F32),
                        pltpu.VMEM((2, 2, 1, t), F32),
                        pltpu.VMEM((2, t, t), BF16),
                        pltpu.VMEM((2, t, t), F32), pltpu.VMEM((2, t, t), F32),
                        pltpu.VMEM((2, t, t), F32), pltpu.VMEM((2, t, t), F32),
                        pltpu.VMEM((2, t, t), BF16), pltpu.VMEM((2, t, t), BF16)])
    return pl.pallas_call(
        functools.partial(_sb_flat_kernel, t=t, n_diag=n_diag, n_below=n_below),
        grid_spec=grid_spec,
        out_shape=jax.ShapeDtypeStruct((b, s, HEAD_W), BF16),
        compiler_params=_params(2),
        name="sb_attention",
    )(table, qt, k, vt)


def _softmax_attn_kernel(q_ref, k_ref, v_ref, o_ref, acc_ref, s0_ref, s1_ref, *, tq, tk,
                         chunk_shift):
    qi = pl.program_id(2)
    in_head = _head_masks()
    qs = [q_ref[0, :, LANES * hh:LANES * (hh + 1)] for hh in range(2)]
    qpos = qi * tq + lax.broadcasted_iota(jnp.int32, (1, tq), 1)
    acc_ref[...] = jnp.zeros_like(acc_ref)

    def produce(j, buf):
        start = pl.multiple_of(j * tk, tk)
        for hh in range(2):
            k = k_ref[0, pl.ds(start, tk), LANES * hh:LANES * (hh + 1)]
            buf[hh] = _dot_nt(k, qs[hh])

    def consume(j, buf, m, masked):
        start = pl.multiple_of(j * tk, tk)
        v = v_ref[0, pl.ds(start, tk), :]
        m_out = []
        for hh in range(2):
            s = buf[hh]
            if masked:
                kpos = start + lax.broadcasted_iota(jnp.int32, (tk, 1), 0)
                s = jnp.where((kpos >> chunk_shift) <= (qpos >> chunk_shift), s, NEG)
            m_new = jnp.maximum(m[hh], _colmax(s))
            p = jnp.exp(s - m_new)
            alpha = jnp.exp(m[hh] - m_new)
            v1 = jnp.where(in_head[hh], v, jnp.ones_like(v))
            acc_ref[hh] = acc_ref[hh] * alpha + _dot_tn(v1, p.astype(BF16))
            m_out.append(m_new)
        return tuple(m_out)

    n_last = (qi * tq + tq - 1) // tk
    n_pairs = n_last // 2

    def body(i, m):
        j = 2 * i
        produce(j + 1, s1_ref)
        m = consume(j, s0_ref, m, False)
        produce(j + 2, s0_ref)
        return consume(j + 1, s1_ref, m, False)

    produce(0, s0_ref)
    m = (jnp.full((1, tq), NEG, F32),) * 2
    m = lax.fori_loop(0, n_pairs, body, m)
    j = 2 * n_pairs

    @pl.when(n_last % 2 == 1)
    def _():
        produce(j + 1, s1_ref)
        consume(j + 1, s1_ref, consume(j, s0_ref, m, False), True)

    @pl.when(n_last % 2 == 0)
    def _():
        consume(j, s0_ref, m, True)

    sub = lax.broadcasted_iota(jnp.int32, (LANES, 1), 0)
    outs = []
    for hh in range(2):
        acc = acc_ref[hh]
        denom = acc[HEAD_DIM * (1 - hh):HEAD_DIM * (1 - hh) + 1, :]
        outs.append(acc / denom)
    o_ref[0] = jnp.where(sub < HEAD_DIM, outs[0], outs[1]).T.astype(o_ref.dtype)


def _softmax_attention(q, k, v, *, tq, tk, chunk_shift, name):
    b, s, _ = q.shape
    return pl.pallas_call(
        functools.partial(_softmax_attn_kernel, tq=tq, tk=tk, chunk_shift=chunk_shift),
        grid=(b, N_HEADS // 2, s // tq),
        in_specs=[pl.BlockSpec((1, tq, 2 * LANES), lambda i, h, j: (i, j, h)),
                  pl.BlockSpec((1, s, 2 * LANES), lambda i, h, j: (i, 0, h)),
                  pl.BlockSpec((1, s, LANES), lambda i, h, j: (i, 0, h))],
        out_specs=pl.BlockSpec((1, tq, LANES), lambda i, h, j: (i, j, h)),
        out_shape=jax.ShapeDtypeStruct((b, s, HEAD_W), BF16),
        scratch_shapes=[pltpu.VMEM((2, LANES, tq), F32), pltpu.VMEM((2, tk, tq), F32),
                        pltpu.VMEM((2, tk, tq), F32)],
        compiler_params=_params(3),
        name=name,
    )(q, k, v)


def _sb_kernel(q_ref, k_ref, v_ref, o_ref, acc_ref, z_ref, r_ref, *, t):
    qi = pl.program_id(2)
    in_head = _head_masks()
    q = q_ref[0]
    qs = [jnp.where(in_head[hh], q, jnp.zeros_like(q)) for hh in range(2)]
    qpos = lax.broadcasted_iota(jnp.int32, (1, t), 1)
    kpos = lax.broadcasted_iota(jnp.int32, (t, 1), 0)
    ss = lax.broadcasted_iota(jnp.int32, (t, 2 * t), 0)
    jj = lax.broadcasted_iota(jnp.int32, (t, 2 * t), 1) & (t - 1)
    from_here = jnp.where(jj >= ss, 1.0, 0.0).astype(BF16)
    acc_ref[...] = jnp.zeros_like(acc_ref)
    r_ref[...] = jnp.zeros_like(r_ref)

    def key_start(blk):
        return pl.multiple_of(jnp.maximum(blk, 0) * t, t)

    def scores(blk_hi, nblk):
        for u in range(nblk):
            k = k_ref[0, pl.ds(key_start(blk_hi - u), t), :]
            for hh in range(2):
                z_ref[u, hh] = _dot_nt(k, qs[hh])

    def group(blk_hi, nblk, diag_first, next_hi):
        chains = []
        for u in range(nblk):
            for hh in range(2):
                z = z_ref[u, hh]
                drop = jnp.maximum(z, 0.0) + jnp.log(1.0 + jnp.exp(-jnp.abs(z)))
                if diag_first and u == 0:
                    ok = kpos < qpos
                    drop = jnp.where(ok, drop, 0.0)
                hi = drop.astype(BF16)
                lo = (drop - hi.astype(F32)).astype(BF16)
                incl = _bdot(from_here, jnp.concatenate([hi, lo], axis=0))
                e = z - incl
                if diag_first and u == 0:
                    e = jnp.where(ok, e, NEG)
                chains.append((u, hh, e, incl[0:1, :]))
        scores(next_hi, 2)
        r = [r_ref[hh] for hh in range(2)]
        upd = None
        for u, hh, e, total in chains:
            w = jnp.exp(e - r[hh])
            v = v_ref[0, pl.ds(key_start(blk_hi - u), t), :]
            vh = jnp.where(in_head[hh], v, jnp.zeros_like(v))
            pv = _dot_tn(vh, w.astype(BF16))
            upd = pv if upd is None else upd + pv
            r[hh] = r[hh] + total
        acc_ref[...] = acc_ref[...] + upd
        for hh in range(2):
            r_ref[hh] = r[hh]

    @pl.when(qi % 2 == 0)
    def _():
        scores(qi, 1)
        group(qi, 1, True, qi - 1)

    @pl.when(qi % 2 == 1)
    def _():
        scores(qi, 2)
        group(qi, 2, True, qi - 2)

    n_rest = qi // 2
    top = 2 * n_rest - 1

    def body(i, carry):
        group(top - 2 * i, 2, False, top - 2 * i - 2)
        return carry

    lax.fori_loop(0, n_rest, body, 0)
    o_ref[0] = acc_ref[...].T.astype(o_ref.dtype)


def _sb_attention(qkv, *, t):
    b, s, _ = qkv.shape
    return pl.pallas_call(
        functools.partial(_sb_kernel, t=t),
        grid=(b, N_HEADS // 2, s // t),
        in_specs=[pl.BlockSpec((1, t, LANES), lambda i, h, j: (i, j, h)),
                  pl.BlockSpec((1, s, LANES), lambda i, h, j: (i, 0, 4 + h)),
                  pl.BlockSpec((1, s, LANES), lambda i, h, j: (i, 0, 8 + h))],
        out_specs=pl.BlockSpec((1, t, LANES), lambda i, h, j: (i, j, h)),
        out_shape=jax.ShapeDtypeStruct((b, s, HEAD_W), BF16),
        scratch_shapes=[pltpu.VMEM((LANES, t), F32), pltpu.VMEM((2, 2, t, t), F32),
                        pltpu.VMEM((2, 1, t), F32)],
        compiler_params=_params(3),
        name="sb_attention",
    )(qkv, qkv, qkv)


def _merge_kernel(x_ref, yf_ref, ym_ref, ys_ref, g_ref, mod_ref, wf_ref, wm_ref, ws_ref,
                  wo_ref, o_ref):
    d = D_MODEL
    merged = (g_ref[0, :, 0:d].astype(F32) * _bdot(yf_ref[0], wf_ref[...])
              + g_ref[0, :, d:2 * d].astype(F32) * _bdot(ym_ref[0], wm_ref[...])
              + g_ref[0, :, 2 * d:3 * d].astype(F32) * _bdot(ys_ref[0], ws_ref[...]))
    o_ref[0] = x_ref[0] + mod_ref[0, 2:3, :] * _bdot(merged.astype(BF16), wo_ref[...])


def _merge(x, y_fox, y_mla, y_sb, gates, mod, wf, wm, ws, wo, *, tm):
    b, s, d = x.shape
    row = lambda w: pl.BlockSpec((1, tm, w), lambda i, j: (i, j, 0))
    return pl.pallas_call(
        _merge_kernel,
        grid=(b, s // tm),
        in_specs=[row(d), row(HEAD_W), row(HEAD_W), row(HEAD_W), row(3 * d),
                  pl.BlockSpec((1, 6, d), lambda i, j: (i, 0, 0)),
                  _resident((HEAD_W, d)), _resident((HEAD_W, d)), _resident((HEAD_W, d)),
                  _resident((d, d))],
        out_specs=row(d),
        out_shape=jax.ShapeDtypeStruct((b, s, d), F32),
        compiler_params=_params(2),
        name="merge_out",
    )(x, y_fox, y_mla, y_sb, gates, mod, wf, wm, ws, wo)


def _ffn_kernel(x_ref, mod_ref, g_ref, wg_ref, wu_ref, wd_ref, gfin_ref, o_ref, *, final, fc):
    x = x_ref[0]
    sh, sc, gt = mod_ref[0, 3:4, :], mod_ref[0, 4:5, :], mod_ref[0, 5:6, :]
    u = ((_rms(x) * g_ref[...]) * (1.0 + sc) + sh).astype(BF16)
    acc = None
    for a in range(0, D_FF, fc):
        hg = _bdot(u, wg_ref[:, a:a + fc])
        hu = _bdot(u, wu_ref[:, a:a + fc])
        h = ((hg * _sigmoid(hg)) * hu).astype(BF16)
        part = _bdot(h, wd_ref[a:a + fc, :])
        acc = part if acc is None else acc + part
    out = x + gt * acc
    if final:
        out = _rms(out) * gfin_ref[...]
    o_ref[0] = out


def _ffn(x, mod, g_ffn, wg, wu, wd, g_final, *, tm, final):
    b, s, d = x.shape
    row = lambda w: pl.BlockSpec((1, tm, w), lambda i, j: (i, j, 0))
    return pl.pallas_call(
        functools.partial(_ffn_kernel, final=final, fc=D_FF // 2),
        grid=(b, s // tm),
        in_specs=[row(d), pl.BlockSpec((1, 6, d), lambda i, j: (i, 0, 0)), _resident((1, d)),
                  _resident((d, D_FF)), _resident((d, D_FF)), _resident((D_FF, d)),
                  _resident((1, d))],
        out_specs=row(d),
        out_shape=jax.ShapeDtypeStruct((b, s, d), F32),
        compiler_params=_params(2),
        name="ffn",
    )(x, mod, g_ffn, wg, wu, wd, g_final)


def _slot_cols(w, width_per_head, used):
    k = w.shape[0]
    w = w.reshape(k, N_HEADS, width_per_head)[:, :, :used]
    return jnp.pad(w, ((0, 0), (0, 0), (0, LANES - used))).reshape(k, SLOT_W)


def _layout_w_in(w):
    d = w.shape[0]
    fox, ff, ql, kvl, kr, sb, gate = jnp.split(
        w, [1536, 1544, 1928, 2184, 2216, 3752], axis=-1)
    z = lambda n: jnp.zeros((d, n), w.dtype)
    fq, fk, fv = jnp.split(fox, 3, axis=-1)
    kr_slot = jnp.concatenate([z(MLA_NOPE), kr, z(LANES - MLA_NOPE - MLA_ROPE)], axis=-1)
    ff_slot = jnp.concatenate([ff, z(LANES - N_HEADS)], axis=-1)
    cols = [_slot_cols(fq, HEAD_DIM, HEAD_DIM), _slot_cols(fk, HEAD_DIM, HEAD_DIM), fv, sb, gate,
            ql, kvl, kr_slot, ff_slot]
    return jnp.concatenate(cols, axis=-1).astype(BF16)


def kernel(x, c, positions, g_mix, w_ada, b_ada, w_in, b_fox_f, g_mla_q, w_mla_uq, g_mla_kv,
           w_mla_ukv, w_o_fox, w_o_mla, w_o_sb, w_out, g_ffn, w_ffn_gate, w_ffn_up, w_ffn_down,
           g_final):
    b, s, d = x.shape
    depth = w_in.shape[0]
    tm = min(s, 512)
    ta = min(s, 512)
    tsb = min(s, 256)

    mod = _ada(c, w_ada, b_ada).reshape(depth, b, 6, d)
    tables = _rope_tables(positions)
    for l in range(depth):
        w_in_r = _layout_w_in(w_in[l])
        wuq_r = _slot_cols(w_mla_uq[l], MLA_NOPE + MLA_ROPE, MLA_NOPE + MLA_ROPE).astype(BF16)
        wkn_r = _slot_cols(w_mla_ukv[l], 2 * HEAD_DIM, MLA_NOPE).astype(BF16)
        wkv_r = w_mla_ukv[l].reshape(MLA_KV_RANK, N_HEADS, 2 * HEAD_DIM)[:, :, MLA_NOPE:]
        wkv_r = wkv_r.reshape(MLA_KV_RANK, HEAD_W).astype(BF16)
        bff = jnp.pad(b_fox_f[l], (0, LANES - N_HEADS)).reshape(1, LANES)

        fox_qt, fox_k, fox_vt, sb_qt, sb_k, sb_vt, gates, mqt, mk, mvt = _inproj(
            x, mod[l], g_mix[l].reshape(1, d), w_in_r, bff, g_mla_q[l].reshape(1, -1),
            g_mla_kv[l].reshape(1, -1), wuq_r, wkn_r, wkv_r, tables, tm=tm)
        y_fox = _softmax_attention_flat(fox_qt, fox_k, fox_vt, t=ta, chunk_shift=0,
                                        name="fox_attention")
        y_mla = _softmax_attention_flat(mqt, mk, mvt, t=ta, chunk_shift=CHUNK_SHIFT,
                                        name="mla_attention")
        y_sb = _sb_attention_flat(sb_qt, sb_k, sb_vt, t=tsb)
        x = _merge(x, y_fox, y_mla, y_sb, gates, mod[l], w_o_fox[l].astype(BF16),
                   w_o_mla[l].astype(BF16), w_o_sb[l].astype(BF16), w_out[l].astype(BF16), tm=tm)
        x = _ffn(x, mod[l], g_ffn[l].reshape(1, d), w_ffn_gate[l].astype(BF16),
                 w_ffn_up[l].astype(BF16), w_ffn_down[l].astype(BF16), g_final.reshape(1, d),
                 tm=tm, final=(l == depth - 1))
    return x
```

```python
import functools

import numpy as np
import jax
import jax.numpy as jnp
from jax import lax
from jax.experimental import pallas as pl
from jax.experimental.pallas import tpu as pltpu

F32 = jnp.float32
BF16 = jnp.bfloat16

D_MODEL = 1024
EPS = 1e-6
HEAD_DIM = 64
N_HEADS = 8
HEAD_W = N_HEADS * HEAD_DIM
CHUNK_SHIFT = 6
MLA_Q_RANK = 384
MLA_KV_RANK = 256
MLA_NOPE = 64
MLA_ROPE = 32
ROPE_HALF = MLA_ROPE // 2
ROPE_BASE = 10000.0
MLA_SCALE = (MLA_NOPE + MLA_ROPE) ** -0.5
QK_SCALE = HEAD_DIM ** -0.5
D_FF = 2816
LANES = 128
SLOT_W = N_HEADS * LANES
N_PIECES = 3

C_FQ = 0
C_FK = 512
C_FV = 1024
C_SB = 1536
C_GATE = 3072
C_QL = 6144
C_KVL = 6528
C_KR = 6784
C_FF = 6912
IN_W = 7040

ROW_CHUNK = 64
PIPELINE_UNROLL = 2
SIGN_BIT = np.uint32(0x80000000)
NEG = -1e30
VMEM_LIMIT = 56 * 1024 * 1024


def _params(n_grid):
    return pltpu.CompilerParams(dimension_semantics=("arbitrary",) * n_grid,
                                vmem_limit_bytes=VMEM_LIMIT)


def _resident(shape):
    nd = len(shape)
    return pl.BlockSpec(shape, lambda *_: (0,) * nd, pipeline_mode=pl.Buffered(1))


def _bdot(a, b):
    return jnp.dot(a, b, preferred_element_type=F32)


def _dot_nt(a, b):
    return lax.dot_general(a, b, (((1,), (1,)), ((), ())), preferred_element_type=F32)


def _dot_tn(a, b):
    return lax.dot_general(a, b, (((0,), (0,)), ((), ())), preferred_element_type=F32)


def _split3(v):
    hi = v.astype(BF16)
    r = v - hi.astype(F32)
    mid = r.astype(BF16)
    lo = (r - mid.astype(F32)).astype(BF16)
    return hi, mid, lo


def _log_sigmoid(z):
    return jnp.minimum(z, 0.0) - jnp.log1p(jnp.exp(-jnp.abs(z)))


def _sigmoid(z):
    return 1.0 / (1.0 + jnp.exp(-z))


def _rms(x):
    return x * lax.rsqrt(jnp.mean(x * x, axis=-1, keepdims=True) + EPS)


def _ada_kernel(c_ref, w_ref, b_ref, o_ref):
    c = c_ref[...]
    cond = c * _sigmoid(c)
    chi, cmid, clo = _split3(cond)
    whi, wmid, wlo = _split3(w_ref[0])
    acc = _bdot(chi, whi) + (_bdot(chi, wmid) + _bdot(cmid, whi))
    acc = acc + (_bdot(chi, wlo) + _bdot(clo, whi) + _bdot(cmid, wmid))
    o_ref[0] = acc + b_ref[0]


def _ada(c, w_ada, b_ada):
    depth, d, n = w_ada.shape
    b = c.shape[0]
    tn = 1024
    return pl.pallas_call(
        _ada_kernel,
        grid=(depth, n // tn),
        in_specs=[pl.BlockSpec((b, d), lambda l, j: (0, 0)),
                  pl.BlockSpec((1, d, tn), lambda l, j: (l, 0, j)),
                  pl.BlockSpec((1, 1, tn), lambda l, j: (l, 0, j))],
        out_specs=pl.BlockSpec((1, b, tn), lambda l, j: (l, 0, j)),
        out_shape=jax.ShapeDtypeStruct((depth, b, n), F32),
        compiler_params=_params(2),
        name="adaln",
    )(c, w_ada, b_ada.reshape(depth, 1, n))


def _rope_kernel(pos_ref, inv_ref, cos_ref, sina_ref, sinb_ref):
    ang = pos_ref[0].astype(F32) * inv_ref[...]
    lane = lax.broadcasted_iota(jnp.int32, (1, LANES), 1)
    sin = jnp.sin(ang)
    cos_ref[0] = jnp.cos(ang)
    lo, mid, hi = MLA_NOPE, MLA_NOPE + ROPE_HALF, MLA_NOPE + MLA_ROPE
    sina_ref[0] = jnp.where((lane >= lo) & (lane < mid), -sin, 0.0)
    sinb_ref[0] = jnp.where((lane >= mid) & (lane < hi), sin, 0.0)


def _rope_tables(positions):
    b, s = positions.shape
    ts = min(s, 512)
    inv = ROPE_BASE ** (-jnp.arange(ROPE_HALF, dtype=F32) / ROPE_HALF)
    inv_slot = jnp.concatenate([jnp.zeros((MLA_NOPE,), F32), inv, inv,
                                jnp.zeros((LANES - MLA_NOPE - MLA_ROPE,), F32)]).reshape(1, LANES)
    spec = pl.BlockSpec((1, ts, LANES), lambda i, j: (i, j, 0))
    return pl.pallas_call(
        _rope_kernel,
        grid=(b, s // ts),
        in_specs=[pl.BlockSpec((1, ts, 1), lambda i, j: (i, j, 0)),
                  pl.BlockSpec((1, LANES), lambda i, j: (0, 0))],
        out_specs=[spec, spec, spec],
        out_shape=[jax.ShapeDtypeStruct((b, s, LANES), F32)] * 3,
        compiler_params=_params(2),
        name="rope_tables",
    )(positions.reshape(b, s, 1), inv_slot)


def _rope_slot(x, cos, sina, sinb):
    return (x * cos + pltpu.roll(x, LANES - ROPE_HALF, 1) * sina
            + pltpu.roll(x, ROPE_HALF, 1) * sinb)


def _inproj_kernel(x_ref, mod_ref, g_ref, w_ref, bff_ref, gq_ref, gkv_ref, wuq_ref, wkn_ref,
                   wkv_ref, cos_ref, sina_ref, sinb_ref, place_ref, qc_ref, kc_ref,
                   fq_ref, fk_ref, fv_ref, sb_ref, sv_ref, gate_ref, mq_ref, mk_ref, mv_ref,
                   carry_ref, *, tm):
    x = x_ref[0]
    sh, sc = mod_ref[0, 0:1, :], mod_ref[0, 1:2, :]
    u = ((_rms(x) * g_ref[...]) * (1.0 + sc) + sh).astype(BF16)

    def seg(a, b):
        return _bdot(u, w_ref[:, a:b])

    @pl.when(pl.program_id(1) == 0)
    def _():
        carry_ref[...] = jnp.zeros_like(carry_ref)

    logf = _log_sigmoid(seg(C_FF, IN_W) + bff_ref[...])
    r = lax.broadcasted_iota(jnp.int32, (tm, tm), 0)
    c = lax.broadcasted_iota(jnp.int32, (tm, tm), 1)
    tri = jnp.where(r >= c, 1.0, 0.0).astype(BF16)
    hi, mid, lo = _split3(logf)
    cum = (_bdot(tri, hi) + _bdot(tri, mid) + _bdot(tri, lo)) + carry_ref[...]
    carry_ref[...] = cum[tm - 1:tm, :]
    p0, p1, p2 = (p.astype(F32) for p in _split3(cum))
    lane = lax.broadcasted_iota(jnp.int32, (1, LANES), 1)
    pieces = jnp.where(lane < N_HEADS, p0,
                       jnp.where(lane < 2 * N_HEADS, pltpu.roll(p1, N_HEADS, 1),
                                 jnp.where(lane < 3 * N_HEADS, pltpu.roll(p2, 2 * N_HEADS, 1),
                                           0.0))).astype(BF16)

    def to_slots(compact, placed, const_ref, out_ref):
        for h in range(N_HEADS):
            x = compact[:, LANES * (h // 2):LANES * (h // 2 + 1)]
            head = pltpu.roll(x, HEAD_DIM, 1) if h % 2 else x
            slot = slice(LANES * h, LANES * (h + 1))
            extra = placed[:, slot] + const_ref[:, slot]
            out_ref[0, :, slot] = jnp.where(lane < HEAD_DIM, head, extra).astype(BF16)

    to_slots(seg(C_FQ, C_FK) * QK_SCALE, _bdot(pieces, place_ref[:, :SLOT_W]), qc_ref, fq_ref)
    to_slots(seg(C_FK, C_FV), _bdot(pieces, place_ref[:, SLOT_W:]), kc_ref, fk_ref)
    fv_ref[0] = seg(C_FV, C_SB).astype(BF16).T

    sb_ref[0, :, 0:512] = (seg(C_SB, C_SB + 512) * QK_SCALE).astype(BF16)
    sb_ref[0, :, 512:1024] = seg(C_SB + 512, C_SB + 1024).astype(BF16)
    sv_ref[0] = seg(C_SB + 1024, C_GATE).astype(BF16).T
    for c in range(6):
        a = 512 * c
        gate_ref[0, :, a:a + 512] = _sigmoid(seg(C_GATE + a, C_GATE + a + 512)).astype(BF16)

    cos, sina, sinb = cos_ref[0], sina_ref[0], sinb_ref[0]
    cq = (_rms(seg(C_QL, C_KVL)) * gq_ref[...]).astype(BF16)
    ckv = (_rms(seg(C_KVL, C_KR)) * gkv_ref[...]).astype(BF16)
    kr = _rope_slot(seg(C_KR, C_FF), cos, sina, sinb)
    for h in range(N_HEADS):
        a = LANES * h
        q = _rope_slot(_bdot(cq, wuq_ref[:, a:a + LANES]), cos, sina, sinb)
        mq_ref[0, :, a:a + LANES] = (q * MLA_SCALE).astype(BF16)
        mk_ref[0, :, a:a + LANES] = (_bdot(ckv, wkn_ref[:, a:a + LANES]) + kr).astype(BF16)
    mv_ref[0] = _bdot(ckv, wkv_ref[...]).astype(BF16).T


def _fox_placement():
    place = np.zeros((LANES, 2 * SLOT_W), np.float32)
    qc = np.zeros((1, SLOT_W), np.float32)
    kc = np.zeros((1, SLOT_W), np.float32)
    for h in range(N_HEADS):
        for p in range(N_PIECES):
            place[N_HEADS * p + h, LANES * h + HEAD_DIM + N_PIECES + p] = 1.0
            place[N_HEADS * p + h, SLOT_W + LANES * h + HEAD_DIM + p] = 1.0
            qc[0, LANES * h + HEAD_DIM + p] = -1.0
            kc[0, LANES * h + HEAD_DIM + N_PIECES + p] = 1.0
    return jnp.asarray(place, BF16), jnp.asarray(qc), jnp.asarray(kc)


def _inproj(x, mod, g_mix, w_in_r, bff, g_q, g_kv, wuq_r, wkn_r, wkv_r, tables, *, tm):
    b, s, d = x.shape
    cos, sina, sinb = tables
    place, qc, kc = _fox_placement()
    row = lambda w: pl.BlockSpec((1, tm, w), lambda i, j: (i, j, 0))
    col = lambda w: pl.BlockSpec((1, w, tm), lambda i, j: (i, 0, j))
    outs = [(SLOT_W, False),
            (SLOT_W, False),
            (HEAD_W, True),
            (2 * HEAD_W, False),
            (HEAD_W, True),
            (3 * D_MODEL, False),
            (SLOT_W, False),
            (SLOT_W, False),
            (HEAD_W, True)]
    out_specs = [col(w) if tr else row(w) for w, tr in outs]
    out_shape = [jax.ShapeDtypeStruct((b, w, s) if tr else (b, s, w), BF16) for w, tr in outs]
    return pl.pallas_call(
        functools.partial(_inproj_kernel, tm=tm),
        grid=(b, s // tm),
        in_specs=[row(d),
                  pl.BlockSpec((1, 6, d), lambda i, j: (i, 0, 0)),
                  _resident((1, d)), _resident((d, IN_W)), _resident((1, LANES)),
                  _resident((1, MLA_Q_RANK)), _resident((1, MLA_KV_RANK)),
                  _resident((MLA_Q_RANK, SLOT_W)), _resident((MLA_KV_RANK, SLOT_W)),
                  _resident((MLA_KV_RANK, HEAD_W)),
                  row(LANES), row(LANES), row(LANES),
                  _resident((LANES, 2 * SLOT_W)), _resident((1, SLOT_W)), _resident((1, SLOT_W))],
        out_specs=out_specs,
        out_shape=out_shape,
        scratch_shapes=[pltpu.VMEM((1, LANES), F32)],
        compiler_params=_params(2),
        name="inproj",
    )(x, mod, g_mix, w_in_r, bff, g_q, g_kv, wuq_r, wkn_r, wkv_r, cos, sina, sinb, place, qc, kc)


def _head_rows():
    row = lax.broadcasted_iota(jnp.int32, (LANES, 1), 0)
    return [(row >= HEAD_DIM * hh) & (row < HEAD_DIM * (hh + 1)) for hh in range(2)]


def _head_masks():
    lane = lax.broadcasted_iota(jnp.int32, (1, LANES), 1)
    return [(lane >= HEAD_DIM * hh) & (lane < HEAD_DIM * (hh + 1)) for hh in range(2)]


def _col_reduce(x, op, final, chunk=64):
    rows = x.shape[0]
    if rows > chunk and rows % chunk == 0:
        acc = x[:chunk]
        for a in range(chunk, rows, chunk):
            acc = op(acc, x[a:a + chunk])
        x, rows = acc, chunk
    while rows > 8 and rows % 16 == 0:
        rows //= 2
        x = op(x[:rows], x[rows:])
    return final(x, axis=0, keepdims=True)


def _colmax(x):
    return _col_reduce(x, jnp.maximum, jnp.max)


def _colsum(x):
    return _col_reduce(x, jnp.add, jnp.sum)


def _pipeline(n, stages, unroll=PIPELINE_UNROLL):
    depth = len(stages)

    def static_step(t):
        for s, stage in enumerate(stages):
            if 0 <= t - s < n:
                stage(t - s, (t - s) % 2)

    first_full = depth - 1
    n_full = max(n - depth + 1, 0)
    for t in range(min(first_full, n + depth - 1)):
        static_step(t)
    n_loops = n_full // unroll
    if n_loops >= 1:
        def body(it, carry):
            t = first_full + unroll * it
            for u in range(unroll):
                for s, stage in enumerate(stages):
                    stage(t + u - s, (first_full + u - s) % 2)
            return carry

        lax.fori_loop(0, n_loops, body, 0)
    for t in range(first_full + unroll * n_loops, n + depth - 1):
        static_step(t)


def _block_tables(n_tiles, t, descending):
    diag = [(i * t, i * t) for i in range(n_tiles)]
    below = []
    for i in range(1, n_tiles):
        js = range(i - 1, -1, -1) if descending else range(i)
        below += [(i * t, j * t) for j in js]
    return jnp.asarray(np.array(diag + below, np.int32).T), len(diag), len(below)


def _softmax_flat_kernel(tab_ref, q_ref, k_ref, vt_ref, o_ref, acc_ref, m_ref, al_ref, top_ref,
                         s0_ref, s1_ref, p0_ref, p1_ref, *, t, n_diag, n_below, chunk_shift):
    own_rows = _head_rows()
    s_bufs, p_bufs = (s0_ref, s1_ref), (p0_ref, p1_ref)
    kio = lax.broadcasted_iota(jnp.int32, (t, 1), 0)
    qio = lax.broadcasted_iota(jnp.int32, (1, t), 1)
    acc_ref[...] = jnp.zeros_like(acc_ref)
    m_ref[...] = jnp.full_like(m_ref, NEG)

    def stages(base, diag):
        def item(i):
            return (pl.multiple_of(tab_ref[0, base + i], t), pl.multiple_of(tab_ref[1, base + i], t))

        def scores(i, par):
            q0, k0 = item(i)
            for hh in range(2):
                slot = slice(LANES * hh, LANES * (hh + 1))
                s = _dot_nt(k_ref[0, pl.ds(k0, t), slot],
                            q_ref[0, pl.ds(q0, t), slot])
                if diag:
                    s = jnp.where((kio >> chunk_shift) <= (qio >> chunk_shift), s, NEG)
                s_bufs[par][hh] = s
                top_ref[par, hh] = _colmax(s)

        def numerators(i, par):
            q0, _ = item(i)
            for hh in range(2):
                m_old = m_ref[hh, :, pl.ds(q0, t)]
                m_new = jnp.maximum(m_old, top_ref[par, hh])
                for a in range(0, t, ROW_CHUNK):
                    rows = slice(a, a + ROW_CHUNK)
                    p_bufs[par][hh, rows, :] = jnp.exp(s_bufs[par][hh, rows, :] - m_new).astype(BF16)
                al_ref[par, hh] = jnp.exp(m_old - m_new)
                m_ref[hh, :, pl.ds(q0, t)] = m_new

        def values(i, par):
            q0, k0 = item(i)
            vt = vt_ref[0, :, pl.ds(k0, t)]
            for hh in range(2):
                v1 = jnp.where(own_rows[hh], vt, jnp.ones_like(vt))
                acc_ref[hh, :, pl.ds(q0, t)] = (acc_ref[hh, :, pl.ds(q0, t)] * al_ref[par, hh]
                                                + _bdot(v1, p_bufs[par][hh]))

        return [scores, numerators, values]

    _pipeline(n_diag, stages(0, True))
    _pipeline(n_below, stages(n_diag, False))

    sub = lax.broadcasted_iota(jnp.int32, (LANES, 1), 0)

    def finish(tile, carry):
        q0 = pl.multiple_of(tile * t, t)
        outs = []
        for hh in range(2):
            acc = acc_ref[hh, :, pl.ds(q0, t)]
            denom = acc[HEAD_DIM * (1 - hh):HEAD_DIM * (1 - hh) + 1, :]
            outs.append(acc / denom)
        o_ref[0, pl.ds(q0, t), :] = jnp.where(sub < HEAD_DIM, outs[0], outs[1]).T.astype(o_ref.dtype)
        return carry

    lax.fori_loop(0, n_diag, finish, 0)


def _softmax_attention_flat(q, k, vt, *, t, chunk_shift, name):
    b, s, _ = k.shape
    table, n_diag, n_below = _block_tables(s // t, t, descending=False)
    grid_spec = pltpu.PrefetchScalarGridSpec(
        num_scalar_prefetch=1,
        grid=(b, N_HEADS // 2),
        in_specs=[pl.BlockSpec((1, s, 2 * LANES), lambda i, h, tab: (i, 0, h)),
                  pl.BlockSpec((1, s, 2 * LANES), lambda i, h, tab: (i, 0, h)),
                  pl.BlockSpec((1, LANES, s), lambda i, h, tab: (i, h, 0))],
        out_specs=pl.BlockSpec((1, s, LANES), lambda i, h, tab: (i, 0, h)),
        scratch_shapes=[pltpu.VMEM((2, LANES, s), F32),
                        pltpu.VMEM((2, 1, s), F32),
                        pltpu.VMEM((2, 2, 1, t), F32),
                        pltpu.VMEM((2, 2, 1, t), F32),
                        pltpu.VMEM((2, t, t), F32), pltpu.VMEM((2, t, t), F32),
                        pltpu.VMEM((2, t, t), BF16), pltpu.VMEM((2, t, t), BF16)])
    return pl.pallas_call(
        functools.partial(_softmax_flat_kernel, t=t, n_diag=n_diag, n_below=n_below,
                          chunk_shift=chunk_shift),
        grid_spec=grid_spec,
        out_shape=jax.ShapeDtypeStruct((b, s, HEAD_W), BF16),
        compiler_params=_params(2),
        name=name,
    )(table, q, k, vt)


def _sb_flat_kernel(tab_ref, q_ref, k_ref, vt_ref, o_ref, acc_ref, r_ref, tot_ref, d_ref,
                    z0_ref, z1_ref, e0_ref, e1_ref, w0_ref, w1_ref, *, t, n_diag, n_below):
    in_head = _head_masks()
    own_rows = _head_rows()
    z_bufs, e_bufs, w_bufs = (z0_ref, z1_ref), (e0_ref, e1_ref), (w0_ref, w1_ref)
    kio = lax.broadcasted_iota(jnp.int32, (t, 1), 0)
    qio = lax.broadcasted_iota(jnp.int32, (1, t), 1)
    ss = lax.broadcasted_iota(jnp.int32, (t, t), 0)
    jj = lax.broadcasted_iota(jnp.int32, (t, t), 1)
    from_here = jnp.where(jj >= ss, 1.0, 0.0).astype(BF16)
    acc_ref[...] = jnp.zeros_like(acc_ref)
    r_ref[...] = jnp.zeros_like(r_ref)

    def stages(base, diag):
        def item(i):
            return (pl.multiple_of(tab_ref[0, base + i], t), pl.multiple_of(tab_ref[1, base + i], t))

        def scores(i, par):
            q0, k0 = item(i)
            k = k_ref[0, pl.ds(k0, t), :]
            q = q_ref[0, pl.ds(q0, t), :]
            for hh in range(2):
                qh = jnp.where(in_head[hh], q, jnp.zeros_like(q))
                z_bufs[par][hh] = _dot_nt(k, qh)

        def drops(i, par):
            for hh in range(2):
                for a in range(0, t, ROW_CHUNK):
                    rows = slice(a, a + ROW_CHUNK)
                    z = z_bufs[par][hh, rows, :]
                    neg_abs = pltpu.bitcast(pltpu.bitcast(z, jnp.uint32) | SIGN_BIT, F32)
                    drop = jnp.maximum(z, 0.0) + jnp.log(1.0 + jnp.exp(neg_abs))
                    if diag:
                        drop = jnp.where(kio[rows] < qio, drop, 0.0)
                    d_ref[hh, rows, :] = drop.astype(BF16)
                incl = _bdot(from_here, d_ref[hh])
                for a in range(0, t, ROW_CHUNK):
                    rows = slice(a, a + ROW_CHUNK)
                    e = z_bufs[par][hh, rows, :] - incl[rows]
                    if diag:
                        e = jnp.where(kio[rows] < qio, e, NEG)
                    e_bufs[par][hh, rows, :] = e
                tot_ref[par, hh] = incl[0:1, :]

        def weights(i, par):
            q0, _ = item(i)
            for hh in range(2):
                r_old = r_ref[hh, :, pl.ds(q0, t)]
                for a in range(0, t, ROW_CHUNK):
                    rows = slice(a, a + ROW_CHUNK)
                    w_bufs[par][hh, rows, :] = jnp.exp(e_bufs[par][hh, rows, :] - r_old).astype(BF16)
                r_ref[hh, :, pl.ds(q0, t)] = r_old + tot_ref[par, hh]

        def values(i, par):
            q0, k0 = item(i)
            vt = vt_ref[0, :, pl.ds(k0, t)]
            zero = jnp.zeros_like(vt)
            lhs = jnp.concatenate([jnp.where(own_rows[hh], vt, zero) for hh in range(2)], axis=1)
            acc_ref[:, pl.ds(q0, t)] = (acc_ref[:, pl.ds(q0, t)]
                                        + _bdot(lhs, w_bufs[par][...].reshape(2 * t, t)))

        return [scores, drops, weights, values]

    _pipeline(n_diag, stages(0, True))
    _pipeline(n_below, stages(n_diag, False))

    def finish(tile, carry):
        q0 = pl.multiple_of(tile * t, t)
        o_ref[0, pl.ds(q0, t), :] = acc_ref[:, pl.ds(q0, t)].T.astype(o_ref.dtype)
        return carry

    lax.fori_loop(0, n_diag, finish, 0)


def _sb_attention_flat(qk, vt, *, t):
    b, s, _ = qk.shape
    table, n_diag, n_below = _block_tables(s // t, t, descending=True)
    grid_spec = pltpu.PrefetchScalarGridSpec(
        num_scalar_prefetch=1,
        grid=(b, N_HEADS // 2),
        in_specs=[pl.BlockSpec((1, s, LANES), lambda i, h, tab: (i, 0, h)),
                  pl.BlockSpec((1, s, LANES), lambda i, h, tab: (i, 0, 4 + h)),
                  pl.BlockSpec((1, LANES, s), lambda i, h, tab: (i, h, 0))],
        out_specs=pl.BlockSpec((1, s, LANES), lambda i, h, tab: (i, 0, h)),
        scratch_shapes=[pltpu.VMEM((LANES, s), F32),
                        pltpu.VMEM((2, 1, s), F32),
                        pltpu.VMEM((2, 2, 1, t), F32),
                        pltpu.VMEM((2, t, t), BF16),
                        pltpu.VMEM((2, t, t), F32), pltpu.VMEM((2, t, t), F32),
                        pltpu.VMEM((2, t, t), F32), pltpu.VMEM((2, t, t), F32),
                        pltpu.VMEM((2, t, t), BF16), pltpu.VMEM((2, t, t), BF16)])
    return pl.pallas_call(
        functools.partial(_sb_flat_kernel, t=t, n_diag=n_diag, n_below=n_below),
        grid_spec=grid_spec,
        out_shape=jax.ShapeDtypeStruct((b, s, HEAD_W), BF16),
        compiler_params=_params(2),
        name="sb_attention",
    )(table, qk, qk, vt)


def _softmax_attn_kernel(q_ref, k_ref, v_ref, o_ref, acc_ref, s0_ref, s1_ref, *, tq, tk,
                         chunk_shift):
    qi = pl.program_id(2)
    in_head = _head_masks()
    qs = [q_ref[0, :, LANES * hh:LANES * (hh + 1)] for hh in range(2)]
    qpos = qi * tq + lax.broadcasted_iota(jnp.int32, (1, tq), 1)
    acc_ref[...] = jnp.zeros_like(acc_ref)

    def produce(j, buf):
        start = pl.multiple_of(j * tk, tk)
        for hh in range(2):
            k = k_ref[0, pl.ds(start, tk), LANES * hh:LANES * (hh + 1)]
            buf[hh] = _dot_nt(k, qs[hh])

    def consume(j, buf, m, masked):
        start = pl.multiple_of(j * tk, tk)
        v = v_ref[0, pl.ds(start, tk), :]
        m_out = []
        for hh in range(2):
            s = buf[hh]
            if masked:
                kpos = start + lax.broadcasted_iota(jnp.int32, (tk, 1), 0)
                s = jnp.where((kpos >> chunk_shift) <= (qpos >> chunk_shift), s, NEG)
            m_new = jnp.maximum(m[hh], _colmax(s))
            p = jnp.exp(s - m_new)
            alpha = jnp.exp(m[hh] - m_new)
            v1 = jnp.where(in_head[hh], v, jnp.ones_like(v))
            acc_ref[hh] = acc_ref[hh] * alpha + _dot_tn(v1, p.astype(BF16))
            m_out.append(m_new)
        return tuple(m_out)

    n_last = (qi * tq + tq - 1) // tk
    n_pairs = n_last // 2

    def body(i, m):
        j = 2 * i
        produce(j + 1, s1_ref)
        m = consume(j, s0_ref, m, False)
        produce(j + 2, s0_ref)
        return consume(j + 1, s1_ref, m, False)

    produce(0, s0_ref)
    m = (jnp.full((1, tq), NEG, F32),) * 2
    m = lax.fori_loop(0, n_pairs, body, m)
    j = 2 * n_pairs

    @pl.when(n_last % 2 == 1)
    def _():
        produce(j + 1, s1_ref)
        consume(j + 1, s1_ref, consume(j, s0_ref, m, False), True)

    @pl.when(n_last % 2 == 0)
    def _():
        consume(j, s0_ref, m, True)

    sub = lax.broadcasted_iota(jnp.int32, (LANES, 1), 0)
    outs = []
    for hh in range(2):
        acc = acc_ref[hh]
        denom = acc[HEAD_DIM * (1 - hh):HEAD_DIM * (1 - hh) + 1, :]
        outs.append(acc / denom)
    o_ref[0] = jnp.where(sub < HEAD_DIM, outs[0], outs[1]).T.astype(o_ref.dtype)


def _softmax_attention(q, k, v, *, tq, tk, chunk_shift, name):
    b, s, _ = q.shape
    return pl.pallas_call(
        functools.partial(_softmax_attn_kernel, tq=tq, tk=tk, chunk_shift=chunk_shift),
        grid=(b, N_HEADS // 2, s // tq),
        in_specs=[pl.BlockSpec((1, tq, 2 * LANES), lambda i, h, j: (i, j, h)),
                  pl.BlockSpec((1, s, 2 * LANES), lambda i, h, j: (i, 0, h)),
                  pl.BlockSpec((1, s, LANES), lambda i, h, j: (i, 0, h))],
        out_specs=pl.BlockSpec((1, tq, LANES), lambda i, h, j: (i, j, h)),
        out_shape=jax.ShapeDtypeStruct((b, s, HEAD_W), BF16),
        scratch_shapes=[pltpu.VMEM((2, LANES, tq), F32), pltpu.VMEM((2, tk, tq), F32),
                        pltpu.VMEM((2, tk, tq), F32)],
        compiler_params=_params(3),
        name=name,
    )(q, k, v)


def _sb_kernel(q_ref, k_ref, v_ref, o_ref, acc_ref, z_ref, r_ref, *, t):
    qi = pl.program_id(2)
    in_head = _head_masks()
    q = q_ref[0]
    qs = [jnp.where(in_head[hh], q, jnp.zeros_like(q)) for hh in range(2)]
    qpos = lax.broadcasted_iota(jnp.int32, (1, t), 1)
    kpos = lax.broadcasted_iota(jnp.int32, (t, 1), 0)
    ss = lax.broadcasted_iota(jnp.int32, (t, 2 * t), 0)
    jj = lax.broadcasted_iota(jnp.int32, (t, 2 * t), 1) & (t - 1)
    from_here = jnp.where(jj >= ss, 1.0, 0.0).astype(BF16)
    acc_ref[...] = jnp.zeros_like(acc_ref)
    r_ref[...] = jnp.zeros_like(r_ref)

    def key_start(blk):
        return pl.multiple_of(jnp.maximum(blk, 0) * t, t)

    def scores(blk_hi, nblk):
        for u in range(nblk):
            k = k_ref[0, pl.ds(key_start(blk_hi - u), t), :]
            for hh in range(2):
                z_ref[u, hh] = _dot_nt(k, qs[hh])

    def group(blk_hi, nblk, diag_first, next_hi):
        chains = []
        for u in range(nblk):
            for hh in range(2):
                z = z_ref[u, hh]
                drop = jnp.maximum(z, 0.0) + jnp.log(1.0 + jnp.exp(-jnp.abs(z)))
                if diag_first and u == 0:
                    ok = kpos < qpos
                    drop = jnp.where(ok, drop, 0.0)
                hi = drop.astype(BF16)
                lo = (drop - hi.astype(F32)).astype(BF16)
                incl = _bdot(from_here, jnp.concatenate([hi, lo], axis=0))
                e = z - incl
                if diag_first and u == 0:
                    e = jnp.where(ok, e, NEG)
                chains.append((u, hh, e, incl[0:1, :]))
        scores(next_hi, 2)
        r = [r_ref[hh] for hh in range(2)]
        upd = None
        for u, hh, e, total in chains:
            w = jnp.exp(e - r[hh])
            v = v_ref[0, pl.ds(key_start(blk_hi - u), t), :]
            vh = jnp.where(in_head[hh], v, jnp.zeros_like(v))
            pv = _dot_tn(vh, w.astype(BF16))
            upd = pv if upd is None else upd + pv
            r[hh] = r[hh] + total
        acc_ref[...] = acc_ref[...] + upd
        for hh in range(2):
            r_ref[hh] = r[hh]

    @pl.when(qi % 2 == 0)
    def _():
        scores(qi, 1)
        group(qi, 1, True, qi - 1)

    @pl.when(qi % 2 == 1)
    def _():
        scores(qi, 2)
        group(qi, 2, True, qi - 2)

    n_rest = qi // 2
    top = 2 * n_rest - 1

    def body(i, carry):
        group(top - 2 * i, 2, False, top - 2 * i - 2)
        return carry

    lax.fori_loop(0, n_rest, body, 0)
    o_ref[0] = acc_ref[...].T.astype(o_ref.dtype)


def _sb_attention(qkv, *, t):
    b, s, _ = qkv.shape
    return pl.pallas_call(
        functools.partial(_sb_kernel, t=t),
        grid=(b, N_HEADS // 2, s // t),
        in_specs=[pl.BlockSpec((1, t, LANES), lambda i, h, j: (i, j, h)),
                  pl.BlockSpec((1, s, LANES), lambda i, h, j: (i, 0, 4 + h)),
                  pl.BlockSpec((1, s, LANES), lambda i, h, j: (i, 0, 8 + h))],
        out_specs=pl.BlockSpec((1, t, LANES), lambda i, h, j: (i, j, h)),
        out_shape=jax.ShapeDtypeStruct((b, s, HEAD_W), BF16),
        scratch_shapes=[pltpu.VMEM((LANES, t), F32), pltpu.VMEM((2, 2, t, t), F32),
                        pltpu.VMEM((2, 1, t), F32)],
        compiler_params=_params(3),
        name="sb_attention",
    )(qkv, qkv, qkv)


def _merge_kernel(x_ref, yf_ref, ym_ref, ys_ref, g_ref, mod_ref, wf_ref, wm_ref, ws_ref,
                  wo_ref, o_ref):
    d = D_MODEL
    merged = (g_ref[0, :, 0:d].astype(F32) * _bdot(yf_ref[0], wf_ref[...])
              + g_ref[0, :, d:2 * d].astype(F32) * _bdot(ym_ref[0], wm_ref[...])
              + g_ref[0, :, 2 * d:3 * d].astype(F32) * _bdot(ys_ref[0], ws_ref[...]))
    o_ref[0] = x_ref[0] + mod_ref[0, 2:3, :] * _bdot(merged.astype(BF16), wo_ref[...])


def _merge(x, y_fox, y_mla, y_sb, gates, mod, wf, wm, ws, wo, *, tm):
    b, s, d = x.shape
    row = lambda w: pl.BlockSpec((1, tm, w), lambda i, j: (i, j, 0))
    return pl.pallas_call(
        _merge_kernel,
        grid=(b, s // tm),
        in_specs=[row(d), row(HEAD_W), row(HEAD_W), row(HEAD_W), row(3 * d),
                  pl.BlockSpec((1, 6, d), lambda i, j: (i, 0, 0)),
                  _resident((HEAD_W, d)), _resident((HEAD_W, d)), _resident((HEAD_W, d)),
                  _resident((d, d))],
        out_specs=row(d),
        out_shape=jax.ShapeDtypeStruct((b, s, d), F32),
        compiler_params=_params(2),
        name="merge_out",
    )(x, y_fox, y_mla, y_sb, gates, mod, wf, wm, ws, wo)


def _ffn_kernel(x_ref, mod_ref, g_ref, wg_ref, wu_ref, wd_ref, gfin_ref, o_ref, *, final, fc):
    x = x_ref[0]
    sh, sc, gt = mod_ref[0, 3:4, :], mod_ref[0, 4:5, :], mod_ref[0, 5:6, :]
    u = ((_rms(x) * g_ref[...]) * (1.0 + sc) + sh).astype(BF16)
    acc = None
    for a in range(0, D_FF, fc):
        hg = _bdot(u, wg_ref[:, a:a + fc])
        hu = _bdot(u, wu_ref[:, a:a + fc])
        h = ((hg * _sigmoid(hg)) * hu).astype(BF16)
        part = _bdot(h, wd_ref[a:a + fc, :])
        acc = part if acc is None else acc + part
    out = x + gt * acc
    if final:
        out = _rms(out) * gfin_ref[...]
    o_ref[0] = out


def _ffn(x, mod, g_ffn, wg, wu, wd, g_final, *, tm, final):
    b, s, d = x.shape
    row = lambda w: pl.BlockSpec((1, tm, w), lambda i, j: (i, j, 0))
    return pl.pallas_call(
        functools.partial(_ffn_kernel, final=final, fc=D_FF // 2),
        grid=(b, s // tm),
        in_specs=[row(d), pl.BlockSpec((1, 6, d), lambda i, j: (i, 0, 0)), _resident((1, d)),
                  _resident((d, D_FF)), _resident((d, D_FF)), _resident((D_FF, d)),
                  _resident((1, d))],
        out_specs=row(d),
        out_shape=jax.ShapeDtypeStruct((b, s, d), F32),
        compiler_params=_params(2),
        name="ffn",
    )(x, mod, g_ffn, wg, wu, wd, g_final)


def _slot_cols(w, width_per_head, used):
    k = w.shape[0]
    w = w.reshape(k, N_HEADS, width_per_head)[:, :, :used]
    return jnp.pad(w, ((0, 0), (0, 0), (0, LANES - used))).reshape(k, SLOT_W)


def _layout_w_in(w):
    d = w.shape[0]
    fox, ff, ql, kvl, kr, sb, gate = jnp.split(
        w, [1536, 1544, 1928, 2184, 2216, 3752], axis=-1)
    z = lambda n: jnp.zeros((d, n), w.dtype)
    kr_slot = jnp.concatenate([z(MLA_NOPE), kr, z(LANES - MLA_NOPE - MLA_ROPE)], axis=-1)
    ff_slot = jnp.concatenate([ff, z(LANES - N_HEADS)], axis=-1)
    return jnp.concatenate([fox, sb, gate, ql, kvl, kr_slot, ff_slot], axis=-1).astype(BF16)


def kernel(x, c, positions, g_mix, w_ada, b_ada, w_in, b_fox_f, g_mla_q, w_mla_uq, g_mla_kv,
           w_mla_ukv, w_o_fox, w_o_mla, w_o_sb, w_out, g_ffn, w_ffn_gate, w_ffn_up, w_ffn_down,
           g_final):
    b, s, d = x.shape
    depth = w_in.shape[0]
    tm = min(s, 512)
    ta = min(s, 512)
    tsb = min(s, 256)

    mod = _ada(c, w_ada, b_ada).reshape(depth, b, 6, d)
    tables = _rope_tables(positions)
    for l in range(depth):
        w_in_r = _layout_w_in(w_in[l])
        wuq_r = _slot_cols(w_mla_uq[l], MLA_NOPE + MLA_ROPE, MLA_NOPE + MLA_ROPE).astype(BF16)
        wkn_r = _slot_cols(w_mla_ukv[l], 2 * HEAD_DIM, MLA_NOPE).astype(BF16)
        wkv_r = w_mla_ukv[l].reshape(MLA_KV_RANK, N_HEADS, 2 * HEAD_DIM)[:, :, MLA_NOPE:]
        wkv_r = wkv_r.reshape(MLA_KV_RANK, HEAD_W).astype(BF16)
        bff = jnp.pad(b_fox_f[l], (0, LANES - N_HEADS)).reshape(1, LANES)

        fox_q, fox_k, fox_vt, sb_qk, sb_vt, gates, mq, mk, mvt = _inproj(
            x, mod[l], g_mix[l].reshape(1, d), w_in_r, bff, g_mla_q[l].reshape(1, -1),
            g_mla_kv[l].reshape(1, -1), wuq_r, wkn_r, wkv_r, tables, tm=tm)
        y_fox = _softmax_attention_flat(fox_q, fox_k, fox_vt, t=ta, chunk_shift=0,
                                        name="fox_attention")
        y_mla = _softmax_attention_flat(mq, mk, mvt, t=ta, chunk_shift=CHUNK_SHIFT,
                                        name="mla_attention")
        y_sb = _sb_attention_flat(sb_qk, sb_vt, t=tsb)
        x = _merge(x, y_fox, y_mla, y_sb, gates, mod[l], w_o_fox[l].astype(BF16),
                   w_o_mla[l].astype(BF16), w_o_sb[l].astype(BF16), w_out[l].astype(BF16), tm=tm)
        x = _ffn(x, mod[l], g_ffn[l].reshape(1, d), w_ffn_gate[l].astype(BF16),
                 w_ffn_up[l].astype(BF16), w_ffn_down[l].astype(BF16), g_final.reshape(1, d),
                 tm=tm, final=(l == depth - 1))
    return x
```

```python
import functools

import numpy as np
import jax
import jax.numpy as jnp
from jax import lax
from jax.experimental import pallas as pl
from jax.experimental.pallas import tpu as pltpu

F32 = jnp.float32
BF16 = jnp.bfloat16

D_MODEL = 1024
EPS = 1e-6
HEAD_DIM = 64
N_HEADS = 8
HEAD_W = N_HEADS * HEAD_DIM
CHUNK_SHIFT = 6
MLA_Q_RANK = 384
MLA_KV_RANK = 256
MLA_NOPE = 64
MLA_ROPE = 32
ROPE_HALF = MLA_ROPE // 2
ROPE_BASE = 10000.0
MLA_SCALE = (MLA_NOPE + MLA_ROPE) ** -0.5
QK_SCALE = HEAD_DIM ** -0.5
D_FF = 2816
LANES = 128
SLOT_W = N_HEADS * LANES
N_PIECES = 3

C_FQ = 0
C_FK = 512
C_FV = 1024
C_SB = 1536
C_GATE = 3072
C_QL = 6144
C_KVL = 6528
C_KR = 6784
C_FF = 6912
IN_W = 7040

ROW_CHUNK = 64
PIPELINE_UNROLL = 2
SIGN_BIT = np.uint32(0x80000000)
UNDERFLOW = 105.0
NORM_MARGIN = 1.02
NEG = -1e30
VMEM_LIMIT = 56 * 1024 * 1024


def _params(n_grid):
    return pltpu.CompilerParams(dimension_semantics=("arbitrary",) * n_grid,
                                vmem_limit_bytes=VMEM_LIMIT)


def _resident(shape):
    nd = len(shape)
    return pl.BlockSpec(shape, lambda *_: (0,) * nd, pipeline_mode=pl.Buffered(1))


def _bdot(a, b):
    return jnp.dot(a, b, preferred_element_type=F32)


def _dot_nt(a, b):
    return lax.dot_general(a, b, (((1,), (1,)), ((), ())), preferred_element_type=F32)


def _dot_tn(a, b):
    return lax.dot_general(a, b, (((0,), (0,)), ((), ())), preferred_element_type=F32)


def _split3(v):
    hi = v.astype(BF16)
    r = v - hi.astype(F32)
    mid = r.astype(BF16)
    lo = (r - mid.astype(F32)).astype(BF16)
    return hi, mid, lo


def _log_sigmoid(z):
    return jnp.minimum(z, 0.0) - jnp.log1p(jnp.exp(-jnp.abs(z)))


def _sigmoid(z):
    return 1.0 / (1.0 + jnp.exp(-z))


def _rms(x):
    return x * lax.rsqrt(jnp.mean(x * x, axis=-1, keepdims=True) + EPS)


def _ada_kernel(c_ref, w_ref, b_ref, o_ref):
    c = c_ref[...]
    cond = c * _sigmoid(c)
    chi, cmid, clo = _split3(cond)
    whi, wmid, wlo = _split3(w_ref[0])
    acc = _bdot(chi, whi) + (_bdot(chi, wmid) + _bdot(cmid, whi))
    acc = acc + (_bdot(chi, wlo) + _bdot(clo, whi) + _bdot(cmid, wmid))
    o_ref[0] = acc + b_ref[0]


def _ada(c, w_ada, b_ada):
    depth, d, n = w_ada.shape
    b = c.shape[0]
    tn = 1024
    return pl.pallas_call(
        _ada_kernel,
        grid=(depth, n // tn),
        in_specs=[pl.BlockSpec((b, d), lambda l, j: (0, 0)),
                  pl.BlockSpec((1, d, tn), lambda l, j: (l, 0, j)),
                  pl.BlockSpec((1, 1, tn), lambda l, j: (l, 0, j))],
        out_specs=pl.BlockSpec((1, b, tn), lambda l, j: (l, 0, j)),
        out_shape=jax.ShapeDtypeStruct((depth, b, n), F32),
        compiler_params=_params(2),
        name="adaln",
    )(c, w_ada, b_ada.reshape(depth, 1, n))


def _rope_kernel(pos_ref, inv_ref, cos_ref, sina_ref, sinb_ref):
    ang = pos_ref[0].astype(F32) * inv_ref[...]
    lane = lax.broadcasted_iota(jnp.int32, (1, LANES), 1)
    sin = jnp.sin(ang)
    cos_ref[0] = jnp.cos(ang)
    lo, mid, hi = MLA_NOPE, MLA_NOPE + ROPE_HALF, MLA_NOPE + MLA_ROPE
    sina_ref[0] = jnp.where((lane >= lo) & (lane < mid), -sin, 0.0)
    sinb_ref[0] = jnp.where((lane >= mid) & (lane < hi), sin, 0.0)


def _rope_tables(positions):
    b, s = positions.shape
    ts = min(s, 512)
    inv = ROPE_BASE ** (-jnp.arange(ROPE_HALF, dtype=F32) / ROPE_HALF)
    inv_slot = jnp.concatenate([jnp.zeros((MLA_NOPE,), F32), inv, inv,
                                jnp.zeros((LANES - MLA_NOPE - MLA_ROPE,), F32)]).reshape(1, LANES)
    spec = pl.BlockSpec((1, ts, LANES), lambda i, j: (i, j, 0))
    return pl.pallas_call(
        _rope_kernel,
        grid=(b, s // ts),
        in_specs=[pl.BlockSpec((1, ts, 1), lambda i, j: (i, j, 0)),
                  pl.BlockSpec((1, LANES), lambda i, j: (0, 0))],
        out_specs=[spec, spec, spec],
        out_shape=[jax.ShapeDtypeStruct((b, s, LANES), F32)] * 3,
        compiler_params=_params(2),
        name="rope_tables",
    )(positions.reshape(b, s, 1), inv_slot)


def _rope_slot(x, cos, sina, sinb):
    return (x * cos + pltpu.roll(x, LANES - ROPE_HALF, 1) * sina
            + pltpu.roll(x, ROPE_HALF, 1) * sinb)


def _inproj_kernel(x_ref, mod_ref, g_ref, w_ref, bff_ref, gq_ref, gkv_ref, wuq_ref, wkn_ref,
                   wkv_ref, cos_ref, sina_ref, sinb_ref, place_ref, qc_ref, kc_ref,
                   fq_ref, fk_ref, fv_ref, sb_ref, sv_ref, gate_ref, mq_ref, mk_ref, mv_ref,
                   carry_ref, *, tm):
    x = x_ref[0]
    sh, sc = mod_ref[0, 0:1, :], mod_ref[0, 1:2, :]
    u = ((_rms(x) * g_ref[...]) * (1.0 + sc) + sh).astype(BF16)

    def seg(a, b):
        return _bdot(u, w_ref[:, a:b])

    @pl.when(pl.program_id(1) == 0)
    def _():
        carry_ref[...] = jnp.zeros_like(carry_ref)

    logf = _log_sigmoid(seg(C_FF, IN_W) + bff_ref[...])
    r = lax.broadcasted_iota(jnp.int32, (tm, tm), 0)
    c = lax.broadcasted_iota(jnp.int32, (tm, tm), 1)
    tri = jnp.where(r >= c, 1.0, 0.0).astype(BF16)
    hi, mid, lo = _split3(logf)
    cum = (_bdot(tri, hi) + _bdot(tri, mid) + _bdot(tri, lo)) + carry_ref[...]
    carry_ref[...] = cum[tm - 1:tm, :]
    p0, p1, p2 = (p.astype(F32) for p in _split3(cum))
    lane = lax.broadcasted_iota(jnp.int32, (1, LANES), 1)
    pieces = jnp.where(lane < N_HEADS, p0,
                       jnp.where(lane < 2 * N_HEADS, pltpu.roll(p1, N_HEADS, 1),
                                 jnp.where(lane < 3 * N_HEADS, pltpu.roll(p2, 2 * N_HEADS, 1),
                                           0.0))).astype(BF16)

    def to_slots(compact, placed, const_ref, out_ref):
        for h in range(N_HEADS):
            x = compact[:, LANES * (h // 2):LANES * (h // 2 + 1)]
            head = pltpu.roll(x, HEAD_DIM, 1) if h % 2 else x
            slot = slice(LANES * h, LANES * (h + 1))
            extra = placed[:, slot] + const_ref[:, slot]
            out_ref[0, :, slot] = jnp.where(lane < HEAD_DIM, head, extra).astype(BF16)

    to_slots(seg(C_FQ, C_FK) * QK_SCALE, _bdot(pieces, place_ref[:, :SLOT_W]), qc_ref, fq_ref)
    to_slots(seg(C_FK, C_FV), _bdot(pieces, place_ref[:, SLOT_W:]), kc_ref, fk_ref)
    fv_ref[0] = seg(C_FV, C_SB).astype(BF16).T

    sb_ref[0, :, 0:512] = (seg(C_SB, C_SB + 512) * QK_SCALE).astype(BF16)
    sb_ref[0, :, 512:1024] = seg(C_SB + 512, C_SB + 1024).astype(BF16)
    sv_ref[0] = seg(C_SB + 1024, C_GATE).astype(BF16).T
    for c in range(6):
        a = 512 * c
        gate_ref[0, :, a:a + 512] = _sigmoid(seg(C_GATE + a, C_GATE + a + 512)).astype(BF16)

    cos, sina, sinb = cos_ref[0], sina_ref[0], sinb_ref[0]
    cq = (_rms(seg(C_QL, C_KVL)) * gq_ref[...]).astype(BF16)
    ckv = (_rms(seg(C_KVL, C_KR)) * gkv_ref[...]).astype(BF16)
    kr = _rope_slot(seg(C_KR, C_FF), cos, sina, sinb)
    for h in range(N_HEADS):
        a = LANES * h
        q = _rope_slot(_bdot(cq, wuq_ref[:, a:a + LANES]), cos, sina, sinb)
        mq_ref[0, :, a:a + LANES] = (q * MLA_SCALE).astype(BF16)
        mk_ref[0, :, a:a + LANES] = (_bdot(ckv, wkn_ref[:, a:a + LANES]) + kr).astype(BF16)
    mv_ref[0] = _bdot(ckv, wkv_ref[...]).astype(BF16).T


def _fox_placement():
    place = np.zeros((LANES, 2 * SLOT_W), np.float32)
    qc = np.zeros((1, SLOT_W), np.float32)
    kc = np.zeros((1, SLOT_W), np.float32)
    for h in range(N_HEADS):
        for p in range(N_PIECES):
            place[N_HEADS * p + h, LANES * h + HEAD_DIM + N_PIECES + p] = 1.0
            place[N_HEADS * p + h, SLOT_W + LANES * h + HEAD_DIM + p] = 1.0
            qc[0, LANES * h + HEAD_DIM + p] = -1.0
            kc[0, LANES * h + HEAD_DIM + N_PIECES + p] = 1.0
    return jnp.asarray(place, BF16), jnp.asarray(qc), jnp.asarray(kc)


def _inproj(x, mod, g_mix, w_in_r, bff, g_q, g_kv, wuq_r, wkn_r, wkv_r, tables, *, tm):
    b, s, d = x.shape
    cos, sina, sinb = tables
    place, qc, kc = _fox_placement()
    row = lambda w: pl.BlockSpec((1, tm, w), lambda i, j: (i, j, 0))
    col = lambda w: pl.BlockSpec((1, w, tm), lambda i, j: (i, 0, j))
    outs = [(SLOT_W, False),
            (SLOT_W, False),
            (HEAD_W, True),
            (2 * HEAD_W, False),
            (HEAD_W, True),
            (3 * D_MODEL, False),
            (SLOT_W, False),
            (SLOT_W, False),
            (HEAD_W, True)]
    out_specs = [col(w) if tr else row(w) for w, tr in outs]
    out_shape = [jax.ShapeDtypeStruct((b, w, s) if tr else (b, s, w), BF16) for w, tr in outs]
    return pl.pallas_call(
        functools.partial(_inproj_kernel, tm=tm),
        grid=(b, s // tm),
        in_specs=[row(d),
                  pl.BlockSpec((1, 6, d), lambda i, j: (i, 0, 0)),
                  _resident((1, d)), _resident((d, IN_W)), _resident((1, LANES)),
                  _resident((1, MLA_Q_RANK)), _resident((1, MLA_KV_RANK)),
                  _resident((MLA_Q_RANK, SLOT_W)), _resident((MLA_KV_RANK, SLOT_W)),
                  _resident((MLA_KV_RANK, HEAD_W)),
                  row(LANES), row(LANES), row(LANES),
                  _resident((LANES, 2 * SLOT_W)), _resident((1, SLOT_W)), _resident((1, SLOT_W))],
        out_specs=out_specs,
        out_shape=out_shape,
        scratch_shapes=[pltpu.VMEM((1, LANES), F32)],
        compiler_params=_params(2),
        name="inproj",
    )(x, mod, g_mix, w_in_r, bff, g_q, g_kv, wuq_r, wkn_r, wkv_r, cos, sina, sinb, place, qc, kc)


def _head_rows():
    row = lax.broadcasted_iota(jnp.int32, (LANES, 1), 0)
    return [(row >= HEAD_DIM * hh) & (row < HEAD_DIM * (hh + 1)) for hh in range(2)]


def _head_masks():
    lane = lax.broadcasted_iota(jnp.int32, (1, LANES), 1)
    return [(lane >= HEAD_DIM * hh) & (lane < HEAD_DIM * (hh + 1)) for hh in range(2)]


def _col_reduce(x, op, final, chunk=64):
    rows = x.shape[0]
    if rows > chunk and rows % chunk == 0:
        acc = x[:chunk]
        for a in range(chunk, rows, chunk):
            acc = op(acc, x[a:a + chunk])
        x, rows = acc, chunk
    while rows > 8 and rows % 16 == 0:
        rows //= 2
        x = op(x[:rows], x[rows:])
    return final(x, axis=0, keepdims=True)


def _colmax(x):
    return _col_reduce(x, jnp.maximum, jnp.max)


def _colsum(x):
    return _col_reduce(x, jnp.add, jnp.sum)


def _pipeline(n, stages, unroll=PIPELINE_UNROLL):
    depth = len(stages)

    def static_step(t):
        for s, stage in enumerate(stages):
            if 0 <= t - s < n:
                stage(t - s, (t - s) % 2)

    first_full = depth - 1
    n_full = max(n - depth + 1, 0)
    for t in range(min(first_full, n + depth - 1)):
        static_step(t)
    n_loops = n_full // unroll
    if n_loops >= 1:
        def body(it, carry):
            t = first_full + unroll * it
            for u in range(unroll):
                for s, stage in enumerate(stages):
                    stage(t + u - s, (first_full + u - s) % 2)
            return carry

        lax.fori_loop(0, n_loops, body, 0)
    for t in range(first_full + unroll * n_loops, n + depth - 1):
        static_step(t)


def _block_tables(n_tiles, t, descending):
    diag = [(i * t, i * t) for i in range(n_tiles)]
    below = []
    for i in range(1, n_tiles):
        js = range(i - 1, -1, -1) if descending else range(i)
        below += [(i * t, j * t) for j in js]
    return jnp.asarray(np.array(diag + below, np.int32).T), len(diag), len(below)


def _softmax_flat_kernel(tab_ref, q_ref, k_ref, vt_ref, o_ref, acc_ref, m_ref, al_ref, top_ref,
                         s0_ref, s1_ref, p0_ref, p1_ref, *, t, n_diag, n_below, chunk_shift):
    own_rows = _head_rows()
    s_bufs, p_bufs = (s0_ref, s1_ref), (p0_ref, p1_ref)
    kio = lax.broadcasted_iota(jnp.int32, (t, 1), 0)
    qio = lax.broadcasted_iota(jnp.int32, (1, t), 1)
    acc_ref[...] = jnp.zeros_like(acc_ref)
    m_ref[...] = jnp.full_like(m_ref, NEG)

    def stages(base, diag):
        def item(i):
            return (pl.multiple_of(tab_ref[0, base + i], t), pl.multiple_of(tab_ref[1, base + i], t))

        def scores(i, par):
            q0, k0 = item(i)
            for hh in range(2):
                slot = slice(LANES * hh, LANES * (hh + 1))
                s = _dot_nt(k_ref[0, pl.ds(k0, t), slot],
                            q_ref[0, pl.ds(q0, t), slot])
                if diag:
                    s = jnp.where((kio >> chunk_shift) <= (qio >> chunk_shift), s, NEG)
                s_bufs[par][hh] = s
                top_ref[par, hh] = _colmax(s)

        def numerators(i, par):
            q0, _ = item(i)
            for hh in range(2):
                m_old = m_ref[hh, :, pl.ds(q0, t)]
                m_new = jnp.maximum(m_old, top_ref[par, hh])
                for a in range(0, t, ROW_CHUNK):
                    rows = slice(a, a + ROW_CHUNK)
                    p_bufs[par][hh, rows, :] = jnp.exp(s_bufs[par][hh, rows, :] - m_new).astype(BF16)
                al_ref[par, hh] = jnp.exp(m_old - m_new)
                m_ref[hh, :, pl.ds(q0, t)] = m_new

        def values(i, par):
            q0, k0 = item(i)
            vt = vt_ref[0, :, pl.ds(k0, t)]
            for hh in range(2):
                v1 = jnp.where(own_rows[hh], vt, jnp.ones_like(vt))
                acc_ref[hh, :, pl.ds(q0, t)] = (acc_ref[hh, :, pl.ds(q0, t)] * al_ref[par, hh]
                                                + _bdot(v1, p_bufs[par][hh]))

        return [scores, numerators, values]

    _pipeline(n_diag, stages(0, True))
    _pipeline(n_below, stages(n_diag, False))

    sub = lax.broadcasted_iota(jnp.int32, (LANES, 1), 0)

    def finish(tile, carry):
        q0 = pl.multiple_of(tile * t, t)
        outs = []
        for hh in range(2):
            acc = acc_ref[hh, :, pl.ds(q0, t)]
            denom = acc[HEAD_DIM * (1 - hh):HEAD_DIM * (1 - hh) + 1, :]
            outs.append(acc / denom)
        o_ref[0, pl.ds(q0, t), :] = jnp.where(sub < HEAD_DIM, outs[0], outs[1]).T.astype(o_ref.dtype)
        return carry

    lax.fori_loop(0, n_diag, finish, 0)


def _softmax_attention_flat(q, k, vt, *, t, chunk_shift, name):
    b, s, _ = k.shape
    table, n_diag, n_below = _block_tables(s // t, t, descending=False)
    grid_spec = pltpu.PrefetchScalarGridSpec(
        num_scalar_prefetch=1,
        grid=(b, N_HEADS // 2),
        in_specs=[pl.BlockSpec((1, s, 2 * LANES), lambda i, h, tab: (i, 0, h)),
                  pl.BlockSpec((1, s, 2 * LANES), lambda i, h, tab: (i, 0, h)),
                  pl.BlockSpec((1, LANES, s), lambda i, h, tab: (i, h, 0))],
        out_specs=pl.BlockSpec((1, s, LANES), lambda i, h, tab: (i, 0, h)),
        scratch_shapes=[pltpu.VMEM((2, LANES, s), F32),
                        pltpu.VMEM((2, 1, s), F32),
                        pltpu.VMEM((2, 2, 1, t), F32),
                        pltpu.VMEM((2, 2, 1, t), F32),
                        pltpu.VMEM((2, t, t), F32), pltpu.VMEM((2, t, t), F32),
                        pltpu.VMEM((2, t, t), BF16), pltpu.VMEM((2, t, t), BF16)])
    return pl.pallas_call(
        functools.partial(_softmax_flat_kernel, t=t, n_diag=n_diag, n_below=n_below,
                          chunk_shift=chunk_shift),
        grid_spec=grid_spec,
        out_shape=jax.ShapeDtypeStruct((b, s, HEAD_W), BF16),
        compiler_params=_params(2),
        name=name,
    )(table, q, k, vt)


def _sb_block_table(n_tiles, t):
    diag = [(i * t, i * t, i * t, i) for i in range(n_tiles)]
    below = [(i * t, (i - d) * t, i * t, i) for d in range(1, n_tiles) for i in range(d, n_tiles)]
    idle = [(0, 0, n_tiles * t, n_tiles)]
    return jnp.asarray(np.array(diag + below + idle, np.int32).T), len(diag), len(below)


def _sb_flat_kernel(tab_ref, q_ref, k_ref, vt_ref, o_ref, acc_ref, r_ref, cut_ref, tot_ref, d_ref,
                    done_ref, z0_ref, z1_ref, e0_ref, e1_ref, w0_ref, w1_ref, *, t, n_diag,
                    n_below):
    in_head = _head_masks()
    own_rows = _head_rows()
    z_bufs, e_bufs, w_bufs = (z0_ref, z1_ref), (e0_ref, e1_ref), (w0_ref, w1_ref)
    kio = lax.broadcasted_iota(jnp.int32, (t, 1), 0)
    qio = lax.broadcasted_iota(jnp.int32, (1, t), 1)
    ss = lax.broadcasted_iota(jnp.int32, (t, t), 0)
    jj = lax.broadcasted_iota(jnp.int32, (t, t), 1)
    from_here = jnp.where(jj >= ss, 1.0, 0.0).astype(BF16)
    acc_ref[...] = jnp.zeros_like(acc_ref)
    r_ref[...] = jnp.zeros_like(r_ref)
    cut_ref[...] = jnp.zeros_like(cut_ref)
    for tile in range(n_diag + 1):
        done_ref[tile] = 0

    head_ones = [jnp.broadcast_to(jnp.where(in_head[hh], 1.0, 0.0), (8, LANES)).astype(BF16)
                 for hh in range(2)]

    def sq_norms(ref, tile):
        x = ref[0, pl.ds(pl.multiple_of(tile * t, t), t), :].astype(F32)
        sq = (x * x).astype(BF16)
        return [_dot_nt(head_ones[hh], sq)[0:1, :] for hh in range(2)]

    def key_max(tile, carry):
        return tuple(jnp.maximum(c, jnp.max(n, axis=1, keepdims=True))
                     for c, n in zip(carry, sq_norms(k_ref, tile)))

    k_max = lax.fori_loop(0, n_diag, key_max, (jnp.zeros((1, 1), F32),) * 2)

    def set_cut(tile, carry):
        for hh, n in enumerate(sq_norms(q_ref, tile)):
            cut_ref[hh, :, pl.ds(pl.multiple_of(tile * t, t), t)] = (
                NORM_MARGIN * jnp.sqrt(n * k_max[hh]) + UNDERFLOW)
        return carry

    lax.fori_loop(0, n_diag, set_cut, 0)

    def stages(base, diag):
        def item(i):
            return (pl.multiple_of(tab_ref[0, base + i], t), pl.multiple_of(tab_ref[1, base + i], t),
                    pl.multiple_of(tab_ref[2, base + i], t), tab_ref[3, base + i])

        def scores(i, par):
            q0, k0, _, _ = item(i)
            k = k_ref[0, pl.ds(k0, t), :]
            q = q_ref[0, pl.ds(q0, t), :]
            for hh in range(2):
                qh = jnp.where(in_head[hh], q, jnp.zeros_like(q))
                z_bufs[par][hh] = _dot_nt(k, qh)

        def drops(i, par):
            for hh in range(2):
                for a in range(0, t, ROW_CHUNK):
                    rows = slice(a, a + ROW_CHUNK)
                    z = z_bufs[par][hh, rows, :]
                    neg_abs = pltpu.bitcast(pltpu.bitcast(z, jnp.uint32) | SIGN_BIT, F32)
                    drop = jnp.maximum(z, 0.0) + jnp.log(1.0 + jnp.exp(neg_abs))
                    if diag:
                        drop = jnp.where(kio[rows] < qio, drop, 0.0)
                    d_ref[hh, rows, :] = drop.astype(BF16)
                incl = _bdot(from_here, d_ref[hh])
                for a in range(0, t, ROW_CHUNK):
                    rows = slice(a, a + ROW_CHUNK)
                    e = z_bufs[par][hh, rows, :] - incl[rows]
                    if diag:
                        e = jnp.where(kio[rows] < qio, e, NEG)
                    e_bufs[par][hh, rows, :] = e
                tot_ref[par, hh] = incl[0:1, :]

        def weights(i, par):
            _, _, q0, tile = item(i)
            slack = None
            for hh in range(2):
                r_old = r_ref[hh, :, pl.ds(q0, t)]
                for a in range(0, t, ROW_CHUNK):
                    rows = slice(a, a + ROW_CHUNK)
                    w_bufs[par][hh, rows, :] = jnp.exp(e_bufs[par][hh, rows, :] - r_old).astype(BF16)
                r_new = r_old + tot_ref[par, hh]
                r_ref[hh, :, pl.ds(q0, t)] = r_new
                over = r_new - cut_ref[hh, :, pl.ds(q0, t)]
                slack = over if slack is None else jnp.minimum(slack, over)
            if not diag:
                done_ref[tile] = (jnp.min(slack) > 0.0).astype(jnp.int32)

        def values(i, par):
            _, k0, q0, _ = item(i)
            vt = vt_ref[0, :, pl.ds(k0, t)]
            zero = jnp.zeros_like(vt)
            lhs = jnp.concatenate([jnp.where(own_rows[hh], vt, zero) for hh in range(2)], axis=1)
            acc_ref[:, pl.ds(q0, t)] = (acc_ref[:, pl.ds(q0, t)]
                                        + _bdot(lhs, w_bufs[par][...].reshape(2 * t, t)))

        return [scores, drops, weights, values]

    _pipeline(n_diag, stages(0, True))

    scores, drops, weights, values = stages(n_diag, False)

    def next_live(c):
        return lax.while_loop(
            lambda c: (c < n_below) & (done_ref[tab_ref[3, n_diag + c]] != 0), lambda c: c + 1, c)

    def two_steps(state):
        cur, i1, i2, i3 = state
        for par in range(2):
            issue = next_live(cur)
            scores(issue, par)
            drops(i1, 1 - par)
            weights(i2, par)
            values(i3, 1 - par)
            cur, i1, i2, i3 = jnp.minimum(issue + 1, n_below), issue, i1, i2
        return cur, i1, i2, i3

    idle = jnp.int32(n_below)
    lax.while_loop(lambda st: (st[0] < n_below) | (st[1] < n_below) | (st[2] < n_below)
                   | (st[3] < n_below), two_steps, (jnp.int32(0), idle, idle, idle))

    def finish(tile, carry):
        q0 = pl.multiple_of(tile * t, t)
        o_ref[0, pl.ds(q0, t), :] = acc_ref[:, pl.ds(q0, t)].T.astype(o_ref.dtype)
        return carry

    lax.fori_loop(0, n_diag, finish, 0)


def _sb_attention_flat(qk, vt, *, t):
    b, s, _ = qk.shape
    table, n_diag, n_below = _sb_block_table(s // t, t)
    grid_spec = pltpu.PrefetchScalarGridSpec(
        num_scalar_prefetch=1,
        grid=(b, N_HEADS // 2),
        in_specs=[pl.BlockSpec((1, s, LANES), lambda i, h, tab: (i, 0, h)),
                  pl.BlockSpec((1, s, LANES), lambda i, h, tab: (i, 0, 4 + h)),
                  pl.BlockSpec((1, LANES, s), lambda i, h, tab: (i, h, 0))],
        out_specs=pl.BlockSpec((1, s, LANES), lambda i, h, tab: (i, 0, h)),
        scratch_shapes=[pltpu.VMEM((LANES, s + t), F32),
                        pltpu.VMEM((2, 1, s + t), F32),
                        pltpu.VMEM((2, 1, s + t), F32),
                        pltpu.VMEM((2, 2, 1, t), F32),
                        pltpu.VMEM((2, t, t), BF16),
                        pltpu.SMEM((s // t + 1,), jnp.int32),
                        pltpu.VMEM((2, t, t), F32), pltpu.VMEM((2, t, t), F32),
                        pltpu.VMEM((2, t, t), F32), pltpu.VMEM((2, t, t), F32),
                        pltpu.VMEM((2, t, t), BF16), pltpu.VMEM((2, t, t), BF16)])
    return pl.pallas_call(
        functools.partial(_sb_flat_kernel, t=t, n_diag=n_diag, n_below=n_below),
        grid_spec=grid_spec,
        out_shape=jax.ShapeDtypeStruct((b, s, HEAD_W), BF16),
        compiler_params=_params(2),
        name="sb_attention",
    )(table, qk, qk, vt)


def _softmax_attn_kernel(q_ref, k_ref, v_ref, o_ref, acc_ref, s0_ref, s1_ref, *, tq, tk,
                         chunk_shift):
    qi = pl.program_id(2)
    in_head = _head_masks()
    qs = [q_ref[0, :, LANES * hh:LANES * (hh + 1)] for hh in range(2)]
    qpos = qi * tq + lax.broadcasted_iota(jnp.int32, (1, tq), 1)
    acc_ref[...] = jnp.zeros_like(acc_ref)

    def produce(j, buf):
        start = pl.multiple_of(j * tk, tk)
        for hh in range(2):
            k = k_ref[0, pl.ds(start, tk), LANES * hh:LANES * (hh + 1)]
            buf[hh] = _dot_nt(k, qs[hh])

    def consume(j, buf, m, masked):
        start = pl.multiple_of(j * tk, tk)
        v = v_ref[0, pl.ds(start, tk), :]
        m_out = []
        for hh in range(2):
            s = buf[hh]
            if masked:
                kpos = start + lax.broadcasted_iota(jnp.int32, (tk, 1), 0)
                s = jnp.where((kpos >> chunk_shift) <= (qpos >> chunk_shift), s, NEG)
            m_new = jnp.maximum(m[hh], _colmax(s))
            p = jnp.exp(s - m_new)
            alpha = jnp.exp(m[hh] - m_new)
            v1 = jnp.where(in_head[hh], v, jnp.ones_like(v))
            acc_ref[hh] = acc_ref[hh] * alpha + _dot_tn(v1, p.astype(BF16))
            m_out.append(m_new)
        return tuple(m_out)

    n_last = (qi * tq + tq - 1) // tk
    n_pairs = n_last // 2

    def body(i, m):
        j = 2 * i
        produce(j + 1, s1_ref)
        m = consume(j, s0_ref, m, False)
        produce(j + 2, s0_ref)
        return consume(j + 1, s1_ref, m, False)

    produce(0, s0_ref)
    m = (jnp.full((1, tq), NEG, F32),) * 2
    m = lax.fori_loop(0, n_pairs, body, m)
    j = 2 * n_pairs

    @pl.when(n_last % 2 == 1)
    def _():
        produce(j + 1, s1_ref)
        consume(j + 1, s1_ref, consume(j, s0_ref, m, False), True)

    @pl.when(n_last % 2 == 0)
    def _():
        consume(j, s0_ref, m, True)

    sub = lax.broadcasted_iota(jnp.int32, (LANES, 1), 0)
    outs = []
    for hh in range(2):
        acc = acc_ref[hh]
        denom = acc[HEAD_DIM * (1 - hh):HEAD_DIM * (1 - hh) + 1, :]
        outs.append(acc / denom)
    o_ref[0] = jnp.where(sub < HEAD_DIM, outs[0], outs[1]).T.astype(o_ref.dtype)


def _softmax_attention(q, k, v, *, tq, tk, chunk_shift, name):
    b, s, _ = q.shape
    return pl.pallas_call(
        functools.partial(_softmax_attn_kernel, tq=tq, tk=tk, chunk_shift=chunk_shift),
        grid=(b, N_HEADS // 2, s // tq),
        in_specs=[pl.BlockSpec((1, tq, 2 * LANES), lambda i, h, j: (i, j, h)),
                  pl.BlockSpec((1, s, 2 * LANES), lambda i, h, j: (i, 0, h)),
                  pl.BlockSpec((1, s, LANES), lambda i, h, j: (i, 0, h))],
        out_specs=pl.BlockSpec((1, tq, LANES), lambda i, h, j: (i, j, h)),
        out_shape=jax.ShapeDtypeStruct((b, s, HEAD_W), BF16),
        scratch_shapes=[pltpu.VMEM((2, LANES, tq), F32), pltpu.VMEM((2, tk, tq), F32),
                        pltpu.VMEM((2, tk, tq), F32)],
        compiler_params=_params(3),
        name=name,
    )(q, k, v)


def _sb_kernel(q_ref, k_ref, v_ref, o_ref, acc_ref, z_ref, r_ref, *, t):
    qi = pl.program_id(2)
    in_head = _head_masks()
    q = q_ref[0]
    qs = [jnp.where(in_head[hh], q, jnp.zeros_like(q)) for hh in range(2)]
    qpos = lax.broadcasted_iota(jnp.int32, (1, t), 1)
    kpos = lax.broadcasted_iota(jnp.int32, (t, 1), 0)
    ss = lax.broadcasted_iota(jnp.int32, (t, 2 * t), 0)
    jj = lax.broadcasted_iota(jnp.int32, (t, 2 * t), 1) & (t - 1)
    from_here = jnp.where(jj >= ss, 1.0, 0.0).astype(BF16)
    acc_ref[...] = jnp.zeros_like(acc_ref)
    r_ref[...] = jnp.zeros_like(r_ref)

    def key_start(blk):
        return pl.multiple_of(jnp.maximum(blk, 0) * t, t)

    def scores(blk_hi, nblk):
        for u in range(nblk):
            k = k_ref[0, pl.ds(key_start(blk_hi - u), t), :]
            for hh in range(2):
                z_ref[u, hh] = _dot_nt(k, qs[hh])

    def group(blk_hi, nblk, diag_first, next_hi):
        chains = []
        for u in range(nblk):
            for hh in range(2):
                z = z_ref[u, hh]
                drop = jnp.maximum(z, 0.0) + jnp.log(1.0 + jnp.exp(-jnp.abs(z)))
                if diag_first and u == 0:
                    ok = kpos < qpos
                    drop = jnp.where(ok, drop, 0.0)
                hi = drop.astype(BF16)
                lo = (drop - hi.astype(F32)).astype(BF16)
                incl = _bdot(from_here, jnp.concatenate([hi, lo], axis=0))
                e = z - incl
                if diag_first and u == 0:
                    e = jnp.where(ok, e, NEG)
                chains.append((u, hh, e, incl[0:1, :]))
        scores(next_hi, 2)
        r = [r_ref[hh] for hh in range(2)]
        upd = None
        for u, hh, e, total in chains:
            w = jnp.exp(e - r[hh])
            v = v_ref[0, pl.ds(key_start(blk_hi - u), t), :]
            vh = jnp.where(in_head[hh], v, jnp.zeros_like(v))
            pv = _dot_tn(vh, w.astype(BF16))
            upd = pv if upd is None else upd + pv
            r[hh] = r[hh] + total
        acc_ref[...] = acc_ref[...] + upd
        for hh in range(2):
            r_ref[hh] = r[hh]

    @pl.when(qi % 2 == 0)
    def _():
        scores(qi, 1)
        group(qi, 1, True, qi - 1)

    @pl.when(qi % 2 == 1)
    def _():
        scores(qi, 2)
        group(qi, 2, True, qi - 2)

    n_rest = qi // 2
    top = 2 * n_rest - 1

    def body(i, carry):
        group(top - 2 * i, 2, False, top - 2 * i - 2)
        return carry

    lax.fori_loop(0, n_rest, body, 0)
    o_ref[0] = acc_ref[...].T.astype(o_ref.dtype)


def _sb_attention(qkv, *, t):
    b, s, _ = qkv.shape
    return pl.pallas_call(
        functools.partial(_sb_kernel, t=t),
        grid=(b, N_HEADS // 2, s // t),
        in_specs=[pl.BlockSpec((1, t, LANES), lambda i, h, j: (i, j, h)),
                  pl.BlockSpec((1, s, LANES), lambda i, h, j: (i, 0, 4 + h)),
                  pl.BlockSpec((1, s, LANES), lambda i, h, j: (i, 0, 8 + h))],
        out_specs=pl.BlockSpec((1, t, LANES), lambda i, h, j: (i, j, h)),
        out_shape=jax.ShapeDtypeStruct((b, s, HEAD_W), BF16),
        scratch_shapes=[pltpu.VMEM((LANES, t), F32), pltpu.VMEM((2, 2, t, t), F32),
                        pltpu.VMEM((2, 1, t), F32)],
        compiler_params=_params(3),
        name="sb_attention",
    )(qkv, qkv, qkv)


def _merge_kernel(x_ref, yf_ref, ym_ref, ys_ref, g_ref, mod_ref, wf_ref, wm_ref, ws_ref,
                  wo_ref, o_ref):
    d = D_MODEL
    merged = (g_ref[0, :, 0:d].astype(F32) * _bdot(yf_ref[0], wf_ref[...])
              + g_ref[0, :, d:2 * d].astype(F32) * _bdot(ym_ref[0], wm_ref[...])
              + g_ref[0, :, 2 * d:3 * d].astype(F32) * _bdot(ys_ref[0], ws_ref[...]))
    o_ref[0] = x_ref[0] + mod_ref[0, 2:3, :] * _bdot(merged.astype(BF16), wo_ref[...])


def _merge(x, y_fox, y_mla, y_sb, gates, mod, wf, wm, ws, wo, *, tm):
    b, s, d = x.shape
    row = lambda w: pl.BlockSpec((1, tm, w), lambda i, j: (i, j, 0))
    return pl.pallas_call(
        _merge_kernel,
        grid=(b, s // tm),
        in_specs=[row(d), row(HEAD_W), row(HEAD_W), row(HEAD_W), row(3 * d),
                  pl.BlockSpec((1, 6, d), lambda i, j: (i, 0, 0)),
                  _resident((HEAD_W, d)), _resident((HEAD_W, d)), _resident((HEAD_W, d)),
                  _resident((d, d))],
        out_specs=row(d),
        out_shape=jax.ShapeDtypeStruct((b, s, d), F32),
        compiler_params=_params(2),
        name="merge_out",
    )(x, y_fox, y_mla, y_sb, gates, mod, wf, wm, ws, wo)


def _ffn_kernel(x_ref, mod_ref, g_ref, wg_ref, wu_ref, wd_ref, gfin_ref, o_ref, *, final, fc):
    x = x_ref[0]
    sh, sc, gt = mod_ref[0, 3:4, :], mod_ref[0, 4:5, :], mod_ref[0, 5:6, :]
    u = ((_rms(x) * g_ref[...]) * (1.0 + sc) + sh).astype(BF16)
    acc = None
    for a in range(0, D_FF, fc):
        hg = _bdot(u, wg_ref[:, a:a + fc])
        hu = _bdot(u, wu_ref[:, a:a + fc])
        h = ((hg * _sigmoid(hg)) * hu).astype(BF16)
        part = _bdot(h, wd_ref[a:a + fc, :])
        acc = part if acc is None else acc + part
    out = x + gt * acc
    if final:
        out = _rms(out) * gfin_ref[...]
    o_ref[0] = out


def _ffn(x, mod, g_ffn, wg, wu, wd, g_final, *, tm, final):
    b, s, d = x.shape
    row = lambda w: pl.BlockSpec((1, tm, w), lambda i, j: (i, j, 0))
    return pl.pallas_call(
        functools.partial(_ffn_kernel, final=final, fc=D_FF // 2),
        grid=(b, s // tm),
        in_specs=[row(d), pl.BlockSpec((1, 6, d), lambda i, j: (i, 0, 0)), _resident((1, d)),
                  _resident((d, D_FF)), _resident((d, D_FF)), _resident((D_FF, d)),
                  _resident((1, d))],
        out_specs=row(d),
        out_shape=jax.ShapeDtypeStruct((b, s, d), F32),
        compiler_params=_params(2),
        name="ffn",
    )(x, mod, g_ffn, wg, wu, wd, g_final)


def _slot_cols(w, width_per_head, used):
    k = w.shape[0]
    w = w.reshape(k, N_HEADS, width_per_head)[:, :, :used]
    return jnp.pad(w, ((0, 0), (0, 0), (0, LANES - used))).reshape(k, SLOT_W)


def _layout_w_in(w):
    d = w.shape[0]
    fox, ff, ql, kvl, kr, sb, gate = jnp.split(
        w, [1536, 1544, 1928, 2184, 2216, 3752], axis=-1)
    z = lambda n: jnp.zeros((d, n), w.dtype)
    kr_slot = jnp.concatenate([z(MLA_NOPE), kr, z(LANES - MLA_NOPE - MLA_ROPE)], axis=-1)
    ff_slot = jnp.concatenate([ff, z(LANES - N_HEADS)], axis=-1)
    return jnp.concatenate([fox, sb, gate, ql, kvl, kr_slot, ff_slot], axis=-1).astype(BF16)


def kernel(x, c, positions, g_mix, w_ada, b_ada, w_in, b_fox_f, g_mla_q, w_mla_uq, g_mla_kv,
           w_mla_ukv, w_o_fox, w_o_mla, w_o_sb, w_out, g_ffn, w_ffn_gate, w_ffn_up, w_ffn_down,
           g_final):
    b, s, d = x.shape
    depth = w_in.shape[0]
    tm = min(s, 512)
    ta = min(s, 512)
    tsb = min(s, 256)

    mod = _ada(c, w_ada, b_ada).reshape(depth, b, 6, d)
    tables = _rope_tables(positions)
    for l in range(depth):
        w_in_r = _layout_w_in(w_in[l])
        wuq_r = _slot_cols(w_mla_uq[l], MLA_NOPE + MLA_ROPE, MLA_NOPE + MLA_ROPE).astype(BF16)
        wkn_r = _slot_cols(w_mla_ukv[l], 2 * HEAD_DIM, MLA_NOPE).astype(BF16)
        wkv_r = w_mla_ukv[l].reshape(MLA_KV_RANK, N_HEADS, 2 * HEAD_DIM)[:, :, MLA_NOPE:]
        wkv_r = wkv_r.reshape(MLA_KV_RANK, HEAD_W).astype(BF16)
        bff = jnp.pad(b_fox_f[l], (0, LANES - N_HEADS)).reshape(1, LANES)

        fox_q, fox_k, fox_vt, sb_qk, sb_vt, gates, mq, mk, mvt = _inproj(
            x, mod[l], g_mix[l].reshape(1, d), w_in_r, bff, g_mla_q[l].reshape(1, -1),
            g_mla_kv[l].reshape(1, -1), wuq_r, wkn_r, wkv_r, tables, tm=tm)
        y_fox = _softmax_attention_flat(fox_q, fox_k, fox_vt, t=ta, chunk_shift=0,
                                        name="fox_attention")
        y_mla = _softmax_attention_flat(mq, mk, mvt, t=ta, chunk_shift=CHUNK_SHIFT,
                                        name="mla_attention")
        y_sb = _sb_attention_flat(sb_qk, sb_vt, t=tsb)
        x = _merge(x, y_fox, y_mla, y_sb, gates, mod[l], w_o_fox[l].astype(BF16),
                   w_o_mla[l].astype(BF16), w_o_sb[l].astype(BF16), w_out[l].astype(BF16), tm=tm)
        x = _ffn(x, mod[l], g_ffn[l].reshape(1, d), w_ffn_gate[l].astype(BF16),
                 w_ffn_up[l].astype(BF16), w_ffn_down[l].astype(BF16), g_final.reshape(1, d),
                 tm=tm, final=(l == depth - 1))
    return x
```

```python
import functools

import numpy as np
import jax
import jax.numpy as jnp
from jax import lax
from jax.experimental import pallas as pl
from jax.experimental.pallas import tpu as pltpu

F32 = jnp.float32
BF16 = jnp.bfloat16

D_MODEL = 1024
EPS = 1e-6
HEAD_DIM = 64
N_HEADS = 8
HEAD_W = N_HEADS * HEAD_DIM
CHUNK_SHIFT = 6
MLA_Q_RANK = 384
MLA_KV_RANK = 256
MLA_NOPE = 64
MLA_ROPE = 32
ROPE_HALF = MLA_ROPE // 2
ROPE_BASE = 10000.0
MLA_SCALE = (MLA_NOPE + MLA_ROPE) ** -0.5
QK_SCALE = HEAD_DIM ** -0.5
D_FF = 2816
LANES = 128
SLOT_W = N_HEADS * LANES
N_PIECES = 3

C_FQ = 0
C_FK = 512
C_FV = 1024
C_SB = 1536
C_GATE = 3072
C_QL = 6144
C_KVL = 6528
C_KR = 6784
C_FF = 6912
IN_W = 7040

ROW_CHUNK = 64
PIPELINE_UNROLL = 2
SIGN_BIT = np.uint32(0x80000000)
UNDERFLOW = 105.0
NORM_MARGIN = 1.02
NEG = -1e30
VMEM_LIMIT = 56 * 1024 * 1024


def _params(n_grid):
    return pltpu.CompilerParams(dimension_semantics=("arbitrary",) * n_grid,
                                vmem_limit_bytes=VMEM_LIMIT)


def _resident(shape):
    nd = len(shape)
    return pl.BlockSpec(shape, lambda *_: (0,) * nd, pipeline_mode=pl.Buffered(1))


def _bdot(a, b):
    return jnp.dot(a, b, preferred_element_type=F32)


def _dot_nt(a, b):
    return lax.dot_general(a, b, (((1,), (1,)), ((), ())), preferred_element_type=F32)


def _dot_tn(a, b):
    return lax.dot_general(a, b, (((0,), (0,)), ((), ())), preferred_element_type=F32)


def _split3(v):
    hi = v.astype(BF16)
    r = v - hi.astype(F32)
    mid = r.astype(BF16)
    lo = (r - mid.astype(F32)).astype(BF16)
    return hi, mid, lo


def _log_sigmoid(z):
    return jnp.minimum(z, 0.0) - jnp.log1p(jnp.exp(-jnp.abs(z)))


def _sigmoid(z):
    return 1.0 / (1.0 + jnp.exp(-z))


def _rms(x):
    return x * lax.rsqrt(jnp.mean(x * x, axis=-1, keepdims=True) + EPS)


def _ada_kernel(c_ref, w_ref, b_ref, o_ref):
    c = c_ref[...]
    cond = c * _sigmoid(c)
    chi, cmid, clo = _split3(cond)
    whi, wmid, wlo = _split3(w_ref[0])
    acc = _bdot(chi, whi) + (_bdot(chi, wmid) + _bdot(cmid, whi))
    acc = acc + (_bdot(chi, wlo) + _bdot(clo, whi) + _bdot(cmid, wmid))
    o_ref[0] = acc + b_ref[0]


def _ada(c, w_ada, b_ada):
    depth, d, n = w_ada.shape
    b = c.shape[0]
    tn = 1024
    return pl.pallas_call(
        _ada_kernel,
        grid=(depth, n // tn),
        in_specs=[pl.BlockSpec((b, d), lambda l, j: (0, 0)),
                  pl.BlockSpec((1, d, tn), lambda l, j: (l, 0, j)),
                  pl.BlockSpec((1, 1, tn), lambda l, j: (l, 0, j))],
        out_specs=pl.BlockSpec((1, b, tn), lambda l, j: (l, 0, j)),
        out_shape=jax.ShapeDtypeStruct((depth, b, n), F32),
        compiler_params=_params(2),
        name="adaln",
    )(c, w_ada, b_ada.reshape(depth, 1, n))


def _rope_kernel(pos_ref, inv_ref, cos_ref, sina_ref, sinb_ref):
    ang = pos_ref[0].astype(F32) * inv_ref[...]
    lane = lax.broadcasted_iota(jnp.int32, (1, LANES), 1)
    sin = jnp.sin(ang)
    cos_ref[0] = jnp.cos(ang)
    lo, mid, hi = MLA_NOPE, MLA_NOPE + ROPE_HALF, MLA_NOPE + MLA_ROPE
    sina_ref[0] = jnp.where((lane >= lo) & (lane < mid), -sin, 0.0)
    sinb_ref[0] = jnp.where((lane >= mid) & (lane < hi), sin, 0.0)


def _rope_tables(positions):
    b, s = positions.shape
    ts = min(s, 512)
    inv = ROPE_BASE ** (-jnp.arange(ROPE_HALF, dtype=F32) / ROPE_HALF)
    inv_slot = jnp.concatenate([jnp.zeros((MLA_NOPE,), F32), inv, inv,
                                jnp.zeros((LANES - MLA_NOPE - MLA_ROPE,), F32)]).reshape(1, LANES)
    spec = pl.BlockSpec((1, ts, LANES), lambda i, j: (i, j, 0))
    return pl.pallas_call(
        _rope_kernel,
        grid=(b, s // ts),
        in_specs=[pl.BlockSpec((1, ts, 1), lambda i, j: (i, j, 0)),
                  pl.BlockSpec((1, LANES), lambda i, j: (0, 0))],
        out_specs=[spec, spec, spec],
        out_shape=[jax.ShapeDtypeStruct((b, s, LANES), F32)] * 3,
        compiler_params=_params(2),
        name="rope_tables",
    )(positions.reshape(b, s, 1), inv_slot)


def _rope_slot(x, cos, sina, sinb):
    return (x * cos + pltpu.roll(x, LANES - ROPE_HALF, 1) * sina
            + pltpu.roll(x, ROPE_HALF, 1) * sinb)


def _inproj_kernel(x_ref, mod_ref, g_ref, w_ref, bff_ref, gq_ref, gkv_ref, wuq_ref, wkn_ref,
                   wkv_ref, cos_ref, sina_ref, sinb_ref, place_ref, qc_ref, kc_ref,
                   fq_ref, fk_ref, fv_ref, sb_ref, sv_ref, gate_ref, mq_ref, mk_ref, mv_ref,
                   carry_ref, *, tm):
    x = x_ref[0]
    sh, sc = mod_ref[0, 0:1, :], mod_ref[0, 1:2, :]
    u = ((_rms(x) * g_ref[...]) * (1.0 + sc) + sh).astype(BF16)

    def seg(a, b):
        return _bdot(u, w_ref[:, a:b])

    @pl.when(pl.program_id(1) == 0)
    def _():
        carry_ref[...] = jnp.zeros_like(carry_ref)

    logf = _log_sigmoid(seg(C_FF, IN_W) + bff_ref[...])
    r = lax.broadcasted_iota(jnp.int32, (tm, tm), 0)
    c = lax.broadcasted_iota(jnp.int32, (tm, tm), 1)
    tri = jnp.where(r >= c, 1.0, 0.0).astype(BF16)
    hi, mid, lo = _split3(logf)
    cum = (_bdot(tri, hi) + _bdot(tri, mid) + _bdot(tri, lo)) + carry_ref[...]
    carry_ref[...] = cum[tm - 1:tm, :]
    p0, p1, p2 = (p.astype(F32) for p in _split3(cum))
    lane = lax.broadcasted_iota(jnp.int32, (1, LANES), 1)
    pieces = jnp.where(lane < N_HEADS, p0,
                       jnp.where(lane < 2 * N_HEADS, pltpu.roll(p1, N_HEADS, 1),
                                 jnp.where(lane < 3 * N_HEADS, pltpu.roll(p2, 2 * N_HEADS, 1),
                                           0.0))).astype(BF16)

    def to_slots(compact, placed, const_ref, out_ref):
        for h in range(N_HEADS):
            x = compact[:, LANES * (h // 2):LANES * (h // 2 + 1)]
            head = pltpu.roll(x, HEAD_DIM, 1) if h % 2 else x
            slot = slice(LANES * h, LANES * (h + 1))
            extra = placed[:, slot] + const_ref[:, slot]
            out_ref[0, :, slot] = jnp.where(lane < HEAD_DIM, head, extra).astype(BF16)

    to_slots(seg(C_FQ, C_FK) * QK_SCALE, _bdot(pieces, place_ref[:, :SLOT_W]), qc_ref, fq_ref)
    to_slots(seg(C_FK, C_FV), _bdot(pieces, place_ref[:, SLOT_W:]), kc_ref, fk_ref)
    fv_ref[0] = seg(C_FV, C_SB).astype(BF16).T

    sb_ref[0, :, 0:512] = (seg(C_SB, C_SB + 512) * QK_SCALE).astype(BF16)
    sb_ref[0, :, 512:1024] = seg(C_SB + 512, C_SB + 1024).astype(BF16)
    sv_ref[0] = seg(C_SB + 1024, C_GATE).astype(BF16).T
    for c in range(6):
        a = 512 * c
        gate_ref[0, :, a:a + 512] = _sigmoid(seg(C_GATE + a, C_GATE + a + 512)).astype(BF16)

    cos, sina, sinb = cos_ref[0], sina_ref[0], sinb_ref[0]
    cq = (_rms(seg(C_QL, C_KVL)) * gq_ref[...]).astype(BF16)
    ckv = (_rms(seg(C_KVL, C_KR)) * gkv_ref[...]).astype(BF16)
    kr = _rope_slot(seg(C_KR, C_FF), cos, sina, sinb)
    for h in range(N_HEADS):
        a = LANES * h
        q = _rope_slot(_bdot(cq, wuq_ref[:, a:a + LANES]), cos, sina, sinb)
        mq_ref[0, :, a:a + LANES] = (q * MLA_SCALE).astype(BF16)
        mk_ref[0, :, a:a + LANES] = (_bdot(ckv, wkn_ref[:, a:a + LANES]) + kr).astype(BF16)
    mv_ref[0] = _bdot(ckv, wkv_ref[...]).astype(BF16).T


def _fox_placement():
    place = np.zeros((LANES, 2 * SLOT_W), np.float32)
    qc = np.zeros((1, SLOT_W), np.float32)
    kc = np.zeros((1, SLOT_W), np.float32)
    for h in range(N_HEADS):
        for p in range(N_PIECES):
            place[N_HEADS * p + h, LANES * h + HEAD_DIM + N_PIECES + p] = 1.0
            place[N_HEADS * p + h, SLOT_W + LANES * h + HEAD_DIM + p] = 1.0
            qc[0, LANES * h + HEAD_DIM + p] = -1.0
            kc[0, LANES * h + HEAD_DIM + N_PIECES + p] = 1.0
    return jnp.asarray(place, BF16), jnp.asarray(qc), jnp.asarray(kc)


def _inproj(x, mod, g_mix, w_in_r, bff, g_q, g_kv, wuq_r, wkn_r, wkv_r, tables, *, tm):
    b, s, d = x.shape
    cos, sina, sinb = tables
    place, qc, kc = _fox_placement()
    row = lambda w: pl.BlockSpec((1, tm, w), lambda i, j: (i, j, 0))
    col = lambda w: pl.BlockSpec((1, w, tm), lambda i, j: (i, 0, j))
    outs = [(SLOT_W, False),
            (SLOT_W, False),
            (HEAD_W, True),
            (2 * HEAD_W, False),
            (HEAD_W, True),
            (3 * D_MODEL, False),
            (SLOT_W, False),
            (SLOT_W, False),
            (HEAD_W, True)]
    out_specs = [col(w) if tr else row(w) for w, tr in outs]
    out_shape = [jax.ShapeDtypeStruct((b, w, s) if tr else (b, s, w), BF16) for w, tr in outs]
    return pl.pallas_call(
        functools.partial(_inproj_kernel, tm=tm),
        grid=(b, s // tm),
        in_specs=[row(d),
                  pl.BlockSpec((1, 6, d), lambda i, j: (i, 0, 0)),
                  _resident((1, d)), _resident((d, IN_W)), _resident((1, LANES)),
                  _resident((1, MLA_Q_RANK)), _resident((1, MLA_KV_RANK)),
                  _resident((MLA_Q_RANK, SLOT_W)), _resident((MLA_KV_RANK, SLOT_W)),
                  _resident((MLA_KV_RANK, HEAD_W)),
                  row(LANES), row(LANES), row(LANES),
                  _resident((LANES, 2 * SLOT_W)), _resident((1, SLOT_W)), _resident((1, SLOT_W))],
        out_specs=out_specs,
        out_shape=out_shape,
        scratch_shapes=[pltpu.VMEM((1, LANES), F32)],
        compiler_params=_params(2),
        name="inproj",
    )(x, mod, g_mix, w_in_r, bff, g_q, g_kv, wuq_r, wkn_r, wkv_r, cos, sina, sinb, place, qc, kc)


def _head_rows():
    row = lax.broadcasted_iota(jnp.int32, (LANES, 1), 0)
    return [(row >= HEAD_DIM * hh) & (row < HEAD_DIM * (hh + 1)) for hh in range(2)]


def _head_masks():
    lane = lax.broadcasted_iota(jnp.int32, (1, LANES), 1)
    return [(lane >= HEAD_DIM * hh) & (lane < HEAD_DIM * (hh + 1)) for hh in range(2)]


def _col_reduce(x, op, final, chunk=64):
    rows = x.shape[0]
    if rows > chunk and rows % chunk == 0:
        acc = x[:chunk]
        for a in range(chunk, rows, chunk):
            acc = op(acc, x[a:a + chunk])
        x, rows = acc, chunk
    while rows > 8 and rows % 16 == 0:
        rows //= 2
        x = op(x[:rows], x[rows:])
    return final(x, axis=0, keepdims=True)


def _colmax(x):
    return _col_reduce(x, jnp.maximum, jnp.max)


def _colsum(x):
    return _col_reduce(x, jnp.add, jnp.sum)


def _pipeline(n, stages, unroll=PIPELINE_UNROLL):
    depth = len(stages)

    def static_step(t):
        for s, stage in enumerate(stages):
            if 0 <= t - s < n:
                stage(t - s, (t - s) % 2)

    first_full = depth - 1
    n_full = max(n - depth + 1, 0)
    for t in range(min(first_full, n + depth - 1)):
        static_step(t)
    n_loops = n_full // unroll
    if n_loops >= 1:
        def body(it, carry):
            t = first_full + unroll * it
            for u in range(unroll):
                for s, stage in enumerate(stages):
                    stage(t + u - s, (first_full + u - s) % 2)
            return carry

        lax.fori_loop(0, n_loops, body, 0)
    for t in range(first_full + unroll * n_loops, n + depth - 1):
        static_step(t)


def _block_tables(n_tiles, t, descending):
    diag = [(i * t, i * t) for i in range(n_tiles)]
    below = []
    for i in range(1, n_tiles):
        js = range(i - 1, -1, -1) if descending else range(i)
        below += [(i * t, j * t) for j in js]
    return jnp.asarray(np.array(diag + below, np.int32).T), len(diag), len(below)


def _softmax_flat_kernel(tab_ref, q_ref, k_ref, vt_ref, o_ref, acc_ref, m_ref, al_ref, top_ref,
                         s0_ref, s1_ref, p0_ref, p1_ref, *, t, n_diag, n_below, chunk_shift):
    own_rows = _head_rows()
    s_bufs, p_bufs = (s0_ref, s1_ref), (p0_ref, p1_ref)
    kio = lax.broadcasted_iota(jnp.int32, (t, 1), 0)
    qio = lax.broadcasted_iota(jnp.int32, (1, t), 1)
    acc_ref[...] = jnp.zeros_like(acc_ref)
    m_ref[...] = jnp.full_like(m_ref, NEG)

    def stages(base, diag):
        def item(i):
            return (pl.multiple_of(tab_ref[0, base + i], t), pl.multiple_of(tab_ref[1, base + i], t))

        def scores(i, par):
            q0, k0 = item(i)
            for hh in range(2):
                slot = slice(LANES * hh, LANES * (hh + 1))
                s = _dot_nt(k_ref[0, pl.ds(k0, t), slot],
                            q_ref[0, pl.ds(q0, t), slot])
                if diag:
                    s = jnp.where((kio >> chunk_shift) <= (qio >> chunk_shift), s, NEG)
                s_bufs[par][hh] = s
                top_ref[par, hh] = _colmax(s)

        def numerators(i, par):
            q0, _ = item(i)
            for hh in range(2):
                m_old = m_ref[hh, :, pl.ds(q0, t)]
                m_new = jnp.maximum(m_old, top_ref[par, hh])
                for a in range(0, t, ROW_CHUNK):
                    rows = slice(a, a + ROW_CHUNK)
                    p_bufs[par][hh, rows, :] = jnp.exp(s_bufs[par][hh, rows, :] - m_new).astype(BF16)
                al_ref[par, hh] = jnp.exp(m_old - m_new)
                m_ref[hh, :, pl.ds(q0, t)] = m_new

        def values(i, par):
            q0, k0 = item(i)
            vt = vt_ref[0, :, pl.ds(k0, t)]
            for hh in range(2):
                v1 = jnp.where(own_rows[hh], vt, jnp.ones_like(vt))
                acc_ref[hh, :, pl.ds(q0, t)] = (acc_ref[hh, :, pl.ds(q0, t)] * al_ref[par, hh]
                                                + _bdot(v1, p_bufs[par][hh]))

        return [scores, numerators, values]

    _pipeline(n_diag, stages(0, True))
    _pipeline(n_below, stages(n_diag, False))

    sub = lax.broadcasted_iota(jnp.int32, (LANES, 1), 0)

    def finish(tile, carry):
        q0 = pl.multiple_of(tile * t, t)
        outs = []
        for hh in range(2):
            acc = acc_ref[hh, :, pl.ds(q0, t)]
            denom = acc[HEAD_DIM * (1 - hh):HEAD_DIM * (1 - hh) + 1, :]
            outs.append(acc / denom)
        o_ref[0, pl.ds(q0, t), :] = jnp.where(sub < HEAD_DIM, outs[0], outs[1]).T.astype(o_ref.dtype)
        return carry

    lax.fori_loop(0, n_diag, finish, 0)


def _softmax_attention_flat(q, k, vt, *, t, chunk_shift, name):
    b, s, _ = k.shape
    table, n_diag, n_below = _block_tables(s // t, t, descending=False)
    grid_spec = pltpu.PrefetchScalarGridSpec(
        num_scalar_prefetch=1,
        grid=(b, N_HEADS // 2),
        in_specs=[pl.BlockSpec((1, s, 2 * LANES), lambda i, h, tab: (i, 0, h)),
                  pl.BlockSpec((1, s, 2 * LANES), lambda i, h, tab: (i, 0, h)),
                  pl.BlockSpec((1, LANES, s), lambda i, h, tab: (i, h, 0))],
        out_specs=pl.BlockSpec((1, s, LANES), lambda i, h, tab: (i, 0, h)),
        scratch_shapes=[pltpu.VMEM((2, LANES, s), F32),
                        pltpu.VMEM((2, 1, s), F32),
                        pltpu.VMEM((2, 2, 1, t), F32),
                        pltpu.VMEM((2, 2, 1, t), F32),
                        pltpu.VMEM((2, t, t), F32), pltpu.VMEM((2, t, t), F32),
                        pltpu.VMEM((2, t, t), BF16), pltpu.VMEM((2, t, t), BF16)])
    return pl.pallas_call(
        functools.partial(_softmax_flat_kernel, t=t, n_diag=n_diag, n_below=n_below,
                          chunk_shift=chunk_shift),
        grid_spec=grid_spec,
        out_shape=jax.ShapeDtypeStruct((b, s, HEAD_W), BF16),
        compiler_params=_params(2),
        name=name,
    )(table, q, k, vt)


def _sb_block_table(n_tiles, t):
    diag = [(i * t, i * t, i * t, i) for i in range(n_tiles)]
    below = [(i * t, (i - d) * t, i * t, i) for d in range(1, n_tiles) for i in range(d, n_tiles)]
    idle = [(0, 0, n_tiles * t, n_tiles)]
    return jnp.asarray(np.array(diag + below + idle, np.int32).T), len(diag), len(below)


def _sb_flat_kernel(tab_ref, q_ref, k_ref, vt_ref, o_ref, acc_ref, r_ref, cut_ref, tot_ref, d_ref,
                    done_ref, z0_ref, z1_ref, e0_ref, e1_ref, w0_ref, w1_ref, *, t, n_diag,
                    n_below):
    in_head = _head_masks()
    own_rows = _head_rows()
    z_bufs, e_bufs, w_bufs = (z0_ref, z1_ref), (e0_ref, e1_ref), (w0_ref, w1_ref)
    kio = lax.broadcasted_iota(jnp.int32, (t, 1), 0)
    qio = lax.broadcasted_iota(jnp.int32, (1, t), 1)
    ss = lax.broadcasted_iota(jnp.int32, (t, t), 0)
    jj = lax.broadcasted_iota(jnp.int32, (t, t), 1)
    from_here = jnp.where(jj >= ss, 1.0, 0.0).astype(BF16)
    acc_ref[...] = jnp.zeros_like(acc_ref)
    r_ref[...] = jnp.zeros_like(r_ref)
    cut_ref[...] = jnp.zeros_like(cut_ref)
    for tile in range(n_diag + 1):
        done_ref[tile] = 0

    s_len = n_diag * t
    sub8 = lax.broadcasted_iota(jnp.int32, (8, 1), 0)
    head_rows = jnp.where(sub8 == 0, jnp.where(in_head[0], 1.0, 0.0),
                          jnp.where(sub8 == 1, jnp.where(in_head[1], 1.0, 0.0), 0.0)).astype(BF16)

    def sq_norms(ref):
        x = ref[0].astype(F32)
        return _dot_nt(head_rows, (x * x).astype(BF16))

    k_max = jnp.max(sq_norms(k_ref), axis=1, keepdims=True)
    cut = NORM_MARGIN * jnp.sqrt(sq_norms(q_ref) * k_max) + UNDERFLOW
    for hh in range(2):
        cut_ref[hh, :, 0:s_len] = cut[hh:hh + 1, :]

    def stages(base, diag):
        def item(i):
            return (pl.multiple_of(tab_ref[0, base + i], t), pl.multiple_of(tab_ref[1, base + i], t),
                    pl.multiple_of(tab_ref[2, base + i], t), tab_ref[3, base + i])

        def scores(i, par):
            q0, k0, _, _ = item(i)
            k = k_ref[0, pl.ds(k0, t), :]
            q = q_ref[0, pl.ds(q0, t), :]
            for hh in range(2):
                qh = jnp.where(in_head[hh], q, jnp.zeros_like(q))
                z_bufs[par][hh] = _dot_nt(k, qh)

        def drops(i, par):
            for hh in range(2):
                for a in range(0, t, ROW_CHUNK):
                    rows = slice(a, a + ROW_CHUNK)
                    z = z_bufs[par][hh, rows, :]
                    neg_abs = pltpu.bitcast(pltpu.bitcast(z, jnp.uint32) | SIGN_BIT, F32)
                    drop = jnp.maximum(z, 0.0) + jnp.log(1.0 + jnp.exp(neg_abs))
                    if diag:
                        drop = jnp.where(kio[rows] < qio, drop, 0.0)
                    d_ref[hh, rows, :] = drop.astype(BF16)
                incl = _bdot(from_here, d_ref[hh])
                for a in range(0, t, ROW_CHUNK):
                    rows = slice(a, a + ROW_CHUNK)
                    e = z_bufs[par][hh, rows, :] - incl[rows]
                    if diag:
                        e = jnp.where(kio[rows] < qio, e, NEG)
                    e_bufs[par][hh, rows, :] = e
                tot_ref[par, hh] = incl[0:1, :]

        def weights(i, par):
            _, _, q0, tile = item(i)
            slack = None
            for hh in range(2):
                r_old = r_ref[hh, :, pl.ds(q0, t)]
                for a in range(0, t, ROW_CHUNK):
                    rows = slice(a, a + ROW_CHUNK)
                    w_bufs[par][hh, rows, :] = jnp.exp(e_bufs[par][hh, rows, :] - r_old).astype(BF16)
                r_new = r_old + tot_ref[par, hh]
                r_ref[hh, :, pl.ds(q0, t)] = r_new
                over = r_new - cut_ref[hh, :, pl.ds(q0, t)]
                slack = over if slack is None else jnp.minimum(slack, over)
            if not diag:
                done_ref[tile] = (jnp.min(slack) > 0.0).astype(jnp.int32)

        def values(i, par):
            _, k0, q0, _ = item(i)
            vt = vt_ref[0, :, pl.ds(k0, t)]
            zero = jnp.zeros_like(vt)
            lhs = jnp.concatenate([jnp.where(own_rows[hh], vt, zero) for hh in range(2)], axis=1)
            acc_ref[:, pl.ds(q0, t)] = (acc_ref[:, pl.ds(q0, t)]
                                        + _bdot(lhs, w_bufs[par][...].reshape(2 * t, t)))

        return [scores, drops, weights, values]

    _pipeline(n_diag, stages(0, True))

    scores, drops, weights, values = stages(n_diag, False)

    def next_live(c):
        return lax.while_loop(
            lambda c: (c < n_below) & (done_ref[tab_ref[3, n_diag + c]] != 0), lambda c: c + 1, c)

    def after(c):
        return next_live(jnp.minimum(c + 1, n_below))

    def two_steps(state):
        a, b, i1, i2, i3 = state
        scores(a, 0)
        drops(i1, 1)
        weights(i2, 0)
        values(i3, 1)
        scores(b, 1)
        drops(a, 0)
        weights(i1, 1)
        values(i2, 0)
        a_next = after(b)
        return a_next, after(a_next), b, a, i1

    idle = jnp.int32(n_below)
    first = next_live(jnp.int32(0))
    lax.while_loop(lambda st: (st[0] < n_below) | (st[2] < n_below) | (st[3] < n_below)
                   | (st[4] < n_below), two_steps, (first, after(first), idle, idle, idle))

    def finish(tile, carry):
        q0 = pl.multiple_of(tile * t, t)
        o_ref[0, pl.ds(q0, t), :] = acc_ref[:, pl.ds(q0, t)].T.astype(o_ref.dtype)
        return carry

    lax.fori_loop(0, n_diag, finish, 0)


def _sb_attention_flat(qk, vt, *, t):
    b, s, _ = qk.shape
    table, n_diag, n_below = _sb_block_table(s // t, t)
    grid_spec = pltpu.PrefetchScalarGridSpec(
        num_scalar_prefetch=1,
        grid=(b, N_HEADS // 2),
        in_specs=[pl.BlockSpec((1, s, LANES), lambda i, h, tab: (i, 0, h)),
                  pl.BlockSpec((1, s, LANES), lambda i, h, tab: (i, 0, 4 + h)),
                  pl.BlockSpec((1, LANES, s), lambda i, h, tab: (i, h, 0))],
        out_specs=pl.BlockSpec((1, s, LANES), lambda i, h, tab: (i, 0, h)),
        scratch_shapes=[pltpu.VMEM((LANES, s + t), F32),
                        pltpu.VMEM((2, 1, s + t), F32),
                        pltpu.VMEM((2, 1, s + t), F32),
                        pltpu.VMEM((2, 2, 1, t), F32),
                        pltpu.VMEM((2, t, t), BF16),
                        pltpu.SMEM((s // t + 1,), jnp.int32),
                        pltpu.VMEM((2, t, t), F32), pltpu.VMEM((2, t, t), F32),
                        pltpu.VMEM((2, t, t), F32), pltpu.VMEM((2, t, t), F32),
                        pltpu.VMEM((2, t, t), BF16), pltpu.VMEM((2, t, t), BF16)])
    return pl.pallas_call(
        functools.partial(_sb_flat_kernel, t=t, n_diag=n_diag, n_below=n_below),
        grid_spec=grid_spec,
        out_shape=jax.ShapeDtypeStruct((b, s, HEAD_W), BF16),
        compiler_params=_params(2),
        name="sb_attention",
    )(table, qk, qk, vt)


def _softmax_attn_kernel(q_ref, k_ref, v_ref, o_ref, acc_ref, s0_ref, s1_ref, *, tq, tk,
                         chunk_shift):
    qi = pl.program_id(2)
    in_head = _head_masks()
    qs = [q_ref[0, :, LANES * hh:LANES * (hh + 1)] for hh in range(2)]
    qpos = qi * tq + lax.broadcasted_iota(jnp.int32, (1, tq), 1)
    acc_ref[...] = jnp.zeros_like(acc_ref)

    def produce(j, buf):
        start = pl.multiple_of(j * tk, tk)
        for hh in range(2):
            k = k_ref[0, pl.ds(start, tk), LANES * hh:LANES * (hh + 1)]
            buf[hh] = _dot_nt(k, qs[hh])

    def consume(j, buf, m, masked):
        start = pl.multiple_of(j * tk, tk)
        v = v_ref[0, pl.ds(start, tk), :]
        m_out = []
        for hh in range(2):
            s = buf[hh]
            if masked:
                kpos = start + lax.broadcasted_iota(jnp.int32, (tk, 1), 0)
                s = jnp.where((kpos >> chunk_shift) <= (qpos >> chunk_shift), s, NEG)
            m_new = jnp.maximum(m[hh], _colmax(s))
            p = jnp.exp(s - m_new)
            alpha = jnp.exp(m[hh] - m_new)
            v1 = jnp.where(in_head[hh], v, jnp.ones_like(v))
            acc_ref[hh] = acc_ref[hh] * alpha + _dot_tn(v1, p.astype(BF16))
            m_out.append(m_new)
        return tuple(m_out)

    n_last = (qi * tq + tq - 1) // tk
    n_pairs = n_last // 2

    def body(i, m):
        j = 2 * i
        produce(j + 1, s1_ref)
        m = consume(j, s0_ref, m, False)
        produce(j + 2, s0_ref)
        return consume(j + 1, s1_ref, m, False)

    produce(0, s0_ref)
    m = (jnp.full((1, tq), NEG, F32),) * 2
    m = lax.fori_loop(0, n_pairs, body, m)
    j = 2 * n_pairs

    @pl.when(n_last % 2 == 1)
    def _():
        produce(j + 1, s1_ref)
        consume(j + 1, s1_ref, consume(j, s0_ref, m, False), True)

    @pl.when(n_last % 2 == 0)
    def _():
        consume(j, s0_ref, m, True)

    sub = lax.broadcasted_iota(jnp.int32, (LANES, 1), 0)
    outs = []
    for hh in range(2):
        acc = acc_ref[hh]
        denom = acc[HEAD_DIM * (1 - hh):HEAD_DIM * (1 - hh) + 1, :]
        outs.append(acc / denom)
    o_ref[0] = jnp.where(sub < HEAD_DIM, outs[0], outs[1]).T.astype(o_ref.dtype)


def _softmax_attention(q, k, v, *, tq, tk, chunk_shift, name):
    b, s, _ = q.shape
    return pl.pallas_call(
        functools.partial(_softmax_attn_kernel, tq=tq, tk=tk, chunk_shift=chunk_shift),
        grid=(b, N_HEADS // 2, s // tq),
        in_specs=[pl.BlockSpec((1, tq, 2 * LANES), lambda i, h, j: (i, j, h)),
                  pl.BlockSpec((1, s, 2 * LANES), lambda i, h, j: (i, 0, h)),
                  pl.BlockSpec((1, s, LANES), lambda i, h, j: (i, 0, h))],
        out_specs=pl.BlockSpec((1, tq, LANES), lambda i, h, j: (i, j, h)),
        out_shape=jax.ShapeDtypeStruct((b, s, HEAD_W), BF16),
        scratch_shapes=[pltpu.VMEM((2, LANES, tq), F32), pltpu.VMEM((2, tk, tq), F32),
                        pltpu.VMEM((2, tk, tq), F32)],
        compiler_params=_params(3),
        name=name,
    )(q, k, v)


def _sb_kernel(q_ref, k_ref, v_ref, o_ref, acc_ref, z_ref, r_ref, *, t):
    qi = pl.program_id(2)
    in_head = _head_masks()
    q = q_ref[0]
    qs = [jnp.where(in_head[hh], q, jnp.zeros_like(q)) for hh in range(2)]
    qpos = lax.broadcasted_iota(jnp.int32, (1, t), 1)
    kpos = lax.broadcasted_iota(jnp.int32, (t, 1), 0)
    ss = lax.broadcasted_iota(jnp.int32, (t, 2 * t), 0)
    jj = lax.broadcasted_iota(jnp.int32, (t, 2 * t), 1) & (t - 1)
    from_here = jnp.where(jj >= ss, 1.0, 0.0).astype(BF16)
    acc_ref[...] = jnp.zeros_like(acc_ref)
    r_ref[...] = jnp.zeros_like(r_ref)

    def key_start(blk):
        return pl.multiple_of(jnp.maximum(blk, 0) * t, t)

    def scores(blk_hi, nblk):
        for u in range(nblk):
            k = k_ref[0, pl.ds(key_start(blk_hi - u), t), :]
            for hh in range(2):
                z_ref[u, hh] = _dot_nt(k, qs[hh])

    def group(blk_hi, nblk, diag_first, next_hi):
        chains = []
        for u in range(nblk):
            for hh in range(2):
                z = z_ref[u, hh]
                drop = jnp.maximum(z, 0.0) + jnp.log(1.0 + jnp.exp(-jnp.abs(z)))
                if diag_first and u == 0:
                    ok = kpos < qpos
                    drop = jnp.where(ok, drop, 0.0)
                hi = drop.astype(BF16)
                lo = (drop - hi.astype(F32)).astype(BF16)
                incl = _bdot(from_here, jnp.concatenate([hi, lo], axis=0))
                e = z - incl
                if diag_first and u == 0:
                    e = jnp.where(ok, e, NEG)
                chains.append((u, hh, e, incl[0:1, :]))
        scores(next_hi, 2)
        r = [r_ref[hh] for hh in range(2)]
        upd = None
        for u, hh, e, total in chains:
            w = jnp.exp(e - r[hh])
            v = v_ref[0, pl.ds(key_start(blk_hi - u), t), :]
            vh = jnp.where(in_head[hh], v, jnp.zeros_like(v))
            pv = _dot_tn(vh, w.astype(BF16))
            upd = pv if upd is None else upd + pv
            r[hh] = r[hh] + total
        acc_ref[...] = acc_ref[...] + upd
        for hh in range(2):
            r_ref[hh] = r[hh]

    @pl.when(qi % 2 == 0)
    def _():
        scores(qi, 1)
        group(qi, 1, True, qi - 1)

    @pl.when(qi % 2 == 1)
    def _():
        scores(qi, 2)
        group(qi, 2, True, qi - 2)

    n_rest = qi // 2
    top = 2 * n_rest - 1

    def body(i, carry):
        group(top - 2 * i, 2, False, top - 2 * i - 2)
        return carry

    lax.fori_loop(0, n_rest, body, 0)
    o_ref[0] = acc_ref[...].T.astype(o_ref.dtype)


def _sb_attention(qkv, *, t):
    b, s, _ = qkv.shape
    return pl.pallas_call(
        functools.partial(_sb_kernel, t=t),
        grid=(b, N_HEADS // 2, s // t),
        in_specs=[pl.BlockSpec((1, t, LANES), lambda i, h, j: (i, j, h)),
                  pl.BlockSpec((1, s, LANES), lambda i, h, j: (i, 0, 4 + h)),
                  pl.BlockSpec((1, s, LANES), lambda i, h, j: (i, 0, 8 + h))],
        out_specs=pl.BlockSpec((1, t, LANES), lambda i, h, j: (i, j, h)),
        out_shape=jax.ShapeDtypeStruct((b, s, HEAD_W), BF16),
        scratch_shapes=[pltpu.VMEM((LANES, t), F32), pltpu.VMEM((2, 2, t, t), F32),
                        pltpu.VMEM((2, 1, t), F32)],
        compiler_params=_params(3),
        name="sb_attention",
    )(qkv, qkv, qkv)


def _merge_kernel(x_ref, yf_ref, ym_ref, ys_ref, g_ref, mod_ref, wf_ref, wm_ref, ws_ref,
                  wo_ref, o_ref):
    d = D_MODEL
    merged = (g_ref[0, :, 0:d].astype(F32) * _bdot(yf_ref[0], wf_ref[...])
              + g_ref[0, :, d:2 * d].astype(F32) * _bdot(ym_ref[0], wm_ref[...])
              + g_ref[0, :, 2 * d:3 * d].astype(F32) * _bdot(ys_ref[0], ws_ref[...]))
    o_ref[0] = x_ref[0] + mod_ref[0, 2:3, :] * _bdot(merged.astype(BF16), wo_ref[...])


def _merge(x, y_fox, y_mla, y_sb, gates, mod, wf, wm, ws, wo, *, tm):
    b, s, d = x.shape
    row = lambda w: pl.BlockSpec((1, tm, w), lambda i, j: (i, j, 0))
    return pl.pallas_call(
        _merge_kernel,
        grid=(b, s // tm),
        in_specs=[row(d), row(HEAD_W), row(HEAD_W), row(HEAD_W), row(3 * d),
                  pl.BlockSpec((1, 6, d), lambda i, j: (i, 0, 0)),
                  _resident((HEAD_W, d)), _resident((HEAD_W, d)), _resident((HEAD_W, d)),
                  _resident((d, d))],
        out_specs=row(d),
        out_shape=jax.ShapeDtypeStruct((b, s, d), F32),
        compiler_params=_params(2),
        name="merge_out",
    )(x, y_fox, y_mla, y_sb, gates, mod, wf, wm, ws, wo)


def _ffn_kernel(x_ref, mod_ref, g_ref, wg_ref, wu_ref, wd_ref, gfin_ref, o_ref, *, final, fc):
    x = x_ref[0]
    sh, sc, gt = mod_ref[0, 3:4, :], mod_ref[0, 4:5, :], mod_ref[0, 5:6, :]
    u = ((_rms(x) * g_ref[...]) * (1.0 + sc) + sh).astype(BF16)
    acc = None
    for a in range(0, D_FF, fc):
        hg = _bdot(u, wg_ref[:, a:a + fc])
        hu = _bdot(u, wu_ref[:, a:a + fc])
        h = ((hg * _sigmoid(hg)) * hu).astype(BF16)
        part = _bdot(h, wd_ref[a:a + fc, :])
        acc = part if acc is None else acc + part
    out = x + gt * acc
    if final:
        out = _rms(out) * gfin_ref[...]
    o_ref[0] = out


def _ffn(x, mod, g_ffn, wg, wu, wd, g_final, *, tm, final):
    b, s, d = x.shape
    row = lambda w: pl.BlockSpec((1, tm, w), lambda i, j: (i, j, 0))
    return pl.pallas_call(
        functools.partial(_ffn_kernel, final=final, fc=D_FF // 2),
        grid=(b, s // tm),
        in_specs=[row(d), pl.BlockSpec((1, 6, d), lambda i, j: (i, 0, 0)), _resident((1, d)),
                  _resident((d, D_FF)), _resident((d, D_FF)), _resident((D_FF, d)),
                  _resident((1, d))],
        out_specs=row(d),
        out_shape=jax.ShapeDtypeStruct((b, s, d), F32),
        compiler_params=_params(2),
        name="ffn",
    )(x, mod, g_ffn, wg, wu, wd, g_final)


def _slot_cols(w, width_per_head, used):
    k = w.shape[0]
    w = w.reshape(k, N_HEADS, width_per_head)[:, :, :used]
    return jnp.pad(w, ((0, 0), (0, 0), (0, LANES - used))).reshape(k, SLOT_W)


def _layout_w_in(w):
    d = w.shape[0]
    fox, ff, ql, kvl, kr, sb, gate = jnp.split(
        w, [1536, 1544, 1928, 2184, 2216, 3752], axis=-1)
    z = lambda n: jnp.zeros((d, n), w.dtype)
    kr_slot = jnp.concatenate([z(MLA_NOPE), kr, z(LANES - MLA_NOPE - MLA_ROPE)], axis=-1)
    ff_slot = jnp.concatenate([ff, z(LANES - N_HEADS)], axis=-1)
    return jnp.concatenate([fox, sb, gate, ql, kvl, kr_slot, ff_slot], axis=-1).astype(BF16)


def kernel(x, c, positions, g_mix, w_ada, b_ada, w_in, b_fox_f, g_mla_q, w_mla_uq, g_mla_kv,
           w_mla_ukv, w_o_fox, w_o_mla, w_o_sb, w_out, g_ffn, w_ffn_gate, w_ffn_up, w_ffn_down,
           g_final):
    b, s, d = x.shape
    depth = w_in.shape[0]
    tm = min(s, 512)
    ta = min(s, 512)
    tsb = min(s, 256)

    mod = _ada(c, w_ada, b_ada).reshape(depth, b, 6, d)
    tables = _rope_tables(positions)
    for l in range(depth):
        w_in_r = _layout_w_in(w_in[l])
        wuq_r = _slot_cols(w_mla_uq[l], MLA_NOPE + MLA_ROPE, MLA_NOPE + MLA_ROPE).astype(BF16)
        wkn_r = _slot_cols(w_mla_ukv[l], 2 * HEAD_DIM, MLA_NOPE).astype(BF16)
        wkv_r = w_mla_ukv[l].reshape(MLA_KV_RANK, N_HEADS, 2 * HEAD_DIM)[:, :, MLA_NOPE:]
        wkv_r = wkv_r.reshape(MLA_KV_RANK, HEAD_W).astype(BF16)
        bff = jnp.pad(b_fox_f[l], (0, LANES - N_HEADS)).reshape(1, LANES)

        fox_q, fox_k, fox_vt, sb_qk, sb_vt, gates, mq, mk, mvt = _inproj(
            x, mod[l], g_mix[l].reshape(1, d), w_in_r, bff, g_mla_q[l].reshape(1, -1),
            g_mla_kv[l].reshape(1, -1), wuq_r, wkn_r, wkv_r, tables, tm=tm)
        y_fox = _softmax_attention_flat(fox_q, fox_k, fox_vt, t=ta, chunk_shift=0,
                                        name="fox_attention")
        y_mla = _softmax_attention_flat(mq, mk, mvt, t=ta, chunk_shift=CHUNK_SHIFT,
                                        name="mla_attention")
        y_sb = _sb_attention_flat(sb_qk, sb_vt, t=tsb)
        x = _merge(x, y_fox, y_mla, y_sb, gates, mod[l], w_o_fox[l].astype(BF16),
                   w_o_mla[l].astype(BF16), w_o_sb[l].astype(BF16), w_out[l].astype(BF16), tm=tm)
        x = _ffn(x, mod[l], g_ffn[l].reshape(1, d), w_ffn_gate[l].astype(BF16),
                 w_ffn_up[l].astype(BF16), w_ffn_down[l].astype(BF16), g_final.reshape(1, d),
                 tm=tm, final=(l == depth - 1))
    return x
```

```python
import functools

import numpy as np
import jax
import jax.numpy as jnp
from jax import lax
from jax.experimental import pallas as pl
from jax.experimental.pallas import tpu as pltpu

F32 = jnp.float32
BF16 = jnp.bfloat16

D_MODEL = 1024
EPS = 1e-6
HEAD_DIM = 64
N_HEADS = 8
HEAD_W = N_HEADS * HEAD_DIM
CHUNK_SHIFT = 6
MLA_Q_RANK = 384
MLA_KV_RANK = 256
MLA_NOPE = 64
MLA_ROPE = 32
ROPE_HALF = MLA_ROPE // 2
ROPE_BASE = 10000.0
MLA_SCALE = (MLA_NOPE + MLA_ROPE) ** -0.5
QK_SCALE = HEAD_DIM ** -0.5
D_FF = 2816
LANES = 128
SLOT_W = N_HEADS * LANES
N_PIECES = 3

C_FQ = 0
C_FK = 512
C_FV = 1024
C_SB = 1536
C_GATE = 3072
C_QL = 6144
C_KVL = 6528
C_KR = 6784
C_FF = 6912
IN_W = 7040

ROW_CHUNK = 32
PIPELINE_UNROLL = 2
SIGN_BIT = np.uint32(0x80000000)
UNDERFLOW = 105.0
NORM_MARGIN = 1.02
NEG = -1e30
VMEM_LIMIT = 56 * 1024 * 1024


def _params(n_grid):
    return pltpu.CompilerParams(dimension_semantics=("arbitrary",) * n_grid,
                                vmem_limit_bytes=VMEM_LIMIT)


def _resident(shape):
    nd = len(shape)
    return pl.BlockSpec(shape, lambda *_: (0,) * nd, pipeline_mode=pl.Buffered(1))


def _bdot(a, b):
    return jnp.dot(a, b, preferred_element_type=F32)


def _dot_nt(a, b):
    return lax.dot_general(a, b, (((1,), (1,)), ((), ())), preferred_element_type=F32)


def _dot_tn(a, b):
    return lax.dot_general(a, b, (((0,), (0,)), ((), ())), preferred_element_type=F32)


def _split3(v):
    hi = v.astype(BF16)
    r = v - hi.astype(F32)
    mid = r.astype(BF16)
    lo = (r - mid.astype(F32)).astype(BF16)
    return hi, mid, lo


def _log_sigmoid(z):
    return jnp.minimum(z, 0.0) - jnp.log1p(jnp.exp(-jnp.abs(z)))


def _sigmoid(z):
    return 1.0 / (1.0 + jnp.exp(-z))


def _rms(x):
    return x * lax.rsqrt(jnp.mean(x * x, axis=-1, keepdims=True) + EPS)


def _ada_kernel(c_ref, w_ref, b_ref, o_ref):
    c = c_ref[...]
    cond = c * _sigmoid(c)
    chi, cmid, clo = _split3(cond)
    whi, wmid, wlo = _split3(w_ref[0])
    acc = _bdot(chi, whi) + (_bdot(chi, wmid) + _bdot(cmid, whi))
    acc = acc + (_bdot(chi, wlo) + _bdot(clo, whi) + _bdot(cmid, wmid))
    o_ref[0] = acc + b_ref[0]


def _ada(c, w_ada, b_ada):
    depth, d, n = w_ada.shape
    b = c.shape[0]
    tn = 1024
    return pl.pallas_call(
        _ada_kernel,
        grid=(depth, n // tn),
        in_specs=[pl.BlockSpec((b, d), lambda l, j: (0, 0)),
                  pl.BlockSpec((1, d, tn), lambda l, j: (l, 0, j)),
                  pl.BlockSpec((1, 1, tn), lambda l, j: (l, 0, j))],
        out_specs=pl.BlockSpec((1, b, tn), lambda l, j: (l, 0, j)),
        out_shape=jax.ShapeDtypeStruct((depth, b, n), F32),
        compiler_params=_params(2),
        name="adaln",
    )(c, w_ada, b_ada.reshape(depth, 1, n))


def _rope_kernel(pos_ref, inv_ref, cos_ref, sina_ref, sinb_ref):
    ang = pos_ref[0].astype(F32) * inv_ref[...]
    lane = lax.broadcasted_iota(jnp.int32, (1, LANES), 1)
    sin = jnp.sin(ang)
    cos_ref[0] = jnp.cos(ang)
    lo, mid, hi = MLA_NOPE, MLA_NOPE + ROPE_HALF, MLA_NOPE + MLA_ROPE
    sina_ref[0] = jnp.where((lane >= lo) & (lane < mid), -sin, 0.0)
    sinb_ref[0] = jnp.where((lane >= mid) & (lane < hi), sin, 0.0)


def _rope_tables(positions):
    b, s = positions.shape
    ts = min(s, 512)
    inv = ROPE_BASE ** (-jnp.arange(ROPE_HALF, dtype=F32) / ROPE_HALF)
    inv_slot = jnp.concatenate([jnp.zeros((MLA_NOPE,), F32), inv, inv,
                                jnp.zeros((LANES - MLA_NOPE - MLA_ROPE,), F32)]).reshape(1, LANES)
    spec = pl.BlockSpec((1, ts, LANES), lambda i, j: (i, j, 0))
    return pl.pallas_call(
        _rope_kernel,
        grid=(b, s // ts),
        in_specs=[pl.BlockSpec((1, ts, 1), lambda i, j: (i, j, 0)),
                  pl.BlockSpec((1, LANES), lambda i, j: (0, 0))],
        out_specs=[spec, spec, spec],
        out_shape=[jax.ShapeDtypeStruct((b, s, LANES), F32)] * 3,
        compiler_params=_params(2),
        name="rope_tables",
    )(positions.reshape(b, s, 1), inv_slot)


def _rope_slot(x, cos, sina, sinb):
    return (x * cos + pltpu.roll(x, LANES - ROPE_HALF, 1) * sina
            + pltpu.roll(x, ROPE_HALF, 1) * sinb)


def _inproj_kernel(x_ref, mod_ref, g_ref, w_ref, bff_ref, gq_ref, gkv_ref, wuq_ref, wkn_ref,
                   wkv_ref, cos_ref, sina_ref, sinb_ref, place_ref, qc_ref, kc_ref,
                   fq_ref, fk_ref, fv_ref, sb_ref, sv_ref, gate_ref, mq_ref, mk_ref, mv_ref,
                   carry_ref, *, tm):
    x = x_ref[0]
    sh, sc = mod_ref[0, 0:1, :], mod_ref[0, 1:2, :]
    u = ((_rms(x) * g_ref[...]) * (1.0 + sc) + sh).astype(BF16)

    def seg(a, b):
        return _bdot(u, w_ref[:, a:b])

    @pl.when(pl.program_id(1) == 0)
    def _():
        carry_ref[...] = jnp.zeros_like(carry_ref)

    logf = _log_sigmoid(seg(C_FF, IN_W) + bff_ref[...])
    r = lax.broadcasted_iota(jnp.int32, (tm, tm), 0)
    c = lax.broadcasted_iota(jnp.int32, (tm, tm), 1)
    tri = jnp.where(r >= c, 1.0, 0.0).astype(BF16)
    hi, mid, lo = _split3(logf)
    cum = (_bdot(tri, hi) + _bdot(tri, mid) + _bdot(tri, lo)) + carry_ref[...]
    carry_ref[...] = cum[tm - 1:tm, :]
    p0, p1, p2 = (p.astype(F32) for p in _split3(cum))
    lane = lax.broadcasted_iota(jnp.int32, (1, LANES), 1)
    pieces = jnp.where(lane < N_HEADS, p0,
                       jnp.where(lane < 2 * N_HEADS, pltpu.roll(p1, N_HEADS, 1),
                                 jnp.where(lane < 3 * N_HEADS, pltpu.roll(p2, 2 * N_HEADS, 1),
                                           0.0))).astype(BF16)

    def to_slots(compact, placed, const_ref, out_ref):
        for h in range(N_HEADS):
            x = compact[:, LANES * (h // 2):LANES * (h // 2 + 1)]
            head = pltpu.roll(x, HEAD_DIM, 1) if h % 2 else x
            slot = slice(LANES * h, LANES * (h + 1))
            extra = placed[:, slot] + const_ref[:, slot]
            out_ref[0, :, slot] = jnp.where(lane < HEAD_DIM, head, extra).astype(BF16)

    to_slots(seg(C_FQ, C_FK) * QK_SCALE, _bdot(pieces, place_ref[:, :SLOT_W]), qc_ref, fq_ref)
    to_slots(seg(C_FK, C_FV), _bdot(pieces, place_ref[:, SLOT_W:]), kc_ref, fk_ref)
    fv_ref[0] = seg(C_FV, C_SB).astype(BF16).T

    sb_ref[0, :, 0:512] = (seg(C_SB, C_SB + 512) * QK_SCALE).astype(BF16)
    sb_ref[0, :, 512:1024] = seg(C_SB + 512, C_SB + 1024).astype(BF16)
    sv_ref[0] = seg(C_SB + 1024, C_GATE).astype(BF16).T
    for c in range(6):
        a = 512 * c
        gate_ref[0, :, a:a + 512] = _sigmoid(seg(C_GATE + a, C_GATE + a + 512)).astype(BF16)

    cos, sina, sinb = cos_ref[0], sina_ref[0], sinb_ref[0]
    cq = (_rms(seg(C_QL, C_KVL)) * gq_ref[...]).astype(BF16)
    ckv = (_rms(seg(C_KVL, C_KR)) * gkv_ref[...]).astype(BF16)
    kr = _rope_slot(seg(C_KR, C_FF), cos, sina, sinb)
    for h in range(N_HEADS):
        a = LANES * h
        q = _rope_slot(_bdot(cq, wuq_ref[:, a:a + LANES]), cos, sina, sinb)
        mq_ref[0, :, a:a + LANES] = (q * MLA_SCALE).astype(BF16)
        mk_ref[0, :, a:a + LANES] = (_bdot(ckv, wkn_ref[:, a:a + LANES]) + kr).astype(BF16)
    mv_ref[0] = _bdot(ckv, wkv_ref[...]).astype(BF16).T


def _fox_placement():
    place = np.zeros((LANES, 2 * SLOT_W), np.float32)
    qc = np.zeros((1, SLOT_W), np.float32)
    kc = np.zeros((1, SLOT_W), np.float32)
    for h in range(N_HEADS):
        for p in range(N_PIECES):
            place[N_HEADS * p + h, LANES * h + HEAD_DIM + N_PIECES + p] = 1.0
            place[N_HEADS * p + h, SLOT_W + LANES * h + HEAD_DIM + p] = 1.0
            qc[0, LANES * h + HEAD_DIM + p] = -1.0
            kc[0, LANES * h + HEAD_DIM + N_PIECES + p] = 1.0
    return jnp.asarray(place, BF16), jnp.asarray(qc), jnp.asarray(kc)


def _inproj(x, mod, g_mix, w_in_r, bff, g_q, g_kv, wuq_r, wkn_r, wkv_r, tables, *, tm):
    b, s, d = x.shape
    cos, sina, sinb = tables
    place, qc, kc = _fox_placement()
    row = lambda w: pl.BlockSpec((1, tm, w), lambda i, j: (i, j, 0))
    col = lambda w: pl.BlockSpec((1, w, tm), lambda i, j: (i, 0, j))
    outs = [(SLOT_W, False),
            (SLOT_W, False),
            (HEAD_W, True),
            (2 * HEAD_W, False),
            (HEAD_W, True),
            (3 * D_MODEL, False),
            (SLOT_W, False),
            (SLOT_W, False),
            (HEAD_W, True)]
    out_specs = [col(w) if tr else row(w) for w, tr in outs]
    out_shape = [jax.ShapeDtypeStruct((b, w, s) if tr else (b, s, w), BF16) for w, tr in outs]
    return pl.pallas_call(
        functools.partial(_inproj_kernel, tm=tm),
        grid=(b, s // tm),
        in_specs=[row(d),
                  pl.BlockSpec((1, 6, d), lambda i, j: (i, 0, 0)),
                  _resident((1, d)), _resident((d, IN_W)), _resident((1, LANES)),
                  _resident((1, MLA_Q_RANK)), _resident((1, MLA_KV_RANK)),
                  _resident((MLA_Q_RANK, SLOT_W)), _resident((MLA_KV_RANK, SLOT_W)),
                  _resident((MLA_KV_RANK, HEAD_W)),
                  row(LANES), row(LANES), row(LANES),
                  _resident((LANES, 2 * SLOT_W)), _resident((1, SLOT_W)), _resident((1, SLOT_W))],
        out_specs=out_specs,
        out_shape=out_shape,
        scratch_shapes=[pltpu.VMEM((1, LANES), F32)],
        compiler_params=_params(2),
        name="inproj",
    )(x, mod, g_mix, w_in_r, bff, g_q, g_kv, wuq_r, wkn_r, wkv_r, cos, sina, sinb, place, qc, kc)


def _head_rows():
    row = lax.broadcasted_iota(jnp.int32, (LANES, 1), 0)
    return [(row >= HEAD_DIM * hh) & (row < HEAD_DIM * (hh + 1)) for hh in range(2)]


def _head_masks():
    lane = lax.broadcasted_iota(jnp.int32, (1, LANES), 1)
    return [(lane >= HEAD_DIM * hh) & (lane < HEAD_DIM * (hh + 1)) for hh in range(2)]


def _col_reduce(x, op, final, chunk=64):
    rows = x.shape[0]
    if rows > chunk and rows % chunk == 0:
        acc = x[:chunk]
        for a in range(chunk, rows, chunk):
            acc = op(acc, x[a:a + chunk])
        x, rows = acc, chunk
    while rows > 8 and rows % 16 == 0:
        rows //= 2
        x = op(x[:rows], x[rows:])
    return final(x, axis=0, keepdims=True)


def _colmax(x):
    return _col_reduce(x, jnp.maximum, jnp.max)


def _colsum(x):
    return _col_reduce(x, jnp.add, jnp.sum)


def _pipeline(n, stages, unroll=PIPELINE_UNROLL):
    depth = len(stages)

    def static_step(t):
        for s, stage in enumerate(stages):
            if 0 <= t - s < n:
                stage(t - s, (t - s) % 2)

    first_full = depth - 1
    n_full = max(n - depth + 1, 0)
    for t in range(min(first_full, n + depth - 1)):
        static_step(t)
    n_loops = n_full // unroll
    if n_loops >= 1:
        def body(it, carry):
            t = first_full + unroll * it
            for u in range(unroll):
                for s, stage in enumerate(stages):
                    stage(t + u - s, (first_full + u - s) % 2)
            return carry

        lax.fori_loop(0, n_loops, body, 0)
    for t in range(first_full + unroll * n_loops, n + depth - 1):
        static_step(t)


def _block_tables(n_tiles, t, descending):
    diag = [(i * t, i * t) for i in range(n_tiles)]
    below = []
    for i in range(1, n_tiles):
        js = range(i - 1, -1, -1) if descending else range(i)
        below += [(i * t, j * t) for j in js]
    return jnp.asarray(np.array(diag + below, np.int32).T), len(diag), len(below)


def _softmax_flat_kernel(tab_ref, q_ref, k_ref, vt_ref, o_ref, acc_ref, m_ref, al_ref, top_ref,
                         s0_ref, s1_ref, p0_ref, p1_ref, *, t, n_diag, n_below, chunk_shift):
    own_rows = _head_rows()
    s_bufs, p_bufs = (s0_ref, s1_ref), (p0_ref, p1_ref)
    kio = lax.broadcasted_iota(jnp.int32, (t, 1), 0)
    qio = lax.broadcasted_iota(jnp.int32, (1, t), 1)
    acc_ref[...] = jnp.zeros_like(acc_ref)
    m_ref[...] = jnp.full_like(m_ref, NEG)

    def stages(base, diag):
        def item(i):
            return (pl.multiple_of(tab_ref[0, base + i], t), pl.multiple_of(tab_ref[1, base + i], t))

        def scores(i, par):
            q0, k0 = item(i)
            for hh in range(2):
                slot = slice(LANES * hh, LANES * (hh + 1))
                s = _dot_nt(k_ref[0, pl.ds(k0, t), slot],
                            q_ref[0, pl.ds(q0, t), slot])
                if diag:
                    s = jnp.where((kio >> chunk_shift) <= (qio >> chunk_shift), s, NEG)
                s_bufs[par][hh] = s
                top_ref[par, hh] = _colmax(s)

        def numerators(i, par):
            q0, _ = item(i)
            for hh in range(2):
                m_old = m_ref[hh, :, pl.ds(q0, t)]
                m_new = jnp.maximum(m_old, top_ref[par, hh])
                for a in range(0, t, ROW_CHUNK):
                    rows = slice(a, a + ROW_CHUNK)
                    p_bufs[par][hh, rows, :] = jnp.exp(s_bufs[par][hh, rows, :] - m_new).astype(BF16)
                al_ref[par, hh] = jnp.exp(m_old - m_new)
                m_ref[hh, :, pl.ds(q0, t)] = m_new

        def values(i, par):
            q0, k0 = item(i)
            vt = vt_ref[0, :, pl.ds(k0, t)]
            for hh in range(2):
                v1 = jnp.where(own_rows[hh], vt, jnp.ones_like(vt))
                acc_ref[hh, :, pl.ds(q0, t)] = (acc_ref[hh, :, pl.ds(q0, t)] * al_ref[par, hh]
                                                + _bdot(v1, p_bufs[par][hh]))

        return [scores, numerators, values]

    _pipeline(n_diag, stages(0, True))
    _pipeline(n_below, stages(n_diag, False))

    sub = lax.broadcasted_iota(jnp.int32, (LANES, 1), 0)

    def finish(tile, carry):
        q0 = pl.multiple_of(tile * t, t)
        outs = []
        for hh in range(2):
            acc = acc_ref[hh, :, pl.ds(q0, t)]
            denom = acc[HEAD_DIM * (1 - hh):HEAD_DIM * (1 - hh) + 1, :]
            outs.append(acc / denom)
        o_ref[0, pl.ds(q0, t), :] = jnp.where(sub < HEAD_DIM, outs[0], outs[1]).T.astype(o_ref.dtype)
        return carry

    lax.fori_loop(0, n_diag, finish, 0)


def _softmax_attention_flat(q, k, vt, *, t, chunk_shift, name):
    b, s, _ = k.shape
    table, n_diag, n_below = _block_tables(s // t, t, descending=False)
    grid_spec = pltpu.PrefetchScalarGridSpec(
        num_scalar_prefetch=1,
        grid=(b, N_HEADS // 2),
        in_specs=[pl.BlockSpec((1, s, 2 * LANES), lambda i, h, tab: (i, 0, h)),
                  pl.BlockSpec((1, s, 2 * LANES), lambda i, h, tab: (i, 0, h)),
                  pl.BlockSpec((1, LANES, s), lambda i, h, tab: (i, h, 0))],
        out_specs=pl.BlockSpec((1, s, LANES), lambda i, h, tab: (i, 0, h)),
        scratch_shapes=[pltpu.VMEM((2, LANES, s), F32),
                        pltpu.VMEM((2, 1, s), F32),
                        pltpu.VMEM((2, 2, 1, t), F32),
                        pltpu.VMEM((2, 2, 1, t), F32),
                        pltpu.VMEM((2, t, t), F32), pltpu.VMEM((2, t, t), F32),
                        pltpu.VMEM((2, t, t), BF16), pltpu.VMEM((2, t, t), BF16)])
    return pl.pallas_call(
        functools.partial(_softmax_flat_kernel, t=t, n_diag=n_diag, n_below=n_below,
                          chunk_shift=chunk_shift),
        grid_spec=grid_spec,
        out_shape=jax.ShapeDtypeStruct((b, s, HEAD_W), BF16),
        compiler_params=_params(2),
        name=name,
    )(table, q, k, vt)


def _sb_block_table(n_tiles, t):
    diag = [(i * t, i * t, i * t, i) for i in range(n_tiles)]
    below = [(i * t, (i - d) * t, i * t, i) for d in range(1, n_tiles) for i in range(d, n_tiles)]
    idle = [(0, 0, n_tiles * t, n_tiles)]
    return jnp.asarray(np.array(diag + below + idle, np.int32).T), len(diag), len(below)


def _sb_flat_kernel(tab_ref, q_ref, k_ref, vt_ref, o_ref, acc_ref, r_ref, cut_ref, tot_ref, d_ref,
                    done_ref, z0_ref, z1_ref, e0_ref, e1_ref, w0_ref, w1_ref, *, t, n_diag,
                    n_below):
    in_head = _head_masks()
    own_rows = _head_rows()
    z_bufs, e_bufs, w_bufs = (z0_ref, z1_ref), (e0_ref, e1_ref), (w0_ref, w1_ref)
    kio = lax.broadcasted_iota(jnp.int32, (t, 1), 0)
    qio = lax.broadcasted_iota(jnp.int32, (1, t), 1)
    ss = lax.broadcasted_iota(jnp.int32, (t, t), 0)
    jj = lax.broadcasted_iota(jnp.int32, (t, t), 1)
    from_here = jnp.where(jj >= ss, 1.0, 0.0).astype(BF16)
    acc_ref[...] = jnp.zeros_like(acc_ref)
    r_ref[...] = jnp.zeros_like(r_ref)
    cut_ref[...] = jnp.zeros_like(cut_ref)
    for tile in range(n_diag + 1):
        done_ref[tile] = 0

    s_len = n_diag * t
    sub8 = lax.broadcasted_iota(jnp.int32, (8, 1), 0)
    head_rows = jnp.where(sub8 == 0, jnp.where(in_head[0], 1.0, 0.0),
                          jnp.where(sub8 == 1, jnp.where(in_head[1], 1.0, 0.0), 0.0)).astype(BF16)

    def sq_norms(ref):
        x = ref[0].astype(F32)
        return _dot_nt(head_rows, (x * x).astype(BF16))

    k_max = jnp.max(sq_norms(k_ref), axis=1, keepdims=True)
    cut = NORM_MARGIN * jnp.sqrt(sq_norms(q_ref) * k_max) + UNDERFLOW
    for hh in range(2):
        cut_ref[hh, :, 0:s_len] = cut[hh:hh + 1, :]

    def stages(base, diag):
        def item(i):
            return (pl.multiple_of(tab_ref[0, base + i], t), pl.multiple_of(tab_ref[1, base + i], t),
                    pl.multiple_of(tab_ref[2, base + i], t), tab_ref[3, base + i])

        def scores(i, par):
            q0, k0, _, _ = item(i)
            k = k_ref[0, pl.ds(k0, t), :]
            q = q_ref[0, pl.ds(q0, t), :]
            for hh in range(2):
                qh = jnp.where(in_head[hh], q, jnp.zeros_like(q))
                z_bufs[par][hh] = _dot_nt(k, qh)

        def drops(i, par):
            for hh in range(2):
                for a in range(0, t, ROW_CHUNK):
                    rows = slice(a, a + ROW_CHUNK)
                    z = z_bufs[par][hh, rows, :]
                    neg_abs = pltpu.bitcast(pltpu.bitcast(z, jnp.uint32) | SIGN_BIT, F32)
                    drop = jnp.maximum(z, 0.0) + jnp.log(1.0 + jnp.exp(neg_abs))
                    if diag:
                        drop = jnp.where(kio[rows] < qio, drop, 0.0)
                    d_ref[hh, rows, :] = drop.astype(BF16)
                incl = _bdot(from_here, d_ref[hh])
                for a in range(0, t, ROW_CHUNK):
                    rows = slice(a, a + ROW_CHUNK)
                    e = z_bufs[par][hh, rows, :] - incl[rows]
                    if diag:
                        e = jnp.where(kio[rows] < qio, e, NEG)
                    e_bufs[par][hh, rows, :] = e
                tot_ref[par, hh] = incl[0:1, :]

        def weights(i, par):
            _, _, q0, tile = item(i)
            slack = None
            for hh in range(2):
                r_old = r_ref[hh, :, pl.ds(q0, t)]
                for a in range(0, t, ROW_CHUNK):
                    rows = slice(a, a + ROW_CHUNK)
                    w_bufs[par][hh, rows, :] = jnp.exp(e_bufs[par][hh, rows, :] - r_old).astype(BF16)
                r_new = r_old + tot_ref[par, hh]
                r_ref[hh, :, pl.ds(q0, t)] = r_new
                over = r_new - cut_ref[hh, :, pl.ds(q0, t)]
                slack = over if slack is None else jnp.minimum(slack, over)
            if not diag:
                done_ref[tile] = (jnp.min(slack) > 0.0).astype(jnp.int32)

        def values(i, par):
            _, k0, q0, _ = item(i)
            vt = vt_ref[0, :, pl.ds(k0, t)]
            zero = jnp.zeros_like(vt)
            lhs = jnp.concatenate([jnp.where(own_rows[hh], vt, zero) for hh in range(2)], axis=1)
            acc_ref[:, pl.ds(q0, t)] = (acc_ref[:, pl.ds(q0, t)]
                                        + _bdot(lhs, w_bufs[par][...].reshape(2 * t, t)))

        return [scores, drops, weights, values]

    _pipeline(n_diag, stages(0, True))

    scores, drops, weights, values = stages(n_diag, False)

    def next_live(c):
        return lax.while_loop(
            lambda c: (c < n_below) & (done_ref[tab_ref[3, n_diag + c]] != 0), lambda c: c + 1, c)

    def after(c):
        return next_live(jnp.minimum(c + 1, n_below))

    def two_steps(state):
        a, b, i1, i2, i3 = state
        scores(a, 0)
        drops(i1, 1)
        weights(i2, 0)
        values(i3, 1)
        scores(b, 1)
        drops(a, 0)
        weights(i1, 1)
        values(i2, 0)
        a_next = after(b)
        return a_next, after(a_next), b, a, i1

    idle = jnp.int32(n_below)
    first = next_live(jnp.int32(0))
    lax.while_loop(lambda st: (st[0] < n_below) | (st[2] < n_below) | (st[3] < n_below)
                   | (st[4] < n_below), two_steps, (first, after(first), idle, idle, idle))

    def finish(tile, carry):
        q0 = pl.multiple_of(tile * t, t)
        o_ref[0, pl.ds(q0, t), :] = acc_ref[:, pl.ds(q0, t)].T.astype(o_ref.dtype)
        return carry

    lax.fori_loop(0, n_diag, finish, 0)


def _sb_attention_flat(qk, vt, *, t):
    b, s, _ = qk.shape
    table, n_diag, n_below = _sb_block_table(s // t, t)
    grid_spec = pltpu.PrefetchScalarGridSpec(
        num_scalar_prefetch=1,
        grid=(b, N_HEADS // 2),
        in_specs=[pl.BlockSpec((1, s, LANES), lambda i, h, tab: (i, 0, h)),
                  pl.BlockSpec((1, s, LANES), lambda i, h, tab: (i, 0, 4 + h)),
                  pl.BlockSpec((1, LANES, s), lambda i, h, tab: (i, h, 0))],
        out_specs=pl.BlockSpec((1, s, LANES), lambda i, h, tab: (i, 0, h)),
        scratch_shapes=[pltpu.VMEM((LANES, s + t), F32),
                        pltpu.VMEM((2, 1, s + t), F32),
                        pltpu.VMEM((2, 1, s + t), F32),
                        pltpu.VMEM((2, 2, 1, t), F32),
                        pltpu.VMEM((2, t, t), BF16),
                        pltpu.SMEM((s // t + 1,), jnp.int32),
                        pltpu.VMEM((2, t, t), F32), pltpu.VMEM((2, t, t), F32),
                        pltpu.VMEM((2, t, t), F32), pltpu.VMEM((2, t, t), F32),
                        pltpu.VMEM((2, t, t), BF16), pltpu.VMEM((2, t, t), BF16)])
    return pl.pallas_call(
        functools.partial(_sb_flat_kernel, t=t, n_diag=n_diag, n_below=n_below),
        grid_spec=grid_spec,
        out_shape=jax.ShapeDtypeStruct((b, s, HEAD_W), BF16),
        compiler_params=_params(2),
        name="sb_attention",
    )(table, qk, qk, vt)


def _softmax_attn_kernel(q_ref, k_ref, v_ref, o_ref, acc_ref, s0_ref, s1_ref, *, tq, tk,
                         chunk_shift):
    qi = pl.program_id(2)
    in_head = _head_masks()
    qs = [q_ref[0, :, LANES * hh:LANES * (hh + 1)] for hh in range(2)]
    qpos = qi * tq + lax.broadcasted_iota(jnp.int32, (1, tq), 1)
    acc_ref[...] = jnp.zeros_like(acc_ref)

    def produce(j, buf):
        start = pl.multiple_of(j * tk, tk)
        for hh in range(2):
            k = k_ref[0, pl.ds(start, tk), LANES * hh:LANES * (hh + 1)]
            buf[hh] = _dot_nt(k, qs[hh])

    def consume(j, buf, m, masked):
        start = pl.multiple_of(j * tk, tk)
        v = v_ref[0, pl.ds(start, tk), :]
        m_out = []
        for hh in range(2):
            s = buf[hh]
            if masked:
                kpos = start + lax.broadcasted_iota(jnp.int32, (tk, 1), 0)
                s = jnp.where((kpos >> chunk_shift) <= (qpos >> chunk_shift), s, NEG)
            m_new = jnp.maximum(m[hh], _colmax(s))
            p = jnp.exp(s - m_new)
            alpha = jnp.exp(m[hh] - m_new)
            v1 = jnp.where(in_head[hh], v, jnp.ones_like(v))
            acc_ref[hh] = acc_ref[hh] * alpha + _dot_tn(v1, p.astype(BF16))
            m_out.append(m_new)
        return tuple(m_out)

    n_last = (qi * tq + tq - 1) // tk
    n_pairs = n_last // 2

    def body(i, m):
        j = 2 * i
        produce(j + 1, s1_ref)
        m = consume(j, s0_ref, m, False)
        produce(j + 2, s0_ref)
        return consume(j + 1, s1_ref, m, False)

    produce(0, s0_ref)
    m = (jnp.full((1, tq), NEG, F32),) * 2
    m = lax.fori_loop(0, n_pairs, body, m)
    j = 2 * n_pairs

    @pl.when(n_last % 2 == 1)
    def _():
        produce(j + 1, s1_ref)
        consume(j + 1, s1_ref, consume(j, s0_ref, m, False), True)

    @pl.when(n_last % 2 == 0)
    def _():
        consume(j, s0_ref, m, True)

    sub = lax.broadcasted_iota(jnp.int32, (LANES, 1), 0)
    outs = []
    for hh in range(2):
        acc = acc_ref[hh]
        denom = acc[HEAD_DIM * (1 - hh):HEAD_DIM * (1 - hh) + 1, :]
        outs.append(acc / denom)
    o_ref[0] = jnp.where(sub < HEAD_DIM, outs[0], outs[1]).T.astype(o_ref.dtype)


def _softmax_attention(q, k, v, *, tq, tk, chunk_shift, name):
    b, s, _ = q.shape
    return pl.pallas_call(
        functools.partial(_softmax_attn_kernel, tq=tq, tk=tk, chunk_shift=chunk_shift),
        grid=(b, N_HEADS // 2, s // tq),
        in_specs=[pl.BlockSpec((1, tq, 2 * LANES), lambda i, h, j: (i, j, h)),
                  pl.BlockSpec((1, s, 2 * LANES), lambda i, h, j: (i, 0, h)),
                  pl.BlockSpec((1, s, LANES), lambda i, h, j: (i, 0, h))],
        out_specs=pl.BlockSpec((1, tq, LANES), lambda i, h, j: (i, j, h)),
        out_shape=jax.ShapeDtypeStruct((b, s, HEAD_W), BF16),
        scratch_shapes=[pltpu.VMEM((2, LANES, tq), F32), pltpu.VMEM((2, tk, tq), F32),
                        pltpu.VMEM((2, tk, tq), F32)],
        compiler_params=_params(3),
        name=name,
    )(q, k, v)


def _sb_kernel(q_ref, k_ref, v_ref, o_ref, acc_ref, z_ref, r_ref, *, t):
    qi = pl.program_id(2)
    in_head = _head_masks()
    q = q_ref[0]
    qs = [jnp.where(in_head[hh], q, jnp.zeros_like(q)) for hh in range(2)]
    qpos = lax.broadcasted_iota(jnp.int32, (1, t), 1)
    kpos = lax.broadcasted_iota(jnp.int32, (t, 1), 0)
    ss = lax.broadcasted_iota(jnp.int32, (t, 2 * t), 0)
    jj = lax.broadcasted_iota(jnp.int32, (t, 2 * t), 1) & (t - 1)
    from_here = jnp.where(jj >= ss, 1.0, 0.0).astype(BF16)
    acc_ref[...] = jnp.zeros_like(acc_ref)
    r_ref[...] = jnp.zeros_like(r_ref)

    def key_start(blk):
        return pl.multiple_of(jnp.maximum(blk, 0) * t, t)

    def scores(blk_hi, nblk):
        for u in range(nblk):
            k = k_ref[0, pl.ds(key_start(blk_hi - u), t), :]
            for hh in range(2):
                z_ref[u, hh] = _dot_nt(k, qs[hh])

    def group(blk_hi, nblk, diag_first, next_hi):
        chains = []
        for u in range(nblk):
            for hh in range(2):
                z = z_ref[u, hh]
                drop = jnp.maximum(z, 0.0) + jnp.log(1.0 + jnp.exp(-jnp.abs(z)))
                if diag_first and u == 0:
                    ok = kpos < qpos
                    drop = jnp.where(ok, drop, 0.0)
                hi = drop.astype(BF16)
                lo = (drop - hi.astype(F32)).astype(BF16)
                incl = _bdot(from_here, jnp.concatenate([hi, lo], axis=0))
                e = z - incl
                if diag_first and u == 0:
                    e = jnp.where(ok, e, NEG)
                chains.append((u, hh, e, incl[0:1, :]))
        scores(next_hi, 2)
        r = [r_ref[hh] for hh in range(2)]
        upd = None
        for u, hh, e, total in chains:
            w = jnp.exp(e - r[hh])
            v = v_ref[0, pl.ds(key_start(blk_hi - u), t), :]
            vh = jnp.where(in_head[hh], v, jnp.zeros_like(v))
            pv = _dot_tn(vh, w.astype(BF16))
            upd = pv if upd is None else upd + pv
            r[hh] = r[hh] + total
        acc_ref[...] = acc_ref[...] + upd
        for hh in range(2):
            r_ref[hh] = r[hh]

    @pl.when(qi % 2 == 0)
    def _():
        scores(qi, 1)
        group(qi, 1, True, qi - 1)

    @pl.when(qi % 2 == 1)
    def _():
        scores(qi, 2)
        group(qi, 2, True, qi - 2)

    n_rest = qi // 2
    top = 2 * n_rest - 1

    def body(i, carry):
        group(top - 2 * i, 2, False, top - 2 * i - 2)
        return carry

    lax.fori_loop(0, n_rest, body, 0)
    o_ref[0] = acc_ref[...].T.astype(o_ref.dtype)


def _sb_attention(qkv, *, t):
    b, s, _ = qkv.shape
    return pl.pallas_call(
        functools.partial(_sb_kernel, t=t),
        grid=(b, N_HEADS // 2, s // t),
        in_specs=[pl.BlockSpec((1, t, LANES), lambda i, h, j: (i, j, h)),
                  pl.BlockSpec((1, s, LANES), lambda i, h, j: (i, 0, 4 + h)),
                  pl.BlockSpec((1, s, LANES), lambda i, h, j: (i, 0, 8 + h))],
        out_specs=pl.BlockSpec((1, t, LANES), lambda i, h, j: (i, j, h)),
        out_shape=jax.ShapeDtypeStruct((b, s, HEAD_W), BF16),
        scratch_shapes=[pltpu.VMEM((LANES, t), F32), pltpu.VMEM((2, 2, t, t), F32),
                        pltpu.VMEM((2, 1, t), F32)],
        compiler_params=_params(3),
        name="sb_attention",
    )(qkv, qkv, qkv)


def _merge_ffn_kernel(x_ref, yf_ref, ym_ref, ys_ref, gate_ref, mod_ref, wf_ref, wm_ref, ws_ref,
                      wo_ref, g_ref, wg_ref, wu_ref, wd_ref, gfin_ref, o_ref, *, final, fc):
    d = D_MODEL
    merged = (gate_ref[0, :, 0:d].astype(F32) * _bdot(yf_ref[0], wf_ref[...])
              + gate_ref[0, :, d:2 * d].astype(F32) * _bdot(ym_ref[0], wm_ref[...])
              + gate_ref[0, :, 2 * d:3 * d].astype(F32) * _bdot(ys_ref[0], ws_ref[...]))
    x = x_ref[0] + mod_ref[0, 2:3, :] * _bdot(merged.astype(BF16), wo_ref[...])

    sh, sc, gt = mod_ref[0, 3:4, :], mod_ref[0, 4:5, :], mod_ref[0, 5:6, :]
    u = ((_rms(x) * g_ref[...]) * (1.0 + sc) + sh).astype(BF16)
    acc = None
    for a in range(0, D_FF, fc):
        hg = _bdot(u, wg_ref[:, a:a + fc])
        hu = _bdot(u, wu_ref[:, a:a + fc])
        h = ((hg * _sigmoid(hg)) * hu).astype(BF16)
        part = _bdot(h, wd_ref[a:a + fc, :])
        acc = part if acc is None else acc + part
    out = x + gt * acc
    if final:
        out = _rms(out) * gfin_ref[...]
    o_ref[0] = out


def _merge_ffn(x, y_fox, y_mla, y_sb, gates, mod, wf, wm, ws, wo, g_ffn, wg, wu, wd, g_final, *,
               tm, final):
    b, s, d = x.shape
    row = lambda w: pl.BlockSpec((1, tm, w), lambda i, j: (i, j, 0))
    return pl.pallas_call(
        functools.partial(_merge_ffn_kernel, final=final, fc=D_FF // 2),
        grid=(b, s // tm),
        in_specs=[row(d), row(HEAD_W), row(HEAD_W), row(HEAD_W), row(3 * d),
                  pl.BlockSpec((1, 6, d), lambda i, j: (i, 0, 0)),
                  _resident((HEAD_W, d)), _resident((HEAD_W, d)), _resident((HEAD_W, d)),
                  _resident((d, d)), _resident((1, d)),
                  _resident((d, D_FF)), _resident((d, D_FF)), _resident((D_FF, d)),
                  _resident((1, d))],
        out_specs=row(d),
        out_shape=jax.ShapeDtypeStruct((b, s, d), F32),
        compiler_params=_params(2),
        name="merge_ffn",
    )(x, y_fox, y_mla, y_sb, gates, mod, wf, wm, ws, wo, g_ffn, wg, wu, wd, g_final)


def _slot_cols(w, width_per_head, used):
    k = w.shape[0]
    w = w.reshape(k, N_HEADS, width_per_head)[:, :, :used]
    return jnp.pad(w, ((0, 0), (0, 0), (0, LANES - used))).reshape(k, SLOT_W)


def _layout_w_in(w):
    d = w.shape[0]
    fox, ff, ql, kvl, kr, sb, gate = jnp.split(
        w, [1536, 1544, 1928, 2184, 2216, 3752], axis=-1)
    z = lambda n: jnp.zeros((d, n), w.dtype)
    kr_slot = jnp.concatenate([z(MLA_NOPE), kr, z(LANES - MLA_NOPE - MLA_ROPE)], axis=-1)
    ff_slot = jnp.concatenate([ff, z(LANES - N_HEADS)], axis=-1)
    return jnp.concatenate([fox, sb, gate, ql, kvl, kr_slot, ff_slot], axis=-1).astype(BF16)


def kernel(x, c, positions, g_mix, w_ada, b_ada, w_in, b_fox_f, g_mla_q, w_mla_uq, g_mla_kv,
           w_mla_ukv, w_o_fox, w_o_mla, w_o_sb, w_out, g_ffn, w_ffn_gate, w_ffn_up, w_ffn_down,
           g_final):
    b, s, d = x.shape
    depth = w_in.shape[0]
    tm = min(s, 512)
    tm_ffn = min(s, 512)
    ta = min(s, 512)
    tsb = min(s, 256)

    mod = _ada(c, w_ada, b_ada).reshape(depth, b, 6, d)
    tables = _rope_tables(positions)
    for l in range(depth):
        w_in_r = _layout_w_in(w_in[l])
        wuq_r = _slot_cols(w_mla_uq[l], MLA_NOPE + MLA_ROPE, MLA_NOPE + MLA_ROPE).astype(BF16)
        wkn_r = _slot_cols(w_mla_ukv[l], 2 * HEAD_DIM, MLA_NOPE).astype(BF16)
        wkv_r = w_mla_ukv[l].reshape(MLA_KV_RANK, N_HEADS, 2 * HEAD_DIM)[:, :, MLA_NOPE:]
        wkv_r = wkv_r.reshape(MLA_KV_RANK, HEAD_W).astype(BF16)
        bff = jnp.pad(b_fox_f[l], (0, LANES - N_HEADS)).reshape(1, LANES)

        fox_q, fox_k, fox_vt, sb_qk, sb_vt, gates, mq, mk, mvt = _inproj(
            x, mod[l], g_mix[l].reshape(1, d), w_in_r, bff, g_mla_q[l].reshape(1, -1),
            g_mla_kv[l].reshape(1, -1), wuq_r, wkn_r, wkv_r, tables, tm=tm)
        y_fox = _softmax_attention_flat(fox_q, fox_k, fox_vt, t=ta, chunk_shift=0,
                                        name="fox_attention")
        y_mla = _softmax_attention_flat(mq, mk, mvt, t=ta, chunk_shift=CHUNK_SHIFT,
                                        name="mla_attention")
        y_sb = _sb_attention_flat(sb_qk, sb_vt, t=tsb)
        x = _merge_ffn(x, y_fox, y_mla, y_sb, gates, mod[l], w_o_fox[l].astype(BF16),
                       w_o_mla[l].astype(BF16), w_o_sb[l].astype(BF16), w_out[l].astype(BF16),
                       g_ffn[l].reshape(1, d), w_ffn_gate[l].astype(BF16),
                       w_ffn_up[l].astype(BF16), w_ffn_down[l].astype(BF16),
                       g_final.reshape(1, d), tm=tm_ffn, final=(l == depth - 1))
    return x
```

```python
import functools

import numpy as np
import jax
import jax.numpy as jnp
from jax import lax
from jax.experimental import pallas as pl
from jax.experimental.pallas import tpu as pltpu

F32 = jnp.float32
BF16 = jnp.bfloat16

D_MODEL = 1024
EPS = 1e-6
HEAD_DIM = 64
N_HEADS = 8
HEAD_W = N_HEADS * HEAD_DIM
CHUNK_SHIFT = 6
MLA_Q_RANK = 384
MLA_KV_RANK = 256
MLA_NOPE = 64
MLA_ROPE = 32
ROPE_HALF = MLA_ROPE // 2
ROPE_BASE = 10000.0
MLA_SCALE = (MLA_NOPE + MLA_ROPE) ** -0.5
QK_SCALE = HEAD_DIM ** -0.5
D_FF = 2816
LANES = 128
SLOT_W = N_HEADS * LANES
N_PIECES = 3

C_FQ = 0
C_FK = 512
C_FV = 1024
C_SB = 1536
C_GATE = 3072
C_QL = 6144
C_KVL = 6528
C_KR = 6784
C_FF = 6912
IN_W = 7040

ROW_CHUNK = 32
PIPELINE_UNROLL = 2
SIGN_BIT = np.uint32(0x80000000)
UNDERFLOW = 105.0
NORM_MARGIN = 1.02
NEG = -1e30
VMEM_LIMIT = 56 * 1024 * 1024


def _params(n_grid):
    return pltpu.CompilerParams(dimension_semantics=("arbitrary",) * n_grid,
                                vmem_limit_bytes=VMEM_LIMIT)


def _resident(shape):
    nd = len(shape)
    return pl.BlockSpec(shape, lambda *_: (0,) * nd, pipeline_mode=pl.Buffered(1))


def _bdot(a, b):
    return jnp.dot(a, b, preferred_element_type=F32)


def _dot_nt(a, b):
    return lax.dot_general(a, b, (((1,), (1,)), ((), ())), preferred_element_type=F32)


def _split3(v):
    hi = v.astype(BF16)
    r = v - hi.astype(F32)
    mid = r.astype(BF16)
    lo = (r - mid.astype(F32)).astype(BF16)
    return hi, mid, lo


def _log_sigmoid(z):
    return jnp.minimum(z, 0.0) - jnp.log1p(jnp.exp(-jnp.abs(z)))


def _sigmoid(z):
    return 1.0 / (1.0 + jnp.exp(-z))


def _rms(x):
    return x * lax.rsqrt(jnp.mean(x * x, axis=-1, keepdims=True) + EPS)


def _ada_kernel(c_ref, w_ref, b_ref, o_ref):
    c = c_ref[...]
    cond = c * _sigmoid(c)
    chi, cmid, clo = _split3(cond)
    whi, wmid, wlo = _split3(w_ref[0])
    acc = _bdot(chi, whi) + (_bdot(chi, wmid) + _bdot(cmid, whi))
    acc = acc + (_bdot(chi, wlo) + _bdot(clo, whi) + _bdot(cmid, wmid))
    o_ref[0] = acc + b_ref[0]


def _ada(c, w_ada, b_ada):
    depth, d, n = w_ada.shape
    b = c.shape[0]
    tn = 1024
    return pl.pallas_call(
        _ada_kernel,
        grid=(depth, n // tn),
        in_specs=[pl.BlockSpec((b, d), lambda l, j: (0, 0)),
                  pl.BlockSpec((1, d, tn), lambda l, j: (l, 0, j)),
                  pl.BlockSpec((1, 1, tn), lambda l, j: (l, 0, j))],
        out_specs=pl.BlockSpec((1, b, tn), lambda l, j: (l, 0, j)),
        out_shape=jax.ShapeDtypeStruct((depth, b, n), F32),
        compiler_params=_params(2),
        name="adaln",
    )(c, w_ada, b_ada.reshape(depth, 1, n))


def _rope_kernel(pos_ref, inv_ref, cos_ref, sina_ref, sinb_ref):
    ang = pos_ref[0].astype(F32) * inv_ref[...]
    lane = lax.broadcasted_iota(jnp.int32, (1, LANES), 1)
    sin = jnp.sin(ang)
    cos_ref[0] = jnp.cos(ang)
    lo, mid, hi = MLA_NOPE, MLA_NOPE + ROPE_HALF, MLA_NOPE + MLA_ROPE
    sina_ref[0] = jnp.where((lane >= lo) & (lane < mid), -sin, 0.0)
    sinb_ref[0] = jnp.where((lane >= mid) & (lane < hi), sin, 0.0)


def _rope_tables(positions):
    b, s = positions.shape
    ts = min(s, 512)
    inv = ROPE_BASE ** (-jnp.arange(ROPE_HALF, dtype=F32) / ROPE_HALF)
    inv_slot = jnp.concatenate([jnp.zeros((MLA_NOPE,), F32), inv, inv,
                                jnp.zeros((LANES - MLA_NOPE - MLA_ROPE,), F32)]).reshape(1, LANES)
    spec = pl.BlockSpec((1, ts, LANES), lambda i, j: (i, j, 0))
    return pl.pallas_call(
        _rope_kernel,
        grid=(b, s // ts),
        in_specs=[pl.BlockSpec((1, ts, 1), lambda i, j: (i, j, 0)),
                  pl.BlockSpec((1, LANES), lambda i, j: (0, 0))],
        out_specs=[spec, spec, spec],
        out_shape=[jax.ShapeDtypeStruct((b, s, LANES), F32)] * 3,
        compiler_params=_params(2),
        name="rope_tables",
    )(positions.reshape(b, s, 1), inv_slot)


def _rope_slot(x, cos, sina, sinb):
    return (x * cos + pltpu.roll(x, LANES - ROPE_HALF, 1) * sina
            + pltpu.roll(x, ROPE_HALF, 1) * sinb)


def _inproj_kernel(x_ref, mod_ref, g_ref, w_ref, bff_ref, gq_ref, gkv_ref, wuq_ref, wkn_ref,
                   wkv_ref, cos_ref, sina_ref, sinb_ref, place_ref, qc_ref, kc_ref,
                   fq_ref, fk_ref, fv_ref, sb_ref, sv_ref, gate_ref, mq_ref, mk_ref, mv_ref,
                   carry_ref, *, tm):
    x = x_ref[0]
    sh, sc = mod_ref[0, 0:1, :], mod_ref[0, 1:2, :]
    u = ((_rms(x) * g_ref[...]) * (1.0 + sc) + sh).astype(BF16)

    def seg(a, b):
        return _bdot(u, w_ref[:, a:b])

    @pl.when(pl.program_id(1) == 0)
    def _():
        carry_ref[...] = jnp.zeros_like(carry_ref)

    logf = _log_sigmoid(seg(C_FF, IN_W) + bff_ref[...])
    r = lax.broadcasted_iota(jnp.int32, (tm, tm), 0)
    c = lax.broadcasted_iota(jnp.int32, (tm, tm), 1)
    tri = jnp.where(r >= c, 1.0, 0.0).astype(BF16)
    lane = lax.broadcasted_iota(jnp.int32, (1, LANES), 1)

    def pack3(v):
        p0, p1, p2 = (p.astype(F32) for p in _split3(v))
        return jnp.where(lane < N_HEADS, p0,
                         jnp.where(lane < 2 * N_HEADS, pltpu.roll(p1, N_HEADS, 1),
                                   jnp.where(lane < 3 * N_HEADS, pltpu.roll(p2, 2 * N_HEADS, 1),
                                             0.0))).astype(BF16)

    part = _bdot(tri, pack3(logf))
    cum = ((part + pltpu.roll(part, LANES - N_HEADS, 1))
           + pltpu.roll(part, LANES - 2 * N_HEADS, 1)) + carry_ref[...]
    carry_ref[...] = jnp.where(lane < N_HEADS, cum[tm - 1:tm, :], 0.0)
    placed = _bdot(pack3(cum), place_ref[...])

    def to_slots(compact, extra, out_ref):
        for h in range(N_HEADS):
            x = compact[:, LANES * (h // 2):LANES * (h // 2 + 1)]
            head = pltpu.roll(x, HEAD_DIM, 1) if h % 2 else x
            slot = slice(LANES * h, LANES * (h + 1))
            out_ref[0, :, slot] = jnp.where(lane < HEAD_DIM, head, extra[:, slot]).astype(BF16)

    first_aug = (lax.broadcasted_iota(jnp.int32, (1, SLOT_W), 1) & (LANES - 1)) < HEAD_DIM + N_PIECES
    to_slots(seg(C_FQ, C_FK) * QK_SCALE, jnp.where(first_aug, qc_ref[...], placed), fq_ref)
    to_slots(seg(C_FK, C_FV), jnp.where(first_aug, placed, kc_ref[...]), fk_ref)
    fv_ref[0] = seg(C_FV, C_SB).astype(BF16).T

    sb_ref[0, :, 0:512] = (seg(C_SB, C_SB + 512) * QK_SCALE).astype(BF16)
    sb_ref[0, :, 512:1024] = seg(C_SB + 512, C_SB + 1024).astype(BF16)
    sv_ref[0] = seg(C_SB + 1024, C_GATE).astype(BF16).T
    for c in range(6):
        a = 512 * c
        gate_ref[0, :, a:a + 512] = _sigmoid(seg(C_GATE + a, C_GATE + a + 512)).astype(BF16)

    cos, sina, sinb = cos_ref[0], sina_ref[0], sinb_ref[0]
    cq = (_rms(seg(C_QL, C_KVL)) * gq_ref[...]).astype(BF16)
    ckv = (_rms(seg(C_KVL, C_KR)) * gkv_ref[...]).astype(BF16)
    kr = _rope_slot(seg(C_KR, C_FF), cos, sina, sinb)
    for h in range(N_HEADS):
        a = LANES * h
        q = _rope_slot(_bdot(cq, wuq_ref[:, a:a + LANES]), cos, sina, sinb)
        mq_ref[0, :, a:a + LANES] = (q * MLA_SCALE).astype(BF16)
        mk_ref[0, :, a:a + LANES] = (_bdot(ckv, wkn_ref[:, a:a + LANES]) + kr).astype(BF16)
    mv_ref[0] = _bdot(ckv, wkv_ref[...]).astype(BF16).T


def _fox_placement():
    place = np.zeros((LANES, SLOT_W), np.float32)
    qc = np.zeros((1, SLOT_W), np.float32)
    kc = np.zeros((1, SLOT_W), np.float32)
    for h in range(N_HEADS):
        for p in range(N_PIECES):
            place[N_HEADS * p + h, LANES * h + HEAD_DIM + p] = 1.0
            place[N_HEADS * p + h, LANES * h + HEAD_DIM + N_PIECES + p] = 1.0
            qc[0, LANES * h + HEAD_DIM + p] = -1.0
            kc[0, LANES * h + HEAD_DIM + N_PIECES + p] = 1.0
    return jnp.asarray(place, BF16), jnp.asarray(qc), jnp.asarray(kc)


def _inproj(x, mod, g_mix, w_in_r, bff, g_q, g_kv, wuq_r, wkn_r, wkv_r, tables, *, tm):
    b, s, d = x.shape
    cos, sina, sinb = tables
    place, qc, kc = _fox_placement()
    row = lambda w: pl.BlockSpec((1, tm, w), lambda i, j: (i, j, 0))
    col = lambda w: pl.BlockSpec((1, w, tm), lambda i, j: (i, 0, j))
    outs = [(SLOT_W, False),
            (SLOT_W, False),
            (HEAD_W, True),
            (2 * HEAD_W, False),
            (HEAD_W, True),
            (3 * D_MODEL, False),
            (SLOT_W, False),
            (SLOT_W, False),
            (HEAD_W, True)]
    out_specs = [col(w) if tr else row(w) for w, tr in outs]
    out_shape = [jax.ShapeDtypeStruct((b, w, s) if tr else (b, s, w), BF16) for w, tr in outs]
    return pl.pallas_call(
        functools.partial(_inproj_kernel, tm=tm),
        grid=(b, s // tm),
        in_specs=[row(d),
                  pl.BlockSpec((1, 6, d), lambda i, j: (i, 0, 0)),
                  _resident((1, d)), _resident((d, IN_W)), _resident((1, LANES)),
                  _resident((1, MLA_Q_RANK)), _resident((1, MLA_KV_RANK)),
                  _resident((MLA_Q_RANK, SLOT_W)), _resident((MLA_KV_RANK, SLOT_W)),
                  _resident((MLA_KV_RANK, HEAD_W)),
                  row(LANES), row(LANES), row(LANES),
                  _resident((LANES, SLOT_W)), _resident((1, SLOT_W)), _resident((1, SLOT_W))],
        out_specs=out_specs,
        out_shape=out_shape,
        scratch_shapes=[pltpu.VMEM((1, LANES), F32)],
        compiler_params=_params(2),
        name="inproj",
    )(x, mod, g_mix, w_in_r, bff, g_q, g_kv, wuq_r, wkn_r, wkv_r, cos, sina, sinb, place, qc, kc)


def _head_rows():
    row = lax.broadcasted_iota(jnp.int32, (LANES, 1), 0)
    return [(row >= HEAD_DIM * hh) & (row < HEAD_DIM * (hh + 1)) for hh in range(2)]


def _head_masks():
    lane = lax.broadcasted_iota(jnp.int32, (1, LANES), 1)
    return [(lane >= HEAD_DIM * hh) & (lane < HEAD_DIM * (hh + 1)) for hh in range(2)]


def _colmax(x, chunk=64):
    rows = x.shape[0]
    if rows > chunk and rows % chunk == 0:
        acc = x[:chunk]
        for a in range(chunk, rows, chunk):
            acc = jnp.maximum(acc, x[a:a + chunk])
        x, rows = acc, chunk
    while rows > 8 and rows % 16 == 0:
        rows //= 2
        x = jnp.maximum(x[:rows], x[rows:])
    return jnp.max(x, axis=0, keepdims=True)


def _pipeline(n, stages, unroll=PIPELINE_UNROLL):
    depth = len(stages)

    def static_step(t):
        for s, stage in enumerate(stages):
            if 0 <= t - s < n:
                stage(t - s, (t - s) % 2)

    first_full = depth - 1
    n_full = max(n - depth + 1, 0)
    for t in range(min(first_full, n + depth - 1)):
        static_step(t)
    n_loops = n_full // unroll
    if n_loops >= 1:
        def body(it, carry):
            t = first_full + unroll * it
            for u in range(unroll):
                for s, stage in enumerate(stages):
                    stage(t + u - s, (first_full + u - s) % 2)
            return carry

        lax.fori_loop(0, n_loops, body, 0)
    for t in range(first_full + unroll * n_loops, n + depth - 1):
        static_step(t)


def _softmax_block_table(n_tiles, t):
    diag = [(i * t, i * t) for i in range(n_tiles)]
    below = [(i * t, j * t) for i in range(1, n_tiles) for j in range(i)]
    return jnp.asarray(np.array(diag + below, np.int32).T), len(diag), len(below)


def _softmax_flat_kernel(tab_ref, q_ref, k_ref, vt_ref, o_ref, acc_ref, m_ref, al_ref, top_ref,
                         s0_ref, s1_ref, p0_ref, p1_ref, *, t, n_diag, n_below, chunk_shift):
    own_rows = _head_rows()
    s_bufs, p_bufs = (s0_ref, s1_ref), (p0_ref, p1_ref)
    kio = lax.broadcasted_iota(jnp.int32, (t, 1), 0)
    qio = lax.broadcasted_iota(jnp.int32, (1, t), 1)
    acc_ref[...] = jnp.zeros_like(acc_ref)
    m_ref[...] = jnp.full_like(m_ref, NEG)

    def stages(base, diag):
        def item(i):
            return (pl.multiple_of(tab_ref[0, base + i], t), pl.multiple_of(tab_ref[1, base + i], t))

        def scores(i, par):
            q0, k0 = item(i)
            for hh in range(2):
                slot = slice(LANES * hh, LANES * (hh + 1))
                s = _dot_nt(k_ref[0, pl.ds(k0, t), slot],
                            q_ref[0, pl.ds(q0, t), slot])
                if diag:
                    s = jnp.where((kio >> chunk_shift) <= (qio >> chunk_shift), s, NEG)
                s_bufs[par][hh] = s
                top_ref[par, hh] = _colmax(s)

        def numerators(i, par):
            q0, _ = item(i)
            for hh in range(2):
                m_old = m_ref[hh, :, pl.ds(q0, t)]
                m_new = jnp.maximum(m_old, top_ref[par, hh])
                for a in range(0, t, ROW_CHUNK):
                    rows = slice(a, a + ROW_CHUNK)
                    p_bufs[par][hh, rows, :] = jnp.exp(s_bufs[par][hh, rows, :] - m_new).astype(BF16)
                al_ref[par, hh] = jnp.exp(m_old - m_new)
                m_ref[hh, :, pl.ds(q0, t)] = m_new

        def values(i, par):
            q0, k0 = item(i)
            vt = vt_ref[0, :, pl.ds(k0, t)]
            for hh in range(2):
                v1 = jnp.where(own_rows[hh], vt, jnp.ones_like(vt))
                acc_ref[hh, :, pl.ds(q0, t)] = (acc_ref[hh, :, pl.ds(q0, t)] * al_ref[par, hh]
                                                + _bdot(v1, p_bufs[par][hh]))

        return [scores, numerators, values]

    _pipeline(n_diag, stages(0, True))
    _pipeline(n_below, stages(n_diag, False))

    sub = lax.broadcasted_iota(jnp.int32, (LANES, 1), 0)

    def finish(tile, carry):
        q0 = pl.multiple_of(tile * t, t)
        outs = []
        for hh in range(2):
            acc = acc_ref[hh, :, pl.ds(q0, t)]
            denom = acc[HEAD_DIM * (1 - hh):HEAD_DIM * (1 - hh) + 1, :]
            outs.append(acc / denom)
        o_ref[0, pl.ds(q0, t), :] = jnp.where(sub < HEAD_DIM, outs[0], outs[1]).T.astype(o_ref.dtype)
        return carry

    lax.fori_loop(0, n_diag, finish, 0)


def _softmax_attention_flat(q, k, vt, *, t, chunk_shift, name):
    b, s, _ = k.shape
    table, n_diag, n_below = _softmax_block_table(s // t, t)
    grid_spec = pltpu.PrefetchScalarGridSpec(
        num_scalar_prefetch=1,
        grid=(b, N_HEADS // 2),
        in_specs=[pl.BlockSpec((1, s, 2 * LANES), lambda i, h, tab: (i, 0, h)),
                  pl.BlockSpec((1, s, 2 * LANES), lambda i, h, tab: (i, 0, h)),
                  pl.BlockSpec((1, LANES, s), lambda i, h, tab: (i, h, 0))],
        out_specs=pl.BlockSpec((1, s, LANES), lambda i, h, tab: (i, 0, h)),
        scratch_shapes=[pltpu.VMEM((2, LANES, s), F32),
                        pltpu.VMEM((2, 1, s), F32),
                        pltpu.VMEM((2, 2, 1, t), F32),
                        pltpu.VMEM((2, 2, 1, t), F32),
                        pltpu.VMEM((2, t, t), F32), pltpu.VMEM((2, t, t), F32),
                        pltpu.VMEM((2, t, t), BF16), pltpu.VMEM((2, t, t), BF16)])
    return pl.pallas_call(
        functools.partial(_softmax_flat_kernel, t=t, n_diag=n_diag, n_below=n_below,
                          chunk_shift=chunk_shift),
        grid_spec=grid_spec,
        out_shape=jax.ShapeDtypeStruct((b, s, HEAD_W), BF16),
        compiler_params=_params(2),
        name=name,
    )(table, q, k, vt)


def _sb_block_table(n_tiles, t):
    diag = [(i * t, i * t, i * t, i) for i in range(n_tiles)]
    below = [(i * t, (i - d) * t, i * t, i) for d in range(1, n_tiles) for i in range(d, n_tiles)]
    idle = [(0, 0, n_tiles * t, n_tiles)]
    return jnp.asarray(np.array(diag + below + idle, np.int32).T), len(diag), len(below)


def _sb_flat_kernel(tab_ref, q_ref, k_ref, vt_ref, o_ref, acc_ref, r_ref, cut_ref, tot_ref, d_ref,
                    done_ref, z0_ref, z1_ref, e0_ref, e1_ref, w0_ref, w1_ref, *, t, n_diag,
                    n_below):
    in_head = _head_masks()
    own_rows = _head_rows()
    z_bufs, e_bufs, w_bufs = (z0_ref, z1_ref), (e0_ref, e1_ref), (w0_ref, w1_ref)
    kio = lax.broadcasted_iota(jnp.int32, (t, 1), 0)
    qio = lax.broadcasted_iota(jnp.int32, (1, t), 1)
    ss = lax.broadcasted_iota(jnp.int32, (t, t), 0)
    jj = lax.broadcasted_iota(jnp.int32, (t, t), 1)
    from_here = jnp.where(jj >= ss, 1.0, 0.0).astype(BF16)
    acc_ref[...] = jnp.zeros_like(acc_ref)
    r_ref[...] = jnp.zeros_like(r_ref)
    cut_ref[...] = jnp.zeros_like(cut_ref)
    for tile in range(n_diag + 1):
        done_ref[tile] = 0

    s_len = n_diag * t
    sub8 = lax.broadcasted_iota(jnp.int32, (8, 1), 0)
    head_rows = jnp.where(sub8 == 0, jnp.where(in_head[0], 1.0, 0.0),
                          jnp.where(sub8 == 1, jnp.where(in_head[1], 1.0, 0.0), 0.0)).astype(BF16)

    def sq_norms(ref):
        x = ref[0].astype(F32)
        return _dot_nt(head_rows, (x * x).astype(BF16))

    k_max = jnp.max(sq_norms(k_ref), axis=1, keepdims=True)
    cut = NORM_MARGIN * jnp.sqrt(sq_norms(q_ref) * k_max) + UNDERFLOW
    for hh in range(2):
        cut_ref[hh, :, 0:s_len] = cut[hh:hh + 1, :]

    def stages(base, diag):
        def item(i):
            return (pl.multiple_of(tab_ref[0, base + i], t), pl.multiple_of(tab_ref[1, base + i], t),
                    pl.multiple_of(tab_ref[2, base + i], t), tab_ref[3, base + i])

        def scores(i, par):
            q0, k0, _, _ = item(i)
            k = k_ref[0, pl.ds(k0, t), :]
            q = q_ref[0, pl.ds(q0, t), :]
            for hh in range(2):
                qh = jnp.where(in_head[hh], q, jnp.zeros_like(q))
                z_bufs[par][hh] = _dot_nt(k, qh)

        def drops(i, par):
            for hh in range(2):
                for a in range(0, t, ROW_CHUNK):
                    rows = slice(a, a + ROW_CHUNK)
                    z = z_bufs[par][hh, rows, :]
                    neg_abs = pltpu.bitcast(pltpu.bitcast(z, jnp.uint32) | SIGN_BIT, F32)
                    drop = jnp.maximum(z, 0.0) + jnp.log(1.0 + jnp.exp(neg_abs))
                    if diag:
                        drop = jnp.where(kio[rows] < qio, drop, 0.0)
                    d_ref[hh, rows, :] = drop.astype(BF16)
                incl = _bdot(from_here, d_ref[hh])
                for a in range(0, t, ROW_CHUNK):
                    rows = slice(a, a + ROW_CHUNK)
                    e = z_bufs[par][hh, rows, :] - incl[rows]
                    if diag:
                        e = jnp.where(kio[rows] < qio, e, NEG)
                    e_bufs[par][hh, rows, :] = e
                tot_ref[par, hh] = incl[0:1, :]

        def weights(i, par):
            _, _, q0, tile = item(i)
            slack = None
            for hh in range(2):
                r_old = r_ref[hh, :, pl.ds(q0, t)]
                for a in range(0, t, ROW_CHUNK):
                    rows = slice(a, a + ROW_CHUNK)
                    w_bufs[par][hh, rows, :] = jnp.exp(e_bufs[par][hh, rows, :] - r_old).astype(BF16)
                r_new = r_old + tot_ref[par, hh]
                r_ref[hh, :, pl.ds(q0, t)] = r_new
                over = r_new - cut_ref[hh, :, pl.ds(q0, t)]
                slack = over if slack is None else jnp.minimum(slack, over)
            if not diag:
                done_ref[tile] = (jnp.min(slack) > 0.0).astype(jnp.int32)

        def values(i, par):
            _, k0, q0, _ = item(i)
            vt = vt_ref[0, :, pl.ds(k0, t)]
            zero = jnp.zeros_like(vt)
            lhs = jnp.concatenate([jnp.where(own_rows[hh], vt, zero) for hh in range(2)], axis=1)
            acc_ref[:, pl.ds(q0, t)] = (acc_ref[:, pl.ds(q0, t)]
                                        + _bdot(lhs, w_bufs[par][...].reshape(2 * t, t)))

        return [scores, drops, weights, values]

    _pipeline(n_diag, stages(0, True))

    scores, drops, weights, values = stages(n_diag, False)

    def next_live(c):
        return lax.while_loop(
            lambda c: (c < n_below) & (done_ref[tab_ref[3, n_diag + c]] != 0), lambda c: c + 1, c)

    def after(c):
        return next_live(jnp.minimum(c + 1, n_below))

    def two_steps(state):
        a, b, i1, i2, i3 = state
        scores(a, 0)
        drops(i1, 1)
        weights(i2, 0)
        values(i3, 1)
        scores(b, 1)
        drops(a, 0)
        weights(i1, 1)
        values(i2, 0)
        a_next = after(b)
        return a_next, after(a_next), b, a, i1

    idle = jnp.int32(n_below)
    first = next_live(jnp.int32(0))
    lax.while_loop(lambda st: (st[0] < n_below) | (st[2] < n_below) | (st[3] < n_below)
                   | (st[4] < n_below), two_steps, (first, after(first), idle, idle, idle))

    def finish(tile, carry):
        q0 = pl.multiple_of(tile * t, t)
        o_ref[0, pl.ds(q0, t), :] = acc_ref[:, pl.ds(q0, t)].T.astype(o_ref.dtype)
        return carry

    lax.fori_loop(0, n_diag, finish, 0)


def _sb_attention_flat(qk, vt, *, t):
    b, s, _ = qk.shape
    table, n_diag, n_below = _sb_block_table(s // t, t)
    grid_spec = pltpu.PrefetchScalarGridSpec(
        num_scalar_prefetch=1,
        grid=(b, N_HEADS // 2),
        in_specs=[pl.BlockSpec((1, s, LANES), lambda i, h, tab: (i, 0, h)),
                  pl.BlockSpec((1, s, LANES), lambda i, h, tab: (i, 0, 4 + h)),
                  pl.BlockSpec((1, LANES, s), lambda i, h, tab: (i, h, 0))],
        out_specs=pl.BlockSpec((1, s, LANES), lambda i, h, tab: (i, 0, h)),
        scratch_shapes=[pltpu.VMEM((LANES, s + t), F32),
                        pltpu.VMEM((2, 1, s + t), F32),
                        pltpu.VMEM((2, 1, s + t), F32),
                        pltpu.VMEM((2, 2, 1, t), F32),
                        pltpu.VMEM((2, t, t), BF16),
                        pltpu.SMEM((s // t + 1,), jnp.int32),
                        pltpu.VMEM((2, t, t), F32), pltpu.VMEM((2, t, t), F32),
                        pltpu.VMEM((2, t, t), F32), pltpu.VMEM((2, t, t), F32),
                        pltpu.VMEM((2, t, t), BF16), pltpu.VMEM((2, t, t), BF16)])
    return pl.pallas_call(
        functools.partial(_sb_flat_kernel, t=t, n_diag=n_diag, n_below=n_below),
        grid_spec=grid_spec,
        out_shape=jax.ShapeDtypeStruct((b, s, HEAD_W), BF16),
        compiler_params=_params(2),
        name="sb_attention",
    )(table, qk, qk, vt)


def _merge_ffn_kernel(x_ref, yf_ref, ym_ref, ys_ref, gate_ref, mod_ref, wf_ref, wm_ref, ws_ref,
                      wo_ref, g_ref, wg_ref, wu_ref, wd_ref, gfin_ref, o_ref, *, final, fc):
    d = D_MODEL
    merged = (gate_ref[0, :, 0:d].astype(F32) * _bdot(yf_ref[0], wf_ref[...])
              + gate_ref[0, :, d:2 * d].astype(F32) * _bdot(ym_ref[0], wm_ref[...])
              + gate_ref[0, :, 2 * d:3 * d].astype(F32) * _bdot(ys_ref[0], ws_ref[...]))
    x = x_ref[0] + mod_ref[0, 2:3, :] * _bdot(merged.astype(BF16), wo_ref[...])

    sh, sc, gt = mod_ref[0, 3:4, :], mod_ref[0, 4:5, :], mod_ref[0, 5:6, :]
    u = ((_rms(x) * g_ref[...]) * (1.0 + sc) + sh).astype(BF16)
    acc = None
    for a in range(0, D_FF, fc):
        hg = _bdot(u, wg_ref[:, a:a + fc])
        hu = _bdot(u, wu_ref[:, a:a + fc])
        h = ((hg * _sigmoid(hg)) * hu).astype(BF16)
        part = _bdot(h, wd_ref[a:a + fc, :])
        acc = part if acc is None else acc + part
    out = x + gt * acc
    if final:
        out = _rms(out) * gfin_ref[...]
    o_ref[0] = out


def _merge_ffn(x, y_fox, y_mla, y_sb, gates, mod, wf, wm, ws, wo, g_ffn, wg, wu, wd, g_final, *,
               tm, final):
    b, s, d = x.shape
    row = lambda w: pl.BlockSpec((1, tm, w), lambda i, j: (i, j, 0))
    return pl.pallas_call(
        functools.partial(_merge_ffn_kernel, final=final, fc=D_FF // 2),
        grid=(b, s // tm),
        in_specs=[row(d), row(HEAD_W), row(HEAD_W), row(HEAD_W), row(3 * d),
                  pl.BlockSpec((1, 6, d), lambda i, j: (i, 0, 0)),
                  _resident((HEAD_W, d)), _resident((HEAD_W, d)), _resident((HEAD_W, d)),
                  _resident((d, d)), _resident((1, d)),
                  _resident((d, D_FF)), _resident((d, D_FF)), _resident((D_FF, d)),
                  _resident((1, d))],
        out_specs=row(d),
        out_shape=jax.ShapeDtypeStruct((b, s, d), F32),
        compiler_params=_params(2),
        name="merge_ffn",
    )(x, y_fox, y_mla, y_sb, gates, mod, wf, wm, ws, wo, g_ffn, wg, wu, wd, g_final)


def _slot_cols(w, width_per_head, used):
    k = w.shape[0]
    w = w.reshape(k, N_HEADS, width_per_head)[:, :, :used]
    return jnp.pad(w, ((0, 0), (0, 0), (0, LANES - used))).reshape(k, SLOT_W)


def _layout_w_in(w):
    d = w.shape[0]
    fox, ff, ql, kvl, kr, sb, gate = jnp.split(
        w, [1536, 1544, 1928, 2184, 2216, 3752], axis=-1)
    z = lambda n: jnp.zeros((d, n), w.dtype)
    kr_slot = jnp.concatenate([z(MLA_NOPE), kr, z(LANES - MLA_NOPE - MLA_ROPE)], axis=-1)
    ff_slot = jnp.concatenate([ff, z(LANES - N_HEADS)], axis=-1)
    return jnp.concatenate([fox, sb, gate, ql, kvl, kr_slot, ff_slot], axis=-1).astype(BF16)


def kernel(x, c, positions, g_mix, w_ada, b_ada, w_in, b_fox_f, g_mla_q, w_mla_uq, g_mla_kv,
           w_mla_ukv, w_o_fox, w_o_mla, w_o_sb, w_out, g_ffn, w_ffn_gate, w_ffn_up, w_ffn_down,
           g_final):
    b, s, d = x.shape
    depth = w_in.shape[0]
    tm = min(s, 512)
    tm_ffn = min(s, 512)
    ta = min(s, 512)
    tsb = min(s, 256)

    mod = _ada(c, w_ada, b_ada).reshape(depth, b, 6, d)
    tables = _rope_tables(positions)
    for l in range(depth):
        w_in_r = _layout_w_in(w_in[l])
        wuq_r = _slot_cols(w_mla_uq[l], MLA_NOPE + MLA_ROPE, MLA_NOPE + MLA_ROPE).astype(BF16)
        wkn_r = _slot_cols(w_mla_ukv[l], 2 * HEAD_DIM, MLA_NOPE).astype(BF16)
        wkv_r = w_mla_ukv[l].reshape(MLA_KV_RANK, N_HEADS, 2 * HEAD_DIM)[:, :, MLA_NOPE:]
        wkv_r = wkv_r.reshape(MLA_KV_RANK, HEAD_W).astype(BF16)
        bff = jnp.pad(b_fox_f[l], (0, LANES - N_HEADS)).reshape(1, LANES)

        fox_q, fox_k, fox_vt, sb_qk, sb_vt, gates, mq, mk, mvt = _inproj(
            x, mod[l], g_mix[l].reshape(1, d), w_in_r, bff, g_mla_q[l].reshape(1, -1),
            g_mla_kv[l].reshape(1, -1), wuq_r, wkn_r, wkv_r, tables, tm=tm)
        y_fox = _softmax_attention_flat(fox_q, fox_k, fox_vt, t=ta, chunk_shift=0,
                                        name="fox_attention")
        y_mla = _softmax_attention_flat(mq, mk, mvt, t=ta, chunk_shift=CHUNK_SHIFT,
                                        name="mla_attention")
        y_sb = _sb_attention_flat(sb_qk, sb_vt, t=tsb)
        x = _merge_ffn(x, y_fox, y_mla, y_sb, gates, mod[l], w_o_fox[l].astype(BF16),
                       w_o_mla[l].astype(BF16), w_o_sb[l].astype(BF16), w_out[l].astype(BF16),
                       g_ffn[l].reshape(1, d), w_ffn_gate[l].astype(BF16),
                       w_ffn_up[l].astype(BF16), w_ffn_down[l].astype(BF16),
                       g_final.reshape(1, d), tm=tm_ffn, final=(l == depth - 1))
    return x
```

```python
import functools

import numpy as np
import jax
import jax.numpy as jnp
from jax import lax
from jax.experimental import pallas as pl
from jax.experimental.pallas import tpu as pltpu

F32 = jnp.float32
BF16 = jnp.bfloat16

D_MODEL = 1024
EPS = 1e-6
HEAD_DIM = 64
N_HEADS = 8
HEAD_W = N_HEADS * HEAD_DIM
CHUNK_SHIFT = 6
MLA_Q_RANK = 384
MLA_KV_RANK = 256
MLA_NOPE = 64
MLA_ROPE = 32
ROPE_HALF = MLA_ROPE // 2
ROPE_BASE = 10000.0
MLA_SCALE = (MLA_NOPE + MLA_ROPE) ** -0.5
QK_SCALE = HEAD_DIM ** -0.5
D_FF = 2816
LANES = 128
SLOT_W = N_HEADS * LANES
N_PIECES = 3

C_FQ = 0
C_FK = 512
C_FV = 1024
C_SB = 1536
C_GATE = 3072
C_QL = 6144
C_KVL = 6528
C_KR = 6784
C_FF = 6912
IN_W = 7040

ROW_CHUNK = 32
PIPELINE_UNROLL = 2
SIGN_BIT = np.uint32(0x80000000)
UNDERFLOW = 105.0
NORM_MARGIN = 1.02
NEG = -1e30
VMEM_LIMIT = 56 * 1024 * 1024


def _params(n_grid):
    return pltpu.CompilerParams(dimension_semantics=("arbitrary",) * n_grid,
                                vmem_limit_bytes=VMEM_LIMIT)


def _resident(shape):
    nd = len(shape)
    return pl.BlockSpec(shape, lambda *_: (0,) * nd, pipeline_mode=pl.Buffered(1))


def _resident_layer(shape, layer):
    nd = len(shape)
    return pl.BlockSpec((pl.Squeezed(),) + tuple(shape), lambda *_: (layer,) + (0,) * nd,
                        pipeline_mode=pl.Buffered(1))


def _bdot(a, b):
    return jnp.dot(a, b, preferred_element_type=F32)


def _dot_nt(a, b):
    return lax.dot_general(a, b, (((1,), (1,)), ((), ())), preferred_element_type=F32)


def _split3(v):
    hi = v.astype(BF16)
    r = v - hi.astype(F32)
    mid = r.astype(BF16)
    lo = (r - mid.astype(F32)).astype(BF16)
    return hi, mid, lo


def _log_sigmoid(z):
    return jnp.minimum(z, 0.0) - jnp.log1p(jnp.exp(-jnp.abs(z)))


def _sigmoid(z):
    return 1.0 / (1.0 + jnp.exp(-z))


def _rms(x):
    return x * lax.rsqrt(jnp.mean(x * x, axis=-1, keepdims=True) + EPS)


def _ada_kernel(c_ref, w_ref, b_ref, o_ref):
    c = c_ref[...]
    cond = c * _sigmoid(c)
    chi, cmid, clo = _split3(cond)
    whi, wmid, wlo = _split3(w_ref[0])
    acc = _bdot(chi, whi) + (_bdot(chi, wmid) + _bdot(cmid, whi))
    acc = acc + (_bdot(chi, wlo) + _bdot(clo, whi) + _bdot(cmid, wmid))
    o_ref[0] = acc + b_ref[0]


def _ada(c, w_ada, b_ada):
    depth, d, n = w_ada.shape
    b = c.shape[0]
    tn = 1024
    return pl.pallas_call(
        _ada_kernel,
        grid=(depth, n // tn),
        in_specs=[pl.BlockSpec((b, d), lambda l, j: (0, 0)),
                  pl.BlockSpec((1, d, tn), lambda l, j: (l, 0, j)),
                  pl.BlockSpec((1, 1, tn), lambda l, j: (l, 0, j))],
        out_specs=pl.BlockSpec((1, b, tn), lambda l, j: (l, 0, j)),
        out_shape=jax.ShapeDtypeStruct((depth, b, n), F32),
        compiler_params=_params(2),
        name="adaln",
    )(c, w_ada, b_ada.reshape(depth, 1, n))


def _rope_kernel(pos_ref, inv_ref, cos_ref, sina_ref, sinb_ref):
    ang = pos_ref[0].astype(F32) * inv_ref[...]
    lane = lax.broadcasted_iota(jnp.int32, (1, LANES), 1)
    sin = jnp.sin(ang)
    cos_ref[0] = jnp.cos(ang)
    lo, mid, hi = MLA_NOPE, MLA_NOPE + ROPE_HALF, MLA_NOPE + MLA_ROPE
    sina_ref[0] = jnp.where((lane >= lo) & (lane < mid), -sin, 0.0)
    sinb_ref[0] = jnp.where((lane >= mid) & (lane < hi), sin, 0.0)


def _rope_tables(positions):
    b, s = positions.shape
    ts = min(s, 512)
    inv = ROPE_BASE ** (-jnp.arange(ROPE_HALF, dtype=F32) / ROPE_HALF)
    inv_slot = jnp.concatenate([jnp.zeros((MLA_NOPE,), F32), inv, inv,
                                jnp.zeros((LANES - MLA_NOPE - MLA_ROPE,), F32)]).reshape(1, LANES)
    spec = pl.BlockSpec((1, ts, LANES), lambda i, j: (i, j, 0))
    return pl.pallas_call(
        _rope_kernel,
        grid=(b, s // ts),
        in_specs=[pl.BlockSpec((1, ts, 1), lambda i, j: (i, j, 0)),
                  pl.BlockSpec((1, LANES), lambda i, j: (0, 0))],
        out_specs=[spec, spec, spec],
        out_shape=[jax.ShapeDtypeStruct((b, s, LANES), F32)] * 3,
        compiler_params=_params(2),
        name="rope_tables",
    )(positions.reshape(b, s, 1), inv_slot)


def _rope_slot(x, cos, sina, sinb):
    return (x * cos + pltpu.roll(x, LANES - ROPE_HALF, 1) * sina
            + pltpu.roll(x, ROPE_HALF, 1) * sinb)


def _inproj_kernel(x_ref, mod_ref, g_ref, w_ref, bff_ref, gq_ref, gkv_ref, wuq_ref, wkn_ref,
                   wkv_ref, cos_ref, sina_ref, sinb_ref, place_ref, qc_ref, kc_ref,
                   fq_ref, fk_ref, fv_ref, sb_ref, sv_ref, gate_ref, mq_ref, mk_ref, mv_ref,
                   carry_ref, *, tm):
    x = x_ref[0]
    sh, sc = mod_ref[0, 0:1, :], mod_ref[0, 1:2, :]
    u = ((_rms(x) * g_ref[...]) * (1.0 + sc) + sh).astype(BF16)

    def seg(a, b):
        return _bdot(u, w_ref[:, a:b])

    @pl.when(pl.program_id(1) == 0)
    def _():
        carry_ref[...] = jnp.zeros_like(carry_ref)

    logf = _log_sigmoid(seg(C_FF, IN_W) + bff_ref[...])
    r = lax.broadcasted_iota(jnp.int32, (tm, tm), 0)
    c = lax.broadcasted_iota(jnp.int32, (tm, tm), 1)
    tri = jnp.where(r >= c, 1.0, 0.0).astype(BF16)
    lane = lax.broadcasted_iota(jnp.int32, (1, LANES), 1)

    def pack3(v):
        p0, p1, p2 = (p.astype(F32) for p in _split3(v))
        return jnp.where(lane < N_HEADS, p0,
                         jnp.where(lane < 2 * N_HEADS, pltpu.roll(p1, N_HEADS, 1),
                                   jnp.where(lane < 3 * N_HEADS, pltpu.roll(p2, 2 * N_HEADS, 1),
                                             0.0))).astype(BF16)

    part = _bdot(tri, pack3(logf))
    cum = ((part + pltpu.roll(part, LANES - N_HEADS, 1))
           + pltpu.roll(part, LANES - 2 * N_HEADS, 1)) + carry_ref[...]
    carry_ref[...] = jnp.where(lane < N_HEADS, cum[tm - 1:tm, :], 0.0)
    placed = _bdot(pack3(cum), place_ref[...])

    def to_slots(compact, extra, out_ref):
        for h in range(N_HEADS):
            x = compact[:, LANES * (h // 2):LANES * (h // 2 + 1)]
            head = pltpu.roll(x, HEAD_DIM, 1) if h % 2 else x
            slot = slice(LANES * h, LANES * (h + 1))
            out_ref[0, :, slot] = jnp.where(lane < HEAD_DIM, head, extra[:, slot]).astype(BF16)

    first_aug = (lax.broadcasted_iota(jnp.int32, (1, SLOT_W), 1) & (LANES - 1)) < HEAD_DIM + N_PIECES
    to_slots(seg(C_FQ, C_FK) * QK_SCALE, jnp.where(first_aug, qc_ref[...], placed), fq_ref)
    to_slots(seg(C_FK, C_FV), jnp.where(first_aug, placed, kc_ref[...]), fk_ref)
    fv_ref[0] = seg(C_FV, C_SB).astype(BF16).T

    sb_ref[0, :, 0:512] = (seg(C_SB, C_SB + 512) * QK_SCALE).astype(BF16)
    sb_ref[0, :, 512:1024] = seg(C_SB + 512, C_SB + 1024).astype(BF16)
    sv_ref[0] = seg(C_SB + 1024, C_GATE).astype(BF16).T
    for c in range(6):
        a = 512 * c
        gate_ref[0, :, a:a + 512] = _sigmoid(seg(C_GATE + a, C_GATE + a + 512)).astype(BF16)

    cos, sina, sinb = cos_ref[0], sina_ref[0], sinb_ref[0]
    cq = (_rms(seg(C_QL, C_KVL)) * gq_ref[...]).astype(BF16)
    ckv = (_rms(seg(C_KVL, C_KR)) * gkv_ref[...]).astype(BF16)
    kr = _rope_slot(seg(C_KR, C_FF), cos, sina, sinb)
    for h in range(N_HEADS):
        a = LANES * h
        q = _rope_slot(_bdot(cq, wuq_ref[:, a:a + LANES]), cos, sina, sinb)
        mq_ref[0, :, a:a + LANES] = (q * MLA_SCALE).astype(BF16)
        mk_ref[0, :, a:a + LANES] = (_bdot(ckv, wkn_ref[:, a:a + LANES]) + kr).astype(BF16)
    mv_ref[0] = _bdot(ckv, wkv_ref[...]).astype(BF16).T


def _fox_placement():
    place = np.zeros((LANES, SLOT_W), np.float32)
    qc = np.zeros((1, SLOT_W), np.float32)
    kc = np.zeros((1, SLOT_W), np.float32)
    for h in range(N_HEADS):
        for p in range(N_PIECES):
            place[N_HEADS * p + h, LANES * h + HEAD_DIM + p] = 1.0
            place[N_HEADS * p + h, LANES * h + HEAD_DIM + N_PIECES + p] = 1.0
            qc[0, LANES * h + HEAD_DIM + p] = -1.0
            kc[0, LANES * h + HEAD_DIM + N_PIECES + p] = 1.0
    return jnp.asarray(place, BF16), jnp.asarray(qc), jnp.asarray(kc)


def _inproj(x, mod, weights, tables, *, layer, tm):
    b, s, d = x.shape
    cos, sina, sinb = tables
    place, qc, kc = _fox_placement()
    per_layer = lambda *shape: _resident_layer(shape, layer)
    row = lambda w: pl.BlockSpec((1, tm, w), lambda i, j: (i, j, 0))
    col = lambda w: pl.BlockSpec((1, w, tm), lambda i, j: (i, 0, j))
    outs = [(SLOT_W, False),
            (SLOT_W, False),
            (HEAD_W, True),
            (2 * HEAD_W, False),
            (HEAD_W, True),
            (3 * D_MODEL, False),
            (SLOT_W, False),
            (SLOT_W, False),
            (HEAD_W, True)]
    out_specs = [col(w) if tr else row(w) for w, tr in outs]
    out_shape = [jax.ShapeDtypeStruct((b, w, s) if tr else (b, s, w), BF16) for w, tr in outs]
    return pl.pallas_call(
        functools.partial(_inproj_kernel, tm=tm),
        grid=(b, s // tm),
        in_specs=[row(d),
                  pl.BlockSpec((1, 6, d), lambda i, j: (i, 0, 0)),
                  per_layer(1, d), per_layer(d, IN_W), per_layer(1, LANES),
                  per_layer(1, MLA_Q_RANK), per_layer(1, MLA_KV_RANK),
                  per_layer(MLA_Q_RANK, SLOT_W), per_layer(MLA_KV_RANK, SLOT_W),
                  per_layer(MLA_KV_RANK, HEAD_W),
                  row(LANES), row(LANES), row(LANES),
                  _resident((LANES, SLOT_W)), _resident((1, SLOT_W)), _resident((1, SLOT_W))],
        out_specs=out_specs,
        out_shape=out_shape,
        scratch_shapes=[pltpu.VMEM((1, LANES), F32)],
        compiler_params=_params(2),
        name="inproj",
    )(x, mod, *weights, cos, sina, sinb, place, qc, kc)


def _head_rows():
    row = lax.broadcasted_iota(jnp.int32, (LANES, 1), 0)
    return [(row >= HEAD_DIM * hh) & (row < HEAD_DIM * (hh + 1)) for hh in range(2)]


def _head_masks():
    lane = lax.broadcasted_iota(jnp.int32, (1, LANES), 1)
    return [(lane >= HEAD_DIM * hh) & (lane < HEAD_DIM * (hh + 1)) for hh in range(2)]


def _colmax(x, chunk=64):
    rows = x.shape[0]
    if rows > chunk and rows % chunk == 0:
        acc = x[:chunk]
        for a in range(chunk, rows, chunk):
            acc = jnp.maximum(acc, x[a:a + chunk])
        x, rows = acc, chunk
    while rows > 8 and rows % 16 == 0:
        rows //= 2
        x = jnp.maximum(x[:rows], x[rows:])
    return jnp.max(x, axis=0, keepdims=True)


def _pipeline(n, stages, unroll=PIPELINE_UNROLL):
    depth = len(stages)

    def static_step(t):
        for s, stage in enumerate(stages):
            if 0 <= t - s < n:
                stage(t - s, (t - s) % 2)

    first_full = depth - 1
    n_full = max(n - depth + 1, 0)
    for t in range(min(first_full, n + depth - 1)):
        static_step(t)
    n_loops = n_full // unroll
    if n_loops >= 1:
        def body(it, carry):
            t = first_full + unroll * it
            for u in range(unroll):
                for s, stage in enumerate(stages):
                    stage(t + u - s, (first_full + u - s) % 2)
            return carry

        lax.fori_loop(0, n_loops, body, 0)
    for t in range(first_full + unroll * n_loops, n + depth - 1):
        static_step(t)


def _softmax_block_table(n_tiles, t):
    diag = [(i * t, i * t) for i in range(n_tiles)]
    below = [(i * t, j * t) for i in range(1, n_tiles) for j in range(i)]
    return jnp.asarray(np.array(diag + below, np.int32).T), len(diag), len(below)


def _softmax_flat_kernel(tab_ref, q_ref, k_ref, vt_ref, o_ref, acc_ref, m_ref, al_ref, top_ref,
                         s0_ref, s1_ref, p0_ref, p1_ref, *, t, n_diag, n_below, chunk_shift):
    own_rows = _head_rows()
    s_bufs, p_bufs = (s0_ref, s1_ref), (p0_ref, p1_ref)
    kio = lax.broadcasted_iota(jnp.int32, (t, 1), 0)
    qio = lax.broadcasted_iota(jnp.int32, (1, t), 1)
    acc_ref[...] = jnp.zeros_like(acc_ref)
    m_ref[...] = jnp.full_like(m_ref, NEG)

    def stages(base, diag):
        def item(i):
            return (pl.multiple_of(tab_ref[0, base + i], t), pl.multiple_of(tab_ref[1, base + i], t))

        def scores(i, par):
            q0, k0 = item(i)
            for hh in range(2):
                slot = slice(LANES * hh, LANES * (hh + 1))
                s = _dot_nt(k_ref[0, pl.ds(k0, t), slot],
                            q_ref[0, pl.ds(q0, t), slot])
                if diag:
                    s = jnp.where((kio >> chunk_shift) <= (qio >> chunk_shift), s, NEG)
                s_bufs[par][hh] = s
                top_ref[par, hh] = _colmax(s)

        def numerators(i, par):
            q0, _ = item(i)
            for hh in range(2):
                m_old = m_ref[hh, :, pl.ds(q0, t)]
                m_new = jnp.maximum(m_old, top_ref[par, hh])
                for a in range(0, t, ROW_CHUNK):
                    rows = slice(a, a + ROW_CHUNK)
                    p_bufs[par][hh, rows, :] = jnp.exp(s_bufs[par][hh, rows, :] - m_new).astype(BF16)
                al_ref[par, hh] = jnp.exp(m_old - m_new)
                m_ref[hh, :, pl.ds(q0, t)] = m_new

        def values(i, par):
            q0, k0 = item(i)
            vt = vt_ref[0, :, pl.ds(k0, t)]
            for hh in range(2):
                v1 = jnp.where(own_rows[hh], vt, jnp.ones_like(vt))
                acc_ref[hh, :, pl.ds(q0, t)] = (acc_ref[hh, :, pl.ds(q0, t)] * al_ref[par, hh]
                                                + _bdot(v1, p_bufs[par][hh]))

        return [scores, numerators, values]

    _pipeline(n_diag, stages(0, True))
    _pipeline(n_below, stages(n_diag, False))

    sub = lax.broadcasted_iota(jnp.int32, (LANES, 1), 0)

    def finish(tile, carry):
        q0 = pl.multiple_of(tile * t, t)
        outs = []
        for hh in range(2):
            acc = acc_ref[hh, :, pl.ds(q0, t)]
            denom = acc[HEAD_DIM * (1 - hh):HEAD_DIM * (1 - hh) + 1, :]
            outs.append(acc / denom)
        o_ref[0, pl.ds(q0, t), :] = jnp.where(sub < HEAD_DIM, outs[0], outs[1]).T.astype(o_ref.dtype)
        return carry

    lax.fori_loop(0, n_diag, finish, 0)


def _softmax_attention_flat(q, k, vt, *, t, chunk_shift, name):
    b, s, _ = k.shape
    table, n_diag, n_below = _softmax_block_table(s // t, t)
    grid_spec = pltpu.PrefetchScalarGridSpec(
        num_scalar_prefetch=1,
        grid=(b, N_HEADS // 2),
        in_specs=[pl.BlockSpec((1, s, 2 * LANES), lambda i, h, tab: (i, 0, h)),
                  pl.BlockSpec((1, s, 2 * LANES), lambda i, h, tab: (i, 0, h)),
                  pl.BlockSpec((1, LANES, s), lambda i, h, tab: (i, h, 0))],
        out_specs=pl.BlockSpec((1, s, LANES), lambda i, h, tab: (i, 0, h)),
        scratch_shapes=[pltpu.VMEM((2, LANES, s), F32),
                        pltpu.VMEM((2, 1, s), F32),
                        pltpu.VMEM((2, 2, 1, t), F32),
                        pltpu.VMEM((2, 2, 1, t), F32),
                        pltpu.VMEM((2, t, t), F32), pltpu.VMEM((2, t, t), F32),
                        pltpu.VMEM((2, t, t), BF16), pltpu.VMEM((2, t, t), BF16)])
    return pl.pallas_call(
        functools.partial(_softmax_flat_kernel, t=t, n_diag=n_diag, n_below=n_below,
                          chunk_shift=chunk_shift),
        grid_spec=grid_spec,
        out_shape=jax.ShapeDtypeStruct((b, s, HEAD_W), BF16),
        compiler_params=_params(2),
        name=name,
    )(table, q, k, vt)


def _sb_block_table(n_tiles, t):
    diag = [(i * t, i * t, i * t, i) for i in range(n_tiles)]
    below = [(i * t, (i - d) * t, i * t, i) for d in range(1, n_tiles) for i in range(d, n_tiles)]
    idle = [(0, 0, n_tiles * t, n_tiles)]
    return jnp.asarray(np.array(diag + below + idle, np.int32).T), len(diag), len(below)


def _sb_flat_kernel(tab_ref, q_ref, k_ref, vt_ref, o_ref, acc_ref, r_ref, cut_ref, tot_ref, d_ref,
                    done_ref, z0_ref, z1_ref, e0_ref, e1_ref, w0_ref, w1_ref, *, t, n_diag,
                    n_below):
    in_head = _head_masks()
    own_rows = _head_rows()
    z_bufs, e_bufs, w_bufs = (z0_ref, z1_ref), (e0_ref, e1_ref), (w0_ref, w1_ref)
    kio = lax.broadcasted_iota(jnp.int32, (t, 1), 0)
    qio = lax.broadcasted_iota(jnp.int32, (1, t), 1)
    ss = lax.broadcasted_iota(jnp.int32, (t, t), 0)
    jj = lax.broadcasted_iota(jnp.int32, (t, t), 1)
    from_here = jnp.where(jj >= ss, 1.0, 0.0).astype(BF16)
    acc_ref[...] = jnp.zeros_like(acc_ref)
    r_ref[...] = jnp.zeros_like(r_ref)
    cut_ref[...] = jnp.zeros_like(cut_ref)
    for tile in range(n_diag + 1):
        done_ref[tile] = 0

    s_len = n_diag * t
    sub8 = lax.broadcasted_iota(jnp.int32, (8, 1), 0)
    head_rows = jnp.where(sub8 == 0, jnp.where(in_head[0], 1.0, 0.0),
                          jnp.where(sub8 == 1, jnp.where(in_head[1], 1.0, 0.0), 0.0)).astype(BF16)

    def sq_norms(ref):
        x = ref[0].astype(F32)
        return _dot_nt(head_rows, (x * x).astype(BF16))

    k_max = jnp.max(sq_norms(k_ref), axis=1, keepdims=True)
    cut = NORM_MARGIN * jnp.sqrt(sq_norms(q_ref) * k_max) + UNDERFLOW
    for hh in range(2):
        cut_ref[hh, :, 0:s_len] = cut[hh:hh + 1, :]

    def stages(base, diag):
        def item(i):
            return (pl.multiple_of(tab_ref[0, base + i], t), pl.multiple_of(tab_ref[1, base + i], t),
                    pl.multiple_of(tab_ref[2, base + i], t), tab_ref[3, base + i])

        def scores(i, par):
            q0, k0, _, _ = item(i)
            k = k_ref[0, pl.ds(k0, t), :]
            q = q_ref[0, pl.ds(q0, t), :]
            for hh in range(2):
                qh = jnp.where(in_head[hh], q, jnp.zeros_like(q))
                z_bufs[par][hh] = _dot_nt(k, qh)

        def drops(i, par):
            for hh in range(2):
                for a in range(0, t, ROW_CHUNK):
                    rows = slice(a, a + ROW_CHUNK)
                    z = z_bufs[par][hh, rows, :]
                    neg_abs = pltpu.bitcast(pltpu.bitcast(z, jnp.uint32) | SIGN_BIT, F32)
                    drop = jnp.maximum(z, 0.0) + jnp.log(1.0 + jnp.exp(neg_abs))
                    if diag:
                        drop = jnp.where(kio[rows] < qio, drop, 0.0)
                    d_ref[hh, rows, :] = drop.astype(BF16)
                incl = _bdot(from_here, d_ref[hh])
                for a in range(0, t, ROW_CHUNK):
                    rows = slice(a, a + ROW_CHUNK)
                    e = z_bufs[par][hh, rows, :] - incl[rows]
                    if diag:
                        e = jnp.where(kio[rows] < qio, e, NEG)
                    e_bufs[par][hh, rows, :] = e
                tot_ref[par, hh] = incl[0:1, :]

        def weights(i, par):
            _, _, q0, tile = item(i)
            slack = None
            for hh in range(2):
                r_old = r_ref[hh, :, pl.ds(q0, t)]
                for a in range(0, t, ROW_CHUNK):
                    rows = slice(a, a + ROW_CHUNK)
                    w_bufs[par][hh, rows, :] = jnp.exp(e_bufs[par][hh, rows, :] - r_old).astype(BF16)
                r_new = r_old + tot_ref[par, hh]
                r_ref[hh, :, pl.ds(q0, t)] = r_new
                over = r_new - cut_ref[hh, :, pl.ds(q0, t)]
                slack = over if slack is None else jnp.minimum(slack, over)
            if not diag:
                done_ref[tile] = (jnp.min(slack) > 0.0).astype(jnp.int32)

        def values(i, par):
            _, k0, q0, _ = item(i)
            vt = vt_ref[0, :, pl.ds(k0, t)]
            zero = jnp.zeros_like(vt)
            lhs = jnp.concatenate([jnp.where(own_rows[hh], vt, zero) for hh in range(2)], axis=1)
            acc_ref[:, pl.ds(q0, t)] = (acc_ref[:, pl.ds(q0, t)]
                                        + _bdot(lhs, w_bufs[par][...].reshape(2 * t, t)))

        return [scores, drops, weights, values]

    _pipeline(n_diag, stages(0, True))

    scores, drops, weights, values = stages(n_diag, False)

    def next_live(c):
        return lax.while_loop(
            lambda c: (c < n_below) & (done_ref[tab_ref[3, n_diag + c]] != 0), lambda c: c + 1, c)

    def after(c):
        return next_live(jnp.minimum(c + 1, n_below))

    def two_steps(state):
        a, b, i1, i2, i3 = state
        scores(a, 0)
        drops(i1, 1)
        weights(i2, 0)
        values(i3, 1)
        scores(b, 1)
        drops(a, 0)
        weights(i1, 1)
        values(i2, 0)
        a_next = after(b)
        return a_next, after(a_next), b, a, i1

    idle = jnp.int32(n_below)
    first = next_live(jnp.int32(0))
    lax.while_loop(lambda st: (st[0] < n_below) | (st[2] < n_below) | (st[3] < n_below)
                   | (st[4] < n_below), two_steps, (first, after(first), idle, idle, idle))

    def finish(tile, carry):
        q0 = pl.multiple_of(tile * t, t)
        o_ref[0, pl.ds(q0, t), :] = acc_ref[:, pl.ds(q0, t)].T.astype(o_ref.dtype)
        return carry

    lax.fori_loop(0, n_diag, finish, 0)


def _sb_attention_flat(qk, vt, *, t):
    b, s, _ = qk.shape
    table, n_diag, n_below = _sb_block_table(s // t, t)
    grid_spec = pltpu.PrefetchScalarGridSpec(
        num_scalar_prefetch=1,
        grid=(b, N_HEADS // 2),
        in_specs=[pl.BlockSpec((1, s, LANES), lambda i, h, tab: (i, 0, h)),
                  pl.BlockSpec((1, s, LANES), lambda i, h, tab: (i, 0, 4 + h)),
                  pl.BlockSpec((1, LANES, s), lambda i, h, tab: (i, h, 0))],
        out_specs=pl.BlockSpec((1, s, LANES), lambda i, h, tab: (i, 0, h)),
        scratch_shapes=[pltpu.VMEM((LANES, s + t), F32),
                        pltpu.VMEM((2, 1, s + t), F32),
                        pltpu.VMEM((2, 1, s + t), F32),
                        pltpu.VMEM((2, 2, 1, t), F32),
                        pltpu.VMEM((2, t, t), BF16),
                        pltpu.SMEM((s // t + 1,), jnp.int32),
                        pltpu.VMEM((2, t, t), F32), pltpu.VMEM((2, t, t), F32),
                        pltpu.VMEM((2, t, t), F32), pltpu.VMEM((2, t, t), F32),
                        pltpu.VMEM((2, t, t), BF16), pltpu.VMEM((2, t, t), BF16)])
    return pl.pallas_call(
        functools.partial(_sb_flat_kernel, t=t, n_diag=n_diag, n_below=n_below),
        grid_spec=grid_spec,
        out_shape=jax.ShapeDtypeStruct((b, s, HEAD_W), BF16),
        compiler_params=_params(2),
        name="sb_attention",
    )(table, qk, qk, vt)


def _merge_ffn_kernel(x_ref, yf_ref, ym_ref, ys_ref, gate_ref, mod_ref, wf_ref, wm_ref, ws_ref,
                      wo_ref, g_ref, wg_ref, wu_ref, wd_ref, gfin_ref, o_ref, *, final, fc):
    d = D_MODEL
    merged = (gate_ref[0, :, 0:d].astype(F32) * _bdot(yf_ref[0], wf_ref[...])
              + gate_ref[0, :, d:2 * d].astype(F32) * _bdot(ym_ref[0], wm_ref[...])
              + gate_ref[0, :, 2 * d:3 * d].astype(F32) * _bdot(ys_ref[0], ws_ref[...]))
    x = x_ref[0] + mod_ref[0, 2:3, :] * _bdot(merged.astype(BF16), wo_ref[...])

    sh, sc, gt = mod_ref[0, 3:4, :], mod_ref[0, 4:5, :], mod_ref[0, 5:6, :]
    u = ((_rms(x) * g_ref[...]) * (1.0 + sc) + sh).astype(BF16)
    acc = None
    for a in range(0, D_FF, fc):
        hg = _bdot(u, wg_ref[:, a:a + fc])
        hu = _bdot(u, wu_ref[:, a:a + fc])
        h = ((hg * _sigmoid(hg)) * hu).astype(BF16)
        part = _bdot(h, wd_ref[a:a + fc, :])
        acc = part if acc is None else acc + part
    out = x + gt * acc
    if final:
        out = _rms(out) * gfin_ref[...]
    o_ref[0] = out


def _merge_ffn(x, y_fox, y_mla, y_sb, gates, mod, weights, g_final, *, layer, tm, final):
    b, s, d = x.shape
    row = lambda w: pl.BlockSpec((1, tm, w), lambda i, j: (i, j, 0))
    per_layer = lambda *shape: _resident_layer(shape, layer)
    return pl.pallas_call(
        functools.partial(_merge_ffn_kernel, final=final, fc=D_FF // 2),
        grid=(b, s // tm),
        in_specs=[row(d), row(HEAD_W), row(HEAD_W), row(HEAD_W), row(3 * d),
                  pl.BlockSpec((1, 6, d), lambda i, j: (i, 0, 0)),
                  per_layer(HEAD_W, d), per_layer(HEAD_W, d), per_layer(HEAD_W, d),
                  per_layer(d, d), per_layer(1, d),
                  per_layer(d, D_FF), per_layer(d, D_FF), per_layer(D_FF, d),
                  _resident((1, d))],
        out_specs=row(d),
        out_shape=jax.ShapeDtypeStruct((b, s, d), F32),
        compiler_params=_params(2),
        name="merge_ffn",
    )(x, y_fox, y_mla, y_sb, gates, mod, *weights, g_final)


def _slot_cols(w, width_per_head, used):
    lead = w.shape[:-1]
    w = w.reshape(lead + (N_HEADS, width_per_head))[..., :used]
    pad = [(0, 0)] * (w.ndim - 1) + [(0, LANES - used)]
    return jnp.pad(w, pad).reshape(lead + (SLOT_W,))


def _layout_w_in(w):
    fox, ff, ql, kvl, kr, sb, gate = jnp.split(
        w, [1536, 1544, 1928, 2184, 2216, 3752], axis=-1)
    z = lambda n: jnp.zeros(w.shape[:-1] + (n,), w.dtype)
    kr_slot = jnp.concatenate([z(MLA_NOPE), kr, z(LANES - MLA_NOPE - MLA_ROPE)], axis=-1)
    ff_slot = jnp.concatenate([ff, z(LANES - N_HEADS)], axis=-1)
    return jnp.concatenate([fox, sb, gate, ql, kvl, kr_slot, ff_slot], axis=-1).astype(BF16)


def kernel(x, c, positions, g_mix, w_ada, b_ada, w_in, b_fox_f, g_mla_q, w_mla_uq, g_mla_kv,
           w_mla_ukv, w_o_fox, w_o_mla, w_o_sb, w_out, g_ffn, w_ffn_gate, w_ffn_up, w_ffn_down,
           g_final):
    b, s, d = x.shape
    depth = w_in.shape[0]
    tm = min(s, 512)
    tm_ffn = min(s, 512)
    ta = min(s, 512)
    tsb = min(s, 256)

    mod = _ada(c, w_ada, b_ada).reshape(depth, b, 6, d)
    tables = _rope_tables(positions)

    row_vec = lambda v: v.reshape(depth, 1, -1)
    wkv = w_mla_ukv.reshape(depth, MLA_KV_RANK, N_HEADS, 2 * HEAD_DIM)[..., MLA_NOPE:]
    inproj_weights = (
        row_vec(g_mix), _layout_w_in(w_in),
        row_vec(jnp.pad(b_fox_f, ((0, 0), (0, LANES - N_HEADS)))),
        row_vec(g_mla_q), row_vec(g_mla_kv),
        _slot_cols(w_mla_uq, MLA_NOPE + MLA_ROPE, MLA_NOPE + MLA_ROPE).astype(BF16),
        _slot_cols(w_mla_ukv, 2 * HEAD_DIM, MLA_NOPE).astype(BF16),
        wkv.reshape(depth, MLA_KV_RANK, HEAD_W).astype(BF16))
    merge_ffn_weights = (
        w_o_fox.astype(BF16), w_o_mla.astype(BF16), w_o_sb.astype(BF16), w_out.astype(BF16),
        row_vec(g_ffn), w_ffn_gate.astype(BF16), w_ffn_up.astype(BF16), w_ffn_down.astype(BF16))

    for l in range(depth):
        fox_q, fox_k, fox_vt, sb_qk, sb_vt, gates, mq, mk, mvt = _inproj(
            x, mod[l], inproj_weights, tables, layer=l, tm=tm)
        y_fox = _softmax_attention_flat(fox_q, fox_k, fox_vt, t=ta, chunk_shift=0,
                                        name="fox_attention")
        y_mla = _softmax_attention_flat(mq, mk, mvt, t=ta, chunk_shift=CHUNK_SHIFT,
                                        name="mla_attention")
        y_sb = _sb_attention_flat(sb_qk, sb_vt, t=tsb)
        x = _merge_ffn(x, y_fox, y_mla, y_sb, gates, mod[l], merge_ffn_weights,
                       g_final.reshape(1, d), layer=l, tm=tm_ffn, final=(l == depth - 1))
    return x
```

```python
import functools

import numpy as np
import jax
import jax.numpy as jnp
from jax import lax
from jax.experimental import pallas as pl
from jax.experimental.pallas import tpu as pltpu

F32 = jnp.float32
BF16 = jnp.bfloat16

D_MODEL = 1024
EPS = 1e-6
HEAD_DIM = 64
N_HEADS = 8
HEAD_W = N_HEADS * HEAD_DIM
CHUNK_SHIFT = 6
MLA_Q_RANK = 384
MLA_KV_RANK = 256
MLA_NOPE = 64
MLA_ROPE = 32
ROPE_HALF = MLA_ROPE // 2
ROPE_BASE = 10000.0
MLA_SCALE = (MLA_NOPE + MLA_ROPE) ** -0.5
QK_SCALE = HEAD_DIM ** -0.5
D_FF = 2816
LANES = 128
SLOT_W = N_HEADS * LANES
N_PIECES = 3

C_FQ = 0
C_FK = 512
C_FV = 1024
C_SB = 1536
C_GATE = 3072
C_QL = 6144
C_KVL = 6528
C_KR = 6784
C_FF = 6912
IN_W = 7040

ROW_CHUNK = 32
PIPELINE_UNROLL = 2
SIGN_BIT = np.uint32(0x80000000)
UNDERFLOW = 105.0
NEG = -1e30
VMEM_LIMIT = 56 * 1024 * 1024


def _params(n_grid):
    return pltpu.CompilerParams(dimension_semantics=("arbitrary",) * n_grid,
                                vmem_limit_bytes=VMEM_LIMIT)


def _resident(shape):
    nd = len(shape)
    return pl.BlockSpec(shape, lambda *_: (0,) * nd, pipeline_mode=pl.Buffered(1))


def _resident_layer(shape, layer):
    nd = len(shape)
    return pl.BlockSpec((pl.Squeezed(),) + tuple(shape), lambda *_: (layer,) + (0,) * nd,
                        pipeline_mode=pl.Buffered(1))


def _bdot(a, b):
    return jnp.dot(a, b, preferred_element_type=F32)


def _dot_nt(a, b):
    return lax.dot_general(a, b, (((1,), (1,)), ((), ())), preferred_element_type=F32)


def _split3(v):
    hi = v.astype(BF16)
    r = v - hi.astype(F32)
    mid = r.astype(BF16)
    lo = (r - mid.astype(F32)).astype(BF16)
    return hi, mid, lo


def _log_sigmoid(z):
    return jnp.minimum(z, 0.0) - jnp.log1p(jnp.exp(-jnp.abs(z)))


def _sigmoid(z):
    return 1.0 / (1.0 + jnp.exp(-z))


def _rms(x):
    return x * lax.rsqrt(jnp.mean(x * x, axis=-1, keepdims=True) + EPS)


def _ada_kernel(c_ref, w_ref, b_ref, o_ref):
    c = c_ref[...]
    cond = c * _sigmoid(c)
    chi, cmid, clo = _split3(cond)
    whi, wmid, wlo = _split3(w_ref[0])
    acc = _bdot(chi, whi) + (_bdot(chi, wmid) + _bdot(cmid, whi))
    acc = acc + (_bdot(chi, wlo) + _bdot(clo, whi) + _bdot(cmid, wmid))
    o_ref[0] = acc + b_ref[0]


def _ada(c, w_ada, b_ada):
    depth, d, n = w_ada.shape
    b = c.shape[0]
    tn = 1024
    return pl.pallas_call(
        _ada_kernel,
        grid=(depth, n // tn),
        in_specs=[pl.BlockSpec((b, d), lambda l, j: (0, 0)),
                  pl.BlockSpec((1, d, tn), lambda l, j: (l, 0, j)),
                  pl.BlockSpec((1, 1, tn), lambda l, j: (l, 0, j))],
        out_specs=pl.BlockSpec((1, b, tn), lambda l, j: (l, 0, j)),
        out_shape=jax.ShapeDtypeStruct((depth, b, n), F32),
        compiler_params=_params(2),
        name="adaln",
    )(c, w_ada, b_ada.reshape(depth, 1, n))


def _rope_kernel(pos_ref, inv_ref, cos_ref, sina_ref, sinb_ref):
    ang = pos_ref[0].astype(F32) * inv_ref[...]
    lane = lax.broadcasted_iota(jnp.int32, (1, LANES), 1)
    sin = jnp.sin(ang)
    cos_ref[0] = jnp.cos(ang)
    lo, mid, hi = MLA_NOPE, MLA_NOPE + ROPE_HALF, MLA_NOPE + MLA_ROPE
    sina_ref[0] = jnp.where((lane >= lo) & (lane < mid), -sin, 0.0)
    sinb_ref[0] = jnp.where((lane >= mid) & (lane < hi), sin, 0.0)


def _rope_tables(positions):
    b, s = positions.shape
    ts = min(s, 512)
    inv = ROPE_BASE ** (-jnp.arange(ROPE_HALF, dtype=F32) / ROPE_HALF)
    inv_slot = jnp.concatenate([jnp.zeros((MLA_NOPE,), F32), inv, inv,
                                jnp.zeros((LANES - MLA_NOPE - MLA_ROPE,), F32)]).reshape(1, LANES)
    spec = pl.BlockSpec((1, ts, LANES), lambda i, j: (i, j, 0))
    return pl.pallas_call(
        _rope_kernel,
        grid=(b, s // ts),
        in_specs=[pl.BlockSpec((1, ts, 1), lambda i, j: (i, j, 0)),
                  pl.BlockSpec((1, LANES), lambda i, j: (0, 0))],
        out_specs=[spec, spec, spec],
        out_shape=[jax.ShapeDtypeStruct((b, s, LANES), F32)] * 3,
        compiler_params=_params(2),
        name="rope_tables",
    )(positions.reshape(b, s, 1), inv_slot)


def _rope_slot(x, cos, sina, sinb):
    return (x * cos + pltpu.roll(x, LANES - ROPE_HALF, 1) * sina
            + pltpu.roll(x, ROPE_HALF, 1) * sinb)


def _inproj_kernel(x_ref, mod_ref, g_ref, w_ref, bff_ref, gq_ref, gkv_ref, wuq_ref, wkn_ref,
                   wkv_ref, cos_ref, sina_ref, sinb_ref, place_ref, qc_ref, kc_ref,
                   fq_ref, fk_ref, fv_ref, sb_ref, sv_ref, gate_ref, mq_ref, mk_ref, mv_ref,
                   carry_ref, *, tm):
    x = x_ref[0]
    sh, sc = mod_ref[0, 0:1, :], mod_ref[0, 1:2, :]
    u = ((_rms(x) * g_ref[...]) * (1.0 + sc) + sh).astype(BF16)

    def seg(a, b):
        return _bdot(u, w_ref[:, a:b])

    @pl.when(pl.program_id(1) == 0)
    def _():
        carry_ref[...] = jnp.zeros_like(carry_ref)

    logf = _log_sigmoid(seg(C_FF, IN_W) + bff_ref[...])
    r = lax.broadcasted_iota(jnp.int32, (tm, tm), 0)
    c = lax.broadcasted_iota(jnp.int32, (tm, tm), 1)
    tri = jnp.where(r >= c, 1.0, 0.0).astype(BF16)
    lane = lax.broadcasted_iota(jnp.int32, (1, LANES), 1)

    def pack3(v):
        p0, p1, p2 = (p.astype(F32) for p in _split3(v))
        return jnp.where(lane < N_HEADS, p0,
                         jnp.where(lane < 2 * N_HEADS, pltpu.roll(p1, N_HEADS, 1),
                                   jnp.where(lane < 3 * N_HEADS, pltpu.roll(p2, 2 * N_HEADS, 1),
                                             0.0))).astype(BF16)

    part = _bdot(tri, pack3(logf))
    cum = ((part + pltpu.roll(part, LANES - N_HEADS, 1))
           + pltpu.roll(part, LANES - 2 * N_HEADS, 1)) + carry_ref[...]
    carry_ref[...] = jnp.where(lane < N_HEADS, cum[tm - 1:tm, :], 0.0)
    placed = _bdot(pack3(cum), place_ref[...])

    def to_slots(compact, extra, out_ref):
        for h in range(N_HEADS):
            x = compact[:, LANES * (h // 2):LANES * (h // 2 + 1)]
            head = pltpu.roll(x, HEAD_DIM, 1) if h % 2 else x
            slot = slice(LANES * h, LANES * (h + 1))
            out_ref[0, :, slot] = jnp.where(lane < HEAD_DIM, head, extra[:, slot]).astype(BF16)

    first_aug = (lax.broadcasted_iota(jnp.int32, (1, SLOT_W), 1) & (LANES - 1)) < HEAD_DIM + N_PIECES
    to_slots(seg(C_FQ, C_FK) * QK_SCALE, jnp.where(first_aug, qc_ref[...], placed), fq_ref)
    to_slots(seg(C_FK, C_FV), jnp.where(first_aug, placed, kc_ref[...]), fk_ref)
    fv_ref[0] = seg(C_FV, C_SB).astype(BF16).T

    sb_ref[0, :, 0:512] = (seg(C_SB, C_SB + 512) * QK_SCALE).astype(BF16)
    sb_ref[0, :, 512:1024] = seg(C_SB + 512, C_SB + 1024).astype(BF16)
    sv_ref[0] = seg(C_SB + 1024, C_GATE).astype(BF16).T
    for c in range(6):
        a = 512 * c
        gate_ref[0, :, a:a + 512] = _sigmoid(seg(C_GATE + a, C_GATE + a + 512)).astype(BF16)

    cos, sina, sinb = cos_ref[0], sina_ref[0], sinb_ref[0]
    cq = (_rms(seg(C_QL, C_KVL)) * gq_ref[...]).astype(BF16)
    ckv = (_rms(seg(C_KVL, C_KR)) * gkv_ref[...]).astype(BF16)
    kr = _rope_slot(seg(C_KR, C_FF), cos, sina, sinb)
    for h in range(N_HEADS):
        a = LANES * h
        q = _rope_slot(_bdot(cq, wuq_ref[:, a:a + LANES]), cos, sina, sinb)
        mq_ref[0, :, a:a + LANES] = (q * MLA_SCALE).astype(BF16)
        mk_ref[0, :, a:a + LANES] = (_bdot(ckv, wkn_ref[:, a:a + LANES]) + kr).astype(BF16)
    mv_ref[0] = _bdot(ckv, wkv_ref[...]).astype(BF16).T


def _fox_placement():
    place = np.zeros((LANES, SLOT_W), np.float32)
    qc = np.zeros((1, SLOT_W), np.float32)
    kc = np.zeros((1, SLOT_W), np.float32)
    for h in range(N_HEADS):
        for p in range(N_PIECES):
            place[N_HEADS * p + h, LANES * h + HEAD_DIM + p] = 1.0
            place[N_HEADS * p + h, LANES * h + HEAD_DIM + N_PIECES + p] = 1.0
            qc[0, LANES * h + HEAD_DIM + p] = -1.0
            kc[0, LANES * h + HEAD_DIM + N_PIECES + p] = 1.0
    return jnp.asarray(place, BF16), jnp.asarray(qc), jnp.asarray(kc)


def _inproj(x, mod, weights, tables, *, layer, tm):
    b, s, d = x.shape
    cos, sina, sinb = tables
    place, qc, kc = _fox_placement()
    per_layer = lambda *shape: _resident_layer(shape, layer)
    row = lambda w: pl.BlockSpec((1, tm, w), lambda i, j: (i, j, 0))
    col = lambda w: pl.BlockSpec((1, w, tm), lambda i, j: (i, 0, j))
    outs = [(SLOT_W, False),
            (SLOT_W, False),
            (HEAD_W, True),
            (2 * HEAD_W, False),
            (HEAD_W, True),
            (3 * D_MODEL, False),
            (SLOT_W, False),
            (SLOT_W, False),
            (HEAD_W, True)]
    out_specs = [col(w) if tr else row(w) for w, tr in outs]
    out_shape = [jax.ShapeDtypeStruct((b, w, s) if tr else (b, s, w), BF16) for w, tr in outs]
    return pl.pallas_call(
        functools.partial(_inproj_kernel, tm=tm),
        grid=(b, s // tm),
        in_specs=[row(d),
                  pl.BlockSpec((1, 6, d), lambda i, j: (i, 0, 0)),
                  per_layer(1, d), per_layer(d, IN_W), per_layer(1, LANES),
                  per_layer(1, MLA_Q_RANK), per_layer(1, MLA_KV_RANK),
                  per_layer(MLA_Q_RANK, SLOT_W), per_layer(MLA_KV_RANK, SLOT_W),
                  per_layer(MLA_KV_RANK, HEAD_W),
                  row(LANES), row(LANES), row(LANES),
                  _resident((LANES, SLOT_W)), _resident((1, SLOT_W)), _resident((1, SLOT_W))],
        out_specs=out_specs,
        out_shape=out_shape,
        scratch_shapes=[pltpu.VMEM((1, LANES), F32)],
        compiler_params=_params(2),
        name="inproj",
    )(x, mod, *weights, cos, sina, sinb, place, qc, kc)


def _head_rows():
    row = lax.broadcasted_iota(jnp.int32, (LANES, 1), 0)
    return [(row >= HEAD_DIM * hh) & (row < HEAD_DIM * (hh + 1)) for hh in range(2)]


def _head_masks():
    lane = lax.broadcasted_iota(jnp.int32, (1, LANES), 1)
    return [(lane >= HEAD_DIM * hh) & (lane < HEAD_DIM * (hh + 1)) for hh in range(2)]


def _colmax(x, chunk=64):
    rows = x.shape[0]
    if rows > chunk and rows % chunk == 0:
        acc = x[:chunk]
        for a in range(chunk, rows, chunk):
            acc = jnp.maximum(acc, x[a:a + chunk])
        x, rows = acc, chunk
    while rows > 8 and rows % 16 == 0:
        rows //= 2
        x = jnp.maximum(x[:rows], x[rows:])
    return jnp.max(x, axis=0, keepdims=True)


def _pipeline(n, stages, unroll=PIPELINE_UNROLL):
    depth = len(stages)

    def static_step(t):
        for s, stage in enumerate(stages):
            if 0 <= t - s < n:
                stage(t - s, (t - s) % 2)

    first_full = depth - 1
    n_full = max(n - depth + 1, 0)
    for t in range(min(first_full, n + depth - 1)):
        static_step(t)
    n_loops = n_full // unroll
    if n_loops >= 1:
        def body(it, carry):
            t = first_full + unroll * it
            for u in range(unroll):
                for s, stage in enumerate(stages):
                    stage(t + u - s, (first_full + u - s) % 2)
            return carry

        lax.fori_loop(0, n_loops, body, 0)
    for t in range(first_full + unroll * n_loops, n + depth - 1):
        static_step(t)


def _softmax_block_table(n_tiles, t):
    diag = [(i * t, i * t) for i in range(n_tiles)]
    below = [(i * t, j * t) for i in range(1, n_tiles) for j in range(i)]
    return jnp.asarray(np.array(diag + below, np.int32).T), len(diag), len(below)


def _softmax_flat_kernel(tab_ref, q_ref, k_ref, vt_ref, o_ref, acc_ref, m_ref, al_ref, top_ref,
                         s0_ref, s1_ref, p0_ref, p1_ref, *, t, n_diag, n_below, chunk_shift):
    own_rows = _head_rows()
    s_bufs, p_bufs = (s0_ref, s1_ref), (p0_ref, p1_ref)
    kio = lax.broadcasted_iota(jnp.int32, (t, 1), 0)
    qio = lax.broadcasted_iota(jnp.int32, (1, t), 1)
    acc_ref[...] = jnp.zeros_like(acc_ref)
    m_ref[...] = jnp.full_like(m_ref, NEG)

    def stages(base, diag):
        def item(i):
            return (pl.multiple_of(tab_ref[0, base + i], t), pl.multiple_of(tab_ref[1, base + i], t))

        def scores(i, par):
            q0, k0 = item(i)
            for hh in range(2):
                slot = slice(LANES * hh, LANES * (hh + 1))
                s = _dot_nt(k_ref[0, pl.ds(k0, t), slot],
                            q_ref[0, pl.ds(q0, t), slot])
                if diag:
                    s = jnp.where((kio >> chunk_shift) <= (qio >> chunk_shift), s, NEG)
                s_bufs[par][hh] = s
                top_ref[par, hh] = _colmax(s)

        def numerators(i, par):
            q0, _ = item(i)
            for hh in range(2):
                m_old = m_ref[hh, :, pl.ds(q0, t)]
                m_new = jnp.maximum(m_old, top_ref[par, hh])
                for a in range(0, t, ROW_CHUNK):
                    rows = slice(a, a + ROW_CHUNK)
                    p_bufs[par][hh, rows, :] = jnp.exp(s_bufs[par][hh, rows, :] - m_new).astype(BF16)
                al_ref[par, hh] = jnp.exp(m_old - m_new)
                m_ref[hh, :, pl.ds(q0, t)] = m_new

        def values(i, par):
            q0, k0 = item(i)
            vt = vt_ref[0, :, pl.ds(k0, t)]
            for hh in range(2):
                v1 = jnp.where(own_rows[hh], vt, jnp.ones_like(vt))
                acc_ref[hh, :, pl.ds(q0, t)] = (acc_ref[hh, :, pl.ds(q0, t)] * al_ref[par, hh]
                                                + _bdot(v1, p_bufs[par][hh]))

        return [scores, numerators, values]

    _pipeline(n_diag, stages(0, True))
    _pipeline(n_below, stages(n_diag, False))

    sub = lax.broadcasted_iota(jnp.int32, (LANES, 1), 0)

    def finish(tile, carry):
        q0 = pl.multiple_of(tile * t, t)
        outs = []
        for hh in range(2):
            acc = acc_ref[hh, :, pl.ds(q0, t)]
            denom = acc[HEAD_DIM * (1 - hh):HEAD_DIM * (1 - hh) + 1, :]
            outs.append(acc / denom)
        o_ref[0, pl.ds(q0, t), :] = jnp.where(sub < HEAD_DIM, outs[0], outs[1]).T.astype(o_ref.dtype)
        return carry

    lax.fori_loop(0, n_diag, finish, 0)


def _softmax_attention_flat(q, k, vt, *, t, chunk_shift, name):
    b, s, _ = k.shape
    table, n_diag, n_below = _softmax_block_table(s // t, t)
    grid_spec = pltpu.PrefetchScalarGridSpec(
        num_scalar_prefetch=1,
        grid=(b, N_HEADS // 2),
        in_specs=[pl.BlockSpec((1, s, 2 * LANES), lambda i, h, tab: (i, 0, h)),
                  pl.BlockSpec((1, s, 2 * LANES), lambda i, h, tab: (i, 0, h)),
                  pl.BlockSpec((1, LANES, s), lambda i, h, tab: (i, h, 0))],
        out_specs=pl.BlockSpec((1, s, LANES), lambda i, h, tab: (i, 0, h)),
        scratch_shapes=[pltpu.VMEM((2, LANES, s), F32),
                        pltpu.VMEM((2, 1, s), F32),
                        pltpu.VMEM((2, 2, 1, t), F32),
                        pltpu.VMEM((2, 2, 1, t), F32),
                        pltpu.VMEM((2, t, t), F32), pltpu.VMEM((2, t, t), F32),
                        pltpu.VMEM((2, t, t), BF16), pltpu.VMEM((2, t, t), BF16)])
    return pl.pallas_call(
        functools.partial(_softmax_flat_kernel, t=t, n_diag=n_diag, n_below=n_below,
                          chunk_shift=chunk_shift),
        grid_spec=grid_spec,
        out_shape=jax.ShapeDtypeStruct((b, s, HEAD_W), BF16),
        compiler_params=_params(2),
        name=name,
    )(table, q, k, vt)


def _sb_block_table(n_tiles, t):
    diag = [(i * t, i * t, i * t, i) for i in range(n_tiles)]
    below = [(i * t, (i - d) * t, i * t, i) for d in range(1, n_tiles) for i in range(d, n_tiles)]
    idle = [(0, 0, n_tiles * t, n_tiles)]
    return jnp.asarray(np.array(diag + below + idle, np.int32).T), len(diag), len(below)


def _sb_flat_kernel(tab_ref, q_ref, k_ref, vt_ref, o_ref, acc_ref, r_ref, tot_ref, d_ref, done_ref,
                    z0_ref, z1_ref, e0_ref, e1_ref, w0_ref, w1_ref, *, t, n_diag, n_below):
    in_head = _head_masks()
    own_rows = _head_rows()
    z_bufs, e_bufs, w_bufs = (z0_ref, z1_ref), (e0_ref, e1_ref), (w0_ref, w1_ref)
    kio = lax.broadcasted_iota(jnp.int32, (t, 1), 0)
    qio = lax.broadcasted_iota(jnp.int32, (1, t), 1)
    ss = lax.broadcasted_iota(jnp.int32, (t, t), 0)
    jj = lax.broadcasted_iota(jnp.int32, (t, t), 1)
    later = jnp.where(jj > ss, 1.0, 0.0).astype(BF16)
    acc_ref[...] = jnp.zeros_like(acc_ref)
    r_ref[...] = jnp.zeros_like(r_ref)
    for tile in range(n_diag + 1):
        done_ref[tile] = 0

    def stages(base, diag):
        def item(i):
            return (pl.multiple_of(tab_ref[0, base + i], t), pl.multiple_of(tab_ref[1, base + i], t),
                    pl.multiple_of(tab_ref[2, base + i], t), tab_ref[3, base + i])

        def scores(i, par):
            q0, k0, _, _ = item(i)
            k = k_ref[0, pl.ds(k0, t), :]
            q = q_ref[0, pl.ds(q0, t), :]
            for hh in range(2):
                qh = jnp.where(in_head[hh], q, jnp.zeros_like(q))
                z_bufs[par][hh] = _dot_nt(k, qh)

        def drops(i, par):
            for hh in range(2):
                for a in range(0, t, ROW_CHUNK):
                    rows = slice(a, a + ROW_CHUNK)
                    z = z_bufs[par][hh, rows, :]
                    neg_abs = pltpu.bitcast(pltpu.bitcast(z, jnp.uint32) | SIGN_BIT, F32)
                    drop = jnp.maximum(z, 0.0) + jnp.log(1.0 + jnp.exp(neg_abs))
                    log_beta = z - drop
                    if diag:
                        drop = jnp.where(kio[rows] < qio, drop, 0.0)
                        log_beta = jnp.where(kio[rows] < qio, log_beta, NEG)
                    d_ref[hh, rows, :] = drop.astype(BF16)
                    e_bufs[par][hh, rows, :] = log_beta
                after = _bdot(later, d_ref[hh])
                for a in range(0, t, ROW_CHUNK):
                    rows = slice(a, a + ROW_CHUNK)
                    e_bufs[par][hh, rows, :] = e_bufs[par][hh, rows, :] - after[rows]
                tot_ref[par, hh] = after[0:1, :] + d_ref[hh, 0:1, :].astype(F32)

        def weights(i, par):
            _, _, q0, tile = item(i)
            slack = None
            for hh in range(2):
                r_old = r_ref[hh, :, pl.ds(q0, t)]
                for a in range(0, t, ROW_CHUNK):
                    rows = slice(a, a + ROW_CHUNK)
                    w_bufs[par][hh, rows, :] = jnp.exp(e_bufs[par][hh, rows, :] - r_old).astype(BF16)
                r_new = r_old + tot_ref[par, hh]
                r_ref[hh, :, pl.ds(q0, t)] = r_new
                slack = r_new if slack is None else jnp.minimum(slack, r_new)
            if not diag:
                done_ref[tile] = (jnp.min(slack) > UNDERFLOW).astype(jnp.int32)

        def values(i, par):
            _, k0, q0, _ = item(i)
            vt = vt_ref[0, :, pl.ds(k0, t)]
            zero = jnp.zeros_like(vt)
            lhs = jnp.concatenate([jnp.where(own_rows[hh], vt, zero) for hh in range(2)], axis=1)
            acc_ref[:, pl.ds(q0, t)] = (acc_ref[:, pl.ds(q0, t)]
                                        + _bdot(lhs, w_bufs[par][...].reshape(2 * t, t)))

        return [scores, drops, weights, values]

    _pipeline(n_diag, stages(0, True))

    scores, drops, weights, values = stages(n_diag, False)

    def next_live(c):
        return lax.while_loop(
            lambda c: (c < n_below) & (done_ref[tab_ref[3, n_diag + c]] != 0), lambda c: c + 1, c)

    def after(c):
        return next_live(jnp.minimum(c + 1, n_below))

    def two_steps(state):
        a, b, i1, i2, i3 = state
        scores(a, 0)
        drops(i1, 1)
        weights(i2, 0)
        values(i3, 1)
        scores(b, 1)
        drops(a, 0)
        weights(i1, 1)
        values(i2, 0)
        a_next = after(b)
        return a_next, after(a_next), b, a, i1

    idle = jnp.int32(n_below)
    first = next_live(jnp.int32(0))
    lax.while_loop(lambda st: (st[0] < n_below) | (st[2] < n_below) | (st[3] < n_below)
                   | (st[4] < n_below), two_steps, (first, after(first), idle, idle, idle))

    def finish(tile, carry):
        q0 = pl.multiple_of(tile * t, t)
        o_ref[0, pl.ds(q0, t), :] = acc_ref[:, pl.ds(q0, t)].T.astype(o_ref.dtype)
        return carry

    lax.fori_loop(0, n_diag, finish, 0)


def _sb_attention_flat(qk, vt, *, t):
    b, s, _ = qk.shape
    table, n_diag, n_below = _sb_block_table(s // t, t)
    grid_spec = pltpu.PrefetchScalarGridSpec(
        num_scalar_prefetch=1,
        grid=(b, N_HEADS // 2),
        in_specs=[pl.BlockSpec((1, s, LANES), lambda i, h, tab: (i, 0, h)),
                  pl.BlockSpec((1, s, LANES), lambda i, h, tab: (i, 0, 4 + h)),
                  pl.BlockSpec((1, LANES, s), lambda i, h, tab: (i, h, 0))],
        out_specs=pl.BlockSpec((1, s, LANES), lambda i, h, tab: (i, 0, h)),
        scratch_shapes=[pltpu.VMEM((LANES, s + t), F32),
                        pltpu.VMEM((2, 1, s + t), F32),
                        pltpu.VMEM((2, 2, 1, t), F32),
                        pltpu.VMEM((2, t, t), BF16),
                        pltpu.SMEM((s // t + 1,), jnp.int32),
                        pltpu.VMEM((2, t, t), F32), pltpu.VMEM((2, t, t), F32),
                        pltpu.VMEM((2, t, t), F32), pltpu.VMEM((2, t, t), F32),
                        pltpu.VMEM((2, t, t), BF16), pltpu.VMEM((2, t, t), BF16)])
    return pl.pallas_call(
        functools.partial(_sb_flat_kernel, t=t, n_diag=n_diag, n_below=n_below),
        grid_spec=grid_spec,
        out_shape=jax.ShapeDtypeStruct((b, s, HEAD_W), BF16),
        compiler_params=_params(2),
        name="sb_attention",
    )(table, qk, qk, vt)


def _merge_ffn_kernel(x_ref, yf_ref, ym_ref, ys_ref, gate_ref, mod_ref, wf_ref, wm_ref, ws_ref,
                      wo_ref, g_ref, wg_ref, wu_ref, wd_ref, gfin_ref, o_ref, *, final, fc):
    d = D_MODEL
    merged = (gate_ref[0, :, 0:d].astype(F32) * _bdot(yf_ref[0], wf_ref[...])
              + gate_ref[0, :, d:2 * d].astype(F32) * _bdot(ym_ref[0], wm_ref[...])
              + gate_ref[0, :, 2 * d:3 * d].astype(F32) * _bdot(ys_ref[0], ws_ref[...]))
    x = x_ref[0] + mod_ref[0, 2:3, :] * _bdot(merged.astype(BF16), wo_ref[...])

    sh, sc, gt = mod_ref[0, 3:4, :], mod_ref[0, 4:5, :], mod_ref[0, 5:6, :]
    u = ((_rms(x) * g_ref[...]) * (1.0 + sc) + sh).astype(BF16)
    acc = None
    for a in range(0, D_FF, fc):
        hg = _bdot(u, wg_ref[:, a:a + fc])
        hu = _bdot(u, wu_ref[:, a:a + fc])
        h = ((hg * _sigmoid(hg)) * hu).astype(BF16)
        part = _bdot(h, wd_ref[a:a + fc, :])
        acc = part if acc is None else acc + part
    out = x + gt * acc
    if final:
        out = _rms(out) * gfin_ref[...]
    o_ref[0] = out


def _merge_ffn(x, y_fox, y_mla, y_sb, gates, mod, weights, g_final, *, layer, tm, final):
    b, s, d = x.shape
    row = lambda w: pl.BlockSpec((1, tm, w), lambda i, j: (i, j, 0))
    per_layer = lambda *shape: _resident_layer(shape, layer)
    return pl.pallas_call(
        functools.partial(_merge_ffn_kernel, final=final, fc=D_FF // 2),
        grid=(b, s // tm),
        in_specs=[row(d), row(HEAD_W), row(HEAD_W), row(HEAD_W), row(3 * d),
                  pl.BlockSpec((1, 6, d), lambda i, j: (i, 0, 0)),
                  per_layer(HEAD_W, d), per_layer(HEAD_W, d), per_layer(HEAD_W, d),
                  per_layer(d, d), per_layer(1, d),
                  per_layer(d, D_FF), per_layer(d, D_FF), per_layer(D_FF, d),
                  _resident((1, d))],
        out_specs=row(d),
        out_shape=jax.ShapeDtypeStruct((b, s, d), F32),
        compiler_params=_params(2),
        name="merge_ffn",
    )(x, y_fox, y_mla, y_sb, gates, mod, *weights, g_final)


def _slot_cols(w, width_per_head, used):
    lead = w.shape[:-1]
    w = w.reshape(lead + (N_HEADS, width_per_head))[..., :used]
    pad = [(0, 0)] * (w.ndim - 1) + [(0, LANES - used)]
    return jnp.pad(w, pad).reshape(lead + (SLOT_W,))


def _layout_w_in(w):
    fox, ff, ql, kvl, kr, sb, gate = jnp.split(
        w, [1536, 1544, 1928, 2184, 2216, 3752], axis=-1)
    z = lambda n: jnp.zeros(w.shape[:-1] + (n,), w.dtype)
    kr_slot = jnp.concatenate([z(MLA_NOPE), kr, z(LANES - MLA_NOPE - MLA_ROPE)], axis=-1)
    ff_slot = jnp.concatenate([ff, z(LANES - N_HEADS)], axis=-1)
    return jnp.concatenate([fox, sb, gate, ql, kvl, kr_slot, ff_slot], axis=-1).astype(BF16)


def kernel(x, c, positions, g_mix, w_ada, b_ada, w_in, b_fox_f, g_mla_q, w_mla_uq, g_mla_kv,
           w_mla_ukv, w_o_fox, w_o_mla, w_o_sb, w_out, g_ffn, w_ffn_gate, w_ffn_up, w_ffn_down,
           g_final):
    b, s, d = x.shape
    depth = w_in.shape[0]
    tm = min(s, 512)
    tm_ffn = min(s, 512)
    ta = min(s, 512)
    tsb = min(s, 256)

    mod = _ada(c, w_ada, b_ada).reshape(depth, b, 6, d)
    tables = _rope_tables(positions)

    row_vec = lambda v: v.reshape(depth, 1, -1)
    wkv = w_mla_ukv.reshape(depth, MLA_KV_RANK, N_HEADS, 2 * HEAD_DIM)[..., MLA_NOPE:]
    inproj_weights = (
        row_vec(g_mix), _layout_w_in(w_in),
        row_vec(jnp.pad(b_fox_f, ((0, 0), (0, LANES - N_HEADS)))),
        row_vec(g_mla_q), row_vec(g_mla_kv),
        _slot_cols(w_mla_uq, MLA_NOPE + MLA_ROPE, MLA_NOPE + MLA_ROPE).astype(BF16),
        _slot_cols(w_mla_ukv, 2 * HEAD_DIM, MLA_NOPE).astype(BF16),
        wkv.reshape(depth, MLA_KV_RANK, HEAD_W).astype(BF16))
    merge_ffn_weights = (
        w_o_fox.astype(BF16), w_o_mla.astype(BF16), w_o_sb.astype(BF16), w_out.astype(BF16),
        row_vec(g_ffn), w_ffn_gate.astype(BF16), w_ffn_up.astype(BF16), w_ffn_down.astype(BF16))

    for l in range(depth):
        fox_q, fox_k, fox_vt, sb_qk, sb_vt, gates, mq, mk, mvt = _inproj(
            x, mod[l], inproj_weights, tables, layer=l, tm=tm)
        y_fox = _softmax_attention_flat(fox_q, fox_k, fox_vt, t=ta, chunk_shift=0,
                                        name="fox_attention")
        y_mla = _softmax_attention_flat(mq, mk, mvt, t=ta, chunk_shift=CHUNK_SHIFT,
                                        name="mla_attention")
        y_sb = _sb_attention_flat(sb_qk, sb_vt, t=tsb)
        x = _merge_ffn(x, y_fox, y_mla, y_sb, gates, mod[l], merge_ffn_weights,
                       g_final.reshape(1, d), layer=l, tm=tm_ffn, final=(l == depth - 1))
    return x
```

```python
import functools

import numpy as np
import jax
import jax.numpy as jnp
from jax import lax
from jax.experimental import pallas as pl
from jax.experimental.pallas import tpu as pltpu

F32 = jnp.float32
BF16 = jnp.bfloat16

D_MODEL = 1024
EPS = 1e-6
HEAD_DIM = 64
N_HEADS = 8
HEAD_W = N_HEADS * HEAD_DIM
CHUNK_SHIFT = 6
MLA_Q_RANK = 384
MLA_KV_RANK = 256
MLA_NOPE = 64
MLA_ROPE = 32
ROPE_HALF = MLA_ROPE // 2
ROPE_BASE = 10000.0
MLA_SCALE = (MLA_NOPE + MLA_ROPE) ** -0.5
QK_SCALE = HEAD_DIM ** -0.5
D_FF = 2816
LANES = 128
SLOT_W = N_HEADS * LANES
N_PIECES = 3

C_FQ = 0
C_FK = 512
C_FV = 1024
C_SB = 1536
C_GATE = 3072
C_QL = 6144
C_KVL = 6528
C_KR = 6784
C_FF = 6912
IN_W = 7040

ROW_CHUNK = 32
PIPELINE_UNROLL = 2
SIGN_BIT = np.uint32(0x80000000)
UNDERFLOW = 105.0
NEG = -1e30
VMEM_LIMIT = 56 * 1024 * 1024


def _params(n_grid):
    return pltpu.CompilerParams(dimension_semantics=("arbitrary",) * n_grid,
                                vmem_limit_bytes=VMEM_LIMIT)


def _resident(shape):
    nd = len(shape)
    return pl.BlockSpec(shape, lambda *_: (0,) * nd, pipeline_mode=pl.Buffered(1))


def _resident_layer(shape, layer):
    nd = len(shape)
    return pl.BlockSpec((pl.Squeezed(),) + tuple(shape), lambda *_: (layer,) + (0,) * nd,
                        pipeline_mode=pl.Buffered(1))


def _bdot(a, b):
    return jnp.dot(a, b, preferred_element_type=F32)


def _dot_nt(a, b):
    return lax.dot_general(a, b, (((1,), (1,)), ((), ())), preferred_element_type=F32)


def _split3(v):
    hi = v.astype(BF16)
    r = v - hi.astype(F32)
    mid = r.astype(BF16)
    lo = (r - mid.astype(F32)).astype(BF16)
    return hi, mid, lo


def _log_sigmoid(z):
    return jnp.minimum(z, 0.0) - jnp.log1p(jnp.exp(-jnp.abs(z)))


def _sigmoid(z):
    return 1.0 / (1.0 + jnp.exp(-z))


def _rms(x):
    return x * lax.rsqrt(jnp.mean(x * x, axis=-1, keepdims=True) + EPS)


def _ada_kernel(c_ref, w_ref, b_ref, o_ref):
    c = c_ref[...]
    cond = c * _sigmoid(c)
    chi, cmid, clo = _split3(cond)
    whi, wmid, wlo = _split3(w_ref[0])
    acc = _bdot(chi, whi) + (_bdot(chi, wmid) + _bdot(cmid, whi))
    acc = acc + (_bdot(chi, wlo) + _bdot(clo, whi) + _bdot(cmid, wmid))
    o_ref[0] = acc + b_ref[0]


def _ada(c, w_ada, b_ada):
    depth, d, n = w_ada.shape
    b = c.shape[0]
    tn = 1024
    return pl.pallas_call(
        _ada_kernel,
        grid=(depth, n // tn),
        in_specs=[pl.BlockSpec((b, d), lambda l, j: (0, 0)),
                  pl.BlockSpec((1, d, tn), lambda l, j: (l, 0, j)),
                  pl.BlockSpec((1, 1, tn), lambda l, j: (l, 0, j))],
        out_specs=pl.BlockSpec((1, b, tn), lambda l, j: (l, 0, j)),
        out_shape=jax.ShapeDtypeStruct((depth, b, n), F32),
        compiler_params=_params(2),
        name="adaln",
    )(c, w_ada, b_ada.reshape(depth, 1, n))


def _rope_kernel(pos_ref, inv_ref, cos_ref, sina_ref, sinb_ref):
    ang = pos_ref[0].astype(F32) * inv_ref[...]
    lane = lax.broadcasted_iota(jnp.int32, (1, LANES), 1)
    sin = jnp.sin(ang)
    cos_ref[0] = jnp.cos(ang)
    lo, mid, hi = MLA_NOPE, MLA_NOPE + ROPE_HALF, MLA_NOPE + MLA_ROPE
    sina_ref[0] = jnp.where((lane >= lo) & (lane < mid), -sin, 0.0)
    sinb_ref[0] = jnp.where((lane >= mid) & (lane < hi), sin, 0.0)


def _rope_tables(positions):
    b, s = positions.shape
    ts = min(s, 512)
    inv = ROPE_BASE ** (-jnp.arange(ROPE_HALF, dtype=F32) / ROPE_HALF)
    inv_slot = jnp.concatenate([jnp.zeros((MLA_NOPE,), F32), inv, inv,
                                jnp.zeros((LANES - MLA_NOPE - MLA_ROPE,), F32)]).reshape(1, LANES)
    spec = pl.BlockSpec((1, ts, LANES), lambda i, j: (i, j, 0))
    return pl.pallas_call(
        _rope_kernel,
        grid=(b, s // ts),
        in_specs=[pl.BlockSpec((1, ts, 1), lambda i, j: (i, j, 0)),
                  pl.BlockSpec((1, LANES), lambda i, j: (0, 0))],
        out_specs=[spec, spec, spec],
        out_shape=[jax.ShapeDtypeStruct((b, s, LANES), F32)] * 3,
        compiler_params=_params(2),
        name="rope_tables",
    )(positions.reshape(b, s, 1), inv_slot)


def _rope_slot(x, cos, sina, sinb):
    return (x * cos + pltpu.roll(x, LANES - ROPE_HALF, 1) * sina
            + pltpu.roll(x, ROPE_HALF, 1) * sinb)


def _inproj_kernel(x_ref, mod_ref, g_ref, w_ref, bff_ref, gq_ref, gkv_ref, wuq_ref, wkn_ref,
                   wkv_ref, cos_ref, sina_ref, sinb_ref, place_ref, qc_ref, kc_ref,
                   fq_ref, fk_ref, fv_ref, sb_ref, sv_ref, gate_ref, mq_ref, mk_ref, mv_ref,
                   carry_ref, *, tm):
    x = x_ref[0]
    sh, sc = mod_ref[0, 0:1, :], mod_ref[0, 1:2, :]
    u = ((_rms(x) * g_ref[...]) * (1.0 + sc) + sh).astype(BF16)

    def seg(a, b):
        return _bdot(u, w_ref[:, a:b])

    @pl.when(pl.program_id(1) == 0)
    def _():
        carry_ref[...] = jnp.zeros_like(carry_ref)

    logf = _log_sigmoid(seg(C_FF, IN_W) + bff_ref[...])
    r = lax.broadcasted_iota(jnp.int32, (tm, tm), 0)
    c = lax.broadcasted_iota(jnp.int32, (tm, tm), 1)
    tri = jnp.where(r >= c, 1.0, 0.0).astype(BF16)
    lane = lax.broadcasted_iota(jnp.int32, (1, LANES), 1)

    def pack3(v):
        p0, p1, p2 = (p.astype(F32) for p in _split3(v))
        return jnp.where(lane < N_HEADS, p0,
                         jnp.where(lane < 2 * N_HEADS, pltpu.roll(p1, N_HEADS, 1),
                                   jnp.where(lane < 3 * N_HEADS, pltpu.roll(p2, 2 * N_HEADS, 1),
                                             0.0))).astype(BF16)

    part = _bdot(tri, pack3(logf))
    cum = ((part + pltpu.roll(part, LANES - N_HEADS, 1))
           + pltpu.roll(part, LANES - 2 * N_HEADS, 1)) + carry_ref[...]
    carry_ref[...] = jnp.where(lane < N_HEADS, cum[tm - 1:tm, :], 0.0)
    placed = _bdot(pack3(cum), place_ref[...])

    def to_slots(compact, extra, out_ref):
        for h in range(N_HEADS):
            x = compact[:, LANES * (h // 2):LANES * (h // 2 + 1)]
            head = pltpu.roll(x, HEAD_DIM, 1) if h % 2 else x
            slot = slice(LANES * h, LANES * (h + 1))
            out_ref[0, :, slot] = jnp.where(lane < HEAD_DIM, head, extra[:, slot]).astype(BF16)

    first_aug = (lax.broadcasted_iota(jnp.int32, (1, SLOT_W), 1) & (LANES - 1)) < HEAD_DIM + N_PIECES
    to_slots(seg(C_FQ, C_FK) * QK_SCALE, jnp.where(first_aug, qc_ref[...], placed), fq_ref)
    to_slots(seg(C_FK, C_FV), jnp.where(first_aug, placed, kc_ref[...]), fk_ref)
    fv_ref[0] = seg(C_FV, C_SB).astype(BF16).T

    sb_ref[0, :, 0:512] = (seg(C_SB, C_SB + 512) * QK_SCALE).astype(BF16)
    sb_ref[0, :, 512:1024] = seg(C_SB + 512, C_SB + 1024).astype(BF16)
    sv_ref[0] = seg(C_SB + 1024, C_GATE).astype(BF16).T
    for c in range(6):
        a = 512 * c
        gate_ref[0, :, a:a + 512] = _sigmoid(seg(C_GATE + a, C_GATE + a + 512)).astype(BF16)

    cos, sina, sinb = cos_ref[0], sina_ref[0], sinb_ref[0]
    cq = (_rms(seg(C_QL, C_KVL)) * gq_ref[...]).astype(BF16)
    ckv = (_rms(seg(C_KVL, C_KR)) * gkv_ref[...]).astype(BF16)
    kr = _rope_slot(seg(C_KR, C_FF), cos, sina, sinb)
    for h in range(N_HEADS):
        a = LANES * h
        q = _rope_slot(_bdot(cq, wuq_ref[:, a:a + LANES]), cos, sina, sinb)
        mq_ref[0, :, a:a + LANES] = (q * MLA_SCALE).astype(BF16)
        mk_ref[0, :, a:a + LANES] = (_bdot(ckv, wkn_ref[:, a:a + LANES]) + kr).astype(BF16)
    mv_ref[0] = _bdot(ckv, wkv_ref[...]).astype(BF16).T


def _fox_placement():
    place = np.zeros((LANES, SLOT_W), np.float32)
    qc = np.zeros((1, SLOT_W), np.float32)
    kc = np.zeros((1, SLOT_W), np.float32)
    for h in range(N_HEADS):
        for p in range(N_PIECES):
            place[N_HEADS * p + h, LANES * h + HEAD_DIM + p] = 1.0
            place[N_HEADS * p + h, LANES * h + HEAD_DIM + N_PIECES + p] = 1.0
            qc[0, LANES * h + HEAD_DIM + p] = -1.0
            kc[0, LANES * h + HEAD_DIM + N_PIECES + p] = 1.0
    return jnp.asarray(place, BF16), jnp.asarray(qc), jnp.asarray(kc)


def _inproj(x, mod, weights, tables, *, layer, tm):
    b, s, d = x.shape
    cos, sina, sinb = tables
    place, qc, kc = _fox_placement()
    per_layer = lambda *shape: _resident_layer(shape, layer)
    row = lambda w: pl.BlockSpec((1, tm, w), lambda i, j: (i, j, 0))
    col = lambda w: pl.BlockSpec((1, w, tm), lambda i, j: (i, 0, j))
    outs = [(SLOT_W, False),
            (SLOT_W, False),
            (HEAD_W, True),
            (2 * HEAD_W, False),
            (HEAD_W, True),
            (3 * D_MODEL, False),
            (SLOT_W, False),
            (SLOT_W, False),
            (HEAD_W, True)]
    out_specs = [col(w) if tr else row(w) for w, tr in outs]
    out_shape = [jax.ShapeDtypeStruct((b, w, s) if tr else (b, s, w), BF16) for w, tr in outs]
    return pl.pallas_call(
        functools.partial(_inproj_kernel, tm=tm),
        grid=(b, s // tm),
        in_specs=[row(d),
                  pl.BlockSpec((1, 6, d), lambda i, j: (i, 0, 0)),
                  per_layer(1, d), per_layer(d, IN_W), per_layer(1, LANES),
                  per_layer(1, MLA_Q_RANK), per_layer(1, MLA_KV_RANK),
                  per_layer(MLA_Q_RANK, SLOT_W), per_layer(MLA_KV_RANK, SLOT_W),
                  per_layer(MLA_KV_RANK, HEAD_W),
                  row(LANES), row(LANES), row(LANES),
                  _resident((LANES, SLOT_W)), _resident((1, SLOT_W)), _resident((1, SLOT_W))],
        out_specs=out_specs,
        out_shape=out_shape,
        scratch_shapes=[pltpu.VMEM((1, LANES), F32)],
        compiler_params=_params(2),
        name="inproj",
    )(x, mod, *weights, cos, sina, sinb, place, qc, kc)


def _head_rows():
    row = lax.broadcasted_iota(jnp.int32, (LANES, 1), 0)
    return [(row >= HEAD_DIM * hh) & (row < HEAD_DIM * (hh + 1)) for hh in range(2)]


def _head_masks():
    lane = lax.broadcasted_iota(jnp.int32, (1, LANES), 1)
    return [(lane >= HEAD_DIM * hh) & (lane < HEAD_DIM * (hh + 1)) for hh in range(2)]


def _colmax(x, chunk=64):
    rows = x.shape[0]
    if rows > chunk and rows % chunk == 0:
        acc = x[:chunk]
        for a in range(chunk, rows, chunk):
            acc = jnp.maximum(acc, x[a:a + chunk])
        x, rows = acc, chunk
    while rows > 8 and rows % 16 == 0:
        rows //= 2
        x = jnp.maximum(x[:rows], x[rows:])
    return jnp.max(x, axis=0, keepdims=True)


def _pipeline(n, stages, unroll=PIPELINE_UNROLL):
    depth = len(stages)

    def static_step(t):
        for s, stage in enumerate(stages):
            if 0 <= t - s < n:
                stage(t - s, (t - s) % 2)

    first_full = depth - 1
    n_full = max(n - depth + 1, 0)
    for t in range(min(first_full, n + depth - 1)):
        static_step(t)
    n_loops = n_full // unroll
    if n_loops >= 1:
        def body(it, carry):
            t = first_full + unroll * it
            for u in range(unroll):
                for s, stage in enumerate(stages):
                    stage(t + u - s, (first_full + u - s) % 2)
            return carry

        lax.fori_loop(0, n_loops, body, 0)
    for t in range(first_full + unroll * n_loops, n + depth - 1):
        static_step(t)


def _for_each_tile(n_tiles, fn, group=4):
    while n_tiles % group:
        group //= 2

    def body(i, carry):
        for u in range(group):
            fn(i * group + u)
        return carry

    lax.fori_loop(0, n_tiles // group, body, 0)


def _softmax_block_table(n_tiles, t):
    diag = [(i * t, i * t) for i in range(n_tiles)]
    below = [(i * t, j * t) for i in range(1, n_tiles) for j in range(i)]
    return jnp.asarray(np.array(diag + below, np.int32).T), len(diag), len(below)


def _softmax_flat_kernel(tab_ref, q_ref, k_ref, vt_ref, o_ref, acc_ref, m_ref, al_ref, top_ref,
                         s0_ref, s1_ref, p0_ref, p1_ref, *, t, n_diag, n_below, chunk_shift):
    own_rows = _head_rows()
    s_bufs, p_bufs = (s0_ref, s1_ref), (p0_ref, p1_ref)
    kio = lax.broadcasted_iota(jnp.int32, (t, 1), 0)
    qio = lax.broadcasted_iota(jnp.int32, (1, t), 1)
    acc_ref[...] = jnp.zeros_like(acc_ref)
    m_ref[...] = jnp.full_like(m_ref, NEG)

    def stages(base, diag):
        def item(i):
            return (pl.multiple_of(tab_ref[0, base + i], t), pl.multiple_of(tab_ref[1, base + i], t))

        def scores(i, par):
            q0, k0 = item(i)
            for hh in range(2):
                slot = slice(LANES * hh, LANES * (hh + 1))
                s = _dot_nt(k_ref[0, pl.ds(k0, t), slot],
                            q_ref[0, pl.ds(q0, t), slot])
                if diag:
                    s = jnp.where((kio >> chunk_shift) <= (qio >> chunk_shift), s, NEG)
                s_bufs[par][hh] = s
                top_ref[par, hh] = _colmax(s)

        def numerators(i, par):
            q0, _ = item(i)
            for hh in range(2):
                m_old = m_ref[hh, :, pl.ds(q0, t)]
                m_new = jnp.maximum(m_old, top_ref[par, hh])
                for a in range(0, t, ROW_CHUNK):
                    rows = slice(a, a + ROW_CHUNK)
                    p_bufs[par][hh, rows, :] = jnp.exp(s_bufs[par][hh, rows, :] - m_new).astype(BF16)
                al_ref[par, hh] = jnp.exp(m_old - m_new)
                m_ref[hh, :, pl.ds(q0, t)] = m_new

        def values(i, par):
            q0, k0 = item(i)
            vt = vt_ref[0, :, pl.ds(k0, t)]
            for hh in range(2):
                v1 = jnp.where(own_rows[hh], vt, jnp.ones_like(vt))
                acc_ref[hh, :, pl.ds(q0, t)] = (acc_ref[hh, :, pl.ds(q0, t)] * al_ref[par, hh]
                                                + _bdot(v1, p_bufs[par][hh]))

        return [scores, numerators, values]

    _pipeline(n_diag, stages(0, True))
    _pipeline(n_below, stages(n_diag, False))

    sub = lax.broadcasted_iota(jnp.int32, (LANES, 1), 0)

    def finish(tile):
        q0 = pl.multiple_of(tile * t, t)
        outs = []
        for hh in range(2):
            acc = acc_ref[hh, :, pl.ds(q0, t)]
            denom = acc[HEAD_DIM * (1 - hh):HEAD_DIM * (1 - hh) + 1, :]
            outs.append(acc / denom)
        o_ref[0, pl.ds(q0, t), :] = jnp.where(sub < HEAD_DIM, outs[0], outs[1]).astype(o_ref.dtype).T

    _for_each_tile(n_diag, finish)


def _softmax_attention_flat(q, k, vt, *, t, chunk_shift, name):
    b, s, _ = k.shape
    table, n_diag, n_below = _softmax_block_table(s // t, t)
    grid_spec = pltpu.PrefetchScalarGridSpec(
        num_scalar_prefetch=1,
        grid=(b, N_HEADS // 2),
        in_specs=[pl.BlockSpec((1, s, 2 * LANES), lambda i, h, tab: (i, 0, h)),
                  pl.BlockSpec((1, s, 2 * LANES), lambda i, h, tab: (i, 0, h)),
                  pl.BlockSpec((1, LANES, s), lambda i, h, tab: (i, h, 0))],
        out_specs=pl.BlockSpec((1, s, LANES), lambda i, h, tab: (i, 0, h)),
        scratch_shapes=[pltpu.VMEM((2, LANES, s), F32),
                        pltpu.VMEM((2, 1, s), F32),
                        pltpu.VMEM((2, 2, 1, t), F32),
                        pltpu.VMEM((2, 2, 1, t), F32),
                        pltpu.VMEM((2, t, t), F32), pltpu.VMEM((2, t, t), F32),
                        pltpu.VMEM((2, t, t), BF16), pltpu.VMEM((2, t, t), BF16)])
    return pl.pallas_call(
        functools.partial(_softmax_flat_kernel, t=t, n_diag=n_diag, n_below=n_below,
                          chunk_shift=chunk_shift),
        grid_spec=grid_spec,
        out_shape=jax.ShapeDtypeStruct((b, s, HEAD_W), BF16),
        compiler_params=_params(2),
        name=name,
    )(table, q, k, vt)


def _sb_block_table(n_tiles, t):
    diag = [(i * t, i * t, i * t, i) for i in range(n_tiles)]
    below = [(i * t, (i - d) * t, i * t, i) for d in range(1, n_tiles) for i in range(d, n_tiles)]
    idle = [(0, 0, n_tiles * t, n_tiles)]
    return jnp.asarray(np.array(diag + below + idle, np.int32).T), len(diag), len(below)


def _sb_flat_kernel(tab_ref, q_ref, k_ref, vt_ref, o_ref, acc_ref, r_ref, tot_ref, d_ref, done_ref,
                    z0_ref, z1_ref, e0_ref, e1_ref, w0_ref, w1_ref, *, t, n_diag, n_below):
    in_head = _head_masks()
    own_rows = _head_rows()
    z_bufs, e_bufs, w_bufs = (z0_ref, z1_ref), (e0_ref, e1_ref), (w0_ref, w1_ref)
    kio = lax.broadcasted_iota(jnp.int32, (t, 1), 0)
    qio = lax.broadcasted_iota(jnp.int32, (1, t), 1)
    ss = lax.broadcasted_iota(jnp.int32, (t, t), 0)
    jj = lax.broadcasted_iota(jnp.int32, (t, t), 1)
    later = jnp.where(jj > ss, 1.0, 0.0).astype(BF16)
    acc_ref[...] = jnp.zeros_like(acc_ref)
    r_ref[...] = jnp.zeros_like(r_ref)
    for tile in range(n_diag + 1):
        done_ref[tile] = 0

    def stages(base, diag):
        def item(i):
            return (pl.multiple_of(tab_ref[0, base + i], t), pl.multiple_of(tab_ref[1, base + i], t),
                    pl.multiple_of(tab_ref[2, base + i], t), tab_ref[3, base + i])

        def scores(i, par):
            q0, k0, _, _ = item(i)
            k = k_ref[0, pl.ds(k0, t), :]
            q = q_ref[0, pl.ds(q0, t), :]
            for hh in range(2):
                qh = jnp.where(in_head[hh], q, jnp.zeros_like(q))
                z_bufs[par][hh] = _dot_nt(k, qh)

        def drops(i, par):
            for hh in range(2):
                for a in range(0, t, ROW_CHUNK):
                    rows = slice(a, a + ROW_CHUNK)
                    z = z_bufs[par][hh, rows, :]
                    neg_abs = pltpu.bitcast(pltpu.bitcast(z, jnp.uint32) | SIGN_BIT, F32)
                    drop = jnp.maximum(z, 0.0) + jnp.log(1.0 + jnp.exp(neg_abs))
                    log_beta = z - drop
                    if diag:
                        drop = jnp.where(kio[rows] < qio, drop, 0.0)
                        log_beta = jnp.where(kio[rows] < qio, log_beta, NEG)
                    d_ref[hh, rows, :] = drop.astype(BF16)
                    e_bufs[par][hh, rows, :] = log_beta
                after = _bdot(later, d_ref[hh])
                for a in range(0, t, ROW_CHUNK):
                    rows = slice(a, a + ROW_CHUNK)
                    e_bufs[par][hh, rows, :] = e_bufs[par][hh, rows, :] - after[rows]
                tot_ref[par, hh] = after[0:1, :] + d_ref[hh, 0:1, :].astype(F32)

        def weights(i, par):
            _, _, q0, tile = item(i)
            slack = None
            for hh in range(2):
                r_old = r_ref[hh, :, pl.ds(q0, t)]
                for a in range(0, t, ROW_CHUNK):
                    rows = slice(a, a + ROW_CHUNK)
                    w_bufs[par][hh, rows, :] = jnp.exp(e_bufs[par][hh, rows, :] - r_old).astype(BF16)
                r_new = r_old + tot_ref[par, hh]
                r_ref[hh, :, pl.ds(q0, t)] = r_new
                slack = r_new if slack is None else jnp.minimum(slack, r_new)
            if not diag:
                done_ref[tile] = (jnp.min(slack) > UNDERFLOW).astype(jnp.int32)

        def values(i, par):
            _, k0, q0, _ = item(i)
            vt = vt_ref[0, :, pl.ds(k0, t)]
            zero = jnp.zeros_like(vt)
            lhs = jnp.concatenate([jnp.where(own_rows[hh], vt, zero) for hh in range(2)], axis=1)
            acc_ref[:, pl.ds(q0, t)] = (acc_ref[:, pl.ds(q0, t)]
                                        + _bdot(lhs, w_bufs[par][...].reshape(2 * t, t)))

        return [scores, drops, weights, values]

    _pipeline(n_diag, stages(0, True))

    scores, drops, weights, values = stages(n_diag, False)

    def next_live(c):
        return lax.while_loop(
            lambda c: (c < n_below) & (done_ref[tab_ref[3, n_diag + c]] != 0), lambda c: c + 1, c)

    def after(c):
        return next_live(jnp.minimum(c + 1, n_below))

    def two_steps(state):
        a, b, i1, i2, i3 = state
        scores(a, 0)
        drops(i1, 1)
        weights(i2, 0)
        values(i3, 1)
        scores(b, 1)
        drops(a, 0)
        weights(i1, 1)
        values(i2, 0)
        a_next = after(b)
        return a_next, after(a_next), b, a, i1

    idle = jnp.int32(n_below)
    first = next_live(jnp.int32(0))
    lax.while_loop(lambda st: (st[0] < n_below) | (st[2] < n_below) | (st[3] < n_below)
                   | (st[4] < n_below), two_steps, (first, after(first), idle, idle, idle))

    def finish(tile):
        q0 = pl.multiple_of(tile * t, t)
        o_ref[0, pl.ds(q0, t), :] = acc_ref[:, pl.ds(q0, t)].astype(o_ref.dtype).T

    _for_each_tile(n_diag, finish)


def _sb_attention_flat(qk, vt, *, t):
    b, s, _ = qk.shape
    table, n_diag, n_below = _sb_block_table(s // t, t)
    grid_spec = pltpu.PrefetchScalarGridSpec(
        num_scalar_prefetch=1,
        grid=(b, N_HEADS // 2),
        in_specs=[pl.BlockSpec((1, s, LANES), lambda i, h, tab: (i, 0, h)),
                  pl.BlockSpec((1, s, LANES), lambda i, h, tab: (i, 0, 4 + h)),
                  pl.BlockSpec((1, LANES, s), lambda i, h, tab: (i, h, 0))],
        out_specs=pl.BlockSpec((1, s, LANES), lambda i, h, tab: (i, 0, h)),
        scratch_shapes=[pltpu.VMEM((LANES, s + t), F32),
                        pltpu.VMEM((2, 1, s + t), F32),
                        pltpu.VMEM((2, 2, 1, t), F32),
                        pltpu.VMEM((2, t, t), BF16),
                        pltpu.SMEM((s // t + 1,), jnp.int32),
                        pltpu.VMEM((2, t, t), F32), pltpu.VMEM((2, t, t), F32),
                        pltpu.VMEM((2, t, t), F32), pltpu.VMEM((2, t, t), F32),
                        pltpu.VMEM((2, t, t), BF16), pltpu.VMEM((2, t, t), BF16)])
    return pl.pallas_call(
        functools.partial(_sb_flat_kernel, t=t, n_diag=n_diag, n_below=n_below),
        grid_spec=grid_spec,
        out_shape=jax.ShapeDtypeStruct((b, s, HEAD_W), BF16),
        compiler_params=_params(2),
        name="sb_attention",
    )(table, qk, qk, vt)


def _merge_ffn_kernel(x_ref, yf_ref, ym_ref, ys_ref, gate_ref, mod_ref, wf_ref, wm_ref, ws_ref,
                      wo_ref, g_ref, wg_ref, wu_ref, wd_ref, gfin_ref, o_ref, *, final, fc):
    d = D_MODEL
    merged = (gate_ref[0, :, 0:d].astype(F32) * _bdot(yf_ref[0], wf_ref[...])
              + gate_ref[0, :, d:2 * d].astype(F32) * _bdot(ym_ref[0], wm_ref[...])
              + gate_ref[0, :, 2 * d:3 * d].astype(F32) * _bdot(ys_ref[0], ws_ref[...]))
    x = x_ref[0] + mod_ref[0, 2:3, :] * _bdot(merged.astype(BF16), wo_ref[...])

    sh, sc, gt = mod_ref[0, 3:4, :], mod_ref[0, 4:5, :], mod_ref[0, 5:6, :]
    u = ((_rms(x) * g_ref[...]) * (1.0 + sc) + sh).astype(BF16)
    acc = None
    for a in range(0, D_FF, fc):
        hg = _bdot(u, wg_ref[:, a:a + fc])
        hu = _bdot(u, wu_ref[:, a:a + fc])
        h = ((hg * _sigmoid(hg)) * hu).astype(BF16)
        part = _bdot(h, wd_ref[a:a + fc, :])
        acc = part if acc is None else acc + part
    out = x + gt * acc
    if final:
        out = _rms(out) * gfin_ref[...]
    o_ref[0] = out


def _merge_ffn(x, y_fox, y_mla, y_sb, gates, mod, weights, g_final, *, layer, tm, final):
    b, s, d = x.shape
    row = lambda w: pl.BlockSpec((1, tm, w), lambda i, j: (i, j, 0))
    per_layer = lambda *shape: _resident_layer(shape, layer)
    return pl.pallas_call(
        functools.partial(_merge_ffn_kernel, final=final, fc=D_FF // 2),
        grid=(b, s // tm),
        in_specs=[row(d), row(HEAD_W), row(HEAD_W), row(HEAD_W), row(3 * d),
                  pl.BlockSpec((1, 6, d), lambda i, j: (i, 0, 0)),
                  per_layer(HEAD_W, d), per_layer(HEAD_W, d), per_layer(HEAD_W, d),
                  per_layer(d, d), per_layer(1, d),
                  per_layer(d, D_FF), per_layer(d, D_FF), per_layer(D_FF, d),
                  _resident((1, d))],
        out_specs=row(d),
        out_shape=jax.ShapeDtypeStruct((b, s, d), F32),
        compiler_params=_params(2),
        name="merge_ffn",
    )(x, y_fox, y_mla, y_sb, gates, mod, *weights, g_final)


def _slot_cols(w, width_per_head, used):
    lead = w.shape[:-1]
    w = w.reshape(lead + (N_HEADS, width_per_head))[..., :used]
    pad = [(0, 0)] * (w.ndim - 1) + [(0, LANES - used)]
    return jnp.pad(w, pad).reshape(lead + (SLOT_W,))


def _layout_w_in(w):
    fox, ff, ql, kvl, kr, sb, gate = jnp.split(
        w, [1536, 1544, 1928, 2184, 2216, 3752], axis=-1)
    z = lambda n: jnp.zeros(w.shape[:-1] + (n,), w.dtype)
    kr_slot = jnp.concatenate([z(MLA_NOPE), kr, z(LANES - MLA_NOPE - MLA_ROPE)], axis=-1)
    ff_slot = jnp.concatenate([ff, z(LANES - N_HEADS)], axis=-1)
    return jnp.concatenate([fox, sb, gate, ql, kvl, kr_slot, ff_slot], axis=-1).astype(BF16)


def kernel(x, c, positions, g_mix, w_ada, b_ada, w_in, b_fox_f, g_mla_q, w_mla_uq, g_mla_kv,
           w_mla_ukv, w_o_fox, w_o_mla, w_o_sb, w_out, g_ffn, w_ffn_gate, w_ffn_up, w_ffn_down,
           g_final):
    b, s, d = x.shape
    depth = w_in.shape[0]
    tm = min(s, 512)
    tm_ffn = min(s, 512)
    ta = min(s, 512)
    tsb = min(s, 256)

    mod = _ada(c, w_ada, b_ada).reshape(depth, b, 6, d)
    tables = _rope_tables(positions)

    row_vec = lambda v: v.reshape(depth, 1, -1)
    wkv = w_mla_ukv.reshape(depth, MLA_KV_RANK, N_HEADS, 2 * HEAD_DIM)[..., MLA_NOPE:]
    inproj_weights = (
        row_vec(g_mix), _layout_w_in(w_in),
        row_vec(jnp.pad(b_fox_f, ((0, 0), (0, LANES - N_HEADS)))),
        row_vec(g_mla_q), row_vec(g_mla_kv),
        _slot_cols(w_mla_uq, MLA_NOPE + MLA_ROPE, MLA_NOPE + MLA_ROPE).astype(BF16),
        _slot_cols(w_mla_ukv, 2 * HEAD_DIM, MLA_NOPE).astype(BF16),
        wkv.reshape(depth, MLA_KV_RANK, HEAD_W).astype(BF16))
    merge_ffn_weights = (
        w_o_fox.astype(BF16), w_o_mla.astype(BF16), w_o_sb.astype(BF16), w_out.astype(BF16),
        row_vec(g_ffn), w_ffn_gate.astype(BF16), w_ffn_up.astype(BF16), w_ffn_down.astype(BF16))

    for l in range(depth):
        fox_q, fox_k, fox_vt, sb_qk, sb_vt, gates, mq, mk, mvt = _inproj(
            x, mod[l], inproj_weights, tables, layer=l, tm=tm)
        y_fox = _softmax_attention_flat(fox_q, fox_k, fox_vt, t=ta, chunk_shift=0,
                                        name="fox_attention")
        y_mla = _softmax_attention_flat(mq, mk, mvt, t=ta, chunk_shift=CHUNK_SHIFT,
                                        name="mla_attention")
        y_sb = _sb_attention_flat(sb_qk, sb_vt, t=tsb)
        x = _merge_ffn(x, y_fox, y_mla, y_sb, gates, mod[l], merge_ffn_weights,
                       g_final.reshape(1, d), layer=l, tm=tm_ffn, final=(l == depth - 1))
    return x
```

```python
import functools

import numpy as np
import jax
import jax.numpy as jnp
from jax import lax
from jax.experimental import pallas as pl
from jax.experimental.pallas import tpu as pltpu

F32 = jnp.float32
BF16 = jnp.bfloat16

D_MODEL = 1024
EPS = 1e-6
HEAD_DIM = 64
N_HEADS = 8
HEAD_W = N_HEADS * HEAD_DIM
CHUNK_SHIFT = 6
MLA_Q_RANK = 384
MLA_KV_RANK = 256
MLA_NOPE = 64
MLA_ROPE = 32
ROPE_HALF = MLA_ROPE // 2
ROPE_BASE = 10000.0
MLA_SCALE = (MLA_NOPE + MLA_ROPE) ** -0.5
QK_SCALE = HEAD_DIM ** -0.5
D_FF = 2816
LANES = 128
SLOT_W = N_HEADS * LANES
N_PIECES = 3

C_FQ = 0
C_FK = 512
C_FV = 1024
C_SB = 1536
C_GATE = 3072
C_QL = 6144
C_KVL = 6528
C_KR = 6784
C_FF = 6912
IN_W = 7040

ROW_CHUNK = 32
PIPELINE_UNROLL = 2
SIGN_BIT = np.uint32(0x80000000)
UNDERFLOW = 105.0
NEG = -1e30
VMEM_LIMIT = 56 * 1024 * 1024


def _params(n_grid):
    return pltpu.CompilerParams(dimension_semantics=("arbitrary",) * n_grid,
                                vmem_limit_bytes=VMEM_LIMIT)


def _resident(shape):
    nd = len(shape)
    return pl.BlockSpec(shape, lambda *_: (0,) * nd, pipeline_mode=pl.Buffered(1))


def _resident_layer(shape, layer):
    nd = len(shape)
    return pl.BlockSpec((pl.Squeezed(),) + tuple(shape), lambda *_: (layer,) + (0,) * nd,
                        pipeline_mode=pl.Buffered(1))


def _bdot(a, b):
    return jnp.dot(a, b, preferred_element_type=F32)


def _dot_nt(a, b):
    return lax.dot_general(a, b, (((1,), (1,)), ((), ())), preferred_element_type=F32)


def _split3(v):
    hi = v.astype(BF16)
    r = v - hi.astype(F32)
    mid = r.astype(BF16)
    lo = (r - mid.astype(F32)).astype(BF16)
    return hi, mid, lo


def _log_sigmoid(z):
    return jnp.minimum(z, 0.0) - jnp.log1p(jnp.exp(-jnp.abs(z)))


def _sigmoid(z):
    return 1.0 / (1.0 + jnp.exp(-z))


def _rms(x):
    return x * lax.rsqrt(jnp.mean(x * x, axis=-1, keepdims=True) + EPS)


def _ada_kernel(c_ref, w_ref, b_ref, o_ref):
    c = c_ref[...]
    cond = c * _sigmoid(c)
    chi, cmid, clo = _split3(cond)
    whi, wmid, wlo = _split3(w_ref[0])
    acc = _bdot(chi, whi) + (_bdot(chi, wmid) + _bdot(cmid, whi))
    acc = acc + (_bdot(chi, wlo) + _bdot(clo, whi) + _bdot(cmid, wmid))
    o_ref[0] = acc + b_ref[0]


def _ada(c, w_ada, b_ada):
    depth, d, n = w_ada.shape
    b = c.shape[0]
    tn = 1024
    return pl.pallas_call(
        _ada_kernel,
        grid=(depth, n // tn),
        in_specs=[pl.BlockSpec((b, d), lambda l, j: (0, 0)),
                  pl.BlockSpec((1, d, tn), lambda l, j: (l, 0, j)),
                  pl.BlockSpec((1, 1, tn), lambda l, j: (l, 0, j))],
        out_specs=pl.BlockSpec((1, b, tn), lambda l, j: (l, 0, j)),
        out_shape=jax.ShapeDtypeStruct((depth, b, n), F32),
        compiler_params=_params(2),
        name="adaln",
    )(c, w_ada, b_ada.reshape(depth, 1, n))


def _rope_kernel(pos_ref, inv_ref, cos_ref, sina_ref, sinb_ref):
    ang = pos_ref[0].astype(F32) * inv_ref[...]
    lane = lax.broadcasted_iota(jnp.int32, (1, LANES), 1)
    sin = jnp.sin(ang)
    cos_ref[0] = jnp.cos(ang)
    lo, mid, hi = MLA_NOPE, MLA_NOPE + ROPE_HALF, MLA_NOPE + MLA_ROPE
    sina_ref[0] = jnp.where((lane >= lo) & (lane < mid), -sin, 0.0)
    sinb_ref[0] = jnp.where((lane >= mid) & (lane < hi), sin, 0.0)


def _rope_tables(positions):
    b, s = positions.shape
    ts = min(s, 512)
    inv = ROPE_BASE ** (-jnp.arange(ROPE_HALF, dtype=F32) / ROPE_HALF)
    inv_slot = jnp.concatenate([jnp.zeros((MLA_NOPE,), F32), inv, inv,
                                jnp.zeros((LANES - MLA_NOPE - MLA_ROPE,), F32)]).reshape(1, LANES)
    spec = pl.BlockSpec((1, ts, LANES), lambda i, j: (i, j, 0))
    return pl.pallas_call(
        _rope_kernel,
        grid=(b, s // ts),
        in_specs=[pl.BlockSpec((1, ts, 1), lambda i, j: (i, j, 0)),
                  pl.BlockSpec((1, LANES), lambda i, j: (0, 0))],
        out_specs=[spec, spec, spec],
        out_shape=[jax.ShapeDtypeStruct((b, s, LANES), F32)] * 3,
        compiler_params=_params(2),
        name="rope_tables",
    )(positions.reshape(b, s, 1), inv_slot)


def _rope_slot(x, cos, sina, sinb):
    return (x * cos + pltpu.roll(x, LANES - ROPE_HALF, 1) * sina
            + pltpu.roll(x, ROPE_HALF, 1) * sinb)


def _inproj_kernel(x_ref, mod_ref, g_ref, w_ref, bff_ref, gq_ref, gkv_ref, wuq_ref, wkn_ref,
                   wkv_ref, cos_ref, sina_ref, sinb_ref, place_ref, qc_ref, kc_ref,
                   fq_ref, fk_ref, fv_ref, sb_ref, sv_ref, gate_ref, mq_ref, mk_ref, mv_ref,
                   carry_ref, *, tm):
    x = x_ref[0]
    sh, sc = mod_ref[0, 0:1, :], mod_ref[0, 1:2, :]
    u = ((_rms(x) * g_ref[...]) * (1.0 + sc) + sh).astype(BF16)

    def seg(a, b):
        return _bdot(u, w_ref[:, a:b])

    @pl.when(pl.program_id(1) == 0)
    def _():
        carry_ref[...] = jnp.zeros_like(carry_ref)

    logf = _log_sigmoid(seg(C_FF, IN_W) + bff_ref[...])
    r = lax.broadcasted_iota(jnp.int32, (tm, tm), 0)
    c = lax.broadcasted_iota(jnp.int32, (tm, tm), 1)
    tri = jnp.where(r >= c, 1.0, 0.0).astype(BF16)
    lane = lax.broadcasted_iota(jnp.int32, (1, LANES), 1)

    def pack3(v):
        p0, p1, p2 = (p.astype(F32) for p in _split3(v))
        return jnp.where(lane < N_HEADS, p0,
                         jnp.where(lane < 2 * N_HEADS, pltpu.roll(p1, N_HEADS, 1),
                                   jnp.where(lane < 3 * N_HEADS, pltpu.roll(p2, 2 * N_HEADS, 1),
                                             0.0))).astype(BF16)

    part = _bdot(tri, pack3(logf))
    cum = ((part + pltpu.roll(part, LANES - N_HEADS, 1))
           + pltpu.roll(part, LANES - 2 * N_HEADS, 1)) + carry_ref[...]
    carry_ref[...] = jnp.where(lane < N_HEADS, cum[tm - 1:tm, :], 0.0)
    placed = _bdot(pack3(cum), place_ref[...])

    def to_slots(compact, extra, out_ref):
        for h in range(N_HEADS):
            x = compact[:, LANES * (h // 2):LANES * (h // 2 + 1)]
            head = pltpu.roll(x, HEAD_DIM, 1) if h % 2 else x
            slot = slice(LANES * h, LANES * (h + 1))
            out_ref[0, :, slot] = jnp.where(lane < HEAD_DIM, head, extra[:, slot]).astype(BF16)

    first_aug = (lax.broadcasted_iota(jnp.int32, (1, SLOT_W), 1) & (LANES - 1)) < HEAD_DIM + N_PIECES
    to_slots(seg(C_FQ, C_FK) * QK_SCALE, jnp.where(first_aug, qc_ref[...], placed), fq_ref)
    to_slots(seg(C_FK, C_FV), jnp.where(first_aug, placed, kc_ref[...]), fk_ref)
    fv_ref[0] = seg(C_FV, C_SB).astype(BF16).T

    sb_ref[0, :, 0:512] = (seg(C_SB, C_SB + 512) * QK_SCALE).astype(BF16)
    sb_ref[0, :, 512:1024] = seg(C_SB + 512, C_SB + 1024).astype(BF16)
    sv_ref[0] = seg(C_SB + 1024, C_GATE).astype(BF16).T
    for c in range(6):
        a = 512 * c
        gate_ref[0, :, a:a + 512] = _sigmoid(seg(C_GATE + a, C_GATE + a + 512)).astype(BF16)

    cos, sina, sinb = cos_ref[0], sina_ref[0], sinb_ref[0]
    cq = (_rms(seg(C_QL, C_KVL)) * gq_ref[...]).astype(BF16)
    ckv = (_rms(seg(C_KVL, C_KR)) * gkv_ref[...]).astype(BF16)
    kr = _rope_slot(seg(C_KR, C_FF), cos, sina, sinb)
    for h in range(N_HEADS):
        a = LANES * h
        q = _rope_slot(_bdot(cq, wuq_ref[:, a:a + LANES]), cos, sina, sinb)
        mq_ref[0, :, a:a + LANES] = (q * MLA_SCALE).astype(BF16)
        mk_ref[0, :, a:a + LANES] = (_bdot(ckv, wkn_ref[:, a:a + LANES]) + kr).astype(BF16)
    mv_ref[0] = _bdot(ckv, wkv_ref[...]).astype(BF16).T


def _fox_placement():
    place = np.zeros((LANES, SLOT_W), np.float32)
    qc = np.zeros((1, SLOT_W), np.float32)
    kc = np.zeros((1, SLOT_W), np.float32)
    for h in range(N_HEADS):
        for p in range(N_PIECES):
            place[N_HEADS * p + h, LANES * h + HEAD_DIM + p] = 1.0
            place[N_HEADS * p + h, LANES * h + HEAD_DIM + N_PIECES + p] = 1.0
            qc[0, LANES * h + HEAD_DIM + p] = -1.0
            kc[0, LANES * h + HEAD_DIM + N_PIECES + p] = 1.0
    return jnp.asarray(place, BF16), jnp.asarray(qc), jnp.asarray(kc)


def _inproj(x, mod, weights, tables, *, layer, tm):
    b, s, d = x.shape
    cos, sina, sinb = tables
    place, qc, kc = _fox_placement()
    per_layer = lambda *shape: _resident_layer(shape, layer)
    row = lambda w: pl.BlockSpec((1, tm, w), lambda i, j: (i, j, 0))
    col = lambda w: pl.BlockSpec((1, w, tm), lambda i, j: (i, 0, j))
    outs = [(SLOT_W, False),
            (SLOT_W, False),
            (HEAD_W, True),
            (2 * HEAD_W, False),
            (HEAD_W, True),
            (3 * D_MODEL, False),
            (SLOT_W, False),
            (SLOT_W, False),
            (HEAD_W, True)]
    out_specs = [col(w) if tr else row(w) for w, tr in outs]
    out_shape = [jax.ShapeDtypeStruct((b, w, s) if tr else (b, s, w), BF16) for w, tr in outs]
    return pl.pallas_call(
        functools.partial(_inproj_kernel, tm=tm),
        grid=(b, s // tm),
        in_specs=[row(d),
                  pl.BlockSpec((1, 6, d), lambda i, j: (i, 0, 0)),
                  per_layer(1, d), per_layer(d, IN_W), per_layer(1, LANES),
                  per_layer(1, MLA_Q_RANK), per_layer(1, MLA_KV_RANK),
                  per_layer(MLA_Q_RANK, SLOT_W), per_layer(MLA_KV_RANK, SLOT_W),
                  per_layer(MLA_KV_RANK, HEAD_W),
                  row(LANES), row(LANES), row(LANES),
                  _resident((LANES, SLOT_W)), _resident((1, SLOT_W)), _resident((1, SLOT_W))],
        out_specs=out_specs,
        out_shape=out_shape,
        scratch_shapes=[pltpu.VMEM((1, LANES), F32)],
        compiler_params=_params(2),
        name="inproj",
    )(x, mod, *weights, cos, sina, sinb, place, qc, kc)


def _head_rows():
    row = lax.broadcasted_iota(jnp.int32, (LANES, 1), 0)
    return [(row >= HEAD_DIM * hh) & (row < HEAD_DIM * (hh + 1)) for hh in range(2)]


def _head_masks():
    lane = lax.broadcasted_iota(jnp.int32, (1, LANES), 1)
    return [(lane >= HEAD_DIM * hh) & (lane < HEAD_DIM * (hh + 1)) for hh in range(2)]


def _colmax(x, chunk=64):
    rows = x.shape[0]
    if rows > chunk and rows % chunk == 0:
        acc = x[:chunk]
        for a in range(chunk, rows, chunk):
            acc = jnp.maximum(acc, x[a:a + chunk])
        x, rows = acc, chunk
    while rows > 8 and rows % 16 == 0:
        rows //= 2
        x = jnp.maximum(x[:rows], x[rows:])
    return jnp.max(x, axis=0, keepdims=True)


def _pipeline(n, stages, unroll=PIPELINE_UNROLL, n_static=0):
    depth = len(stages)

    def static_step(t):
        for s, stage in enumerate(stages):
            if 0 <= t - s < n:
                stage(t - s, (t - s) % 2)

    first_full = min(max(depth - 1, n_static), n + depth - 1)
    n_full = max(n - first_full, 0)
    for t in range(first_full):
        static_step(t)
    n_loops = n_full // unroll
    if n_loops >= 1:
        def body(it, carry):
            t = first_full + unroll * it
            for u in range(unroll):
                for s, stage in enumerate(stages):
                    stage(t + u - s, (first_full + u - s) % 2)
            return carry

        lax.fori_loop(0, n_loops, body, 0)
    for t in range(first_full + unroll * n_loops, n + depth - 1):
        static_step(t)


def _for_each_tile(n_tiles, fn, group=4):
    while n_tiles % group:
        group //= 2

    def body(i, carry):
        for u in range(group):
            fn(i * group + u)
        return carry

    lax.fori_loop(0, n_tiles // group, body, 0)


def _softmax_block_table(n_tiles, t):
    diag = [(i * t, i * t) for i in range(n_tiles)]
    below = [(i * t, j * t) for i in range(1, n_tiles) for j in range(i)]
    return jnp.asarray(np.array(diag + below, np.int32).T), len(diag), len(below)


def _softmax_flat_kernel(tab_ref, q_ref, k_ref, vt_ref, o_ref, acc_ref, m_ref, al_ref, top_ref,
                         s0_ref, s1_ref, p0_ref, p1_ref, *, t, n_diag, n_below, chunk_shift):
    own_rows = _head_rows()
    s_bufs, p_bufs = (s0_ref, s1_ref), (p0_ref, p1_ref)
    kio = lax.broadcasted_iota(jnp.int32, (t, 1), 0)
    qio = lax.broadcasted_iota(jnp.int32, (1, t), 1)
    acc_ref[...] = jnp.zeros_like(acc_ref)
    m_ref[...] = jnp.full_like(m_ref, NEG)

    def item(i):
        return pl.multiple_of(tab_ref[0, i], t), pl.multiple_of(tab_ref[1, i], t)

    def scores(i, par):
        q0, k0 = item(i)
        for hh in range(2):
            slot = slice(LANES * hh, LANES * (hh + 1))
            s = _dot_nt(k_ref[0, pl.ds(k0, t), slot], q_ref[0, pl.ds(q0, t), slot])
            if isinstance(i, int) and i < n_diag:
                s = jnp.where((kio >> chunk_shift) <= (qio >> chunk_shift), s, NEG)
            s_bufs[par][hh] = s
            top_ref[par, hh] = _colmax(s)

    def numerators(i, par):
        q0, _ = item(i)
        for hh in range(2):
            m_old = m_ref[hh, :, pl.ds(q0, t)]
            m_new = jnp.maximum(m_old, top_ref[par, hh])
            for a in range(0, t, ROW_CHUNK):
                rows = slice(a, a + ROW_CHUNK)
                p_bufs[par][hh, rows, :] = jnp.exp(s_bufs[par][hh, rows, :] - m_new).astype(BF16)
            al_ref[par, hh] = jnp.exp(m_old - m_new)
            m_ref[hh, :, pl.ds(q0, t)] = m_new

    def values(i, par):
        q0, k0 = item(i)
        vt = vt_ref[0, :, pl.ds(k0, t)]
        for hh in range(2):
            v1 = jnp.where(own_rows[hh], vt, jnp.ones_like(vt))
            acc_ref[hh, :, pl.ds(q0, t)] = (acc_ref[hh, :, pl.ds(q0, t)] * al_ref[par, hh]
                                            + _bdot(v1, p_bufs[par][hh]))

    _pipeline(n_diag + n_below, [scores, numerators, values], n_static=n_diag)

    sub = lax.broadcasted_iota(jnp.int32, (LANES, 1), 0)

    def finish(tile):
        q0 = pl.multiple_of(tile * t, t)
        outs = []
        for hh in range(2):
            acc = acc_ref[hh, :, pl.ds(q0, t)]
            denom = acc[HEAD_DIM * (1 - hh):HEAD_DIM * (1 - hh) + 1, :]
            outs.append(acc / denom)
        o_ref[0, pl.ds(q0, t), :] = jnp.where(sub < HEAD_DIM, outs[0], outs[1]).astype(o_ref.dtype).T

    _for_each_tile(n_diag, finish)


def _softmax_attention_flat(q, k, vt, *, t, chunk_shift, name):
    b, s, _ = k.shape
    table, n_diag, n_below = _softmax_block_table(s // t, t)
    grid_spec = pltpu.PrefetchScalarGridSpec(
        num_scalar_prefetch=1,
        grid=(b, N_HEADS // 2),
        in_specs=[pl.BlockSpec((1, s, 2 * LANES), lambda i, h, tab: (i, 0, h)),
                  pl.BlockSpec((1, s, 2 * LANES), lambda i, h, tab: (i, 0, h)),
                  pl.BlockSpec((1, LANES, s), lambda i, h, tab: (i, h, 0))],
        out_specs=pl.BlockSpec((1, s, LANES), lambda i, h, tab: (i, 0, h)),
        scratch_shapes=[pltpu.VMEM((2, LANES, s), F32),
                        pltpu.VMEM((2, 1, s), F32),
                        pltpu.VMEM((2, 2, 1, t), F32),
                        pltpu.VMEM((2, 2, 1, t), F32),
                        pltpu.VMEM((2, t, t), F32), pltpu.VMEM((2, t, t), F32),
                        pltpu.VMEM((2, t, t), BF16), pltpu.VMEM((2, t, t), BF16)])
    return pl.pallas_call(
        functools.partial(_softmax_flat_kernel, t=t, n_diag=n_diag, n_below=n_below,
                          chunk_shift=chunk_shift),
        grid_spec=grid_spec,
        out_shape=jax.ShapeDtypeStruct((b, s, HEAD_W), BF16),
        compiler_params=_params(2),
        name=name,
    )(table, q, k, vt)


def _sb_block_table(n_tiles, t):
    diag = [(i * t, i * t, i * t, i) for i in range(n_tiles)]
    below = [(i * t, (i - d) * t, i * t, i) for d in range(1, n_tiles) for i in range(d, n_tiles)]
    idle = [(0, 0, n_tiles * t, n_tiles)]
    return jnp.asarray(np.array(diag + below + idle, np.int32).T), len(diag), len(below)


def _sb_flat_kernel(tab_ref, q_ref, k_ref, vt_ref, o_ref, acc_ref, r_ref, tot_ref, d_ref, done_ref,
                    z0_ref, z1_ref, e0_ref, e1_ref, w0_ref, w1_ref, *, t, n_diag, n_below):
    in_head = _head_masks()
    own_rows = _head_rows()
    z_bufs, e_bufs, w_bufs = (z0_ref, z1_ref), (e0_ref, e1_ref), (w0_ref, w1_ref)
    kio = lax.broadcasted_iota(jnp.int32, (t, 1), 0)
    qio = lax.broadcasted_iota(jnp.int32, (1, t), 1)
    ss = lax.broadcasted_iota(jnp.int32, (t, t), 0)
    jj = lax.broadcasted_iota(jnp.int32, (t, t), 1)
    later = jnp.where(jj > ss, 1.0, 0.0).astype(BF16)
    acc_ref[...] = jnp.zeros_like(acc_ref)
    r_ref[...] = jnp.zeros_like(r_ref)
    for tile in range(n_diag + 1):
        done_ref[tile] = 0

    def stages(base, diag):
        def item(i):
            return (pl.multiple_of(tab_ref[0, base + i], t), pl.multiple_of(tab_ref[1, base + i], t),
                    pl.multiple_of(tab_ref[2, base + i], t), tab_ref[3, base + i])

        def scores(i, par):
            q0, k0, _, _ = item(i)
            k = k_ref[0, pl.ds(k0, t), :]
            q = q_ref[0, pl.ds(q0, t), :]
            for hh in range(2):
                qh = jnp.where(in_head[hh], q, jnp.zeros_like(q))
                z_bufs[par][hh] = _dot_nt(k, qh)

        def drops(i, par):
            for hh in range(2):
                for a in range(0, t, ROW_CHUNK):
                    rows = slice(a, a + ROW_CHUNK)
                    z = z_bufs[par][hh, rows, :]
                    neg_abs = pltpu.bitcast(pltpu.bitcast(z, jnp.uint32) | SIGN_BIT, F32)
                    drop = jnp.maximum(z, 0.0) + jnp.log(1.0 + jnp.exp(neg_abs))
                    log_beta = z - drop
                    if diag:
                        drop = jnp.where(kio[rows] < qio, drop, 0.0)
                        log_beta = jnp.where(kio[rows] < qio, log_beta, NEG)
                    d_ref[hh, rows, :] = drop.astype(BF16)
                    e_bufs[par][hh, rows, :] = log_beta
                after = _bdot(later, d_ref[hh])
                for a in range(0, t, ROW_CHUNK):
                    rows = slice(a, a + ROW_CHUNK)
                    e_bufs[par][hh, rows, :] = e_bufs[par][hh, rows, :] - after[rows]
                tot_ref[par, hh] = after[0:1, :] + d_ref[hh, 0:1, :].astype(F32)

        def weights(i, par):
            _, _, q0, tile = item(i)
            slack = None
            for hh in range(2):
                r_old = r_ref[hh, :, pl.ds(q0, t)]
                for a in range(0, t, ROW_CHUNK):
                    rows = slice(a, a + ROW_CHUNK)
                    w_bufs[par][hh, rows, :] = jnp.exp(e_bufs[par][hh, rows, :] - r_old).astype(BF16)
                r_new = r_old + tot_ref[par, hh]
                r_ref[hh, :, pl.ds(q0, t)] = r_new
                slack = r_new if slack is None else jnp.minimum(slack, r_new)
            if not diag:
                done_ref[tile] = (jnp.min(slack) > UNDERFLOW).astype(jnp.int32)

        def values(i, par):
            _, k0, q0, _ = item(i)
            vt = vt_ref[0, :, pl.ds(k0, t)]
            zero = jnp.zeros_like(vt)
            lhs = jnp.concatenate([jnp.where(own_rows[hh], vt, zero) for hh in range(2)], axis=1)
            acc_ref[:, pl.ds(q0, t)] = (acc_ref[:, pl.ds(q0, t)]
                                        + _bdot(lhs, w_bufs[par][...].reshape(2 * t, t)))

        return [scores, drops, weights, values]

    _pipeline(n_diag, stages(0, True))

    scores, drops, weights, values = stages(n_diag, False)

    def next_live(c):
        return lax.while_loop(
            lambda c: (c < n_below) & (done_ref[tab_ref[3, n_diag + c]] != 0), lambda c: c + 1, c)

    def after(c):
        return next_live(jnp.minimum(c + 1, n_below))

    def two_steps(state):
        a, b, i1, i2, i3 = state
        scores(a, 0)
        drops(i1, 1)
        weights(i2, 0)
        values(i3, 1)
        scores(b, 1)
        drops(a, 0)
        weights(i1, 1)
        values(i2, 0)
        a_next = after(b)
        return a_next, after(a_next), b, a, i1

    idle = jnp.int32(n_below)
    first = next_live(jnp.int32(0))
    lax.while_loop(lambda st: (st[0] < n_below) | (st[2] < n_below) | (st[3] < n_below)
                   | (st[4] < n_below), two_steps, (first, after(first), idle, idle, idle))

    def finish(tile):
        q0 = pl.multiple_of(tile * t, t)
        o_ref[0, pl.ds(q0, t), :] = acc_ref[:, pl.ds(q0, t)].astype(o_ref.dtype).T

    _for_each_tile(n_diag, finish)


def _sb_attention_flat(qk, vt, *, t):
    b, s, _ = qk.shape
    table, n_diag, n_below = _sb_block_table(s // t, t)
    grid_spec = pltpu.PrefetchScalarGridSpec(
        num_scalar_prefetch=1,
        grid=(b, N_HEADS // 2),
        in_specs=[pl.BlockSpec((1, s, LANES), lambda i, h, tab: (i, 0, h)),
                  pl.BlockSpec((1, s, LANES), lambda i, h, tab: (i, 0, 4 + h)),
                  pl.BlockSpec((1, LANES, s), lambda i, h, tab: (i, h, 0))],
        out_specs=pl.BlockSpec((1, s, LANES), lambda i, h, tab: (i, 0, h)),
        scratch_shapes=[pltpu.VMEM((LANES, s + t), F32),
                        pltpu.VMEM((2, 1, s + t), F32),
                        pltpu.VMEM((2, 2, 1, t), F32),
                        pltpu.VMEM((2, t, t), BF16),
                        pltpu.SMEM((s // t + 1,), jnp.int32),
                        pltpu.VMEM((2, t, t), F32), pltpu.VMEM((2, t, t), F32),
                        pltpu.VMEM((2, t, t), F32), pltpu.VMEM((2, t, t), F32),
                        pltpu.VMEM((2, t, t), BF16), pltpu.VMEM((2, t, t), BF16)])
    return pl.pallas_call(
        functools.partial(_sb_flat_kernel, t=t, n_diag=n_diag, n_below=n_below),
        grid_spec=grid_spec,
        out_shape=jax.ShapeDtypeStruct((b, s, HEAD_W), BF16),
        compiler_params=_params(2),
        name="sb_attention",
    )(table, qk, qk, vt)


def _merge_ffn_kernel(x_ref, yf_ref, ym_ref, ys_ref, gate_ref, mod_ref, wf_ref, wm_ref, ws_ref,
                      wo_ref, g_ref, wg_ref, wu_ref, wd_ref, gfin_ref, o_ref, *, final, fc):
    d = D_MODEL
    merged = (gate_ref[0, :, 0:d].astype(F32) * _bdot(yf_ref[0], wf_ref[...])
              + gate_ref[0, :, d:2 * d].astype(F32) * _bdot(ym_ref[0], wm_ref[...])
              + gate_ref[0, :, 2 * d:3 * d].astype(F32) * _bdot(ys_ref[0], ws_ref[...]))
    x = x_ref[0] + mod_ref[0, 2:3, :] * _bdot(merged.astype(BF16), wo_ref[...])

    sh, sc, gt = mod_ref[0, 3:4, :], mod_ref[0, 4:5, :], mod_ref[0, 5:6, :]
    u = ((_rms(x) * g_ref[...]) * (1.0 + sc) + sh).astype(BF16)
    acc = None
    for a in range(0, D_FF, fc):
        hg = _bdot(u, wg_ref[:, a:a + fc])
        hu = _bdot(u, wu_ref[:, a:a + fc])
        h = ((hg * _sigmoid(hg)) * hu).astype(BF16)
        part = _bdot(h, wd_ref[a:a + fc, :])
        acc = part if acc is None else acc + part
    out = x + gt * acc
    if final:
        out = _rms(out) * gfin_ref[...]
    o_ref[0] = out


def _merge_ffn(x, y_fox, y_mla, y_sb, gates, mod, weights, g_final, *, layer, tm, final):
    b, s, d = x.shape
    row = lambda w: pl.BlockSpec((1, tm, w), lambda i, j: (i, j, 0))
    per_layer = lambda *shape: _resident_layer(shape, layer)
    return pl.pallas_call(
        functools.partial(_merge_ffn_kernel, final=final, fc=D_FF // 2),
        grid=(b, s // tm),
        in_specs=[row(d), row(HEAD_W), row(HEAD_W), row(HEAD_W), row(3 * d),
                  pl.BlockSpec((1, 6, d), lambda i, j: (i, 0, 0)),
                  per_layer(HEAD_W, d), per_layer(HEAD_W, d), per_layer(HEAD_W, d),
                  per_layer(d, d), per_layer(1, d),
                  per_layer(d, D_FF), per_layer(d, D_FF), per_layer(D_FF, d),
                  _resident((1, d))],
        out_specs=row(d),
        out_shape=jax.ShapeDtypeStruct((b, s, d), F32),
        compiler_params=_params(2),
        name="merge_ffn",
    )(x, y_fox, y_mla, y_sb, gates, mod, *weights, g_final)


def _slot_cols(w, width_per_head, used):
    lead = w.shape[:-1]
    w = w.reshape(lead + (N_HEADS, width_per_head))[..., :used]
    pad = [(0, 0)] * (w.ndim - 1) + [(0, LANES - used)]
    return jnp.pad(w, pad).reshape(lead + (SLOT_W,))


def _layout_w_in(w):
    fox, ff, ql, kvl, kr, sb, gate = jnp.split(
        w, [1536, 1544, 1928, 2184, 2216, 3752], axis=-1)
    z = lambda n: jnp.zeros(w.shape[:-1] + (n,), w.dtype)
    kr_slot = jnp.concatenate([z(MLA_NOPE), kr, z(LANES - MLA_NOPE - MLA_ROPE)], axis=-1)
    ff_slot = jnp.concatenate([ff, z(LANES - N_HEADS)], axis=-1)
    return jnp.concatenate([fox, sb, gate, ql, kvl, kr_slot, ff_slot], axis=-1).astype(BF16)


def kernel(x, c, positions, g_mix, w_ada, b_ada, w_in, b_fox_f, g_mla_q, w_mla_uq, g_mla_kv,
           w_mla_ukv, w_o_fox, w_o_mla, w_o_sb, w_out, g_ffn, w_ffn_gate, w_ffn_up, w_ffn_down,
           g_final):
    b, s, d = x.shape
    depth = w_in.shape[0]
    tm = min(s, 512)
    tm_ffn = min(s, 512)
    ta = min(s, 512)
    tsb = min(s, 256)

    mod = _ada(c, w_ada, b_ada).reshape(depth, b, 6, d)
    tables = _rope_tables(positions)

    row_vec = lambda v: v.reshape(depth, 1, -1)
    wkv = w_mla_ukv.reshape(depth, MLA_KV_RANK, N_HEADS, 2 * HEAD_DIM)[..., MLA_NOPE:]
    inproj_weights = (
        row_vec(g_mix), _layout_w_in(w_in),
        row_vec(jnp.pad(b_fox_f, ((0, 0), (0, LANES - N_HEADS)))),
        row_vec(g_mla_q), row_vec(g_mla_kv),
        _slot_cols(w_mla_uq, MLA_NOPE + MLA_ROPE, MLA_NOPE + MLA_ROPE).astype(BF16),
        _slot_cols(w_mla_ukv, 2 * HEAD_DIM, MLA_NOPE).astype(BF16),
        wkv.reshape(depth, MLA_KV_RANK, HEAD_W).astype(BF16))
    merge_ffn_weights = (
        w_o_fox.astype(BF16), w_o_mla.astype(BF16), w_o_sb.astype(BF16), w_out.astype(BF16),
        row_vec(g_ffn), w_ffn_gate.astype(BF16), w_ffn_up.astype(BF16), w_ffn_down.astype(BF16))

    for l in range(depth):
        fox_q, fox_k, fox_vt, sb_qk, sb_vt, gates, mq, mk, mvt = _inproj(
            x, mod[l], inproj_weights, tables, layer=l, tm=tm)
        y_fox = _softmax_attention_flat(fox_q, fox_k, fox_vt, t=ta, chunk_shift=0,
                                        name="fox_attention")
        y_mla = _softmax_attention_flat(mq, mk, mvt, t=ta, chunk_shift=CHUNK_SHIFT,
                                        name="mla_attention")
        y_sb = _sb_attention_flat(sb_qk, sb_vt, t=tsb)
        x = _merge_ffn(x, y_fox, y_mla, y_sb, gates, mod[l], merge_ffn_weights,
                       g_final.reshape(1, d), layer=l, tm=tm_ffn, final=(l == depth - 1))
    return x
```

```python
import functools

import numpy as np
import jax
import jax.numpy as jnp
from jax import lax
from jax.experimental import pallas as pl
from jax.experimental.pallas import tpu as pltpu

F32 = jnp.float32
BF16 = jnp.bfloat16

D_MODEL = 1024
EPS = 1e-6
HEAD_DIM = 64
N_HEADS = 8
HEAD_W = N_HEADS * HEAD_DIM
CHUNK_SHIFT = 6
MLA_Q_RANK = 384
MLA_KV_RANK = 256
MLA_NOPE = 64
MLA_ROPE = 32
ROPE_HALF = MLA_ROPE // 2
ROPE_BASE = 10000.0
MLA_SCALE = (MLA_NOPE + MLA_ROPE) ** -0.5
QK_SCALE = HEAD_DIM ** -0.5
D_FF = 2816
LANES = 128
SLOT_W = N_HEADS * LANES
N_PIECES = 3

C_FQ = 0
C_FK = 512
C_FV = 1024
C_SB = 1536
C_GATE = 3072
C_QL = 6144
C_KVL = 6528
C_KR = 6784
C_FF = 6912
IN_W = 7040

ROW_CHUNK = 32
PIPELINE_UNROLL = 2
SIGN_BIT = np.uint32(0x80000000)
UNDERFLOW = 105.0
NEG = -1e30
VMEM_LIMIT = 56 * 1024 * 1024


def _params(n_grid):
    return pltpu.CompilerParams(dimension_semantics=("arbitrary",) * n_grid,
                                vmem_limit_bytes=VMEM_LIMIT)


def _resident(shape):
    nd = len(shape)
    return pl.BlockSpec(shape, lambda *_: (0,) * nd, pipeline_mode=pl.Buffered(1))


def _resident_layer(shape, layer):
    nd = len(shape)
    return pl.BlockSpec((pl.Squeezed(),) + tuple(shape), lambda *_: (layer,) + (0,) * nd,
                        pipeline_mode=pl.Buffered(1))


def _bdot(a, b):
    return jnp.dot(a, b, preferred_element_type=F32)


def _dot_nt(a, b):
    return lax.dot_general(a, b, (((1,), (1,)), ((), ())), preferred_element_type=F32)


def _split3(v):
    hi = v.astype(BF16)
    r = v - hi.astype(F32)
    mid = r.astype(BF16)
    lo = (r - mid.astype(F32)).astype(BF16)
    return hi, mid, lo


def _log_sigmoid(z):
    return jnp.minimum(z, 0.0) - jnp.log1p(jnp.exp(-jnp.abs(z)))


def _sigmoid(z):
    return 1.0 / (1.0 + jnp.exp(-z))


def _rms(x):
    return x * lax.rsqrt(jnp.mean(x * x, axis=-1, keepdims=True) + EPS)


def _ada_kernel(c_ref, w_ref, b_ref, o_ref):
    c = c_ref[...]
    cond = c * _sigmoid(c)
    chi, cmid, clo = _split3(cond)
    whi, wmid, wlo = _split3(w_ref[0])
    acc = _bdot(chi, whi) + (_bdot(chi, wmid) + _bdot(cmid, whi))
    acc = acc + (_bdot(chi, wlo) + _bdot(clo, whi) + _bdot(cmid, wmid))
    o_ref[0] = acc + b_ref[0]


def _ada(c, w_ada, b_ada):
    depth, d, n = w_ada.shape
    b = c.shape[0]
    tn = 1024
    return pl.pallas_call(
        _ada_kernel,
        grid=(depth, n // tn),
        in_specs=[pl.BlockSpec((b, d), lambda l, j: (0, 0)),
                  pl.BlockSpec((1, d, tn), lambda l, j: (l, 0, j)),
                  pl.BlockSpec((1, 1, tn), lambda l, j: (l, 0, j))],
        out_specs=pl.BlockSpec((1, b, tn), lambda l, j: (l, 0, j)),
        out_shape=jax.ShapeDtypeStruct((depth, b, n), F32),
        compiler_params=_params(2),
        name="adaln",
    )(c, w_ada, b_ada.reshape(depth, 1, n))


def _rope_kernel(pos_ref, inv_ref, cos_ref, sina_ref, sinb_ref):
    ang = pos_ref[0].astype(F32) * inv_ref[...]
    lane = lax.broadcasted_iota(jnp.int32, (1, LANES), 1)
    sin = jnp.sin(ang)
    cos_ref[0] = jnp.cos(ang)
    lo, mid, hi = MLA_NOPE, MLA_NOPE + ROPE_HALF, MLA_NOPE + MLA_ROPE
    sina_ref[0] = jnp.where((lane >= lo) & (lane < mid), -sin, 0.0)
    sinb_ref[0] = jnp.where((lane >= mid) & (lane < hi), sin, 0.0)


def _rope_tables(positions):
    b, s = positions.shape
    ts = min(s, 512)
    inv = ROPE_BASE ** (-jnp.arange(ROPE_HALF, dtype=F32) / ROPE_HALF)
    inv_slot = jnp.concatenate([jnp.zeros((MLA_NOPE,), F32), inv, inv,
                                jnp.zeros((LANES - MLA_NOPE - MLA_ROPE,), F32)]).reshape(1, LANES)
    spec = pl.BlockSpec((1, ts, LANES), lambda i, j: (i, j, 0))
    return pl.pallas_call(
        _rope_kernel,
        grid=(b, s // ts),
        in_specs=[pl.BlockSpec((1, ts, 1), lambda i, j: (i, j, 0)),
                  pl.BlockSpec((1, LANES), lambda i, j: (0, 0))],
        out_specs=[spec, spec, spec],
        out_shape=[jax.ShapeDtypeStruct((b, s, LANES), F32)] * 3,
        compiler_params=_params(2),
        name="rope_tables",
    )(positions.reshape(b, s, 1), inv_slot)


def _rope_slot(x, cos, sina, sinb):
    return (x * cos + pltpu.roll(x, LANES - ROPE_HALF, 1) * sina
            + pltpu.roll(x, ROPE_HALF, 1) * sinb)


def _inproj_kernel(x_ref, mod_ref, g_ref, w_ref, bff_ref, gq_ref, gkv_ref, wuq_ref, wkn_ref,
                   wkv_ref, cos_ref, sina_ref, sinb_ref, place_ref, qc_ref, kc_ref,
                   fq_ref, fk_ref, fv_ref, sb_ref, sv_ref, gate_ref, mq_ref, mk_ref, mv_ref,
                   carry_ref, *, tm):
    x = x_ref[0]
    sh, sc = mod_ref[0, 0:1, :], mod_ref[0, 1:2, :]
    u = ((_rms(x) * g_ref[...]) * (1.0 + sc) + sh).astype(BF16)

    def seg(a, b):
        return _bdot(u, w_ref[:, a:b])

    @pl.when(pl.program_id(1) == 0)
    def _():
        carry_ref[...] = jnp.zeros_like(carry_ref)

    logf = _log_sigmoid(seg(C_FF, IN_W) + bff_ref[...])
    r = lax.broadcasted_iota(jnp.int32, (tm, tm), 0)
    c = lax.broadcasted_iota(jnp.int32, (tm, tm), 1)
    tri = jnp.where(r >= c, 1.0, 0.0).astype(BF16)
    lane = lax.broadcasted_iota(jnp.int32, (1, LANES), 1)

    def pack3(v):
        p0, p1, p2 = (p.astype(F32) for p in _split3(v))
        return jnp.where(lane < N_HEADS, p0,
                         jnp.where(lane < 2 * N_HEADS, pltpu.roll(p1, N_HEADS, 1),
                                   jnp.where(lane < 3 * N_HEADS, pltpu.roll(p2, 2 * N_HEADS, 1),
                                             0.0))).astype(BF16)

    part = _bdot(tri, pack3(logf))
    cum = ((part + pltpu.roll(part, LANES - N_HEADS, 1))
           + pltpu.roll(part, LANES - 2 * N_HEADS, 1)) + carry_ref[...]
    carry_ref[...] = jnp.where(lane < N_HEADS, cum[tm - 1:tm, :], 0.0)
    placed = _bdot(pack3(cum), place_ref[...])

    def to_slots(compact, extra, out_ref):
        for h in range(N_HEADS):
            x = compact[:, LANES * (h // 2):LANES * (h // 2 + 1)]
            head = pltpu.roll(x, HEAD_DIM, 1) if h % 2 else x
            slot = slice(LANES * h, LANES * (h + 1))
            out_ref[0, :, slot] = jnp.where(lane < HEAD_DIM, head, extra[:, slot]).astype(BF16)

    first_aug = (lax.broadcasted_iota(jnp.int32, (1, SLOT_W), 1) & (LANES - 1)) < HEAD_DIM + N_PIECES
    to_slots(seg(C_FQ, C_FK) * QK_SCALE, jnp.where(first_aug, qc_ref[...], placed), fq_ref)
    to_slots(seg(C_FK, C_FV), jnp.where(first_aug, placed, kc_ref[...]), fk_ref)
    fv_ref[0] = seg(C_FV, C_SB).astype(BF16).T

    sb_ref[0, :, 0:512] = (seg(C_SB, C_SB + 512) * QK_SCALE).astype(BF16)
    sb_ref[0, :, 512:1024] = seg(C_SB + 512, C_SB + 1024).astype(BF16)
    sv_ref[0] = seg(C_SB + 1024, C_GATE).astype(BF16).T
    for c in range(6):
        a = 512 * c
        gate_ref[0, :, a:a + 512] = _sigmoid(seg(C_GATE + a, C_GATE + a + 512)).astype(BF16)

    cos, sina, sinb = cos_ref[0], sina_ref[0], sinb_ref[0]
    cq = (_rms(seg(C_QL, C_KVL)) * gq_ref[...]).astype(BF16)
    ckv = (_rms(seg(C_KVL, C_KR)) * gkv_ref[...]).astype(BF16)
    kr = _rope_slot(seg(C_KR, C_FF), cos, sina, sinb)
    for h in range(N_HEADS):
        a = LANES * h
        q = _rope_slot(_bdot(cq, wuq_ref[:, a:a + LANES]), cos, sina, sinb)
        mq_ref[0, :, a:a + LANES] = (q * MLA_SCALE).astype(BF16)
        mk_ref[0, :, a:a + LANES] = (_bdot(ckv, wkn_ref[:, a:a + LANES]) + kr).astype(BF16)
    mv_ref[0] = _bdot(ckv, wkv_ref[...]).astype(BF16).T


def _fox_placement():
    place = np.zeros((LANES, SLOT_W), np.float32)
    qc = np.zeros((1, SLOT_W), np.float32)
    kc = np.zeros((1, SLOT_W), np.float32)
    for h in range(N_HEADS):
        for p in range(N_PIECES):
            place[N_HEADS * p + h, LANES * h + HEAD_DIM + p] = 1.0
            place[N_HEADS * p + h, LANES * h + HEAD_DIM + N_PIECES + p] = 1.0
            qc[0, LANES * h + HEAD_DIM + p] = -1.0
            kc[0, LANES * h + HEAD_DIM + N_PIECES + p] = 1.0
    return jnp.asarray(place, BF16), jnp.asarray(qc), jnp.asarray(kc)


def _inproj(x, mod, weights, tables, *, layer, tm):
    b, s, d = x.shape
    cos, sina, sinb = tables
    place, qc, kc = _fox_placement()
    per_layer = lambda *shape: _resident_layer(shape, layer)
    row = lambda w: pl.BlockSpec((1, tm, w), lambda i, j: (i, j, 0))
    col = lambda w: pl.BlockSpec((1, w, tm), lambda i, j: (i, 0, j))
    outs = [(SLOT_W, False),
            (SLOT_W, False),
            (HEAD_W, True),
            (2 * HEAD_W, False),
            (HEAD_W, True),
            (3 * D_MODEL, False),
            (SLOT_W, False),
            (SLOT_W, False),
            (HEAD_W, True)]
    out_specs = [col(w) if tr else row(w) for w, tr in outs]
    out_shape = [jax.ShapeDtypeStruct((b, w, s) if tr else (b, s, w), BF16) for w, tr in outs]
    return pl.pallas_call(
        functools.partial(_inproj_kernel, tm=tm),
        grid=(b, s // tm),
        in_specs=[row(d),
                  pl.BlockSpec((1, 6, d), lambda i, j: (i, 0, 0)),
                  per_layer(1, d), per_layer(d, IN_W), per_layer(1, LANES),
                  per_layer(1, MLA_Q_RANK), per_layer(1, MLA_KV_RANK),
                  per_layer(MLA_Q_RANK, SLOT_W), per_layer(MLA_KV_RANK, SLOT_W),
                  per_layer(MLA_KV_RANK, HEAD_W),
                  row(LANES), row(LANES), row(LANES),
                  _resident((LANES, SLOT_W)), _resident((1, SLOT_W)), _resident((1, SLOT_W))],
        out_specs=out_specs,
        out_shape=out_shape,
        scratch_shapes=[pltpu.VMEM((1, LANES), F32)],
        compiler_params=_params(2),
        name="inproj",
    )(x, mod, *weights, cos, sina, sinb, place, qc, kc)


def _head_rows():
    row = lax.broadcasted_iota(jnp.int32, (LANES, 1), 0)
    return [(row >= HEAD_DIM * hh) & (row < HEAD_DIM * (hh + 1)) for hh in range(2)]


def _head_masks():
    lane = lax.broadcasted_iota(jnp.int32, (1, LANES), 1)
    return [(lane >= HEAD_DIM * hh) & (lane < HEAD_DIM * (hh + 1)) for hh in range(2)]


def _colmax(x, chunk=64):
    rows = x.shape[0]
    if rows > chunk and rows % chunk == 0:
        acc = x[:chunk]
        for a in range(chunk, rows, chunk):
            acc = jnp.maximum(acc, x[a:a + chunk])
        x, rows = acc, chunk
    while rows > 8 and rows % 16 == 0:
        rows //= 2
        x = jnp.maximum(x[:rows], x[rows:])
    return jnp.max(x, axis=0, keepdims=True)


def _pipeline(n, stages, unroll=PIPELINE_UNROLL, n_static=0):
    depth = len(stages)

    def static_step(t):
        for s, stage in enumerate(stages):
            if 0 <= t - s < n:
                stage(t - s, (t - s) % 2)

    first_full = min(max(depth - 1, n_static), n + depth - 1)
    n_full = max(n - first_full, 0)
    for t in range(first_full):
        static_step(t)
    n_loops = n_full // unroll
    if n_loops >= 1:
        def body(it, carry):
            t = first_full + unroll * it
            for u in range(unroll):
                for s, stage in enumerate(stages):
                    stage(t + u - s, (first_full + u - s) % 2)
            return carry

        lax.fori_loop(0, n_loops, body, 0)
    for t in range(first_full + unroll * n_loops, n + depth - 1):
        static_step(t)


def _for_each_tile(n_tiles, fn, group=4):
    while n_tiles % group:
        group //= 2

    def body(i, carry):
        for u in range(group):
            fn(i * group + u)
        return carry

    lax.fori_loop(0, n_tiles // group, body, 0)


def _softmax_block_table(n_tiles, t):
    diag = [(i * t, i * t) for i in range(n_tiles)]
    below = [(i * t, j * t) for i in range(1, n_tiles) for j in range(i)]
    return jnp.asarray(np.array(diag + below, np.int32).T), len(diag), len(below)


def _softmax_flat_kernel(tab_ref, q_ref, k_ref, vt_ref, o_ref, acc_ref, m_ref, al_ref, top_ref,
                         s0_ref, s1_ref, p0_ref, p1_ref, *, t, n_diag, n_below, chunk_shift):
    own_rows = _head_rows()
    s_bufs, p_bufs = (s0_ref, s1_ref), (p0_ref, p1_ref)
    kio = lax.broadcasted_iota(jnp.int32, (t, 1), 0)
    last_visible = lax.broadcasted_iota(jnp.int32, (1, t), 1) | ((1 << chunk_shift) - 1)
    acc_ref[...] = jnp.zeros_like(acc_ref)
    m_ref[...] = jnp.full_like(m_ref, NEG)

    def item(i):
        return pl.multiple_of(tab_ref[0, i], t), pl.multiple_of(tab_ref[1, i], t)

    def scores(i, par):
        q0, k0 = item(i)
        for hh in range(2):
            slot = slice(LANES * hh, LANES * (hh + 1))
            s = _dot_nt(k_ref[0, pl.ds(k0, t), slot], q_ref[0, pl.ds(q0, t), slot])
            if isinstance(i, int) and i < n_diag:
                s = jnp.where(kio <= last_visible, s, NEG)
            s_bufs[par][hh] = s
            top_ref[par, hh] = _colmax(s)

    def numerators(i, par):
        q0, _ = item(i)
        for hh in range(2):
            m_old = m_ref[hh, :, pl.ds(q0, t)]
            m_new = jnp.maximum(m_old, top_ref[par, hh])
            for a in range(0, t, ROW_CHUNK):
                rows = slice(a, a + ROW_CHUNK)
                p_bufs[par][hh, rows, :] = jnp.exp(s_bufs[par][hh, rows, :] - m_new).astype(BF16)
            al_ref[par, hh] = jnp.exp(m_old - m_new)
            m_ref[hh, :, pl.ds(q0, t)] = m_new

    def values(i, par):
        q0, k0 = item(i)
        vt = vt_ref[0, :, pl.ds(k0, t)]
        for hh in range(2):
            v1 = jnp.where(own_rows[hh], vt, jnp.ones_like(vt))
            acc_ref[hh, :, pl.ds(q0, t)] = (acc_ref[hh, :, pl.ds(q0, t)] * al_ref[par, hh]
                                            + _bdot(v1, p_bufs[par][hh]))

    _pipeline(n_diag + n_below, [scores, numerators, values], n_static=n_diag)

    sub = lax.broadcasted_iota(jnp.int32, (LANES, 1), 0)

    def finish(tile):
        q0 = pl.multiple_of(tile * t, t)
        outs = []
        for hh in range(2):
            acc = acc_ref[hh, :, pl.ds(q0, t)]
            denom = acc[HEAD_DIM * (1 - hh):HEAD_DIM * (1 - hh) + 1, :]
            outs.append(acc / denom)
        o_ref[0, pl.ds(q0, t), :] = jnp.where(sub < HEAD_DIM, outs[0], outs[1]).astype(o_ref.dtype).T

    _for_each_tile(n_diag, finish)


def _softmax_attention_flat(q, k, vt, *, t, chunk_shift, name):
    b, s, _ = k.shape
    table, n_diag, n_below = _softmax_block_table(s // t, t)
    grid_spec = pltpu.PrefetchScalarGridSpec(
        num_scalar_prefetch=1,
        grid=(b, N_HEADS // 2),
        in_specs=[pl.BlockSpec((1, s, 2 * LANES), lambda i, h, tab: (i, 0, h)),
                  pl.BlockSpec((1, s, 2 * LANES), lambda i, h, tab: (i, 0, h)),
                  pl.BlockSpec((1, LANES, s), lambda i, h, tab: (i, h, 0))],
        out_specs=pl.BlockSpec((1, s, LANES), lambda i, h, tab: (i, 0, h)),
        scratch_shapes=[pltpu.VMEM((2, LANES, s), F32),
                        pltpu.VMEM((2, 1, s), F32),
                        pltpu.VMEM((2, 2, 1, t), F32),
                        pltpu.VMEM((2, 2, 1, t), F32),
                        pltpu.VMEM((2, t, t), F32), pltpu.VMEM((2, t, t), F32),
                        pltpu.VMEM((2, t, t), BF16), pltpu.VMEM((2, t, t), BF16)])
    return pl.pallas_call(
        functools.partial(_softmax_flat_kernel, t=t, n_diag=n_diag, n_below=n_below,
                          chunk_shift=chunk_shift),
        grid_spec=grid_spec,
        out_shape=jax.ShapeDtypeStruct((b, s, HEAD_W), BF16),
        compiler_params=_params(2),
        name=name,
    )(table, q, k, vt)


def _sb_block_table(n_tiles, t):
    diag = [(i * t, i * t, i * t, i) for i in range(n_tiles)]
    below = [(i * t, (i - d) * t, i * t, i) for d in range(1, n_tiles) for i in range(d, n_tiles)]
    idle = [(0, 0, n_tiles * t, n_tiles)]
    return jnp.asarray(np.array(diag + below + idle, np.int32).T), len(diag), len(below)


def _sb_flat_kernel(tab_ref, q_ref, k_ref, vt_ref, o_ref, acc_ref, r_ref, tot_ref, d_ref, done_ref,
                    z0_ref, z1_ref, e0_ref, e1_ref, w0_ref, w1_ref, *, t, n_diag, n_below):
    in_head = _head_masks()
    own_rows = _head_rows()
    z_bufs, e_bufs, w_bufs = (z0_ref, z1_ref), (e0_ref, e1_ref), (w0_ref, w1_ref)
    kio = lax.broadcasted_iota(jnp.int32, (t, 1), 0)
    qio = lax.broadcasted_iota(jnp.int32, (1, t), 1)
    ss = lax.broadcasted_iota(jnp.int32, (t, t), 0)
    jj = lax.broadcasted_iota(jnp.int32, (t, t), 1)
    later = jnp.where(jj > ss, 1.0, 0.0).astype(BF16)
    acc_ref[...] = jnp.zeros_like(acc_ref)
    r_ref[...] = jnp.zeros_like(r_ref)
    for tile in range(n_diag + 1):
        done_ref[tile] = 0

    def stages(base, diag):
        def item(i):
            return (pl.multiple_of(tab_ref[0, base + i], t), pl.multiple_of(tab_ref[1, base + i], t),
                    pl.multiple_of(tab_ref[2, base + i], t), tab_ref[3, base + i])

        def scores(i, par):
            q0, k0, _, _ = item(i)
            k = k_ref[0, pl.ds(k0, t), :]
            q = q_ref[0, pl.ds(q0, t), :]
            for hh in range(2):
                qh = jnp.where(in_head[hh], q, jnp.zeros_like(q))
                z_bufs[par][hh] = _dot_nt(k, qh)

        def drops(i, par):
            for hh in range(2):
                for a in range(0, t, ROW_CHUNK):
                    rows = slice(a, a + ROW_CHUNK)
                    z = z_bufs[par][hh, rows, :]
                    neg_abs = pltpu.bitcast(pltpu.bitcast(z, jnp.uint32) | SIGN_BIT, F32)
                    drop = jnp.maximum(z, 0.0) + jnp.log(1.0 + jnp.exp(neg_abs))
                    log_beta = z - drop
                    if diag:
                        drop = jnp.where(kio[rows] < qio, drop, 0.0)
                        log_beta = jnp.where(kio[rows] < qio, log_beta, NEG)
                    d_ref[hh, rows, :] = drop.astype(BF16)
                    e_bufs[par][hh, rows, :] = log_beta
                after = _bdot(later, d_ref[hh])
                for a in range(0, t, ROW_CHUNK):
                    rows = slice(a, a + ROW_CHUNK)
                    e_bufs[par][hh, rows, :] = e_bufs[par][hh, rows, :] - after[rows]
                tot_ref[par, hh] = after[0:1, :] + d_ref[hh, 0:1, :].astype(F32)

        def weights(i, par):
            _, _, q0, tile = item(i)
            slack = None
            for hh in range(2):
                r_old = r_ref[hh, :, pl.ds(q0, t)]
                for a in range(0, t, ROW_CHUNK):
                    rows = slice(a, a + ROW_CHUNK)
                    w_bufs[par][hh, rows, :] = jnp.exp(e_bufs[par][hh, rows, :] - r_old).astype(BF16)
                r_new = r_old + tot_ref[par, hh]
                r_ref[hh, :, pl.ds(q0, t)] = r_new
                slack = r_new if slack is None else jnp.minimum(slack, r_new)
            if not diag:
                done_ref[tile] = (jnp.min(slack) > UNDERFLOW).astype(jnp.int32)

        def values(i, par):
            _, k0, q0, _ = item(i)
            vt = vt_ref[0, :, pl.ds(k0, t)]
            zero = jnp.zeros_like(vt)
            lhs = jnp.concatenate([jnp.where(own_rows[hh], vt, zero) for hh in range(2)], axis=1)
            acc_ref[:, pl.ds(q0, t)] = (acc_ref[:, pl.ds(q0, t)]
                                        + _bdot(lhs, w_bufs[par][...].reshape(2 * t, t)))

        return [scores, drops, weights, values]

    _pipeline(n_diag, stages(0, True))

    scores, drops, weights, values = stages(n_diag, False)

    def next_live(c):
        return lax.while_loop(
            lambda c: (c < n_below) & (done_ref[tab_ref[3, n_diag + c]] != 0), lambda c: c + 1, c)

    def after(c):
        return next_live(jnp.minimum(c + 1, n_below))

    def two_steps(state):
        a, b, i1, i2, i3 = state
        scores(a, 0)
        drops(i1, 1)
        weights(i2, 0)
        values(i3, 1)
        scores(b, 1)
        drops(a, 0)
        weights(i1, 1)
        values(i2, 0)
        a_next = after(b)
        return a_next, after(a_next), b, a, i1

    idle = jnp.int32(n_below)
    first = next_live(jnp.int32(0))
    lax.while_loop(lambda st: (st[0] < n_below) | (st[2] < n_below) | (st[3] < n_below)
                   | (st[4] < n_below), two_steps, (first, after(first), idle, idle, idle))

    def finish(tile):
        q0 = pl.multiple_of(tile * t, t)
        o_ref[0, pl.ds(q0, t), :] = acc_ref[:, pl.ds(q0, t)].astype(o_ref.dtype).T

    _for_each_tile(n_diag, finish)


def _sb_attention_flat(qk, vt, *, t):
    b, s, _ = qk.shape
    table, n_diag, n_below = _sb_block_table(s // t, t)
    grid_spec = pltpu.PrefetchScalarGridSpec(
        num_scalar_prefetch=1,
        grid=(b, N_HEADS // 2),
        in_specs=[pl.BlockSpec((1, s, LANES), lambda i, h, tab: (i, 0, h)),
                  pl.BlockSpec((1, s, LANES), lambda i, h, tab: (i, 0, 4 + h)),
                  pl.BlockSpec((1, LANES, s), lambda i, h, tab: (i, h, 0))],
        out_specs=pl.BlockSpec((1, s, LANES), lambda i, h, tab: (i, 0, h)),
        scratch_shapes=[pltpu.VMEM((LANES, s + t), F32),
                        pltpu.VMEM((2, 1, s + t), F32),
                        pltpu.VMEM((2, 2, 1, t), F32),
                        pltpu.VMEM((2, t, t), BF16),
                        pltpu.SMEM((s // t + 1,), jnp.int32),
                        pltpu.VMEM((2, t, t), F32), pltpu.VMEM((2, t, t), F32),
                        pltpu.VMEM((2, t, t), F32), pltpu.VMEM((2, t, t), F32),
                        pltpu.VMEM((2, t, t), BF16), pltpu.VMEM((2, t, t), BF16)])
    return pl.pallas_call(
        functools.partial(_sb_flat_kernel, t=t, n_diag=n_diag, n_below=n_below),
        grid_spec=grid_spec,
        out_shape=jax.ShapeDtypeStruct((b, s, HEAD_W), BF16),
        compiler_params=_params(2),
        name="sb_attention",
    )(table, qk, qk, vt)


def _merge_ffn_kernel(x_ref, yf_ref, ym_ref, ys_ref, gate_ref, mod_ref, wf_ref, wm_ref, ws_ref,
                      wo_ref, g_ref, wg_ref, wu_ref, wd_ref, gfin_ref, o_ref, *, final, fc):
    d = D_MODEL
    merged = (gate_ref[0, :, 0:d].astype(F32) * _bdot(yf_ref[0], wf_ref[...])
              + gate_ref[0, :, d:2 * d].astype(F32) * _bdot(ym_ref[0], wm_ref[...])
              + gate_ref[0, :, 2 * d:3 * d].astype(F32) * _bdot(ys_ref[0], ws_ref[...]))
    x = x_ref[0] + mod_ref[0, 2:3, :] * _bdot(merged.astype(BF16), wo_ref[...])

    sh, sc, gt = mod_ref[0, 3:4, :], mod_ref[0, 4:5, :], mod_ref[0, 5:6, :]
    u = ((_rms(x) * g_ref[...]) * (1.0 + sc) + sh).astype(BF16)
    acc = None
    for a in range(0, D_FF, fc):
        hg = _bdot(u, wg_ref[:, a:a + fc])
        hu = _bdot(u, wu_ref[:, a:a + fc])
        h = ((hg * _sigmoid(hg)) * hu).astype(BF16)
        part = _bdot(h, wd_ref[a:a + fc, :])
        acc = part if acc is None else acc + part
    out = x + gt * acc
    if final:
        out = _rms(out) * gfin_ref[...]
    o_ref[0] = out


def _merge_ffn(x, y_fox, y_mla, y_sb, gates, mod, weights, g_final, *, layer, tm, final):
    b, s, d = x.shape
    row = lambda w: pl.BlockSpec((1, tm, w), lambda i, j: (i, j, 0))
    per_layer = lambda *shape: _resident_layer(shape, layer)
    return pl.pallas_call(
        functools.partial(_merge_ffn_kernel, final=final, fc=D_FF // 2),
        grid=(b, s // tm),
        in_specs=[row(d), row(HEAD_W), row(HEAD_W), row(HEAD_W), row(3 * d),
                  pl.BlockSpec((1, 6, d), lambda i, j: (i, 0, 0)),
                  per_layer(HEAD_W, d), per_layer(HEAD_W, d), per_layer(HEAD_W, d),
                  per_layer(d, d), per_layer(1, d),
                  per_layer(d, D_FF), per_layer(d, D_FF), per_layer(D_FF, d),
                  _resident((1, d))],
        out_specs=row(d),
        out_shape=jax.ShapeDtypeStruct((b, s, d), F32),
        compiler_params=_params(2),
        name="merge_ffn",
    )(x, y_fox, y_mla, y_sb, gates, mod, *weights, g_final)


def _slot_cols(w, width_per_head, used):
    lead = w.shape[:-1]
    w = w.reshape(lead + (N_HEADS, width_per_head))[..., :used]
    pad = [(0, 0)] * (w.ndim - 1) + [(0, LANES - used)]
    return jnp.pad(w, pad).reshape(lead + (SLOT_W,))


def _layout_w_in(w):
    fox, ff, ql, kvl, kr, sb, gate = jnp.split(
        w, [1536, 1544, 1928, 2184, 2216, 3752], axis=-1)
    z = lambda n: jnp.zeros(w.shape[:-1] + (n,), w.dtype)
    kr_slot = jnp.concatenate([z(MLA_NOPE), kr, z(LANES - MLA_NOPE - MLA_ROPE)], axis=-1)
    ff_slot = jnp.concatenate([ff, z(LANES - N_HEADS)], axis=-1)
    return jnp.concatenate([fox, sb, gate, ql, kvl, kr_slot, ff_slot], axis=-1).astype(BF16)


def kernel(x, c, positions, g_mix, w_ada, b_ada, w_in, b_fox_f, g_mla_q, w_mla_uq, g_mla_kv,
           w_mla_ukv, w_o_fox, w_o_mla, w_o_sb, w_out, g_ffn, w_ffn_gate, w_ffn_up, w_ffn_down,
           g_final):
    b, s, d = x.shape
    depth = w_in.shape[0]
    tm = min(s, 512)
    tm_ffn = min(s, 512)
    ta = min(s, 512)
    tsb = min(s, 256)

    mod = _ada(c, w_ada, b_ada).reshape(depth, b, 6, d)
    tables = _rope_tables(positions)

    row_vec = lambda v: v.reshape(depth, 1, -1)
    wkv = w_mla_ukv.reshape(depth, MLA_KV_RANK, N_HEADS, 2 * HEAD_DIM)[..., MLA_NOPE:]
    inproj_weights = (
        row_vec(g_mix), _layout_w_in(w_in),
        row_vec(jnp.pad(b_fox_f, ((0, 0), (0, LANES - N_HEADS)))),
        row_vec(g_mla_q), row_vec(g_mla_kv),
        _slot_cols(w_mla_uq, MLA_NOPE + MLA_ROPE, MLA_NOPE + MLA_ROPE).astype(BF16),
        _slot_cols(w_mla_ukv, 2 * HEAD_DIM, MLA_NOPE).astype(BF16),
        wkv.reshape(depth, MLA_KV_RANK, HEAD_W).astype(BF16))
    merge_ffn_weights = (
        w_o_fox.astype(BF16), w_o_mla.astype(BF16), w_o_sb.astype(BF16), w_out.astype(BF16),
        row_vec(g_ffn), w_ffn_gate.astype(BF16), w_ffn_up.astype(BF16), w_ffn_down.astype(BF16))

    for l in range(depth):
        fox_q, fox_k, fox_vt, sb_qk, sb_vt, gates, mq, mk, mvt = _inproj(
            x, mod[l], inproj_weights, tables, layer=l, tm=tm)
        y_fox = _softmax_attention_flat(fox_q, fox_k, fox_vt, t=ta, chunk_shift=0,
                                        name="fox_attention")
        y_mla = _softmax_attention_flat(mq, mk, mvt, t=ta, chunk_shift=CHUNK_SHIFT,
                                        name="mla_attention")
        y_sb = _sb_attention_flat(sb_qk, sb_vt, t=tsb)
        x = _merge_ffn(x, y_fox, y_mla, y_sb, gates, mod[l], merge_ffn_weights,
                       g_final.reshape(1, d), layer=l, tm=tm_ffn, final=(l == depth - 1))
    return x
```

```python
import functools

import numpy as np
import jax
import jax.numpy as jnp
from jax import lax
from jax.experimental import pallas as pl
from jax.experimental.pallas import tpu as pltpu

F32 = jnp.float32
BF16 = jnp.bfloat16

D_MODEL = 1024
EPS = 1e-6
HEAD_DIM = 64
N_HEADS = 8
HEAD_W = N_HEADS * HEAD_DIM
CHUNK_SHIFT = 6
MLA_Q_RANK = 384
MLA_KV_RANK = 256
MLA_NOPE = 64
MLA_ROPE = 32
ROPE_HALF = MLA_ROPE // 2
ROPE_BASE = 10000.0
MLA_SCALE = (MLA_NOPE + MLA_ROPE) ** -0.5
QK_SCALE = HEAD_DIM ** -0.5
D_FF = 2816
LANES = 128
SLOT_W = N_HEADS * LANES
N_PIECES = 3

C_FQ = 0
C_FK = 512
C_FV = 1024
C_SB = 1536
C_GATE = 3072
C_QL = 6144
C_KVL = 6528
C_KR = 6784
C_FF = 6912
IN_W = 7040

ROW_CHUNK = 32
PIPELINE_UNROLL = 2
SIGN_BIT = np.uint32(0x80000000)
UNDERFLOW = 105.0
NEG = -1e30
VMEM_LIMIT = 56 * 1024 * 1024


def _params(n_grid):
    return pltpu.CompilerParams(dimension_semantics=("arbitrary",) * n_grid,
                                vmem_limit_bytes=VMEM_LIMIT)


def _resident(shape):
    nd = len(shape)
    return pl.BlockSpec(shape, lambda *_: (0,) * nd, pipeline_mode=pl.Buffered(1))


def _resident_layer(shape, layer):
    nd = len(shape)
    return pl.BlockSpec((pl.Squeezed(),) + tuple(shape), lambda *_: (layer,) + (0,) * nd,
                        pipeline_mode=pl.Buffered(1))


def _bdot(a, b):
    return jnp.dot(a, b, preferred_element_type=F32)


def _dot_nt(a, b):
    return lax.dot_general(a, b, (((1,), (1,)), ((), ())), preferred_element_type=F32)


def _split3(v):
    hi = v.astype(BF16)
    r = v - hi.astype(F32)
    mid = r.astype(BF16)
    lo = (r - mid.astype(F32)).astype(BF16)
    return hi, mid, lo


def _log_sigmoid(z):
    return jnp.minimum(z, 0.0) - jnp.log1p(jnp.exp(-jnp.abs(z)))


def _sigmoid(z):
    return 1.0 / (1.0 + jnp.exp(-z))


def _rms(x):
    return x * lax.rsqrt(jnp.mean(x * x, axis=-1, keepdims=True) + EPS)


def _ada_kernel(c_ref, w_ref, b_ref, o_ref):
    c = c_ref[...]
    cond = c * _sigmoid(c)
    chi, cmid, clo = _split3(cond)
    whi, wmid, wlo = _split3(w_ref[0])
    acc = _bdot(chi, whi) + (_bdot(chi, wmid) + _bdot(cmid, whi))
    acc = acc + (_bdot(chi, wlo) + _bdot(clo, whi) + _bdot(cmid, wmid))
    o_ref[0] = acc + b_ref[0]


def _ada(c, w_ada, b_ada):
    depth, d, n = w_ada.shape
    b = c.shape[0]
    tn = 1024
    return pl.pallas_call(
        _ada_kernel,
        grid=(depth, n // tn),
        in_specs=[pl.BlockSpec((b, d), lambda l, j: (0, 0)),
                  pl.BlockSpec((1, d, tn), lambda l, j: (l, 0, j)),
                  pl.BlockSpec((1, 1, tn), lambda l, j: (l, 0, j))],
        out_specs=pl.BlockSpec((1, b, tn), lambda l, j: (l, 0, j)),
        out_shape=jax.ShapeDtypeStruct((depth, b, n), F32),
        compiler_params=_params(2),
        name="adaln",
    )(c, w_ada, b_ada.reshape(depth, 1, n))


def _rope_kernel(pos_ref, inv_ref, cos_ref, sina_ref, sinb_ref):
    ang = pos_ref[0].astype(F32) * inv_ref[...]
    lane = lax.broadcasted_iota(jnp.int32, (1, LANES), 1)
    sin = jnp.sin(ang)
    cos_ref[0] = jnp.cos(ang)
    lo, mid, hi = MLA_NOPE, MLA_NOPE + ROPE_HALF, MLA_NOPE + MLA_ROPE
    sina_ref[0] = jnp.where((lane >= lo) & (lane < mid), -sin, 0.0)
    sinb_ref[0] = jnp.where((lane >= mid) & (lane < hi), sin, 0.0)


def _rope_tables(positions):
    b, s = positions.shape
    ts = min(s, 512)
    inv = ROPE_BASE ** (-jnp.arange(ROPE_HALF, dtype=F32) / ROPE_HALF)
    inv_slot = jnp.concatenate([jnp.zeros((MLA_NOPE,), F32), inv, inv,
                                jnp.zeros((LANES - MLA_NOPE - MLA_ROPE,), F32)]).reshape(1, LANES)
    spec = pl.BlockSpec((1, ts, LANES), lambda i, j: (i, j, 0))
    return pl.pallas_call(
        _rope_kernel,
        grid=(b, s // ts),
        in_specs=[pl.BlockSpec((1, ts, 1), lambda i, j: (i, j, 0)),
                  pl.BlockSpec((1, LANES), lambda i, j: (0, 0))],
        out_specs=[spec, spec, spec],
        out_shape=[jax.ShapeDtypeStruct((b, s, LANES), F32)] * 3,
        compiler_params=_params(2),
        name="rope_tables",
    )(positions.reshape(b, s, 1), inv_slot)


def _rope_slot(x, cos, sina, sinb):
    return (x * cos + pltpu.roll(x, LANES - ROPE_HALF, 1) * sina
            + pltpu.roll(x, ROPE_HALF, 1) * sinb)


def _inproj_kernel(x_ref, mod_ref, g_ref, w_ref, bff_ref, gq_ref, gkv_ref, wuq_ref, wkn_ref,
                   wkv_ref, cos_ref, sina_ref, sinb_ref, place_ref, qc_ref, kc_ref,
                   fq_ref, fk_ref, fv_ref, sb_ref, sv_ref, gate_ref, mq_ref, mk_ref, mv_ref,
                   carry_ref, *, tm):
    x = x_ref[0]
    sh, sc = mod_ref[0, 0:1, :], mod_ref[0, 1:2, :]
    u = ((_rms(x) * g_ref[...]) * (1.0 + sc) + sh).astype(BF16)

    def seg(a, b):
        return _bdot(u, w_ref[:, a:b])

    lane = lax.broadcasted_iota(jnp.int32, (1, LANES), 1)
    cos, sina, sinb = cos_ref[0], sina_ref[0], sinb_ref[0]

    def gates(chunks):
        for c in chunks:
            a = 512 * c
            gate_ref[0, :, a:a + 512] = _sigmoid(seg(C_GATE + a, C_GATE + a + 512)).astype(BF16)

    def pack3(v):
        p0, p1, p2 = (p.astype(F32) for p in _split3(v))
        return jnp.where(lane < N_HEADS, p0,
                         jnp.where(lane < 2 * N_HEADS, pltpu.roll(p1, N_HEADS, 1),
                                   jnp.where(lane < 3 * N_HEADS, pltpu.roll(p2, 2 * N_HEADS, 1),
                                             0.0))).astype(BF16)

    def to_slots(compact, extra, out_ref):
        for h in range(N_HEADS):
            x = compact[:, LANES * (h // 2):LANES * (h // 2 + 1)]
            head = pltpu.roll(x, HEAD_DIM, 1) if h % 2 else x
            slot = slice(LANES * h, LANES * (h + 1))
            out_ref[0, :, slot] = jnp.where(lane < HEAD_DIM, head, extra[:, slot]).astype(BF16)

    forget_logits = seg(C_FF, IN_W)
    q_latent = seg(C_QL, C_KVL)
    kv_latent = seg(C_KVL, C_KR)
    rope_key = seg(C_KR, C_FF)
    log_f = pack3(_log_sigmoid(forget_logits + bff_ref[...]))
    cq = (_rms(q_latent) * gq_ref[...]).astype(BF16)
    ckv = (_rms(kv_latent) * gkv_ref[...]).astype(BF16)
    gates(range(0, 3))

    r = lax.broadcasted_iota(jnp.int32, (tm, tm), 0)
    c = lax.broadcasted_iota(jnp.int32, (tm, tm), 1)
    part = _bdot(jnp.where(r >= c, 1.0, 0.0).astype(BF16), log_f)
    q_all = _bdot(cq, wuq_ref[...])
    k_all = _bdot(ckv, wkn_ref[...])
    v_all = _bdot(ckv, wkv_ref[...])
    gates(range(3, 6))

    first_tile = pl.program_id(1) == 0
    cum = ((part + pltpu.roll(part, LANES - N_HEADS, 1))
           + pltpu.roll(part, LANES - 2 * N_HEADS, 1)) + jnp.where(first_tile, 0.0, carry_ref[...])
    carry_ref[...] = jnp.where(lane < N_HEADS, cum[tm - 1:tm, :], 0.0)
    placed = _bdot(pack3(cum), place_ref[...])
    first_aug = (lax.broadcasted_iota(jnp.int32, (1, SLOT_W), 1) & (LANES - 1)) < HEAD_DIM + N_PIECES
    fox_q = seg(C_FQ, C_FK) * QK_SCALE
    fox_k = seg(C_FK, C_FV)

    kr = _rope_slot(rope_key, cos, sina, sinb)
    for h in range(N_HEADS):
        slot = slice(LANES * h, LANES * (h + 1))
        q = _rope_slot(q_all[:, slot], cos, sina, sinb)
        mq_ref[0, :, slot] = (q * MLA_SCALE).astype(BF16)
        mk_ref[0, :, slot] = (k_all[:, slot] + kr).astype(BF16)
    mv_ref[0] = v_all.astype(BF16).T
    to_slots(fox_q, jnp.where(first_aug, qc_ref[...], placed), fq_ref)
    to_slots(fox_k, jnp.where(first_aug, placed, kc_ref[...]), fk_ref)

    sv_ref[0] = seg(C_SB + 1024, C_GATE).astype(BF16).T
    fv_ref[0] = seg(C_FV, C_SB).astype(BF16).T
    sb_ref[0, :, 0:512] = (seg(C_SB, C_SB + 512) * QK_SCALE).astype(BF16)
    sb_ref[0, :, 512:1024] = seg(C_SB + 512, C_SB + 1024).astype(BF16)


def _fox_placement():
    place = np.zeros((LANES, SLOT_W), np.float32)
    qc = np.zeros((1, SLOT_W), np.float32)
    kc = np.zeros((1, SLOT_W), np.float32)
    for h in range(N_HEADS):
        for p in range(N_PIECES):
            place[N_HEADS * p + h, LANES * h + HEAD_DIM + p] = 1.0
            place[N_HEADS * p + h, LANES * h + HEAD_DIM + N_PIECES + p] = 1.0
            qc[0, LANES * h + HEAD_DIM + p] = -1.0
            kc[0, LANES * h + HEAD_DIM + N_PIECES + p] = 1.0
    return jnp.asarray(place, BF16), jnp.asarray(qc), jnp.asarray(kc)


def _inproj(x, mod, weights, tables, *, layer, tm):
    b, s, d = x.shape
    cos, sina, sinb = tables
    place, qc, kc = _fox_placement()
    per_layer = lambda *shape: _resident_layer(shape, layer)
    row = lambda w: pl.BlockSpec((1, tm, w), lambda i, j: (i, j, 0))
    col = lambda w: pl.BlockSpec((1, w, tm), lambda i, j: (i, 0, j))
    outs = [(SLOT_W, False),
            (SLOT_W, False),
            (HEAD_W, True),
            (2 * HEAD_W, False),
            (HEAD_W, True),
            (3 * D_MODEL, False),
            (SLOT_W, False),
            (SLOT_W, False),
            (HEAD_W, True)]
    out_specs = [col(w) if tr else row(w) for w, tr in outs]
    out_shape = [jax.ShapeDtypeStruct((b, w, s) if tr else (b, s, w), BF16) for w, tr in outs]
    return pl.pallas_call(
        functools.partial(_inproj_kernel, tm=tm),
        grid=(b, s // tm),
        in_specs=[row(d),
                  pl.BlockSpec((1, 6, d), lambda i, j: (i, 0, 0)),
                  per_layer(1, d), per_layer(d, IN_W), per_layer(1, LANES),
                  per_layer(1, MLA_Q_RANK), per_layer(1, MLA_KV_RANK),
                  per_layer(MLA_Q_RANK, SLOT_W), per_layer(MLA_KV_RANK, SLOT_W),
                  per_layer(MLA_KV_RANK, HEAD_W),
                  row(LANES), row(LANES), row(LANES),
                  _resident((LANES, SLOT_W)), _resident((1, SLOT_W)), _resident((1, SLOT_W))],
        out_specs=out_specs,
        out_shape=out_shape,
        scratch_shapes=[pltpu.VMEM((1, LANES), F32)],
        compiler_params=_params(2),
        name="inproj",
    )(x, mod, *weights, cos, sina, sinb, place, qc, kc)


def _head_rows():
    row = lax.broadcasted_iota(jnp.int32, (LANES, 1), 0)
    return [(row >= HEAD_DIM * hh) & (row < HEAD_DIM * (hh + 1)) for hh in range(2)]


def _head_masks():
    lane = lax.broadcasted_iota(jnp.int32, (1, LANES), 1)
    return [(lane >= HEAD_DIM * hh) & (lane < HEAD_DIM * (hh + 1)) for hh in range(2)]


def _colmax(x, chunk=64):
    rows = x.shape[0]
    if rows > chunk and rows % chunk == 0:
        acc = x[:chunk]
        for a in range(chunk, rows, chunk):
            acc = jnp.maximum(acc, x[a:a + chunk])
        x, rows = acc, chunk
    while rows > 8 and rows % 16 == 0:
        rows //= 2
        x = jnp.maximum(x[:rows], x[rows:])
    return jnp.max(x, axis=0, keepdims=True)


def _pipeline(n, stages, unroll=PIPELINE_UNROLL, n_static=0):
    depth = len(stages)

    def static_step(t):
        for s, stage in enumerate(stages):
            if 0 <= t - s < n:
                stage(t - s, (t - s) % 2)

    first_full = min(max(depth - 1, n_static), n + depth - 1)
    n_full = max(n - first_full, 0)
    for t in range(first_full):
        static_step(t)
    n_loops = n_full // unroll
    if n_loops >= 1:
        def body(it, carry):
            t = first_full + unroll * it
            for u in range(unroll):
                for s, stage in enumerate(stages):
                    stage(t + u - s, (first_full + u - s) % 2)
            return carry

        lax.fori_loop(0, n_loops, body, 0)
    for t in range(first_full + unroll * n_loops, n + depth - 1):
        static_step(t)


def _for_each_tile(n_tiles, fn, group=4):
    while n_tiles % group:
        group //= 2

    def body(i, carry):
        for u in range(group):
            fn(i * group + u)
        return carry

    lax.fori_loop(0, n_tiles // group, body, 0)


def _softmax_block_table(n_tiles, t):
    diag = [(i * t, i * t) for i in range(n_tiles)]
    below = [(i * t, j * t) for i in range(1, n_tiles) for j in range(i)]
    return jnp.asarray(np.array(diag + below, np.int32).T), len(diag), len(below)


def _softmax_flat_kernel(tab_ref, q_ref, k_ref, vt_ref, o_ref, acc_ref, m_ref, al_ref, top_ref,
                         s0_ref, s1_ref, p0_ref, p1_ref, *, t, n_diag, n_below, chunk_shift):
    own_rows = _head_rows()
    s_bufs, p_bufs = (s0_ref, s1_ref), (p0_ref, p1_ref)
    kio = lax.broadcasted_iota(jnp.int32, (t, 1), 0)
    last_visible = lax.broadcasted_iota(jnp.int32, (1, t), 1) | ((1 << chunk_shift) - 1)
    acc_ref[...] = jnp.zeros_like(acc_ref)
    m_ref[...] = jnp.full_like(m_ref, NEG)

    def item(i):
        return pl.multiple_of(tab_ref[0, i], t), pl.multiple_of(tab_ref[1, i], t)

    def scores(i, par):
        q0, k0 = item(i)
        for hh in range(2):
            slot = slice(LANES * hh, LANES * (hh + 1))
            s = _dot_nt(k_ref[0, pl.ds(k0, t), slot], q_ref[0, pl.ds(q0, t), slot])
            if isinstance(i, int) and i < n_diag:
                s = jnp.where(kio <= last_visible, s, NEG)
            s_bufs[par][hh] = s
            top_ref[par, hh] = _colmax(s)

    def numerators(i, par):
        q0, _ = item(i)
        for hh in range(2):
            m_old = m_ref[hh, :, pl.ds(q0, t)]
            m_new = jnp.maximum(m_old, top_ref[par, hh])
            for a in range(0, t, ROW_CHUNK):
                rows = slice(a, a + ROW_CHUNK)
                p_bufs[par][hh, rows, :] = jnp.exp(s_bufs[par][hh, rows, :] - m_new).astype(BF16)
            al_ref[par, hh] = jnp.exp(m_old - m_new)
            m_ref[hh, :, pl.ds(q0, t)] = m_new

    def values(i, par):
        q0, k0 = item(i)
        vt = vt_ref[0, :, pl.ds(k0, t)]
        for hh in range(2):
            v1 = jnp.where(own_rows[hh], vt, jnp.ones_like(vt))
            acc_ref[hh, :, pl.ds(q0, t)] = (acc_ref[hh, :, pl.ds(q0, t)] * al_ref[par, hh]
                                            + _bdot(v1, p_bufs[par][hh]))

    _pipeline(n_diag + n_below, [scores, numerators, values], n_static=n_diag)

    sub = lax.broadcasted_iota(jnp.int32, (LANES, 1), 0)

    def finish(tile):
        q0 = pl.multiple_of(tile * t, t)
        outs = []
        for hh in range(2):
            acc = acc_ref[hh, :, pl.ds(q0, t)]
            denom = acc[HEAD_DIM * (1 - hh):HEAD_DIM * (1 - hh) + 1, :]
            outs.append(acc / denom)
        o_ref[0, pl.ds(q0, t), :] = jnp.where(sub < HEAD_DIM, outs[0], outs[1]).astype(o_ref.dtype).T

    _for_each_tile(n_diag, finish)


def _softmax_attention_flat(q, k, vt, *, t, chunk_shift, name):
    b, s, _ = k.shape
    table, n_diag, n_below = _softmax_block_table(s // t, t)
    grid_spec = pltpu.PrefetchScalarGridSpec(
        num_scalar_prefetch=1,
        grid=(b, N_HEADS // 2),
        in_specs=[pl.BlockSpec((1, s, 2 * LANES), lambda i, h, tab: (i, 0, h)),
                  pl.BlockSpec((1, s, 2 * LANES), lambda i, h, tab: (i, 0, h)),
                  pl.BlockSpec((1, LANES, s), lambda i, h, tab: (i, h, 0))],
        out_specs=pl.BlockSpec((1, s, LANES), lambda i, h, tab: (i, 0, h)),
        scratch_shapes=[pltpu.VMEM((2, LANES, s), F32),
                        pltpu.VMEM((2, 1, s), F32),
                        pltpu.VMEM((2, 2, 1, t), F32),
                        pltpu.VMEM((2, 2, 1, t), F32),
                        pltpu.VMEM((2, t, t), F32), pltpu.VMEM((2, t, t), F32),
                        pltpu.VMEM((2, t, t), BF16), pltpu.VMEM((2, t, t), BF16)])
    return pl.pallas_call(
        functools.partial(_softmax_flat_kernel, t=t, n_diag=n_diag, n_below=n_below,
                          chunk_shift=chunk_shift),
        grid_spec=grid_spec,
        out_shape=jax.ShapeDtypeStruct((b, s, HEAD_W), BF16),
        compiler_params=_params(2),
        name=name,
    )(table, q, k, vt)


def _sb_block_table(n_tiles, t):
    diag = [(i * t, i * t, i * t, i) for i in range(n_tiles)]
    below = [(i * t, (i - d) * t, i * t, i) for d in range(1, n_tiles) for i in range(d, n_tiles)]
    idle = [(0, 0, n_tiles * t, n_tiles)]
    return jnp.asarray(np.array(diag + below + idle, np.int32).T), len(diag), len(below)


def _sb_flat_kernel(tab_ref, q_ref, k_ref, vt_ref, o_ref, acc_ref, r_ref, tot_ref, d_ref, done_ref,
                    z0_ref, z1_ref, e0_ref, e1_ref, w0_ref, w1_ref, *, t, n_diag, n_below):
    in_head = _head_masks()
    own_rows = _head_rows()
    z_bufs, e_bufs, w_bufs = (z0_ref, z1_ref), (e0_ref, e1_ref), (w0_ref, w1_ref)
    kio = lax.broadcasted_iota(jnp.int32, (t, 1), 0)
    qio = lax.broadcasted_iota(jnp.int32, (1, t), 1)
    ss = lax.broadcasted_iota(jnp.int32, (t, t), 0)
    jj = lax.broadcasted_iota(jnp.int32, (t, t), 1)
    later = jnp.where(jj > ss, 1.0, 0.0).astype(BF16)
    acc_ref[...] = jnp.zeros_like(acc_ref)
    r_ref[...] = jnp.zeros_like(r_ref)
    for tile in range(n_diag + 1):
        done_ref[tile] = 0

    def stages(base, diag):
        def item(i):
            return (pl.multiple_of(tab_ref[0, base + i], t), pl.multiple_of(tab_ref[1, base + i], t),
                    pl.multiple_of(tab_ref[2, base + i], t), tab_ref[3, base + i])

        def scores(i, par):
            q0, k0, _, _ = item(i)
            k = k_ref[0, pl.ds(k0, t), :]
            q = q_ref[0, pl.ds(q0, t), :]
            for hh in range(2):
                qh = jnp.where(in_head[hh], q, jnp.zeros_like(q))
                z_bufs[par][hh] = _dot_nt(k, qh)

        def drops(i, par):
            for hh in range(2):
                for a in range(0, t, ROW_CHUNK):
                    rows = slice(a, a + ROW_CHUNK)
                    z = z_bufs[par][hh, rows, :]
                    neg_abs = pltpu.bitcast(pltpu.bitcast(z, jnp.uint32) | SIGN_BIT, F32)
                    drop = jnp.maximum(z, 0.0) + jnp.log(1.0 + jnp.exp(neg_abs))
                    log_beta = z - drop
                    if diag:
                        drop = jnp.where(kio[rows] < qio, drop, 0.0)
                        log_beta = jnp.where(kio[rows] < qio, log_beta, NEG)
                    d_ref[hh, rows, :] = drop.astype(BF16)
                    e_bufs[par][hh, rows, :] = log_beta
                after = _bdot(later, d_ref[hh])
                for a in range(0, t, ROW_CHUNK):
                    rows = slice(a, a + ROW_CHUNK)
                    e_bufs[par][hh, rows, :] = e_bufs[par][hh, rows, :] - after[rows]
                tot_ref[par, hh] = after[0:1, :] + d_ref[hh, 0:1, :].astype(F32)

        def weights(i, par):
            _, _, q0, tile = item(i)
            slack = None
            for hh in range(2):
                r_old = r_ref[hh, :, pl.ds(q0, t)]
                for a in range(0, t, ROW_CHUNK):
                    rows = slice(a, a + ROW_CHUNK)
                    w_bufs[par][hh, rows, :] = jnp.exp(e_bufs[par][hh, rows, :] - r_old).astype(BF16)
                r_new = r_old + tot_ref[par, hh]
                r_ref[hh, :, pl.ds(q0, t)] = r_new
                slack = r_new if slack is None else jnp.minimum(slack, r_new)
            if not diag:
                done_ref[tile] = (jnp.min(slack) > UNDERFLOW).astype(jnp.int32)

        def values(i, par):
            _, k0, q0, _ = item(i)
            vt = vt_ref[0, :, pl.ds(k0, t)]
            zero = jnp.zeros_like(vt)
            lhs = jnp.concatenate([jnp.where(own_rows[hh], vt, zero) for hh in range(2)], axis=1)
            acc_ref[:, pl.ds(q0, t)] = (acc_ref[:, pl.ds(q0, t)]
                                        + _bdot(lhs, w_bufs[par][...].reshape(2 * t, t)))

        return [scores, drops, weights, values]

    _pipeline(n_diag, stages(0, True))

    scores, drops, weights, values = stages(n_diag, False)

    def next_live(c):
        return lax.while_loop(
            lambda c: (c < n_below) & (done_ref[tab_ref[3, n_diag + c]] != 0), lambda c: c + 1, c)

    def after(c):
        return next_live(jnp.minimum(c + 1, n_below))

    def two_steps(state):
        a, b, i1, i2, i3 = state
        scores(a, 0)
        drops(i1, 1)
        weights(i2, 0)
        values(i3, 1)
        scores(b, 1)
        drops(a, 0)
        weights(i1, 1)
        values(i2, 0)
        a_next = after(b)
        return a_next, after(a_next), b, a, i1

    idle = jnp.int32(n_below)
    first = next_live(jnp.int32(0))
    lax.while_loop(lambda st: (st[0] < n_below) | (st[2] < n_below) | (st[3] < n_below)
                   | (st[4] < n_below), two_steps, (first, after(first), idle, idle, idle))

    def finish(tile):
        q0 = pl.multiple_of(tile * t, t)
        o_ref[0, pl.ds(q0, t), :] = acc_ref[:, pl.ds(q0, t)].astype(o_ref.dtype).T

    _for_each_tile(n_diag, finish)


def _sb_attention_flat(qk, vt, *, t):
    b, s, _ = qk.shape
    table, n_diag, n_below = _sb_block_table(s // t, t)
    grid_spec = pltpu.PrefetchScalarGridSpec(
        num_scalar_prefetch=1,
        grid=(b, N_HEADS // 2),
        in_specs=[pl.BlockSpec((1, s, LANES), lambda i, h, tab: (i, 0, h)),
                  pl.BlockSpec((1, s, LANES), lambda i, h, tab: (i, 0, 4 + h)),
                  pl.BlockSpec((1, LANES, s), lambda i, h, tab: (i, h, 0))],
        out_specs=pl.BlockSpec((1, s, LANES), lambda i, h, tab: (i, 0, h)),
        scratch_shapes=[pltpu.VMEM((LANES, s + t), F32),
                        pltpu.VMEM((2, 1, s + t), F32),
                        pltpu.VMEM((2, 2, 1, t), F32),
                        pltpu.VMEM((2, t, t), BF16),
                        pltpu.SMEM((s // t + 1,), jnp.int32),
                        pltpu.VMEM((2, t, t), F32), pltpu.VMEM((2, t, t), F32),
                        pltpu.VMEM((2, t, t), F32), pltpu.VMEM((2, t, t), F32),
                        pltpu.VMEM((2, t, t), BF16), pltpu.VMEM((2, t, t), BF16)])
    return pl.pallas_call(
        functools.partial(_sb_flat_kernel, t=t, n_diag=n_diag, n_below=n_below),
        grid_spec=grid_spec,
        out_shape=jax.ShapeDtypeStruct((b, s, HEAD_W), BF16),
        compiler_params=_params(2),
        name="sb_attention",
    )(table, qk, qk, vt)


def _merge_ffn_kernel(x_ref, yf_ref, ym_ref, ys_ref, gate_ref, mod_ref, wf_ref, wm_ref, ws_ref,
                      wo_ref, g_ref, wg_ref, wu_ref, wd_ref, gfin_ref, o_ref, *, final, fc):
    d = D_MODEL
    merged = (gate_ref[0, :, 0:d].astype(F32) * _bdot(yf_ref[0], wf_ref[...])
              + gate_ref[0, :, d:2 * d].astype(F32) * _bdot(ym_ref[0], wm_ref[...])
              + gate_ref[0, :, 2 * d:3 * d].astype(F32) * _bdot(ys_ref[0], ws_ref[...]))
    x = x_ref[0] + mod_ref[0, 2:3, :] * _bdot(merged.astype(BF16), wo_ref[...])

    sh, sc, gt = mod_ref[0, 3:4, :], mod_ref[0, 4:5, :], mod_ref[0, 5:6, :]
    u = ((_rms(x) * g_ref[...]) * (1.0 + sc) + sh).astype(BF16)
    acc = None
    for a in range(0, D_FF, fc):
        hg = _bdot(u, wg_ref[:, a:a + fc])
        hu = _bdot(u, wu_ref[:, a:a + fc])
        h = ((hg * _sigmoid(hg)) * hu).astype(BF16)
        part = _bdot(h, wd_ref[a:a + fc, :])
        acc = part if acc is None else acc + part
    out = x + gt * acc
    if final:
        out = _rms(out) * gfin_ref[...]
    o_ref[0] = out


def _merge_ffn(x, y_fox, y_mla, y_sb, gates, mod, weights, g_final, *, layer, tm, final):
    b, s, d = x.shape
    row = lambda w: pl.BlockSpec((1, tm, w), lambda i, j: (i, j, 0))
    per_layer = lambda *shape: _resident_layer(shape, layer)
    return pl.pallas_call(
        functools.partial(_merge_ffn_kernel, final=final, fc=D_FF // 2),
        grid=(b, s // tm),
        in_specs=[row(d), row(HEAD_W), row(HEAD_W), row(HEAD_W), row(3 * d),
                  pl.BlockSpec((1, 6, d), lambda i, j: (i, 0, 0)),
                  per_layer(HEAD_W, d), per_layer(HEAD_W, d), per_layer(HEAD_W, d),
                  per_layer(d, d), per_layer(1, d),
                  per_layer(d, D_FF), per_layer(d, D_FF), per_layer(D_FF, d),
                  _resident((1, d))],
        out_specs=row(d),
        out_shape=jax.ShapeDtypeStruct((b, s, d), F32),
        compiler_params=_params(2),
        name="merge_ffn",
    )(x, y_fox, y_mla, y_sb, gates, mod, *weights, g_final)


def _slot_cols(w, width_per_head, used):
    lead = w.shape[:-1]
    w = w.reshape(lead + (N_HEADS, width_per_head))[..., :used]
    pad = [(0, 0)] * (w.ndim - 1) + [(0, LANES - used)]
    return jnp.pad(w, pad).reshape(lead + (SLOT_W,))


def _layout_w_in(w):
    fox, ff, ql, kvl, kr, sb, gate = jnp.split(
        w, [1536, 1544, 1928, 2184, 2216, 3752], axis=-1)
    z = lambda n: jnp.zeros(w.shape[:-1] + (n,), w.dtype)
    kr_slot = jnp.concatenate([z(MLA_NOPE), kr, z(LANES - MLA_NOPE - MLA_ROPE)], axis=-1)
    ff_slot = jnp.concatenate([ff, z(LANES - N_HEADS)], axis=-1)
    return jnp.concatenate([fox, sb, gate, ql, kvl, kr_slot, ff_slot], axis=-1).astype(BF16)


def kernel(x, c, positions, g_mix, w_ada, b_ada, w_in, b_fox_f, g_mla_q, w_mla_uq, g_mla_kv,
           w_mla_ukv, w_o_fox, w_o_mla, w_o_sb, w_out, g_ffn, w_ffn_gate, w_ffn_up, w_ffn_down,
           g_final):
    b, s, d = x.shape
    depth = w_in.shape[0]
    tm = min(s, 512)
    tm_ffn = min(s, 512)
    ta = min(s, 512)
    tsb = min(s, 256)

    mod = _ada(c, w_ada, b_ada).reshape(depth, b, 6, d)
    tables = _rope_tables(positions)

    row_vec = lambda v: v.reshape(depth, 1, -1)
    wkv = w_mla_ukv.reshape(depth, MLA_KV_RANK, N_HEADS, 2 * HEAD_DIM)[..., MLA_NOPE:]
    inproj_weights = (
        row_vec(g_mix), _layout_w_in(w_in),
        row_vec(jnp.pad(b_fox_f, ((0, 0), (0, LANES - N_HEADS)))),
        row_vec(g_mla_q), row_vec(g_mla_kv),
        _slot_cols(w_mla_uq, MLA_NOPE + MLA_ROPE, MLA_NOPE + MLA_ROPE).astype(BF16),
        _slot_cols(w_mla_ukv, 2 * HEAD_DIM, MLA_NOPE).astype(BF16),
        wkv.reshape(depth, MLA_KV_RANK, HEAD_W).astype(BF16))
    merge_ffn_weights = (
        w_o_fox.astype(BF16), w_o_mla.astype(BF16), w_o_sb.astype(BF16), w_out.astype(BF16),
        row_vec(g_ffn), w_ffn_gate.astype(BF16), w_ffn_up.astype(BF16), w_ffn_down.astype(BF16))

    for l in range(depth):
        fox_q, fox_k, fox_vt, sb_qk, sb_vt, gates, mq, mk, mvt = _inproj(
            x, mod[l], inproj_weights, tables, layer=l, tm=tm)
        y_fox = _softmax_attention_flat(fox_q, fox_k, fox_vt, t=ta, chunk_shift=0,
                                        name="fox_attention")
        y_mla = _softmax_attention_flat(mq, mk, mvt, t=ta, chunk_shift=CHUNK_SHIFT,
                                        name="mla_attention")
        y_sb = _sb_attention_flat(sb_qk, sb_vt, t=tsb)
        x = _merge_ffn(x, y_fox, y_mla, y_sb, gates, mod[l], merge_ffn_weights,
                       g_final.reshape(1, d), layer=l, tm=tm_ffn, final=(l == depth - 1))
    return x
```

```python
import functools

import numpy as np
import jax
import jax.numpy as jnp
from jax import lax
from jax.experimental import pallas as pl
from jax.experimental.pallas import tpu as pltpu

F32 = jnp.float32
BF16 = jnp.bfloat16

D_MODEL = 1024
EPS = 1e-6
HEAD_DIM = 64
N_HEADS = 8
HEAD_W = N_HEADS * HEAD_DIM
CHUNK_SHIFT = 6
MLA_Q_RANK = 384
MLA_KV_RANK = 256
MLA_NOPE = 64
MLA_ROPE = 32
ROPE_HALF = MLA_ROPE // 2
ROPE_BASE = 10000.0
MLA_SCALE = (MLA_NOPE + MLA_ROPE) ** -0.5
QK_SCALE = HEAD_DIM ** -0.5
D_FF = 2816
LANES = 128
SLOT_W = N_HEADS * LANES
N_PIECES = 3

C_FQ = 0
C_FK = 512
C_FV = 1024
C_SB = 1536
C_GATE = 3072
C_QL = 6144
C_KVL = 6528
C_KR = 6784
C_FF = 6912
IN_W = 7040

ROW_CHUNK = 32
PIPELINE_UNROLL = 2
SIGN_BIT = np.uint32(0x80000000)
UNDERFLOW = 105.0
NEG = -1e30
VMEM_LIMIT = 56 * 1024 * 1024


def _params(n_grid):
    return pltpu.CompilerParams(dimension_semantics=("arbitrary",) * n_grid,
                                vmem_limit_bytes=VMEM_LIMIT)


def _resident(shape):
    nd = len(shape)
    return pl.BlockSpec(shape, lambda *_: (0,) * nd, pipeline_mode=pl.Buffered(1))


def _resident_layer(shape, layer):
    nd = len(shape)
    return pl.BlockSpec((pl.Squeezed(),) + tuple(shape), lambda *_: (layer,) + (0,) * nd,
                        pipeline_mode=pl.Buffered(1))


def _bdot(a, b):
    return jnp.dot(a, b, preferred_element_type=F32)


def _dot_nt(a, b):
    return lax.dot_general(a, b, (((1,), (1,)), ((), ())), preferred_element_type=F32)


def _split3(v):
    hi = v.astype(BF16)
    r = v - hi.astype(F32)
    mid = r.astype(BF16)
    lo = (r - mid.astype(F32)).astype(BF16)
    return hi, mid, lo


def _log_sigmoid(z):
    return jnp.minimum(z, 0.0) - jnp.log1p(jnp.exp(-jnp.abs(z)))


def _sigmoid(z):
    return 1.0 / (1.0 + jnp.exp(-z))


def _rms(x):
    return x * lax.rsqrt(jnp.mean(x * x, axis=-1, keepdims=True) + EPS)


def _ada_kernel(c_ref, w_ref, b_ref, o_ref):
    c = c_ref[...]
    cond = c * _sigmoid(c)
    chi, cmid, clo = _split3(cond)
    whi, wmid, wlo = _split3(w_ref[0])
    acc = _bdot(chi, whi) + (_bdot(chi, wmid) + _bdot(cmid, whi))
    acc = acc + (_bdot(chi, wlo) + _bdot(clo, whi) + _bdot(cmid, wmid))
    o_ref[0] = acc + b_ref[0]


def _ada(c, w_ada, b_ada):
    depth, d, n = w_ada.shape
    b = c.shape[0]
    tn = 1024
    return pl.pallas_call(
        _ada_kernel,
        grid=(depth, n // tn),
        in_specs=[pl.BlockSpec((b, d), lambda l, j: (0, 0)),
                  pl.BlockSpec((1, d, tn), lambda l, j: (l, 0, j)),
                  pl.BlockSpec((1, 1, tn), lambda l, j: (l, 0, j))],
        out_specs=pl.BlockSpec((1, b, tn), lambda l, j: (l, 0, j)),
        out_shape=jax.ShapeDtypeStruct((depth, b, n), F32),
        compiler_params=_params(2),
        name="adaln",
    )(c, w_ada, b_ada.reshape(depth, 1, n))


def _rope_kernel(pos_ref, inv_ref, cos_ref, sina_ref, sinb_ref):
    ang = pos_ref[0].astype(F32) * inv_ref[...]
    lane = lax.broadcasted_iota(jnp.int32, (1, LANES), 1)
    sin = jnp.sin(ang)
    cos_ref[0] = jnp.cos(ang)
    lo, mid, hi = MLA_NOPE, MLA_NOPE + ROPE_HALF, MLA_NOPE + MLA_ROPE
    sina_ref[0] = jnp.where((lane >= lo) & (lane < mid), -sin, 0.0)
    sinb_ref[0] = jnp.where((lane >= mid) & (lane < hi), sin, 0.0)


def _rope_tables(positions):
    b, s = positions.shape
    ts = min(s, 512)
    inv = ROPE_BASE ** (-jnp.arange(ROPE_HALF, dtype=F32) / ROPE_HALF)
    inv_slot = jnp.concatenate([jnp.zeros((MLA_NOPE,), F32), inv, inv,
                                jnp.zeros((LANES - MLA_NOPE - MLA_ROPE,), F32)]).reshape(1, LANES)
    spec = pl.BlockSpec((1, ts, LANES), lambda i, j: (i, j, 0))
    return pl.pallas_call(
        _rope_kernel,
        grid=(b, s // ts),
        in_specs=[pl.BlockSpec((1, ts, 1), lambda i, j: (i, j, 0)),
                  pl.BlockSpec((1, LANES), lambda i, j: (0, 0))],
        out_specs=[spec, spec, spec],
        out_shape=[jax.ShapeDtypeStruct((b, s, LANES), F32)] * 3,
        compiler_params=_params(2),
        name="rope_tables",
    )(positions.reshape(b, s, 1), inv_slot)


def _rope_slot(x, cos, sina, sinb):
    return (x * cos + pltpu.roll(x, LANES - ROPE_HALF, 1) * sina
            + pltpu.roll(x, ROPE_HALF, 1) * sinb)


def _inproj_kernel(x_ref, mod_ref, g_ref, w_ref, bff_ref, gq_ref, gkv_ref, wuq_ref, wkn_ref,
                   wkv_ref, cos_ref, sina_ref, sinb_ref, place_ref, qc_ref, kc_ref,
                   fq_ref, fk_ref, fv_ref, sb_ref, sv_ref, gate_ref, mq_ref, mk_ref, mv_ref,
                   carry_ref, *, tm):
    x = x_ref[0]
    sh, sc = mod_ref[0, 0:1, :], mod_ref[0, 1:2, :]
    u = ((_rms(x) * g_ref[...]) * (1.0 + sc) + sh).astype(BF16)

    def seg(a, b):
        return _bdot(u, w_ref[:, a:b])

    lane = lax.broadcasted_iota(jnp.int32, (1, LANES), 1)
    cos, sina, sinb = cos_ref[0], sina_ref[0], sinb_ref[0]

    def gates(chunks):
        for c in chunks:
            a = 512 * c
            gate_ref[0, :, a:a + 512] = _sigmoid(seg(C_GATE + a, C_GATE + a + 512)).astype(BF16)

    def pack3(v):
        p0, p1, p2 = (p.astype(F32) for p in _split3(v))
        return jnp.where(lane < N_HEADS, p0,
                         jnp.where(lane < 2 * N_HEADS, pltpu.roll(p1, N_HEADS, 1),
                                   jnp.where(lane < 3 * N_HEADS, pltpu.roll(p2, 2 * N_HEADS, 1),
                                             0.0))).astype(BF16)

    def to_slots(compact, extra, out_ref):
        for h in range(N_HEADS):
            x = compact[:, LANES * (h // 2):LANES * (h // 2 + 1)]
            head = pltpu.roll(x, HEAD_DIM, 1) if h % 2 else x
            slot = slice(LANES * h, LANES * (h + 1))
            out_ref[0, :, slot] = jnp.where(lane < HEAD_DIM, head, extra[:, slot]).astype(BF16)

    forget_logits = seg(C_FF, IN_W)
    q_latent = seg(C_QL, C_KVL)
    kv_latent = seg(C_KVL, C_KR)
    rope_key = seg(C_KR, C_FF)
    log_f = pack3(_log_sigmoid(forget_logits + bff_ref[...]))
    cq = (_rms(q_latent) * gq_ref[...]).astype(BF16)
    ckv = (_rms(kv_latent) * gkv_ref[...]).astype(BF16)
    gates(range(0, 3))

    r = lax.broadcasted_iota(jnp.int32, (tm, tm), 0)
    c = lax.broadcasted_iota(jnp.int32, (tm, tm), 1)
    part = _bdot(jnp.where(r >= c, 1.0, 0.0).astype(BF16), log_f)
    q_all = _bdot(cq, wuq_ref[...])
    k_all = _bdot(ckv, wkn_ref[...])
    v_all = _bdot(ckv, wkv_ref[...])
    gates(range(3, 6))

    first_tile = pl.program_id(1) == 0
    cum = ((part + pltpu.roll(part, LANES - N_HEADS, 1))
           + pltpu.roll(part, LANES - 2 * N_HEADS, 1)) + jnp.where(first_tile, 0.0, carry_ref[...])
    carry_ref[...] = jnp.where(lane < N_HEADS, cum[tm - 1:tm, :], 0.0)
    placed = _bdot(pack3(cum), place_ref[...])
    first_aug = (lax.broadcasted_iota(jnp.int32, (1, SLOT_W), 1) & (LANES - 1)) < HEAD_DIM + N_PIECES
    fox_q = seg(C_FQ, C_FK) * QK_SCALE
    fox_k = seg(C_FK, C_FV)

    kr = _rope_slot(rope_key, cos, sina, sinb)
    for h in range(N_HEADS):
        slot = slice(LANES * h, LANES * (h + 1))
        q = _rope_slot(q_all[:, slot], cos, sina, sinb)
        mq_ref[0, :, slot] = (q * MLA_SCALE).astype(BF16)
        mk_ref[0, :, slot] = (k_all[:, slot] + kr).astype(BF16)
    mv_ref[0] = v_all.astype(BF16).T
    to_slots(fox_q, jnp.where(first_aug, qc_ref[...], placed), fq_ref)
    to_slots(fox_k, jnp.where(first_aug, placed, kc_ref[...]), fk_ref)

    sv_ref[0] = seg(C_SB + 1024, C_GATE).astype(BF16).T
    fv_ref[0] = seg(C_FV, C_SB).astype(BF16).T
    sb_ref[0, :, 0:512] = (seg(C_SB, C_SB + 512) * QK_SCALE).astype(BF16)
    sb_ref[0, :, 512:1024] = seg(C_SB + 512, C_SB + 1024).astype(BF16)


def _fox_placement():
    place = np.zeros((LANES, SLOT_W), np.float32)
    qc = np.zeros((1, SLOT_W), np.float32)
    kc = np.zeros((1, SLOT_W), np.float32)
    for h in range(N_HEADS):
        for p in range(N_PIECES):
            place[N_HEADS * p + h, LANES * h + HEAD_DIM + p] = 1.0
            place[N_HEADS * p + h, LANES * h + HEAD_DIM + N_PIECES + p] = 1.0
            qc[0, LANES * h + HEAD_DIM + p] = -1.0
            kc[0, LANES * h + HEAD_DIM + N_PIECES + p] = 1.0
    return jnp.asarray(place, BF16), jnp.asarray(qc), jnp.asarray(kc)


def _inproj(x, mod, weights, tables, *, layer, tm):
    b, s, d = x.shape
    cos, sina, sinb = tables
    place, qc, kc = _fox_placement()
    per_layer = lambda *shape: _resident_layer(shape, layer)
    row = lambda w: pl.BlockSpec((1, tm, w), lambda i, j: (i, j, 0))
    col = lambda w: pl.BlockSpec((1, w, tm), lambda i, j: (i, 0, j))
    outs = [(SLOT_W, False),
            (SLOT_W, False),
            (HEAD_W, True),
            (2 * HEAD_W, False),
            (HEAD_W, True),
            (3 * D_MODEL, False),
            (SLOT_W, False),
            (SLOT_W, False),
            (HEAD_W, True)]
    out_specs = [col(w) if tr else row(w) for w, tr in outs]
    out_shape = [jax.ShapeDtypeStruct((b, w, s) if tr else (b, s, w), BF16) for w, tr in outs]
    return pl.pallas_call(
        functools.partial(_inproj_kernel, tm=tm),
        grid=(b, s // tm),
        in_specs=[row(d),
                  pl.BlockSpec((1, 6, d), lambda i, j: (i, 0, 0)),
                  per_layer(1, d), per_layer(d, IN_W), per_layer(1, LANES),
                  per_layer(1, MLA_Q_RANK), per_layer(1, MLA_KV_RANK),
                  per_layer(MLA_Q_RANK, SLOT_W), per_layer(MLA_KV_RANK, SLOT_W),
                  per_layer(MLA_KV_RANK, HEAD_W),
                  row(LANES), row(LANES), row(LANES),
                  _resident((LANES, SLOT_W)), _resident((1, SLOT_W)), _resident((1, SLOT_W))],
        out_specs=out_specs,
        out_shape=out_shape,
        scratch_shapes=[pltpu.VMEM((1, LANES), F32)],
        compiler_params=_params(2),
        name="inproj",
    )(x, mod, *weights, cos, sina, sinb, place, qc, kc)


def _head_rows():
    row = lax.broadcasted_iota(jnp.int32, (LANES, 1), 0)
    return [(row >= HEAD_DIM * hh) & (row < HEAD_DIM * (hh + 1)) for hh in range(2)]


def _head_masks():
    lane = lax.broadcasted_iota(jnp.int32, (1, LANES), 1)
    return [(lane >= HEAD_DIM * hh) & (lane < HEAD_DIM * (hh + 1)) for hh in range(2)]


def _colmax(x, chunk=64):
    rows = x.shape[0]
    if rows > chunk and rows % chunk == 0:
        acc = x[:chunk]
        for a in range(chunk, rows, chunk):
            acc = jnp.maximum(acc, x[a:a + chunk])
        x, rows = acc, chunk
    while rows > 8 and rows % 16 == 0:
        rows //= 2
        x = jnp.maximum(x[:rows], x[rows:])
    return jnp.max(x, axis=0, keepdims=True)


def _pipeline(n, stages, unroll=PIPELINE_UNROLL, n_static=0):
    stages = [st if isinstance(st, tuple) else (k, st) for k, st in enumerate(stages)]
    depth = max(s for s, _ in stages) + 1

    def static_step(t):
        for s, stage in stages:
            if 0 <= t - s < n:
                stage(t - s, (t - s) % 2)

    first_full = min(max(depth - 1, n_static), n + depth - 1)
    n_full = max(n - first_full, 0)
    for t in range(first_full):
        static_step(t)
    n_loops = n_full // unroll
    if n_loops >= 1:
        def body(it, carry):
            t = first_full + unroll * it
            for u in range(unroll):
                for s, stage in stages:
                    stage(t + u - s, (first_full + u - s) % 2)
            return carry

        lax.fori_loop(0, n_loops, body, 0)
    for t in range(first_full + unroll * n_loops, n + depth - 1):
        static_step(t)


def _for_each_tile(n_tiles, fn, group=4):
    while n_tiles % group:
        group //= 2

    def body(i, carry):
        for u in range(group):
            fn(i * group + u)
        return carry

    lax.fori_loop(0, n_tiles // group, body, 0)


def _softmax_block_table(n_tiles, t):
    diag = [(i * t, i * t) for i in range(n_tiles)]
    below = [(i * t, j * t) for i in range(1, n_tiles) for j in range(i)]
    return jnp.asarray(np.array(diag + below, np.int32).T), len(diag), len(below)


def _softmax_flat_kernel(tab_ref, q_ref, k_ref, vt_ref, o_ref, acc_ref, m_ref, al_ref, top_ref,
                         s0_ref, s1_ref, p0_ref, p1_ref, *, t, n_diag, n_below, chunk_shift):
    own_rows = _head_rows()
    s_bufs, p_bufs = (s0_ref, s1_ref), (p0_ref, p1_ref)
    kio = lax.broadcasted_iota(jnp.int32, (t, 1), 0)
    last_visible = lax.broadcasted_iota(jnp.int32, (1, t), 1) | ((1 << chunk_shift) - 1)
    acc_ref[...] = jnp.zeros_like(acc_ref)
    m_ref[...] = jnp.full_like(m_ref, NEG)

    def item(i):
        return pl.multiple_of(tab_ref[0, i], t), pl.multiple_of(tab_ref[1, i], t)

    def scores(hh, i, par):
        q0, k0 = item(i)
        slot = slice(LANES * hh, LANES * (hh + 1))
        s = _dot_nt(k_ref[0, pl.ds(k0, t), slot], q_ref[0, pl.ds(q0, t), slot])
        if isinstance(i, int) and i < n_diag:
            s = jnp.where(kio <= last_visible, s, NEG)
        s_bufs[par][hh] = s
        top_ref[par, hh] = _colmax(s)

    def numerators(i, par):
        q0, _ = item(i)
        for hh in range(2):
            m_old = m_ref[hh, :, pl.ds(q0, t)]
            m_new = jnp.maximum(m_old, top_ref[par, hh])
            for a in range(0, t, ROW_CHUNK):
                rows = slice(a, a + ROW_CHUNK)
                p_bufs[par][hh, rows, :] = jnp.exp(s_bufs[par][hh, rows, :] - m_new).astype(BF16)
            al_ref[par, hh] = jnp.exp(m_old - m_new)
            m_ref[hh, :, pl.ds(q0, t)] = m_new

    def values(hh, i, par):
        q0, k0 = item(i)
        vt = vt_ref[0, :, pl.ds(k0, t)]
        v1 = jnp.where(own_rows[hh], vt, jnp.ones_like(vt))
        acc_ref[hh, :, pl.ds(q0, t)] = (acc_ref[hh, :, pl.ds(q0, t)] * al_ref[par, hh]
                                        + _bdot(v1, p_bufs[par][hh]))

    head = functools.partial
    _pipeline(n_diag + n_below,
              [(0, head(scores, 0)), (0, head(scores, 1)), (1, numerators),
               (2, head(values, 0)), (2, head(values, 1))], n_static=n_diag)

    sub = lax.broadcasted_iota(jnp.int32, (LANES, 1), 0)

    def finish(tile):
        q0 = pl.multiple_of(tile * t, t)
        outs = []
        for hh in range(2):
            acc = acc_ref[hh, :, pl.ds(q0, t)]
            denom = acc[HEAD_DIM * (1 - hh):HEAD_DIM * (1 - hh) + 1, :]
            outs.append(acc / denom)
        o_ref[0, pl.ds(q0, t), :] = jnp.where(sub < HEAD_DIM, outs[0], outs[1]).astype(o_ref.dtype).T

    _for_each_tile(n_diag, finish)


def _softmax_attention_flat(q, k, vt, *, t, chunk_shift, name):
    b, s, _ = k.shape
    table, n_diag, n_below = _softmax_block_table(s // t, t)
    grid_spec = pltpu.PrefetchScalarGridSpec(
        num_scalar_prefetch=1,
        grid=(b, N_HEADS // 2),
        in_specs=[pl.BlockSpec((1, s, 2 * LANES), lambda i, h, tab: (i, 0, h)),
                  pl.BlockSpec((1, s, 2 * LANES), lambda i, h, tab: (i, 0, h)),
                  pl.BlockSpec((1, LANES, s), lambda i, h, tab: (i, h, 0))],
        out_specs=pl.BlockSpec((1, s, LANES), lambda i, h, tab: (i, 0, h)),
        scratch_shapes=[pltpu.VMEM((2, LANES, s), F32),
                        pltpu.VMEM((2, 1, s), F32),
                        pltpu.VMEM((2, 2, 1, t), F32),
                        pltpu.VMEM((2, 2, 1, t), F32),
                        pltpu.VMEM((2, t, t), F32), pltpu.VMEM((2, t, t), F32),
                        pltpu.VMEM((2, t, t), BF16), pltpu.VMEM((2, t, t), BF16)])
    return pl.pallas_call(
        functools.partial(_softmax_flat_kernel, t=t, n_diag=n_diag, n_below=n_below,
                          chunk_shift=chunk_shift),
        grid_spec=grid_spec,
        out_shape=jax.ShapeDtypeStruct((b, s, HEAD_W), BF16),
        compiler_params=_params(2),
        name=name,
    )(table, q, k, vt)


def _sb_block_table(n_tiles, t):
    diag = [(i * t, i * t, i * t, i) for i in range(n_tiles)]
    below = [(i * t, (i - d) * t, i * t, i) for d in range(1, n_tiles) for i in range(d, n_tiles)]
    idle = [(0, 0, n_tiles * t, n_tiles)]
    return jnp.asarray(np.array(diag + below + idle, np.int32).T), len(diag), len(below)


def _sb_flat_kernel(tab_ref, q_ref, k_ref, vt_ref, o_ref, acc_ref, r_ref, tot_ref, d_ref, done_ref,
                    z0_ref, z1_ref, e0_ref, e1_ref, w0_ref, w1_ref, *, t, n_diag, n_below):
    in_head = _head_masks()
    own_rows = _head_rows()
    z_bufs, e_bufs, w_bufs = (z0_ref, z1_ref), (e0_ref, e1_ref), (w0_ref, w1_ref)
    kio = lax.broadcasted_iota(jnp.int32, (t, 1), 0)
    qio = lax.broadcasted_iota(jnp.int32, (1, t), 1)
    ss = lax.broadcasted_iota(jnp.int32, (t, t), 0)
    jj = lax.broadcasted_iota(jnp.int32, (t, t), 1)
    later = jnp.where(jj > ss, 1.0, 0.0).astype(BF16)
    acc_ref[...] = jnp.zeros_like(acc_ref)
    r_ref[...] = jnp.zeros_like(r_ref)
    for tile in range(n_diag + 1):
        done_ref[tile] = 0

    def stages(base, diag):
        def item(i):
            return (pl.multiple_of(tab_ref[0, base + i], t), pl.multiple_of(tab_ref[1, base + i], t),
                    pl.multiple_of(tab_ref[2, base + i], t), tab_ref[3, base + i])

        def scores(i, par):
            q0, k0, _, _ = item(i)
            k = k_ref[0, pl.ds(k0, t), :]
            q = q_ref[0, pl.ds(q0, t), :]
            for hh in range(2):
                qh = jnp.where(in_head[hh], q, jnp.zeros_like(q))
                z_bufs[par][hh] = _dot_nt(k, qh)

        def drops(i, par):
            for hh in range(2):
                for a in range(0, t, ROW_CHUNK):
                    rows = slice(a, a + ROW_CHUNK)
                    z = z_bufs[par][hh, rows, :]
                    neg_abs = pltpu.bitcast(pltpu.bitcast(z, jnp.uint32) | SIGN_BIT, F32)
                    drop = jnp.maximum(z, 0.0) + jnp.log(1.0 + jnp.exp(neg_abs))
                    log_beta = z - drop
                    if diag:
                        drop = jnp.where(kio[rows] < qio, drop, 0.0)
                        log_beta = jnp.where(kio[rows] < qio, log_beta, NEG)
                    d_ref[hh, rows, :] = drop.astype(BF16)
                    e_bufs[par][hh, rows, :] = log_beta
                after = _bdot(later, d_ref[hh])
                for a in range(0, t, ROW_CHUNK):
                    rows = slice(a, a + ROW_CHUNK)
                    e_bufs[par][hh, rows, :] = e_bufs[par][hh, rows, :] - after[rows]
                tot_ref[par, hh] = after[0:1, :] + d_ref[hh, 0:1, :].astype(F32)

        def weights(i, par):
            _, _, q0, tile = item(i)
            slack = None
            for hh in range(2):
                r_old = r_ref[hh, :, pl.ds(q0, t)]
                for a in range(0, t, ROW_CHUNK):
                    rows = slice(a, a + ROW_CHUNK)
                    w_bufs[par][hh, rows, :] = jnp.exp(e_bufs[par][hh, rows, :] - r_old).astype(BF16)
                r_new = r_old + tot_ref[par, hh]
                r_ref[hh, :, pl.ds(q0, t)] = r_new
                slack = r_new if slack is None else jnp.minimum(slack, r_new)
            if not diag:
                done_ref[tile] = (jnp.min(slack) > UNDERFLOW).astype(jnp.int32)

        def values(i, par):
            _, k0, q0, _ = item(i)
            vt = vt_ref[0, :, pl.ds(k0, t)]
            zero = jnp.zeros_like(vt)
            lhs = jnp.concatenate([jnp.where(own_rows[hh], vt, zero) for hh in range(2)], axis=1)
            acc_ref[:, pl.ds(q0, t)] = (acc_ref[:, pl.ds(q0, t)]
                                        + _bdot(lhs, w_bufs[par][...].reshape(2 * t, t)))

        return [scores, drops, weights, values]

    _pipeline(n_diag, stages(0, True))

    scores, drops, weights, values = stages(n_diag, False)

    def next_live(c):
        return lax.while_loop(
            lambda c: (c < n_below) & (done_ref[tab_ref[3, n_diag + c]] != 0), lambda c: c + 1, c)

    def after(c):
        return next_live(jnp.minimum(c + 1, n_below))

    def two_steps(state):
        a, b, i1, i2, i3 = state
        scores(a, 0)
        drops(i1, 1)
        weights(i2, 0)
        values(i3, 1)
        scores(b, 1)
        drops(a, 0)
        weights(i1, 1)
        values(i2, 0)
        a_next = after(b)
        return a_next, after(a_next), b, a, i1

    idle = jnp.int32(n_below)
    first = next_live(jnp.int32(0))
    lax.while_loop(lambda st: (st[0] < n_below) | (st[2] < n_below) | (st[3] < n_below)
                   | (st[4] < n_below), two_steps, (first, after(first), idle, idle, idle))

    def finish(tile):
        q0 = pl.multiple_of(tile * t, t)
        o_ref[0, pl.ds(q0, t), :] = acc_ref[:, pl.ds(q0, t)].astype(o_ref.dtype).T

    _for_each_tile(n_diag, finish)


def _sb_attention_flat(qk, vt, *, t):
    b, s, _ = qk.shape
    table, n_diag, n_below = _sb_block_table(s // t, t)
    grid_spec = pltpu.PrefetchScalarGridSpec(
        num_scalar_prefetch=1,
        grid=(b, N_HEADS // 2),
        in_specs=[pl.BlockSpec((1, s, LANES), lambda i, h, tab: (i, 0, h)),
                  pl.BlockSpec((1, s, LANES), lambda i, h, tab: (i, 0, 4 + h)),
                  pl.BlockSpec((1, LANES, s), lambda i, h, tab: (i, h, 0))],
        out_specs=pl.BlockSpec((1, s, LANES), lambda i, h, tab: (i, 0, h)),
        scratch_shapes=[pltpu.VMEM((LANES, s + t), F32),
                        pltpu.VMEM((2, 1, s + t), F32),
                        pltpu.VMEM((2, 2, 1, t), F32),
                        pltpu.VMEM((2, t, t), BF16),
                        pltpu.SMEM((s // t + 1,), jnp.int32),
                        pltpu.VMEM((2, t, t), F32), pltpu.VMEM((2, t, t), F32),
                        pltpu.VMEM((2, t, t), F32), pltpu.VMEM((2, t, t), F32),
                        pltpu.VMEM((2, t, t), BF16), pltpu.VMEM((2, t, t), BF16)])
    return pl.pallas_call(
        functools.partial(_sb_flat_kernel, t=t, n_diag=n_diag, n_below=n_below),
        grid_spec=grid_spec,
        out_shape=jax.ShapeDtypeStruct((b, s, HEAD_W), BF16),
        compiler_params=_params(2),
        name="sb_attention",
    )(table, qk, qk, vt)


def _merge_ffn_kernel(x_ref, yf_ref, ym_ref, ys_ref, gate_ref, mod_ref, wf_ref, wm_ref, ws_ref,
                      wo_ref, g_ref, wg_ref, wu_ref, wd_ref, gfin_ref, o_ref, *, final, fc):
    d = D_MODEL
    merged = (gate_ref[0, :, 0:d].astype(F32) * _bdot(yf_ref[0], wf_ref[...])
              + gate_ref[0, :, d:2 * d].astype(F32) * _bdot(ym_ref[0], wm_ref[...])
              + gate_ref[0, :, 2 * d:3 * d].astype(F32) * _bdot(ys_ref[0], ws_ref[...]))
    x = x_ref[0] + mod_ref[0, 2:3, :] * _bdot(merged.astype(BF16), wo_ref[...])

    sh, sc, gt = mod_ref[0, 3:4, :], mod_ref[0, 4:5, :], mod_ref[0, 5:6, :]
    u = ((_rms(x) * g_ref[...]) * (1.0 + sc) + sh).astype(BF16)
    acc = None
    for a in range(0, D_FF, fc):
        hg = _bdot(u, wg_ref[:, a:a + fc])
        hu = _bdot(u, wu_ref[:, a:a + fc])
        h = ((hg * _sigmoid(hg)) * hu).astype(BF16)
        part = _bdot(h, wd_ref[a:a + fc, :])
        acc = part if acc is None else acc + part
    out = x + gt * acc
    if final:
        out = _rms(out) * gfin_ref[...]
    o_ref[0] = out


def _merge_ffn(x, y_fox, y_mla, y_sb, gates, mod, weights, g_final, *, layer, tm, final):
    b, s, d = x.shape
    row = lambda w: pl.BlockSpec((1, tm, w), lambda i, j: (i, j, 0))
    per_layer = lambda *shape: _resident_layer(shape, layer)
    return pl.pallas_call(
        functools.partial(_merge_ffn_kernel, final=final, fc=D_FF // 2),
        grid=(b, s // tm),
        in_specs=[row(d), row(HEAD_W), row(HEAD_W), row(HEAD_W), row(3 * d),
                  pl.BlockSpec((1, 6, d), lambda i, j: (i, 0, 0)),
                  per_layer(HEAD_W, d), per_layer(HEAD_W, d), per_layer(HEAD_W, d),
                  per_layer(d, d), per_layer(1, d),
                  per_layer(d, D_FF), per_layer(d, D_FF), per_layer(D_FF, d),
                  _resident((1, d))],
        out_specs=row(d),
        out_shape=jax.ShapeDtypeStruct((b, s, d), F32),
        compiler_params=_params(2),
        name="merge_ffn",
    )(x, y_fox, y_mla, y_sb, gates, mod, *weights, g_final)


def _slot_cols(w, width_per_head, used):
    lead = w.shape[:-1]
    w = w.reshape(lead + (N_HEADS, width_per_head))[..., :used]
    pad = [(0, 0)] * (w.ndim - 1) + [(0, LANES - used)]
    return jnp.pad(w, pad).reshape(lead + (SLOT_W,))


def _layout_w_in(w):
    fox, ff, ql, kvl, kr, sb, gate = jnp.split(
        w, [1536, 1544, 1928, 2184, 2216, 3752], axis=-1)
    z = lambda n: jnp.zeros(w.shape[:-1] + (n,), w.dtype)
    kr_slot = jnp.concatenate([z(MLA_NOPE), kr, z(LANES - MLA_NOPE - MLA_ROPE)], axis=-1)
    ff_slot = jnp.concatenate([ff, z(LANES - N_HEADS)], axis=-1)
    return jnp.concatenate([fox, sb, gate, ql, kvl, kr_slot, ff_slot], axis=-1).astype(BF16)


def kernel(x, c, positions, g_mix, w_ada, b_ada, w_in, b_fox_f, g_mla_q, w_mla_uq, g_mla_kv,
           w_mla_ukv, w_o_fox, w_o_mla, w_o_sb, w_out, g_ffn, w_ffn_gate, w_ffn_up, w_ffn_down,
           g_final):
    b, s, d = x.shape
    depth = w_in.shape[0]
    tm = min(s, 512)
    tm_ffn = min(s, 512)
    ta = min(s, 512)
    tsb = min(s, 256)

    mod = _ada(c, w_ada, b_ada).reshape(depth, b, 6, d)
    tables = _rope_tables(positions)

    row_vec = lambda v: v.reshape(depth, 1, -1)
    wkv = w_mla_ukv.reshape(depth, MLA_KV_RANK, N_HEADS, 2 * HEAD_DIM)[..., MLA_NOPE:]
    inproj_weights = (
        row_vec(g_mix), _layout_w_in(w_in),
        row_vec(jnp.pad(b_fox_f, ((0, 0), (0, LANES - N_HEADS)))),
        row_vec(g_mla_q), row_vec(g_mla_kv),
        _slot_cols(w_mla_uq, MLA_NOPE + MLA_ROPE, MLA_NOPE + MLA_ROPE).astype(BF16),
        _slot_cols(w_mla_ukv, 2 * HEAD_DIM, MLA_NOPE).astype(BF16),
        wkv.reshape(depth, MLA_KV_RANK, HEAD_W).astype(BF16))
    merge_ffn_weights = (
        w_o_fox.astype(BF16), w_o_mla.astype(BF16), w_o_sb.astype(BF16), w_out.astype(BF16),
        row_vec(g_ffn), w_ffn_gate.astype(BF16), w_ffn_up.astype(BF16), w_ffn_down.astype(BF16))

    for l in range(depth):
        fox_q, fox_k, fox_vt, sb_qk, sb_vt, gates, mq, mk, mvt = _inproj(
            x, mod[l], inproj_weights, tables, layer=l, tm=tm)
        y_fox = _softmax_attention_flat(fox_q, fox_k, fox_vt, t=ta, chunk_shift=0,
                                        name="fox_attention")
        y_mla = _softmax_attention_flat(mq, mk, mvt, t=ta, chunk_shift=CHUNK_SHIFT,
                                        name="mla_attention")
        y_sb = _sb_attention_flat(sb_qk, sb_vt, t=tsb)
        x = _merge_ffn(x, y_fox, y_mla, y_sb, gates, mod[l], merge_ffn_weights,
                       g_final.reshape(1, d), layer=l, tm=tm_ffn, final=(l == depth - 1))
    return x
```

```python
import functools

import numpy as np
import jax
import jax.numpy as jnp
from jax import lax
from jax.experimental import pallas as pl
from jax.experimental.pallas import tpu as pltpu

F32 = jnp.float32
BF16 = jnp.bfloat16

D_MODEL = 1024
EPS = 1e-6
HEAD_DIM = 64
N_HEADS = 8
HEAD_W = N_HEADS * HEAD_DIM
CHUNK_SHIFT = 6
MLA_Q_RANK = 384
MLA_KV_RANK = 256
MLA_NOPE = 64
MLA_ROPE = 32
ROPE_HALF = MLA_ROPE // 2
ROPE_BASE = 10000.0
MLA_SCALE = (MLA_NOPE + MLA_ROPE) ** -0.5
QK_SCALE = HEAD_DIM ** -0.5
D_FF = 2816
LANES = 128
SLOT_W = N_HEADS * LANES
N_PIECES = 3

C_FQ = 0
C_FK = 512
C_FV = 1024
C_SB = 1536
C_GATE = 3072
C_QL = 6144
C_KVL = 6528
C_KR = 6784
C_FF = 6912
IN_W = 7040

ROW_CHUNK = 32
PIPELINE_UNROLL = 2
SIGN_BIT = np.uint32(0x80000000)
UNDERFLOW = 105.0
NEG = -1e30
VMEM_LIMIT = 56 * 1024 * 1024


def _params(n_grid):
    return pltpu.CompilerParams(dimension_semantics=("arbitrary",) * n_grid,
                                vmem_limit_bytes=VMEM_LIMIT)


def _resident(shape):
    nd = len(shape)
    return pl.BlockSpec(shape, lambda *_: (0,) * nd, pipeline_mode=pl.Buffered(1))


def _resident_layer(shape, layer):
    nd = len(shape)
    return pl.BlockSpec((pl.Squeezed(),) + tuple(shape), lambda *_: (layer,) + (0,) * nd,
                        pipeline_mode=pl.Buffered(1))


def _bdot(a, b):
    return jnp.dot(a, b, preferred_element_type=F32)


def _dot_nt(a, b):
    return lax.dot_general(a, b, (((1,), (1,)), ((), ())), preferred_element_type=F32)


def _split3(v):
    hi = v.astype(BF16)
    r = v - hi.astype(F32)
    mid = r.astype(BF16)
    lo = (r - mid.astype(F32)).astype(BF16)
    return hi, mid, lo


def _log_sigmoid(z):
    return jnp.minimum(z, 0.0) - jnp.log1p(jnp.exp(-jnp.abs(z)))


def _sigmoid(z):
    return 1.0 / (1.0 + jnp.exp(-z))


def _rms(x):
    return x * lax.rsqrt(jnp.mean(x * x, axis=-1, keepdims=True) + EPS)


def _ada_kernel(c_ref, w_ref, b_ref, o_ref):
    c = c_ref[...]
    cond = c * _sigmoid(c)
    chi, cmid, clo = _split3(cond)
    whi, wmid, wlo = _split3(w_ref[0])
    acc = _bdot(chi, whi) + (_bdot(chi, wmid) + _bdot(cmid, whi))
    acc = acc + (_bdot(chi, wlo) + _bdot(clo, whi) + _bdot(cmid, wmid))
    o_ref[0] = acc + b_ref[0]


def _ada(c, w_ada, b_ada):
    depth, d, n = w_ada.shape
    b = c.shape[0]
    tn = 1024
    return pl.pallas_call(
        _ada_kernel,
        grid=(depth, n // tn),
        in_specs=[pl.BlockSpec((b, d), lambda l, j: (0, 0)),
                  pl.BlockSpec((1, d, tn), lambda l, j: (l, 0, j)),
                  pl.BlockSpec((1, 1, tn), lambda l, j: (l, 0, j))],
        out_specs=pl.BlockSpec((1, b, tn), lambda l, j: (l, 0, j)),
        out_shape=jax.ShapeDtypeStruct((depth, b, n), F32),
        compiler_params=_params(2),
        name="adaln",
    )(c, w_ada, b_ada.reshape(depth, 1, n))


def _rope_kernel(pos_ref, inv_ref, cos_ref, sina_ref, sinb_ref):
    ang = pos_ref[0].astype(F32) * inv_ref[...]
    lane = lax.broadcasted_iota(jnp.int32, (1, LANES), 1)
    sin = jnp.sin(ang)
    cos_ref[0] = jnp.cos(ang)
    lo, mid, hi = MLA_NOPE, MLA_NOPE + ROPE_HALF, MLA_NOPE + MLA_ROPE
    sina_ref[0] = jnp.where((lane >= lo) & (lane < mid), -sin, 0.0)
    sinb_ref[0] = jnp.where((lane >= mid) & (lane < hi), sin, 0.0)


def _rope_tables(positions):
    b, s = positions.shape
    ts = min(s, 512)
    inv = ROPE_BASE ** (-jnp.arange(ROPE_HALF, dtype=F32) / ROPE_HALF)
    inv_slot = jnp.concatenate([jnp.zeros((MLA_NOPE,), F32), inv, inv,
                                jnp.zeros((LANES - MLA_NOPE - MLA_ROPE,), F32)]).reshape(1, LANES)
    spec = pl.BlockSpec((1, ts, LANES), lambda i, j: (i, j, 0))
    return pl.pallas_call(
        _rope_kernel,
        grid=(b, s // ts),
        in_specs=[pl.BlockSpec((1, ts, 1), lambda i, j: (i, j, 0)),
                  pl.BlockSpec((1, LANES), lambda i, j: (0, 0))],
        out_specs=[spec, spec, spec],
        out_shape=[jax.ShapeDtypeStruct((b, s, LANES), F32)] * 3,
        compiler_params=_params(2),
        name="rope_tables",
    )(positions.reshape(b, s, 1), inv_slot)


def _rope_slot(x, cos, sina, sinb):
    return (x * cos + pltpu.roll(x, LANES - ROPE_HALF, 1) * sina
            + pltpu.roll(x, ROPE_HALF, 1) * sinb)


def _inproj_kernel(x_ref, mod_ref, g_ref, w_ref, bff_ref, gq_ref, gkv_ref, wuq_ref, wkn_ref,
                   wkv_ref, cos_ref, sina_ref, sinb_ref, place_ref, qc_ref, kc_ref,
                   fq_ref, fk_ref, fv_ref, sb_ref, sv_ref, gate_ref, mq_ref, mk_ref, mv_ref,
                   carry_ref, *, tm):
    x = x_ref[0]
    sh, sc = mod_ref[0, 0:1, :], mod_ref[0, 1:2, :]
    u = ((_rms(x) * g_ref[...]) * (1.0 + sc) + sh).astype(BF16)

    def seg(a, b):
        return _bdot(u, w_ref[:, a:b])

    lane = lax.broadcasted_iota(jnp.int32, (1, LANES), 1)
    cos, sina, sinb = cos_ref[0], sina_ref[0], sinb_ref[0]

    def gates(chunks):
        for c in chunks:
            a = 512 * c
            gate_ref[0, :, a:a + 512] = _sigmoid(seg(C_GATE + a, C_GATE + a + 512)).astype(BF16)

    def pack3(v):
        p0, p1, p2 = (p.astype(F32) for p in _split3(v))
        return jnp.where(lane < N_HEADS, p0,
                         jnp.where(lane < 2 * N_HEADS, pltpu.roll(p1, N_HEADS, 1),
                                   jnp.where(lane < 3 * N_HEADS, pltpu.roll(p2, 2 * N_HEADS, 1),
                                             0.0))).astype(BF16)

    def to_slots(compact, extra, out_ref):
        for h in range(N_HEADS):
            x = compact[:, LANES * (h // 2):LANES * (h // 2 + 1)]
            head = pltpu.roll(x, HEAD_DIM, 1) if h % 2 else x
            slot = slice(LANES * h, LANES * (h + 1))
            out_ref[0, :, slot] = jnp.where(lane < HEAD_DIM, head, extra[:, slot]).astype(BF16)

    forget_logits = seg(C_FF, IN_W)
    q_latent = seg(C_QL, C_KVL)
    kv_latent = seg(C_KVL, C_KR)
    rope_key = seg(C_KR, C_FF)
    log_f = pack3(_log_sigmoid(forget_logits + bff_ref[...]))
    cq = (_rms(q_latent) * gq_ref[...]).astype(BF16)
    ckv = (_rms(kv_latent) * gkv_ref[...]).astype(BF16)
    gates(range(0, 3))

    r = lax.broadcasted_iota(jnp.int32, (tm, tm), 0)
    c = lax.broadcasted_iota(jnp.int32, (tm, tm), 1)
    part = _bdot(jnp.where(r >= c, 1.0, 0.0).astype(BF16), log_f)
    q_all = _bdot(cq, wuq_ref[...])
    k_all = _bdot(ckv, wkn_ref[...])
    v_all = _bdot(ckv, wkv_ref[...])
    gates(range(3, 6))

    first_tile = pl.program_id(1) == 0
    cum = ((part + pltpu.roll(part, LANES - N_HEADS, 1))
           + pltpu.roll(part, LANES - 2 * N_HEADS, 1)) + jnp.where(first_tile, 0.0, carry_ref[...])
    carry_ref[...] = jnp.where(lane < N_HEADS, cum[tm - 1:tm, :], 0.0)
    placed = _bdot(pack3(cum), place_ref[...])
    first_aug = (lax.broadcasted_iota(jnp.int32, (1, SLOT_W), 1) & (LANES - 1)) < HEAD_DIM + N_PIECES
    fox_q = seg(C_FQ, C_FK) * QK_SCALE
    fox_k = seg(C_FK, C_FV)

    kr = _rope_slot(rope_key, cos, sina, sinb)
    for h in range(N_HEADS):
        slot = slice(LANES * h, LANES * (h + 1))
        q = _rope_slot(q_all[:, slot], cos, sina, sinb)
        mq_ref[0, :, slot] = (q * MLA_SCALE).astype(BF16)
        mk_ref[0, :, slot] = (k_all[:, slot] + kr).astype(BF16)
    mv_ref[0] = v_all.astype(BF16).T
    to_slots(fox_q, jnp.where(first_aug, qc_ref[...], placed), fq_ref)
    to_slots(fox_k, jnp.where(first_aug, placed, kc_ref[...]), fk_ref)

    sv_ref[0] = seg(C_SB + 1024, C_GATE).astype(BF16).T
    fv_ref[0] = seg(C_FV, C_SB).astype(BF16).T
    sb_ref[0, :, 0:512] = (seg(C_SB, C_SB + 512) * QK_SCALE).astype(BF16)
    sb_ref[0, :, 512:1024] = seg(C_SB + 512, C_SB + 1024).astype(BF16)


def _fox_placement():
    place = np.zeros((LANES, SLOT_W), np.float32)
    qc = np.zeros((1, SLOT_W), np.float32)
    kc = np.zeros((1, SLOT_W), np.float32)
    for h in range(N_HEADS):
        for p in range(N_PIECES):
            place[N_HEADS * p + h, LANES * h + HEAD_DIM + p] = 1.0
            place[N_HEADS * p + h, LANES * h + HEAD_DIM + N_PIECES + p] = 1.0
            qc[0, LANES * h + HEAD_DIM + p] = -1.0
            kc[0, LANES * h + HEAD_DIM + N_PIECES + p] = 1.0
    return jnp.asarray(place, BF16), jnp.asarray(qc), jnp.asarray(kc)


def _inproj(x, mod, weights, tables, *, layer, tm):
    b, s, d = x.shape
    cos, sina, sinb = tables
    place, qc, kc = _fox_placement()
    per_layer = lambda *shape: _resident_layer(shape, layer)
    row = lambda w: pl.BlockSpec((1, tm, w), lambda i, j: (i, j, 0))
    col = lambda w: pl.BlockSpec((1, w, tm), lambda i, j: (i, 0, j))
    outs = [(SLOT_W, False),
            (SLOT_W, False),
            (HEAD_W, True),
            (2 * HEAD_W, False),
            (HEAD_W, True),
            (3 * D_MODEL, False),
            (SLOT_W, False),
            (SLOT_W, False),
            (HEAD_W, True)]
    out_specs = [col(w) if tr else row(w) for w, tr in outs]
    out_shape = [jax.ShapeDtypeStruct((b, w, s) if tr else (b, s, w), BF16) for w, tr in outs]
    return pl.pallas_call(
        functools.partial(_inproj_kernel, tm=tm),
        grid=(b, s // tm),
        in_specs=[row(d),
                  pl.BlockSpec((1, 6, d), lambda i, j: (i, 0, 0)),
                  per_layer(1, d), per_layer(d, IN_W), per_layer(1, LANES),
                  per_layer(1, MLA_Q_RANK), per_layer(1, MLA_KV_RANK),
                  per_layer(MLA_Q_RANK, SLOT_W), per_layer(MLA_KV_RANK, SLOT_W),
                  per_layer(MLA_KV_RANK, HEAD_W),
                  row(LANES), row(LANES), row(LANES),
                  _resident((LANES, SLOT_W)), _resident((1, SLOT_W)), _resident((1, SLOT_W))],
        out_specs=out_specs,
        out_shape=out_shape,
        scratch_shapes=[pltpu.VMEM((1, LANES), F32)],
        compiler_params=_params(2),
        name="inproj",
    )(x, mod, *weights, cos, sina, sinb, place, qc, kc)


def _head_rows():
    row = lax.broadcasted_iota(jnp.int32, (LANES, 1), 0)
    return [(row >= HEAD_DIM * hh) & (row < HEAD_DIM * (hh + 1)) for hh in range(2)]


def _head_masks():
    lane = lax.broadcasted_iota(jnp.int32, (1, LANES), 1)
    return [(lane >= HEAD_DIM * hh) & (lane < HEAD_DIM * (hh + 1)) for hh in range(2)]


def _colmax(x, chunk=64):
    rows = x.shape[0]
    if rows > chunk and rows % chunk == 0:
        acc = x[:chunk]
        for a in range(chunk, rows, chunk):
            acc = jnp.maximum(acc, x[a:a + chunk])
        x, rows = acc, chunk
    while rows > 8 and rows % 16 == 0:
        rows //= 2
        x = jnp.maximum(x[:rows], x[rows:])
    return jnp.max(x, axis=0, keepdims=True)


def _pipeline(n, stages, unroll=PIPELINE_UNROLL, n_static=0):
    stages = [st if isinstance(st, tuple) else (k, st) for k, st in enumerate(stages)]
    depth = max(s for s, _ in stages) + 1

    def static_step(t):
        for s, stage in stages:
            if 0 <= t - s < n:
                stage(t - s, (t - s) % 2)

    first_full = min(max(depth - 1, n_static), n + depth - 1)
    n_full = max(n - first_full, 0)
    for t in range(first_full):
        static_step(t)
    n_loops = n_full // unroll
    if n_loops >= 1:
        def body(it, carry):
            t = first_full + unroll * it
            for u in range(unroll):
                for s, stage in stages:
                    stage(t + u - s, (first_full + u - s) % 2)
            return carry

        lax.fori_loop(0, n_loops, body, 0)
    for t in range(first_full + unroll * n_loops, n + depth - 1):
        static_step(t)


def _for_each_tile(n_tiles, fn, group=4):
    while n_tiles % group:
        group //= 2

    def body(i, carry):
        for u in range(group):
            fn(i * group + u)
        return carry

    lax.fori_loop(0, n_tiles // group, body, 0)


def _softmax_block_table(n_tiles, t):
    diag = [(i * t, i * t) for i in range(n_tiles)]
    below = [(i * t, j * t) for i in range(1, n_tiles) for j in range(i)]
    return jnp.asarray(np.array(diag + below, np.int32).T), len(diag), len(below)


def _softmax_flat_kernel(tab_ref, q_ref, k_ref, vt_ref, o_ref, acc_ref, m_ref, al_ref, top_ref,
                         s0_ref, s1_ref, p0_ref, p1_ref, *, t, n_diag, n_below, chunk_shift):
    own_rows = _head_rows()
    s_bufs, p_bufs = (s0_ref, s1_ref), (p0_ref, p1_ref)
    kio = lax.broadcasted_iota(jnp.int32, (t, 1), 0)
    last_visible = lax.broadcasted_iota(jnp.int32, (1, t), 1) | ((1 << chunk_shift) - 1)
    def item(i):
        return pl.multiple_of(tab_ref[0, i], t), pl.multiple_of(tab_ref[1, i], t)

    def is_diag(i):
        return isinstance(i, int) and i < n_diag

    def scores(hh, i, par):
        q0, k0 = item(i)
        slot = slice(LANES * hh, LANES * (hh + 1))
        s = _dot_nt(k_ref[0, pl.ds(k0, t), slot], q_ref[0, pl.ds(q0, t), slot])
        if is_diag(i):
            s = jnp.where(kio <= last_visible, s, NEG)
        s_bufs[par][hh] = s
        top_ref[par, hh] = _colmax(s)

    def numerators(i, par):
        q0, _ = item(i)
        for hh in range(2):
            m_new = top_ref[par, hh]
            if not is_diag(i):
                m_old = m_ref[hh, :, pl.ds(q0, t)]
                m_new = jnp.maximum(m_old, m_new)
                al_ref[par, hh] = jnp.exp(m_old - m_new)
            for a in range(0, t, ROW_CHUNK):
                rows = slice(a, a + ROW_CHUNK)
                p_bufs[par][hh, rows, :] = jnp.exp(s_bufs[par][hh, rows, :] - m_new).astype(BF16)
            m_ref[hh, :, pl.ds(q0, t)] = m_new

    def values(hh, i, par):
        q0, k0 = item(i)
        vt = vt_ref[0, :, pl.ds(k0, t)]
        v1 = jnp.where(own_rows[hh], vt, jnp.ones_like(vt))
        pv = _bdot(v1, p_bufs[par][hh])
        if not is_diag(i):
            pv = acc_ref[hh, :, pl.ds(q0, t)] * al_ref[par, hh] + pv
        acc_ref[hh, :, pl.ds(q0, t)] = pv

    head = functools.partial
    stages = [(0, head(scores, 0)), (0, head(scores, 1)), (1, numerators),
              (2, head(values, 0)), (2, head(values, 1))]
    _pipeline(n_diag + n_below, stages, n_static=n_diag + max(s for s, _ in stages))

    sub = lax.broadcasted_iota(jnp.int32, (LANES, 1), 0)

    def finish(tile):
        q0 = pl.multiple_of(tile * t, t)
        outs = []
        for hh in range(2):
            acc = acc_ref[hh, :, pl.ds(q0, t)]
            denom = acc[HEAD_DIM * (1 - hh):HEAD_DIM * (1 - hh) + 1, :]
            outs.append(acc / denom)
        o_ref[0, pl.ds(q0, t), :] = jnp.where(sub < HEAD_DIM, outs[0], outs[1]).astype(o_ref.dtype).T

    _for_each_tile(n_diag, finish)


def _softmax_attention_flat(q, k, vt, *, t, chunk_shift, name):
    b, s, _ = k.shape
    table, n_diag, n_below = _softmax_block_table(s // t, t)
    grid_spec = pltpu.PrefetchScalarGridSpec(
        num_scalar_prefetch=1,
        grid=(b, N_HEADS // 2),
        in_specs=[pl.BlockSpec((1, s, 2 * LANES), lambda i, h, tab: (i, 0, h)),
                  pl.BlockSpec((1, s, 2 * LANES), lambda i, h, tab: (i, 0, h)),
                  pl.BlockSpec((1, LANES, s), lambda i, h, tab: (i, h, 0))],
        out_specs=pl.BlockSpec((1, s, LANES), lambda i, h, tab: (i, 0, h)),
        scratch_shapes=[pltpu.VMEM((2, LANES, s), F32),
                        pltpu.VMEM((2, 1, s), F32),
                        pltpu.VMEM((2, 2, 1, t), F32),
                        pltpu.VMEM((2, 2, 1, t), F32),
                        pltpu.VMEM((2, t, t), F32), pltpu.VMEM((2, t, t), F32),
                        pltpu.VMEM((2, t, t), BF16), pltpu.VMEM((2, t, t), BF16)])
    return pl.pallas_call(
        functools.partial(_softmax_flat_kernel, t=t, n_diag=n_diag, n_below=n_below,
                          chunk_shift=chunk_shift),
        grid_spec=grid_spec,
        out_shape=jax.ShapeDtypeStruct((b, s, HEAD_W), BF16),
        compiler_params=_params(2),
        name=name,
    )(table, q, k, vt)


def _sb_block_table(n_tiles, t):
    diag = [(i * t, i * t, i * t, i) for i in range(n_tiles)]
    below = [(i * t, (i - d) * t, i * t, i) for d in range(1, n_tiles) for i in range(d, n_tiles)]
    idle = [(0, 0, n_tiles * t, n_tiles)]
    return jnp.asarray(np.array(diag + below + idle, np.int32).T), len(diag), len(below)


def _sb_flat_kernel(tab_ref, q_ref, k_ref, vt_ref, o_ref, acc_ref, r_ref, tot_ref, d_ref, done_ref,
                    z0_ref, z1_ref, e0_ref, e1_ref, w0_ref, w1_ref, *, t, n_diag, n_below):
    in_head = _head_masks()
    own_rows = _head_rows()
    z_bufs, e_bufs, w_bufs = (z0_ref, z1_ref), (e0_ref, e1_ref), (w0_ref, w1_ref)
    kio = lax.broadcasted_iota(jnp.int32, (t, 1), 0)
    qio = lax.broadcasted_iota(jnp.int32, (1, t), 1)
    ss = lax.broadcasted_iota(jnp.int32, (t, t), 0)
    jj = lax.broadcasted_iota(jnp.int32, (t, t), 1)
    later = jnp.where(jj > ss, 1.0, 0.0).astype(BF16)
    acc_ref[...] = jnp.zeros_like(acc_ref)
    r_ref[...] = jnp.zeros_like(r_ref)
    for tile in range(n_diag + 1):
        done_ref[tile] = 0

    def stages(base, diag):
        def item(i):
            return (pl.multiple_of(tab_ref[0, base + i], t), pl.multiple_of(tab_ref[1, base + i], t),
                    pl.multiple_of(tab_ref[2, base + i], t), tab_ref[3, base + i])

        def scores(i, par):
            q0, k0, _, _ = item(i)
            k = k_ref[0, pl.ds(k0, t), :]
            q = q_ref[0, pl.ds(q0, t), :]
            for hh in range(2):
                qh = jnp.where(in_head[hh], q, jnp.zeros_like(q))
                z_bufs[par][hh] = _dot_nt(k, qh)

        def drops(i, par):
            for hh in range(2):
                for a in range(0, t, ROW_CHUNK):
                    rows = slice(a, a + ROW_CHUNK)
                    z = z_bufs[par][hh, rows, :]
                    neg_abs = pltpu.bitcast(pltpu.bitcast(z, jnp.uint32) | SIGN_BIT, F32)
                    drop = jnp.maximum(z, 0.0) + jnp.log(1.0 + jnp.exp(neg_abs))
                    log_beta = z - drop
                    if diag:
                        drop = jnp.where(kio[rows] < qio, drop, 0.0)
                        log_beta = jnp.where(kio[rows] < qio, log_beta, NEG)
                    d_ref[hh, rows, :] = drop.astype(BF16)
                    e_bufs[par][hh, rows, :] = log_beta
                after = _bdot(later, d_ref[hh])
                for a in range(0, t, ROW_CHUNK):
                    rows = slice(a, a + ROW_CHUNK)
                    e_bufs[par][hh, rows, :] = e_bufs[par][hh, rows, :] - after[rows]
                tot_ref[par, hh] = after[0:1, :] + d_ref[hh, 0:1, :].astype(F32)

        def weights(i, par):
            _, _, q0, tile = item(i)
            slack = None
            for hh in range(2):
                r_old = r_ref[hh, :, pl.ds(q0, t)]
                for a in range(0, t, ROW_CHUNK):
                    rows = slice(a, a + ROW_CHUNK)
                    w_bufs[par][hh, rows, :] = jnp.exp(e_bufs[par][hh, rows, :] - r_old).astype(BF16)
                r_new = r_old + tot_ref[par, hh]
                r_ref[hh, :, pl.ds(q0, t)] = r_new
                slack = r_new if slack is None else jnp.minimum(slack, r_new)
            if not diag:
                done_ref[tile] = (jnp.min(slack) > UNDERFLOW).astype(jnp.int32)

        def values(i, par):
            _, k0, q0, _ = item(i)
            vt = vt_ref[0, :, pl.ds(k0, t)]
            zero = jnp.zeros_like(vt)
            lhs = jnp.concatenate([jnp.where(own_rows[hh], vt, zero) for hh in range(2)], axis=1)
            acc_ref[:, pl.ds(q0, t)] = (acc_ref[:, pl.ds(q0, t)]
                                        + _bdot(lhs, w_bufs[par][...].reshape(2 * t, t)))

        return [scores, drops, weights, values]

    _pipeline(n_diag, stages(0, True))

    scores, drops, weights, values = stages(n_diag, False)

    def next_live(c):
        return lax.while_loop(
            lambda c: (c < n_below) & (done_ref[tab_ref[3, n_diag + c]] != 0), lambda c: c + 1, c)

    def after(c):
        return next_live(jnp.minimum(c + 1, n_below))

    def two_steps(state):
        a, b, i1, i2, i3 = state
        scores(a, 0)
        drops(i1, 1)
        weights(i2, 0)
        values(i3, 1)
        scores(b, 1)
        drops(a, 0)
        weights(i1, 1)
        values(i2, 0)
        a_next = after(b)
        return a_next, after(a_next), b, a, i1

    idle = jnp.int32(n_below)
    first = next_live(jnp.int32(0))
    lax.while_loop(lambda st: (st[0] < n_below) | (st[2] < n_below) | (st[3] < n_below)
                   | (st[4] < n_below), two_steps, (first, after(first), idle, idle, idle))

    def finish(tile):
        q0 = pl.multiple_of(tile * t, t)
        o_ref[0, pl.ds(q0, t), :] = acc_ref[:, pl.ds(q0, t)].astype(o_ref.dtype).T

    _for_each_tile(n_diag, finish)


def _sb_attention_flat(qk, vt, *, t):
    b, s, _ = qk.shape
    table, n_diag, n_below = _sb_block_table(s // t, t)
    grid_spec = pltpu.PrefetchScalarGridSpec(
        num_scalar_prefetch=1,
        grid=(b, N_HEADS // 2),
        in_specs=[pl.BlockSpec((1, s, LANES), lambda i, h, tab: (i, 0, h)),
                  pl.BlockSpec((1, s, LANES), lambda i, h, tab: (i, 0, 4 + h)),
                  pl.BlockSpec((1, LANES, s), lambda i, h, tab: (i, h, 0))],
        out_specs=pl.BlockSpec((1, s, LANES), lambda i, h, tab: (i, 0, h)),
        scratch_shapes=[pltpu.VMEM((LANES, s + t), F32),
                        pltpu.VMEM((2, 1, s + t), F32),
                        pltpu.VMEM((2, 2, 1, t), F32),
                        pltpu.VMEM((2, t, t), BF16),
                        pltpu.SMEM((s // t + 1,), jnp.int32),
                        pltpu.VMEM((2, t, t), F32), pltpu.VMEM((2, t, t), F32),
                        pltpu.VMEM((2, t, t), F32), pltpu.VMEM((2, t, t), F32),
                        pltpu.VMEM((2, t, t), BF16), pltpu.VMEM((2, t, t), BF16)])
    return pl.pallas_call(
        functools.partial(_sb_flat_kernel, t=t, n_diag=n_diag, n_below=n_below),
        grid_spec=grid_spec,
        out_shape=jax.ShapeDtypeStruct((b, s, HEAD_W), BF16),
        compiler_params=_params(2),
        name="sb_attention",
    )(table, qk, qk, vt)


def _merge_ffn_kernel(x_ref, yf_ref, ym_ref, ys_ref, gate_ref, mod_ref, wf_ref, wm_ref, ws_ref,
                      wo_ref, g_ref, wg_ref, wu_ref, wd_ref, gfin_ref, o_ref, *, final, fc):
    d = D_MODEL
    merged = (gate_ref[0, :, 0:d].astype(F32) * _bdot(yf_ref[0], wf_ref[...])
              + gate_ref[0, :, d:2 * d].astype(F32) * _bdot(ym_ref[0], wm_ref[...])
              + gate_ref[0, :, 2 * d:3 * d].astype(F32) * _bdot(ys_ref[0], ws_ref[...]))
    x = x_ref[0] + mod_ref[0, 2:3, :] * _bdot(merged.astype(BF16), wo_ref[...])

    sh, sc, gt = mod_ref[0, 3:4, :], mod_ref[0, 4:5, :], mod_ref[0, 5:6, :]
    u = ((_rms(x) * g_ref[...]) * (1.0 + sc) + sh).astype(BF16)
    acc = None
    for a in range(0, D_FF, fc):
        hg = _bdot(u, wg_ref[:, a:a + fc])
        hu = _bdot(u, wu_ref[:, a:a + fc])
        h = ((hg * _sigmoid(hg)) * hu).astype(BF16)
        part = _bdot(h, wd_ref[a:a + fc, :])
        acc = part if acc is None else acc + part
    out = x + gt * acc
    if final:
        out = _rms(out) * gfin_ref[...]
    o_ref[0] = out


def _merge_ffn(x, y_fox, y_mla, y_sb, gates, mod, weights, g_final, *, layer, tm, final):
    b, s, d = x.shape
    row = lambda w: pl.BlockSpec((1, tm, w), lambda i, j: (i, j, 0))
    per_layer = lambda *shape: _resident_layer(shape, layer)
    return pl.pallas_call(
        functools.partial(_merge_ffn_kernel, final=final, fc=D_FF // 2),
        grid=(b, s // tm),
        in_specs=[row(d), row(HEAD_W), row(HEAD_W), row(HEAD_W), row(3 * d),
                  pl.BlockSpec((1, 6, d), lambda i, j: (i, 0, 0)),
                  per_layer(HEAD_W, d), per_layer(HEAD_W, d), per_layer(HEAD_W, d),
                  per_layer(d, d), per_layer(1, d),
                  per_layer(d, D_FF), per_layer(d, D_FF), per_layer(D_FF, d),
                  _resident((1, d))],
        out_specs=row(d),
        out_shape=jax.ShapeDtypeStruct((b, s, d), F32),
        compiler_params=_params(2),
        name="merge_ffn",
    )(x, y_fox, y_mla, y_sb, gates, mod, *weights, g_final)


def _slot_cols(w, width_per_head, used):
    lead = w.shape[:-1]
    w = w.reshape(lead + (N_HEADS, width_per_head))[..., :used]
    pad = [(0, 0)] * (w.ndim - 1) + [(0, LANES - used)]
    return jnp.pad(w, pad).reshape(lead + (SLOT_W,))


def _layout_w_in(w):
    fox, ff, ql, kvl, kr, sb, gate = jnp.split(
        w, [1536, 1544, 1928, 2184, 2216, 3752], axis=-1)
    z = lambda n: jnp.zeros(w.shape[:-1] + (n,), w.dtype)
    kr_slot = jnp.concatenate([z(MLA_NOPE), kr, z(LANES - MLA_NOPE - MLA_ROPE)], axis=-1)
    ff_slot = jnp.concatenate([ff, z(LANES - N_HEADS)], axis=-1)
    return jnp.concatenate([fox, sb, gate, ql, kvl, kr_slot, ff_slot], axis=-1).astype(BF16)


def kernel(x, c, positions, g_mix, w_ada, b_ada, w_in, b_fox_f, g_mla_q, w_mla_uq, g_mla_kv,
           w_mla_ukv, w_o_fox, w_o_mla, w_o_sb, w_out, g_ffn, w_ffn_gate, w_ffn_up, w_ffn_down,
           g_final):
    b, s, d = x.shape
    depth = w_in.shape[0]
    tm = min(s, 512)
    tm_ffn = min(s, 512)
    ta = min(s, 512)
    tsb = min(s, 256)

    mod = _ada(c, w_ada, b_ada).reshape(depth, b, 6, d)
    tables = _rope_tables(positions)

    row_vec = lambda v: v.reshape(depth, 1, -1)
    wkv = w_mla_ukv.reshape(depth, MLA_KV_RANK, N_HEADS, 2 * HEAD_DIM)[..., MLA_NOPE:]
    inproj_weights = (
        row_vec(g_mix), _layout_w_in(w_in),
        row_vec(jnp.pad(b_fox_f, ((0, 0), (0, LANES - N_HEADS)))),
        row_vec(g_mla_q), row_vec(g_mla_kv),
        _slot_cols(w_mla_uq, MLA_NOPE + MLA_ROPE, MLA_NOPE + MLA_ROPE).astype(BF16),
        _slot_cols(w_mla_ukv, 2 * HEAD_DIM, MLA_NOPE).astype(BF16),
        wkv.reshape(depth, MLA_KV_RANK, HEAD_W).astype(BF16))
    merge_ffn_weights = (
        w_o_fox.astype(BF16), w_o_mla.astype(BF16), w_o_sb.astype(BF16), w_out.astype(BF16),
        row_vec(g_ffn), w_ffn_gate.astype(BF16), w_ffn_up.astype(BF16), w_ffn_down.astype(BF16))

    for l in range(depth):
        fox_q, fox_k, fox_vt, sb_qk, sb_vt, gates, mq, mk, mvt = _inproj(
            x, mod[l], inproj_weights, tables, layer=l, tm=tm)
        y_fox = _softmax_attention_flat(fox_q, fox_k, fox_vt, t=ta, chunk_shift=0,
                                        name="fox_attention")
        y_mla = _softmax_attention_flat(mq, mk, mvt, t=ta, chunk_shift=CHUNK_SHIFT,
                                        name="mla_attention")
        y_sb = _sb_attention_flat(sb_qk, sb_vt, t=tsb)
        x = _merge_ffn(x, y_fox, y_mla, y_sb, gates, mod[l], merge_ffn_weights,
                       g_final.reshape(1, d), layer=l, tm=tm_ffn, final=(l == depth - 1))
    return x
```

```python
import functools

import numpy as np
import jax
import jax.numpy as jnp
from jax import lax
from jax.experimental import pallas as pl
from jax.experimental.pallas import tpu as pltpu

F32 = jnp.float32
BF16 = jnp.bfloat16

D_MODEL = 1024
EPS = 1e-6
HEAD_DIM = 64
N_HEADS = 8
HEAD_W = N_HEADS * HEAD_DIM
CHUNK_SHIFT = 6
MLA_Q_RANK = 384
MLA_KV_RANK = 256
MLA_NOPE = 64
MLA_ROPE = 32
ROPE_HALF = MLA_ROPE // 2
ROPE_BASE = 10000.0
MLA_SCALE = (MLA_NOPE + MLA_ROPE) ** -0.5
QK_SCALE = HEAD_DIM ** -0.5
D_FF = 2816
LANES = 128
SLOT_W = N_HEADS * LANES
N_PIECES = 3

C_FQ = 0
C_FK = 512
C_FV = 1024
C_SB = 1536
C_GATE = 3072
C_QL = 6144
C_KVL = 6528
C_KR = 6784
C_FF = 6912
IN_W = 7040

ROW_CHUNK = 32
PIPELINE_UNROLL = 2
SIGN_BIT = np.uint32(0x80000000)
UNDERFLOW = 105.0
NEG = -1e30
VMEM_LIMIT = 56 * 1024 * 1024


def _params(n_grid):
    return pltpu.CompilerParams(dimension_semantics=("arbitrary",) * n_grid,
                                vmem_limit_bytes=VMEM_LIMIT)


def _resident(shape):
    nd = len(shape)
    return pl.BlockSpec(shape, lambda *_: (0,) * nd, pipeline_mode=pl.Buffered(1))


def _resident_layer(shape, layer):
    nd = len(shape)
    return pl.BlockSpec((pl.Squeezed(),) + tuple(shape), lambda *_: (layer,) + (0,) * nd,
                        pipeline_mode=pl.Buffered(1))


def _bdot(a, b):
    return jnp.dot(a, b, preferred_element_type=F32)


def _dot_nt(a, b):
    return lax.dot_general(a, b, (((1,), (1,)), ((), ())), preferred_element_type=F32)


def _split3(v):
    hi = v.astype(BF16)
    r = v - hi.astype(F32)
    mid = r.astype(BF16)
    lo = (r - mid.astype(F32)).astype(BF16)
    return hi, mid, lo


def _log_sigmoid(z):
    return jnp.minimum(z, 0.0) - jnp.log1p(jnp.exp(-jnp.abs(z)))


def _sigmoid(z):
    return 1.0 / (1.0 + jnp.exp(-z))


def _rms(x):
    return x * lax.rsqrt(jnp.mean(x * x, axis=-1, keepdims=True) + EPS)


def _ada_kernel(c_ref, w_ref, b_ref, o_ref):
    c = c_ref[...]
    cond = c * _sigmoid(c)
    chi, cmid, clo = _split3(cond)
    whi, wmid, wlo = _split3(w_ref[0])
    acc = _bdot(chi, whi) + (_bdot(chi, wmid) + _bdot(cmid, whi))
    acc = acc + (_bdot(chi, wlo) + _bdot(clo, whi) + _bdot(cmid, wmid))
    o_ref[0] = acc + b_ref[0]


def _ada(c, w_ada, b_ada):
    depth, d, n = w_ada.shape
    b = c.shape[0]
    tn = 1024
    return pl.pallas_call(
        _ada_kernel,
        grid=(depth, n // tn),
        in_specs=[pl.BlockSpec((b, d), lambda l, j: (0, 0)),
                  pl.BlockSpec((1, d, tn), lambda l, j: (l, 0, j)),
                  pl.BlockSpec((1, 1, tn), lambda l, j: (l, 0, j))],
        out_specs=pl.BlockSpec((1, b, tn), lambda l, j: (l, 0, j)),
        out_shape=jax.ShapeDtypeStruct((depth, b, n), F32),
        compiler_params=_params(2),
        name="adaln",
    )(c, w_ada, b_ada.reshape(depth, 1, n))


def _rope_kernel(pos_ref, inv_ref, cos_ref, sina_ref, sinb_ref):
    ang = pos_ref[0].astype(F32) * inv_ref[...]
    lane = lax.broadcasted_iota(jnp.int32, (1, LANES), 1)
    sin = jnp.sin(ang)
    cos_ref[0] = jnp.cos(ang)
    lo, mid, hi = MLA_NOPE, MLA_NOPE + ROPE_HALF, MLA_NOPE + MLA_ROPE
    sina_ref[0] = jnp.where((lane >= lo) & (lane < mid), -sin, 0.0)
    sinb_ref[0] = jnp.where((lane >= mid) & (lane < hi), sin, 0.0)


def _rope_tables(positions):
    b, s = positions.shape
    ts = min(s, 512)
    inv = ROPE_BASE ** (-jnp.arange(ROPE_HALF, dtype=F32) / ROPE_HALF)
    inv_slot = jnp.concatenate([jnp.zeros((MLA_NOPE,), F32), inv, inv,
                                jnp.zeros((LANES - MLA_NOPE - MLA_ROPE,), F32)]).reshape(1, LANES)
    spec = pl.BlockSpec((1, ts, LANES), lambda i, j: (i, j, 0))
    return pl.pallas_call(
        _rope_kernel,
        grid=(b, s // ts),
        in_specs=[pl.BlockSpec((1, ts, 1), lambda i, j: (i, j, 0)),
                  pl.BlockSpec((1, LANES), lambda i, j: (0, 0))],
        out_specs=[spec, spec, spec],
        out_shape=[jax.ShapeDtypeStruct((b, s, LANES), F32)] * 3,
        compiler_params=_params(2),
        name="rope_tables",
    )(positions.reshape(b, s, 1), inv_slot)


def _rope_slot(x, cos, sina, sinb):
    return (x * cos + pltpu.roll(x, LANES - ROPE_HALF, 1) * sina
            + pltpu.roll(x, ROPE_HALF, 1) * sinb)


def _inproj_kernel(x_ref, mod_ref, g_ref, w_ref, bff_ref, gq_ref, gkv_ref, wuq_ref, wkn_ref,
                   wkv_ref, cos_ref, sina_ref, sinb_ref, place_ref, qc_ref, kc_ref,
                   fq_ref, fk_ref, fv_ref, sb_ref, sv_ref, gate_ref, mq_ref, mk_ref, mv_ref,
                   carry_ref, *, tm):
    x = x_ref[0]
    sh, sc = mod_ref[0, 0:1, :], mod_ref[0, 1:2, :]
    u = ((_rms(x) * g_ref[...]) * (1.0 + sc) + sh).astype(BF16)

    def seg(a, b):
        return _bdot(u, w_ref[:, a:b])

    lane = lax.broadcasted_iota(jnp.int32, (1, LANES), 1)
    cos, sina, sinb = cos_ref[0], sina_ref[0], sinb_ref[0]

    def gates(chunks):
        for c in chunks:
            a = 512 * c
            gate_ref[0, :, a:a + 512] = _sigmoid(seg(C_GATE + a, C_GATE + a + 512)).astype(BF16)

    def pack3(v):
        p0, p1, p2 = (p.astype(F32) for p in _split3(v))
        return jnp.where(lane < N_HEADS, p0,
                         jnp.where(lane < 2 * N_HEADS, pltpu.roll(p1, N_HEADS, 1),
                                   jnp.where(lane < 3 * N_HEADS, pltpu.roll(p2, 2 * N_HEADS, 1),
                                             0.0))).astype(BF16)

    def to_slots(compact, extra, out_ref):
        for h in range(N_HEADS):
            x = compact[:, LANES * (h // 2):LANES * (h // 2 + 1)]
            head = pltpu.roll(x, HEAD_DIM, 1) if h % 2 else x
            slot = slice(LANES * h, LANES * (h + 1))
            out_ref[0, :, slot] = jnp.where(lane < HEAD_DIM, head, extra[:, slot]).astype(BF16)

    forget_logits = seg(C_FF, IN_W)
    q_latent = seg(C_QL, C_KVL)
    kv_latent = seg(C_KVL, C_KR)
    rope_key = seg(C_KR, C_FF)
    log_f = pack3(_log_sigmoid(forget_logits + bff_ref[...]))
    cq = (_rms(q_latent) * gq_ref[...]).astype(BF16)
    ckv = (_rms(kv_latent) * gkv_ref[...]).astype(BF16)
    gates(range(0, 3))

    r = lax.broadcasted_iota(jnp.int32, (tm, tm), 0)
    c = lax.broadcasted_iota(jnp.int32, (tm, tm), 1)
    part = _bdot(jnp.where(r >= c, 1.0, 0.0).astype(BF16), log_f)
    q_all = _bdot(cq, wuq_ref[...])
    k_all = _bdot(ckv, wkn_ref[...])
    v_all = _bdot(ckv, wkv_ref[...])
    gates(range(3, 6))

    first_tile = pl.program_id(1) == 0
    cum = ((part + pltpu.roll(part, LANES - N_HEADS, 1))
           + pltpu.roll(part, LANES - 2 * N_HEADS, 1)) + jnp.where(first_tile, 0.0, carry_ref[...])
    carry_ref[...] = jnp.where(lane < N_HEADS, cum[tm - 1:tm, :], 0.0)
    placed = _bdot(pack3(cum), place_ref[...])
    first_aug = (lax.broadcasted_iota(jnp.int32, (1, SLOT_W), 1) & (LANES - 1)) < HEAD_DIM + N_PIECES
    fox_q = seg(C_FQ, C_FK) * QK_SCALE
    fox_k = seg(C_FK, C_FV)

    kr = _rope_slot(rope_key, cos, sina, sinb)
    for h in range(N_HEADS):
        slot = slice(LANES * h, LANES * (h + 1))
        q = _rope_slot(q_all[:, slot], cos, sina, sinb)
        mq_ref[0, :, slot] = (q * MLA_SCALE).astype(BF16)
        mk_ref[0, :, slot] = (k_all[:, slot] + kr).astype(BF16)
    mv_ref[0] = v_all.astype(BF16).T
    to_slots(fox_q, jnp.where(first_aug, qc_ref[...], placed), fq_ref)
    to_slots(fox_k, jnp.where(first_aug, placed, kc_ref[...]), fk_ref)

    sv_ref[0] = seg(C_SB + 1024, C_GATE).astype(BF16).T
    fv_ref[0] = seg(C_FV, C_SB).astype(BF16).T
    sb_ref[0, :, 0:512] = (seg(C_SB, C_SB + 512) * QK_SCALE).astype(BF16)
    sb_ref[0, :, 512:1024] = seg(C_SB + 512, C_SB + 1024).astype(BF16)


def _fox_placement():
    place = np.zeros((LANES, SLOT_W), np.float32)
    qc = np.zeros((1, SLOT_W), np.float32)
    kc = np.zeros((1, SLOT_W), np.float32)
    for h in range(N_HEADS):
        for p in range(N_PIECES):
            place[N_HEADS * p + h, LANES * h + HEAD_DIM + p] = 1.0
            place[N_HEADS * p + h, LANES * h + HEAD_DIM + N_PIECES + p] = 1.0
            qc[0, LANES * h + HEAD_DIM + p] = -1.0
            kc[0, LANES * h + HEAD_DIM + N_PIECES + p] = 1.0
    return jnp.asarray(place, BF16), jnp.asarray(qc), jnp.asarray(kc)


def _inproj(x, mod, weights, tables, *, layer, tm):
    b, s, d = x.shape
    cos, sina, sinb = tables
    place, qc, kc = _fox_placement()
    per_layer = lambda *shape: _resident_layer(shape, layer)
    row = lambda w: pl.BlockSpec((1, tm, w), lambda i, j: (i, j, 0))
    col = lambda w: pl.BlockSpec((1, w, tm), lambda i, j: (i, 0, j))
    outs = [(SLOT_W, False),
            (SLOT_W, False),
            (HEAD_W, True),
            (2 * HEAD_W, False),
            (HEAD_W, True),
            (3 * D_MODEL, False),
            (SLOT_W, False),
            (SLOT_W, False),
            (HEAD_W, True)]
    out_specs = [col(w) if tr else row(w) for w, tr in outs]
    out_shape = [jax.ShapeDtypeStruct((b, w, s) if tr else (b, s, w), BF16) for w, tr in outs]
    return pl.pallas_call(
        functools.partial(_inproj_kernel, tm=tm),
        grid=(b, s // tm),
        in_specs=[row(d),
                  pl.BlockSpec((1, 6, d), lambda i, j: (i, 0, 0)),
                  per_layer(1, d), per_layer(d, IN_W), per_layer(1, LANES),
                  per_layer(1, MLA_Q_RANK), per_layer(1, MLA_KV_RANK),
                  per_layer(MLA_Q_RANK, SLOT_W), per_layer(MLA_KV_RANK, SLOT_W),
                  per_layer(MLA_KV_RANK, HEAD_W),
                  row(LANES), row(LANES), row(LANES),
                  _resident((LANES, SLOT_W)), _resident((1, SLOT_W)), _resident((1, SLOT_W))],
        out_specs=out_specs,
        out_shape=out_shape,
        scratch_shapes=[pltpu.VMEM((1, LANES), F32)],
        compiler_params=_params(2),
        name="inproj",
    )(x, mod, *weights, cos, sina, sinb, place, qc, kc)


def _head_rows():
    row = lax.broadcasted_iota(jnp.int32, (LANES, 1), 0)
    return [(row >= HEAD_DIM * hh) & (row < HEAD_DIM * (hh + 1)) for hh in range(2)]


def _head_masks():
    lane = lax.broadcasted_iota(jnp.int32, (1, LANES), 1)
    return [(lane >= HEAD_DIM * hh) & (lane < HEAD_DIM * (hh + 1)) for hh in range(2)]


def _colmax(x, chunk=64):
    rows = x.shape[0]
    if rows > chunk and rows % chunk == 0:
        acc = x[:chunk]
        for a in range(chunk, rows, chunk):
            acc = jnp.maximum(acc, x[a:a + chunk])
        x, rows = acc, chunk
    while rows > 8 and rows % 16 == 0:
        rows //= 2
        x = jnp.maximum(x[:rows], x[rows:])
    return jnp.max(x, axis=0, keepdims=True)


def _pipeline(n, stages, unroll=PIPELINE_UNROLL, n_static=0):
    stages = [st if isinstance(st, tuple) else (k, st) for k, st in enumerate(stages)]
    depth = max(s for s, _ in stages) + 1

    def static_step(t):
        for s, stage in stages:
            if 0 <= t - s < n:
                stage(t - s, (t - s) % 2)

    first_full = min(max(depth - 1, n_static), n + depth - 1)
    n_full = max(n - first_full, 0)
    for t in range(first_full):
        static_step(t)
    n_loops = n_full // unroll
    if n_loops >= 1:
        def body(it, carry):
            t = first_full + unroll * it
            for u in range(unroll):
                for s, stage in stages:
                    stage(t + u - s, (first_full + u - s) % 2)
            return carry

        lax.fori_loop(0, n_loops, body, 0)
    for t in range(first_full + unroll * n_loops, n + depth - 1):
        static_step(t)


def _for_each_tile(n_tiles, fn, group=4):
    while n_tiles % group:
        group //= 2

    def body(i, carry):
        for u in range(group):
            fn(i * group + u)
        return carry

    lax.fori_loop(0, n_tiles // group, body, 0)


def _softmax_block_table(n_tiles, t):
    diag = [(i * t, i * t) for i in range(n_tiles)]
    below = [(i * t, j * t) for i in range(1, n_tiles) for j in range(i)]
    return jnp.asarray(np.array(diag + below, np.int32).T), len(diag), len(below)


def _softmax_flat_kernel(tab_ref, q_ref, k_ref, vt_ref, o_ref, acc_ref, m_ref, al_ref, top_ref,
                         s0_ref, s1_ref, p0_ref, p1_ref, *, t, n_diag, n_below, chunk_shift):
    own_rows = _head_rows()
    s_bufs, p_bufs = (s0_ref, s1_ref), (p0_ref, p1_ref)
    kio = lax.broadcasted_iota(jnp.int32, (t, 1), 0)
    last_visible = lax.broadcasted_iota(jnp.int32, (1, t), 1) | ((1 << chunk_shift) - 1)
    def item(i):
        return pl.multiple_of(tab_ref[0, i], t), pl.multiple_of(tab_ref[1, i], t)

    def is_diag(i):
        return isinstance(i, int) and i < n_diag

    half = t // 2

    def scores(hh, i, par):
        q0, k0 = item(i)
        slot = slice(LANES * hh, LANES * (hh + 1))
        if is_diag(i):
            s_lo = _dot_nt(k_ref[0, pl.ds(k0, half), slot], q_ref[0, pl.ds(q0, t), slot])
            s_hi = _dot_nt(k_ref[0, pl.ds(k0 + half, half), slot],
                           q_ref[0, pl.ds(q0 + half, half), slot])
            s_lo = jnp.where(kio[:half] <= last_visible, s_lo, NEG)
            s_hi = jnp.where(kio[:half] <= last_visible[:, :half], s_hi, NEG)
            s_bufs[par][hh, :half, :] = s_lo
            s_bufs[par][hh, half:, half:] = s_hi
            top_ref[par, hh] = jnp.concatenate(
                [_colmax(s_lo[:, :half]), jnp.maximum(_colmax(s_lo[:, half:]), _colmax(s_hi))],
                axis=1)
        else:
            s = _dot_nt(k_ref[0, pl.ds(k0, t), slot], q_ref[0, pl.ds(q0, t), slot])
            s_bufs[par][hh] = s
            top_ref[par, hh] = _colmax(s)

    def numerators(i, par):
        q0, _ = item(i)
        for hh in range(2):
            m_new = top_ref[par, hh]
            if not is_diag(i):
                m_old = m_ref[hh, :, pl.ds(q0, t)]
                m_new = jnp.maximum(m_old, m_new)
                al_ref[par, hh] = jnp.exp(m_old - m_new)
            for a in range(0, t, ROW_CHUNK):
                rows = slice(a, a + ROW_CHUNK)
                if is_diag(i) and a >= half:
                    p_bufs[par][hh, rows, :half] = jnp.zeros((ROW_CHUNK, half), BF16)
                    p_bufs[par][hh, rows, half:] = jnp.exp(
                        s_bufs[par][hh, rows, half:] - m_new[:, half:]).astype(BF16)
                else:
                    p_bufs[par][hh, rows, :] = jnp.exp(s_bufs[par][hh, rows, :] - m_new).astype(BF16)
            m_ref[hh, :, pl.ds(q0, t)] = m_new

    def values(hh, i, par):
        q0, k0 = item(i)
        vt = vt_ref[0, :, pl.ds(k0, t)]
        v1 = jnp.where(own_rows[hh], vt, jnp.ones_like(vt))
        pv = _bdot(v1, p_bufs[par][hh])
        if not is_diag(i):
            pv = acc_ref[hh, :, pl.ds(q0, t)] * al_ref[par, hh] + pv
        acc_ref[hh, :, pl.ds(q0, t)] = pv

    head = functools.partial
    stages = [(0, head(scores, 0)), (0, head(scores, 1)), (1, numerators),
              (2, head(values, 0)), (2, head(values, 1))]
    _pipeline(n_diag + n_below, stages, n_static=n_diag + max(s for s, _ in stages))

    sub = lax.broadcasted_iota(jnp.int32, (LANES, 1), 0)

    def finish(tile):
        q0 = pl.multiple_of(tile * t, t)
        outs = []
        for hh in range(2):
            acc = acc_ref[hh, :, pl.ds(q0, t)]
            denom = acc[HEAD_DIM * (1 - hh):HEAD_DIM * (1 - hh) + 1, :]
            outs.append(acc / denom)
        o_ref[0, pl.ds(q0, t), :] = jnp.where(sub < HEAD_DIM, outs[0], outs[1]).astype(o_ref.dtype).T

    _for_each_tile(n_diag, finish)


def _softmax_attention_flat(q, k, vt, *, t, chunk_shift, name):
    b, s, _ = k.shape
    table, n_diag, n_below = _softmax_block_table(s // t, t)
    grid_spec = pltpu.PrefetchScalarGridSpec(
        num_scalar_prefetch=1,
        grid=(b, N_HEADS // 2),
        in_specs=[pl.BlockSpec((1, s, 2 * LANES), lambda i, h, tab: (i, 0, h)),
                  pl.BlockSpec((1, s, 2 * LANES), lambda i, h, tab: (i, 0, h)),
                  pl.BlockSpec((1, LANES, s), lambda i, h, tab: (i, h, 0))],
        out_specs=pl.BlockSpec((1, s, LANES), lambda i, h, tab: (i, 0, h)),
        scratch_shapes=[pltpu.VMEM((2, LANES, s), F32),
                        pltpu.VMEM((2, 1, s), F32),
                        pltpu.VMEM((2, 2, 1, t), F32),
                        pltpu.VMEM((2, 2, 1, t), F32),
                        pltpu.VMEM((2, t, t), F32), pltpu.VMEM((2, t, t), F32),
                        pltpu.VMEM((2, t, t), BF16), pltpu.VMEM((2, t, t), BF16)])
    return pl.pallas_call(
        functools.partial(_softmax_flat_kernel, t=t, n_diag=n_diag, n_below=n_below,
                          chunk_shift=chunk_shift),
        grid_spec=grid_spec,
        out_shape=jax.ShapeDtypeStruct((b, s, HEAD_W), BF16),
        compiler_params=_params(2),
        name=name,
    )(table, q, k, vt)


def _sb_block_table(n_tiles, t):
    diag = [(i * t, i * t, i * t, i) for i in range(n_tiles)]
    below = [(i * t, (i - d) * t, i * t, i) for d in range(1, n_tiles) for i in range(d, n_tiles)]
    idle = [(0, 0, n_tiles * t, n_tiles)]
    return jnp.asarray(np.array(diag + below + idle, np.int32).T), len(diag), len(below)


def _sb_flat_kernel(tab_ref, q_ref, k_ref, vt_ref, o_ref, acc_ref, r_ref, tot_ref, d_ref, done_ref,
                    z0_ref, z1_ref, e0_ref, e1_ref, w0_ref, w1_ref, *, t, n_diag, n_below):
    in_head = _head_masks()
    own_rows = _head_rows()
    z_bufs, e_bufs, w_bufs = (z0_ref, z1_ref), (e0_ref, e1_ref), (w0_ref, w1_ref)
    kio = lax.broadcasted_iota(jnp.int32, (t, 1), 0)
    qio = lax.broadcasted_iota(jnp.int32, (1, t), 1)
    ss = lax.broadcasted_iota(jnp.int32, (t, t), 0)
    jj = lax.broadcasted_iota(jnp.int32, (t, t), 1)
    later = jnp.where(jj > ss, 1.0, 0.0).astype(BF16)
    acc_ref[...] = jnp.zeros_like(acc_ref)
    r_ref[...] = jnp.zeros_like(r_ref)
    for tile in range(n_diag + 1):
        done_ref[tile] = 0

    def stages(base, diag):
        def item(i):
            return (pl.multiple_of(tab_ref[0, base + i], t), pl.multiple_of(tab_ref[1, base + i], t),
                    pl.multiple_of(tab_ref[2, base + i], t), tab_ref[3, base + i])

        def scores(i, par):
            q0, k0, _, _ = item(i)
            k = k_ref[0, pl.ds(k0, t), :]
            q = q_ref[0, pl.ds(q0, t), :]
            for hh in range(2):
                qh = jnp.where(in_head[hh], q, jnp.zeros_like(q))
                z_bufs[par][hh] = _dot_nt(k, qh)

        def drops(i, par):
            for hh in range(2):
                for a in range(0, t, ROW_CHUNK):
                    rows = slice(a, a + ROW_CHUNK)
                    z = z_bufs[par][hh, rows, :]
                    neg_abs = pltpu.bitcast(pltpu.bitcast(z, jnp.uint32) | SIGN_BIT, F32)
                    drop = jnp.maximum(z, 0.0) + jnp.log(1.0 + jnp.exp(neg_abs))
                    log_beta = z - drop
                    if diag:
                        drop = jnp.where(kio[rows] < qio, drop, 0.0)
                        log_beta = jnp.where(kio[rows] < qio, log_beta, NEG)
                    d_ref[hh, rows, :] = drop.astype(BF16)
                    e_bufs[par][hh, rows, :] = log_beta
                after = _bdot(later, d_ref[hh])
                for a in range(0, t, ROW_CHUNK):
                    rows = slice(a, a + ROW_CHUNK)
                    e_bufs[par][hh, rows, :] = e_bufs[par][hh, rows, :] - after[rows]
                tot_ref[par, hh] = after[0:1, :] + d_ref[hh, 0:1, :].astype(F32)

        def weights(i, par):
            _, _, q0, tile = item(i)
            slack = None
            for hh in range(2):
                r_old = r_ref[hh, :, pl.ds(q0, t)]
                for a in range(0, t, ROW_CHUNK):
                    rows = slice(a, a + ROW_CHUNK)
                    w_bufs[par][hh, rows, :] = jnp.exp(e_bufs[par][hh, rows, :] - r_old).astype(BF16)
                r_new = r_old + tot_ref[par, hh]
                r_ref[hh, :, pl.ds(q0, t)] = r_new
                slack = r_new if slack is None else jnp.minimum(slack, r_new)
            if not diag:
                done_ref[tile] = (jnp.min(slack) > UNDERFLOW).astype(jnp.int32)

        def values(i, par):
            _, k0, q0, _ = item(i)
            vt = vt_ref[0, :, pl.ds(k0, t)]
            zero = jnp.zeros_like(vt)
            lhs = jnp.concatenate([jnp.where(own_rows[hh], vt, zero) for hh in range(2)], axis=1)
            acc_ref[:, pl.ds(q0, t)] = (acc_ref[:, pl.ds(q0, t)]
                                        + _bdot(lhs, w_bufs[par][...].reshape(2 * t, t)))

        return [scores, drops, weights, values]

    _pipeline(n_diag, stages(0, True))

    scores, drops, weights, values = stages(n_diag, False)

    def next_live(c):
        return lax.while_loop(
            lambda c: (c < n_below) & (done_ref[tab_ref[3, n_diag + c]] != 0), lambda c: c + 1, c)

    def after(c):
        return next_live(jnp.minimum(c + 1, n_below))

    def two_steps(state):
        a, b, i1, i2, i3 = state
        scores(a, 0)
        drops(i1, 1)
        weights(i2, 0)
        values(i3, 1)
        scores(b, 1)
        drops(a, 0)
        weights(i1, 1)
        values(i2, 0)
        a_next = after(b)
        return a_next, after(a_next), b, a, i1

    idle = jnp.int32(n_below)
    first = next_live(jnp.int32(0))
    lax.while_loop(lambda st: (st[0] < n_below) | (st[2] < n_below) | (st[3] < n_below)
                   | (st[4] < n_below), two_steps, (first, after(first), idle, idle, idle))

    def finish(tile):
        q0 = pl.multiple_of(tile * t, t)
        o_ref[0, pl.ds(q0, t), :] = acc_ref[:, pl.ds(q0, t)].astype(o_ref.dtype).T

    _for_each_tile(n_diag, finish)


def _sb_attention_flat(qk, vt, *, t):
    b, s, _ = qk.shape
    table, n_diag, n_below = _sb_block_table(s // t, t)
    grid_spec = pltpu.PrefetchScalarGridSpec(
        num_scalar_prefetch=1,
        grid=(b, N_HEADS // 2),
        in_specs=[pl.BlockSpec((1, s, LANES), lambda i, h, tab: (i, 0, h)),
                  pl.BlockSpec((1, s, LANES), lambda i, h, tab: (i, 0, 4 + h)),
                  pl.BlockSpec((1, LANES, s), lambda i, h, tab: (i, h, 0))],
        out_specs=pl.BlockSpec((1, s, LANES), lambda i, h, tab: (i, 0, h)),
        scratch_shapes=[pltpu.VMEM((LANES, s + t), F32),
                        pltpu.VMEM((2, 1, s + t), F32),
                        pltpu.VMEM((2, 2, 1, t), F32),
                        pltpu.VMEM((2, t, t), BF16),
                        pltpu.SMEM((s // t + 1,), jnp.int32),
                        pltpu.VMEM((2, t, t), F32), pltpu.VMEM((2, t, t), F32),
                        pltpu.VMEM((2, t, t), F32), pltpu.VMEM((2, t, t), F32),
                        pltpu.VMEM((2, t, t), BF16), pltpu.VMEM((2, t, t), BF16)])
    return pl.pallas_call(
        functools.partial(_sb_flat_kernel, t=t, n_diag=n_diag, n_below=n_below),
        grid_spec=grid_spec,
        out_shape=jax.ShapeDtypeStruct((b, s, HEAD_W), BF16),
        compiler_params=_params(2),
        name="sb_attention",
    )(table, qk, qk, vt)


def _merge_ffn_kernel(x_ref, yf_ref, ym_ref, ys_ref, gate_ref, mod_ref, wf_ref, wm_ref, ws_ref,
                      wo_ref, g_ref, wg_ref, wu_ref, wd_ref, gfin_ref, o_ref, *, final, fc):
    d = D_MODEL
    merged = (gate_ref[0, :, 0:d].astype(F32) * _bdot(yf_ref[0], wf_ref[...])
              + gate_ref[0, :, d:2 * d].astype(F32) * _bdot(ym_ref[0], wm_ref[...])
              + gate_ref[0, :, 2 * d:3 * d].astype(F32) * _bdot(ys_ref[0], ws_ref[...]))
    x = x_ref[0] + mod_ref[0, 2:3, :] * _bdot(merged.astype(BF16), wo_ref[...])

    sh, sc, gt = mod_ref[0, 3:4, :], mod_ref[0, 4:5, :], mod_ref[0, 5:6, :]
    u = ((_rms(x) * g_ref[...]) * (1.0 + sc) + sh).astype(BF16)
    acc = None
    for a in range(0, D_FF, fc):
        hg = _bdot(u, wg_ref[:, a:a + fc])
        hu = _bdot(u, wu_ref[:, a:a + fc])
        h = ((hg * _sigmoid(hg)) * hu).astype(BF16)
        part = _bdot(h, wd_ref[a:a + fc, :])
        acc = part if acc is None else acc + part
    out = x + gt * acc
    if final:
        out = _rms(out) * gfin_ref[...]
    o_ref[0] = out


def _merge_ffn(x, y_fox, y_mla, y_sb, gates, mod, weights, g_final, *, layer, tm, final):
    b, s, d = x.shape
    row = lambda w: pl.BlockSpec((1, tm, w), lambda i, j: (i, j, 0))
    per_layer = lambda *shape: _resident_layer(shape, layer)
    return pl.pallas_call(
        functools.partial(_merge_ffn_kernel, final=final, fc=D_FF // 2),
        grid=(b, s // tm),
        in_specs=[row(d), row(HEAD_W), row(HEAD_W), row(HEAD_W), row(3 * d),
                  pl.BlockSpec((1, 6, d), lambda i, j: (i, 0, 0)),
                  per_layer(HEAD_W, d), per_layer(HEAD_W, d), per_layer(HEAD_W, d),
                  per_layer(d, d), per_layer(1, d),
                  per_layer(d, D_FF), per_layer(d, D_FF), per_layer(D_FF, d),
                  _resident((1, d))],
        out_specs=row(d),
        out_shape=jax.ShapeDtypeStruct((b, s, d), F32),
        compiler_params=_params(2),
        name="merge_ffn",
    )(x, y_fox, y_mla, y_sb, gates, mod, *weights, g_final)


def _slot_cols(w, width_per_head, used):
    lead = w.shape[:-1]
    w = w.reshape(lead + (N_HEADS, width_per_head))[..., :used]
    pad = [(0, 0)] * (w.ndim - 1) + [(0, LANES - used)]
    return jnp.pad(w, pad).reshape(lead + (SLOT_W,))


def _layout_w_in(w):
    fox, ff, ql, kvl, kr, sb, gate = jnp.split(
        w, [1536, 1544, 1928, 2184, 2216, 3752], axis=-1)
    z = lambda n: jnp.zeros(w.shape[:-1] + (n,), w.dtype)
    kr_slot = jnp.concatenate([z(MLA_NOPE), kr, z(LANES - MLA_NOPE - MLA_ROPE)], axis=-1)
    ff_slot = jnp.concatenate([ff, z(LANES - N_HEADS)], axis=-1)
    return jnp.concatenate([fox, sb, gate, ql, kvl, kr_slot, ff_slot], axis=-1).astype(BF16)


def kernel(x, c, positions, g_mix, w_ada, b_ada, w_in, b_fox_f, g_mla_q, w_mla_uq, g_mla_kv,
           w_mla_ukv, w_o_fox, w_o_mla, w_o_sb, w_out, g_ffn, w_ffn_gate, w_ffn_up, w_ffn_down,
           g_final):
    b, s, d = x.shape
    depth = w_in.shape[0]
    tm = min(s, 512)
    tm_ffn = min(s, 512)
    ta = min(s, 512)
    tsb = min(s, 256)

    mod = _ada(c, w_ada, b_ada).reshape(depth, b, 6, d)
    tables = _rope_tables(positions)

    row_vec = lambda v: v.reshape(depth, 1, -1)
    wkv = w_mla_ukv.reshape(depth, MLA_KV_RANK, N_HEADS, 2 * HEAD_DIM)[..., MLA_NOPE:]
    inproj_weights = (
        row_vec(g_mix), _layout_w_in(w_in),
        row_vec(jnp.pad(b_fox_f, ((0, 0), (0, LANES - N_HEADS)))),
        row_vec(g_mla_q), row_vec(g_mla_kv),
        _slot_cols(w_mla_uq, MLA_NOPE + MLA_ROPE, MLA_NOPE + MLA_ROPE).astype(BF16),
        _slot_cols(w_mla_ukv, 2 * HEAD_DIM, MLA_NOPE).astype(BF16),
        wkv.reshape(depth, MLA_KV_RANK, HEAD_W).astype(BF16))
    merge_ffn_weights = (
        w_o_fox.astype(BF16), w_o_mla.astype(BF16), w_o_sb.astype(BF16), w_out.astype(BF16),
        row_vec(g_ffn), w_ffn_gate.astype(BF16), w_ffn_up.astype(BF16), w_ffn_down.astype(BF16))

    for l in range(depth):
        fox_q, fox_k, fox_vt, sb_qk, sb_vt, gates, mq, mk, mvt = _inproj(
            x, mod[l], inproj_weights, tables, layer=l, tm=tm)
        y_fox = _softmax_attention_flat(fox_q, fox_k, fox_vt, t=ta, chunk_shift=0,
                                        name="fox_attention")
        y_mla = _softmax_attention_flat(mq, mk, mvt, t=ta, chunk_shift=CHUNK_SHIFT,
                                        name="mla_attention")
        y_sb = _sb_attention_flat(sb_qk, sb_vt, t=tsb)
        x = _merge_ffn(x, y_fox, y_mla, y_sb, gates, mod[l], merge_ffn_weights,
                       g_final.reshape(1, d), layer=l, tm=tm_ffn, final=(l == depth - 1))
    return x
```

```python
import functools

import numpy as np
import jax
import jax.numpy as jnp
from jax import lax
from jax.experimental import pallas as pl
from jax.experimental.pallas import tpu as pltpu

F32 = jnp.float32
BF16 = jnp.bfloat16

D_MODEL = 1024
EPS = 1e-6
HEAD_DIM = 64
N_HEADS = 8
HEAD_W = N_HEADS * HEAD_DIM
CHUNK_SHIFT = 6
MLA_Q_RANK = 384
MLA_KV_RANK = 256
MLA_NOPE = 64
MLA_ROPE = 32
ROPE_HALF = MLA_ROPE // 2
ROPE_BASE = 10000.0
MLA_SCALE = (MLA_NOPE + MLA_ROPE) ** -0.5
QK_SCALE = HEAD_DIM ** -0.5
D_FF = 2816
LANES = 128
SLOT_W = N_HEADS * LANES
N_PIECES = 3

C_FQ = 0
C_FK = 512
C_FV = 1024
C_SB = 1536
C_GATE = 3072
C_QL = 6144
C_KVL = 6528
C_KR = 6784
C_FF = 6912
IN_W = 7040

ROW_CHUNK = 32
PIPELINE_UNROLL = 2
SIGN_BIT = np.uint32(0x80000000)
UNDERFLOW = 105.0
NEG = -1e30
VMEM_LIMIT = 56 * 1024 * 1024


def _params(n_grid):
    return pltpu.CompilerParams(dimension_semantics=("arbitrary",) * n_grid,
                                vmem_limit_bytes=VMEM_LIMIT)


def _resident(shape):
    nd = len(shape)
    return pl.BlockSpec(shape, lambda *_: (0,) * nd, pipeline_mode=pl.Buffered(1))


def _resident_layer(shape, layer):
    nd = len(shape)
    return pl.BlockSpec((pl.Squeezed(),) + tuple(shape), lambda *_: (layer,) + (0,) * nd,
                        pipeline_mode=pl.Buffered(1))


def _bdot(a, b):
    return jnp.dot(a, b, preferred_element_type=F32)


def _dot_nt(a, b):
    return lax.dot_general(a, b, (((1,), (1,)), ((), ())), preferred_element_type=F32)


def _split3(v):
    hi = v.astype(BF16)
    r = v - hi.astype(F32)
    mid = r.astype(BF16)
    lo = (r - mid.astype(F32)).astype(BF16)
    return hi, mid, lo


def _log_sigmoid(z):
    return jnp.minimum(z, 0.0) - jnp.log1p(jnp.exp(-jnp.abs(z)))


def _sigmoid(z):
    return 1.0 / (1.0 + jnp.exp(-z))


def _rms(x):
    return x * lax.rsqrt(jnp.mean(x * x, axis=-1, keepdims=True) + EPS)


def _ada_kernel(c_ref, w_ref, b_ref, o_ref):
    c = c_ref[...]
    cond = c * _sigmoid(c)
    chi, cmid, clo = _split3(cond)
    whi, wmid, wlo = _split3(w_ref[0])
    acc = _bdot(chi, whi) + (_bdot(chi, wmid) + _bdot(cmid, whi))
    acc = acc + (_bdot(chi, wlo) + _bdot(clo, whi) + _bdot(cmid, wmid))
    o_ref[0] = acc + b_ref[0]


def _ada(c, w_ada, b_ada):
    depth, d, n = w_ada.shape
    b = c.shape[0]
    tn = 1024
    return pl.pallas_call(
        _ada_kernel,
        grid=(depth, n // tn),
        in_specs=[pl.BlockSpec((b, d), lambda l, j: (0, 0)),
                  pl.BlockSpec((1, d, tn), lambda l, j: (l, 0, j)),
                  pl.BlockSpec((1, 1, tn), lambda l, j: (l, 0, j))],
        out_specs=pl.BlockSpec((1, b, tn), lambda l, j: (l, 0, j)),
        out_shape=jax.ShapeDtypeStruct((depth, b, n), F32),
        compiler_params=_params(2),
        name="adaln",
    )(c, w_ada, b_ada.reshape(depth, 1, n))


def _rope_kernel(pos_ref, inv_ref, cos_ref, sina_ref, sinb_ref):
    ang = pos_ref[0].astype(F32) * inv_ref[...]
    lane = lax.broadcasted_iota(jnp.int32, (1, LANES), 1)
    sin = jnp.sin(ang)
    cos_ref[0] = jnp.cos(ang)
    lo, mid, hi = MLA_NOPE, MLA_NOPE + ROPE_HALF, MLA_NOPE + MLA_ROPE
    sina_ref[0] = jnp.where((lane >= lo) & (lane < mid), -sin, 0.0)
    sinb_ref[0] = jnp.where((lane >= mid) & (lane < hi), sin, 0.0)


def _rope_tables(positions):
    b, s = positions.shape
    ts = min(s, 512)
    inv = ROPE_BASE ** (-jnp.arange(ROPE_HALF, dtype=F32) / ROPE_HALF)
    inv_slot = jnp.concatenate([jnp.zeros((MLA_NOPE,), F32), inv, inv,
                                jnp.zeros((LANES - MLA_NOPE - MLA_ROPE,), F32)]).reshape(1, LANES)
    spec = pl.BlockSpec((1, ts, LANES), lambda i, j: (i, j, 0))
    return pl.pallas_call(
        _rope_kernel,
        grid=(b, s // ts),
        in_specs=[pl.BlockSpec((1, ts, 1), lambda i, j: (i, j, 0)),
                  pl.BlockSpec((1, LANES), lambda i, j: (0, 0))],
        out_specs=[spec, spec, spec],
        out_shape=[jax.ShapeDtypeStruct((b, s, LANES), F32)] * 3,
        compiler_params=_params(2),
        name="rope_tables",
    )(positions.reshape(b, s, 1), inv_slot)


def _rope_slot(x, cos, sina, sinb):
    return (x * cos + pltpu.roll(x, LANES - ROPE_HALF, 1) * sina
            + pltpu.roll(x, ROPE_HALF, 1) * sinb)


def _inproj_kernel(x_ref, mod_ref, g_ref, w_ref, bff_ref, gq_ref, gkv_ref, wuq_ref, wkn_ref,
                   wkv_ref, cos_ref, sina_ref, sinb_ref, place_ref, qc_ref, kc_ref,
                   fq_ref, fk_ref, fv_ref, sb_ref, sv_ref, gate_ref, mq_ref, mk_ref, mv_ref,
                   carry_ref, *, tm):
    x = x_ref[0]
    sh, sc = mod_ref[0, 0:1, :], mod_ref[0, 1:2, :]
    u = ((_rms(x) * g_ref[...]) * (1.0 + sc) + sh).astype(BF16)

    def seg(a, b):
        return _bdot(u, w_ref[:, a:b])

    lane = lax.broadcasted_iota(jnp.int32, (1, LANES), 1)
    cos, sina, sinb = cos_ref[0], sina_ref[0], sinb_ref[0]

    def gates(chunks):
        for c in chunks:
            a = 512 * c
            gate_ref[0, :, a:a + 512] = _sigmoid(seg(C_GATE + a, C_GATE + a + 512)).astype(BF16)

    def pack3(v):
        p0, p1, p2 = (p.astype(F32) for p in _split3(v))
        return jnp.where(lane < N_HEADS, p0,
                         jnp.where(lane < 2 * N_HEADS, pltpu.roll(p1, N_HEADS, 1),
                                   jnp.where(lane < 3 * N_HEADS, pltpu.roll(p2, 2 * N_HEADS, 1),
                                             0.0))).astype(BF16)

    def to_slots(compact, extra, out_ref):
        for h in range(N_HEADS):
            x = compact[:, LANES * (h // 2):LANES * (h // 2 + 1)]
            head = pltpu.roll(x, HEAD_DIM, 1) if h % 2 else x
            slot = slice(LANES * h, LANES * (h + 1))
            out_ref[0, :, slot] = jnp.where(lane < HEAD_DIM, head, extra[:, slot]).astype(BF16)

    forget_logits = seg(C_FF, IN_W)
    q_latent = seg(C_QL, C_KVL)
    kv_latent = seg(C_KVL, C_KR)
    rope_key = seg(C_KR, C_FF)
    log_f = pack3(_log_sigmoid(forget_logits + bff_ref[...]))
    cq = (_rms(q_latent) * gq_ref[...]).astype(BF16)
    ckv = (_rms(kv_latent) * gkv_ref[...]).astype(BF16)
    gates(range(0, 3))

    r = lax.broadcasted_iota(jnp.int32, (tm, tm), 0)
    c = lax.broadcasted_iota(jnp.int32, (tm, tm), 1)
    part = _bdot(jnp.where(r >= c, 1.0, 0.0).astype(BF16), log_f)
    q_all = _bdot(cq, wuq_ref[...])
    k_all = _bdot(ckv, wkn_ref[...])
    v_all = _bdot(ckv, wkv_ref[...])
    gates(range(3, 6))

    first_tile = pl.program_id(1) == 0
    cum = ((part + pltpu.roll(part, LANES - N_HEADS, 1))
           + pltpu.roll(part, LANES - 2 * N_HEADS, 1)) + jnp.where(first_tile, 0.0, carry_ref[...])
    carry_ref[...] = jnp.where(lane < N_HEADS, cum[tm - 1:tm, :], 0.0)
    placed = _bdot(pack3(cum), place_ref[...])
    first_aug = (lax.broadcasted_iota(jnp.int32, (1, SLOT_W), 1) & (LANES - 1)) < HEAD_DIM + N_PIECES
    fox_q = seg(C_FQ, C_FK) * QK_SCALE
    fox_k = seg(C_FK, C_FV)

    kr = _rope_slot(rope_key, cos, sina, sinb)
    for h in range(N_HEADS):
        slot = slice(LANES * h, LANES * (h + 1))
        q = _rope_slot(q_all[:, slot], cos, sina, sinb)
        mq_ref[0, :, slot] = (q * MLA_SCALE).astype(BF16)
        mk_ref[0, :, slot] = (k_all[:, slot] + kr).astype(BF16)
    mv_ref[0] = v_all.astype(BF16).T
    to_slots(fox_q, jnp.where(first_aug, qc_ref[...], placed), fq_ref)
    to_slots(fox_k, jnp.where(first_aug, placed, kc_ref[...]), fk_ref)

    sv_ref[0] = seg(C_SB + 1024, C_GATE).astype(BF16).T
    fv_ref[0] = seg(C_FV, C_SB).astype(BF16).T
    sb_ref[0, :, 0:512] = (seg(C_SB, C_SB + 512) * QK_SCALE).astype(BF16)
    sb_ref[0, :, 512:1024] = seg(C_SB + 512, C_SB + 1024).astype(BF16)


def _fox_placement():
    place = np.zeros((LANES, SLOT_W), np.float32)
    qc = np.zeros((1, SLOT_W), np.float32)
    kc = np.zeros((1, SLOT_W), np.float32)
    for h in range(N_HEADS):
        for p in range(N_PIECES):
            place[N_HEADS * p + h, LANES * h + HEAD_DIM + p] = 1.0
            place[N_HEADS * p + h, LANES * h + HEAD_DIM + N_PIECES + p] = 1.0
            qc[0, LANES * h + HEAD_DIM + p] = -1.0
            kc[0, LANES * h + HEAD_DIM + N_PIECES + p] = 1.0
    return jnp.asarray(place, BF16), jnp.asarray(qc), jnp.asarray(kc)


def _inproj(x, mod, weights, tables, *, layer, tm):
    b, s, d = x.shape
    cos, sina, sinb = tables
    place, qc, kc = _fox_placement()
    per_layer = lambda *shape: _resident_layer(shape, layer)
    row = lambda w: pl.BlockSpec((1, tm, w), lambda i, j: (i, j, 0))
    col = lambda w: pl.BlockSpec((1, w, tm), lambda i, j: (i, 0, j))
    outs = [(SLOT_W, False),
            (SLOT_W, False),
            (HEAD_W, True),
            (2 * HEAD_W, False),
            (HEAD_W, True),
            (3 * D_MODEL, False),
            (SLOT_W, False),
            (SLOT_W, False),
            (HEAD_W, True)]
    out_specs = [col(w) if tr else row(w) for w, tr in outs]
    out_shape = [jax.ShapeDtypeStruct((b, w, s) if tr else (b, s, w), BF16) for w, tr in outs]
    return pl.pallas_call(
        functools.partial(_inproj_kernel, tm=tm),
        grid=(b, s // tm),
        in_specs=[row(d),
                  pl.BlockSpec((1, 6, d), lambda i, j: (i, 0, 0)),
                  per_layer(1, d), per_layer(d, IN_W), per_layer(1, LANES),
                  per_layer(1, MLA_Q_RANK), per_layer(1, MLA_KV_RANK),
                  per_layer(MLA_Q_RANK, SLOT_W), per_layer(MLA_KV_RANK, SLOT_W),
                  per_layer(MLA_KV_RANK, HEAD_W),
                  row(LANES), row(LANES), row(LANES),
                  _resident((LANES, SLOT_W)), _resident((1, SLOT_W)), _resident((1, SLOT_W))],
        out_specs=out_specs,
        out_shape=out_shape,
        scratch_shapes=[pltpu.VMEM((1, LANES), F32)],
        compiler_params=_params(2),
        name="inproj",
    )(x, mod, *weights, cos, sina, sinb, place, qc, kc)


def _head_rows():
    row = lax.broadcasted_iota(jnp.int32, (LANES, 1), 0)
    return [(row >= HEAD_DIM * hh) & (row < HEAD_DIM * (hh + 1)) for hh in range(2)]


def _head_masks():
    lane = lax.broadcasted_iota(jnp.int32, (1, LANES), 1)
    return [(lane >= HEAD_DIM * hh) & (lane < HEAD_DIM * (hh + 1)) for hh in range(2)]


def _colmax(x, chunk=64):
    rows = x.shape[0]
    if rows > chunk and rows % chunk == 0:
        acc = x[:chunk]
        for a in range(chunk, rows, chunk):
            acc = jnp.maximum(acc, x[a:a + chunk])
        x, rows = acc, chunk
    while rows > 8 and rows % 16 == 0:
        rows //= 2
        x = jnp.maximum(x[:rows], x[rows:])
    return jnp.max(x, axis=0, keepdims=True)


def _pipeline(n, stages, unroll=PIPELINE_UNROLL, n_static=0):
    stages = [st if isinstance(st, tuple) else (k, st) for k, st in enumerate(stages)]
    depth = max(s for s, _ in stages) + 1

    def static_step(t):
        for s, stage in stages:
            if 0 <= t - s < n:
                stage(t - s, (t - s) % 2)

    first_full = min(max(depth - 1, n_static), n + depth - 1)
    n_full = max(n - first_full, 0)
    for t in range(first_full):
        static_step(t)
    n_loops = n_full // unroll
    if n_loops >= 1:
        def body(it, carry):
            t = first_full + unroll * it
            for u in range(unroll):
                for s, stage in stages:
                    stage(t + u - s, (first_full + u - s) % 2)
            return carry

        lax.fori_loop(0, n_loops, body, 0)
    for t in range(first_full + unroll * n_loops, n + depth - 1):
        static_step(t)


def _for_each_tile(n_tiles, fn, group=4):
    while n_tiles % group:
        group //= 2

    def body(i, carry):
        for u in range(group):
            fn(i * group + u)
        return carry

    lax.fori_loop(0, n_tiles // group, body, 0)


def _softmax_block_table(n_tiles, t):
    diag = [(i * t, i * t) for i in range(n_tiles)]
    below = [(i * t, j * t) for i in range(1, n_tiles) for j in range(i)]
    return jnp.asarray(np.array(diag + below, np.int32).T), len(diag), len(below)


def _softmax_flat_kernel(tab_ref, q_ref, k_ref, vt_ref, o_ref, acc_ref, m_ref, al_ref, top_ref,
                         s0_ref, s1_ref, p0_ref, p1_ref, *, t, n_diag, n_below, chunk_shift):
    own_rows = _head_rows()
    s_bufs, p_bufs = (s0_ref, s1_ref), (p0_ref, p1_ref)
    kio = lax.broadcasted_iota(jnp.int32, (t, 1), 0)
    last_visible = lax.broadcasted_iota(jnp.int32, (1, t), 1) | ((1 << chunk_shift) - 1)
    def item(i):
        return pl.multiple_of(tab_ref[0, i], t), pl.multiple_of(tab_ref[1, i], t)

    def is_diag(i):
        return isinstance(i, int) and i < n_diag

    half = t // 2

    def scores(hh, i, par):
        q0, k0 = item(i)
        slot = slice(LANES * hh, LANES * (hh + 1))
        if is_diag(i):
            s_lo = _dot_nt(k_ref[0, pl.ds(k0, half), slot], q_ref[0, pl.ds(q0, t), slot])
            s_hi = _dot_nt(k_ref[0, pl.ds(k0 + half, half), slot],
                           q_ref[0, pl.ds(q0 + half, half), slot])
            s_lo = jnp.where(kio[:half] <= last_visible, s_lo, NEG)
            s_hi = jnp.where(kio[:half] <= last_visible[:, :half], s_hi, NEG)
            s_bufs[par][hh, :half, :] = s_lo
            s_bufs[par][hh, half:, half:] = s_hi
            top_ref[par, hh] = jnp.concatenate(
                [_colmax(s_lo[:, :half]), jnp.maximum(_colmax(s_lo[:, half:]), _colmax(s_hi))],
                axis=1)
        else:
            s = _dot_nt(k_ref[0, pl.ds(k0, t), slot], q_ref[0, pl.ds(q0, t), slot])
            s_bufs[par][hh] = s
            top_ref[par, hh] = _colmax(s)

    def numerators(i, par):
        q0, _ = item(i)
        for hh in range(2):
            m_new = top_ref[par, hh]
            if not is_diag(i):
                m_old = m_ref[hh, :, pl.ds(q0, t)]
                m_new = jnp.maximum(m_old, m_new)
                al_ref[par, hh] = jnp.exp(m_old - m_new)
            for a in range(0, t, ROW_CHUNK):
                rows = slice(a, a + ROW_CHUNK)
                if is_diag(i) and a >= half:
                    p_bufs[par][hh, rows, :half] = jnp.zeros((ROW_CHUNK, half), BF16)
                    p_bufs[par][hh, rows, half:] = jnp.exp(
                        s_bufs[par][hh, rows, half:] - m_new[:, half:]).astype(BF16)
                else:
                    p_bufs[par][hh, rows, :] = jnp.exp(s_bufs[par][hh, rows, :] - m_new).astype(BF16)
            m_ref[hh, :, pl.ds(q0, t)] = m_new

    def values(hh, i, par):
        q0, k0 = item(i)
        vt = vt_ref[0, :, pl.ds(k0, t)]
        v1 = jnp.where(own_rows[hh], vt, jnp.ones_like(vt))
        pv = _bdot(v1, p_bufs[par][hh])
        if not is_diag(i):
            pv = acc_ref[hh, :, pl.ds(q0, t)] * al_ref[par, hh] + pv
        acc_ref[hh, :, pl.ds(q0, t)] = pv

    head = functools.partial
    stages = [(0, head(scores, 0)), (0, head(scores, 1)), (1, numerators),
              (2, head(values, 0)), (2, head(values, 1))]
    _pipeline(n_diag + n_below, stages, n_static=n_diag + max(s for s, _ in stages))

    sub = lax.broadcasted_iota(jnp.int32, (LANES, 1), 0)

    def finish(tile):
        q0 = pl.multiple_of(tile * t, t)
        outs = []
        for hh in range(2):
            acc = acc_ref[hh, :, pl.ds(q0, t)]
            denom = acc[HEAD_DIM * (1 - hh):HEAD_DIM * (1 - hh) + 1, :]
            outs.append(acc / denom)
        o_ref[0, pl.ds(q0, t), :] = jnp.where(sub < HEAD_DIM, outs[0], outs[1]).astype(o_ref.dtype).T

    _for_each_tile(n_diag, finish)


def _softmax_attention_flat(q, k, vt, *, t, chunk_shift, name):
    b, s, _ = k.shape
    table, n_diag, n_below = _softmax_block_table(s // t, t)
    grid_spec = pltpu.PrefetchScalarGridSpec(
        num_scalar_prefetch=1,
        grid=(b, N_HEADS // 2),
        in_specs=[pl.BlockSpec((1, s, 2 * LANES), lambda i, h, tab: (i, 0, h)),
                  pl.BlockSpec((1, s, 2 * LANES), lambda i, h, tab: (i, 0, h)),
                  pl.BlockSpec((1, LANES, s), lambda i, h, tab: (i, h, 0))],
        out_specs=pl.BlockSpec((1, s, LANES), lambda i, h, tab: (i, 0, h)),
        scratch_shapes=[pltpu.VMEM((2, LANES, s), F32),
                        pltpu.VMEM((2, 1, s), F32),
                        pltpu.VMEM((2, 2, 1, t), F32),
                        pltpu.VMEM((2, 2, 1, t), F32),
                        pltpu.VMEM((2, t, t), F32), pltpu.VMEM((2, t, t), F32),
                        pltpu.VMEM((2, t, t), BF16), pltpu.VMEM((2, t, t), BF16)])
    return pl.pallas_call(
        functools.partial(_softmax_flat_kernel, t=t, n_diag=n_diag, n_below=n_below,
                          chunk_shift=chunk_shift),
        grid_spec=grid_spec,
        out_shape=jax.ShapeDtypeStruct((b, s, HEAD_W), BF16),
        compiler_params=_params(2),
        name=name,
    )(table, q, k, vt)


def _sb_block_table(n_tiles, t):
    diag = [(i * t, i * t, i * t, i) for i in range(n_tiles)]
    below = [(i * t, (i - d) * t, i * t, i) for d in range(1, n_tiles) for i in range(d, n_tiles)]
    idle = [(0, 0, n_tiles * t, n_tiles)]
    return jnp.asarray(np.array(diag + below + idle, np.int32).T), len(diag), len(below)


def _sb_flat_kernel(tab_ref, q_ref, k_ref, vt_ref, o_ref, acc_ref, r_ref, tot_ref, d_ref, done_ref,
                    busy_ref, z0_ref, z1_ref, e0_ref, e1_ref, w0_ref, w1_ref, *, t, n_diag,
                    n_below):
    in_head = _head_masks()
    own_rows = _head_rows()
    z_bufs, e_bufs, w_bufs = (z0_ref, z1_ref), (e0_ref, e1_ref), (w0_ref, w1_ref)
    kio = lax.broadcasted_iota(jnp.int32, (t, 1), 0)
    qio = lax.broadcasted_iota(jnp.int32, (1, t), 1)
    ss = lax.broadcasted_iota(jnp.int32, (t, t), 0)
    jj = lax.broadcasted_iota(jnp.int32, (t, t), 1)
    later = jnp.where(jj > ss, 1.0, 0.0).astype(BF16)
    acc_ref[...] = jnp.zeros_like(acc_ref)
    r_ref[...] = jnp.zeros_like(r_ref)
    for tile in range(n_diag + 1):
        done_ref[tile] = 0
        busy_ref[tile] = 0

    def stages(base, diag):
        def item(i):
            return (pl.multiple_of(tab_ref[0, base + i], t), pl.multiple_of(tab_ref[1, base + i], t),
                    pl.multiple_of(tab_ref[2, base + i], t), tab_ref[3, base + i])

        def scores(i, par):
            q0, k0, _, _ = item(i)
            k = k_ref[0, pl.ds(k0, t), :]
            q = q_ref[0, pl.ds(q0, t), :]
            for hh in range(2):
                qh = jnp.where(in_head[hh], q, jnp.zeros_like(q))
                z_bufs[par][hh] = _dot_nt(k, qh)

        def drops(i, par):
            for hh in range(2):
                for a in range(0, t, ROW_CHUNK):
                    rows = slice(a, a + ROW_CHUNK)
                    z = z_bufs[par][hh, rows, :]
                    neg_abs = pltpu.bitcast(pltpu.bitcast(z, jnp.uint32) | SIGN_BIT, F32)
                    drop = jnp.maximum(z, 0.0) + jnp.log(1.0 + jnp.exp(neg_abs))
                    log_beta = z - drop
                    if diag:
                        drop = jnp.where(kio[rows] < qio, drop, 0.0)
                        log_beta = jnp.where(kio[rows] < qio, log_beta, NEG)
                    d_ref[hh, rows, :] = drop.astype(BF16)
                    e_bufs[par][hh, rows, :] = log_beta
                after = _bdot(later, d_ref[hh])
                for a in range(0, t, ROW_CHUNK):
                    rows = slice(a, a + ROW_CHUNK)
                    e_bufs[par][hh, rows, :] = e_bufs[par][hh, rows, :] - after[rows]
                tot_ref[par, hh] = after[0:1, :] + d_ref[hh, 0:1, :].astype(F32)

        def weights(i, par):
            _, _, q0, tile = item(i)
            slack = None
            for hh in range(2):
                r_old = r_ref[hh, :, pl.ds(q0, t)]
                for a in range(0, t, ROW_CHUNK):
                    rows = slice(a, a + ROW_CHUNK)
                    w_bufs[par][hh, rows, :] = jnp.exp(e_bufs[par][hh, rows, :] - r_old).astype(BF16)
                r_new = r_old + tot_ref[par, hh]
                r_ref[hh, :, pl.ds(q0, t)] = r_new
                slack = r_new if slack is None else jnp.minimum(slack, r_new)
            if not diag:
                done_ref[tile] = (jnp.min(slack) > UNDERFLOW).astype(jnp.int32)
                busy_ref[tile] = 0

        def values(i, par):
            _, k0, q0, _ = item(i)
            vt = vt_ref[0, :, pl.ds(k0, t)]
            zero = jnp.zeros_like(vt)
            lhs = jnp.concatenate([jnp.where(own_rows[hh], vt, zero) for hh in range(2)], axis=1)
            acc_ref[:, pl.ds(q0, t)] = (acc_ref[:, pl.ds(q0, t)]
                                        + _bdot(lhs, w_bufs[par][...].reshape(2 * t, t)))

        return [scores, drops, weights, values]

    _pipeline(n_diag, stages(0, True))

    scores, drops, weights, values = stages(n_diag, False)

    def next_live(c):
        return lax.while_loop(
            lambda c: (c < n_below) & (done_ref[tab_ref[3, n_diag + c]] != 0), lambda c: c + 1, c)

    def pick(cursor):
        c = next_live(cursor)
        wait = (c < n_below) & (busy_ref[tab_ref[3, n_diag + c]] != 0)
        issue = jnp.where(wait, n_below, c)
        busy_ref[tab_ref[3, n_diag + issue]] = 1
        return issue, jnp.where(wait, c, jnp.minimum(c + 1, n_below))

    def two_steps(state):
        a, b, cursor, i1, i2, i3 = state
        scores(a, 0)
        drops(i1, 1)
        weights(i2, 0)
        values(i3, 1)
        scores(b, 1)
        drops(a, 0)
        weights(i1, 1)
        values(i2, 0)
        a_next, cursor = pick(cursor)
        b_next, cursor = pick(cursor)
        return a_next, b_next, cursor, b, a, i1

    idle = jnp.int32(n_below)
    a0, cursor = pick(jnp.int32(0))
    b0, cursor = pick(cursor)
    lax.while_loop(lambda st: functools.reduce(lambda x, y: x | y, [v < n_below for v in st]),
                   two_steps, (a0, b0, cursor, idle, idle, idle))

    def finish(tile):
        q0 = pl.multiple_of(tile * t, t)
        o_ref[0, pl.ds(q0, t), :] = acc_ref[:, pl.ds(q0, t)].astype(o_ref.dtype).T

    _for_each_tile(n_diag, finish)


def _sb_attention_flat(qk, vt, *, t):
    b, s, _ = qk.shape
    table, n_diag, n_below = _sb_block_table(s // t, t)
    grid_spec = pltpu.PrefetchScalarGridSpec(
        num_scalar_prefetch=1,
        grid=(b, N_HEADS // 2),
        in_specs=[pl.BlockSpec((1, s, LANES), lambda i, h, tab: (i, 0, h)),
                  pl.BlockSpec((1, s, LANES), lambda i, h, tab: (i, 0, 4 + h)),
                  pl.BlockSpec((1, LANES, s), lambda i, h, tab: (i, h, 0))],
        out_specs=pl.BlockSpec((1, s, LANES), lambda i, h, tab: (i, 0, h)),
        scratch_shapes=[pltpu.VMEM((LANES, s + t), F32),
                        pltpu.VMEM((2, 1, s + t), F32),
                        pltpu.VMEM((2, 2, 1, t), F32),
                        pltpu.VMEM((2, t, t), BF16),
                        pltpu.SMEM((s // t + 1,), jnp.int32),
                        pltpu.SMEM((s // t + 1,), jnp.int32),
                        pltpu.VMEM((2, t, t), F32), pltpu.VMEM((2, t, t), F32),
                        pltpu.VMEM((2, t, t), F32), pltpu.VMEM((2, t, t), F32),
                        pltpu.VMEM((2, t, t), BF16), pltpu.VMEM((2, t, t), BF16)])
    return pl.pallas_call(
        functools.partial(_sb_flat_kernel, t=t, n_diag=n_diag, n_below=n_below),
        grid_spec=grid_spec,
        out_shape=jax.ShapeDtypeStruct((b, s, HEAD_W), BF16),
        compiler_params=_params(2),
        name="sb_attention",
    )(table, qk, qk, vt)


def _merge_ffn_kernel(x_ref, yf_ref, ym_ref, ys_ref, gate_ref, mod_ref, wf_ref, wm_ref, ws_ref,
                      wo_ref, g_ref, wg_ref, wu_ref, wd_ref, gfin_ref, o_ref, *, final, fc):
    d = D_MODEL
    merged = (gate_ref[0, :, 0:d].astype(F32) * _bdot(yf_ref[0], wf_ref[...])
              + gate_ref[0, :, d:2 * d].astype(F32) * _bdot(ym_ref[0], wm_ref[...])
              + gate_ref[0, :, 2 * d:3 * d].astype(F32) * _bdot(ys_ref[0], ws_ref[...]))
    x = x_ref[0] + mod_ref[0, 2:3, :] * _bdot(merged.astype(BF16), wo_ref[...])

    sh, sc, gt = mod_ref[0, 3:4, :], mod_ref[0, 4:5, :], mod_ref[0, 5:6, :]
    u = ((_rms(x) * g_ref[...]) * (1.0 + sc) + sh).astype(BF16)
    acc = None
    for a in range(0, D_FF, fc):
        hg = _bdot(u, wg_ref[:, a:a + fc])
        hu = _bdot(u, wu_ref[:, a:a + fc])
        h = ((hg * _sigmoid(hg)) * hu).astype(BF16)
        part = _bdot(h, wd_ref[a:a + fc, :])
        acc = part if acc is None else acc + part
    out = x + gt * acc
    if final:
        out = _rms(out) * gfin_ref[...]
    o_ref[0] = out


def _merge_ffn(x, y_fox, y_mla, y_sb, gates, mod, weights, g_final, *, layer, tm, final):
    b, s, d = x.shape
    row = lambda w: pl.BlockSpec((1, tm, w), lambda i, j: (i, j, 0))
    per_layer = lambda *shape: _resident_layer(shape, layer)
    return pl.pallas_call(
        functools.partial(_merge_ffn_kernel, final=final, fc=D_FF // 2),
        grid=(b, s // tm),
        in_specs=[row(d), row(HEAD_W), row(HEAD_W), row(HEAD_W), row(3 * d),
                  pl.BlockSpec((1, 6, d), lambda i, j: (i, 0, 0)),
                  per_layer(HEAD_W, d), per_layer(HEAD_W, d), per_layer(HEAD_W, d),
                  per_layer(d, d), per_layer(1, d),
                  per_layer(d, D_FF), per_layer(d, D_FF), per_layer(D_FF, d),
                  _resident((1, d))],
        out_specs=row(d),
        out_shape=jax.ShapeDtypeStruct((b, s, d), F32),
        compiler_params=_params(2),
        name="merge_ffn",
    )(x, y_fox, y_mla, y_sb, gates, mod, *weights, g_final)


def _slot_cols(w, width_per_head, used):
    lead = w.shape[:-1]
    w = w.reshape(lead + (N_HEADS, width_per_head))[..., :used]
    pad = [(0, 0)] * (w.ndim - 1) + [(0, LANES - used)]
    return jnp.pad(w, pad).reshape(lead + (SLOT_W,))


def _layout_w_in(w):
    fox, ff, ql, kvl, kr, sb, gate = jnp.split(
        w, [1536, 1544, 1928, 2184, 2216, 3752], axis=-1)
    z = lambda n: jnp.zeros(w.shape[:-1] + (n,), w.dtype)
    kr_slot = jnp.concatenate([z(MLA_NOPE), kr, z(LANES - MLA_NOPE - MLA_ROPE)], axis=-1)
    ff_slot = jnp.concatenate([ff, z(LANES - N_HEADS)], axis=-1)
    return jnp.concatenate([fox, sb, gate, ql, kvl, kr_slot, ff_slot], axis=-1).astype(BF16)


def kernel(x, c, positions, g_mix, w_ada, b_ada, w_in, b_fox_f, g_mla_q, w_mla_uq, g_mla_kv,
           w_mla_ukv, w_o_fox, w_o_mla, w_o_sb, w_out, g_ffn, w_ffn_gate, w_ffn_up, w_ffn_down,
           g_final):
    b, s, d = x.shape
    depth = w_in.shape[0]
    tm = min(s, 512)
    tm_ffn = min(s, 512)
    ta = min(s, 512)
    tsb = min(s, 256)

    mod = _ada(c, w_ada, b_ada).reshape(depth, b, 6, d)
    tables = _rope_tables(positions)

    row_vec = lambda v: v.reshape(depth, 1, -1)
    wkv = w_mla_ukv.reshape(depth, MLA_KV_RANK, N_HEADS, 2 * HEAD_DIM)[..., MLA_NOPE:]
    inproj_weights = (
        row_vec(g_mix), _layout_w_in(w_in),
        row_vec(jnp.pad(b_fox_f, ((0, 0), (0, LANES - N_HEADS)))),
        row_vec(g_mla_q), row_vec(g_mla_kv),
        _slot_cols(w_mla_uq, MLA_NOPE + MLA_ROPE, MLA_NOPE + MLA_ROPE).astype(BF16),
        _slot_cols(w_mla_ukv, 2 * HEAD_DIM, MLA_NOPE).astype(BF16),
        wkv.reshape(depth, MLA_KV_RANK, HEAD_W).astype(BF16))
    merge_ffn_weights = (
        w_o_fox.astype(BF16), w_o_mla.astype(BF16), w_o_sb.astype(BF16), w_out.astype(BF16),
        row_vec(g_ffn), w_ffn_gate.astype(BF16), w_ffn_up.astype(BF16), w_ffn_down.astype(BF16))

    for l in range(depth):
        fox_q, fox_k, fox_vt, sb_qk, sb_vt, gates, mq, mk, mvt = _inproj(
            x, mod[l], inproj_weights, tables, layer=l, tm=tm)
        y_fox = _softmax_attention_flat(fox_q, fox_k, fox_vt, t=ta, chunk_shift=0,
                                        name="fox_attention")
        y_mla = _softmax_attention_flat(mq, mk, mvt, t=ta, chunk_shift=CHUNK_SHIFT,
                                        name="mla_attention")
        y_sb = _sb_attention_flat(sb_qk, sb_vt, t=tsb)
        x = _merge_ffn(x, y_fox, y_mla, y_sb, gates, mod[l], merge_ffn_weights,
                       g_final.reshape(1, d), layer=l, tm=tm_ffn, final=(l == depth - 1))
    return x
```

```python
import functools

import numpy as np
import jax
import jax.numpy as jnp
from jax import lax
from jax.experimental import pallas as pl
from jax.experimental.pallas import tpu as pltpu

F32 = jnp.float32
BF16 = jnp.bfloat16

D_MODEL = 1024
EPS = 1e-6
HEAD_DIM = 64
N_HEADS = 8
HEAD_W = N_HEADS * HEAD_DIM
CHUNK_SHIFT = 6
MLA_Q_RANK = 384
MLA_KV_RANK = 256
MLA_NOPE = 64
MLA_ROPE = 32
ROPE_HALF = MLA_ROPE // 2
ROPE_BASE = 10000.0
MLA_SCALE = (MLA_NOPE + MLA_ROPE) ** -0.5
QK_SCALE = HEAD_DIM ** -0.5
D_FF = 2816
LANES = 128
SLOT_W = N_HEADS * LANES
N_PIECES = 3

C_FQ = 0
C_FK = 512
C_FV = 1024
C_SB = 1536
C_GATE = 3072
C_QL = 6144
C_KVL = 6528
C_KR = 6784
C_FF = 6912
IN_W = 7040

ROW_CHUNK = 32
PIPELINE_UNROLL = 2
SIGN_BIT = np.uint32(0x80000000)
UNDERFLOW = 105.0
NEG = -1e30
VMEM_LIMIT = 56 * 1024 * 1024


def _params(n_grid):
    return pltpu.CompilerParams(dimension_semantics=("arbitrary",) * n_grid,
                                vmem_limit_bytes=VMEM_LIMIT)


def _resident(shape):
    nd = len(shape)
    return pl.BlockSpec(shape, lambda *_: (0,) * nd, pipeline_mode=pl.Buffered(1))


def _resident_layer(shape, layer):
    nd = len(shape)
    return pl.BlockSpec((pl.Squeezed(),) + tuple(shape), lambda *_: (layer,) + (0,) * nd,
                        pipeline_mode=pl.Buffered(1))


def _bdot(a, b):
    return jnp.dot(a, b, preferred_element_type=F32)


def _dot_nt(a, b):
    return lax.dot_general(a, b, (((1,), (1,)), ((), ())), preferred_element_type=F32)


def _split3(v):
    hi = v.astype(BF16)
    r = v - hi.astype(F32)
    mid = r.astype(BF16)
    lo = (r - mid.astype(F32)).astype(BF16)
    return hi, mid, lo


def _log_sigmoid(z):
    return jnp.minimum(z, 0.0) - jnp.log1p(jnp.exp(-jnp.abs(z)))


def _sigmoid(z):
    return 1.0 / (1.0 + jnp.exp(-z))


def _rms(x):
    return x * lax.rsqrt(jnp.mean(x * x, axis=-1, keepdims=True) + EPS)


def _ada_kernel(c_ref, w_ref, b_ref, o_ref):
    c = c_ref[...]
    cond = c * _sigmoid(c)
    chi, cmid, clo = _split3(cond)
    whi, wmid, wlo = _split3(w_ref[0])
    acc = _bdot(chi, whi) + (_bdot(chi, wmid) + _bdot(cmid, whi))
    acc = acc + (_bdot(chi, wlo) + _bdot(clo, whi) + _bdot(cmid, wmid))
    o_ref[0] = acc + b_ref[0]


def _ada(c, w_ada, b_ada):
    depth, d, n = w_ada.shape
    b = c.shape[0]
    tn = 1024
    return pl.pallas_call(
        _ada_kernel,
        grid=(depth, n // tn),
        in_specs=[pl.BlockSpec((b, d), lambda l, j: (0, 0)),
                  pl.BlockSpec((1, d, tn), lambda l, j: (l, 0, j)),
                  pl.BlockSpec((1, 1, tn), lambda l, j: (l, 0, j))],
        out_specs=pl.BlockSpec((1, b, tn), lambda l, j: (l, 0, j)),
        out_shape=jax.ShapeDtypeStruct((depth, b, n), F32),
        compiler_params=_params(2),
        name="adaln",
    )(c, w_ada, b_ada.reshape(depth, 1, n))


def _rope_kernel(pos_ref, inv_ref, cos_ref, sina_ref, sinb_ref):
    ang = pos_ref[0].astype(F32) * inv_ref[...]
    lane = lax.broadcasted_iota(jnp.int32, (1, LANES), 1)
    sin = jnp.sin(ang)
    cos_ref[0] = jnp.cos(ang)
    lo, mid, hi = MLA_NOPE, MLA_NOPE + ROPE_HALF, MLA_NOPE + MLA_ROPE
    sina_ref[0] = jnp.where((lane >= lo) & (lane < mid), -sin, 0.0)
    sinb_ref[0] = jnp.where((lane >= mid) & (lane < hi), sin, 0.0)


def _rope_tables(positions):
    b, s = positions.shape
    ts = min(s, 512)
    inv = ROPE_BASE ** (-jnp.arange(ROPE_HALF, dtype=F32) / ROPE_HALF)
    inv_slot = jnp.concatenate([jnp.zeros((MLA_NOPE,), F32), inv, inv,
                                jnp.zeros((LANES - MLA_NOPE - MLA_ROPE,), F32)]).reshape(1, LANES)
    spec = pl.BlockSpec((1, ts, LANES), lambda i, j: (i, j, 0))
    return pl.pallas_call(
        _rope_kernel,
        grid=(b, s // ts),
        in_specs=[pl.BlockSpec((1, ts, 1), lambda i, j: (i, j, 0)),
                  pl.BlockSpec((1, LANES), lambda i, j: (0, 0))],
        out_specs=[spec, spec, spec],
        out_shape=[jax.ShapeDtypeStruct((b, s, LANES), F32)] * 3,
        compiler_params=_params(2),
        name="rope_tables",
    )(positions.reshape(b, s, 1), inv_slot)


def _rope_slot(x, cos, sina, sinb):
    return (x * cos + pltpu.roll(x, LANES - ROPE_HALF, 1) * sina
            + pltpu.roll(x, ROPE_HALF, 1) * sinb)


def _inproj_kernel(x_ref, mod_ref, g_ref, w_ref, bff_ref, gq_ref, gkv_ref, wuq_ref, wkn_ref,
                   wkv_ref, cos_ref, sina_ref, sinb_ref, place_ref, qc_ref, kc_ref,
                   fq_ref, fk_ref, fv_ref, sb_ref, sv_ref, gate_ref, mq_ref, mk_ref, mv_ref,
                   carry_ref, *, tm):
    x = x_ref[0]
    sh, sc = mod_ref[0, 0:1, :], mod_ref[0, 1:2, :]
    u = ((_rms(x) * g_ref[...]) * (1.0 + sc) + sh).astype(BF16)

    def seg(a, b):
        return _bdot(u, w_ref[:, a:b])

    lane = lax.broadcasted_iota(jnp.int32, (1, LANES), 1)
    cos, sina, sinb = cos_ref[0], sina_ref[0], sinb_ref[0]

    def gates(chunks):
        for c in chunks:
            a = 512 * c
            gate_ref[0, :, a:a + 512] = _sigmoid(seg(C_GATE + a, C_GATE + a + 512)).astype(BF16)

    def pack3(v):
        p0, p1, p2 = (p.astype(F32) for p in _split3(v))
        return jnp.where(lane < N_HEADS, p0,
                         jnp.where(lane < 2 * N_HEADS, pltpu.roll(p1, N_HEADS, 1),
                                   jnp.where(lane < 3 * N_HEADS, pltpu.roll(p2, 2 * N_HEADS, 1),
                                             0.0))).astype(BF16)

    def to_slots(compact, extra, out_ref):
        for h in range(N_HEADS):
            x = compact[:, LANES * (h // 2):LANES * (h // 2 + 1)]
            head = pltpu.roll(x, HEAD_DIM, 1) if h % 2 else x
            slot = slice(LANES * h, LANES * (h + 1))
            out_ref[0, :, slot] = jnp.where(lane < HEAD_DIM, head, extra[:, slot]).astype(BF16)

    forget_logits = seg(C_FF, IN_W)
    q_latent = seg(C_QL, C_KVL)
    kv_latent = seg(C_KVL, C_KR)
    rope_key = seg(C_KR, C_FF)
    log_f = pack3(_log_sigmoid(forget_logits + bff_ref[...]))
    cq = (_rms(q_latent) * gq_ref[...]).astype(BF16)
    ckv = (_rms(kv_latent) * gkv_ref[...]).astype(BF16)
    gates(range(0, 3))

    r = lax.broadcasted_iota(jnp.int32, (tm, tm), 0)
    c = lax.broadcasted_iota(jnp.int32, (tm, tm), 1)
    part = _bdot(jnp.where(r >= c, 1.0, 0.0).astype(BF16), log_f)
    q_all = _bdot(cq, wuq_ref[...])
    k_all = _bdot(ckv, wkn_ref[...])
    v_all = _bdot(ckv, wkv_ref[...])
    gates(range(3, 6))

    first_tile = pl.program_id(1) == 0
    cum = ((part + pltpu.roll(part, LANES - N_HEADS, 1))
           + pltpu.roll(part, LANES - 2 * N_HEADS, 1)) + jnp.where(first_tile, 0.0, carry_ref[...])
    carry_ref[...] = jnp.where(lane < N_HEADS, cum[tm - 1:tm, :], 0.0)
    placed = _bdot(pack3(cum), place_ref[...])
    first_aug = (lax.broadcasted_iota(jnp.int32, (1, SLOT_W), 1) & (LANES - 1)) < HEAD_DIM + N_PIECES
    fox_q = seg(C_FQ, C_FK) * QK_SCALE
    fox_k = seg(C_FK, C_FV)

    kr = _rope_slot(rope_key, cos, sina, sinb)
    for h in range(N_HEADS):
        slot = slice(LANES * h, LANES * (h + 1))
        q = _rope_slot(q_all[:, slot], cos, sina, sinb)
        mq_ref[0, :, slot] = (q * MLA_SCALE).astype(BF16)
        mk_ref[0, :, slot] = (k_all[:, slot] + kr).astype(BF16)
    mv_ref[0] = v_all.astype(BF16).T
    to_slots(fox_q, jnp.where(first_aug, qc_ref[...], placed), fq_ref)
    to_slots(fox_k, jnp.where(first_aug, placed, kc_ref[...]), fk_ref)

    sv_ref[0] = seg(C_SB + 1024, C_GATE).astype(BF16).T
    fv_ref[0] = seg(C_FV, C_SB).astype(BF16).T
    sb_ref[0, :, 0:512] = (seg(C_SB, C_SB + 512) * QK_SCALE).astype(BF16)
    sb_ref[0, :, 512:1024] = seg(C_SB + 512, C_SB + 1024).astype(BF16)


def _fox_placement():
    place = np.zeros((LANES, SLOT_W), np.float32)
    qc = np.zeros((1, SLOT_W), np.float32)
    kc = np.zeros((1, SLOT_W), np.float32)
    for h in range(N_HEADS):
        for p in range(N_PIECES):
            place[N_HEADS * p + h, LANES * h + HEAD_DIM + p] = 1.0
            place[N_HEADS * p + h, LANES * h + HEAD_DIM + N_PIECES + p] = 1.0
            qc[0, LANES * h + HEAD_DIM + p] = -1.0
            kc[0, LANES * h + HEAD_DIM + N_PIECES + p] = 1.0
    return jnp.asarray(place, BF16), jnp.asarray(qc), jnp.asarray(kc)


def _inproj(x, mod, weights, tables, *, layer, tm):
    b, s, d = x.shape
    cos, sina, sinb = tables
    place, qc, kc = _fox_placement()
    per_layer = lambda *shape: _resident_layer(shape, layer)
    row = lambda w: pl.BlockSpec((1, tm, w), lambda i, j: (i, j, 0))
    col = lambda w: pl.BlockSpec((1, w, tm), lambda i, j: (i, 0, j))
    outs = [(SLOT_W, False),
            (SLOT_W, False),
            (HEAD_W, True),
            (2 * HEAD_W, False),
            (HEAD_W, True),
            (3 * D_MODEL, False),
            (SLOT_W, False),
            (SLOT_W, False),
            (HEAD_W, True)]
    out_specs = [col(w) if tr else row(w) for w, tr in outs]
    out_shape = [jax.ShapeDtypeStruct((b, w, s) if tr else (b, s, w), BF16) for w, tr in outs]
    return pl.pallas_call(
        functools.partial(_inproj_kernel, tm=tm),
        grid=(b, s // tm),
        in_specs=[row(d),
                  pl.BlockSpec((1, 6, d), lambda i, j: (i, 0, 0)),
                  per_layer(1, d), per_layer(d, IN_W), per_layer(1, LANES),
                  per_layer(1, MLA_Q_RANK), per_layer(1, MLA_KV_RANK),
                  per_layer(MLA_Q_RANK, SLOT_W), per_layer(MLA_KV_RANK, SLOT_W),
                  per_layer(MLA_KV_RANK, HEAD_W),
                  row(LANES), row(LANES), row(LANES),
                  _resident((LANES, SLOT_W)), _resident((1, SLOT_W)), _resident((1, SLOT_W))],
        out_specs=out_specs,
        out_shape=out_shape,
        scratch_shapes=[pltpu.VMEM((1, LANES), F32)],
        compiler_params=_params(2),
        name="inproj",
    )(x, mod, *weights, cos, sina, sinb, place, qc, kc)


def _head_rows():
    row = lax.broadcasted_iota(jnp.int32, (LANES, 1), 0)
    return [(row >= HEAD_DIM * hh) & (row < HEAD_DIM * (hh + 1)) for hh in range(2)]


def _head_masks():
    lane = lax.broadcasted_iota(jnp.int32, (1, LANES), 1)
    return [(lane >= HEAD_DIM * hh) & (lane < HEAD_DIM * (hh + 1)) for hh in range(2)]


def _colmax(x, chunk=64):
    rows = x.shape[0]
    if rows > chunk and rows % chunk == 0:
        acc = x[:chunk]
        for a in range(chunk, rows, chunk):
            acc = jnp.maximum(acc, x[a:a + chunk])
        x, rows = acc, chunk
    while rows > 8 and rows % 16 == 0:
        rows //= 2
        x = jnp.maximum(x[:rows], x[rows:])
    return jnp.max(x, axis=0, keepdims=True)


def _pipeline(n, stages, unroll=PIPELINE_UNROLL, n_static=0):
    stages = [st if isinstance(st, tuple) else (k, st) for k, st in enumerate(stages)]
    depth = max(s for s, _ in stages) + 1

    def static_step(t):
        for s, stage in stages:
            if 0 <= t - s < n:
                stage(t - s, (t - s) % 2)

    first_full = min(max(depth - 1, n_static), n + depth - 1)
    n_full = max(n - first_full, 0)
    for t in range(first_full):
        static_step(t)
    n_loops = n_full // unroll
    if n_loops >= 1:
        def body(it, carry):
            t = first_full + unroll * it
            for u in range(unroll):
                for s, stage in stages:
                    stage(t + u - s, (first_full + u - s) % 2)
            return carry

        lax.fori_loop(0, n_loops, body, 0)
    for t in range(first_full + unroll * n_loops, n + depth - 1):
        static_step(t)


def _for_each_tile(n_tiles, fn, group=4):
    while n_tiles % group:
        group //= 2

    def body(i, carry):
        for u in range(group):
            fn(i * group + u)
        return carry

    lax.fori_loop(0, n_tiles // group, body, 0)


def _softmax_block_table(n_tiles, t):
    diag = [(i * t, i * t) for i in range(n_tiles)]
    below = [(i * t, j * t) for i in range(1, n_tiles) for j in range(i)]
    return jnp.asarray(np.array(diag + below, np.int32).T), len(diag), len(below)


def _softmax_flat_kernel(tab_ref, q_ref, k_ref, vt_ref, o_ref, acc_ref, m_ref, al_ref, top_ref,
                         s0_ref, s1_ref, p0_ref, p1_ref, *, t, n_diag, n_below, chunk_shift):
    own_rows = _head_rows()
    s_bufs, p_bufs = (s0_ref, s1_ref), (p0_ref, p1_ref)
    kio = lax.broadcasted_iota(jnp.int32, (t, 1), 0)
    last_visible = lax.broadcasted_iota(jnp.int32, (1, t), 1) | ((1 << chunk_shift) - 1)
    def item(i):
        return pl.multiple_of(tab_ref[0, i], t), pl.multiple_of(tab_ref[1, i], t)

    def is_diag(i):
        return isinstance(i, int) and i < n_diag

    half = t // 2

    def scores(hh, i, par):
        q0, k0 = item(i)
        slot = slice(LANES * hh, LANES * (hh + 1))
        if is_diag(i):
            s_lo = _dot_nt(k_ref[0, pl.ds(k0, half), slot], q_ref[0, pl.ds(q0, t), slot])
            s_hi = _dot_nt(k_ref[0, pl.ds(k0 + half, half), slot],
                           q_ref[0, pl.ds(q0 + half, half), slot])
            s_lo = jnp.where(kio[:half] <= last_visible, s_lo, NEG)
            s_hi = jnp.where(kio[:half] <= last_visible[:, :half], s_hi, NEG)
            s_bufs[par][hh, :half, :] = s_lo
            s_bufs[par][hh, half:, half:] = s_hi
            top_ref[par, hh] = jnp.concatenate(
                [_colmax(s_lo[:, :half]), jnp.maximum(_colmax(s_lo[:, half:]), _colmax(s_hi))],
                axis=1)
        else:
            s = _dot_nt(k_ref[0, pl.ds(k0, t), slot], q_ref[0, pl.ds(q0, t), slot])
            s_bufs[par][hh] = s
            top_ref[par, hh] = _colmax(s)

    def numerators(i, par):
        q0, _ = item(i)
        for hh in range(2):
            m_new = top_ref[par, hh]
            if not is_diag(i):
                m_old = m_ref[hh, :, pl.ds(q0, t)]
                m_new = jnp.maximum(m_old, m_new)
                al_ref[par, hh] = jnp.exp(m_old - m_new)
            for a in range(0, t, ROW_CHUNK):
                rows = slice(a, a + ROW_CHUNK)
                if is_diag(i) and a >= half:
                    p_bufs[par][hh, rows, :half] = jnp.zeros((ROW_CHUNK, half), BF16)
                    p_bufs[par][hh, rows, half:] = jnp.exp(
                        s_bufs[par][hh, rows, half:] - m_new[:, half:]).astype(BF16)
                else:
                    p_bufs[par][hh, rows, :] = jnp.exp(s_bufs[par][hh, rows, :] - m_new).astype(BF16)
            m_ref[hh, :, pl.ds(q0, t)] = m_new

    def values(hh, i, par):
        q0, k0 = item(i)
        vt = vt_ref[0, :, pl.ds(k0, t)]
        v1 = jnp.where(own_rows[hh], vt, jnp.ones_like(vt))
        pv = _bdot(v1, p_bufs[par][hh])
        if not is_diag(i):
            pv = acc_ref[hh, :, pl.ds(q0, t)] * al_ref[par, hh] + pv
        acc_ref[hh, :, pl.ds(q0, t)] = pv

    head = functools.partial
    stages = [(0, head(scores, 0)), (0, head(scores, 1)), (1, numerators),
              (2, head(values, 0)), (2, head(values, 1))]
    _pipeline(n_diag + n_below, stages, n_static=n_diag + max(s for s, _ in stages))

    sub = lax.broadcasted_iota(jnp.int32, (LANES, 1), 0)

    def finish(tile):
        q0 = pl.multiple_of(tile * t, t)
        outs = []
        for hh in range(2):
            acc = acc_ref[hh, :, pl.ds(q0, t)]
            denom = acc[HEAD_DIM * (1 - hh):HEAD_DIM * (1 - hh) + 1, :]
            outs.append(acc / denom)
        o_ref[0, pl.ds(q0, t), :] = jnp.where(sub < HEAD_DIM, outs[0], outs[1]).astype(o_ref.dtype).T

    _for_each_tile(n_diag, finish)


def _softmax_attention_flat(q, k, vt, *, t, chunk_shift, name):
    b, s, _ = k.shape
    table, n_diag, n_below = _softmax_block_table(s // t, t)
    grid_spec = pltpu.PrefetchScalarGridSpec(
        num_scalar_prefetch=1,
        grid=(b, N_HEADS // 2),
        in_specs=[pl.BlockSpec((1, s, 2 * LANES), lambda i, h, tab: (i, 0, h)),
                  pl.BlockSpec((1, s, 2 * LANES), lambda i, h, tab: (i, 0, h)),
                  pl.BlockSpec((1, LANES, s), lambda i, h, tab: (i, h, 0))],
        out_specs=pl.BlockSpec((1, s, LANES), lambda i, h, tab: (i, 0, h)),
        scratch_shapes=[pltpu.VMEM((2, LANES, s), F32),
                        pltpu.VMEM((2, 1, s), F32),
                        pltpu.VMEM((2, 2, 1, t), F32),
                        pltpu.VMEM((2, 2, 1, t), F32),
                        pltpu.VMEM((2, t, t), F32), pltpu.VMEM((2, t, t), F32),
                        pltpu.VMEM((2, t, t), BF16), pltpu.VMEM((2, t, t), BF16)])
    return pl.pallas_call(
        functools.partial(_softmax_flat_kernel, t=t, n_diag=n_diag, n_below=n_below,
                          chunk_shift=chunk_shift),
        grid_spec=grid_spec,
        out_shape=jax.ShapeDtypeStruct((b, s, HEAD_W), BF16),
        compiler_params=_params(2),
        name=name,
    )(table, q, k, vt)


def _sb_block_table(n_tiles, t):
    diag = [(i * t, i * t, i * t, i) for i in range(n_tiles)]
    below = [(i * t, (i - d) * t, i * t, i) for d in range(1, n_tiles) for i in range(d, n_tiles)]
    idle = [(0, 0, n_tiles * t, n_tiles)]
    return jnp.asarray(np.array(diag + below + idle, np.int32).T), len(diag), len(below)


def _sb_flat_kernel(tab_ref, q_ref, k_ref, vt_ref, o_ref, acc_ref, r_ref, tot_ref, d_ref, done_ref,
                    busy_ref, z0_ref, z1_ref, e0_ref, e1_ref, w0_ref, w1_ref, *, t, n_diag,
                    n_below):
    in_head = _head_masks()
    own_rows = _head_rows()
    z_bufs, e_bufs, w_bufs = (z0_ref, z1_ref), (e0_ref, e1_ref), (w0_ref, w1_ref)
    kio = lax.broadcasted_iota(jnp.int32, (t, 1), 0)
    qio = lax.broadcasted_iota(jnp.int32, (1, t), 1)
    ss = lax.broadcasted_iota(jnp.int32, (t, t), 0)
    jj = lax.broadcasted_iota(jnp.int32, (t, t), 1)
    later = jnp.where(jj > ss, 1.0, 0.0).astype(BF16)
    acc_ref[...] = jnp.zeros_like(acc_ref)
    r_ref[...] = jnp.zeros_like(r_ref)
    for tile in range(n_diag + 1):
        done_ref[tile] = 0
        busy_ref[tile] = 0

    def stages(base, diag):
        def item(i):
            return (pl.multiple_of(tab_ref[0, base + i], t), pl.multiple_of(tab_ref[1, base + i], t),
                    pl.multiple_of(tab_ref[2, base + i], t), tab_ref[3, base + i])

        def scores(i, par):
            q0, k0, _, _ = item(i)
            k = k_ref[0, pl.ds(k0, t), :]
            q = q_ref[0, pl.ds(q0, t), :]
            for hh in range(2):
                qh = jnp.where(in_head[hh], q, jnp.zeros_like(q))
                z_bufs[par][hh] = _dot_nt(k, qh)

        def drops(i, par):
            _, _, q0, tile = item(i)
            slack = None
            for hh in range(2):
                for a in range(0, t, ROW_CHUNK):
                    rows = slice(a, a + ROW_CHUNK)
                    z = z_bufs[par][hh, rows, :]
                    neg_abs = pltpu.bitcast(pltpu.bitcast(z, jnp.uint32) | SIGN_BIT, F32)
                    drop = jnp.maximum(z, 0.0) + jnp.log(1.0 + jnp.exp(neg_abs))
                    log_beta = z - drop
                    if diag:
                        drop = jnp.where(kio[rows] < qio, drop, 0.0)
                        log_beta = jnp.where(kio[rows] < qio, log_beta, NEG)
                    d_ref[hh, rows, :] = drop.astype(BF16)
                    e_bufs[par][hh, rows, :] = log_beta
                after = _bdot(later, d_ref[hh])
                for a in range(0, t, ROW_CHUNK):
                    rows = slice(a, a + ROW_CHUNK)
                    e_bufs[par][hh, rows, :] = e_bufs[par][hh, rows, :] - after[rows]
                total = after[0:1, :] + d_ref[hh, 0:1, :].astype(F32)
                tot_ref[par, hh] = total
                r_new = r_ref[hh, :, pl.ds(q0, t)] + total
                slack = r_new if slack is None else jnp.minimum(slack, r_new)
            if not diag:
                done_ref[tile] = (jnp.min(slack) > UNDERFLOW).astype(jnp.int32)
                busy_ref[tile] = 0

        def weights(i, par):
            _, _, q0, _ = item(i)
            for hh in range(2):
                r_old = r_ref[hh, :, pl.ds(q0, t)]
                for a in range(0, t, ROW_CHUNK):
                    rows = slice(a, a + ROW_CHUNK)
                    w_bufs[par][hh, rows, :] = jnp.exp(e_bufs[par][hh, rows, :] - r_old).astype(BF16)
                r_ref[hh, :, pl.ds(q0, t)] = r_old + tot_ref[par, hh]

        def values(i, par):
            _, k0, q0, _ = item(i)
            vt = vt_ref[0, :, pl.ds(k0, t)]
            zero = jnp.zeros_like(vt)
            lhs = jnp.concatenate([jnp.where(own_rows[hh], vt, zero) for hh in range(2)], axis=1)
            acc_ref[:, pl.ds(q0, t)] = (acc_ref[:, pl.ds(q0, t)]
                                        + _bdot(lhs, w_bufs[par][...].reshape(2 * t, t)))

        return [scores, drops, weights, values]

    _pipeline(n_diag, stages(0, True))

    scores, drops, weights, values = stages(n_diag, False)

    def next_live(c):
        return lax.while_loop(
            lambda c: (c < n_below) & (done_ref[tab_ref[3, n_diag + c]] != 0), lambda c: c + 1, c)

    def pick(cursor):
        c = next_live(cursor)
        wait = (c < n_below) & (busy_ref[tab_ref[3, n_diag + c]] != 0)
        issue = jnp.where(wait, n_below, c)
        busy_ref[tab_ref[3, n_diag + issue]] = 1
        return issue, jnp.where(wait, c, jnp.minimum(c + 1, n_below))

    def two_steps(state):
        a, b, cursor, i1, i2, i3 = state
        scores(a, 0)
        drops(i1, 1)
        weights(i2, 0)
        values(i3, 1)
        scores(b, 1)
        drops(a, 0)
        weights(i1, 1)
        values(i2, 0)
        a_next, cursor = pick(cursor)
        b_next, cursor = pick(cursor)
        return a_next, b_next, cursor, b, a, i1

    idle = jnp.int32(n_below)
    a0, cursor = pick(jnp.int32(0))
    b0, cursor = pick(cursor)
    lax.while_loop(lambda st: functools.reduce(lambda x, y: x | y, [v < n_below for v in st]),
                   two_steps, (a0, b0, cursor, idle, idle, idle))

    def finish(tile):
        q0 = pl.multiple_of(tile * t, t)
        o_ref[0, pl.ds(q0, t), :] = acc_ref[:, pl.ds(q0, t)].astype(o_ref.dtype).T

    _for_each_tile(n_diag, finish)


def _sb_attention_flat(qk, vt, *, t):
    b, s, _ = qk.shape
    table, n_diag, n_below = _sb_block_table(s // t, t)
    grid_spec = pltpu.PrefetchScalarGridSpec(
        num_scalar_prefetch=1,
        grid=(b, N_HEADS // 2),
        in_specs=[pl.BlockSpec((1, s, LANES), lambda i, h, tab: (i, 0, h)),
                  pl.BlockSpec((1, s, LANES), lambda i, h, tab: (i, 0, 4 + h)),
                  pl.BlockSpec((1, LANES, s), lambda i, h, tab: (i, h, 0))],
        out_specs=pl.BlockSpec((1, s, LANES), lambda i, h, tab: (i, 0, h)),
        scratch_shapes=[pltpu.VMEM((LANES, s + t), F32),
                        pltpu.VMEM((2, 1, s + t), F32),
                        pltpu.VMEM((2, 2, 1, t), F32),
                        pltpu.VMEM((2, t, t), BF16),
                        pltpu.SMEM((s // t + 1,), jnp.int32),
                        pltpu.SMEM((s // t + 1,), jnp.int32),
                        pltpu.VMEM((2, t, t), F32), pltpu.VMEM((2, t, t), F32),
                        pltpu.VMEM((2, t, t), F32), pltpu.VMEM((2, t, t), F32),
                        pltpu.VMEM((2, t, t), BF16), pltpu.VMEM((2, t, t), BF16)])
    return pl.pallas_call(
        functools.partial(_sb_flat_kernel, t=t, n_diag=n_diag, n_below=n_below),
        grid_spec=grid_spec,
        out_shape=jax.ShapeDtypeStruct((b, s, HEAD_W), BF16),
        compiler_params=_params(2),
        name="sb_attention",
    )(table, qk, qk, vt)


def _merge_ffn_kernel(x_ref, yf_ref, ym_ref, ys_ref, gate_ref, mod_ref, wf_ref, wm_ref, ws_ref,
                      wo_ref, g_ref, wg_ref, wu_ref, wd_ref, gfin_ref, o_ref, *, final, fc):
    d = D_MODEL
    merged = (gate_ref[0, :, 0:d].astype(F32) * _bdot(yf_ref[0], wf_ref[...])
              + gate_ref[0, :, d:2 * d].astype(F32) * _bdot(ym_ref[0], wm_ref[...])
              + gate_ref[0, :, 2 * d:3 * d].astype(F32) * _bdot(ys_ref[0], ws_ref[...]))
    x = x_ref[0] + mod_ref[0, 2:3, :] * _bdot(merged.astype(BF16), wo_ref[...])

    sh, sc, gt = mod_ref[0, 3:4, :], mod_ref[0, 4:5, :], mod_ref[0, 5:6, :]
    u = ((_rms(x) * g_ref[...]) * (1.0 + sc) + sh).astype(BF16)
    acc = None
    for a in range(0, D_FF, fc):
        hg = _bdot(u, wg_ref[:, a:a + fc])
        hu = _bdot(u, wu_ref[:, a:a + fc])
        h = ((hg * _sigmoid(hg)) * hu).astype(BF16)
        part = _bdot(h, wd_ref[a:a + fc, :])
        acc = part if acc is None else acc + part
    out = x + gt * acc
    if final:
        out = _rms(out) * gfin_ref[...]
    o_ref[0] = out


def _merge_ffn(x, y_fox, y_mla, y_sb, gates, mod, weights, g_final, *, layer, tm, final):
    b, s, d = x.shape
    row = lambda w: pl.BlockSpec((1, tm, w), lambda i, j: (i, j, 0))
    per_layer = lambda *shape: _resident_layer(shape, layer)
    return pl.pallas_call(
        functools.partial(_merge_ffn_kernel, final=final, fc=D_FF // 2),
        grid=(b, s // tm),
        in_specs=[row(d), row(HEAD_W), row(HEAD_W), row(HEAD_W), row(3 * d),
                  pl.BlockSpec((1, 6, d), lambda i, j: (i, 0, 0)),
                  per_layer(HEAD_W, d), per_layer(HEAD_W, d), per_layer(HEAD_W, d),
                  per_layer(d, d), per_layer(1, d),
                  per_layer(d, D_FF), per_layer(d, D_FF), per_layer(D_FF, d),
                  _resident((1, d))],
        out_specs=row(d),
        out_shape=jax.ShapeDtypeStruct((b, s, d), F32),
        compiler_params=_params(2),
        name="merge_ffn",
    )(x, y_fox, y_mla, y_sb, gates, mod, *weights, g_final)


def _slot_cols(w, width_per_head, used):
    lead = w.shape[:-1]
    w = w.reshape(lead + (N_HEADS, width_per_head))[..., :used]
    pad = [(0, 0)] * (w.ndim - 1) + [(0, LANES - used)]
    return jnp.pad(w, pad).reshape(lead + (SLOT_W,))


def _layout_w_in(w):
    fox, ff, ql, kvl, kr, sb, gate = jnp.split(
        w, [1536, 1544, 1928, 2184, 2216, 3752], axis=-1)
    z = lambda n: jnp.zeros(w.shape[:-1] + (n,), w.dtype)
    kr_slot = jnp.concatenate([z(MLA_NOPE), kr, z(LANES - MLA_NOPE - MLA_ROPE)], axis=-1)
    ff_slot = jnp.concatenate([ff, z(LANES - N_HEADS)], axis=-1)
    return jnp.concatenate([fox, sb, gate, ql, kvl, kr_slot, ff_slot], axis=-1).astype(BF16)


def kernel(x, c, positions, g_mix, w_ada, b_ada, w_in, b_fox_f, g_mla_q, w_mla_uq, g_mla_kv,
           w_mla_ukv, w_o_fox, w_o_mla, w_o_sb, w_out, g_ffn, w_ffn_gate, w_ffn_up, w_ffn_down,
           g_final):
    b, s, d = x.shape
    depth = w_in.shape[0]
    tm = min(s, 512)
    tm_ffn = min(s, 512)
    ta = min(s, 512)
    tsb = min(s, 256)

    mod = _ada(c, w_ada, b_ada).reshape(depth, b, 6, d)
    tables = _rope_tables(positions)

    row_vec = lambda v: v.reshape(depth, 1, -1)
    wkv = w_mla_ukv.reshape(depth, MLA_KV_RANK, N_HEADS, 2 * HEAD_DIM)[..., MLA_NOPE:]
    inproj_weights = (
        row_vec(g_mix), _layout_w_in(w_in),
        row_vec(jnp.pad(b_fox_f, ((0, 0), (0, LANES - N_HEADS)))),
        row_vec(g_mla_q), row_vec(g_mla_kv),
        _slot_cols(w_mla_uq, MLA_NOPE + MLA_ROPE, MLA_NOPE + MLA_ROPE).astype(BF16),
        _slot_cols(w_mla_ukv, 2 * HEAD_DIM, MLA_NOPE).astype(BF16),
        wkv.reshape(depth, MLA_KV_RANK, HEAD_W).astype(BF16))
    merge_ffn_weights = (
        w_o_fox.astype(BF16), w_o_mla.astype(BF16), w_o_sb.astype(BF16), w_out.astype(BF16),
        row_vec(g_ffn), w_ffn_gate.astype(BF16), w_ffn_up.astype(BF16), w_ffn_down.astype(BF16))

    for l in range(depth):
        fox_q, fox_k, fox_vt, sb_qk, sb_vt, gates, mq, mk, mvt = _inproj(
            x, mod[l], inproj_weights, tables, layer=l, tm=tm)
        y_fox = _softmax_attention_flat(fox_q, fox_k, fox_vt, t=ta, chunk_shift=0,
                                        name="fox_attention")
        y_mla = _softmax_attention_flat(mq, mk, mvt, t=ta, chunk_shift=CHUNK_SHIFT,
                                        name="mla_attention")
        y_sb = _sb_attention_flat(sb_qk, sb_vt, t=tsb)
        x = _merge_ffn(x, y_fox, y_mla, y_sb, gates, mod[l], merge_ffn_weights,
                       g_final.reshape(1, d), layer=l, tm=tm_ffn, final=(l == depth - 1))
    return x
```

```python
import functools

import numpy as np
import jax
import jax.numpy as jnp
from jax import lax
from jax.experimental import pallas as pl
from jax.experimental.pallas import tpu as pltpu

F32 = jnp.float32
BF16 = jnp.bfloat16

D_MODEL = 1024
EPS = 1e-6
HEAD_DIM = 64
N_HEADS = 8
HEAD_W = N_HEADS * HEAD_DIM
CHUNK_SHIFT = 6
MLA_Q_RANK = 384
MLA_KV_RANK = 256
MLA_NOPE = 64
MLA_ROPE = 32
ROPE_HALF = MLA_ROPE // 2
ROPE_BASE = 10000.0
MLA_SCALE = (MLA_NOPE + MLA_ROPE) ** -0.5
QK_SCALE = HEAD_DIM ** -0.5
D_FF = 2816
LANES = 128
SLOT_W = N_HEADS * LANES
N_PIECES = 3

C_FQ = 0
C_FK = 512
C_FV = 1024
C_SB = 1536
C_GATE = 3072
C_QL = 6144
C_KVL = 6528
C_KR = 6784
C_FF = 6912
IN_W = 7040

ROW_CHUNK = 32
PIPELINE_UNROLL = 2
SIGN_BIT = np.uint32(0x80000000)
UNDERFLOW = 105.0
NEG = -1e30
VMEM_LIMIT = 56 * 1024 * 1024


def _params(n_grid):
    return pltpu.CompilerParams(dimension_semantics=("arbitrary",) * n_grid,
                                vmem_limit_bytes=VMEM_LIMIT)


def _resident(shape):
    nd = len(shape)
    return pl.BlockSpec(shape, lambda *_: (0,) * nd, pipeline_mode=pl.Buffered(1))


def _resident_layer(shape, layer):
    nd = len(shape)
    return pl.BlockSpec((pl.Squeezed(),) + tuple(shape), lambda *_: (layer,) + (0,) * nd,
                        pipeline_mode=pl.Buffered(1))


def _bdot(a, b):
    return jnp.dot(a, b, preferred_element_type=F32)


def _dot_nt(a, b):
    return lax.dot_general(a, b, (((1,), (1,)), ((), ())), preferred_element_type=F32)


def _split3(v):
    hi = v.astype(BF16)
    r = v - hi.astype(F32)
    mid = r.astype(BF16)
    lo = (r - mid.astype(F32)).astype(BF16)
    return hi, mid, lo


def _log_sigmoid(z):
    return jnp.minimum(z, 0.0) - jnp.log1p(jnp.exp(-jnp.abs(z)))


def _sigmoid(z):
    return 1.0 / (1.0 + jnp.exp(-z))


def _rms(x):
    return x * lax.rsqrt(jnp.mean(x * x, axis=-1, keepdims=True) + EPS)


def _modulated_norm(x, g, scale, shift):
    return (_rms(x) * (g * (1.0 + scale)) + shift).astype(BF16)


def _ada_kernel(c_ref, w_ref, b_ref, o_ref):
    c = c_ref[...]
    cond = c * _sigmoid(c)
    chi, cmid, clo = _split3(cond)
    whi, wmid, wlo = _split3(w_ref[0])
    acc = _bdot(chi, whi) + (_bdot(chi, wmid) + _bdot(cmid, whi))
    acc = acc + (_bdot(chi, wlo) + _bdot(clo, whi) + _bdot(cmid, wmid))
    o_ref[0] = acc + b_ref[0]


def _ada(c, w_ada, b_ada):
    depth, d, n = w_ada.shape
    b = c.shape[0]
    tn = 1024
    return pl.pallas_call(
        _ada_kernel,
        grid=(depth, n // tn),
        in_specs=[pl.BlockSpec((b, d), lambda l, j: (0, 0)),
                  pl.BlockSpec((1, d, tn), lambda l, j: (l, 0, j)),
                  pl.BlockSpec((1, 1, tn), lambda l, j: (l, 0, j))],
        out_specs=pl.BlockSpec((1, b, tn), lambda l, j: (l, 0, j)),
        out_shape=jax.ShapeDtypeStruct((depth, b, n), F32),
        compiler_params=_params(2),
        name="adaln",
    )(c, w_ada, b_ada.reshape(depth, 1, n))


def _rope_kernel(pos_ref, inv_ref, cos_ref, sina_ref, sinb_ref):
    ang = pos_ref[0].astype(F32) * inv_ref[...]
    lane = lax.broadcasted_iota(jnp.int32, (1, LANES), 1)
    sin = jnp.sin(ang)
    cos_ref[0] = jnp.cos(ang)
    lo, mid, hi = MLA_NOPE, MLA_NOPE + ROPE_HALF, MLA_NOPE + MLA_ROPE
    sina_ref[0] = jnp.where((lane >= lo) & (lane < mid), -sin, 0.0)
    sinb_ref[0] = jnp.where((lane >= mid) & (lane < hi), sin, 0.0)


def _rope_tables(positions):
    b, s = positions.shape
    ts = min(s, 512)
    inv = ROPE_BASE ** (-jnp.arange(ROPE_HALF, dtype=F32) / ROPE_HALF)
    inv_slot = jnp.concatenate([jnp.zeros((MLA_NOPE,), F32), inv, inv,
                                jnp.zeros((LANES - MLA_NOPE - MLA_ROPE,), F32)]).reshape(1, LANES)
    spec = pl.BlockSpec((1, ts, LANES), lambda i, j: (i, j, 0))
    return pl.pallas_call(
        _rope_kernel,
        grid=(b, s // ts),
        in_specs=[pl.BlockSpec((1, ts, 1), lambda i, j: (i, j, 0)),
                  pl.BlockSpec((1, LANES), lambda i, j: (0, 0))],
        out_specs=[spec, spec, spec],
        out_shape=[jax.ShapeDtypeStruct((b, s, LANES), F32)] * 3,
        compiler_params=_params(2),
        name="rope_tables",
    )(positions.reshape(b, s, 1), inv_slot)


def _rope_slot(x, cos, sina, sinb):
    return (x * cos + pltpu.roll(x, LANES - ROPE_HALF, 1) * sina
            + pltpu.roll(x, ROPE_HALF, 1) * sinb)


def _inproj_kernel(x_ref, mod_ref, g_ref, w_ref, bff_ref, gq_ref, gkv_ref, wuq_ref, wkn_ref,
                   wkv_ref, cos_ref, sina_ref, sinb_ref, place_ref, qc_ref, kc_ref,
                   fq_ref, fk_ref, fv_ref, sb_ref, sv_ref, gate_ref, mq_ref, mk_ref, mv_ref,
                   carry_ref, *, tm):
    x = x_ref[0]
    sh, sc = mod_ref[0, 0:1, :], mod_ref[0, 1:2, :]
    u = _modulated_norm(x, g_ref[...], sc, sh)

    def seg(a, b):
        return _bdot(u, w_ref[:, a:b])

    lane = lax.broadcasted_iota(jnp.int32, (1, LANES), 1)
    cos, sina, sinb = cos_ref[0], sina_ref[0], sinb_ref[0]

    def gates(chunks):
        for c in chunks:
            a = 512 * c
            gate_ref[0, :, a:a + 512] = _sigmoid(seg(C_GATE + a, C_GATE + a + 512)).astype(BF16)

    def pack3(v):
        p0, p1, p2 = (p.astype(F32) for p in _split3(v))
        return jnp.where(lane < N_HEADS, p0,
                         jnp.where(lane < 2 * N_HEADS, pltpu.roll(p1, N_HEADS, 1),
                                   jnp.where(lane < 3 * N_HEADS, pltpu.roll(p2, 2 * N_HEADS, 1),
                                             0.0))).astype(BF16)

    def to_slots(compact, extra, out_ref):
        for h in range(N_HEADS):
            x = compact[:, LANES * (h // 2):LANES * (h // 2 + 1)]
            head = pltpu.roll(x, HEAD_DIM, 1) if h % 2 else x
            slot = slice(LANES * h, LANES * (h + 1))
            out_ref[0, :, slot] = jnp.where(lane < HEAD_DIM, head, extra[:, slot]).astype(BF16)

    forget_logits = seg(C_FF, IN_W)
    q_latent = seg(C_QL, C_KVL)
    kv_latent = seg(C_KVL, C_KR)
    rope_key = seg(C_KR, C_FF)
    log_f = pack3(_log_sigmoid(forget_logits + bff_ref[...]))
    cq = (_rms(q_latent) * gq_ref[...]).astype(BF16)
    ckv = (_rms(kv_latent) * gkv_ref[...]).astype(BF16)
    gates(range(0, 3))

    r = lax.broadcasted_iota(jnp.int32, (tm, tm), 0)
    c = lax.broadcasted_iota(jnp.int32, (tm, tm), 1)
    part = _bdot(jnp.where(r >= c, 1.0, 0.0).astype(BF16), log_f)
    q_all = _bdot(cq, wuq_ref[...])
    k_all = _bdot(ckv, wkn_ref[...])
    v_all = _bdot(ckv, wkv_ref[...])
    gates(range(3, 6))

    first_tile = pl.program_id(1) == 0
    cum = ((part + pltpu.roll(part, LANES - N_HEADS, 1))
           + pltpu.roll(part, LANES - 2 * N_HEADS, 1)) + jnp.where(first_tile, 0.0, carry_ref[...])
    carry_ref[...] = jnp.where(lane < N_HEADS, cum[tm - 1:tm, :], 0.0)
    placed = _bdot(pack3(cum), place_ref[...])
    first_aug = (lax.broadcasted_iota(jnp.int32, (1, SLOT_W), 1) & (LANES - 1)) < HEAD_DIM + N_PIECES
    fox_q = seg(C_FQ, C_FK) * QK_SCALE
    fox_k = seg(C_FK, C_FV)

    kr = _rope_slot(rope_key, cos, sina, sinb)
    for h in range(N_HEADS):
        slot = slice(LANES * h, LANES * (h + 1))
        q = _rope_slot(q_all[:, slot], cos, sina, sinb)
        mq_ref[0, :, slot] = (q * MLA_SCALE).astype(BF16)
        mk_ref[0, :, slot] = (k_all[:, slot] + kr).astype(BF16)
    mv_ref[0] = v_all.astype(BF16).T
    to_slots(fox_q, jnp.where(first_aug, qc_ref[...], placed), fq_ref)
    to_slots(fox_k, jnp.where(first_aug, placed, kc_ref[...]), fk_ref)

    sv_ref[0] = seg(C_SB + 1024, C_GATE).astype(BF16).T
    fv_ref[0] = seg(C_FV, C_SB).astype(BF16).T
    sb_ref[0, :, 0:512] = (seg(C_SB, C_SB + 512) * QK_SCALE).astype(BF16)
    sb_ref[0, :, 512:1024] = seg(C_SB + 512, C_SB + 1024).astype(BF16)


def _fox_placement():
    place = np.zeros((LANES, SLOT_W), np.float32)
    qc = np.zeros((1, SLOT_W), np.float32)
    kc = np.zeros((1, SLOT_W), np.float32)
    for h in range(N_HEADS):
        for p in range(N_PIECES):
            place[N_HEADS * p + h, LANES * h + HEAD_DIM + p] = 1.0
            place[N_HEADS * p + h, LANES * h + HEAD_DIM + N_PIECES + p] = 1.0
            qc[0, LANES * h + HEAD_DIM + p] = -1.0
            kc[0, LANES * h + HEAD_DIM + N_PIECES + p] = 1.0
    return jnp.asarray(place, BF16), jnp.asarray(qc), jnp.asarray(kc)


def _inproj(x, mod, weights, tables, *, layer, tm):
    b, s, d = x.shape
    cos, sina, sinb = tables
    place, qc, kc = _fox_placement()
    per_layer = lambda *shape: _resident_layer(shape, layer)
    row = lambda w: pl.BlockSpec((1, tm, w), lambda i, j: (i, j, 0))
    col = lambda w: pl.BlockSpec((1, w, tm), lambda i, j: (i, 0, j))
    outs = [(SLOT_W, False),
            (SLOT_W, False),
            (HEAD_W, True),
            (2 * HEAD_W, False),
            (HEAD_W, True),
            (3 * D_MODEL, False),
            (SLOT_W, False),
            (SLOT_W, False),
            (HEAD_W, True)]
    out_specs = [col(w) if tr else row(w) for w, tr in outs]
    out_shape = [jax.ShapeDtypeStruct((b, w, s) if tr else (b, s, w), BF16) for w, tr in outs]
    return pl.pallas_call(
        functools.partial(_inproj_kernel, tm=tm),
        grid=(b, s // tm),
        in_specs=[row(d),
                  pl.BlockSpec((1, 6, d), lambda i, j: (i, 0, 0)),
                  per_layer(1, d), per_layer(d, IN_W), per_layer(1, LANES),
                  per_layer(1, MLA_Q_RANK), per_layer(1, MLA_KV_RANK),
                  per_layer(MLA_Q_RANK, SLOT_W), per_layer(MLA_KV_RANK, SLOT_W),
                  per_layer(MLA_KV_RANK, HEAD_W),
                  row(LANES), row(LANES), row(LANES),
                  _resident((LANES, SLOT_W)), _resident((1, SLOT_W)), _resident((1, SLOT_W))],
        out_specs=out_specs,
        out_shape=out_shape,
        scratch_shapes=[pltpu.VMEM((1, LANES), F32)],
        compiler_params=_params(2),
        name="inproj",
    )(x, mod, *weights, cos, sina, sinb, place, qc, kc)


def _head_rows():
    row = lax.broadcasted_iota(jnp.int32, (LANES, 1), 0)
    return [(row >= HEAD_DIM * hh) & (row < HEAD_DIM * (hh + 1)) for hh in range(2)]


def _head_masks():
    lane = lax.broadcasted_iota(jnp.int32, (1, LANES), 1)
    return [(lane >= HEAD_DIM * hh) & (lane < HEAD_DIM * (hh + 1)) for hh in range(2)]


def _colmax(x, chunk=64):
    rows = x.shape[0]
    if rows > chunk and rows % chunk == 0:
        acc = x[:chunk]
        for a in range(chunk, rows, chunk):
            acc = jnp.maximum(acc, x[a:a + chunk])
        x, rows = acc, chunk
    while rows > 8 and rows % 16 == 0:
        rows //= 2
        x = jnp.maximum(x[:rows], x[rows:])
    return jnp.max(x, axis=0, keepdims=True)


def _pipeline(n, stages, unroll=PIPELINE_UNROLL, n_static=0):
    stages = [st if isinstance(st, tuple) else (k, st) for k, st in enumerate(stages)]
    depth = max(s for s, _ in stages) + 1

    def static_step(t):
        for s, stage in stages:
            if 0 <= t - s < n:
                stage(t - s, (t - s) % 2)

    first_full = min(max(depth - 1, n_static), n + depth - 1)
    n_full = max(n - first_full, 0)
    for t in range(first_full):
        static_step(t)
    n_loops = n_full // unroll
    if n_loops >= 1:
        def body(it, carry):
            t = first_full + unroll * it
            for u in range(unroll):
                for s, stage in stages:
                    stage(t + u - s, (first_full + u - s) % 2)
            return carry

        lax.fori_loop(0, n_loops, body, 0)
    for t in range(first_full + unroll * n_loops, n + depth - 1):
        static_step(t)


def _for_each_tile(n_tiles, fn, group=4):
    while n_tiles % group:
        group //= 2

    def body(i, carry):
        for u in range(group):
            fn(i * group + u)
        return carry

    lax.fori_loop(0, n_tiles // group, body, 0)


def _softmax_block_table(n_tiles, t):
    diag = [(i * t, i * t) for i in range(n_tiles)]
    below = [(i * t, j * t) for i in range(1, n_tiles) for j in range(i)]
    return jnp.asarray(np.array(diag + below, np.int32).T), len(diag), len(below)


def _softmax_flat_kernel(tab_ref, q_ref, k_ref, vt_ref, o_ref, acc_ref, m_ref, al_ref, top_ref,
                         s0_ref, s1_ref, p0_ref, p1_ref, *, t, n_diag, n_below, chunk_shift):
    own_rows = _head_rows()
    s_bufs, p_bufs = (s0_ref, s1_ref), (p0_ref, p1_ref)
    kio = lax.broadcasted_iota(jnp.int32, (t, 1), 0)
    last_visible = lax.broadcasted_iota(jnp.int32, (1, t), 1) | ((1 << chunk_shift) - 1)
    def item(i):
        return pl.multiple_of(tab_ref[0, i], t), pl.multiple_of(tab_ref[1, i], t)

    def is_diag(i):
        return isinstance(i, int) and i < n_diag

    half = t // 2

    def scores(hh, i, par):
        q0, k0 = item(i)
        slot = slice(LANES * hh, LANES * (hh + 1))
        if is_diag(i):
            s_lo = _dot_nt(k_ref[0, pl.ds(k0, half), slot], q_ref[0, pl.ds(q0, t), slot])
            s_hi = _dot_nt(k_ref[0, pl.ds(k0 + half, half), slot],
                           q_ref[0, pl.ds(q0 + half, half), slot])
            s_lo = jnp.where(kio[:half] <= last_visible, s_lo, NEG)
            s_hi = jnp.where(kio[:half] <= last_visible[:, :half], s_hi, NEG)
            s_bufs[par][hh, :half, :] = s_lo
            s_bufs[par][hh, half:, half:] = s_hi
            top_ref[par, hh] = jnp.concatenate(
                [_colmax(s_lo[:, :half]), jnp.maximum(_colmax(s_lo[:, half:]), _colmax(s_hi))],
                axis=1)
        else:
            s = _dot_nt(k_ref[0, pl.ds(k0, t), slot], q_ref[0, pl.ds(q0, t), slot])
            s_bufs[par][hh] = s
            top_ref[par, hh] = _colmax(s)

    def numerators(i, par):
        q0, _ = item(i)
        for hh in range(2):
            m_new = top_ref[par, hh]
            if not is_diag(i):
                m_old = m_ref[hh, :, pl.ds(q0, t)]
                m_new = jnp.maximum(m_old, m_new)
                al_ref[par, hh] = jnp.exp(m_old - m_new)
            for a in range(0, t, ROW_CHUNK):
                rows = slice(a, a + ROW_CHUNK)
                if is_diag(i) and a >= half:
                    p_bufs[par][hh, rows, :half] = jnp.zeros((ROW_CHUNK, half), BF16)
                    p_bufs[par][hh, rows, half:] = jnp.exp(
                        s_bufs[par][hh, rows, half:] - m_new[:, half:]).astype(BF16)
                else:
                    p_bufs[par][hh, rows, :] = jnp.exp(s_bufs[par][hh, rows, :] - m_new).astype(BF16)
            m_ref[hh, :, pl.ds(q0, t)] = m_new

    def values(hh, i, par):
        q0, k0 = item(i)
        vt = vt_ref[0, :, pl.ds(k0, t)]
        v1 = jnp.where(own_rows[hh], vt, jnp.ones_like(vt))
        pv = _bdot(v1, p_bufs[par][hh])
        if not is_diag(i):
            pv = acc_ref[hh, :, pl.ds(q0, t)] * al_ref[par, hh] + pv
        acc_ref[hh, :, pl.ds(q0, t)] = pv

    head = functools.partial
    stages = [(0, head(scores, 0)), (0, head(scores, 1)), (1, numerators),
              (2, head(values, 0)), (2, head(values, 1))]
    _pipeline(n_diag + n_below, stages, n_static=n_diag + max(s for s, _ in stages))

    sub = lax.broadcasted_iota(jnp.int32, (LANES, 1), 0)

    def finish(tile):
        q0 = pl.multiple_of(tile * t, t)
        outs = []
        for hh in range(2):
            acc = acc_ref[hh, :, pl.ds(q0, t)]
            denom = acc[HEAD_DIM * (1 - hh):HEAD_DIM * (1 - hh) + 1, :]
            outs.append(acc / denom)
        o_ref[0, pl.ds(q0, t), :] = jnp.where(sub < HEAD_DIM, outs[0], outs[1]).astype(o_ref.dtype).T

    _for_each_tile(n_diag, finish)


def _softmax_attention_flat(q, k, vt, *, t, chunk_shift, name):
    b, s, _ = k.shape
    table, n_diag, n_below = _softmax_block_table(s // t, t)
    grid_spec = pltpu.PrefetchScalarGridSpec(
        num_scalar_prefetch=1,
        grid=(b, N_HEADS // 2),
        in_specs=[pl.BlockSpec((1, s, 2 * LANES), lambda i, h, tab: (i, 0, h)),
                  pl.BlockSpec((1, s, 2 * LANES), lambda i, h, tab: (i, 0, h)),
                  pl.BlockSpec((1, LANES, s), lambda i, h, tab: (i, h, 0))],
        out_specs=pl.BlockSpec((1, s, LANES), lambda i, h, tab: (i, 0, h)),
        scratch_shapes=[pltpu.VMEM((2, LANES, s), F32),
                        pltpu.VMEM((2, 1, s), F32),
                        pltpu.VMEM((2, 2, 1, t), F32),
                        pltpu.VMEM((2, 2, 1, t), F32),
                        pltpu.VMEM((2, t, t), F32), pltpu.VMEM((2, t, t), F32),
                        pltpu.VMEM((2, t, t), BF16), pltpu.VMEM((2, t, t), BF16)])
    return pl.pallas_call(
        functools.partial(_softmax_flat_kernel, t=t, n_diag=n_diag, n_below=n_below,
                          chunk_shift=chunk_shift),
        grid_spec=grid_spec,
        out_shape=jax.ShapeDtypeStruct((b, s, HEAD_W), BF16),
        compiler_params=_params(2),
        name=name,
    )(table, q, k, vt)


def _sb_block_table(n_tiles, t):
    diag = [(i * t, i * t, i * t, i) for i in range(n_tiles)]
    below = [(i * t, (i - d) * t, i * t, i) for d in range(1, n_tiles) for i in range(d, n_tiles)]
    idle = [(0, 0, n_tiles * t, n_tiles)]
    return jnp.asarray(np.array(diag + below + idle, np.int32).T), len(diag), len(below)


def _sb_flat_kernel(tab_ref, q_ref, k_ref, vt_ref, o_ref, acc_ref, r_ref, tot_ref, d_ref, done_ref,
                    busy_ref, z0_ref, z1_ref, e0_ref, e1_ref, w0_ref, w1_ref, *, t, n_diag,
                    n_below):
    in_head = _head_masks()
    own_rows = _head_rows()
    z_bufs, e_bufs, w_bufs = (z0_ref, z1_ref), (e0_ref, e1_ref), (w0_ref, w1_ref)
    kio = lax.broadcasted_iota(jnp.int32, (t, 1), 0)
    qio = lax.broadcasted_iota(jnp.int32, (1, t), 1)
    ss = lax.broadcasted_iota(jnp.int32, (t, t), 0)
    jj = lax.broadcasted_iota(jnp.int32, (t, t), 1)
    later = jnp.where(jj > ss, 1.0, 0.0).astype(BF16)
    acc_ref[...] = jnp.zeros_like(acc_ref)
    r_ref[...] = jnp.zeros_like(r_ref)
    for tile in range(n_diag + 1):
        done_ref[tile] = 0
        busy_ref[tile] = 0

    def stages(base, diag):
        def item(i):
            return (pl.multiple_of(tab_ref[0, base + i], t), pl.multiple_of(tab_ref[1, base + i], t),
                    pl.multiple_of(tab_ref[2, base + i], t), tab_ref[3, base + i])

        def scores(i, par):
            q0, k0, _, _ = item(i)
            k = k_ref[0, pl.ds(k0, t), :]
            q = q_ref[0, pl.ds(q0, t), :]
            for hh in range(2):
                qh = jnp.where(in_head[hh], q, jnp.zeros_like(q))
                z_bufs[par][hh] = _dot_nt(k, qh)

        def drops(i, par):
            for hh in range(2):
                for a in range(0, t, ROW_CHUNK):
                    rows = slice(a, a + ROW_CHUNK)
                    z = z_bufs[par][hh, rows, :]
                    neg_abs = pltpu.bitcast(pltpu.bitcast(z, jnp.uint32) | SIGN_BIT, F32)
                    drop = jnp.maximum(z, 0.0) + jnp.log(1.0 + jnp.exp(neg_abs))
                    log_beta = z - drop
                    if diag:
                        drop = jnp.where(kio[rows] < qio, drop, 0.0)
                        log_beta = jnp.where(kio[rows] < qio, log_beta, NEG)
                    d_ref[hh, rows, :] = drop.astype(BF16)
                    e_bufs[par][hh, rows, :] = log_beta
                after = _bdot(later, d_ref[hh])
                for a in range(0, t, ROW_CHUNK):
                    rows = slice(a, a + ROW_CHUNK)
                    e_bufs[par][hh, rows, :] = e_bufs[par][hh, rows, :] - after[rows]
                tot_ref[par, hh] = after[0:1, :] + d_ref[hh, 0:1, :].astype(F32)

        def weights(i, par):
            _, _, q0, tile = item(i)
            slack = None
            for hh in range(2):
                r_old = r_ref[hh, :, pl.ds(q0, t)]
                for a in range(0, t, ROW_CHUNK):
                    rows = slice(a, a + ROW_CHUNK)
                    w_bufs[par][hh, rows, :] = jnp.exp(e_bufs[par][hh, rows, :] - r_old).astype(BF16)
                r_new = r_old + tot_ref[par, hh]
                r_ref[hh, :, pl.ds(q0, t)] = r_new
                slack = r_new if slack is None else jnp.minimum(slack, r_new)
            if not diag:
                done_ref[tile] = (jnp.min(slack) > UNDERFLOW).astype(jnp.int32)
                busy_ref[tile] = 0

        def values(i, par):
            _, k0, q0, _ = item(i)
            vt = vt_ref[0, :, pl.ds(k0, t)]
            zero = jnp.zeros_like(vt)
            lhs = jnp.concatenate([jnp.where(own_rows[hh], vt, zero) for hh in range(2)], axis=1)
            acc_ref[:, pl.ds(q0, t)] = (acc_ref[:, pl.ds(q0, t)]
                                        + _bdot(lhs, w_bufs[par][...].reshape(2 * t, t)))

        return [scores, drops, weights, values]

    _pipeline(n_diag, stages(0, True))

    scores, drops, weights, values = stages(n_diag, False)

    def next_live(c):
        return lax.while_loop(
            lambda c: (c < n_below) & (done_ref[tab_ref[3, n_diag + c]] != 0), lambda c: c + 1, c)

    def pick(cursor):
        c = next_live(cursor)
        wait = (c < n_below) & (busy_ref[tab_ref[3, n_diag + c]] != 0)
        issue = jnp.where(wait, n_below, c)
        busy_ref[tab_ref[3, n_diag + issue]] = 1
        return issue, jnp.where(wait, c, jnp.minimum(c + 1, n_below))

    def two_steps(state):
        a, b, cursor, i1, i2, i3 = state
        scores(a, 0)
        values(i3, 1)
        drops(i1, 1)
        weights(i2, 0)
        scores(b, 1)
        values(i2, 0)
        drops(a, 0)
        weights(i1, 1)
        a_next, cursor = pick(cursor)
        b_next, cursor = pick(cursor)
        return a_next, b_next, cursor, b, a, i1

    idle = jnp.int32(n_below)
    a0, cursor = pick(jnp.int32(0))
    b0, cursor = pick(cursor)
    lax.while_loop(lambda st: functools.reduce(lambda x, y: x | y, [v < n_below for v in st]),
                   two_steps, (a0, b0, cursor, idle, idle, idle))

    def finish(tile):
        q0 = pl.multiple_of(tile * t, t)
        o_ref[0, pl.ds(q0, t), :] = acc_ref[:, pl.ds(q0, t)].astype(o_ref.dtype).T

    _for_each_tile(n_diag, finish)


def _sb_attention_flat(qk, vt, *, t):
    b, s, _ = qk.shape
    table, n_diag, n_below = _sb_block_table(s // t, t)
    grid_spec = pltpu.PrefetchScalarGridSpec(
        num_scalar_prefetch=1,
        grid=(b, N_HEADS // 2),
        in_specs=[pl.BlockSpec((1, s, LANES), lambda i, h, tab: (i, 0, h)),
                  pl.BlockSpec((1, s, LANES), lambda i, h, tab: (i, 0, 4 + h)),
                  pl.BlockSpec((1, LANES, s), lambda i, h, tab: (i, h, 0))],
        out_specs=pl.BlockSpec((1, s, LANES), lambda i, h, tab: (i, 0, h)),
        scratch_shapes=[pltpu.VMEM((LANES, s + t), F32),
                        pltpu.VMEM((2, 1, s + t), F32),
                        pltpu.VMEM((2, 2, 1, t), F32),
                        pltpu.VMEM((2, t, t), BF16),
                        pltpu.SMEM((s // t + 1,), jnp.int32),
                        pltpu.SMEM((s // t + 1,), jnp.int32),
                        pltpu.VMEM((2, t, t), F32), pltpu.VMEM((2, t, t), F32),
                        pltpu.VMEM((2, t, t), F32), pltpu.VMEM((2, t, t), F32),
                        pltpu.VMEM((2, t, t), BF16), pltpu.VMEM((2, t, t), BF16)])
    return pl.pallas_call(
        functools.partial(_sb_flat_kernel, t=t, n_diag=n_diag, n_below=n_below),
        grid_spec=grid_spec,
        out_shape=jax.ShapeDtypeStruct((b, s, HEAD_W), BF16),
        compiler_params=_params(2),
        name="sb_attention",
    )(table, qk, qk, vt)


def _merge_ffn_kernel(x_ref, yf_ref, ym_ref, ys_ref, gate_ref, mod_ref, wf_ref, wm_ref, ws_ref,
                      wo_ref, g_ref, wg_ref, wu_ref, wd_ref, gfin_ref, o_ref, *, final, fc):
    d = D_MODEL
    merged = (gate_ref[0, :, 0:d].astype(F32) * _bdot(yf_ref[0], wf_ref[...])
              + gate_ref[0, :, d:2 * d].astype(F32) * _bdot(ym_ref[0], wm_ref[...])
              + gate_ref[0, :, 2 * d:3 * d].astype(F32) * _bdot(ys_ref[0], ws_ref[...]))
    x = x_ref[0] + mod_ref[0, 2:3, :] * _bdot(merged.astype(BF16), wo_ref[...])

    sh, sc, gt = mod_ref[0, 3:4, :], mod_ref[0, 4:5, :], mod_ref[0, 5:6, :]
    u = _modulated_norm(x, g_ref[...], sc, sh)
    acc = None
    for a in range(0, D_FF, fc):
        hg = _bdot(u, wg_ref[:, a:a + fc])
        hu = _bdot(u, wu_ref[:, a:a + fc])
        h = ((hg * _sigmoid(hg)) * hu).astype(BF16)
        part = _bdot(h, wd_ref[a:a + fc, :])
        acc = part if acc is None else acc + part
    out = x + gt * acc
    if final:
        out = _rms(out) * gfin_ref[...]
    o_ref[0] = out


def _merge_ffn(x, y_fox, y_mla, y_sb, gates, mod, weights, g_final, *, layer, tm, final):
    b, s, d = x.shape
    row = lambda w: pl.BlockSpec((1, tm, w), lambda i, j: (i, j, 0))
    per_layer = lambda *shape: _resident_layer(shape, layer)
    return pl.pallas_call(
        functools.partial(_merge_ffn_kernel, final=final, fc=D_FF // 2),
        grid=(b, s // tm),
        in_specs=[row(d), row(HEAD_W), row(HEAD_W), row(HEAD_W), row(3 * d),
                  pl.BlockSpec((1, 6, d), lambda i, j: (i, 0, 0)),
                  per_layer(HEAD_W, d), per_layer(HEAD_W, d), per_layer(HEAD_W, d),
                  per_layer(d, d), per_layer(1, d),
                  per_layer(d, D_FF), per_layer(d, D_FF), per_layer(D_FF, d),
                  _resident((1, d))],
        out_specs=row(d),
        out_shape=jax.ShapeDtypeStruct((b, s, d), F32),
        compiler_params=_params(2),
        name="merge_ffn",
    )(x, y_fox, y_mla, y_sb, gates, mod, *weights, g_final)


def _slot_cols(w, width_per_head, used):
    lead = w.shape[:-1]
    w = w.reshape(lead + (N_HEADS, width_per_head))[..., :used]
    pad = [(0, 0)] * (w.ndim - 1) + [(0, LANES - used)]
    return jnp.pad(w, pad).reshape(lead + (SLOT_W,))


def _layout_w_in(w):
    fox, ff, ql, kvl, kr, sb, gate = jnp.split(
        w, [1536, 1544, 1928, 2184, 2216, 3752], axis=-1)
    z = lambda n: jnp.zeros(w.shape[:-1] + (n,), w.dtype)
    kr_slot = jnp.concatenate([z(MLA_NOPE), kr, z(LANES - MLA_NOPE - MLA_ROPE)], axis=-1)
    ff_slot = jnp.concatenate([ff, z(LANES - N_HEADS)], axis=-1)
    return jnp.concatenate([fox, sb, gate, ql, kvl, kr_slot, ff_slot], axis=-1).astype(BF16)


def kernel(x, c, positions, g_mix, w_ada, b_ada, w_in, b_fox_f, g_mla_q, w_mla_uq, g_mla_kv,
           w_mla_ukv, w_o_fox, w_o_mla, w_o_sb, w_out, g_ffn, w_ffn_gate, w_ffn_up, w_ffn_down,
           g_final):
    b, s, d = x.shape
    depth = w_in.shape[0]
    tm = min(s, 512)
    tm_ffn = min(s, 512)
    ta = min(s, 512)
    tsb = min(s, 256)

    mod = _ada(c, w_ada, b_ada).reshape(depth, b, 6, d)
    tables = _rope_tables(positions)

    row_vec = lambda v: v.reshape(depth, 1, -1)
    wkv = w_mla_ukv.reshape(depth, MLA_KV_RANK, N_HEADS, 2 * HEAD_DIM)[..., MLA_NOPE:]
    inproj_weights = (
        row_vec(g_mix), _layout_w_in(w_in),
        row_vec(jnp.pad(b_fox_f, ((0, 0), (0, LANES - N_HEADS)))),
        row_vec(g_mla_q), row_vec(g_mla_kv),
        _slot_cols(w_mla_uq, MLA_NOPE + MLA_ROPE, MLA_NOPE + MLA_ROPE).astype(BF16),
        _slot_cols(w_mla_ukv, 2 * HEAD_DIM, MLA_NOPE).astype(BF16),
        wkv.reshape(depth, MLA_KV_RANK, HEAD_W).astype(BF16))
    merge_ffn_weights = (
        w_o_fox.astype(BF16), w_o_mla.astype(BF16), w_o_sb.astype(BF16), w_out.astype(BF16),
        row_vec(g_ffn), w_ffn_gate.astype(BF16), w_ffn_up.astype(BF16), w_ffn_down.astype(BF16))

    for l in range(depth):
        fox_q, fox_k, fox_vt, sb_qk, sb_vt, gates, mq, mk, mvt = _inproj(
            x, mod[l], inproj_weights, tables, layer=l, tm=tm)
        y_fox = _softmax_attention_flat(fox_q, fox_k, fox_vt, t=ta, chunk_shift=0,
                                        name="fox_attention")
        y_mla = _softmax_attention_flat(mq, mk, mvt, t=ta, chunk_shift=CHUNK_SHIFT,
                                        name="mla_attention")
        y_sb = _sb_attention_flat(sb_qk, sb_vt, t=tsb)
        x = _merge_ffn(x, y_fox, y_mla, y_sb, gates, mod[l], merge_ffn_weights,
                       g_final.reshape(1, d), layer=l, tm=tm_ffn, final=(l == depth - 1))
    return x
```

```python
import functools

import numpy as np
import jax
import jax.numpy as jnp
from jax import lax
from jax.experimental import pallas as pl
from jax.experimental.pallas import tpu as pltpu

F32 = jnp.float32
BF16 = jnp.bfloat16

D_MODEL = 1024
EPS = 1e-6
HEAD_DIM = 64
N_HEADS = 8
HEAD_W = N_HEADS * HEAD_DIM
CHUNK_SHIFT = 6
MLA_Q_RANK = 384
MLA_KV_RANK = 256
MLA_NOPE = 64
MLA_ROPE = 32
ROPE_HALF = MLA_ROPE // 2
ROPE_BASE = 10000.0
MLA_SCALE = (MLA_NOPE + MLA_ROPE) ** -0.5
QK_SCALE = HEAD_DIM ** -0.5
D_FF = 2816
LANES = 128
SLOT_W = N_HEADS * LANES
N_PIECES = 3

C_FQ = 0
C_FK = 512
C_FV = 1024
C_SB = 1536
C_GATE = 3072
C_QL = 6144
C_KVL = 6528
C_KR = 6784
C_FF = 6912
IN_W = 7040

ROW_CHUNK = 32
PIPELINE_UNROLL = 2
SIGN_BIT = np.uint32(0x80000000)
UNDERFLOW = 105.0
NEG = -1e30
VMEM_LIMIT = 56 * 1024 * 1024


def _params(n_grid):
    return pltpu.CompilerParams(dimension_semantics=("arbitrary",) * n_grid,
                                vmem_limit_bytes=VMEM_LIMIT)


def _resident(shape):
    nd = len(shape)
    return pl.BlockSpec(shape, lambda *_: (0,) * nd, pipeline_mode=pl.Buffered(1))


def _resident_layer(shape, layer):
    nd = len(shape)
    return pl.BlockSpec((pl.Squeezed(),) + tuple(shape), lambda *_: (layer,) + (0,) * nd,
                        pipeline_mode=pl.Buffered(1))


def _bdot(a, b):
    return jnp.dot(a, b, preferred_element_type=F32)


def _dot_nt(a, b):
    return lax.dot_general(a, b, (((1,), (1,)), ((), ())), preferred_element_type=F32)


def _split3(v):
    hi = v.astype(BF16)
    r = v - hi.astype(F32)
    mid = r.astype(BF16)
    lo = (r - mid.astype(F32)).astype(BF16)
    return hi, mid, lo


def _log_sigmoid(z):
    return jnp.minimum(z, 0.0) - jnp.log1p(jnp.exp(-jnp.abs(z)))


def _sigmoid(z):
    return 1.0 / (1.0 + jnp.exp(-z))


def _rms(x):
    return x * lax.rsqrt(jnp.mean(x * x, axis=-1, keepdims=True) + EPS)


def _modulated_norm(x, g, scale, shift):
    return (_rms(x) * (g * (1.0 + scale)) + shift).astype(BF16)


def _ada_kernel(c_ref, w_ref, b_ref, o_ref):
    c = c_ref[...]
    cond = c * _sigmoid(c)
    chi, cmid, _ = _split3(cond)
    whi, wmid, _ = _split3(w_ref[0])
    o_ref[0] = _bdot(chi, whi) + (_bdot(chi, wmid) + _bdot(cmid, whi)) + b_ref[0]


def _ada(c, w_ada, b_ada):
    depth, d, n = w_ada.shape
    b = c.shape[0]
    tn = 1024
    return pl.pallas_call(
        _ada_kernel,
        grid=(depth, n // tn),
        in_specs=[pl.BlockSpec((b, d), lambda l, j: (0, 0)),
                  pl.BlockSpec((1, d, tn), lambda l, j: (l, 0, j)),
                  pl.BlockSpec((1, 1, tn), lambda l, j: (l, 0, j))],
        out_specs=pl.BlockSpec((1, b, tn), lambda l, j: (l, 0, j)),
        out_shape=jax.ShapeDtypeStruct((depth, b, n), F32),
        compiler_params=_params(2),
        name="adaln",
    )(c, w_ada, b_ada.reshape(depth, 1, n))


def _rope_kernel(pos_ref, inv_ref, cos_ref, sina_ref, sinb_ref):
    ang = pos_ref[0].astype(F32) * inv_ref[...]
    lane = lax.broadcasted_iota(jnp.int32, (1, LANES), 1)
    sin = jnp.sin(ang)
    cos_ref[0] = jnp.cos(ang)
    lo, mid, hi = MLA_NOPE, MLA_NOPE + ROPE_HALF, MLA_NOPE + MLA_ROPE
    sina_ref[0] = jnp.where((lane >= lo) & (lane < mid), -sin, 0.0)
    sinb_ref[0] = jnp.where((lane >= mid) & (lane < hi), sin, 0.0)


def _rope_tables(positions):
    b, s = positions.shape
    ts = min(s, 512)
    inv = ROPE_BASE ** (-jnp.arange(ROPE_HALF, dtype=F32) / ROPE_HALF)
    inv_slot = jnp.concatenate([jnp.zeros((MLA_NOPE,), F32), inv, inv,
                                jnp.zeros((LANES - MLA_NOPE - MLA_ROPE,), F32)]).reshape(1, LANES)
    spec = pl.BlockSpec((1, ts, LANES), lambda i, j: (i, j, 0))
    return pl.pallas_call(
        _rope_kernel,
        grid=(b, s // ts),
        in_specs=[pl.BlockSpec((1, ts, 1), lambda i, j: (i, j, 0)),
                  pl.BlockSpec((1, LANES), lambda i, j: (0, 0))],
        out_specs=[spec, spec, spec],
        out_shape=[jax.ShapeDtypeStruct((b, s, LANES), F32)] * 3,
        compiler_params=_params(2),
        name="rope_tables",
    )(positions.reshape(b, s, 1), inv_slot)


def _rope_slot(x, cos, sina, sinb):
    return (x * cos + pltpu.roll(x, LANES - ROPE_HALF, 1) * sina
            + pltpu.roll(x, ROPE_HALF, 1) * sinb)


def _inproj_kernel(x_ref, mod_ref, g_ref, w_ref, bff_ref, gq_ref, gkv_ref, wuq_ref, wkn_ref,
                   wkv_ref, cos_ref, sina_ref, sinb_ref, place_ref, qc_ref, kc_ref,
                   fq_ref, fk_ref, fv_ref, sb_ref, sv_ref, gate_ref, mq_ref, mk_ref, mv_ref,
                   carry_ref, *, tm):
    x = x_ref[0]
    sh, sc = mod_ref[0, 0:1, :], mod_ref[0, 1:2, :]
    u = _modulated_norm(x, g_ref[...], sc, sh)

    def seg(a, b):
        return _bdot(u, w_ref[:, a:b])

    lane = lax.broadcasted_iota(jnp.int32, (1, LANES), 1)
    cos, sina, sinb = cos_ref[0], sina_ref[0], sinb_ref[0]

    def gates(chunks):
        for c in chunks:
            a = 512 * c
            gate_ref[0, :, a:a + 512] = _sigmoid(seg(C_GATE + a, C_GATE + a + 512)).astype(BF16)

    def pack3(v):
        p0, p1, p2 = (p.astype(F32) for p in _split3(v))
        return jnp.where(lane < N_HEADS, p0,
                         jnp.where(lane < 2 * N_HEADS, pltpu.roll(p1, N_HEADS, 1),
                                   jnp.where(lane < 3 * N_HEADS, pltpu.roll(p2, 2 * N_HEADS, 1),
                                             0.0))).astype(BF16)

    def to_slots(compact, extra, out_ref):
        for h in range(N_HEADS):
            x = compact[:, LANES * (h // 2):LANES * (h // 2 + 1)]
            head = pltpu.roll(x, HEAD_DIM, 1) if h % 2 else x
            slot = slice(LANES * h, LANES * (h + 1))
            out_ref[0, :, slot] = jnp.where(lane < HEAD_DIM, head, extra[:, slot]).astype(BF16)

    forget_logits = seg(C_FF, IN_W)
    q_latent = seg(C_QL, C_KVL)
    kv_latent = seg(C_KVL, C_KR)
    rope_key = seg(C_KR, C_FF)
    log_f = pack3(_log_sigmoid(forget_logits + bff_ref[...]))
    cq = (_rms(q_latent) * gq_ref[...]).astype(BF16)
    ckv = (_rms(kv_latent) * gkv_ref[...]).astype(BF16)
    gates(range(0, 3))

    r = lax.broadcasted_iota(jnp.int32, (tm, tm), 0)
    c = lax.broadcasted_iota(jnp.int32, (tm, tm), 1)
    part = _bdot(jnp.where(r >= c, 1.0, 0.0).astype(BF16), log_f)
    q_all = _bdot(cq, wuq_ref[...])
    k_all = _bdot(ckv, wkn_ref[...])
    v_all = _bdot(ckv, wkv_ref[...])
    gates(range(3, 6))

    first_tile = pl.program_id(1) == 0
    cum = ((part + pltpu.roll(part, LANES - N_HEADS, 1))
           + pltpu.roll(part, LANES - 2 * N_HEADS, 1)) + jnp.where(first_tile, 0.0, carry_ref[...])
    carry_ref[...] = jnp.where(lane < N_HEADS, cum[tm - 1:tm, :], 0.0)
    placed = _bdot(pack3(cum), place_ref[...])
    first_aug = (lax.broadcasted_iota(jnp.int32, (1, SLOT_W), 1) & (LANES - 1)) < HEAD_DIM + N_PIECES
    fox_q = seg(C_FQ, C_FK) * QK_SCALE
    fox_k = seg(C_FK, C_FV)

    kr = _rope_slot(rope_key, cos, sina, sinb)
    for h in range(N_HEADS):
        slot = slice(LANES * h, LANES * (h + 1))
        q = _rope_slot(q_all[:, slot], cos, sina, sinb)
        mq_ref[0, :, slot] = (q * MLA_SCALE).astype(BF16)
        mk_ref[0, :, slot] = (k_all[:, slot] + kr).astype(BF16)
    mv_ref[0] = v_all.astype(BF16).T
    to_slots(fox_q, jnp.where(first_aug, qc_ref[...], placed), fq_ref)
    to_slots(fox_k, jnp.where(first_aug, placed, kc_ref[...]), fk_ref)

    sv_ref[0] = seg(C_SB + 1024, C_GATE).astype(BF16).T
    fv_ref[0] = seg(C_FV, C_SB).astype(BF16).T
    sb_ref[0, :, 0:512] = (seg(C_SB, C_SB + 512) * QK_SCALE).astype(BF16)
    sb_ref[0, :, 512:1024] = seg(C_SB + 512, C_SB + 1024).astype(BF16)


def _fox_placement():
    place = np.zeros((LANES, SLOT_W), np.float32)
    qc = np.zeros((1, SLOT_W), np.float32)
    kc = np.zeros((1, SLOT_W), np.float32)
    for h in range(N_HEADS):
        for p in range(N_PIECES):
            place[N_HEADS * p + h, LANES * h + HEAD_DIM + p] = 1.0
            place[N_HEADS * p + h, LANES * h + HEAD_DIM + N_PIECES + p] = 1.0
            qc[0, LANES * h + HEAD_DIM + p] = -1.0
            kc[0, LANES * h + HEAD_DIM + N_PIECES + p] = 1.0
    return jnp.asarray(place, BF16), jnp.asarray(qc), jnp.asarray(kc)


def _inproj(x, mod, weights, tables, *, layer, tm):
    b, s, d = x.shape
    cos, sina, sinb = tables
    place, qc, kc = _fox_placement()
    per_layer = lambda *shape: _resident_layer(shape, layer)
    row = lambda w: pl.BlockSpec((1, tm, w), lambda i, j: (i, j, 0))
    col = lambda w: pl.BlockSpec((1, w, tm), lambda i, j: (i, 0, j))
    outs = [(SLOT_W, False),
            (SLOT_W, False),
            (HEAD_W, True),
            (2 * HEAD_W, False),
            (HEAD_W, True),
            (3 * D_MODEL, False),
            (SLOT_W, False),
            (SLOT_W, False),
            (HEAD_W, True)]
    out_specs = [col(w) if tr else row(w) for w, tr in outs]
    out_shape = [jax.ShapeDtypeStruct((b, w, s) if tr else (b, s, w), BF16) for w, tr in outs]
    return pl.pallas_call(
        functools.partial(_inproj_kernel, tm=tm),
        grid=(b, s // tm),
        in_specs=[row(d),
                  pl.BlockSpec((1, 6, d), lambda i, j: (i, 0, 0)),
                  per_layer(1, d), per_layer(d, IN_W), per_layer(1, LANES),
                  per_layer(1, MLA_Q_RANK), per_layer(1, MLA_KV_RANK),
                  per_layer(MLA_Q_RANK, SLOT_W), per_layer(MLA_KV_RANK, SLOT_W),
                  per_layer(MLA_KV_RANK, HEAD_W),
                  row(LANES), row(LANES), row(LANES),
                  _resident((LANES, SLOT_W)), _resident((1, SLOT_W)), _resident((1, SLOT_W))],
        out_specs=out_specs,
        out_shape=out_shape,
        scratch_shapes=[pltpu.VMEM((1, LANES), F32)],
        compiler_params=_params(2),
        name="inproj",
    )(x, mod, *weights, cos, sina, sinb, place, qc, kc)


def _head_rows():
    row = lax.broadcasted_iota(jnp.int32, (LANES, 1), 0)
    return [(row >= HEAD_DIM * hh) & (row < HEAD_DIM * (hh + 1)) for hh in range(2)]


def _head_masks():
    lane = lax.broadcasted_iota(jnp.int32, (1, LANES), 1)
    return [(lane >= HEAD_DIM * hh) & (lane < HEAD_DIM * (hh + 1)) for hh in range(2)]


def _colmax(x, chunk=64):
    rows = x.shape[0]
    if rows > chunk and rows % chunk == 0:
        acc = x[:chunk]
        for a in range(chunk, rows, chunk):
            acc = jnp.maximum(acc, x[a:a + chunk])
        x, rows = acc, chunk
    while rows > 8 and rows % 16 == 0:
        rows //= 2
        x = jnp.maximum(x[:rows], x[rows:])
    return jnp.max(x, axis=0, keepdims=True)


def _pipeline(n, stages, unroll=PIPELINE_UNROLL, n_static=0):
    stages = [st if isinstance(st, tuple) else (k, st) for k, st in enumerate(stages)]
    depth = max(s for s, _ in stages) + 1

    def static_step(t):
        for s, stage in stages:
            if 0 <= t - s < n:
                stage(t - s, (t - s) % 2)

    first_full = min(max(depth - 1, n_static), n + depth - 1)
    n_full = max(n - first_full, 0)
    for t in range(first_full):
        static_step(t)
    n_loops = n_full // unroll
    if n_loops >= 1:
        def body(it, carry):
            t = first_full + unroll * it
            for u in range(unroll):
                for s, stage in stages:
                    stage(t + u - s, (first_full + u - s) % 2)
            return carry

        lax.fori_loop(0, n_loops, body, 0)
    for t in range(first_full + unroll * n_loops, n + depth - 1):
        static_step(t)


def _for_each_tile(n_tiles, fn, group=4):
    while n_tiles % group:
        group //= 2

    def body(i, carry):
        for u in range(group):
            fn(i * group + u)
        return carry

    lax.fori_loop(0, n_tiles // group, body, 0)


def _softmax_block_table(n_tiles, t):
    diag = [(i * t, i * t) for i in range(n_tiles)]
    below = [(i * t, j * t) for i in range(1, n_tiles) for j in range(i)]
    return jnp.asarray(np.array(diag + below, np.int32).T), len(diag), len(below)


def _softmax_flat_kernel(tab_ref, q_ref, k_ref, vt_ref, o_ref, acc_ref, m_ref, al_ref, top_ref,
                         s0_ref, s1_ref, p0_ref, p1_ref, *, t, n_diag, n_below, chunk_shift):
    own_rows = _head_rows()
    s_bufs, p_bufs = (s0_ref, s1_ref), (p0_ref, p1_ref)
    kio = lax.broadcasted_iota(jnp.int32, (t, 1), 0)
    last_visible = lax.broadcasted_iota(jnp.int32, (1, t), 1) | ((1 << chunk_shift) - 1)
    def item(i):
        return pl.multiple_of(tab_ref[0, i], t), pl.multiple_of(tab_ref[1, i], t)

    def is_diag(i):
        return isinstance(i, int) and i < n_diag

    half = t // 2

    def scores(hh, i, par):
        q0, k0 = item(i)
        slot = slice(LANES * hh, LANES * (hh + 1))
        if is_diag(i):
            s_lo = _dot_nt(k_ref[0, pl.ds(k0, half), slot], q_ref[0, pl.ds(q0, t), slot])
            s_hi = _dot_nt(k_ref[0, pl.ds(k0 + half, half), slot],
                           q_ref[0, pl.ds(q0 + half, half), slot])
            s_lo = jnp.where(kio[:half] <= last_visible, s_lo, NEG)
            s_hi = jnp.where(kio[:half] <= last_visible[:, :half], s_hi, NEG)
            s_bufs[par][hh, :half, :] = s_lo
            s_bufs[par][hh, half:, half:] = s_hi
            top_ref[par, hh] = jnp.concatenate(
                [_colmax(s_lo[:, :half]), jnp.maximum(_colmax(s_lo[:, half:]), _colmax(s_hi))],
                axis=1)
        else:
            s = _dot_nt(k_ref[0, pl.ds(k0, t), slot], q_ref[0, pl.ds(q0, t), slot])
            s_bufs[par][hh] = s
            top_ref[par, hh] = _colmax(s)

    def numerators(i, par):
        q0, _ = item(i)
        for hh in range(2):
            m_new = top_ref[par, hh]
            if not is_diag(i):
                m_old = m_ref[hh, :, pl.ds(q0, t)]
                m_new = jnp.maximum(m_old, m_new)
                al_ref[par, hh] = jnp.exp(m_old - m_new)
            for a in range(0, t, ROW_CHUNK):
                rows = slice(a, a + ROW_CHUNK)
                if is_diag(i) and a >= half:
                    p_bufs[par][hh, rows, :half] = jnp.zeros((ROW_CHUNK, half), BF16)
                    p_bufs[par][hh, rows, half:] = jnp.exp(
                        s_bufs[par][hh, rows, half:] - m_new[:, half:]).astype(BF16)
                else:
                    p_bufs[par][hh, rows, :] = jnp.exp(s_bufs[par][hh, rows, :] - m_new).astype(BF16)
            m_ref[hh, :, pl.ds(q0, t)] = m_new

    def values(hh, i, par):
        q0, k0 = item(i)
        vt = vt_ref[0, :, pl.ds(k0, t)]
        v1 = jnp.where(own_rows[hh], vt, jnp.ones_like(vt))
        pv = _bdot(v1, p_bufs[par][hh])
        if not is_diag(i):
            pv = acc_ref[hh, :, pl.ds(q0, t)] * al_ref[par, hh] + pv
        acc_ref[hh, :, pl.ds(q0, t)] = pv

    head = functools.partial
    stages = [(0, head(scores, 0)), (0, head(scores, 1)), (1, numerators),
              (2, head(values, 0)), (2, head(values, 1))]
    _pipeline(n_diag + n_below, stages, n_static=n_diag + max(s for s, _ in stages))

    sub = lax.broadcasted_iota(jnp.int32, (LANES, 1), 0)

    def finish(tile):
        q0 = pl.multiple_of(tile * t, t)
        outs = []
        for hh in range(2):
            acc = acc_ref[hh, :, pl.ds(q0, t)]
            denom = acc[HEAD_DIM * (1 - hh):HEAD_DIM * (1 - hh) + 1, :]
            outs.append(acc / denom)
        o_ref[0, pl.ds(q0, t), :] = jnp.where(sub < HEAD_DIM, outs[0], outs[1]).astype(o_ref.dtype).T

    _for_each_tile(n_diag, finish)


def _softmax_attention_flat(q, k, vt, *, t, chunk_shift, name):
    b, s, _ = k.shape
    table, n_diag, n_below = _softmax_block_table(s // t, t)
    grid_spec = pltpu.PrefetchScalarGridSpec(
        num_scalar_prefetch=1,
        grid=(b, N_HEADS // 2),
        in_specs=[pl.BlockSpec((1, s, 2 * LANES), lambda i, h, tab: (i, 0, h)),
                  pl.BlockSpec((1, s, 2 * LANES), lambda i, h, tab: (i, 0, h)),
                  pl.BlockSpec((1, LANES, s), lambda i, h, tab: (i, h, 0))],
        out_specs=pl.BlockSpec((1, s, LANES), lambda i, h, tab: (i, 0, h)),
        scratch_shapes=[pltpu.VMEM((2, LANES, s), F32),
                        pltpu.VMEM((2, 1, s), F32),
                        pltpu.VMEM((2, 2, 1, t), F32),
                        pltpu.VMEM((2, 2, 1, t), F32),
                        pltpu.VMEM((2, t, t), F32), pltpu.VMEM((2, t, t), F32),
                        pltpu.VMEM((2, t, t), BF16), pltpu.VMEM((2, t, t), BF16)])
    return pl.pallas_call(
        functools.partial(_softmax_flat_kernel, t=t, n_diag=n_diag, n_below=n_below,
                          chunk_shift=chunk_shift),
        grid_spec=grid_spec,
        out_shape=jax.ShapeDtypeStruct((b, s, HEAD_W), BF16),
        compiler_params=_params(2),
        name=name,
    )(table, q, k, vt)


def _sb_block_table(n_tiles, t):
    diag = [(i * t, i * t, i * t, i) for i in range(n_tiles)]
    below = [(i * t, (i - d) * t, i * t, i) for d in range(1, n_tiles) for i in range(d, n_tiles)]
    idle = [(0, 0, n_tiles * t, n_tiles)]
    return jnp.asarray(np.array(diag + below + idle, np.int32).T), len(diag), len(below)


def _sb_flat_kernel(tab_ref, q_ref, k_ref, vt_ref, o_ref, acc_ref, r_ref, tot_ref, d_ref, done_ref,
                    busy_ref, z0_ref, z1_ref, e0_ref, e1_ref, w0_ref, w1_ref, *, t, n_diag,
                    n_below):
    in_head = _head_masks()
    own_rows = _head_rows()
    z_bufs, e_bufs, w_bufs = (z0_ref, z1_ref), (e0_ref, e1_ref), (w0_ref, w1_ref)
    kio = lax.broadcasted_iota(jnp.int32, (t, 1), 0)
    qio = lax.broadcasted_iota(jnp.int32, (1, t), 1)
    ss = lax.broadcasted_iota(jnp.int32, (t, t), 0)
    jj = lax.broadcasted_iota(jnp.int32, (t, t), 1)
    later = jnp.where(jj > ss, 1.0, 0.0).astype(BF16)
    acc_ref[...] = jnp.zeros_like(acc_ref)
    r_ref[...] = jnp.zeros_like(r_ref)
    for tile in range(n_diag + 1):
        done_ref[tile] = 0
        busy_ref[tile] = 0

    def stages(base, diag):
        def item(i):
            return (pl.multiple_of(tab_ref[0, base + i], t), pl.multiple_of(tab_ref[1, base + i], t),
                    pl.multiple_of(tab_ref[2, base + i], t), tab_ref[3, base + i])

        def scores(i, par):
            q0, k0, _, _ = item(i)
            k = k_ref[0, pl.ds(k0, t), :]
            q = q_ref[0, pl.ds(q0, t), :]
            for hh in range(2):
                qh = jnp.where(in_head[hh], q, jnp.zeros_like(q))
                z_bufs[par][hh] = _dot_nt(k, qh)

        def drops(i, par):
            for hh in range(2):
                for a in range(0, t, ROW_CHUNK):
                    rows = slice(a, a + ROW_CHUNK)
                    z = z_bufs[par][hh, rows, :]
                    neg_abs = pltpu.bitcast(pltpu.bitcast(z, jnp.uint32) | SIGN_BIT, F32)
                    drop = jnp.maximum(z, 0.0) + jnp.log(1.0 + jnp.exp(neg_abs))
                    log_beta = z - drop
                    if diag:
                        drop = jnp.where(kio[rows] < qio, drop, 0.0)
                        log_beta = jnp.where(kio[rows] < qio, log_beta, NEG)
                    d_ref[hh, rows, :] = drop.astype(BF16)
                    e_bufs[par][hh, rows, :] = log_beta
                after = _bdot(later, d_ref[hh])
                for a in range(0, t, ROW_CHUNK):
                    rows = slice(a, a + ROW_CHUNK)
                    e_bufs[par][hh, rows, :] = e_bufs[par][hh, rows, :] - after[rows]
                tot_ref[par, hh] = after[0:1, :] + d_ref[hh, 0:1, :].astype(F32)

        def weights(i, par):
            _, _, q0, tile = item(i)
            slack = None
            for hh in range(2):
                r_old = r_ref[hh, :, pl.ds(q0, t)]
                for a in range(0, t, ROW_CHUNK):
                    rows = slice(a, a + ROW_CHUNK)
                    w_bufs[par][hh, rows, :] = jnp.exp(e_bufs[par][hh, rows, :] - r_old).astype(BF16)
                r_new = r_old + tot_ref[par, hh]
                r_ref[hh, :, pl.ds(q0, t)] = r_new
                slack = r_new if slack is None else jnp.minimum(slack, r_new)
            if not diag:
                done_ref[tile] = (jnp.min(slack) > UNDERFLOW).astype(jnp.int32)
                busy_ref[tile] = 0

        def values(i, par):
            _, k0, q0, _ = item(i)
            vt = vt_ref[0, :, pl.ds(k0, t)]
            zero = jnp.zeros_like(vt)
            lhs = jnp.concatenate([jnp.where(own_rows[hh], vt, zero) for hh in range(2)], axis=1)
            acc_ref[:, pl.ds(q0, t)] = (acc_ref[:, pl.ds(q0, t)]
                                        + _bdot(lhs, w_bufs[par][...].reshape(2 * t, t)))

        return [scores, drops, weights, values]

    _pipeline(n_diag, stages(0, True))

    scores, drops, weights, values = stages(n_diag, False)

    def next_live(c):
        return lax.while_loop(
            lambda c: (c < n_below) & (done_ref[tab_ref[3, n_diag + c]] != 0), lambda c: c + 1, c)

    def pick(cursor):
        c = next_live(cursor)
        wait = (c < n_below) & (busy_ref[tab_ref[3, n_diag + c]] != 0)
        issue = jnp.where(wait, n_below, c)
        busy_ref[tab_ref[3, n_diag + issue]] = 1
        return issue, jnp.where(wait, c, jnp.minimum(c + 1, n_below))

    def two_steps(state):
        a, b, cursor, i1, i2, i3 = state
        scores(a, 0)
        values(i3, 1)
        drops(i1, 1)
        weights(i2, 0)
        scores(b, 1)
        values(i2, 0)
        drops(a, 0)
        weights(i1, 1)
        a_next, cursor = pick(cursor)
        b_next, cursor = pick(cursor)
        return a_next, b_next, cursor, b, a, i1

    idle = jnp.int32(n_below)
    a0, cursor = pick(jnp.int32(0))
    b0, cursor = pick(cursor)
    lax.while_loop(lambda st: functools.reduce(lambda x, y: x | y, [v < n_below for v in st]),
                   two_steps, (a0, b0, cursor, idle, idle, idle))

    def finish(tile):
        q0 = pl.multiple_of(tile * t, t)
        o_ref[0, pl.ds(q0, t), :] = acc_ref[:, pl.ds(q0, t)].astype(o_ref.dtype).T

    _for_each_tile(n_diag, finish)


def _sb_attention_flat(qk, vt, *, t):
    b, s, _ = qk.shape
    table, n_diag, n_below = _sb_block_table(s // t, t)
    grid_spec = pltpu.PrefetchScalarGridSpec(
        num_scalar_prefetch=1,
        grid=(b, N_HEADS // 2),
        in_specs=[pl.BlockSpec((1, s, LANES), lambda i, h, tab: (i, 0, h)),
                  pl.BlockSpec((1, s, LANES), lambda i, h, tab: (i, 0, 4 + h)),
                  pl.BlockSpec((1, LANES, s), lambda i, h, tab: (i, h, 0))],
        out_specs=pl.BlockSpec((1, s, LANES), lambda i, h, tab: (i, 0, h)),
        scratch_shapes=[pltpu.VMEM((LANES, s + t), F32),
                        pltpu.VMEM((2, 1, s + t), F32),
                        pltpu.VMEM((2, 2, 1, t), F32),
                        pltpu.VMEM((2, t, t), BF16),
                        pltpu.SMEM((s // t + 1,), jnp.int32),
                        pltpu.SMEM((s // t + 1,), jnp.int32),
                        pltpu.VMEM((2, t, t), F32), pltpu.VMEM((2, t, t), F32),
                        pltpu.VMEM((2, t, t), F32), pltpu.VMEM((2, t, t), F32),
                        pltpu.VMEM((2, t, t), BF16), pltpu.VMEM((2, t, t), BF16)])
    return pl.pallas_call(
        functools.partial(_sb_flat_kernel, t=t, n_diag=n_diag, n_below=n_below),
        grid_spec=grid_spec,
        out_shape=jax.ShapeDtypeStruct((b, s, HEAD_W), BF16),
        compiler_params=_params(2),
        name="sb_attention",
    )(table, qk, qk, vt)


def _merge_ffn_kernel(x_ref, yf_ref, ym_ref, ys_ref, gate_ref, mod_ref, wf_ref, wm_ref, ws_ref,
                      wo_ref, g_ref, wg_ref, wu_ref, wd_ref, gfin_ref, o_ref, *, final, fc):
    d = D_MODEL
    merged = (gate_ref[0, :, 0:d].astype(F32) * _bdot(yf_ref[0], wf_ref[...])
              + gate_ref[0, :, d:2 * d].astype(F32) * _bdot(ym_ref[0], wm_ref[...])
              + gate_ref[0, :, 2 * d:3 * d].astype(F32) * _bdot(ys_ref[0], ws_ref[...]))
    x = x_ref[0] + mod_ref[0, 2:3, :] * _bdot(merged.astype(BF16), wo_ref[...])

    sh, sc, gt = mod_ref[0, 3:4, :], mod_ref[0, 4:5, :], mod_ref[0, 5:6, :]
    u = _modulated_norm(x, g_ref[...], sc, sh)
    acc = None
    for a in range(0, D_FF, fc):
        hg = _bdot(u, wg_ref[:, a:a + fc])
        hu = _bdot(u, wu_ref[:, a:a + fc])
        h = ((hg * _sigmoid(hg)) * hu).astype(BF16)
        part = _bdot(h, wd_ref[a:a + fc, :])
        acc = part if acc is None else acc + part
    out = x + gt * acc
    if final:
        out = _rms(out) * gfin_ref[...]
    o_ref[0] = out


def _merge_ffn(x, y_fox, y_mla, y_sb, gates, mod, weights, g_final, *, layer, tm, final):
    b, s, d = x.shape
    row = lambda w: pl.BlockSpec((1, tm, w), lambda i, j: (i, j, 0))
    per_layer = lambda *shape: _resident_layer(shape, layer)
    return pl.pallas_call(
        functools.partial(_merge_ffn_kernel, final=final, fc=D_FF // 2),
        grid=(b, s // tm),
        in_specs=[row(d), row(HEAD_W), row(HEAD_W), row(HEAD_W), row(3 * d),
                  pl.BlockSpec((1, 6, d), lambda i, j: (i, 0, 0)),
                  per_layer(HEAD_W, d), per_layer(HEAD_W, d), per_layer(HEAD_W, d),
                  per_layer(d, d), per_layer(1, d),
                  per_layer(d, D_FF), per_layer(d, D_FF), per_layer(D_FF, d),
                  _resident((1, d))],
        out_specs=row(d),
        out_shape=jax.ShapeDtypeStruct((b, s, d), F32),
        compiler_params=_params(2),
        name="merge_ffn",
    )(x, y_fox, y_mla, y_sb, gates, mod, *weights, g_final)


def _slot_cols(w, width_per_head, used):
    lead = w.shape[:-1]
    w = w.reshape(lead + (N_HEADS, width_per_head))[..., :used]
    pad = [(0, 0)] * (w.ndim - 1) + [(0, LANES - used)]
    return jnp.pad(w, pad).reshape(lead + (SLOT_W,))


def _layout_w_in(w):
    fox, ff, ql, kvl, kr, sb, gate = jnp.split(
        w, [1536, 1544, 1928, 2184, 2216, 3752], axis=-1)
    z = lambda n: jnp.zeros(w.shape[:-1] + (n,), w.dtype)
    kr_slot = jnp.concatenate([z(MLA_NOPE), kr, z(LANES - MLA_NOPE - MLA_ROPE)], axis=-1)
    ff_slot = jnp.concatenate([ff, z(LANES - N_HEADS)], axis=-1)
    return jnp.concatenate([fox, sb, gate, ql, kvl, kr_slot, ff_slot], axis=-1).astype(BF16)


def kernel(x, c, positions, g_mix, w_ada, b_ada, w_in, b_fox_f, g_mla_q, w_mla_uq, g_mla_kv,
           w_mla_ukv, w_o_fox, w_o_mla, w_o_sb, w_out, g_ffn, w_ffn_gate, w_ffn_up, w_ffn_down,
           g_final):
    b, s, d = x.shape
    depth = w_in.shape[0]
    tm = min(s, 512)
    tm_ffn = min(s, 512)
    ta = min(s, 512)
    tsb = min(s, 256)

    mod = _ada(c, w_ada, b_ada).reshape(depth, b, 6, d)
    tables = _rope_tables(positions)

    row_vec = lambda v: v.reshape(depth, 1, -1)
    wkv = w_mla_ukv.reshape(depth, MLA_KV_RANK, N_HEADS, 2 * HEAD_DIM)[..., MLA_NOPE:]
    inproj_weights = (
        row_vec(g_mix), _layout_w_in(w_in),
        row_vec(jnp.pad(b_fox_f, ((0, 0), (0, LANES - N_HEADS)))),
        row_vec(g_mla_q), row_vec(g_mla_kv),
        _slot_cols(w_mla_uq, MLA_NOPE + MLA_ROPE, MLA_NOPE + MLA_ROPE).astype(BF16),
        _slot_cols(w_mla_ukv, 2 * HEAD_DIM, MLA_NOPE).astype(BF16),
        wkv.reshape(depth, MLA_KV_RANK, HEAD_W).astype(BF16))
    merge_ffn_weights = (
        w_o_fox.astype(BF16), w_o_mla.astype(BF16), w_o_sb.astype(BF16), w_out.astype(BF16),
        row_vec(g_ffn), w_ffn_gate.astype(BF16), w_ffn_up.astype(BF16), w_ffn_down.astype(BF16))

    for l in range(depth):
        fox_q, fox_k, fox_vt, sb_qk, sb_vt, gates, mq, mk, mvt = _inproj(
            x, mod[l], inproj_weights, tables, layer=l, tm=tm)
        y_fox = _softmax_attention_flat(fox_q, fox_k, fox_vt, t=ta, chunk_shift=0,
                                        name="fox_attention")
        y_mla = _softmax_attention_flat(mq, mk, mvt, t=ta, chunk_shift=CHUNK_SHIFT,
                                        name="mla_attention")
        y_sb = _sb_attention_flat(sb_qk, sb_vt, t=tsb)
        x = _merge_ffn(x, y_fox, y_mla, y_sb, gates, mod[l], merge_ffn_weights,
                       g_final.reshape(1, d), layer=l, tm=tm_ffn, final=(l == depth - 1))
    return x
```
